```python
import math
import jax, jax.numpy as jnp
from jax import lax
import numpy as np

D_MODEL = 1024
BATCH = 16
SEQ = 2048
DEPTH = 4

N_A_LAYERS = DEPTH // 2
N_B_LAYERS = DEPTH - N_A_LAYERS
HEAD_DIM = 64
MEM_LEN = 256
MEM_HEADS = 4
MEM_WIDTH = MEM_HEADS * HEAD_DIM
MIX_WIDTH = D_MODEL - MEM_WIDTH
LRU_WIDTH = MIX_WIDTH
LRU_BLOCKS = LRU_WIDTH // HEAD_DIM
LRU_BLOCK = LRU_WIDTH // LRU_BLOCKS
LRU_CONV = 4
LRU_C = 8.0
SWA_HEADS = MIX_WIDTH // HEAD_DIM
SWA_KV_HEADS = 4
SWA_GROUP = SWA_HEADS // SWA_KV_HEADS
WINDOW = 128
BLOCK = 128
D_FF = 2816
FFN_CONV = 3
EPS = 1e-6

kernel_name = "hawk_yoco_swa_sink_alibi_hybrid"


def rmsnorm(x, g):
    xf = x.astype(jnp.float32)
    y = xf * lax.rsqrt(jnp.mean(xf * xf, axis=-1, keepdims=True) + EPS)
    return (y * g.astype(jnp.float32)).astype(x.dtype)


def causal_dwconv(x, w, b):
    width, ch = w.shape
    y = lax.conv_general_dilated(
        x, w[:, None, :].astype(x.dtype), window_strides=(1,), padding=[(width - 1, 0)],
        dimension_numbers=("NWC", "WIO", "NWC"), feature_group_count=ch)
    return y + b.astype(x.dtype)


def alibi_slopes(n):
    def pow2_slopes(m):
        start = 2.0 ** (-8.0 / m)
        return [start ** (i + 1) for i in range(m)]
    c = 2 ** int(math.floor(math.log2(n)))
    s = pow2_slopes(c)
    if c != n:
        s = s + pow2_slopes(2 * c)[0::2][: n - c]
    return np.asarray(s, dtype=np.float32)


def rglru(u_x, u_gate, w_conv, b_conv, w_r, b_r, w_i, b_i, lam):
    bsz, t, _ = u_x.shape
    xc = causal_dwconv(u_x, w_conv, b_conv)
    xb = xc.reshape(bsz, t, LRU_BLOCKS, LRU_BLOCK)
    r = jax.nn.sigmoid(jnp.einsum("btni,nij->btnj", xb, w_r) + b_r).reshape(bsz, t, LRU_WIDTH)
    i = jax.nn.sigmoid(jnp.einsum("btni,nij->btnj", xb, w_i) + b_i).reshape(bsz, t, LRU_WIDTH)
    log_a = -LRU_C * r.astype(jnp.float32) * jax.nn.softplus(-lam.astype(jnp.float32))
    a = jnp.exp(log_a)
    b = jnp.sqrt(-jnp.expm1(2.0 * log_a)) * (i * xc).astype(jnp.float32)

    def combine(lhs, rhs):
        a1, b1 = lhs
        a2, b2 = rhs
        return a1 * a2, a2 * b1 + b2

    _, h = lax.associative_scan(combine, (a, b), axis=1)
    return h.astype(u_x.dtype) * jax.nn.gelu(u_gate, approximate=True)


def band_blocks(t):
    bsz, seq, nh, hd = t.shape
    tb = t.reshape(bsz, seq // BLOCK, BLOCK, nh, hd)
    prev = jnp.pad(tb[:, :-1], ((0, 0), (1, 0), (0, 0), (0, 0), (0, 0)))
    return jnp.concatenate([prev, tb], axis=2)


def swa_sink_attention(q, k_blk, v_blk, sinks, slopes):
    bsz, seq, _, hd = q.shape
    nb = seq // BLOCK
    qb = q.reshape(bsz, nb, BLOCK, SWA_KV_HEADS, SWA_GROUP, hd)
    s = jnp.einsum("bnqkgd,bnskd->bnkgqs", qb, k_blk,
                   preferred_element_type=jnp.float32) * (hd ** -0.5)
    q_pos = jnp.arange(BLOCK)[:, None] + BLOCK
    k_pos = jnp.arange(2 * BLOCK)[None, :]
    dist = q_pos - k_pos
    in_window = (dist >= 0) & (dist < WINDOW)
    has_prev = (jnp.arange(nb)[:, None, None] > 0) | (k_pos[None] >= BLOCK)
    mask = in_window[None] & has_prev
    alibi = -slopes.reshape(SWA_KV_HEADS, SWA_GROUP, 1, 1) * dist.astype(jnp.float32)
    s = jnp.where(mask[None, :, None, None], s + alibi, -jnp.inf)
    sink = sinks.astype(jnp.float32).reshape(1, 1, SWA_KV_HEADS, SWA_GROUP, 1, 1)
    m = jnp.maximum(jnp.max(s, axis=-1, keepdims=True), sink)
    p = jnp.exp(s - m)
    p = p / (jnp.sum(p, axis=-1, keepdims=True) + jnp.exp(sink - m))
    o = jnp.einsum("bnkgqs,bnskd->bnqkgd", p.astype(v_blk.dtype), v_blk)
    return o.reshape(bsz, seq, SWA_HEADS * hd)


def memory_attention(q, k, v):
    s = jnp.einsum("bthd,bmhd->bhtm", q, k, preferred_element_type=jnp.float32) * (q.shape[-1] ** -0.5)
    p = jax.nn.softmax(s, axis=-1)
    o = jnp.einsum("bhtm,bmhd->bthd", p.astype(v.dtype), v)
    return o.reshape(q.shape[0], q.shape[1], MEM_WIDTH)


def conv_gated_ffn(h, w_up, w_conv, b_conv, w_down):
    u = causal_dwconv(h @ w_up, w_conv, b_conv)
    g, v = jnp.split(u, 2, axis=-1)
    return (jax.nn.gelu(g, approximate=True) * v) @ w_down


def _fwd_setup_inputs(seed: int = 0) -> dict:
    key = jax.random.key(seed)
    ks = list(jax.random.split(key, 25))

    def nrm(k, shape, scale):
        return jax.random.normal(k, shape, jnp.float32) * scale

    din = D_MODEL ** -0.5
    x = nrm(ks[0], (BATCH, SEQ, D_MODEL), 1.0)
    mem = nrm(ks[1], (BATCH, MEM_LEN, D_MODEL), 1.0)
    g_mix_pre = 1.0 + nrm(ks[2], (DEPTH, D_MODEL), 0.02)
    g_mix_post = 1.0 + nrm(ks[3], (DEPTH, D_MODEL), 0.02)
    g_ffn_pre = 1.0 + nrm(ks[4], (DEPTH, D_MODEL), 0.02)
    g_ffn_post = 1.0 + nrm(ks[5], (DEPTH, D_MODEL), 0.02)
    g_mem = 1.0 + nrm(ks[6], (DEPTH, D_MODEL), 0.02)
    w_mem_kv = nrm(ks[7], (DEPTH, D_MODEL, 2 * MEM_WIDTH), din)
    w_mix_out = nrm(ks[8], (DEPTH, MIX_WIDTH + MEM_WIDTH, D_MODEL), (MIX_WIDTH + MEM_WIDTH) ** -0.5)
    w_ffn_up = nrm(ks[9], (DEPTH, D_MODEL, 2 * D_FF), din)
    w_ffn_conv = nrm(ks[10], (DEPTH, FFN_CONV, 2 * D_FF), FFN_CONV ** -0.5)
    b_ffn_conv = nrm(ks[11], (DEPTH, 2 * D_FF), 0.01)
    w_ffn_down = nrm(ks[12], (DEPTH, D_FF, D_MODEL), D_FF ** -0.5)
    w_in_a = nrm(ks[13], (N_A_LAYERS, D_MODEL, 2 * LRU_WIDTH + MEM_WIDTH), din)
    w_conv_a = nrm(ks[14], (N_A_LAYERS, LRU_CONV, LRU_WIDTH), LRU_CONV ** -0.5)
    b_conv_a = nrm(ks[15], (N_A_LAYERS, LRU_WIDTH), 0.01)
    w_rg_r = nrm(ks[16], (N_A_LAYERS, LRU_BLOCKS, LRU_BLOCK, LRU_BLOCK), LRU_BLOCK ** -0.5)
    b_rg_r = nrm(ks[17], (N_A_LAYERS, LRU_BLOCKS, LRU_BLOCK), 0.01)
    w_rg_i = nrm(ks[18], (N_A_LAYERS, LRU_BLOCKS, LRU_BLOCK, LRU_BLOCK), LRU_BLOCK ** -0.5)
    b_rg_i = nrm(ks[19], (N_A_LAYERS, LRU_BLOCKS, LRU_BLOCK), 0.01)
    a_base = jax.random.uniform(ks[20], (N_A_LAYERS, LRU_WIDTH), jnp.float32, 0.9, 0.999) ** (1.0 / LRU_C)
    lru_lambda = jnp.log(a_base) - jnp.log1p(-a_base)
    w_in_b = nrm(ks[21], (N_B_LAYERS, D_MODEL, MIX_WIDTH + MEM_WIDTH), din)
    sinks_b = nrm(ks[22], (N_B_LAYERS, SWA_HEADS), 0.5)
    g_kv = 1.0 + nrm(ks[23], (D_MODEL,), 0.02)
    w_kv = nrm(ks[24], (D_MODEL, 2 * SWA_KV_HEADS * HEAD_DIM), din)
    return {"x": x, "mem": mem, "g_mix_pre": g_mix_pre, "g_mix_post": g_mix_post,
            "g_ffn_pre": g_ffn_pre, "g_ffn_post": g_ffn_post, "g_mem": g_mem,
            "w_mem_kv": w_mem_kv, "w_mix_out": w_mix_out, "w_ffn_up": w_ffn_up,
            "w_ffn_conv": w_ffn_conv, "b_ffn_conv": b_ffn_conv, "w_ffn_down": w_ffn_down,
            "w_in_a": w_in_a, "w_conv_a": w_conv_a, "b_conv_a": b_conv_a,
            "w_rg_r": w_rg_r, "b_rg_r": b_rg_r, "w_rg_i": w_rg_i, "b_rg_i": b_rg_i,
            "lru_lambda": lru_lambda, "w_in_b": w_in_b, "sinks_b": sinks_b,
            "g_kv": g_kv, "w_kv": w_kv}


def _fwd_reference(x, mem, g_mix_pre, g_mix_post, g_ffn_pre, g_ffn_post, g_mem, w_mem_kv, w_mix_out,
              w_ffn_up, w_ffn_conv, b_ffn_conv, w_ffn_down, w_in_a, w_conv_a, b_conv_a,
              w_rg_r, b_rg_r, w_rg_i, b_rg_i, lru_lambda, w_in_b, sinks_b, g_kv, w_kv):
    bsz, seq, _ = x.shape
    mem_len = mem.shape[1]
    slopes = jnp.asarray(alibi_slopes(SWA_HEADS))
    k_blk = None
    v_blk = None
    for layer in range(DEPTH):
        if layer == N_A_LAYERS:
            kv = (rmsnorm(x, g_kv) @ w_kv).reshape(bsz, seq, 2, SWA_KV_HEADS, HEAD_DIM)
            k_blk = band_blocks(kv[:, :, 0])
            v_blk = band_blocks(kv[:, :, 1])

        h = rmsnorm(x, g_mix_pre[layer])
        mkv = (rmsnorm(mem, g_mem[layer]) @ w_mem_kv[layer]).reshape(bsz, mem_len, 2, MEM_HEADS, HEAD_DIM)
        if layer < N_A_LAYERS:
            j = layer
            proj = h @ w_in_a[j]
            u_gate, u_x, q_mem = jnp.split(proj, [LRU_WIDTH, 2 * LRU_WIDTH], axis=-1)
            y_main = rglru(u_x, u_gate, w_conv_a[j], b_conv_a[j], w_rg_r[j], b_rg_r[j],
                           w_rg_i[j], b_rg_i[j], lru_lambda[j])
        else:
            j = layer - N_A_LAYERS
            proj = h @ w_in_b[j]
            q_swa, q_mem = jnp.split(proj, [MIX_WIDTH], axis=-1)
            y_main = swa_sink_attention(q_swa.reshape(bsz, seq, SWA_HEADS, HEAD_DIM),
                                        k_blk, v_blk, sinks_b[j], slopes)
        y_mem = memory_attention(q_mem.reshape(bsz, seq, MEM_HEADS, HEAD_DIM), mkv[:, :, 0], mkv[:, :, 1])
        y = jnp.concatenate([y_main, y_mem], axis=-1) @ w_mix_out[layer]
        x = x + rmsnorm(y, g_mix_post[layer])

        h = rmsnorm(x, g_ffn_pre[layer])
        f = conv_gated_ffn(h, w_ffn_up[layer], w_ffn_conv[layer], b_ffn_conv[layer], w_ffn_down[layer])
        x = x + rmsnorm(f, g_ffn_post[layer])
    return x


import jax as _jax
import jax.numpy as _jnp

TWIN_FORMAT = 'train_step'
FWD_PARAMS = ['x', 'mem', 'g_mix_pre', 'g_mix_post', 'g_ffn_pre', 'g_ffn_post', 'g_mem', 'w_mem_kv', 'w_mix_out', 'w_ffn_up', 'w_ffn_conv', 'b_ffn_conv', 'w_ffn_down', 'w_in_a', 'w_conv_a', 'b_conv_a', 'w_rg_r', 'b_rg_r', 'w_rg_i', 'b_rg_i', 'lru_lambda', 'w_in_b', 'sinks_b', 'g_kv', 'w_kv']
TWIN_WEIGHTS = ['g_mix_pre', 'g_mix_post', 'g_ffn_pre', 'g_ffn_post', 'g_mem', 'w_mem_kv', 'w_mix_out', 'w_ffn_up', 'w_ffn_conv', 'b_ffn_conv', 'w_ffn_down', 'w_in_a', 'w_conv_a', 'b_conv_a', 'w_rg_r', 'b_rg_r', 'w_rg_i', 'b_rg_i', 'lru_lambda', 'w_in_b', 'sinks_b', 'g_kv', 'w_kv']
TWIN_DIFF_INPUT = 'x'
TWIN_INPUTS = ['x', 'mem', 'g_mix_pre', 'g_mix_post', 'g_ffn_pre', 'g_ffn_post', 'g_mem', 'w_mem_kv', 'w_mix_out', 'w_ffn_up', 'w_ffn_conv', 'b_ffn_conv', 'w_ffn_down', 'w_in_a', 'w_conv_a', 'b_conv_a', 'w_rg_r', 'b_rg_r', 'w_rg_i', 'b_rg_i', 'lru_lambda', 'w_in_b', 'sinks_b', 'g_kv', 'w_kv', 'loss_target', 'm_g_mix_pre', 'm_g_mix_post', 'm_g_ffn_pre', 'm_g_ffn_post', 'm_g_mem', 'm_w_mem_kv', 'm_w_mix_out', 'm_w_ffn_up', 'm_w_ffn_conv', 'm_b_ffn_conv', 'm_w_ffn_down', 'm_w_in_a', 'm_w_conv_a', 'm_b_conv_a', 'm_w_rg_r', 'm_b_rg_r', 'm_w_rg_i', 'm_b_rg_i', 'm_lru_lambda', 'm_w_in_b', 'm_sinks_b', 'm_g_kv', 'm_w_kv', 'v_g_mix_pre', 'v_g_mix_post', 'v_g_ffn_pre', 'v_g_ffn_post', 'v_g_mem', 'v_w_mem_kv', 'v_w_mix_out', 'v_w_ffn_up', 'v_w_ffn_conv', 'v_b_ffn_conv', 'v_w_ffn_down', 'v_w_in_a', 'v_w_conv_a', 'v_b_conv_a', 'v_w_rg_r', 'v_b_rg_r', 'v_w_rg_i', 'v_b_rg_i', 'v_lru_lambda', 'v_w_in_b', 'v_sinks_b', 'v_g_kv', 'v_w_kv']
TWIN_OUTPUTS = ['loss', 'grad_x', 'grad_g_mix_pre', 'grad_g_mix_post', 'grad_g_ffn_pre', 'grad_g_ffn_post', 'grad_g_mem', 'grad_w_mem_kv', 'grad_w_mix_out', 'grad_w_ffn_up', 'grad_w_ffn_conv', 'grad_b_ffn_conv', 'grad_w_ffn_down', 'grad_w_in_a', 'grad_w_conv_a', 'grad_b_conv_a', 'grad_w_rg_r', 'grad_b_rg_r', 'grad_w_rg_i', 'grad_b_rg_i', 'grad_lru_lambda', 'grad_w_in_b', 'grad_sinks_b', 'grad_g_kv', 'grad_w_kv', 'delta_g_mix_pre', 'delta_g_mix_post', 'delta_g_ffn_pre', 'delta_g_ffn_post', 'delta_g_mem', 'delta_w_mem_kv', 'delta_w_mix_out', 'delta_w_ffn_up', 'delta_w_ffn_conv', 'delta_b_ffn_conv', 'delta_w_ffn_down', 'delta_w_in_a', 'delta_w_conv_a', 'delta_b_conv_a', 'delta_w_rg_r', 'delta_b_rg_r', 'delta_w_rg_i', 'delta_b_rg_i', 'delta_lru_lambda', 'delta_w_in_b', 'delta_sinks_b', 'delta_g_kv', 'delta_w_kv', 'new_m_g_mix_pre', 'new_m_g_mix_post', 'new_m_g_ffn_pre', 'new_m_g_ffn_post', 'new_m_g_mem', 'new_m_w_mem_kv', 'new_m_w_mix_out', 'new_m_w_ffn_up', 'new_m_w_ffn_conv', 'new_m_b_ffn_conv', 'new_m_w_ffn_down', 'new_m_w_in_a', 'new_m_w_conv_a', 'new_m_b_conv_a', 'new_m_w_rg_r', 'new_m_b_rg_r', 'new_m_w_rg_i', 'new_m_b_rg_i', 'new_m_lru_lambda', 'new_m_w_in_b', 'new_m_sinks_b', 'new_m_g_kv', 'new_m_w_kv', 'new_v_g_mix_pre', 'new_v_g_mix_post', 'new_v_g_ffn_pre', 'new_v_g_ffn_post', 'new_v_g_mem', 'new_v_w_mem_kv', 'new_v_w_mix_out', 'new_v_w_ffn_up', 'new_v_w_ffn_conv', 'new_v_b_ffn_conv', 'new_v_w_ffn_down', 'new_v_w_in_a', 'new_v_w_conv_a', 'new_v_b_conv_a', 'new_v_w_rg_r', 'new_v_b_rg_r', 'new_v_w_rg_i', 'new_v_b_rg_i', 'new_v_lru_lambda', 'new_v_w_in_b', 'new_v_sinks_b', 'new_v_g_kv', 'new_v_w_kv']
TWIN_LEAF_KINDS = {'loss': 'loss', 'grad_x': 'grad_x', 'grad_g_mix_pre': 'grad_w', 'grad_g_mix_post': 'grad_w', 'grad_g_ffn_pre': 'grad_w', 'grad_g_ffn_post': 'grad_w', 'grad_g_mem': 'grad_w', 'grad_w_mem_kv': 'grad_w', 'grad_w_mix_out': 'grad_w', 'grad_w_ffn_up': 'grad_w', 'grad_w_ffn_conv': 'grad_w', 'grad_b_ffn_conv': 'grad_w', 'grad_w_ffn_down': 'grad_w', 'grad_w_in_a': 'grad_w', 'grad_w_conv_a': 'grad_w', 'grad_b_conv_a': 'grad_w', 'grad_w_rg_r': 'grad_w', 'grad_b_rg_r': 'grad_w', 'grad_w_rg_i': 'grad_w', 'grad_b_rg_i': 'grad_w', 'grad_lru_lambda': 'grad_w', 'grad_w_in_b': 'grad_w', 'grad_sinks_b': 'grad_w', 'grad_g_kv': 'grad_w', 'grad_w_kv': 'grad_w', 'delta_g_mix_pre': 'delta_w', 'delta_g_mix_post': 'delta_w', 'delta_g_ffn_pre': 'delta_w', 'delta_g_ffn_post': 'delta_w', 'delta_g_mem': 'delta_w', 'delta_w_mem_kv': 'delta_w', 'delta_w_mix_out': 'delta_w', 'delta_w_ffn_up': 'delta_w', 'delta_w_ffn_conv': 'delta_w', 'delta_b_ffn_conv': 'delta_w', 'delta_w_ffn_down': 'delta_w', 'delta_w_in_a': 'delta_w', 'delta_w_conv_a': 'delta_w', 'delta_b_conv_a': 'delta_w', 'delta_w_rg_r': 'delta_w', 'delta_b_rg_r': 'delta_w', 'delta_w_rg_i': 'delta_w', 'delta_b_rg_i': 'delta_w', 'delta_lru_lambda': 'delta_w', 'delta_w_in_b': 'delta_w', 'delta_sinks_b': 'delta_w', 'delta_g_kv': 'delta_w', 'delta_w_kv': 'delta_w', 'new_m_g_mix_pre': 'new_m', 'new_m_g_mix_post': 'new_m', 'new_m_g_ffn_pre': 'new_m', 'new_m_g_ffn_post': 'new_m', 'new_m_g_mem': 'new_m', 'new_m_w_mem_kv': 'new_m', 'new_m_w_mix_out': 'new_m', 'new_m_w_ffn_up': 'new_m', 'new_m_w_ffn_conv': 'new_m', 'new_m_b_ffn_conv': 'new_m', 'new_m_w_ffn_down': 'new_m', 'new_m_w_in_a': 'new_m', 'new_m_w_conv_a': 'new_m', 'new_m_b_conv_a': 'new_m', 'new_m_w_rg_r': 'new_m', 'new_m_b_rg_r': 'new_m', 'new_m_w_rg_i': 'new_m', 'new_m_b_rg_i': 'new_m', 'new_m_lru_lambda': 'new_m', 'new_m_w_in_b': 'new_m', 'new_m_sinks_b': 'new_m', 'new_m_g_kv': 'new_m', 'new_m_w_kv': 'new_m', 'new_v_g_mix_pre': 'new_v', 'new_v_g_mix_post': 'new_v', 'new_v_g_ffn_pre': 'new_v', 'new_v_g_ffn_post': 'new_v', 'new_v_g_mem': 'new_v', 'new_v_w_mem_kv': 'new_v', 'new_v_w_mix_out': 'new_v', 'new_v_w_ffn_up': 'new_v', 'new_v_w_ffn_conv': 'new_v', 'new_v_b_ffn_conv': 'new_v', 'new_v_w_ffn_down': 'new_v', 'new_v_w_in_a': 'new_v', 'new_v_w_conv_a': 'new_v', 'new_v_b_conv_a': 'new_v', 'new_v_w_rg_r': 'new_v', 'new_v_b_rg_r': 'new_v', 'new_v_w_rg_i': 'new_v', 'new_v_b_rg_i': 'new_v', 'new_v_lru_lambda': 'new_v', 'new_v_w_in_b': 'new_v', 'new_v_sinks_b': 'new_v', 'new_v_g_kv': 'new_v', 'new_v_w_kv': 'new_v'}


def _forward(args):
    return _fwd_reference(*[args[k] for k in FWD_PARAMS])


def _output_shape():
    out = _jax.eval_shape(lambda: _forward(_fwd_setup_inputs(0)))
    return out.shape, out.dtype

N_MICROBATCH = 1
ADAM_LR = 0.001
ADAM_B1 = 0.9
ADAM_B2 = 0.999
ADAM_EPS = 1e-08
ADAM_WD = 0.01
ADAM_STEP = 10
PER_EXAMPLE_BATCH_AXIS = {'x': 0, 'mem': 0, 'loss_target': 0}
SHARED_INPUTS = []
_WEIGHT_DTYPES = {'g_mix_pre': _jnp.float32, 'g_mix_post': _jnp.float32, 'g_ffn_pre': _jnp.float32, 'g_ffn_post': _jnp.float32, 'g_mem': _jnp.float32, 'w_mem_kv': _jnp.float32, 'w_mix_out': _jnp.float32, 'w_ffn_up': _jnp.float32, 'w_ffn_conv': _jnp.float32, 'b_ffn_conv': _jnp.float32, 'w_ffn_down': _jnp.float32, 'w_in_a': _jnp.float32, 'w_conv_a': _jnp.float32, 'b_conv_a': _jnp.float32, 'w_rg_r': _jnp.float32, 'b_rg_r': _jnp.float32, 'w_rg_i': _jnp.float32, 'b_rg_i': _jnp.float32, 'lru_lambda': _jnp.float32, 'w_in_b': _jnp.float32, 'sinks_b': _jnp.float32, 'g_kv': _jnp.float32, 'w_kv': _jnp.float32}
MOMENT_SCALE = {'g_mix_pre': 5.255539e+00, 'g_mix_post': 3.529947e+01, 'g_ffn_pre': 3.812337e+00, 'g_ffn_post': 3.171324e+01, 'g_mem': 1.151294e+00, 'w_mem_kv': 1.621453e+00, 'w_mix_out': 1.250940e+01, 'w_ffn_up': 1.490918e+00, 'w_ffn_conv': 1.722759e+00, 'b_ffn_conv': 8.053332e+00, 'w_ffn_down': 3.395621e+00, 'w_in_a': 5.733291e+00, 'w_conv_a': 1.343745e+01, 'b_conv_a': 1.383638e+02, 'w_rg_r': 5.364356e+00, 'b_rg_r': 3.342084e+00, 'w_rg_i': 9.863122e+00, 'b_rg_i': 3.965537e+00, 'lru_lambda': 5.631842e+00, 'w_in_b': 5.038746e-01, 'sinks_b': 9.360041e+00, 'g_kv': 1.291763e+01, 'w_kv': 1.842101e+01}


def _to_microbatches(a, axis):
    t = _jnp.moveaxis(a, axis, 0)
    t = t.reshape((N_MICROBATCH, t.shape[0] // N_MICROBATCH) + t.shape[1:])
    return _jnp.moveaxis(t, 1, axis + 1)


def setup_inputs(seed: int = 0) -> dict:
    inp = _fwd_setup_inputs(seed)
    key = _jax.random.fold_in(_jax.random.key(seed), 7919)
    shape, _ = _output_shape()
    out = dict(inp)
    out["loss_target"] = _jax.random.normal(_jax.random.fold_in(key, 0), shape, _jnp.float32)
    for i, name in enumerate(TWIN_WEIGHTS):
        w = inp[name].astype(_jnp.float32)
        if MOMENT_SCALE is None:
            s = _jnp.sqrt(_jnp.mean(_jnp.square(w)) + 1e-30)
        else:
            s = MOMENT_SCALE[name]
        km, kv = _jax.random.split(_jax.random.fold_in(key, i + 1))
        out[name] = w
        out["m_" + name] = s * _jax.random.normal(km, w.shape, _jnp.float32)
        out["v_" + name] = (s * s) * _jax.random.uniform(kv, w.shape, _jnp.float32, 0.5, 1.5)
    if N_MICROBATCH > 1:
        for name, axis in PER_EXAMPLE_BATCH_AXIS.items():
            out[name] = _to_microbatches(out[name], axis)
    return {'x': out['x'], 'mem': out['mem'], 'g_mix_pre': out['g_mix_pre'], 'g_mix_post': out['g_mix_post'], 'g_ffn_pre': out['g_ffn_pre'], 'g_ffn_post': out['g_ffn_post'], 'g_mem': out['g_mem'], 'w_mem_kv': out['w_mem_kv'], 'w_mix_out': out['w_mix_out'], 'w_ffn_up': out['w_ffn_up'], 'w_ffn_conv': out['w_ffn_conv'], 'b_ffn_conv': out['b_ffn_conv'], 'w_ffn_down': out['w_ffn_down'], 'w_in_a': out['w_in_a'], 'w_conv_a': out['w_conv_a'], 'b_conv_a': out['b_conv_a'], 'w_rg_r': out['w_rg_r'], 'b_rg_r': out['b_rg_r'], 'w_rg_i': out['w_rg_i'], 'b_rg_i': out['b_rg_i'], 'lru_lambda': out['lru_lambda'], 'w_in_b': out['w_in_b'], 'sinks_b': out['sinks_b'], 'g_kv': out['g_kv'], 'w_kv': out['w_kv'], 'loss_target': out['loss_target'], 'm_g_mix_pre': out['m_g_mix_pre'], 'm_g_mix_post': out['m_g_mix_post'], 'm_g_ffn_pre': out['m_g_ffn_pre'], 'm_g_ffn_post': out['m_g_ffn_post'], 'm_g_mem': out['m_g_mem'], 'm_w_mem_kv': out['m_w_mem_kv'], 'm_w_mix_out': out['m_w_mix_out'], 'm_w_ffn_up': out['m_w_ffn_up'], 'm_w_ffn_conv': out['m_w_ffn_conv'], 'm_b_ffn_conv': out['m_b_ffn_conv'], 'm_w_ffn_down': out['m_w_ffn_down'], 'm_w_in_a': out['m_w_in_a'], 'm_w_conv_a': out['m_w_conv_a'], 'm_b_conv_a': out['m_b_conv_a'], 'm_w_rg_r': out['m_w_rg_r'], 'm_b_rg_r': out['m_b_rg_r'], 'm_w_rg_i': out['m_w_rg_i'], 'm_b_rg_i': out['m_b_rg_i'], 'm_lru_lambda': out['m_lru_lambda'], 'm_w_in_b': out['m_w_in_b'], 'm_sinks_b': out['m_sinks_b'], 'm_g_kv': out['m_g_kv'], 'm_w_kv': out['m_w_kv'], 'v_g_mix_pre': out['v_g_mix_pre'], 'v_g_mix_post': out['v_g_mix_post'], 'v_g_ffn_pre': out['v_g_ffn_pre'], 'v_g_ffn_post': out['v_g_ffn_post'], 'v_g_mem': out['v_g_mem'], 'v_w_mem_kv': out['v_w_mem_kv'], 'v_w_mix_out': out['v_w_mix_out'], 'v_w_ffn_up': out['v_w_ffn_up'], 'v_w_ffn_conv': out['v_w_ffn_conv'], 'v_b_ffn_conv': out['v_b_ffn_conv'], 'v_w_ffn_down': out['v_w_ffn_down'], 'v_w_in_a': out['v_w_in_a'], 'v_w_conv_a': out['v_w_conv_a'], 'v_b_conv_a': out['v_b_conv_a'], 'v_w_rg_r': out['v_w_rg_r'], 'v_b_rg_r': out['v_b_rg_r'], 'v_w_rg_i': out['v_w_rg_i'], 'v_b_rg_i': out['v_b_rg_i'], 'v_lru_lambda': out['v_lru_lambda'], 'v_w_in_b': out['v_w_in_b'], 'v_sinks_b': out['v_sinks_b'], 'v_g_kv': out['v_g_kv'], 'v_w_kv': out['v_w_kv']}


def _loss(weights, diff, rest, loss_target):
    with _jax.named_scope("forward"):
        args = {**rest, TWIN_DIFF_INPUT: diff, **{k: w.astype(_WEIGHT_DTYPES[k]) for k, w in weights.items()}}
        y = _forward(args)
    with _jax.named_scope("loss_head"):
        err = _jnp.square(y.astype(_jnp.float32) - loss_target)
        return 0.5 * _jnp.sum(_jnp.mean(err, axis=-1)) if err.ndim else 0.5 * err


def _adamw(w, g, m, v):
    m = ADAM_B1 * m + (1.0 - ADAM_B1) * g
    v = ADAM_B2 * v + (1.0 - ADAM_B2) * _jnp.square(g)
    m_hat = m / (1.0 - ADAM_B1 ** ADAM_STEP)
    v_hat = v / (1.0 - ADAM_B2 ** ADAM_STEP)
    delta = -ADAM_LR * (m_hat / (_jnp.sqrt(v_hat) + ADAM_EPS) + ADAM_WD * w)
    return delta, m, v


def reference(x, mem, g_mix_pre, g_mix_post, g_ffn_pre, g_ffn_post, g_mem, w_mem_kv, w_mix_out, w_ffn_up, w_ffn_conv, b_ffn_conv, w_ffn_down, w_in_a, w_conv_a, b_conv_a, w_rg_r, b_rg_r, w_rg_i, b_rg_i, lru_lambda, w_in_b, sinks_b, g_kv, w_kv, loss_target, m_g_mix_pre, m_g_mix_post, m_g_ffn_pre, m_g_ffn_post, m_g_mem, m_w_mem_kv, m_w_mix_out, m_w_ffn_up, m_w_ffn_conv, m_b_ffn_conv, m_w_ffn_down, m_w_in_a, m_w_conv_a, m_b_conv_a, m_w_rg_r, m_b_rg_r, m_w_rg_i, m_b_rg_i, m_lru_lambda, m_w_in_b, m_sinks_b, m_g_kv, m_w_kv, v_g_mix_pre, v_g_mix_post, v_g_ffn_pre, v_g_ffn_post, v_g_mem, v_w_mem_kv, v_w_mix_out, v_w_ffn_up, v_w_ffn_conv, v_b_ffn_conv, v_w_ffn_down, v_w_in_a, v_w_conv_a, v_b_conv_a, v_w_rg_r, v_b_rg_r, v_w_rg_i, v_b_rg_i, v_lru_lambda, v_w_in_b, v_sinks_b, v_g_kv, v_w_kv):
    given = dict(x=x, mem=mem, g_mix_pre=g_mix_pre, g_mix_post=g_mix_post, g_ffn_pre=g_ffn_pre, g_ffn_post=g_ffn_post, g_mem=g_mem, w_mem_kv=w_mem_kv, w_mix_out=w_mix_out, w_ffn_up=w_ffn_up, w_ffn_conv=w_ffn_conv, b_ffn_conv=b_ffn_conv, w_ffn_down=w_ffn_down, w_in_a=w_in_a, w_conv_a=w_conv_a, b_conv_a=b_conv_a, w_rg_r=w_rg_r, b_rg_r=b_rg_r, w_rg_i=w_rg_i, b_rg_i=b_rg_i, lru_lambda=lru_lambda, w_in_b=w_in_b, sinks_b=sinks_b, g_kv=g_kv, w_kv=w_kv, loss_target=loss_target, m_g_mix_pre=m_g_mix_pre, m_g_mix_post=m_g_mix_post, m_g_ffn_pre=m_g_ffn_pre, m_g_ffn_post=m_g_ffn_post, m_g_mem=m_g_mem, m_w_mem_kv=m_w_mem_kv, m_w_mix_out=m_w_mix_out, m_w_ffn_up=m_w_ffn_up, m_w_ffn_conv=m_w_ffn_conv, m_b_ffn_conv=m_b_ffn_conv, m_w_ffn_down=m_w_ffn_down, m_w_in_a=m_w_in_a, m_w_conv_a=m_w_conv_a, m_b_conv_a=m_b_conv_a, m_w_rg_r=m_w_rg_r, m_b_rg_r=m_b_rg_r, m_w_rg_i=m_w_rg_i, m_b_rg_i=m_b_rg_i, m_lru_lambda=m_lru_lambda, m_w_in_b=m_w_in_b, m_sinks_b=m_sinks_b, m_g_kv=m_g_kv, m_w_kv=m_w_kv, v_g_mix_pre=v_g_mix_pre, v_g_mix_post=v_g_mix_post, v_g_ffn_pre=v_g_ffn_pre, v_g_ffn_post=v_g_ffn_post, v_g_mem=v_g_mem, v_w_mem_kv=v_w_mem_kv, v_w_mix_out=v_w_mix_out, v_w_ffn_up=v_w_ffn_up, v_w_ffn_conv=v_w_ffn_conv, v_b_ffn_conv=v_b_ffn_conv, v_w_ffn_down=v_w_ffn_down, v_w_in_a=v_w_in_a, v_w_conv_a=v_w_conv_a, v_b_conv_a=v_b_conv_a, v_w_rg_r=v_w_rg_r, v_b_rg_r=v_b_rg_r, v_w_rg_i=v_w_rg_i, v_b_rg_i=v_b_rg_i, v_lru_lambda=v_lru_lambda, v_w_in_b=v_w_in_b, v_sinks_b=v_sinks_b, v_g_kv=v_g_kv, v_w_kv=v_w_kv)
    weights = {n: given[n] for n in TWIN_WEIGHTS}
    shared = {n: given[n] for n in SHARED_INPUTS}
    per_example = {n: given[n] for n in ['x', 'mem']}
    grad_fn = _jax.value_and_grad(_loss, argnums=(0, 1))

    def one_microbatch(ex, loss_target):
        ex = dict(ex)
        diff = ex.pop(TWIN_DIFF_INPUT)
        return grad_fn(weights, diff, {**shared, **ex}, loss_target)

    if N_MICROBATCH == 1:
        loss, (grad_w, grad_x) = one_microbatch(per_example, given["loss_target"])
    else:
        def body(carry, xs):
            loss_sum, grad_sum = carry
            l_k, (gw_k, gx_k) = one_microbatch(xs[0], xs[1])
            with _jax.named_scope("update"):
                return (loss_sum + l_k, _jax.tree.map(_jnp.add, grad_sum, gw_k)), gx_k

        init = (_jnp.zeros((), _jnp.float32), _jax.tree.map(_jnp.zeros_like, weights))
        (loss, grad_w), grad_x = _jax.lax.scan(body, init, (per_example, given["loss_target"]))
    with _jax.named_scope("update"):
        delta_w, new_m, new_v = {}, {}, {}
        for n in TWIN_WEIGHTS:
            delta_w[n], new_m[n], new_v[n] = _adamw(weights[n], grad_w[n], given["m_" + n], given["v_" + n])
    return (loss, grad_x, *[grad_w[n] for n in TWIN_WEIGHTS], *[delta_w[n] for n in TWIN_WEIGHTS],
            *[new_m[n] for n in TWIN_WEIGHTS], *[new_v[n] for n in TWIN_WEIGHTS])
```

```python
import functools
import math

import numpy as np
import jax
import jax.numpy as jnp
from jax import lax
from jax.experimental import pallas as pl
from jax.experimental.pallas import tpu as pltpu

F32 = jnp.float32
BF16 = jnp.bfloat16
S = jax.ShapeDtypeStruct
MESH = pl.DeviceIdType.MESH
ANY = pl.BlockSpec(memory_space=pl.ANY)

HEAD = 64
MEM_HEADS = 4
MEM_W = MEM_HEADS * HEAD
SWA_HEADS = 12
SWA_GROUP = 3
MIX_W = SWA_HEADS * HEAD
WIN = 128
LRU_C = 8.0
EPS = 1e-6
ADAM_LR, ADAM_B1, ADAM_B2, ADAM_EPS, ADAM_WD, ADAM_STEP = 0.001, 0.9, 0.999, 1e-08, 0.01, 10
GELU_C0 = math.sqrt(2.0 / math.pi)
GELU_C1 = 0.044715
N_DEV = 8
LANES = 128
CT = 128
VMEM_LIMIT = 48 * 1024 * 1024
REPL_ROWS = 256

SHARDED = (("w_mem_kv", 1), ("w_mix_out", 1), ("w_ffn_up", 2), ("w_ffn_conv", 2), ("w_ffn_down", 1), ("w_in_a", 2),
           ("w_conv_a", 2), ("b_conv_a", 1), ("lru_lambda", 1), ("w_in_b", 1), ("w_kv", 0))
SMALL_SHARDED = ("w_ffn_conv", "w_conv_a", "b_conv_a", "lru_lambda")
REPL = ("g_mix_pre", "g_mix_post", "g_ffn_pre", "g_ffn_post", "g_mem", "b_ffn_conv", "w_rg_r", "b_rg_r", "w_rg_i",
        "b_rg_i", "sinks_b", "g_kv")
WEIGHTS = ("g_mix_pre", "g_mix_post", "g_ffn_pre", "g_ffn_post", "g_mem", "w_mem_kv", "w_mix_out", "w_ffn_up",
           "w_ffn_conv", "b_ffn_conv", "w_ffn_down", "w_in_a", "w_conv_a", "b_conv_a", "w_rg_r", "b_rg_r", "w_rg_i",
           "b_rg_i", "lru_lambda", "w_in_b", "sinks_b", "g_kv", "w_kv")


def _alibi_slopes(n):
    def pow2(m):
        start = 2.0 ** (-8.0 / m)
        return [start ** (i + 1) for i in range(m)]
    c = 2 ** int(math.floor(math.log2(n)))
    s = pow2(c)
    if c != n:
        s = s + pow2(2 * c)[0::2][: n - c]
    return [float(v) for v in np.asarray(s, dtype=np.float32)]


SLOPES = _alibi_slopes(SWA_HEADS)


def _tile(n, cands):
    for c in cands:
        if n % c == 0:
            return c
    return n


def _cparams(*sem):
    return pltpu.CompilerParams(dimension_semantics=sem, vmem_limit_bytes=VMEM_LIMIT)


def _mm(a, b, *, ta=False, tb=False, n=None, k=None, bl=None, b_off=(0, 0), out_dtype=F32, add=None, into=None,
        name="mm"):
    if ta:
        K, M = a.shape
    else:
        M, K = a.shape
    if tb:
        N = b.shape[-2] if n is None else n
    else:
        N = b.shape[-1] if n is None else n
    assert k is None or k == K
    tm = _tile(M, (512, 256, 128))
    tn = _tile(N, (1408, 1024, 896, 768, 512, 384, 256, 128))
    tk = K if K <= 1024 else _tile(K, (1024, 512, 256, 128))
    nk = K // tk
    ro, co = b_off
    lead = () if bl is None else (bl,)
    lead_blk = () if bl is None else (None,)
    assert (b.ndim == 3) == (bl is not None)
    if tb:
        assert ro % tn == 0 and co % tk == 0
        b_spec = pl.BlockSpec(lead_blk + (tn, tk), lambda i, j, kk: lead + (j + ro // tn, kk + co // tk))
        b_dims = (1,)
    else:
        assert ro % tk == 0 and co % tn == 0
        b_spec = pl.BlockSpec(lead_blk + (tk, tn), lambda i, j, kk: lead + (kk + ro // tk, j + co // tn))
        b_dims = (0,)
    if ta:
        a_spec = pl.BlockSpec((tk, tm), lambda i, j, kk: (kk, i))
        a_dims = (0,)
    else:
        a_spec = pl.BlockSpec((tm, tk), lambda i, j, kk: (i, kk))
        a_dims = (1,)
    dims = ((a_dims, b_dims), ((), ()))
    add_spec = pl.BlockSpec((tm, tn), lambda i, j, kk: (i, j))
    has_add = add is not None
    if into is None:
        o_spec, o_shape, buf = add_spec, (M, N), None
    else:
        buf, o_shape, ol, (oro, oco) = into
        assert oro % tm == 0 and oco % tn == 0 and not has_add
        o_spec = pl.BlockSpec((None, tm, tn), lambda i, j, kk: (ol, i + oro // tm, j + oco // tn))
    has_buf = buf is not None

    def body(*refs):
        if has_add:
            a_ref, b_ref, add_ref, o_ref, acc_ref = refs
        elif has_buf:
            a_ref, b_ref, _, o_ref, acc_ref = refs
        else:
            a_ref, b_ref, o_ref, acc_ref = refs
        kk = pl.program_id(2)

        @pl.when(kk == 0)
        def _():
            acc_ref[...] = jnp.zeros_like(acc_ref)

        acc_ref[...] += lax.dot_general(a_ref[...].astype(BF16), b_ref[...].astype(BF16), dims,
                                        preferred_element_type=F32)

        @pl.when(kk == nk - 1)
        def _():
            r = acc_ref[...]
            if has_add:
                r = r + add_ref[...].astype(F32)
            o_ref[...] = r.astype(out_dtype)

    in_specs = [a_spec, b_spec] + ([add_spec] if has_add else []) + ([ANY] if has_buf else [])
    args = (a, b) + ((add,) if has_add else ()) + ((buf,) if has_buf else ())
    return pl.pallas_call(
        body, grid=(M // tm, N // tn, nk), in_specs=in_specs, out_specs=o_spec,
        out_shape=S(o_shape, out_dtype), scratch_shapes=[pltpu.VMEM((tm, tn), F32)],
        input_output_aliases={2: 0} if has_buf else {},
        compiler_params=_cparams("parallel", "parallel", "arbitrary"), name=name)(*args)


def _rms_fwd(x, g, out_dtype, res=None, name="rms_fwd"):
    N, D = x.shape
    tm = _tile(N, (512, 256, 128))
    has_res = res is not None

    def body(*refs):
        if has_res:
            x_ref, g_ref, r_ref, o_ref = refs
        else:
            x_ref, g_ref, o_ref = refs
        xv = x_ref[...].astype(F32)
        y = xv * lax.rsqrt(jnp.mean(xv * xv, axis=-1, keepdims=True) + EPS) * g_ref[...]
        if has_res:
            y = y + r_ref[...]
        o_ref[...] = y.astype(out_dtype)

    row = pl.BlockSpec((tm, D), lambda i: (i, 0))
    vec = pl.BlockSpec((1, D), lambda i: (0, 0))
    return pl.pallas_call(
        body, grid=(N // tm,), in_specs=[row, vec] + ([row] if has_res else []), out_specs=row,
        out_shape=S((N, D), out_dtype), compiler_params=_cparams("parallel"), name=name)(
            *((x, g) + ((res,) if has_res else ())))


def _rms_bwd(x, g, dy, add=None, out_dtype=F32, name="rms_bwd"):
    N, D = x.shape
    tm = _tile(N, (512, 256, 128))
    has_add = add is not None

    def body(*refs):
        if has_add:
            x_ref, g_ref, dy_ref, add_ref, dx_ref, dg_ref = refs
        else:
            x_ref, g_ref, dy_ref, dx_ref, dg_ref = refs
        xv = x_ref[...].astype(F32)
        dyv = dy_ref[...].astype(F32)
        r = lax.rsqrt(jnp.mean(xv * xv, axis=-1, keepdims=True) + EPS)
        u = dyv * g_ref[...]
        dx = r * u - xv * (r * r * r * jnp.mean(u * xv, axis=-1, keepdims=True))
        if has_add:
            dx = dx + add_ref[...]
        dx_ref[...] = dx.astype(out_dtype)

        @pl.when(pl.program_id(0) == 0)
        def _():
            dg_ref[...] = jnp.zeros_like(dg_ref)

        dg_ref[...] += jnp.sum(dyv * xv * r, axis=0, keepdims=True)

    row = pl.BlockSpec((tm, D), lambda i: (i, 0))
    vec = pl.BlockSpec((1, D), lambda i: (0, 0))
    return pl.pallas_call(
        body, grid=(N // tm,), in_specs=[row, vec, row] + ([row] if has_add else []), out_specs=(row, vec),
        out_shape=(S((N, D), out_dtype), S((1, D), F32)), compiler_params=_cparams("arbitrary"), name=name)(
            *((x, g, dy) + ((add,) if has_add else ())))


def _shift_down(x, s, row):
    return jnp.where(row >= s, pltpu.roll(x, s, axis=0), 0.0)


def _shift_up(x, s, row):
    T = x.shape[0]
    return jnp.where(row < T - s, pltpu.roll(x, T - s, axis=0), 0.0)


def _conv(x, w_ref, b_ref, row):
    W = w_ref.shape[0]
    y = x * w_ref[W - 1:W, :] + b_ref[...]
    for s in range(1, W):
        y = y + _shift_down(x, s, row) * w_ref[W - 1 - s:W - s, :]
    return y


def _conv_bwd(dy, x, w_ref, row):
    W = w_ref.shape[0]
    dx = dy * w_ref[W - 1:W, :]
    dws = [None] * W
    dws[W - 1] = jnp.sum(dy * x, axis=0, keepdims=True)
    for s in range(1, W):
        dx = dx + _shift_up(dy, s, row) * w_ref[W - 1 - s:W - s, :]
        dws[W - 1 - s] = jnp.sum(dy * _shift_down(x, s, row), axis=0, keepdims=True)
    return dx, jnp.concatenate(dws, axis=0), jnp.sum(dy, axis=0, keepdims=True)


def _gelu(g):
    t = jnp.tanh(GELU_C0 * (g + GELU_C1 * g * g * g))
    return 0.5 * g * (1.0 + t), t


def _dgelu(g, t):
    return 0.5 * (1.0 + t) + 0.5 * g * (1.0 - t * t) * (GELU_C0 * (1.0 + 3.0 * GELU_C1 * g * g))


def _cspec(T, off=0):
    return pl.BlockSpec((1, T, CT), lambda j, b: (b, 0, j + off))


def _pspec(rows, off=0):
    return pl.BlockSpec((rows, CT), lambda j, b: (0, j + off))


def _conv_fwd_call(x3, x_off, C, w, b, name):
    Bl, T, _ = x3.shape
    W = w.shape[0]

    def body(x_ref, w_ref, b_ref, o_ref):
        row = lax.broadcasted_iota(jnp.int32, (T, CT), 0)
        o_ref[0] = _conv(x_ref[0], w_ref, b_ref, row)

    return pl.pallas_call(
        body, grid=(C // CT, Bl), in_specs=[_cspec(T, x_off // CT), _pspec(W), _pspec(1)], out_specs=_cspec(T),
        out_shape=S((Bl, T, C), F32), compiler_params=_cparams("parallel", "arbitrary"), name=name)(x3, w, b)


def _conv_bwd_call(dy3, x3, x_off, C, w, name):
    Bl, T, _ = x3.shape
    W = w.shape[0]

    def body(dy_ref, x_ref, w_ref, dx_ref, dw_ref, db_ref):
        row = lax.broadcasted_iota(jnp.int32, (T, CT), 0)
        dx, dw, db = _conv_bwd(dy_ref[0], x_ref[0], w_ref, row)
        dx_ref[0] = dx.astype(BF16)

        @pl.when(pl.program_id(1) == 0)
        def _():
            dw_ref[...] = jnp.zeros_like(dw_ref)
            db_ref[...] = jnp.zeros_like(db_ref)

        dw_ref[...] += dw
        db_ref[...] += db

    return pl.pallas_call(
        body, grid=(C // CT, Bl), in_specs=[_cspec(T), _cspec(T, x_off // CT), _pspec(W)],
        out_specs=(_cspec(T), _pspec(W), _pspec(1)),
        out_shape=(S((Bl, T, C), BF16), S((W, C), F32), S((1, C), F32)),
        compiler_params=_cparams("parallel", "arbitrary"), name=name)(dy3, x3, w)


def _ffn_mid_fwd(ug3, uv3, wc, bc, name):
    Bl, T, F = ug3.shape
    nf = F // CT

    def body(ug_ref, uv_ref, wg_ref, wv_ref, bg_ref, bv_ref, o_ref):
        row = lax.broadcasted_iota(jnp.int32, (T, CT), 0)
        g = _conv(ug_ref[0], wg_ref, bg_ref, row)
        v = _conv(uv_ref[0], wv_ref, bv_ref, row)
        o_ref[0] = (_gelu(g)[0] * v).astype(BF16)

    return pl.pallas_call(
        body, grid=(nf, Bl),
        in_specs=[_cspec(T), _cspec(T), _pspec(3), _pspec(3, nf), _pspec(1), _pspec(1, nf)], out_specs=_cspec(T),
        out_shape=S((Bl, T, F), BF16), compiler_params=_cparams("parallel", "arbitrary"), name=name)(
            ug3, uv3, wc, wc, bc, bc)


def _ffn_mid_bwd(ug3, uv3, dact3, wc, bc, name):
    Bl, T, F = ug3.shape
    nf = F // CT

    def body(ug_ref, uv_ref, da_ref, wg_ref, wv_ref, bg_ref, bv_ref, dug_ref, duv_ref, dwg_ref, dwv_ref, dbg_ref,
             dbv_ref):
        row = lax.broadcasted_iota(jnp.int32, (T, CT), 0)
        ug = ug_ref[0]
        uv = uv_ref[0]
        g = _conv(ug, wg_ref, bg_ref, row)
        v = _conv(uv, wv_ref, bv_ref, row)
        da = da_ref[0]
        gel, t = _gelu(g)
        dg = da * v * _dgelu(g, t)
        dv = da * gel
        dug, dwg, dbg = _conv_bwd(dg, ug, wg_ref, row)
        duv, dwv, dbv = _conv_bwd(dv, uv, wv_ref, row)
        dug_ref[0] = dug.astype(BF16)
        duv_ref[0] = duv.astype(BF16)

        @pl.when(pl.program_id(1) == 0)
        def _():
            dwg_ref[...] = jnp.zeros_like(dwg_ref)
            dwv_ref[...] = jnp.zeros_like(dwv_ref)
            dbg_ref[...] = jnp.zeros_like(dbg_ref)
            dbv_ref[...] = jnp.zeros_like(dbv_ref)

        dwg_ref[...] += dwg
        dwv_ref[...] += dwv
        dbg_ref[...] += dbg
        dbv_ref[...] += dbv

    return pl.pallas_call(
        body, grid=(nf, Bl),
        in_specs=[_cspec(T), _cspec(T), _cspec(T), _pspec(3), _pspec(3, nf), _pspec(1), _pspec(1, nf)],
        out_specs=(_cspec(T), _cspec(T), _pspec(3), _pspec(3), _pspec(1), _pspec(1)),
        out_shape=(S((Bl, T, F), BF16), S((Bl, T, F), BF16), S((3, F), F32), S((3, F), F32), S((1, F), F32),
                   S((1, F), F32)),
        compiler_params=_cparams("parallel", "arbitrary"), name=name)(ug3, uv3, dact3, wc, wc, bc, bc)


def _lru_gates(xc, rp, ip, br_ref, bi_ref, lam_ref):
    r = jax.nn.sigmoid(rp + br_ref[...])
    i = jax.nn.sigmoid(ip + bi_ref[...])
    lam = lam_ref[...]
    sp = jnp.maximum(-lam, 0.0) + jnp.log1p(jnp.exp(-jnp.abs(lam)))
    log_a = (-LRU_C) * r * sp
    a = jnp.exp(log_a)
    z = 2.0 * log_a
    one_m_a2 = jnp.where(z > -0.05, -z * (1.0 + z * (0.5 + z * (1.0 / 6.0 + z * (1.0 / 24.0)))), 1.0 - a * a)
    mult = jnp.sqrt(one_m_a2)
    return r, i, sp, a, mult


def _rglru_fwd(xc3, gates3, proj3, br, bi, lam, name):
    Bl, T, C = xc3.shape
    nsteps = int(math.log2(T))
    assert 1 << nsteps == T

    def body(xc_ref, rp_ref, ip_ref, ug_ref, br_ref, bi_ref, lam_ref, y_ref, h_ref):
        row = lax.broadcasted_iota(jnp.int32, (T, CT), 0)
        xc = xc_ref[0]
        r, i, sp, a, mult = _lru_gates(xc, rp_ref[0], ip_ref[0], br_ref, bi_ref, lam_ref)
        b = mult * (i * xc)
        for st in range(nsteps):
            s = 1 << st
            a_sh = jnp.where(row >= s, pltpu.roll(a, s, axis=0), 1.0)
            b = a * _shift_down(b, s, row) + b
            a = a * a_sh
        h_ref[0] = b
        y_ref[0] = (b * _gelu(ug_ref[0])[0]).astype(BF16)

    return pl.pallas_call(
        body, grid=(C // CT, Bl),
        in_specs=[_cspec(T), _cspec(T), _cspec(T, C // CT), _cspec(T), _pspec(1), _pspec(1), _pspec(1)],
        out_specs=(_cspec(T), _cspec(T)), out_shape=(S((Bl, T, C), BF16), S((Bl, T, C), F32)),
        compiler_params=_cparams("parallel", "arbitrary"), name=name)(xc3, gates3, gates3, proj3, br, bi, lam)


def _rglru_bwd(dy3, xc3, gates3, proj3, h3, br, bi, lam, name):
    Bl, T, C = xc3.shape
    nsteps = int(math.log2(T))

    def body(dy_ref, xc_ref, rp_ref, ip_ref, ug_ref, h_ref, br_ref, bi_ref, lam_ref,
             dxc_ref, drp_ref, dip_ref, dug_ref, dbr_ref, dbi_ref, dlam_ref):
        row = lax.broadcasted_iota(jnp.int32, (T, CT), 0)
        xc = xc_ref[0]
        r, i, sp, a, mult = _lru_gates(xc, rp_ref[0], ip_ref[0], br_ref, bi_ref, lam_ref)
        h = h_ref[0]
        dy = dy_ref[0]
        ug = ug_ref[0]
        gel, t = _gelu(ug)
        dug_ref[0] = (dy * h * _dgelu(ug, t)).astype(BF16)
        gacc = dy * gel
        an = _shift_up(a, 1, row)
        for st in range(nsteps):
            s = 1 << st
            an_sh = jnp.where(row < T - s, pltpu.roll(an, T - s, axis=0), 1.0)
            gacc = an * _shift_up(gacc, s, row) + gacc
            an = an * an_sh
        da = gacc * _shift_down(h, 1, row)
        ix = i * xc
        d_mult = gacc * ix
        d_i = gacc * mult * xc
        dxc_ref[0] = gacc * mult * i
        d_log_a = da * a - d_mult * (a * a) / mult
        d_r = d_log_a * ((-LRU_C) * sp)
        d_sp = jnp.sum(d_log_a * ((-LRU_C) * r), axis=0, keepdims=True)
        drp = d_r * r * (1.0 - r)
        dip = d_i * i * (1.0 - i)
        drp_ref[0] = drp.astype(BF16)
        dip_ref[0] = dip.astype(BF16)

        @pl.when(pl.program_id(1) == 0)
        def _():
            dbr_ref[...] = jnp.zeros_like(dbr_ref)
            dbi_ref[...] = jnp.zeros_like(dbi_ref)
            dlam_ref[...] = jnp.zeros_like(dlam_ref)

        dbr_ref[...] += jnp.sum(drp, axis=0, keepdims=True)
        dbi_ref[...] += jnp.sum(dip, axis=0, keepdims=True)
        dlam_ref[...] += d_sp * (-jax.nn.sigmoid(-lam_ref[...]))

    vec = S((1, C), F32)
    act = S((Bl, T, C), BF16)
    return pl.pallas_call(
        body, grid=(C // CT, Bl),
        in_specs=[_cspec(T), _cspec(T), _cspec(T), _cspec(T, C // CT), _cspec(T), _cspec(T)] + [_pspec(1)] * 3,
        out_specs=(_cspec(T), _cspec(T), _cspec(T), _cspec(T), _pspec(1), _pspec(1), _pspec(1)),
        out_shape=(S((Bl, T, C), F32), act, act, act, vec, vec, vec),
        compiler_params=_cparams("parallel", "arbitrary"), name=name)(dy3, xc3, gates3, gates3, proj3, h3, br, bi, lam)


NT = (((1,), (1,)), ((), ()))
TN = (((0,), (0,)), ((), ()))


def _hs(h):
    return slice(h * HEAD, (h + 1) * HEAD)


def _mem_softmax(qb, kb):
    s = lax.dot_general(qb, kb, NT, preferred_element_type=F32) * (HEAD ** -0.5)
    e = jnp.exp(s - jnp.max(s, axis=-1, keepdims=True))
    return e / jnp.sum(e, axis=-1, keepdims=True)


def _mem_attn_fwd(proj3, q_off, mkv3, name):
    Bl, T, _ = proj3.shape
    M = mkv3.shape[1]
    tq = _tile(T, (512, 256, 128))

    def body(q_ref, k_ref, v_ref, o_ref):
        q = q_ref[0].astype(BF16)
        k = k_ref[0].astype(BF16)
        v = v_ref[0].astype(BF16)
        outs = []
        for h in range(MEM_HEADS):
            p = _mem_softmax(q[:, _hs(h)], k[:, _hs(h)])
            outs.append(jnp.dot(p.astype(BF16), v[:, _hs(h)], preferred_element_type=F32))
        o_ref[0] = jnp.concatenate(outs, axis=-1).astype(BF16)

    return pl.pallas_call(
        body, grid=(Bl, T // tq),
        in_specs=[pl.BlockSpec((1, tq, MEM_W), lambda b, t: (b, t, q_off // MEM_W)),
                  pl.BlockSpec((1, M, MEM_W), lambda b, t: (b, 0, 0)),
                  pl.BlockSpec((1, M, MEM_W), lambda b, t: (b, 0, 1))],
        out_specs=pl.BlockSpec((1, tq, MEM_W), lambda b, t: (b, t, 0)),
        out_shape=S((Bl, T, MEM_W), BF16), compiler_params=_cparams("parallel", "parallel"), name=name)(
            proj3, mkv3, mkv3)


def _mem_attn_bwd(proj3, q_off, mkv3, do3, name):
    Bl, T, _ = proj3.shape
    M = mkv3.shape[1]
    tq = _tile(T, (512, 256, 128))
    scale = HEAD ** -0.5

    def body(q_ref, k_ref, v_ref, do_ref, dq_ref, dkv_ref):
        q = q_ref[0].astype(BF16)
        k = k_ref[0].astype(BF16)
        v = v_ref[0].astype(BF16)
        do = do_ref[0].astype(BF16)
        dqs, dks, dvs = [], [], []
        for h in range(MEM_HEADS):
            qh, kh, vh, doh = q[:, _hs(h)], k[:, _hs(h)], v[:, _hs(h)], do[:, _hs(h)]
            p = _mem_softmax(qh, kh)
            dvs.append(lax.dot_general(p.astype(BF16), doh, TN, preferred_element_type=F32))
            dp = lax.dot_general(doh, vh, NT, preferred_element_type=F32)
            ds = (p * (dp - jnp.sum(p * dp, axis=-1, keepdims=True)) * scale).astype(BF16)
            dqs.append(jnp.dot(ds, kh, preferred_element_type=F32))
            dks.append(lax.dot_general(ds, qh, TN, preferred_element_type=F32))
        dq_ref[0] = jnp.concatenate(dqs, axis=-1).astype(BF16)

        @pl.when(pl.program_id(1) == 0)
        def _():
            dkv_ref[...] = jnp.zeros_like(dkv_ref)

        dkv_ref[0] += jnp.concatenate(dks + dvs, axis=-1)

    return pl.pallas_call(
        body, grid=(Bl, T // tq),
        in_specs=[pl.BlockSpec((1, tq, MEM_W), lambda b, t: (b, t, q_off // MEM_W)),
                  pl.BlockSpec((1, M, MEM_W), lambda b, t: (b, 0, 0)),
                  pl.BlockSpec((1, M, MEM_W), lambda b, t: (b, 0, 1)),
                  pl.BlockSpec((1, tq, MEM_W), lambda b, t: (b, t, 0))],
        out_specs=(pl.BlockSpec((1, tq, MEM_W), lambda b, t: (b, t, 0)),
                   pl.BlockSpec((1, M, 2 * MEM_W), lambda b, t: (b, 0, 0))),
        out_shape=(S((Bl, T, MEM_W), BF16), S((Bl, M, 2 * MEM_W), F32)),
        compiler_params=_cparams("parallel", "arbitrary"), name=name)(proj3, mkv3, mkv3, do3)


def _swa_probs(qh, kph, kch, sink, slope, has_prev):
    qi = lax.broadcasted_iota(jnp.int32, (WIN, WIN), 0)
    kj = lax.broadcasted_iota(jnp.int32, (WIN, WIN), 1)
    scale = HEAD ** -0.5
    sp = lax.dot_general(qh, kph, NT, preferred_element_type=F32) * scale
    sc = lax.dot_general(qh, kch, NT, preferred_element_type=F32) * scale
    dist_p = (qi + WIN - kj).astype(F32)
    dist_c = (qi - kj).astype(F32)
    neg = -jnp.inf
    sp = jnp.where(kj > qi + jnp.where(has_prev, 0, WIN), sp - slope * dist_p, neg)
    sc = jnp.where(kj <= qi, sc - slope * dist_c, neg)
    m = jnp.maximum(jnp.maximum(jnp.max(sp, axis=-1, keepdims=True), jnp.max(sc, axis=-1, keepdims=True)), sink)
    ep = jnp.exp(sp - m)
    ec = jnp.exp(sc - m)
    es = jnp.exp(sink - m)
    inv = 1.0 / (jnp.sum(ep, axis=-1, keepdims=True) + jnp.sum(ec, axis=-1, keepdims=True) + es)
    return ep * inv, ec * inv, es * inv


def _swa_specs(nb):
    prev = lambda n: jnp.maximum(n - 1, 0)
    q = pl.BlockSpec((1, WIN, MIX_W), lambda b, n: (b, n, 0))
    kp = pl.BlockSpec((1, WIN, MEM_W), lambda b, n: (b, prev(n), 0))
    kc = pl.BlockSpec((1, WIN, MEM_W), lambda b, n: (b, n, 0))
    vp = pl.BlockSpec((1, WIN, MEM_W), lambda b, n: (b, prev(n), 1))
    vc = pl.BlockSpec((1, WIN, MEM_W), lambda b, n: (b, n, 1))
    sm = pl.BlockSpec(memory_space=pltpu.SMEM)
    return q, kp, kc, vp, vc, sm


def _swa_fwd(proj3, kv3, sinks, name):
    Bl, T, _ = proj3.shape
    nb = T // WIN
    q_s, kp_s, kc_s, vp_s, vc_s, sm = _swa_specs(nb)

    def body(q_ref, kp_ref, kc_ref, vp_ref, vc_ref, sink_ref, o_ref):
        has_prev = pl.program_id(1) > 0
        q = q_ref[0].astype(BF16)
        kp, kc = kp_ref[0].astype(BF16), kc_ref[0].astype(BF16)
        vp, vc = vp_ref[0].astype(BF16), vc_ref[0].astype(BF16)
        outs = []
        for h in range(SWA_HEADS):
            kvs = _hs(h // SWA_GROUP)
            pp, pc, _ = _swa_probs(q[:, _hs(h)], kp[:, kvs], kc[:, kvs], sink_ref[h], SLOPES[h], has_prev)
            outs.append(jnp.dot(pp.astype(BF16), vp[:, kvs], preferred_element_type=F32)
                        + jnp.dot(pc.astype(BF16), vc[:, kvs], preferred_element_type=F32))
        o_ref[0] = jnp.concatenate(outs, axis=-1).astype(BF16)

    return pl.pallas_call(
        body, grid=(Bl, nb), in_specs=[q_s, kp_s, kc_s, vp_s, vc_s, sm], out_specs=q_s,
        out_shape=S((Bl, T, MIX_W), BF16), compiler_params=_cparams("parallel", "parallel"), name=name)(
            proj3, kv3, kv3, kv3, kv3, sinks)


def _swa_bwd(proj3, kv3, sinks, do3, name):
    Bl, T, _ = proj3.shape
    nb = T // WIN
    q_s, kp_s, kc_s, vp_s, vc_s, sm = _swa_specs(nb)
    kv_s = pl.BlockSpec((1, WIN, 2 * MEM_W), lambda b, n: (b, n, 0))
    sk_s = pl.BlockSpec((8, LANES), lambda b, n: (0, 0))
    scale = HEAD ** -0.5

    def body(q_ref, kp_ref, kc_ref, vp_ref, vc_ref, sink_ref, do_ref, dq_ref, dkc_ref, dkp_ref, dsk_ref):
        has_prev = pl.program_id(1) > 0
        q = q_ref[0].astype(BF16)
        kp, kc = kp_ref[0].astype(BF16), kc_ref[0].astype(BF16)
        vp, vc = vp_ref[0].astype(BF16), vc_ref[0].astype(BF16)
        do = do_ref[0].astype(BF16)
        lane = lax.broadcasted_iota(jnp.int32, (8, LANES), 1)
        srow = lax.broadcasted_iota(jnp.int32, (8, LANES), 0)
        dsk = jnp.zeros((8, LANES), F32)
        dqs = []
        nkv = SWA_HEADS // SWA_GROUP
        dkc, dkp, dvc, dvp = [None] * nkv, [None] * nkv, [None] * nkv, [None] * nkv
        for h in range(SWA_HEADS):
            kvh = h // SWA_GROUP
            kvs = _hs(kvh)
            qh, doh = q[:, _hs(h)], do[:, _hs(h)]
            pp, pc, ps = _swa_probs(qh, kp[:, kvs], kc[:, kvs], sink_ref[h], SLOPES[h], has_prev)
            dpp = lax.dot_general(doh, vp[:, kvs], NT, preferred_element_type=F32)
            dpc = lax.dot_general(doh, vc[:, kvs], NT, preferred_element_type=F32)
            delta = jnp.sum(pp * dpp, axis=-1, keepdims=True) + jnp.sum(pc * dpc, axis=-1, keepdims=True)
            dsp = (pp * (dpp - delta) * scale).astype(BF16)
            dsc = (pc * (dpc - delta) * scale).astype(BF16)
            dqs.append(jnp.dot(dsp, kp[:, kvs], preferred_element_type=F32)
                       + jnp.dot(dsc, kc[:, kvs], preferred_element_type=F32))
            parts = (lax.dot_general(dsc, qh, TN, preferred_element_type=F32),
                     lax.dot_general(dsp, qh, TN, preferred_element_type=F32),
                     lax.dot_general(pc.astype(BF16), doh, TN, preferred_element_type=F32),
                     lax.dot_general(pp.astype(BF16), doh, TN, preferred_element_type=F32))
            for acc, part in zip((dkc, dkp, dvc, dvp), parts):
                acc[kvh] = part if acc[kvh] is None else acc[kvh] + part
            dsk = dsk + jnp.where((lane == h) & (srow == 0), -jnp.sum(ps * delta), 0.0)
        dq_ref[0] = jnp.concatenate(dqs, axis=-1).astype(BF16)
        dkc_ref[0] = jnp.concatenate(dkc + dvc, axis=-1)
        dkp_ref[0] = jnp.concatenate(dkp + dvp, axis=-1)

        @pl.when((pl.program_id(0) == 0) & (pl.program_id(1) == 0))
        def _():
            dsk_ref[...] = jnp.zeros_like(dsk_ref)

        dsk_ref[...] += dsk

    return pl.pallas_call(
        body, grid=(Bl, nb), in_specs=[q_s, kp_s, kc_s, vp_s, vc_s, sm, q_s], out_specs=(q_s, kv_s, kv_s, sk_s),
        out_shape=(S((Bl, T, MIX_W), BF16), S((Bl, T, 2 * MEM_W), F32), S((Bl, T, 2 * MEM_W), F32), S((8, LANES), F32)),
        compiler_params=_cparams("arbitrary", "arbitrary"), name=name)(proj3, kv3, kv3, kv3, kv3, sinks, do3)


def _kv_grad_combine(parts, name):
    Bl, T, W = parts[0][0].shape
    nb = T // WIN
    nl = len(parts)

    def body(*refs):
        o_ref = refs[-1]
        has_next = jnp.where(pl.program_id(1) == nb - 1, 0.0, 1.0)
        acc = None
        for l in range(nl):
            c = refs[2 * l][0] + has_next * refs[2 * l + 1][0]
            acc = c if acc is None else acc + c
        o_ref[0] = acc.astype(BF16)

    cur = pl.BlockSpec((1, WIN, W), lambda b, n: (b, n, 0))
    nxt = pl.BlockSpec((1, WIN, W), lambda b, n: (b, jnp.minimum(n + 1, nb - 1), 0))
    return pl.pallas_call(
        body, grid=(Bl, nb), in_specs=[cur, nxt] * nl, out_specs=cur, out_shape=S((Bl, T, W), BF16),
        compiler_params=_cparams("parallel", "parallel"), name=name)(*[a for pr in parts for a in pr])


def _loss_bwd(y, target, name="loss"):
    N, D = y.shape
    tm = _tile(N, (512, 256, 128))

    def body(y_ref, t_ref, dy_ref, l_ref):
        e = y_ref[...] - t_ref[...]
        dy_ref[...] = e * (1.0 / D)

        @pl.when(pl.program_id(0) == 0)
        def _():
            l_ref[...] = jnp.zeros_like(l_ref)

        l_ref[...] += jnp.sum(e * e, axis=0, keepdims=True) * (0.5 / D)

    row = pl.BlockSpec((tm, D), lambda i: (i, 0))
    vec = pl.BlockSpec((1, D), lambda i: (0, 0))
    return pl.pallas_call(
        body, grid=(N // tm,), in_specs=[row, row], out_specs=(row, vec), out_shape=(S((N, D), F32), S((1, D), F32)),
        compiler_params=_cparams("arbitrary"), name=name)(y, target)


def _all_gather(x, name):
    R, C = x.shape

    def body(x_ref, out_ref, send_sems, recv_sems, local_sem):
        mx, my, mc = lax.axis_index("x"), lax.axis_index("y"), lax.axis_index("c")
        me, sibling = (mx, my, mc), (mx, my, 1 - mc)
        chips = [(1 - mx, my), (mx, 1 - my), (1 - mx, 1 - my)]

        def rows(px, py, pc):
            return out_ref.at[4 * px + 2 * py + pc]

        def copy(kk, block, to, src=None):
            return pltpu.make_async_remote_copy(
                src_ref=rows(*block) if src is None else src, dst_ref=rows(*block), send_sem=send_sems.at[kk],
                recv_sem=recv_sems.at[kk], device_id=to, device_id_type=MESH)

        mine = pltpu.make_async_copy(x_ref, rows(*me), local_sem)
        mine.start()
        first = [copy(0, me, sibling, src=x_ref)]
        first += [copy(1 + j, me, (*chip, mc), src=x_ref) for j, chip in enumerate(chips)]
        for cp in first:
            cp.start()
        passed = [copy(4 + j, (*chip, mc), sibling) for j, chip in enumerate(chips)]
        for j, chip in enumerate(chips):
            copy(1 + j, (*chip, mc), me).wait_recv()
            passed[j].start()
        copy(0, sibling, me).wait_recv()
        for j, chip in enumerate(chips):
            copy(4 + j, (*chip, 1 - mc), me).wait_recv()
        for cp in first + passed:
            cp.wait_send()
        mine.wait()

    return pl.pallas_call(
        body, out_shape=S((N_DEV, R, C), x.dtype), in_specs=[ANY], out_specs=ANY,
        scratch_shapes=[pltpu.SemaphoreType.DMA((7,)), pltpu.SemaphoreType.DMA((7,)), pltpu.SemaphoreType.DMA(())],
        name=name)(x)


def _ag_weights(shards, row_sharded, name):
    n = len(shards)

    def full_shape(a, rows):
        if rows:
            return a.shape[:-2] + (N_DEV * a.shape[-2],) + a.shape[-1:]
        return (N_DEV,) + a.shape

    def body(*refs):
        x_refs, o_refs = refs[:n], refs[n:2 * n]
        send_sems, recv_sems, local_sems = refs[2 * n:]
        mx, my, mc = lax.axis_index("x"), lax.axis_index("y"), lax.axis_index("c")
        me, sibling = (mx, my, mc), (mx, my, 1 - mc)
        chips = [(1 - mx, my), (mx, 1 - my), (1 - mx, 1 - my)]

        def dst(t, px, py, pc):
            d = 4 * px + 2 * py + pc
            if not row_sharded[t]:
                return o_refs[t].at[d]
            r = shards[t].shape[-2]
            idx = (slice(None),) * (shards[t].ndim - 2) + (pl.ds(pl.multiple_of(d * r, 16), r), slice(None))
            return o_refs[t].at[idx]

        def copy(kk, t, block, to, src=None):
            return pltpu.make_async_remote_copy(
                src_ref=dst(t, *block) if src is None else src, dst_ref=dst(t, *block),
                send_sem=send_sems.at[kk * n + t], recv_sem=recv_sems.at[kk * n + t], device_id=to,
                device_id_type=MESH)

        mine = [pltpu.make_async_copy(x_refs[t], dst(t, *me), local_sems.at[t]) for t in range(n)]
        for cp in mine:
            cp.start()
        first = []
        for t in range(n):
            first.append(copy(0, t, me, sibling, src=x_refs[t]))
            first += [copy(1 + j, t, me, (*chip, mc), src=x_refs[t]) for j, chip in enumerate(chips)]
        for cp in first:
            cp.start()
        passed = []
        for j, chip in enumerate(chips):
            for t in range(n):
                copy(1 + j, t, (*chip, mc), me).wait_recv()
                cp = copy(4 + j, t, (*chip, mc), sibling)
                cp.start()
                passed.append(cp)
        for t in range(n):
            copy(0, t, sibling, me).wait_recv()
            for j, chip in enumerate(chips):
                copy(4 + j, t, (*chip, 1 - mc), me).wait_recv()
        for cp in first + passed:
            cp.wait_send()
        for cp in mine:
            cp.wait()

    return pl.pallas_call(
        body, out_shape=tuple(S(full_shape(a, r), a.dtype) for a, r in zip(shards, row_sharded)),
        in_specs=[ANY] * n, out_specs=tuple([ANY] * n),
        scratch_shapes=[pltpu.SemaphoreType.DMA((7 * n,)), pltpu.SemaphoreType.DMA((7 * n,)),
                        pltpu.SemaphoreType.DMA((n,))],
        name=name)(*shards)


def _rs_sibling(gs, name):
    n = len(gs)

    def body(*refs):
        g_refs, o_refs = refs[:n], refs[n:2 * n]
        send_sems, recv_sems = refs[2 * n:]
        mx, my, mc = lax.axis_index("x"), lax.axis_index("y"), lax.axis_index("c")
        copies = [pltpu.make_async_remote_copy(
            src_ref=g_refs[t].at[:, 2 * j + (1 - mc)], dst_ref=o_refs[t].at[j], send_sem=send_sems.at[j * n + t],
            recv_sem=recv_sems.at[j * n + t], device_id=(mx, my, 1 - mc), device_id_type=MESH)
            for t in range(n) for j in range(4)]
        for cp in copies:
            cp.start()
        for cp in copies:
            cp.wait_recv()
        for cp in copies:
            cp.wait_send()

    return pl.pallas_call(
        body, out_shape=tuple(S((4, g.shape[0]) + g.shape[2:], g.dtype) for g in gs), in_specs=[ANY] * n,
        out_specs=tuple([ANY] * n),
        scratch_shapes=[pltpu.SemaphoreType.DMA((4 * n,)), pltpu.SemaphoreType.DMA((4 * n,))], name=name)(*gs)


def _rs_chips(ps, name):
    n = len(ps)

    def body(*refs):
        p_refs, o_refs = refs[:n], refs[n:2 * n]
        send_sems, recv_sems = refs[2 * n:]
        mx, my, mc = lax.axis_index("x"), lax.axis_index("y"), lax.axis_index("c")
        chips = [(1 - mx, my), (mx, 1 - my), (1 - mx, 1 - my)]
        copies = [pltpu.make_async_remote_copy(
            src_ref=p_refs[t].at[2 * cx + cy], dst_ref=o_refs[t].at[j], send_sem=send_sems.at[j * n + t],
            recv_sem=recv_sems.at[j * n + t], device_id=(cx, cy, mc), device_id_type=MESH)
            for t in range(n) for j, (cx, cy) in enumerate(chips)]
        for cp in copies:
            cp.start()
        for cp in copies:
            cp.wait_recv()
        for cp in copies:
            cp.wait_send()

    return pl.pallas_call(
        body, out_shape=tuple(S((3,) + p.shape[1:], p.dtype) for p in ps), in_specs=[ANY] * n,
        out_specs=tuple([ANY] * n),
        scratch_shapes=[pltpu.SemaphoreType.DMA((3 * n,)), pltpu.SemaphoreType.DMA((3 * n,))], name=name)(*ps)


def _rows_tile(b):
    return _tile(b, (512, 256, 128)) if b > 512 else b


def _pair_sum(g, got, name):
    A, _, B, C = g.shape
    tb = _rows_tile(B)
    core = lax.axis_index("c").astype(jnp.int32).reshape(1)

    def body(c_ref, g_ref, r_ref, o_ref):
        o_ref[...] = (g_ref[...].astype(F32) + r_ref[...].astype(F32)).astype(o_ref.dtype)

    return pl.pallas_call(
        body,
        grid_spec=pltpu.PrefetchScalarGridSpec(
            num_scalar_prefetch=1, grid=(4, A, B // tb),
            in_specs=[pl.BlockSpec((1, 1, tb, C), lambda j, a, i, c_ref: (a, 2 * j + c_ref[0], i, 0)),
                      pl.BlockSpec((1, 1, tb, C), lambda j, a, i, c_ref: (j, a, i, 0))],
            out_specs=pl.BlockSpec((1, 1, tb, C), lambda j, a, i, c_ref: (j, a, i, 0))),
        out_shape=S((4, A, B, C), g.dtype), compiler_params=_cparams("parallel", "parallel", "parallel"),
        name=name)(core, g, got)


def _cols_to_natural(g8, name):
    _, L, K, c = g8.shape
    tk = _tile(K, (256, 128))

    def body(x_ref, o_ref):
        o_ref[...] = jnp.concatenate([x_ref[d] for d in range(N_DEV)], axis=-1)

    return pl.pallas_call(
        body, grid=(L, K // tk), in_specs=[pl.BlockSpec((N_DEV, None, tk, c), lambda l, i: (0, l, i, 0))],
        out_specs=pl.BlockSpec((None, tk, N_DEV * c), lambda l, i: (l, i, 0)),
        out_shape=S((L, K, N_DEV * c), g8.dtype), compiler_params=_cparams("parallel", "parallel"), name=name)(g8)


def _natural_to_cols(g, name):
    L, K, c8 = g.shape
    c = c8 // N_DEV
    tk = _tile(K, (256, 128))

    def body(x_ref, o_ref):
        xv = x_ref[...]
        for d in range(N_DEV):
            o_ref[d] = xv[:, d * c:(d + 1) * c]

    return pl.pallas_call(
        body, grid=(L, K // tk), in_specs=[pl.BlockSpec((None, tk, c8), lambda l, i: (l, i, 0))],
        out_specs=pl.BlockSpec((N_DEV, None, tk, c), lambda l, i: (0, l, i, 0)),
        out_shape=S((N_DEV, L, K, c), g.dtype), compiler_params=_cparams("parallel", "parallel"), name=name)(g)


def _adamw_math(w, g, m, v):
    m = ADAM_B1 * m + (1.0 - ADAM_B1) * g
    v = ADAM_B2 * v + (1.0 - ADAM_B2) * (g * g)
    m_hat = m / (1.0 - ADAM_B1 ** ADAM_STEP)
    v_hat = v / (1.0 - ADAM_B2 ** ADAM_STEP)
    delta = -ADAM_LR * (m_hat / (jnp.sqrt(v_hat) + ADAM_EPS) + ADAM_WD * w)
    return delta, m, v


def _adamw_sharded(p, got, w, m, v, name):
    A, B, C = w.shape
    tb = _rows_tile(B)
    chip = (2 * lax.axis_index("x") + lax.axis_index("y")).astype(jnp.int32).reshape(1)

    def body(c_ref, p_ref, got_ref, w_ref, m_ref, v_ref, g_out, d_out, m_out, v_out):
        g = p_ref[0].astype(F32)
        for j in range(3):
            g = g + got_ref[j].astype(F32)
        d, mn, vn = _adamw_math(w_ref[...], g, m_ref[...], v_ref[...])
        g_out[...] = g
        d_out[...] = d
        m_out[...] = mn
        v_out[...] = vn

    blk = pl.BlockSpec((1, tb, C), lambda a, i, c_ref: (a, i, 0))
    return pl.pallas_call(
        body,
        grid_spec=pltpu.PrefetchScalarGridSpec(
            num_scalar_prefetch=1, grid=(A, B // tb),
            in_specs=[pl.BlockSpec((1, 1, tb, C), lambda a, i, c_ref: (c_ref[0], a, i, 0)),
                      pl.BlockSpec((3, 1, tb, C), lambda a, i, c_ref: (0, a, i, 0)), blk, blk, blk],
            out_specs=(blk, blk, blk, blk)),
        out_shape=(S((A, B, C), F32),) * 4, compiler_params=_cparams("parallel", "parallel"), name=name)(
            chip, p, got, w, m, v)


def _adamw_replicated(parts, w, m, v, name):
    R, C = w.shape
    rb = _tile(R, (512, 256, 128, 64, 32, 16, 8))

    def body(p_ref, w_ref, m_ref, v_ref, g_out, d_out, m_out, v_out):
        g = p_ref[0]
        for j in range(1, N_DEV):
            g = g + p_ref[j]
        d, mn, vn = _adamw_math(w_ref[...], g, m_ref[...], v_ref[...])
        g_out[...] = g
        d_out[...] = d
        m_out[...] = mn
        v_out[...] = vn

    blk = pl.BlockSpec((rb, C), lambda i: (i, 0))
    return pl.pallas_call(
        body, grid=(R // rb,), in_specs=[pl.BlockSpec((N_DEV, rb, C), lambda i: (0, i, 0)), blk, blk, blk],
        out_specs=(blk, blk, blk, blk), out_shape=(S((R, C), F32),) * 4, compiler_params=_cparams("parallel"),
        name=name)(parts, w, m, v)


def _pack(arrs, rows_mult, dtype):
    flat = jnp.concatenate([a.reshape(-1).astype(dtype) for a in arrs])
    n = flat.shape[0]
    per = rows_mult * LANES
    tot = -(-n // per) * per
    return jnp.pad(flat, (0, tot - n)).reshape(tot // LANES, LANES)


def _unpack(blob, shapes):
    flat = blob.reshape(-1)
    out, off = [], 0
    for shp in shapes:
        n = int(np.prod(shp))
        out.append(flat[off:off + n].reshape(shp))
        off += n
    return out


def _small_to_natural(g8):
    t = jnp.moveaxis(g8, 0, -2)
    return t.reshape(t.shape[:-2] + (N_DEV * t.shape[-1],))


def _small_to_cols(g):
    t = g.reshape(g.shape[:-1] + (N_DEV, g.shape[-1] // N_DEV))
    return jnp.moveaxis(t, -2, 0)


def _block_diag(w):
    nb, bs, _ = w.shape
    eye = jnp.eye(nb, dtype=w.dtype)
    return (eye[:, None, :, None] * w[:, :, None, :]).reshape(nb * bs, nb * bs)


def _diag_blocks(d, nb, bs):
    d4 = d.reshape(nb, bs, nb, bs)
    return jnp.stack([d4[i, :, i, :] for i in range(nb)])


def kernel(x, mem, g_mix_pre, g_mix_post, g_ffn_pre, g_ffn_post, g_mem, w_mem_kv, w_mix_out, w_ffn_up, w_ffn_conv, b_ffn_conv, w_ffn_down, w_in_a, w_conv_a, b_conv_a, w_rg_r, b_rg_r, w_rg_i, b_rg_i, lru_lambda, w_in_b, sinks_b, g_kv, w_kv, loss_target, m_g_mix_pre, m_g_mix_post, m_g_ffn_pre, m_g_ffn_post, m_g_mem, m_w_mem_kv, m_w_mix_out, m_w_ffn_up, m_w_ffn_conv, m_b_ffn_conv, m_w_ffn_down, m_w_in_a, m_w_conv_a, m_b_conv_a, m_w_rg_r, m_b_rg_r, m_w_rg_i, m_b_rg_i, m_lru_lambda, m_w_in_b, m_sinks_b, m_g_kv, m_w_kv, v_g_mix_pre, v_g_mix_post, v_g_ffn_pre, v_g_ffn_post, v_g_mem, v_w_mem_kv, v_w_mix_out, v_w_ffn_up, v_w_ffn_conv, v_b_ffn_conv, v_w_ffn_down, v_w_in_a, v_w_conv_a, v_b_conv_a, v_w_rg_r, v_b_rg_r, v_w_rg_i, v_b_rg_i, v_lru_lambda, v_w_in_b, v_sinks_b, v_g_kv, v_w_kv):
    w_loc = dict(g_mix_pre=g_mix_pre, g_mix_post=g_mix_post, g_ffn_pre=g_ffn_pre, g_ffn_post=g_ffn_post, g_mem=g_mem,
                 w_mem_kv=w_mem_kv, w_mix_out=w_mix_out, w_ffn_up=w_ffn_up, w_ffn_conv=w_ffn_conv,
                 b_ffn_conv=b_ffn_conv, w_ffn_down=w_ffn_down, w_in_a=w_in_a, w_conv_a=w_conv_a, b_conv_a=b_conv_a,
                 w_rg_r=w_rg_r, b_rg_r=b_rg_r, w_rg_i=w_rg_i, b_rg_i=b_rg_i, lru_lambda=lru_lambda, w_in_b=w_in_b,
                 sinks_b=sinks_b, g_kv=g_kv, w_kv=w_kv)
    m_loc = dict(g_mix_pre=m_g_mix_pre, g_mix_post=m_g_mix_post, g_ffn_pre=m_g_ffn_pre, g_ffn_post=m_g_ffn_post,
                 g_mem=m_g_mem, w_mem_kv=m_w_mem_kv, w_mix_out=m_w_mix_out, w_ffn_up=m_w_ffn_up,
                 w_ffn_conv=m_w_ffn_conv, b_ffn_conv=m_b_ffn_conv, w_ffn_down=m_w_ffn_down, w_in_a=m_w_in_a,
                 w_conv_a=m_w_conv_a, b_conv_a=m_b_conv_a, w_rg_r=m_w_rg_r, b_rg_r=m_b_rg_r, w_rg_i=m_w_rg_i,
                 b_rg_i=m_b_rg_i, lru_lambda=m_lru_lambda, w_in_b=m_w_in_b, sinks_b=m_sinks_b, g_kv=m_g_kv,
                 w_kv=m_w_kv)
    v_loc = dict(g_mix_pre=v_g_mix_pre, g_mix_post=v_g_mix_post, g_ffn_pre=v_g_ffn_pre, g_ffn_post=v_g_ffn_post,
                 g_mem=v_g_mem, w_mem_kv=v_w_mem_kv, w_mix_out=v_w_mix_out, w_ffn_up=v_w_ffn_up,
                 w_ffn_conv=v_w_ffn_conv, b_ffn_conv=v_b_ffn_conv, w_ffn_down=v_w_ffn_down, w_in_a=v_w_in_a,
                 w_conv_a=v_w_conv_a, b_conv_a=v_b_conv_a, w_rg_r=v_w_rg_r, b_rg_r=v_b_rg_r, w_rg_i=v_w_rg_i,
                 b_rg_i=v_b_rg_i, lru_lambda=v_lru_lambda, w_in_b=v_w_in_b, sinks_b=v_sinks_b, g_kv=v_g_kv,
                 w_kv=v_w_kv)

    Bl, T, D = x.shape
    Ml = mem.shape[1]
    N = Bl * T
    depth = g_mix_pre.shape[0]
    n_a = w_in_a.shape[0]
    F = w_ffn_down.shape[1] * N_DEV
    sh_names = [n for n, _ in SHARDED]
    row_sharded = {n: ax == w_loc[n].ndim - 2 for n, ax in SHARDED}

    payload = [w_loc[n] if n in SMALL_SHARDED else w_loc[n].astype(BF16) for n in sh_names]
    W = dict(zip(sh_names, _ag_weights(payload, [row_sharded[n] for n in sh_names], name="ag_weights")))
    W["w_ffn_up"] = _cols_to_natural(W["w_ffn_up"], name="unshard_w_ffn_up")
    W["w_in_a"] = _cols_to_natural(W["w_in_a"], name="unshard_w_in_a")
    for n in SMALL_SHARDED:
        W[n] = _small_to_natural(W[n])
    nblk, bsz = w_rg_r.shape[1], w_rg_r.shape[2]
    wbd = [jnp.concatenate([_block_diag(w_rg_r[j]), _block_diag(w_rg_i[j])], axis=1).astype(BF16) for j in range(n_a)]

    def vec(a):
        return a.reshape(1, -1)

    x2 = x.reshape(N, D)
    mem2 = mem.reshape(Bl * Ml, D)
    saved = []
    kvn = kv3 = x_kv = None
    xs = x2
    for l in range(depth):
        sv = {"x0": xs}
        h1 = _rms_fwd(xs, vec(g_mix_pre[l]), BF16, name=f"rms_mixpre_{l}")
        memn = _rms_fwd(mem2, vec(g_mem[l]), BF16, name=f"rms_mem_{l}")
        mkv3 = _mm(memn, W["w_mem_kv"], bl=l, name=f"mm_memkv_{l}").reshape(Bl, Ml, 2 * MEM_W)
        if l < n_a:
            j = l
            proj = _mm(h1, W["w_in_a"], bl=j, name=f"mm_in_{l}")
            proj3 = proj.reshape(Bl, T, -1)
            xc3 = _conv_fwd_call(proj3, MIX_W, MIX_W, W["w_conv_a"][j], vec(W["b_conv_a"][j]), name=f"conv_a_{l}")
            gates3 = _mm(xc3.reshape(N, MIX_W), wbd[j], name=f"mm_gates_{l}").reshape(Bl, T, 2 * MIX_W)
            y_main3, hs3 = _rglru_fwd(xc3, gates3, proj3, vec(b_rg_r[j]), vec(b_rg_i[j]), vec(W["lru_lambda"][j]),
                                      name=f"rglru_fwd_{l}")
            q_off = 2 * MIX_W
            sv.update(xc3=xc3, gates3=gates3, hs3=hs3)
        else:
            j = l - n_a
            if l == n_a:
                x_kv = xs
                kvn = _rms_fwd(xs, vec(g_kv), BF16, name="rms_kv")
                kv3 = _mm(kvn, W["w_kv"], name="mm_kv").reshape(Bl, T, 2 * MEM_W)
            proj = _mm(h1, W["w_in_b"], bl=j, name=f"mm_in_{l}")
            proj3 = proj.reshape(Bl, T, -1)
            y_main3 = _swa_fwd(proj3, kv3, sinks_b[j], name=f"swa_fwd_{l}")
            q_off = MIX_W
        y_mem3 = _mem_attn_fwd(proj3, q_off, mkv3, name=f"memattn_fwd_{l}")
        y_main = y_main3.reshape(N, MIX_W)
        y_mem = y_mem3.reshape(N, MEM_W)
        y = _mm(y_main, W["w_mix_out"], bl=l, n=D, k=MIX_W, name=f"mm_mixout_main_{l}")
        y = _mm(y_mem, W["w_mix_out"], bl=l, n=D, k=MEM_W, b_off=(MIX_W, 0), add=y, name=f"mm_mixout_mem_{l}")
        x1 = _rms_fwd(y, vec(g_mix_post[l]), F32, res=xs, name=f"rms_mixpost_{l}")
        h2 = _rms_fwd(x1, vec(g_ffn_pre[l]), BF16, name=f"rms_ffnpre_{l}")
        ug = _mm(h2, W["w_ffn_up"], bl=l, n=F, name=f"mm_up_g_{l}")
        uv = _mm(h2, W["w_ffn_up"], bl=l, n=F, b_off=(0, F), name=f"mm_up_v_{l}")
        ug3, uv3 = ug.reshape(Bl, T, F), uv.reshape(Bl, T, F)
        act3 = _ffn_mid_fwd(ug3, uv3, W["w_ffn_conv"][l], vec(b_ffn_conv[l]), name=f"ffn_mid_fwd_{l}")
        act = act3.reshape(N, F)
        f = _mm(act, W["w_ffn_down"], bl=l, name=f"mm_down_{l}")
        x_next = _rms_fwd(f, vec(g_ffn_post[l]), F32, res=x1, name=f"rms_ffnpost_{l}")
        sv.update(h1=h1, memn=memn, mkv3=mkv3, proj3=proj3, q_off=q_off, y_main=y_main, y_mem=y_mem, y=y, x1=x1,
                  h2=h2, ug3=ug3, uv3=uv3, act=act, f=f)
        saved.append(sv)
        xs = x_next

    dxs, loss_vec = _loss_bwd(xs, loss_target.reshape(N, D))
    loss = lax.psum(jnp.sum(loss_vec), ("x", "y", "c"))

    G = {n: [None] * w_loc[n].shape[0] for n in REPL + SMALL_SHARDED if n != "g_kv"}
    GW = {}

    def dw(wname, layer, off, a, b_, nm):
        full = W[wname].shape if W[wname].ndim == 3 else (1,) + W[wname].shape
        GW[wname] = _mm(a, b_, ta=True, out_dtype=BF16, into=(GW.get(wname), full, layer, off), name=nm)

    kv_parts = []
    for l in reversed(range(depth)):
        sv = saved[l]
        proj3 = sv["proj3"]
        df, dg = _rms_bwd(sv["f"], vec(g_ffn_post[l]), dxs, out_dtype=BF16, name=f"rmsb_ffnpost_{l}")
        G["g_ffn_post"][l] = dg[0]
        dact = _mm(df, W["w_ffn_down"], bl=l, tb=True, name=f"mmb_down_dx_{l}")
        dw("w_ffn_down", l, (0, 0), sv["act"], df, f"mmb_down_dw_{l}")
        dug3, duv3, dwg, dwv, dbg, dbv = _ffn_mid_bwd(sv["ug3"], sv["uv3"], dact.reshape(Bl, T, F),
                                                      W["w_ffn_conv"][l], vec(b_ffn_conv[l]), name=f"ffn_mid_bwd_{l}")
        G["w_ffn_conv"][l] = jnp.concatenate([dwg, dwv], axis=1)
        G["b_ffn_conv"][l] = jnp.concatenate([dbg, dbv], axis=1)[0]
        dug, duv = dug3.reshape(N, F), duv3.reshape(N, F)
        dh2 = _mm(dug, W["w_ffn_up"], bl=l, tb=True, n=D, k=F, name=f"mmb_up_dx_g_{l}")
        dh2 = _mm(duv, W["w_ffn_up"], bl=l, tb=True, n=D, k=F, b_off=(0, F), add=dh2, name=f"mmb_up_dx_v_{l}")
        dw("w_ffn_up", l, (0, 0), sv["h2"], dug, f"mmb_up_dw_g_{l}")
        dw("w_ffn_up", l, (0, F), sv["h2"], duv, f"mmb_up_dw_v_{l}")
        dx1, dg = _rms_bwd(sv["x1"], vec(g_ffn_pre[l]), dh2, add=dxs, name=f"rmsb_ffnpre_{l}")
        G["g_ffn_pre"][l] = dg[0]
        dy, dg = _rms_bwd(sv["y"], vec(g_mix_post[l]), dx1, out_dtype=BF16, name=f"rmsb_mixpost_{l}")
        G["g_mix_post"][l] = dg[0]
        dy_main = _mm(dy, W["w_mix_out"], bl=l, tb=True, n=MIX_W, k=D, name=f"mmb_mixout_dmain_{l}")
        dy_mem = _mm(dy, W["w_mix_out"], bl=l, tb=True, n=MEM_W, k=D, b_off=(MIX_W, 0),
                     name=f"mmb_mixout_dmem_{l}")
        dw("w_mix_out", l, (0, 0), sv["y_main"], dy, f"mmb_mixout_dw_main_{l}")
        dw("w_mix_out", l, (MIX_W, 0), sv["y_mem"], dy, f"mmb_mixout_dw_mem_{l}")
        dq_mem3, dmkv3 = _mem_attn_bwd(proj3, sv["q_off"], sv["mkv3"], dy_mem.reshape(Bl, T, MEM_W),
                                       name=f"memattn_bwd_{l}")
        dq_mem = dq_mem3.reshape(N, MEM_W)
        dmkv = dmkv3.reshape(Bl * Ml, 2 * MEM_W)
        dw("w_mem_kv", l, (0, 0), sv["memn"], dmkv, f"mmb_memkv_dw_{l}")
        dmemn = _mm(dmkv, W["w_mem_kv"], bl=l, tb=True, name=f"mmb_memkv_dx_{l}")
        _, dg = _rms_bwd(mem2, vec(g_mem[l]), dmemn, name=f"rmsb_mem_{l}")
        G["g_mem"][l] = dg[0]
        dy_main3 = dy_main.reshape(Bl, T, MIX_W)
        if l < n_a:
            j = l
            dxc3, drp3, dip3, dugate3, dbr, dbi, dlam = _rglru_bwd(
                dy_main3, sv["xc3"], sv["gates3"], proj3, sv["hs3"], vec(b_rg_r[j]), vec(b_rg_i[j]),
                vec(W["lru_lambda"][j]), name=f"rglru_bwd_{l}")
            G["b_rg_r"][j] = dbr.reshape(nblk, bsz)
            G["b_rg_i"][j] = dbi.reshape(nblk, bsz)
            G["lru_lambda"][j] = dlam[0]
            drp, dip = drp3.reshape(N, MIX_W), dip3.reshape(N, MIX_W)
            xc2 = sv["xc3"].reshape(N, MIX_W)
            G["w_rg_r"][j] = _diag_blocks(_mm(xc2, drp, ta=True, name=f"mmb_gates_dw_r_{l}"), nblk, bsz)
            G["w_rg_i"][j] = _diag_blocks(_mm(xc2, dip, ta=True, name=f"mmb_gates_dw_i_{l}"), nblk, bsz)
            dxc = _mm(drp, wbd[j], tb=True, n=MIX_W, k=MIX_W, add=dxc3.reshape(N, MIX_W), name=f"mmb_gates_dx_r_{l}")
            dxc = _mm(dip, wbd[j], tb=True, n=MIX_W, k=MIX_W, b_off=(0, MIX_W), add=dxc, name=f"mmb_gates_dx_i_{l}")
            dux3, dwc, dbc = _conv_bwd_call(dxc.reshape(Bl, T, MIX_W), proj3, MIX_W, MIX_W, W["w_conv_a"][j],
                                            name=f"conv_a_bwd_{l}")
            G["w_conv_a"][j] = dwc
            G["b_conv_a"][j] = dbc[0]
            pieces = [(dugate3.reshape(N, MIX_W), 0), (dux3.reshape(N, MIX_W), MIX_W), (dq_mem, 2 * MIX_W)]
            gname = "w_in_a"
        else:
            j = l - n_a
            dq3, dkc, dkp, dsk = _swa_bwd(proj3, kv3, sinks_b[j], dy_main3, name=f"swa_bwd_{l}")
            kv_parts.append((dkc, dkp))
            G["sinks_b"][j] = dsk[0, :SWA_HEADS]
            pieces = [(dq3.reshape(N, MIX_W), 0), (dq_mem, MIX_W)]
            gname = "w_in_b"
        dh1 = None
        for pi, (piece, off) in enumerate(pieces):
            dh1 = _mm(piece, W[gname], bl=j, tb=True, n=D, k=piece.shape[1], b_off=(0, off), add=dh1,
                      name=f"mmb_in_dx_{pi}_{l}")
            dw(gname, j, (0, off), sv["h1"], piece, f"mmb_in_dw_{pi}_{l}")
        dxs, dg = _rms_bwd(sv["x0"], vec(g_mix_pre[l]), dh1, add=dx1, name=f"rmsb_mixpre_{l}")
        G["g_mix_pre"][l] = dg[0]
        if l == n_a:
            dkv = _kv_grad_combine(kv_parts, name="kv_grad_combine").reshape(N, 2 * MEM_W)
            dw("w_kv", 0, (0, 0), kvn, dkv, "mmb_kv_dw")
            dkvn = _mm(dkv, W["w_kv"], tb=True, name="mmb_kv_dx")
            dxs, dg = _rms_bwd(x_kv, vec(g_kv), dkvn, add=dxs, name="rmsb_kv")
            G["g_kv"] = dg[0]
    grad_x = dxs.reshape(Bl, T, D)
    Gf = {n: (jnp.stack(g) if isinstance(g, list) else g) for n, g in G.items()}

    g4 = {}
    for n in sh_names:
        if n in SMALL_SHARDED:
            t = _small_to_cols(Gf[n]).astype(BF16)
            g4[n] = t.reshape(1, N_DEV, -1, t.shape[-1])
        elif row_sharded[n]:
            L, R, C = GW[n].shape
            g4[n] = GW[n].reshape(L, N_DEV, R // N_DEV, C)
        else:
            t = _natural_to_cols(GW[n], name=f"shard_grad_{n}")
            g4[n] = t.reshape(1, N_DEV, -1, t.shape[-1])
    glist = [g4[n] for n in sh_names]
    got = _rs_sibling(glist, name="rs_sibling")
    psum4 = [_pair_sum(g, r, name=f"rs_pair_sum_{n}") for n, g, r in zip(sh_names, glist, got)]
    got2 = _rs_chips(psum4, name="rs_chips")
    r_blob = _pack([Gf[n].astype(F32) for n in REPL], REPL_ROWS, F32)
    r_parts = _all_gather(r_blob, name="ag_repl_grads")

    res = [{} for _ in range(4)]
    for n, p4, g2 in zip(sh_names, psum4, got2):
        shp3 = p4.shape[1:]
        outs = _adamw_sharded(p4, g2, w_loc[n].reshape(shp3), m_loc[n].reshape(shp3), v_loc[n].reshape(shp3),
                              name=f"adamw_{n}")
        for k in range(4):
            res[k][n] = outs[k].reshape(w_loc[n].shape)
    outs_rp = _adamw_replicated(r_parts, _pack([w_loc[n] for n in REPL], REPL_ROWS, F32),
                                _pack([m_loc[n] for n in REPL], REPL_ROWS, F32),
                                _pack([v_loc[n] for n in REPL], REPL_ROWS, F32),
                                name="adamw_replicated")
    rp_shapes = [w_loc[n].shape for n in REPL]
    for k in range(4):
        res[k].update(zip(REPL, _unpack(outs_rp[k], rp_shapes)))
    out = [loss, grad_x]
    for k in range(4):
        out += [res[k][n] for n in WEIGHTS]
    return tuple(out)
```

```python
import functools
import math

import numpy as np
import jax
import jax.numpy as jnp
from jax import lax
from jax.experimental import pallas as pl
from jax.experimental.pallas import tpu as pltpu

F32 = jnp.float32
BF16 = jnp.bfloat16
S = jax.ShapeDtypeStruct
MESH = pl.DeviceIdType.MESH
ANY = pl.BlockSpec(memory_space=pl.ANY)

HEAD = 64
MEM_HEADS = 4
MEM_W = MEM_HEADS * HEAD
SWA_HEADS = 12
SWA_GROUP = 3
MIX_W = SWA_HEADS * HEAD
WIN = 128
LRU_C = 8.0
EPS = 1e-6
ADAM_LR, ADAM_B1, ADAM_B2, ADAM_EPS, ADAM_WD, ADAM_STEP = 0.001, 0.9, 0.999, 1e-08, 0.01, 10
GELU_C0 = math.sqrt(2.0 / math.pi)
GELU_C1 = 0.044715
N_DEV = 8
LANES = 128
CT = 128
VMEM_LIMIT = 48 * 1024 * 1024
MM_VMEM_BUDGET = 36 * 1024 * 1024
REPL_ROWS = 256

SHARDED = (("w_mem_kv", 1), ("w_mix_out", 1), ("w_ffn_up", 2), ("w_ffn_conv", 2), ("w_ffn_down", 1), ("w_in_a", 2),
           ("w_conv_a", 2), ("b_conv_a", 1), ("lru_lambda", 1), ("w_in_b", 1), ("w_kv", 0))
SMALL_SHARDED = ("w_ffn_conv", "w_conv_a", "b_conv_a", "lru_lambda")
REPL = ("g_mix_pre", "g_mix_post", "g_ffn_pre", "g_ffn_post", "g_mem", "b_ffn_conv", "w_rg_r", "b_rg_r", "w_rg_i",
        "b_rg_i", "sinks_b", "g_kv")
WEIGHTS = ("g_mix_pre", "g_mix_post", "g_ffn_pre", "g_ffn_post", "g_mem", "w_mem_kv", "w_mix_out", "w_ffn_up",
           "w_ffn_conv", "b_ffn_conv", "w_ffn_down", "w_in_a", "w_conv_a", "b_conv_a", "w_rg_r", "b_rg_r", "w_rg_i",
           "b_rg_i", "lru_lambda", "w_in_b", "sinks_b", "g_kv", "w_kv")


def _alibi_slopes(n):
    def pow2(m):
        start = 2.0 ** (-8.0 / m)
        return [start ** (i + 1) for i in range(m)]
    c = 2 ** int(math.floor(math.log2(n)))
    s = pow2(c)
    if c != n:
        s = s + pow2(2 * c)[0::2][: n - c]
    return [float(v) for v in np.asarray(s, dtype=np.float32)]


SLOPES = _alibi_slopes(SWA_HEADS)


def _tile(n, cands):
    for c in cands:
        if n % c == 0:
            return c
    return n


def _cparams(*sem):
    return pltpu.CompilerParams(dimension_semantics=sem, vmem_limit_bytes=VMEM_LIMIT)


def _mm_tiles(M, N, K, a_bytes, b_bytes, o_bytes, add_bytes, offsets):
    m_off, n_offs, k_off = offsets
    tms = [c for c in (1024, 512, 256, 128) if M % c == 0 and m_off % c == 0] or [M]
    tns = [c for c in (1408, 1024, 896, 768, 512, 384, 256, 128)
           if N % c == 0 and all(o % c == 0 for o in n_offs)] or [N]
    tks = [c for c in (K, 2048, 1408, 1024, 512, 256, 128) if c <= K and K % c == 0 and k_off % c == 0]
    best = None
    for tk in tks:
        fits = []
        for tm in tms:
            for tn in tns:
                need = 2 * (tm * tk * a_bytes + tk * tn * b_bytes + tm * tn * (o_bytes + add_bytes))
                need += tm * tn * 4 * (2 if tk < K else 1)
                need += (tm * tk * 2 if a_bytes != 2 else 0) + (tk * tn * 2 if b_bytes != 2 else 0)
                if need <= MM_VMEM_BUDGET:
                    fits.append((tm * tn, min(tm, 512), tm, tn))
        if fits:
            _, _, tm, tn = max(fits)
            best = (tm, tn, tk)
            break
    assert best is not None, (M, N, K)
    return best


def _mm(a, b, *, ta=False, tb=False, n=None, k=None, bl=None, b_off=(0, 0), out_dtype=F32, add=None, into=None,
        name="mm"):
    if ta:
        K, M = a.shape
    else:
        M, K = a.shape
    if tb:
        N = b.shape[-2] if n is None else n
    else:
        N = b.shape[-1] if n is None else n
    assert k is None or k == K
    ro, co = b_off
    n_off, k_off = (ro, co) if tb else (co, ro)
    oro, oco = (0, 0) if into is None else into[3]
    tm, tn, tk = _mm_tiles(M, N, K, a.dtype.itemsize, b.dtype.itemsize, jnp.dtype(out_dtype).itemsize,
                           0 if add is None else add.dtype.itemsize, (oro, (n_off, oco), k_off))
    nk = K // tk
    lead = () if bl is None else (bl,)
    lead_blk = () if bl is None else (None,)
    assert (b.ndim == 3) == (bl is not None)
    if tb:
        b_spec = pl.BlockSpec(lead_blk + (tn, tk), lambda i, j, kk: lead + (j + ro // tn, kk + co // tk))
        b_dims = (1,)
    else:
        b_spec = pl.BlockSpec(lead_blk + (tk, tn), lambda i, j, kk: lead + (kk + ro // tk, j + co // tn))
        b_dims = (0,)
    if ta:
        a_spec = pl.BlockSpec((tk, tm), lambda i, j, kk: (kk, i))
        a_dims = (0,)
    else:
        a_spec = pl.BlockSpec((tm, tk), lambda i, j, kk: (i, kk))
        a_dims = (1,)
    dims = ((a_dims, b_dims), ((), ()))
    add_spec = pl.BlockSpec((tm, tn), lambda i, j, kk: (i, j))
    has_add = add is not None
    if into is None:
        o_spec, o_shape, buf = add_spec, (M, N), None
    else:
        buf, o_shape, ol, _ = into
        assert not has_add
        o_spec = pl.BlockSpec((None, tm, tn), lambda i, j, kk: (ol, i + oro // tm, j + oco // tn))
    has_buf = buf is not None

    def body(*refs):
        refs = list(refs)
        acc_ref = refs.pop() if nk > 1 else None
        o_ref = refs.pop()
        a_ref, b_ref = refs[0], refs[1]
        add_ref = refs[2] if has_add else None
        part = lax.dot_general(a_ref[...].astype(BF16), b_ref[...].astype(BF16), dims, preferred_element_type=F32)

        def finish(r):
            if has_add:
                r = r + add_ref[...].astype(F32)
            o_ref[...] = r.astype(out_dtype)

        if nk == 1:
            finish(part)
        else:
            kk = pl.program_id(2)

            @pl.when(kk == 0)
            def _():
                acc_ref[...] = part

            @pl.when(kk > 0)
            def _():
                acc_ref[...] += part

            @pl.when(kk == nk - 1)
            def _():
                finish(acc_ref[...])

    in_specs = [a_spec, b_spec] + ([add_spec] if has_add else []) + ([ANY] if has_buf else [])
    args = (a, b) + ((add,) if has_add else ()) + ((buf,) if has_buf else ())
    return pl.pallas_call(
        body, grid=(M // tm, N // tn, nk), in_specs=in_specs, out_specs=o_spec,
        out_shape=S(o_shape, out_dtype), scratch_shapes=[pltpu.VMEM((tm, tn), F32)] if nk > 1 else [],
        input_output_aliases={2: 0} if has_buf else {},
        compiler_params=_cparams("parallel", "parallel", "arbitrary"), name=name)(*args)


def _rms_fwd(x, g, out_dtype, res=None, name="rms_fwd"):
    N, D = x.shape
    tm = _tile(N, (512, 256, 128))
    has_res = res is not None

    def body(*refs):
        if has_res:
            x_ref, g_ref, r_ref, o_ref = refs
        else:
            x_ref, g_ref, o_ref = refs
        xv = x_ref[...].astype(F32)
        y = xv * lax.rsqrt(jnp.mean(xv * xv, axis=-1, keepdims=True) + EPS) * g_ref[...]
        if has_res:
            y = y + r_ref[...]
        o_ref[...] = y.astype(out_dtype)

    row = pl.BlockSpec((tm, D), lambda i: (i, 0))
    vec = pl.BlockSpec((1, D), lambda i: (0, 0))
    return pl.pallas_call(
        body, grid=(N // tm,), in_specs=[row, vec] + ([row] if has_res else []), out_specs=row,
        out_shape=S((N, D), out_dtype), compiler_params=_cparams("parallel"), name=name)(
            *((x, g) + ((res,) if has_res else ())))


def _rms_bwd(x, g, dy, add=None, out_dtype=F32, name="rms_bwd"):
    N, D = x.shape
    tm = _tile(N, (512, 256, 128))
    has_add = add is not None

    def body(*refs):
        if has_add:
            x_ref, g_ref, dy_ref, add_ref, dx_ref, dg_ref = refs
        else:
            x_ref, g_ref, dy_ref, dx_ref, dg_ref = refs
        xv = x_ref[...].astype(F32)
        dyv = dy_ref[...].astype(F32)
        r = lax.rsqrt(jnp.mean(xv * xv, axis=-1, keepdims=True) + EPS)
        u = dyv * g_ref[...]
        dx = r * u - xv * (r * r * r * jnp.mean(u * xv, axis=-1, keepdims=True))
        if has_add:
            dx = dx + add_ref[...]
        dx_ref[...] = dx.astype(out_dtype)

        @pl.when(pl.program_id(0) == 0)
        def _():
            dg_ref[...] = jnp.zeros_like(dg_ref)

        dg_ref[...] += jnp.sum(dyv * xv * r, axis=0, keepdims=True)

    row = pl.BlockSpec((tm, D), lambda i: (i, 0))
    vec = pl.BlockSpec((1, D), lambda i: (0, 0))
    return pl.pallas_call(
        body, grid=(N // tm,), in_specs=[row, vec, row] + ([row] if has_add else []), out_specs=(row, vec),
        out_shape=(S((N, D), out_dtype), S((1, D), F32)), compiler_params=_cparams("arbitrary"), name=name)(
            *((x, g, dy) + ((add,) if has_add else ())))


def _shift_down(x, s, row):
    return jnp.where(row >= s, pltpu.roll(x, s, axis=0), 0.0)


def _shift_up(x, s, row):
    T = x.shape[0]
    return jnp.where(row < T - s, pltpu.roll(x, T - s, axis=0), 0.0)


def _conv(x, w_ref, b_ref, row):
    W = w_ref.shape[0]
    y = x * w_ref[W - 1:W, :] + b_ref[...]
    for s in range(1, W):
        y = y + _shift_down(x, s, row) * w_ref[W - 1 - s:W - s, :]
    return y


def _conv_bwd(dy, x, w_ref, row):
    W = w_ref.shape[0]
    dx = dy * w_ref[W - 1:W, :]
    dws = [None] * W
    dws[W - 1] = jnp.sum(dy * x, axis=0, keepdims=True)
    for s in range(1, W):
        dx = dx + _shift_up(dy, s, row) * w_ref[W - 1 - s:W - s, :]
        dws[W - 1 - s] = jnp.sum(dy * _shift_down(x, s, row), axis=0, keepdims=True)
    return dx, jnp.concatenate(dws, axis=0), jnp.sum(dy, axis=0, keepdims=True)


def _gelu(g):
    t = jnp.tanh(GELU_C0 * (g + GELU_C1 * g * g * g))
    return 0.5 * g * (1.0 + t), t


def _dgelu(g, t):
    return 0.5 * (1.0 + t) + 0.5 * g * (1.0 - t * t) * (GELU_C0 * (1.0 + 3.0 * GELU_C1 * g * g))


def _cspec(T, off=0):
    return pl.BlockSpec((1, T, CT), lambda j, b: (b, 0, j + off))


def _pspec(rows, off=0):
    return pl.BlockSpec((rows, CT), lambda j, b: (0, j + off))


def _conv_fwd_call(x3, x_off, C, w, b, name):
    Bl, T, _ = x3.shape
    W = w.shape[0]

    def body(x_ref, w_ref, b_ref, o_ref):
        row = lax.broadcasted_iota(jnp.int32, (T, CT), 0)
        o_ref[0] = _conv(x_ref[0], w_ref, b_ref, row)

    return pl.pallas_call(
        body, grid=(C // CT, Bl), in_specs=[_cspec(T, x_off // CT), _pspec(W), _pspec(1)], out_specs=_cspec(T),
        out_shape=S((Bl, T, C), F32), compiler_params=_cparams("parallel", "arbitrary"), name=name)(x3, w, b)


def _conv_bwd_call(dy3, x3, x_off, C, w, name):
    Bl, T, _ = x3.shape
    W = w.shape[0]

    def body(dy_ref, x_ref, w_ref, dx_ref, dw_ref, db_ref):
        row = lax.broadcasted_iota(jnp.int32, (T, CT), 0)
        dx, dw, db = _conv_bwd(dy_ref[0], x_ref[0], w_ref, row)
        dx_ref[0] = dx.astype(BF16)

        @pl.when(pl.program_id(1) == 0)
        def _():
            dw_ref[...] = jnp.zeros_like(dw_ref)
            db_ref[...] = jnp.zeros_like(db_ref)

        dw_ref[...] += dw
        db_ref[...] += db

    return pl.pallas_call(
        body, grid=(C // CT, Bl), in_specs=[_cspec(T), _cspec(T, x_off // CT), _pspec(W)],
        out_specs=(_cspec(T), _pspec(W), _pspec(1)),
        out_shape=(S((Bl, T, C), BF16), S((W, C), F32), S((1, C), F32)),
        compiler_params=_cparams("parallel", "arbitrary"), name=name)(dy3, x3, w)


def _ffn_mid_fwd(ug3, uv3, wc, bc, name):
    Bl, T, F = ug3.shape
    nf = F // CT

    def body(ug_ref, uv_ref, wg_ref, wv_ref, bg_ref, bv_ref, o_ref):
        row = lax.broadcasted_iota(jnp.int32, (T, CT), 0)
        g = _conv(ug_ref[0], wg_ref, bg_ref, row)
        v = _conv(uv_ref[0], wv_ref, bv_ref, row)
        o_ref[0] = (_gelu(g)[0] * v).astype(BF16)

    return pl.pallas_call(
        body, grid=(nf, Bl),
        in_specs=[_cspec(T), _cspec(T), _pspec(3), _pspec(3, nf), _pspec(1), _pspec(1, nf)], out_specs=_cspec(T),
        out_shape=S((Bl, T, F), BF16), compiler_params=_cparams("parallel", "arbitrary"), name=name)(
            ug3, uv3, wc, wc, bc, bc)


def _ffn_mid_bwd(ug3, uv3, dact3, wc, bc, name):
    Bl, T, F = ug3.shape
    nf = F // CT

    def body(ug_ref, uv_ref, da_ref, wg_ref, wv_ref, bg_ref, bv_ref, dug_ref, duv_ref, dwg_ref, dwv_ref, dbg_ref,
             dbv_ref):
        row = lax.broadcasted_iota(jnp.int32, (T, CT), 0)
        ug = ug_ref[0]
        uv = uv_ref[0]
        g = _conv(ug, wg_ref, bg_ref, row)
        v = _conv(uv, wv_ref, bv_ref, row)
        da = da_ref[0]
        gel, t = _gelu(g)
        dg = da * v * _dgelu(g, t)
        dv = da * gel
        dug, dwg, dbg = _conv_bwd(dg, ug, wg_ref, row)
        duv, dwv, dbv = _conv_bwd(dv, uv, wv_ref, row)
        dug_ref[0] = dug.astype(BF16)
        duv_ref[0] = duv.astype(BF16)

        @pl.when(pl.program_id(1) == 0)
        def _():
            dwg_ref[...] = jnp.zeros_like(dwg_ref)
            dwv_ref[...] = jnp.zeros_like(dwv_ref)
            dbg_ref[...] = jnp.zeros_like(dbg_ref)
            dbv_ref[...] = jnp.zeros_like(dbv_ref)

        dwg_ref[...] += dwg
        dwv_ref[...] += dwv
        dbg_ref[...] += dbg
        dbv_ref[...] += dbv

    return pl.pallas_call(
        body, grid=(nf, Bl),
        in_specs=[_cspec(T), _cspec(T), _cspec(T), _pspec(3), _pspec(3, nf), _pspec(1), _pspec(1, nf)],
        out_specs=(_cspec(T), _cspec(T), _pspec(3), _pspec(3), _pspec(1), _pspec(1)),
        out_shape=(S((Bl, T, F), BF16), S((Bl, T, F), BF16), S((3, F), F32), S((3, F), F32), S((1, F), F32),
                   S((1, F), F32)),
        compiler_params=_cparams("parallel", "arbitrary"), name=name)(ug3, uv3, dact3, wc, wc, bc, bc)


def _lru_gates(xc, rp, ip, br_ref, bi_ref, lam_ref):
    r = jax.nn.sigmoid(rp + br_ref[...])
    i = jax.nn.sigmoid(ip + bi_ref[...])
    lam = lam_ref[...]
    sp = jnp.maximum(-lam, 0.0) + jnp.log1p(jnp.exp(-jnp.abs(lam)))
    log_a = (-LRU_C) * r * sp
    a = jnp.exp(log_a)
    z = 2.0 * log_a
    one_m_a2 = jnp.where(z > -0.05, -z * (1.0 + z * (0.5 + z * (1.0 / 6.0 + z * (1.0 / 24.0)))), 1.0 - a * a)
    mult = jnp.sqrt(one_m_a2)
    return r, i, sp, a, mult


def _rglru_fwd(xc3, gates3, proj3, br, bi, lam, name):
    Bl, T, C = xc3.shape
    nsteps = int(math.log2(T))
    assert 1 << nsteps == T

    def body(xc_ref, rp_ref, ip_ref, ug_ref, br_ref, bi_ref, lam_ref, y_ref, h_ref):
        row = lax.broadcasted_iota(jnp.int32, (T, CT), 0)
        xc = xc_ref[0]
        r, i, sp, a, mult = _lru_gates(xc, rp_ref[0], ip_ref[0], br_ref, bi_ref, lam_ref)
        b = mult * (i * xc)
        for st in range(nsteps):
            s = 1 << st
            a_sh = jnp.where(row >= s, pltpu.roll(a, s, axis=0), 1.0)
            b = a * _shift_down(b, s, row) + b
            a = a * a_sh
        h_ref[0] = b
        y_ref[0] = (b * _gelu(ug_ref[0])[0]).astype(BF16)

    return pl.pallas_call(
        body, grid=(C // CT, Bl),
        in_specs=[_cspec(T), _cspec(T), _cspec(T, C // CT), _cspec(T), _pspec(1), _pspec(1), _pspec(1)],
        out_specs=(_cspec(T), _cspec(T)), out_shape=(S((Bl, T, C), BF16), S((Bl, T, C), F32)),
        compiler_params=_cparams("parallel", "arbitrary"), name=name)(xc3, gates3, gates3, proj3, br, bi, lam)


def _rglru_bwd(dy3, xc3, gates3, proj3, h3, br, bi, lam, name):
    Bl, T, C = xc3.shape
    nsteps = int(math.log2(T))

    def body(dy_ref, xc_ref, rp_ref, ip_ref, ug_ref, h_ref, br_ref, bi_ref, lam_ref,
             dxc_ref, drp_ref, dip_ref, dug_ref, dbr_ref, dbi_ref, dlam_ref):
        row = lax.broadcasted_iota(jnp.int32, (T, CT), 0)
        xc = xc_ref[0]
        r, i, sp, a, mult = _lru_gates(xc, rp_ref[0], ip_ref[0], br_ref, bi_ref, lam_ref)
        h = h_ref[0]
        dy = dy_ref[0]
        ug = ug_ref[0]
        gel, t = _gelu(ug)
        dug_ref[0] = (dy * h * _dgelu(ug, t)).astype(BF16)
        gacc = dy * gel
        an = _shift_up(a, 1, row)
        for st in range(nsteps):
            s = 1 << st
            an_sh = jnp.where(row < T - s, pltpu.roll(an, T - s, axis=0), 1.0)
            gacc = an * _shift_up(gacc, s, row) + gacc
            an = an * an_sh
        da = gacc * _shift_down(h, 1, row)
        ix = i * xc
        d_mult = gacc * ix
        d_i = gacc * mult * xc
        dxc_ref[0] = gacc * mult * i
        d_log_a = da * a - d_mult * (a * a) / mult
        d_r = d_log_a * ((-LRU_C) * sp)
        d_sp = jnp.sum(d_log_a * ((-LRU_C) * r), axis=0, keepdims=True)
        drp = d_r * r * (1.0 - r)
        dip = d_i * i * (1.0 - i)
        drp_ref[0] = drp.astype(BF16)
        dip_ref[0] = dip.astype(BF16)

        @pl.when(pl.program_id(1) == 0)
        def _():
            dbr_ref[...] = jnp.zeros_like(dbr_ref)
            dbi_ref[...] = jnp.zeros_like(dbi_ref)
            dlam_ref[...] = jnp.zeros_like(dlam_ref)

        dbr_ref[...] += jnp.sum(drp, axis=0, keepdims=True)
        dbi_ref[...] += jnp.sum(dip, axis=0, keepdims=True)
        dlam_ref[...] += d_sp * (-jax.nn.sigmoid(-lam_ref[...]))

    vec = S((1, C), F32)
    act = S((Bl, T, C), BF16)
    return pl.pallas_call(
        body, grid=(C // CT, Bl),
        in_specs=[_cspec(T), _cspec(T), _cspec(T), _cspec(T, C // CT), _cspec(T), _cspec(T)] + [_pspec(1)] * 3,
        out_specs=(_cspec(T), _cspec(T), _cspec(T), _cspec(T), _pspec(1), _pspec(1), _pspec(1)),
        out_shape=(S((Bl, T, C), F32), act, act, act, vec, vec, vec),
        compiler_params=_cparams("parallel", "arbitrary"), name=name)(dy3, xc3, gates3, gates3, proj3, h3, br, bi, lam)


NT = (((1,), (1,)), ((), ()))
TN = (((0,), (0,)), ((), ()))


def _hs(h):
    return slice(h * HEAD, (h + 1) * HEAD)


def _mem_softmax(qb, kb):
    s = lax.dot_general(qb, kb, NT, preferred_element_type=F32) * (HEAD ** -0.5)
    e = jnp.exp(s - jnp.max(s, axis=-1, keepdims=True))
    return e / jnp.sum(e, axis=-1, keepdims=True)


def _mem_attn_fwd(proj3, q_off, mkv3, name):
    Bl, T, _ = proj3.shape
    M = mkv3.shape[1]
    tq = _tile(T, (512, 256, 128))

    def body(q_ref, k_ref, v_ref, o_ref):
        q = q_ref[0].astype(BF16)
        k = k_ref[0].astype(BF16)
        v = v_ref[0].astype(BF16)
        outs = []
        for h in range(MEM_HEADS):
            p = _mem_softmax(q[:, _hs(h)], k[:, _hs(h)])
            outs.append(jnp.dot(p.astype(BF16), v[:, _hs(h)], preferred_element_type=F32))
        o_ref[0] = jnp.concatenate(outs, axis=-1).astype(BF16)

    return pl.pallas_call(
        body, grid=(Bl, T // tq),
        in_specs=[pl.BlockSpec((1, tq, MEM_W), lambda b, t: (b, t, q_off // MEM_W)),
                  pl.BlockSpec((1, M, MEM_W), lambda b, t: (b, 0, 0)),
                  pl.BlockSpec((1, M, MEM_W), lambda b, t: (b, 0, 1))],
        out_specs=pl.BlockSpec((1, tq, MEM_W), lambda b, t: (b, t, 0)),
        out_shape=S((Bl, T, MEM_W), BF16), compiler_params=_cparams("parallel", "parallel"), name=name)(
            proj3, mkv3, mkv3)


def _mem_attn_bwd(proj3, q_off, mkv3, do3, name):
    Bl, T, _ = proj3.shape
    M = mkv3.shape[1]
    tq = _tile(T, (512, 256, 128))
    scale = HEAD ** -0.5

    def body(q_ref, k_ref, v_ref, do_ref, dq_ref, dkv_ref):
        q = q_ref[0].astype(BF16)
        k = k_ref[0].astype(BF16)
        v = v_ref[0].astype(BF16)
        do = do_ref[0].astype(BF16)
        dqs, dks, dvs = [], [], []
        for h in range(MEM_HEADS):
            qh, kh, vh, doh = q[:, _hs(h)], k[:, _hs(h)], v[:, _hs(h)], do[:, _hs(h)]
            p = _mem_softmax(qh, kh)
            dvs.append(lax.dot_general(p.astype(BF16), doh, TN, preferred_element_type=F32))
            dp = lax.dot_general(doh, vh, NT, preferred_element_type=F32)
            ds = (p * (dp - jnp.sum(p * dp, axis=-1, keepdims=True)) * scale).astype(BF16)
            dqs.append(jnp.dot(ds, kh, preferred_element_type=F32))
            dks.append(lax.dot_general(ds, qh, TN, preferred_element_type=F32))
        dq_ref[0] = jnp.concatenate(dqs, axis=-1).astype(BF16)

        @pl.when(pl.program_id(1) == 0)
        def _():
            dkv_ref[...] = jnp.zeros_like(dkv_ref)

        dkv_ref[0] += jnp.concatenate(dks + dvs, axis=-1)

    return pl.pallas_call(
        body, grid=(Bl, T // tq),
        in_specs=[pl.BlockSpec((1, tq, MEM_W), lambda b, t: (b, t, q_off // MEM_W)),
                  pl.BlockSpec((1, M, MEM_W), lambda b, t: (b, 0, 0)),
                  pl.BlockSpec((1, M, MEM_W), lambda b, t: (b, 0, 1)),
                  pl.BlockSpec((1, tq, MEM_W), lambda b, t: (b, t, 0))],
        out_specs=(pl.BlockSpec((1, tq, MEM_W), lambda b, t: (b, t, 0)),
                   pl.BlockSpec((1, M, 2 * MEM_W), lambda b, t: (b, 0, 0))),
        out_shape=(S((Bl, T, MEM_W), BF16), S((Bl, M, 2 * MEM_W), F32)),
        compiler_params=_cparams("parallel", "arbitrary"), name=name)(proj3, mkv3, mkv3, do3)


def _swa_probs(qh, kph, kch, sink, slope, has_prev):
    qi = lax.broadcasted_iota(jnp.int32, (WIN, WIN), 0)
    kj = lax.broadcasted_iota(jnp.int32, (WIN, WIN), 1)
    scale = HEAD ** -0.5
    sp = lax.dot_general(qh, kph, NT, preferred_element_type=F32) * scale
    sc = lax.dot_general(qh, kch, NT, preferred_element_type=F32) * scale
    dist_p = (qi + WIN - kj).astype(F32)
    dist_c = (qi - kj).astype(F32)
    neg = -jnp.inf
    sp = jnp.where(kj > qi + jnp.where(has_prev, 0, WIN), sp - slope * dist_p, neg)
    sc = jnp.where(kj <= qi, sc - slope * dist_c, neg)
    m = jnp.maximum(jnp.maximum(jnp.max(sp, axis=-1, keepdims=True), jnp.max(sc, axis=-1, keepdims=True)), sink)
    ep = jnp.exp(sp - m)
    ec = jnp.exp(sc - m)
    es = jnp.exp(sink - m)
    inv = 1.0 / (jnp.sum(ep, axis=-1, keepdims=True) + jnp.sum(ec, axis=-1, keepdims=True) + es)
    return ep * inv, ec * inv, es * inv


def _swa_specs(nb):
    prev = lambda n: jnp.maximum(n - 1, 0)
    q = pl.BlockSpec((1, WIN, MIX_W), lambda b, n: (b, n, 0))
    kp = pl.BlockSpec((1, WIN, MEM_W), lambda b, n: (b, prev(n), 0))
    kc = pl.BlockSpec((1, WIN, MEM_W), lambda b, n: (b, n, 0))
    vp = pl.BlockSpec((1, WIN, MEM_W), lambda b, n: (b, prev(n), 1))
    vc = pl.BlockSpec((1, WIN, MEM_W), lambda b, n: (b, n, 1))
    sm = pl.BlockSpec(memory_space=pltpu.SMEM)
    return q, kp, kc, vp, vc, sm


def _swa_fwd(proj3, kv3, sinks, name):
    Bl, T, _ = proj3.shape
    nb = T // WIN
    q_s, kp_s, kc_s, vp_s, vc_s, sm = _swa_specs(nb)

    def body(q_ref, kp_ref, kc_ref, vp_ref, vc_ref, sink_ref, o_ref):
        has_prev = pl.program_id(1) > 0
        q = q_ref[0].astype(BF16)
        kp, kc = kp_ref[0].astype(BF16), kc_ref[0].astype(BF16)
        vp, vc = vp_ref[0].astype(BF16), vc_ref[0].astype(BF16)
        outs = []
        for h in range(SWA_HEADS):
            kvs = _hs(h // SWA_GROUP)
            pp, pc, _ = _swa_probs(q[:, _hs(h)], kp[:, kvs], kc[:, kvs], sink_ref[h], SLOPES[h], has_prev)
            outs.append(jnp.dot(pp.astype(BF16), vp[:, kvs], preferred_element_type=F32)
                        + jnp.dot(pc.astype(BF16), vc[:, kvs], preferred_element_type=F32))
        o_ref[0] = jnp.concatenate(outs, axis=-1).astype(BF16)

    return pl.pallas_call(
        body, grid=(Bl, nb), in_specs=[q_s, kp_s, kc_s, vp_s, vc_s, sm], out_specs=q_s,
        out_shape=S((Bl, T, MIX_W), BF16), compiler_params=_cparams("parallel", "parallel"), name=name)(
            proj3, kv3, kv3, kv3, kv3, sinks)


def _swa_bwd(proj3, kv3, sinks, do3, name):
    Bl, T, _ = proj3.shape
    nb = T // WIN
    q_s, kp_s, kc_s, vp_s, vc_s, sm = _swa_specs(nb)
    kv_s = pl.BlockSpec((1, WIN, 2 * MEM_W), lambda b, n: (b, n, 0))
    sk_s = pl.BlockSpec((8, LANES), lambda b, n: (0, 0))
    scale = HEAD ** -0.5

    def body(q_ref, kp_ref, kc_ref, vp_ref, vc_ref, sink_ref, do_ref, dq_ref, dkc_ref, dkp_ref, dsk_ref):
        has_prev = pl.program_id(1) > 0
        q = q_ref[0].astype(BF16)
        kp, kc = kp_ref[0].astype(BF16), kc_ref[0].astype(BF16)
        vp, vc = vp_ref[0].astype(BF16), vc_ref[0].astype(BF16)
        do = do_ref[0].astype(BF16)
        lane = lax.broadcasted_iota(jnp.int32, (8, LANES), 1)
        srow = lax.broadcasted_iota(jnp.int32, (8, LANES), 0)
        dsk = jnp.zeros((8, LANES), F32)
        dqs = []
        nkv = SWA_HEADS // SWA_GROUP
        dkc, dkp, dvc, dvp = [None] * nkv, [None] * nkv, [None] * nkv, [None] * nkv
        for h in range(SWA_HEADS):
            kvh = h // SWA_GROUP
            kvs = _hs(kvh)
            qh, doh = q[:, _hs(h)], do[:, _hs(h)]
            pp, pc, ps = _swa_probs(qh, kp[:, kvs], kc[:, kvs], sink_ref[h], SLOPES[h], has_prev)
            dpp = lax.dot_general(doh, vp[:, kvs], NT, preferred_element_type=F32)
            dpc = lax.dot_general(doh, vc[:, kvs], NT, preferred_element_type=F32)
            delta = jnp.sum(pp * dpp, axis=-1, keepdims=True) + jnp.sum(pc * dpc, axis=-1, keepdims=True)
            dsp = (pp * (dpp - delta) * scale).astype(BF16)
            dsc = (pc * (dpc - delta) * scale).astype(BF16)
            dqs.append(jnp.dot(dsp, kp[:, kvs], preferred_element_type=F32)
                       + jnp.dot(dsc, kc[:, kvs], preferred_element_type=F32))
            parts = (lax.dot_general(dsc, qh, TN, preferred_element_type=F32),
                     lax.dot_general(dsp, qh, TN, preferred_element_type=F32),
                     lax.dot_general(pc.astype(BF16), doh, TN, preferred_element_type=F32),
                     lax.dot_general(pp.astype(BF16), doh, TN, preferred_element_type=F32))
            for acc, part in zip((dkc, dkp, dvc, dvp), parts):
                acc[kvh] = part if acc[kvh] is None else acc[kvh] + part
            dsk = dsk + jnp.where((lane == h) & (srow == 0), -jnp.sum(ps * delta), 0.0)
        dq_ref[0] = jnp.concatenate(dqs, axis=-1).astype(BF16)
        dkc_ref[0] = jnp.concatenate(dkc + dvc, axis=-1)
        dkp_ref[0] = jnp.concatenate(dkp + dvp, axis=-1)

        @pl.when((pl.program_id(0) == 0) & (pl.program_id(1) == 0))
        def _():
            dsk_ref[...] = jnp.zeros_like(dsk_ref)

        dsk_ref[...] += dsk

    return pl.pallas_call(
        body, grid=(Bl, nb), in_specs=[q_s, kp_s, kc_s, vp_s, vc_s, sm, q_s], out_specs=(q_s, kv_s, kv_s, sk_s),
        out_shape=(S((Bl, T, MIX_W), BF16), S((Bl, T, 2 * MEM_W), F32), S((Bl, T, 2 * MEM_W), F32), S((8, LANES), F32)),
        compiler_params=_cparams("arbitrary", "arbitrary"), name=name)(proj3, kv3, kv3, kv3, kv3, sinks, do3)


def _kv_grad_combine(parts, name):
    Bl, T, W = parts[0][0].shape
    nb = T // WIN
    nl = len(parts)

    def body(*refs):
        o_ref = refs[-1]
        has_next = jnp.where(pl.program_id(1) == nb - 1, 0.0, 1.0)
        acc = None
        for l in range(nl):
            c = refs[2 * l][0] + has_next * refs[2 * l + 1][0]
            acc = c if acc is None else acc + c
        o_ref[0] = acc.astype(BF16)

    cur = pl.BlockSpec((1, WIN, W), lambda b, n: (b, n, 0))
    nxt = pl.BlockSpec((1, WIN, W), lambda b, n: (b, jnp.minimum(n + 1, nb - 1), 0))
    return pl.pallas_call(
        body, grid=(Bl, nb), in_specs=[cur, nxt] * nl, out_specs=cur, out_shape=S((Bl, T, W), BF16),
        compiler_params=_cparams("parallel", "parallel"), name=name)(*[a for pr in parts for a in pr])


def _loss_bwd(y, target, name="loss"):
    N, D = y.shape
    tm = _tile(N, (512, 256, 128))

    def body(y_ref, t_ref, dy_ref, l_ref):
        e = y_ref[...] - t_ref[...]
        dy_ref[...] = e * (1.0 / D)

        @pl.when(pl.program_id(0) == 0)
        def _():
            l_ref[...] = jnp.zeros_like(l_ref)

        l_ref[...] += jnp.sum(e * e, axis=0, keepdims=True) * (0.5 / D)

    row = pl.BlockSpec((tm, D), lambda i: (i, 0))
    vec = pl.BlockSpec((1, D), lambda i: (0, 0))
    return pl.pallas_call(
        body, grid=(N // tm,), in_specs=[row, row], out_specs=(row, vec), out_shape=(S((N, D), F32), S((1, D), F32)),
        compiler_params=_cparams("arbitrary"), name=name)(y, target)


def _all_gather(x, name):
    R, C = x.shape

    def body(x_ref, out_ref, send_sems, recv_sems, local_sem):
        mx, my, mc = lax.axis_index("x"), lax.axis_index("y"), lax.axis_index("c")
        me, sibling = (mx, my, mc), (mx, my, 1 - mc)
        chips = [(1 - mx, my), (mx, 1 - my), (1 - mx, 1 - my)]

        def rows(px, py, pc):
            return out_ref.at[4 * px + 2 * py + pc]

        def copy(kk, block, to, src=None):
            return pltpu.make_async_remote_copy(
                src_ref=rows(*block) if src is None else src, dst_ref=rows(*block), send_sem=send_sems.at[kk],
                recv_sem=recv_sems.at[kk], device_id=to, device_id_type=MESH)

        mine = pltpu.make_async_copy(x_ref, rows(*me), local_sem)
        mine.start()
        first = [copy(0, me, sibling, src=x_ref)]
        first += [copy(1 + j, me, (*chip, mc), src=x_ref) for j, chip in enumerate(chips)]
        for cp in first:
            cp.start()
        passed = [copy(4 + j, (*chip, mc), sibling) for j, chip in enumerate(chips)]
        for j, chip in enumerate(chips):
            copy(1 + j, (*chip, mc), me).wait_recv()
            passed[j].start()
        copy(0, sibling, me).wait_recv()
        for j, chip in enumerate(chips):
            copy(4 + j, (*chip, 1 - mc), me).wait_recv()
        for cp in first + passed:
            cp.wait_send()
        mine.wait()

    return pl.pallas_call(
        body, out_shape=S((N_DEV, R, C), x.dtype), in_specs=[ANY], out_specs=ANY,
        scratch_shapes=[pltpu.SemaphoreType.DMA((7,)), pltpu.SemaphoreType.DMA((7,)), pltpu.SemaphoreType.DMA(())],
        name=name)(x)


def _ag_weights(shards, row_sharded, name):
    n = len(shards)

    def full_shape(a, rows):
        if rows:
            return a.shape[:-2] + (N_DEV * a.shape[-2],) + a.shape[-1:]
        return (N_DEV,) + a.shape

    def body(*refs):
        x_refs, o_refs = refs[:n], refs[n:2 * n]
        send_sems, recv_sems, local_sems = refs[2 * n:]
        mx, my, mc = lax.axis_index("x"), lax.axis_index("y"), lax.axis_index("c")
        me, sibling = (mx, my, mc), (mx, my, 1 - mc)
        chips = [(1 - mx, my), (mx, 1 - my), (1 - mx, 1 - my)]

        def dst(t, px, py, pc):
            d = 4 * px + 2 * py + pc
            if not row_sharded[t]:
                return o_refs[t].at[d]
            r = shards[t].shape[-2]
            idx = (slice(None),) * (shards[t].ndim - 2) + (pl.ds(pl.multiple_of(d * r, 16), r), slice(None))
            return o_refs[t].at[idx]

        def copy(kk, t, block, to, src=None):
            return pltpu.make_async_remote_copy(
                src_ref=dst(t, *block) if src is None else src, dst_ref=dst(t, *block),
                send_sem=send_sems.at[kk * n + t], recv_sem=recv_sems.at[kk * n + t], device_id=to,
                device_id_type=MESH)

        mine = [pltpu.make_async_copy(x_refs[t], dst(t, *me), local_sems.at[t]) for t in range(n)]
        for cp in mine:
            cp.start()
        first = []
        for t in range(n):
            first.append(copy(0, t, me, sibling, src=x_refs[t]))
            first += [copy(1 + j, t, me, (*chip, mc), src=x_refs[t]) for j, chip in enumerate(chips)]
        for cp in first:
            cp.start()
        passed = []
        for j, chip in enumerate(chips):
            for t in range(n):
                copy(1 + j, t, (*chip, mc), me).wait_recv()
                cp = copy(4 + j, t, (*chip, mc), sibling)
                cp.start()
                passed.append(cp)
        for t in range(n):
            copy(0, t, sibling, me).wait_recv()
            for j, chip in enumerate(chips):
                copy(4 + j, t, (*chip, 1 - mc), me).wait_recv()
        for cp in first + passed:
            cp.wait_send()
        for cp in mine:
            cp.wait()

    return pl.pallas_call(
        body, out_shape=tuple(S(full_shape(a, r), a.dtype) for a, r in zip(shards, row_sharded)),
        in_specs=[ANY] * n, out_specs=tuple([ANY] * n),
        scratch_shapes=[pltpu.SemaphoreType.DMA((7 * n,)), pltpu.SemaphoreType.DMA((7 * n,)),
                        pltpu.SemaphoreType.DMA((n,))],
        name=name)(*shards)


def _rs_sibling(gs, name):
    n = len(gs)

    def body(*refs):
        g_refs, o_refs = refs[:n], refs[n:2 * n]
        send_sems, recv_sems = refs[2 * n:]
        mx, my, mc = lax.axis_index("x"), lax.axis_index("y"), lax.axis_index("c")
        copies = [pltpu.make_async_remote_copy(
            src_ref=g_refs[t].at[:, 2 * j + (1 - mc)], dst_ref=o_refs[t].at[j], send_sem=send_sems.at[j * n + t],
            recv_sem=recv_sems.at[j * n + t], device_id=(mx, my, 1 - mc), device_id_type=MESH)
            for t in range(n) for j in range(4)]
        for cp in copies:
            cp.start()
        for cp in copies:
            cp.wait_recv()
        for cp in copies:
            cp.wait_send()

    return pl.pallas_call(
        body, out_shape=tuple(S((4, g.shape[0]) + g.shape[2:], g.dtype) for g in gs), in_specs=[ANY] * n,
        out_specs=tuple([ANY] * n),
        scratch_shapes=[pltpu.SemaphoreType.DMA((4 * n,)), pltpu.SemaphoreType.DMA((4 * n,))], name=name)(*gs)


def _rs_chips(ps, name):
    n = len(ps)

    def body(*refs):
        p_refs, o_refs = refs[:n], refs[n:2 * n]
        send_sems, recv_sems = refs[2 * n:]
        mx, my, mc = lax.axis_index("x"), lax.axis_index("y"), lax.axis_index("c")
        chips = [(1 - mx, my), (mx, 1 - my), (1 - mx, 1 - my)]
        copies = [pltpu.make_async_remote_copy(
            src_ref=p_refs[t].at[2 * cx + cy], dst_ref=o_refs[t].at[j], send_sem=send_sems.at[j * n + t],
            recv_sem=recv_sems.at[j * n + t], device_id=(cx, cy, mc), device_id_type=MESH)
            for t in range(n) for j, (cx, cy) in enumerate(chips)]
        for cp in copies:
            cp.start()
        for cp in copies:
            cp.wait_recv()
        for cp in copies:
            cp.wait_send()

    return pl.pallas_call(
        body, out_shape=tuple(S((3,) + p.shape[1:], p.dtype) for p in ps), in_specs=[ANY] * n,
        out_specs=tuple([ANY] * n),
        scratch_shapes=[pltpu.SemaphoreType.DMA((3 * n,)), pltpu.SemaphoreType.DMA((3 * n,))], name=name)(*ps)


def _rows_tile(b):
    return _tile(b, (512, 256, 128)) if b > 512 else b


def _pair_sum(g, got, name):
    A, _, B, C = g.shape
    tb = _rows_tile(B)
    core = lax.axis_index("c").astype(jnp.int32).reshape(1)

    def body(c_ref, g_ref, r_ref, o_ref):
        o_ref[...] = (g_ref[...].astype(F32) + r_ref[...].astype(F32)).astype(o_ref.dtype)

    return pl.pallas_call(
        body,
        grid_spec=pltpu.PrefetchScalarGridSpec(
            num_scalar_prefetch=1, grid=(4, A, B // tb),
            in_specs=[pl.BlockSpec((1, 1, tb, C), lambda j, a, i, c_ref: (a, 2 * j + c_ref[0], i, 0)),
                      pl.BlockSpec((1, 1, tb, C), lambda j, a, i, c_ref: (j, a, i, 0))],
            out_specs=pl.BlockSpec((1, 1, tb, C), lambda j, a, i, c_ref: (j, a, i, 0))),
        out_shape=S((4, A, B, C), g.dtype), compiler_params=_cparams("parallel", "parallel", "parallel"),
        name=name)(core, g, got)


def _cols_to_natural(g8, name):
    _, L, K, c = g8.shape
    tk = _tile(K, (256, 128))

    def body(x_ref, o_ref):
        o_ref[...] = jnp.concatenate([x_ref[d] for d in range(N_DEV)], axis=-1)

    return pl.pallas_call(
        body, grid=(L, K // tk), in_specs=[pl.BlockSpec((N_DEV, None, tk, c), lambda l, i: (0, l, i, 0))],
        out_specs=pl.BlockSpec((None, tk, N_DEV * c), lambda l, i: (l, i, 0)),
        out_shape=S((L, K, N_DEV * c), g8.dtype), compiler_params=_cparams("parallel", "parallel"), name=name)(g8)


def _natural_to_cols(g, name):
    L, K, c8 = g.shape
    c = c8 // N_DEV
    tk = _tile(K, (256, 128))

    def body(x_ref, o_ref):
        xv = x_ref[...]
        for d in range(N_DEV):
            o_ref[d] = xv[:, d * c:(d + 1) * c]

    return pl.pallas_call(
        body, grid=(L, K // tk), in_specs=[pl.BlockSpec((None, tk, c8), lambda l, i: (l, i, 0))],
        out_specs=pl.BlockSpec((N_DEV, None, tk, c), lambda l, i: (0, l, i, 0)),
        out_shape=S((N_DEV, L, K, c), g.dtype), compiler_params=_cparams("parallel", "parallel"), name=name)(g)


def _adamw_math(w, g, m, v):
    m = ADAM_B1 * m + (1.0 - ADAM_B1) * g
    v = ADAM_B2 * v + (1.0 - ADAM_B2) * (g * g)
    m_hat = m / (1.0 - ADAM_B1 ** ADAM_STEP)
    v_hat = v / (1.0 - ADAM_B2 ** ADAM_STEP)
    delta = -ADAM_LR * (m_hat / (jnp.sqrt(v_hat) + ADAM_EPS) + ADAM_WD * w)
    return delta, m, v


def _adamw_sharded(p, got, w, m, v, name):
    A, B, C = w.shape
    tb = _rows_tile(B)
    chip = (2 * lax.axis_index("x") + lax.axis_index("y")).astype(jnp.int32).reshape(1)

    def body(c_ref, p_ref, got_ref, w_ref, m_ref, v_ref, g_out, d_out, m_out, v_out):
        g = p_ref[0].astype(F32)
        for j in range(3):
            g = g + got_ref[j].astype(F32)
        d, mn, vn = _adamw_math(w_ref[...], g, m_ref[...], v_ref[...])
        g_out[...] = g
        d_out[...] = d
        m_out[...] = mn
        v_out[...] = vn

    blk = pl.BlockSpec((1, tb, C), lambda a, i, c_ref: (a, i, 0))
    return pl.pallas_call(
        body,
        grid_spec=pltpu.PrefetchScalarGridSpec(
            num_scalar_prefetch=1, grid=(A, B // tb),
            in_specs=[pl.BlockSpec((1, 1, tb, C), lambda a, i, c_ref: (c_ref[0], a, i, 0)),
                      pl.BlockSpec((3, 1, tb, C), lambda a, i, c_ref: (0, a, i, 0)), blk, blk, blk],
            out_specs=(blk, blk, blk, blk)),
        out_shape=(S((A, B, C), F32),) * 4, compiler_params=_cparams("parallel", "parallel"), name=name)(
            chip, p, got, w, m, v)


def _adamw_replicated(parts, w, m, v, name):
    R, C = w.shape
    rb = _tile(R, (512, 256, 128, 64, 32, 16, 8))

    def body(p_ref, w_ref, m_ref, v_ref, g_out, d_out, m_out, v_out):
        g = p_ref[0]
        for j in range(1, N_DEV):
            g = g + p_ref[j]
        d, mn, vn = _adamw_math(w_ref[...], g, m_ref[...], v_ref[...])
        g_out[...] = g
        d_out[...] = d
        m_out[...] = mn
        v_out[...] = vn

    blk = pl.BlockSpec((rb, C), lambda i: (i, 0))
    return pl.pallas_call(
        body, grid=(R // rb,), in_specs=[pl.BlockSpec((N_DEV, rb, C), lambda i: (0, i, 0)), blk, blk, blk],
        out_specs=(blk, blk, blk, blk), out_shape=(S((R, C), F32),) * 4, compiler_params=_cparams("parallel"),
        name=name)(parts, w, m, v)


def _pack(arrs, rows_mult, dtype):
    flat = jnp.concatenate([a.reshape(-1).astype(dtype) for a in arrs])
    n = flat.shape[0]
    per = rows_mult * LANES
    tot = -(-n // per) * per
    return jnp.pad(flat, (0, tot - n)).reshape(tot // LANES, LANES)


def _unpack(blob, shapes):
    flat = blob.reshape(-1)
    out, off = [], 0
    for shp in shapes:
        n = int(np.prod(shp))
        out.append(flat[off:off + n].reshape(shp))
        off += n
    return out


def _small_to_natural(g8):
    t = jnp.moveaxis(g8, 0, -2)
    return t.reshape(t.shape[:-2] + (N_DEV * t.shape[-1],))


def _small_to_cols(g):
    t = g.reshape(g.shape[:-1] + (N_DEV, g.shape[-1] // N_DEV))
    return jnp.moveaxis(t, -2, 0)


def _block_diag(w):
    nb, bs, _ = w.shape
    eye = jnp.eye(nb, dtype=w.dtype)
    return (eye[:, None, :, None] * w[:, :, None, :]).reshape(nb * bs, nb * bs)


def _diag_blocks(d, nb, bs):
    d4 = d.reshape(nb, bs, nb, bs)
    return jnp.stack([d4[i, :, i, :] for i in range(nb)])


def kernel(x, mem, g_mix_pre, g_mix_post, g_ffn_pre, g_ffn_post, g_mem, w_mem_kv, w_mix_out, w_ffn_up, w_ffn_conv, b_ffn_conv, w_ffn_down, w_in_a, w_conv_a, b_conv_a, w_rg_r, b_rg_r, w_rg_i, b_rg_i, lru_lambda, w_in_b, sinks_b, g_kv, w_kv, loss_target, m_g_mix_pre, m_g_mix_post, m_g_ffn_pre, m_g_ffn_post, m_g_mem, m_w_mem_kv, m_w_mix_out, m_w_ffn_up, m_w_ffn_conv, m_b_ffn_conv, m_w_ffn_down, m_w_in_a, m_w_conv_a, m_b_conv_a, m_w_rg_r, m_b_rg_r, m_w_rg_i, m_b_rg_i, m_lru_lambda, m_w_in_b, m_sinks_b, m_g_kv, m_w_kv, v_g_mix_pre, v_g_mix_post, v_g_ffn_pre, v_g_ffn_post, v_g_mem, v_w_mem_kv, v_w_mix_out, v_w_ffn_up, v_w_ffn_conv, v_b_ffn_conv, v_w_ffn_down, v_w_in_a, v_w_conv_a, v_b_conv_a, v_w_rg_r, v_b_rg_r, v_w_rg_i, v_b_rg_i, v_lru_lambda, v_w_in_b, v_sinks_b, v_g_kv, v_w_kv):
    w_loc = dict(g_mix_pre=g_mix_pre, g_mix_post=g_mix_post, g_ffn_pre=g_ffn_pre, g_ffn_post=g_ffn_post, g_mem=g_mem,
                 w_mem_kv=w_mem_kv, w_mix_out=w_mix_out, w_ffn_up=w_ffn_up, w_ffn_conv=w_ffn_conv,
                 b_ffn_conv=b_ffn_conv, w_ffn_down=w_ffn_down, w_in_a=w_in_a, w_conv_a=w_conv_a, b_conv_a=b_conv_a,
                 w_rg_r=w_rg_r, b_rg_r=b_rg_r, w_rg_i=w_rg_i, b_rg_i=b_rg_i, lru_lambda=lru_lambda, w_in_b=w_in_b,
                 sinks_b=sinks_b, g_kv=g_kv, w_kv=w_kv)
    m_loc = dict(g_mix_pre=m_g_mix_pre, g_mix_post=m_g_mix_post, g_ffn_pre=m_g_ffn_pre, g_ffn_post=m_g_ffn_post,
                 g_mem=m_g_mem, w_mem_kv=m_w_mem_kv, w_mix_out=m_w_mix_out, w_ffn_up=m_w_ffn_up,
                 w_ffn_conv=m_w_ffn_conv, b_ffn_conv=m_b_ffn_conv, w_ffn_down=m_w_ffn_down, w_in_a=m_w_in_a,
                 w_conv_a=m_w_conv_a, b_conv_a=m_b_conv_a, w_rg_r=m_w_rg_r, b_rg_r=m_b_rg_r, w_rg_i=m_w_rg_i,
                 b_rg_i=m_b_rg_i, lru_lambda=m_lru_lambda, w_in_b=m_w_in_b, sinks_b=m_sinks_b, g_kv=m_g_kv,
                 w_kv=m_w_kv)
    v_loc = dict(g_mix_pre=v_g_mix_pre, g_mix_post=v_g_mix_post, g_ffn_pre=v_g_ffn_pre, g_ffn_post=v_g_ffn_post,
                 g_mem=v_g_mem, w_mem_kv=v_w_mem_kv, w_mix_out=v_w_mix_out, w_ffn_up=v_w_ffn_up,
                 w_ffn_conv=v_w_ffn_conv, b_ffn_conv=v_b_ffn_conv, w_ffn_down=v_w_ffn_down, w_in_a=v_w_in_a,
                 w_conv_a=v_w_conv_a, b_conv_a=v_b_conv_a, w_rg_r=v_w_rg_r, b_rg_r=v_b_rg_r, w_rg_i=v_w_rg_i,
                 b_rg_i=v_b_rg_i, lru_lambda=v_lru_lambda, w_in_b=v_w_in_b, sinks_b=v_sinks_b, g_kv=v_g_kv,
                 w_kv=v_w_kv)

    Bl, T, D = x.shape
    Ml = mem.shape[1]
    N = Bl * T
    depth = g_mix_pre.shape[0]
    n_a = w_in_a.shape[0]
    F = w_ffn_down.shape[1] * N_DEV
    sh_names = [n for n, _ in SHARDED]
    row_sharded = {n: ax == w_loc[n].ndim - 2 for n, ax in SHARDED}

    payload = [w_loc[n] if n in SMALL_SHARDED else w_loc[n].astype(BF16) for n in sh_names]
    W = dict(zip(sh_names, _ag_weights(payload, [row_sharded[n] for n in sh_names], name="ag_weights")))
    W["w_ffn_up"] = _cols_to_natural(W["w_ffn_up"], name="unshard_w_ffn_up")
    W["w_in_a"] = _cols_to_natural(W["w_in_a"], name="unshard_w_in_a")
    for n in SMALL_SHARDED:
        W[n] = _small_to_natural(W[n])
    nblk, bsz = w_rg_r.shape[1], w_rg_r.shape[2]
    wbd = [jnp.concatenate([_block_diag(w_rg_r[j]), _block_diag(w_rg_i[j])], axis=1).astype(BF16) for j in range(n_a)]

    def vec(a):
        return a.reshape(1, -1)

    x2 = x.reshape(N, D)
    mem2 = mem.reshape(Bl * Ml, D)
    saved = []
    kvn = kv3 = x_kv = None
    xs = x2
    for l in range(depth):
        sv = {"x0": xs}
        h1 = _rms_fwd(xs, vec(g_mix_pre[l]), BF16, name=f"rms_mixpre_{l}")
        memn = _rms_fwd(mem2, vec(g_mem[l]), BF16, name=f"rms_mem_{l}")
        mkv3 = _mm(memn, W["w_mem_kv"], bl=l, name=f"mm_memkv_{l}").reshape(Bl, Ml, 2 * MEM_W)
        if l < n_a:
            j = l
            proj = _mm(h1, W["w_in_a"], bl=j, name=f"mm_in_{l}")
            proj3 = proj.reshape(Bl, T, -1)
            xc3 = _conv_fwd_call(proj3, MIX_W, MIX_W, W["w_conv_a"][j], vec(W["b_conv_a"][j]), name=f"conv_a_{l}")
            gates3 = _mm(xc3.reshape(N, MIX_W), wbd[j], name=f"mm_gates_{l}").reshape(Bl, T, 2 * MIX_W)
            y_main3, hs3 = _rglru_fwd(xc3, gates3, proj3, vec(b_rg_r[j]), vec(b_rg_i[j]), vec(W["lru_lambda"][j]),
                                      name=f"rglru_fwd_{l}")
            q_off = 2 * MIX_W
            sv.update(xc3=xc3, gates3=gates3, hs3=hs3)
        else:
            j = l - n_a
            if l == n_a:
                x_kv = xs
                kvn = _rms_fwd(xs, vec(g_kv), BF16, name="rms_kv")
                kv3 = _mm(kvn, W["w_kv"], name="mm_kv").reshape(Bl, T, 2 * MEM_W)
            proj = _mm(h1, W["w_in_b"], bl=j, name=f"mm_in_{l}")
            proj3 = proj.reshape(Bl, T, -1)
            y_main3 = _swa_fwd(proj3, kv3, sinks_b[j], name=f"swa_fwd_{l}")
            q_off = MIX_W
        y_mem3 = _mem_attn_fwd(proj3, q_off, mkv3, name=f"memattn_fwd_{l}")
        y_main = y_main3.reshape(N, MIX_W)
        y_mem = y_mem3.reshape(N, MEM_W)
        y = _mm(y_main, W["w_mix_out"], bl=l, n=D, k=MIX_W, name=f"mm_mixout_main_{l}")
        y = _mm(y_mem, W["w_mix_out"], bl=l, n=D, k=MEM_W, b_off=(MIX_W, 0), add=y, name=f"mm_mixout_mem_{l}")
        x1 = _rms_fwd(y, vec(g_mix_post[l]), F32, res=xs, name=f"rms_mixpost_{l}")
        h2 = _rms_fwd(x1, vec(g_ffn_pre[l]), BF16, name=f"rms_ffnpre_{l}")
        ug = _mm(h2, W["w_ffn_up"], bl=l, n=F, name=f"mm_up_g_{l}")
        uv = _mm(h2, W["w_ffn_up"], bl=l, n=F, b_off=(0, F), name=f"mm_up_v_{l}")
        ug3, uv3 = ug.reshape(Bl, T, F), uv.reshape(Bl, T, F)
        act3 = _ffn_mid_fwd(ug3, uv3, W["w_ffn_conv"][l], vec(b_ffn_conv[l]), name=f"ffn_mid_fwd_{l}")
        act = act3.reshape(N, F)
        f = _mm(act, W["w_ffn_down"], bl=l, name=f"mm_down_{l}")
        x_next = _rms_fwd(f, vec(g_ffn_post[l]), F32, res=x1, name=f"rms_ffnpost_{l}")
        sv.update(h1=h1, memn=memn, mkv3=mkv3, proj3=proj3, q_off=q_off, y_main=y_main, y_mem=y_mem, y=y, x1=x1,
                  h2=h2, ug3=ug3, uv3=uv3, act=act, f=f)
        saved.append(sv)
        xs = x_next

    dxs, loss_vec = _loss_bwd(xs, loss_target.reshape(N, D))
    loss = lax.psum(jnp.sum(loss_vec), ("x", "y", "c"))

    G = {n: [None] * w_loc[n].shape[0] for n in REPL + SMALL_SHARDED if n != "g_kv"}
    GW = {}

    def dw(wname, layer, off, a, b_, nm):
        full = W[wname].shape if W[wname].ndim == 3 else (1,) + W[wname].shape
        GW[wname] = _mm(a, b_, ta=True, out_dtype=BF16, into=(GW.get(wname), full, layer, off), name=nm)

    kv_parts = []
    for l in reversed(range(depth)):
        sv = saved[l]
        proj3 = sv["proj3"]
        df, dg = _rms_bwd(sv["f"], vec(g_ffn_post[l]), dxs, out_dtype=BF16, name=f"rmsb_ffnpost_{l}")
        G["g_ffn_post"][l] = dg[0]
        dact = _mm(df, W["w_ffn_down"], bl=l, tb=True, name=f"mmb_down_dx_{l}")
        dw("w_ffn_down", l, (0, 0), sv["act"], df, f"mmb_down_dw_{l}")
        dug3, duv3, dwg, dwv, dbg, dbv = _ffn_mid_bwd(sv["ug3"], sv["uv3"], dact.reshape(Bl, T, F),
                                                      W["w_ffn_conv"][l], vec(b_ffn_conv[l]), name=f"ffn_mid_bwd_{l}")
        G["w_ffn_conv"][l] = jnp.concatenate([dwg, dwv], axis=1)
        G["b_ffn_conv"][l] = jnp.concatenate([dbg, dbv], axis=1)[0]
        dug, duv = dug3.reshape(N, F), duv3.reshape(N, F)
        dh2 = _mm(dug, W["w_ffn_up"], bl=l, tb=True, n=D, k=F, name=f"mmb_up_dx_g_{l}")
        dh2 = _mm(duv, W["w_ffn_up"], bl=l, tb=True, n=D, k=F, b_off=(0, F), add=dh2, name=f"mmb_up_dx_v_{l}")
        dw("w_ffn_up", l, (0, 0), sv["h2"], dug, f"mmb_up_dw_g_{l}")
        dw("w_ffn_up", l, (0, F), sv["h2"], duv, f"mmb_up_dw_v_{l}")
        dx1, dg = _rms_bwd(sv["x1"], vec(g_ffn_pre[l]), dh2, add=dxs, name=f"rmsb_ffnpre_{l}")
        G["g_ffn_pre"][l] = dg[0]
        dy, dg = _rms_bwd(sv["y"], vec(g_mix_post[l]), dx1, out_dtype=BF16, name=f"rmsb_mixpost_{l}")
        G["g_mix_post"][l] = dg[0]
        dy_main = _mm(dy, W["w_mix_out"], bl=l, tb=True, n=MIX_W, k=D, name=f"mmb_mixout_dmain_{l}")
        dy_mem = _mm(dy, W["w_mix_out"], bl=l, tb=True, n=MEM_W, k=D, b_off=(MIX_W, 0),
                     name=f"mmb_mixout_dmem_{l}")
        dw("w_mix_out", l, (0, 0), sv["y_main"], dy, f"mmb_mixout_dw_main_{l}")
        dw("w_mix_out", l, (MIX_W, 0), sv["y_mem"], dy, f"mmb_mixout_dw_mem_{l}")
        dq_mem3, dmkv3 = _mem_attn_bwd(proj3, sv["q_off"], sv["mkv3"], dy_mem.reshape(Bl, T, MEM_W),
                                       name=f"memattn_bwd_{l}")
        dq_mem = dq_mem3.reshape(N, MEM_W)
        dmkv = dmkv3.reshape(Bl * Ml, 2 * MEM_W)
        dw("w_mem_kv", l, (0, 0), sv["memn"], dmkv, f"mmb_memkv_dw_{l}")
        dmemn = _mm(dmkv, W["w_mem_kv"], bl=l, tb=True, name=f"mmb_memkv_dx_{l}")
        _, dg = _rms_bwd(mem2, vec(g_mem[l]), dmemn, name=f"rmsb_mem_{l}")
        G["g_mem"][l] = dg[0]
        dy_main3 = dy_main.reshape(Bl, T, MIX_W)
        if l < n_a:
            j = l
            dxc3, drp3, dip3, dugate3, dbr, dbi, dlam = _rglru_bwd(
                dy_main3, sv["xc3"], sv["gates3"], proj3, sv["hs3"], vec(b_rg_r[j]), vec(b_rg_i[j]),
                vec(W["lru_lambda"][j]), name=f"rglru_bwd_{l}")
            G["b_rg_r"][j] = dbr.reshape(nblk, bsz)
            G["b_rg_i"][j] = dbi.reshape(nblk, bsz)
            G["lru_lambda"][j] = dlam[0]
            drp, dip = drp3.reshape(N, MIX_W), dip3.reshape(N, MIX_W)
            xc2 = sv["xc3"].reshape(N, MIX_W)
            G["w_rg_r"][j] = _diag_blocks(_mm(xc2, drp, ta=True, name=f"mmb_gates_dw_r_{l}"), nblk, bsz)
            G["w_rg_i"][j] = _diag_blocks(_mm(xc2, dip, ta=True, name=f"mmb_gates_dw_i_{l}"), nblk, bsz)
            dxc = _mm(drp, wbd[j], tb=True, n=MIX_W, k=MIX_W, add=dxc3.reshape(N, MIX_W), name=f"mmb_gates_dx_r_{l}")
            dxc = _mm(dip, wbd[j], tb=True, n=MIX_W, k=MIX_W, b_off=(0, MIX_W), add=dxc, name=f"mmb_gates_dx_i_{l}")
            dux3, dwc, dbc = _conv_bwd_call(dxc.reshape(Bl, T, MIX_W), proj3, MIX_W, MIX_W, W["w_conv_a"][j],
                                            name=f"conv_a_bwd_{l}")
            G["w_conv_a"][j] = dwc
            G["b_conv_a"][j] = dbc[0]
            pieces = [(dugate3.reshape(N, MIX_W), 0), (dux3.reshape(N, MIX_W), MIX_W), (dq_mem, 2 * MIX_W)]
            gname = "w_in_a"
        else:
            j = l - n_a
            dq3, dkc, dkp, dsk = _swa_bwd(proj3, kv3, sinks_b[j], dy_main3, name=f"swa_bwd_{l}")
            kv_parts.append((dkc, dkp))
            G["sinks_b"][j] = dsk[0, :SWA_HEADS]
            pieces = [(dq3.reshape(N, MIX_W), 0), (dq_mem, MIX_W)]
            gname = "w_in_b"
        dh1 = None
        for pi, (piece, off) in enumerate(pieces):
            dh1 = _mm(piece, W[gname], bl=j, tb=True, n=D, k=piece.shape[1], b_off=(0, off), add=dh1,
                      name=f"mmb_in_dx_{pi}_{l}")
            dw(gname, j, (0, off), sv["h1"], piece, f"mmb_in_dw_{pi}_{l}")
        dxs, dg = _rms_bwd(sv["x0"], vec(g_mix_pre[l]), dh1, add=dx1, name=f"rmsb_mixpre_{l}")
        G["g_mix_pre"][l] = dg[0]
        if l == n_a:
            dkv = _kv_grad_combine(kv_parts, name="kv_grad_combine").reshape(N, 2 * MEM_W)
            dw("w_kv", 0, (0, 0), kvn, dkv, "mmb_kv_dw")
            dkvn = _mm(dkv, W["w_kv"], tb=True, name="mmb_kv_dx")
            dxs, dg = _rms_bwd(x_kv, vec(g_kv), dkvn, add=dxs, name="rmsb_kv")
            G["g_kv"] = dg[0]
    grad_x = dxs.reshape(Bl, T, D)
    Gf = {n: (jnp.stack(g) if isinstance(g, list) else g) for n, g in G.items()}

    g4 = {}
    for n in sh_names:
        if n in SMALL_SHARDED:
            t = _small_to_cols(Gf[n]).astype(BF16)
            g4[n] = t.reshape(1, N_DEV, -1, t.shape[-1])
        elif row_sharded[n]:
            L, R, C = GW[n].shape
            g4[n] = GW[n].reshape(L, N_DEV, R // N_DEV, C)
        else:
            t = _natural_to_cols(GW[n], name=f"shard_grad_{n}")
            g4[n] = t.reshape(1, N_DEV, -1, t.shape[-1])
    glist = [g4[n] for n in sh_names]
    got = _rs_sibling(glist, name="rs_sibling")
    psum4 = [_pair_sum(g, r, name=f"rs_pair_sum_{n}") for n, g, r in zip(sh_names, glist, got)]
    got2 = _rs_chips(psum4, name="rs_chips")
    r_blob = _pack([Gf[n].astype(F32) for n in REPL], REPL_ROWS, F32)
    r_parts = _all_gather(r_blob, name="ag_repl_grads")

    res = [{} for _ in range(4)]
    for n, p4, g2 in zip(sh_names, psum4, got2):
        shp3 = p4.shape[1:]
        outs = _adamw_sharded(p4, g2, w_loc[n].reshape(shp3), m_loc[n].reshape(shp3), v_loc[n].reshape(shp3),
                              name=f"adamw_{n}")
        for k in range(4):
            res[k][n] = outs[k].reshape(w_loc[n].shape)
    outs_rp = _adamw_replicated(r_parts, _pack([w_loc[n] for n in REPL], REPL_ROWS, F32),
                                _pack([m_loc[n] for n in REPL], REPL_ROWS, F32),
                                _pack([v_loc[n] for n in REPL], REPL_ROWS, F32),
                                name="adamw_replicated")
    rp_shapes = [w_loc[n].shape for n in REPL]
    for k in range(4):
        res[k].update(zip(REPL, _unpack(outs_rp[k], rp_shapes)))
    out = [loss, grad_x]
    for k in range(4):
        out += [res[k][n] for n in WEIGHTS]
    return tuple(out)
```

```python
import functools
import math

import numpy as np
import jax
import jax.numpy as jnp
from jax import lax
from jax.experimental import pallas as pl
from jax.experimental.pallas import tpu as pltpu

F32 = jnp.float32
BF16 = jnp.bfloat16
S = jax.ShapeDtypeStruct
MESH = pl.DeviceIdType.MESH
ANY = pl.BlockSpec(memory_space=pl.ANY)

HEAD = 64
MEM_HEADS = 4
MEM_W = MEM_HEADS * HEAD
SWA_HEADS = 12
SWA_GROUP = 3
MIX_W = SWA_HEADS * HEAD
WIN = 128
LRU_C = 8.0
EPS = 1e-6
ADAM_LR, ADAM_B1, ADAM_B2, ADAM_EPS, ADAM_WD, ADAM_STEP = 0.001, 0.9, 0.999, 1e-08, 0.01, 10
GELU_C0 = math.sqrt(2.0 / math.pi)
GELU_C1 = 0.044715
N_DEV = 8
LANES = 128
CT = 128
VMEM_LIMIT = 48 * 1024 * 1024
MM_VMEM_BUDGET = 36 * 1024 * 1024
REPL_ROWS = 256

SHARDED = (("w_mem_kv", 1), ("w_mix_out", 1), ("w_ffn_up", 2), ("w_ffn_conv", 2), ("w_ffn_down", 1), ("w_in_a", 2),
           ("w_conv_a", 2), ("b_conv_a", 1), ("lru_lambda", 1), ("w_in_b", 1), ("w_kv", 0))
SMALL_SHARDED = ("w_ffn_conv", "w_conv_a", "b_conv_a", "lru_lambda")
REPL = ("g_mix_pre", "g_mix_post", "g_ffn_pre", "g_ffn_post", "g_mem", "b_ffn_conv", "w_rg_r", "b_rg_r", "w_rg_i",
        "b_rg_i", "sinks_b", "g_kv")
WEIGHTS = ("g_mix_pre", "g_mix_post", "g_ffn_pre", "g_ffn_post", "g_mem", "w_mem_kv", "w_mix_out", "w_ffn_up",
           "w_ffn_conv", "b_ffn_conv", "w_ffn_down", "w_in_a", "w_conv_a", "b_conv_a", "w_rg_r", "b_rg_r", "w_rg_i",
           "b_rg_i", "lru_lambda", "w_in_b", "sinks_b", "g_kv", "w_kv")


def _alibi_slopes(n):
    def pow2(m):
        start = 2.0 ** (-8.0 / m)
        return [start ** (i + 1) for i in range(m)]
    c = 2 ** int(math.floor(math.log2(n)))
    s = pow2(c)
    if c != n:
        s = s + pow2(2 * c)[0::2][: n - c]
    return [float(v) for v in np.asarray(s, dtype=np.float32)]


SLOPES = _alibi_slopes(SWA_HEADS)


def _tile(n, cands):
    for c in cands:
        if n % c == 0:
            return c
    return n


def _cparams(*sem):
    return pltpu.CompilerParams(dimension_semantics=sem, vmem_limit_bytes=VMEM_LIMIT)


def _mm_tiles(M, N, K, a_bytes, b_bytes, o_bytes, add_bytes, offsets):
    m_off, n_offs, k_off = offsets
    tms = [c for c in (1024, 512, 256, 128) if M % c == 0 and m_off % c == 0] or [M]
    tns = [c for c in (1408, 1024, 896, 768, 512, 384, 256, 128)
           if N % c == 0 and all(o % c == 0 for o in n_offs)] or [N]
    tks = [c for c in (K, 2048, 1408, 1024, 512, 256, 128) if c <= K and K % c == 0 and k_off % c == 0]
    best = None
    for tk in tks:
        fits = []
        for tm in tms:
            for tn in tns:
                need = 2 * (tm * tk * a_bytes + tk * tn * b_bytes + tm * tn * (o_bytes + add_bytes))
                need += tm * tn * 4 * (2 if tk < K else 1)
                need += (tm * tk * 2 if a_bytes != 2 else 0) + (tk * tn * 2 if b_bytes != 2 else 0)
                if need <= MM_VMEM_BUDGET:
                    fits.append((tm * tn, min(tm, 512), tm, tn))
        if fits:
            _, _, tm, tn = max(fits)
            best = (tm, tn, tk)
            break
    assert best is not None, (M, N, K)
    return best


def _mm(a, b, *, ta=False, tb=False, n=None, k=None, b_off=(0, 0), out_dtype=F32, add=None, into=None, name="mm"):
    if ta:
        K, M = a.shape
    else:
        M, K = a.shape
    if tb:
        N = b.shape[-2] if n is None else n
    else:
        N = b.shape[-1] if n is None else n
    assert k is None or k == K
    ro, co = b_off
    n_off, k_off = (ro, co) if tb else (co, ro)
    oro, oco = (0, 0) if into is None else into[3]
    tm, tn, tk = _mm_tiles(M, N, K, a.dtype.itemsize, b.dtype.itemsize, jnp.dtype(out_dtype).itemsize,
                           0 if add is None else add.dtype.itemsize, (oro, (n_off, oco), k_off))
    nk = K // tk
    if tb:
        b_spec = pl.BlockSpec((tn, tk), lambda i, j, kk: (j + ro // tn, kk + co // tk))
        b_dims = (1,)
    else:
        b_spec = pl.BlockSpec((tk, tn), lambda i, j, kk: (kk + ro // tk, j + co // tn))
        b_dims = (0,)
    if ta:
        a_spec = pl.BlockSpec((tk, tm), lambda i, j, kk: (kk, i))
        a_dims = (0,)
    else:
        a_spec = pl.BlockSpec((tm, tk), lambda i, j, kk: (i, kk))
        a_dims = (1,)
    dims = ((a_dims, b_dims), ((), ()))
    add_spec = pl.BlockSpec((tm, tn), lambda i, j, kk: (i, j))
    has_add = add is not None
    if into is None:
        o_spec, o_shape, buf = add_spec, (M, N), None
    else:
        buf, o_shape, ol, _ = into
        assert not has_add
        o_spec = pl.BlockSpec((None, tm, tn), lambda i, j, kk: (ol, i + oro // tm, j + oco // tn))
    has_buf = buf is not None

    def body(*refs):
        refs = list(refs)
        acc_ref = refs.pop() if nk > 1 else None
        o_ref = refs.pop()
        a_ref, b_ref = refs[0], refs[1]
        add_ref = refs[2] if has_add else None
        part = lax.dot_general(a_ref[...].astype(BF16), b_ref[...].astype(BF16), dims, preferred_element_type=F32)

        def finish(r):
            if has_add:
                r = r + add_ref[...].astype(F32)
            o_ref[...] = r.astype(out_dtype)

        if nk == 1:
            finish(part)
        else:
            kk = pl.program_id(2)

            @pl.when(kk == 0)
            def _():
                acc_ref[...] = part

            @pl.when(kk > 0)
            def _():
                acc_ref[...] += part

            @pl.when(kk == nk - 1)
            def _():
                finish(acc_ref[...])

    in_specs = [a_spec, b_spec] + ([add_spec] if has_add else []) + ([ANY] if has_buf else [])
    args = (a, b) + ((add,) if has_add else ()) + ((buf,) if has_buf else ())
    return pl.pallas_call(
        body, grid=(M // tm, N // tn, nk), in_specs=in_specs, out_specs=o_spec,
        out_shape=S(o_shape, out_dtype), scratch_shapes=[pltpu.VMEM((tm, tn), F32)] if nk > 1 else [],
        input_output_aliases={2: 0} if has_buf else {},
        compiler_params=_cparams("parallel", "parallel", "arbitrary"), name=name)(*args)


def _rms_fwd(x, g, out_dtype, res=None, name="rms_fwd"):
    N, D = x.shape
    tm = _tile(N, (512, 256, 128))
    has_res = res is not None

    def body(*refs):
        if has_res:
            x_ref, g_ref, r_ref, o_ref = refs
        else:
            x_ref, g_ref, o_ref = refs
        xv = x_ref[...].astype(F32)
        y = xv * lax.rsqrt(jnp.mean(xv * xv, axis=-1, keepdims=True) + EPS) * g_ref[...]
        if has_res:
            y = y + r_ref[...]
        o_ref[...] = y.astype(out_dtype)

    row = pl.BlockSpec((tm, D), lambda i: (i, 0))
    vec = pl.BlockSpec((1, D), lambda i: (0, 0))
    return pl.pallas_call(
        body, grid=(N // tm,), in_specs=[row, vec] + ([row] if has_res else []), out_specs=row,
        out_shape=S((N, D), out_dtype), compiler_params=_cparams("parallel"), name=name)(
            *((x, g) + ((res,) if has_res else ())))


def _rms_bwd(x, g, dy, add=None, out_dtype=F32, name="rms_bwd"):
    N, D = x.shape
    tm = _tile(N, (512, 256, 128))
    has_add = add is not None

    def body(*refs):
        if has_add:
            x_ref, g_ref, dy_ref, add_ref, dx_ref, dg_ref = refs
        else:
            x_ref, g_ref, dy_ref, dx_ref, dg_ref = refs
        xv = x_ref[...].astype(F32)
        dyv = dy_ref[...].astype(F32)
        r = lax.rsqrt(jnp.mean(xv * xv, axis=-1, keepdims=True) + EPS)
        u = dyv * g_ref[...]
        dx = r * u - xv * (r * r * r * jnp.mean(u * xv, axis=-1, keepdims=True))
        if has_add:
            dx = dx + add_ref[...]
        dx_ref[...] = dx.astype(out_dtype)

        @pl.when(pl.program_id(0) == 0)
        def _():
            dg_ref[...] = jnp.zeros_like(dg_ref)

        dg_ref[...] += jnp.sum(dyv * xv * r, axis=0, keepdims=True)

    row = pl.BlockSpec((tm, D), lambda i: (i, 0))
    vec = pl.BlockSpec((1, D), lambda i: (0, 0))
    return pl.pallas_call(
        body, grid=(N // tm,), in_specs=[row, vec, row] + ([row] if has_add else []), out_specs=(row, vec),
        out_shape=(S((N, D), out_dtype), S((1, D), F32)), compiler_params=_cparams("arbitrary"), name=name)(
            *((x, g, dy) + ((add,) if has_add else ())))


def _shift_down(x, s, row):
    return jnp.where(row >= s, pltpu.roll(x, s, axis=0), 0.0)


def _shift_up(x, s, row):
    T = x.shape[0]
    return jnp.where(row < T - s, pltpu.roll(x, T - s, axis=0), 0.0)


def _conv(x, w_ref, b_ref, row):
    W = w_ref.shape[0]
    y = x * w_ref[W - 1:W, :] + b_ref[...]
    for s in range(1, W):
        y = y + _shift_down(x, s, row) * w_ref[W - 1 - s:W - s, :]
    return y


def _conv_bwd(dy, x, w_ref, row):
    W = w_ref.shape[0]
    dx = dy * w_ref[W - 1:W, :]
    dws = [None] * W
    dws[W - 1] = jnp.sum(dy * x, axis=0, keepdims=True)
    for s in range(1, W):
        dx = dx + _shift_up(dy, s, row) * w_ref[W - 1 - s:W - s, :]
        dws[W - 1 - s] = jnp.sum(dy * _shift_down(x, s, row), axis=0, keepdims=True)
    return dx, jnp.concatenate(dws, axis=0), jnp.sum(dy, axis=0, keepdims=True)


def _gelu(g):
    t = jnp.tanh(GELU_C0 * (g + GELU_C1 * g * g * g))
    return 0.5 * g * (1.0 + t), t


def _dgelu(g, t):
    return 0.5 * (1.0 + t) + 0.5 * g * (1.0 - t * t) * (GELU_C0 * (1.0 + 3.0 * GELU_C1 * g * g))


def _cspec(T, off=0):
    return pl.BlockSpec((1, T, CT), lambda j, b: (b, 0, j + off))


def _pspec(rows, off=0):
    return pl.BlockSpec((rows, CT), lambda j, b: (0, j + off))


def _conv_fwd_call(x3, x_off, C, w, b, name):
    Bl, T, _ = x3.shape
    W = w.shape[0]

    def body(x_ref, w_ref, b_ref, o_ref):
        row = lax.broadcasted_iota(jnp.int32, (T, CT), 0)
        o_ref[0] = _conv(x_ref[0], w_ref, b_ref, row)

    return pl.pallas_call(
        body, grid=(C // CT, Bl), in_specs=[_cspec(T, x_off // CT), _pspec(W), _pspec(1)], out_specs=_cspec(T),
        out_shape=S((Bl, T, C), F32), compiler_params=_cparams("parallel", "arbitrary"), name=name)(x3, w, b)


def _conv_bwd_call(dy3, x3, x_off, C, w, name):
    Bl, T, _ = x3.shape
    W = w.shape[0]

    def body(dy_ref, x_ref, w_ref, dx_ref, dw_ref, db_ref):
        row = lax.broadcasted_iota(jnp.int32, (T, CT), 0)
        dx, dw, db = _conv_bwd(dy_ref[0], x_ref[0], w_ref, row)
        dx_ref[0] = dx.astype(BF16)

        @pl.when(pl.program_id(1) == 0)
        def _():
            dw_ref[...] = jnp.zeros_like(dw_ref)
            db_ref[...] = jnp.zeros_like(db_ref)

        dw_ref[...] += dw
        db_ref[...] += db

    return pl.pallas_call(
        body, grid=(C // CT, Bl), in_specs=[_cspec(T), _cspec(T, x_off // CT), _pspec(W)],
        out_specs=(_cspec(T), _pspec(W), _pspec(1)),
        out_shape=(S((Bl, T, C), BF16), S((W, C), F32), S((1, C), F32)),
        compiler_params=_cparams("parallel", "arbitrary"), name=name)(dy3, x3, w)


def _ffn_mid_fwd(ug3, uv3, wc, bc, name):
    Bl, T, F = ug3.shape
    nf = F // CT

    def body(ug_ref, uv_ref, wg_ref, wv_ref, bg_ref, bv_ref, o_ref):
        row = lax.broadcasted_iota(jnp.int32, (T, CT), 0)
        g = _conv(ug_ref[0], wg_ref, bg_ref, row)
        v = _conv(uv_ref[0], wv_ref, bv_ref, row)
        o_ref[0] = (_gelu(g)[0] * v).astype(BF16)

    return pl.pallas_call(
        body, grid=(nf, Bl),
        in_specs=[_cspec(T), _cspec(T), _pspec(3), _pspec(3, nf), _pspec(1), _pspec(1, nf)], out_specs=_cspec(T),
        out_shape=S((Bl, T, F), BF16), compiler_params=_cparams("parallel", "arbitrary"), name=name)(
            ug3, uv3, wc, wc, bc, bc)


def _ffn_mid_bwd(ug3, uv3, dact3, wc, bc, name):
    Bl, T, F = ug3.shape
    nf = F // CT

    def body(ug_ref, uv_ref, da_ref, wg_ref, wv_ref, bg_ref, bv_ref, dug_ref, duv_ref, dwg_ref, dwv_ref, dbg_ref,
             dbv_ref):
        row = lax.broadcasted_iota(jnp.int32, (T, CT), 0)
        ug = ug_ref[0]
        uv = uv_ref[0]
        g = _conv(ug, wg_ref, bg_ref, row)
        v = _conv(uv, wv_ref, bv_ref, row)
        da = da_ref[0]
        gel, t = _gelu(g)
        dg = da * v * _dgelu(g, t)
        dv = da * gel
        dug, dwg, dbg = _conv_bwd(dg, ug, wg_ref, row)
        duv, dwv, dbv = _conv_bwd(dv, uv, wv_ref, row)
        dug_ref[0] = dug.astype(BF16)
        duv_ref[0] = duv.astype(BF16)

        @pl.when(pl.program_id(1) == 0)
        def _():
            dwg_ref[...] = jnp.zeros_like(dwg_ref)
            dwv_ref[...] = jnp.zeros_like(dwv_ref)
            dbg_ref[...] = jnp.zeros_like(dbg_ref)
            dbv_ref[...] = jnp.zeros_like(dbv_ref)

        dwg_ref[...] += dwg
        dwv_ref[...] += dwv
        dbg_ref[...] += dbg
        dbv_ref[...] += dbv

    return pl.pallas_call(
        body, grid=(nf, Bl),
        in_specs=[_cspec(T), _cspec(T), _cspec(T), _pspec(3), _pspec(3, nf), _pspec(1), _pspec(1, nf)],
        out_specs=(_cspec(T), _cspec(T), _pspec(3), _pspec(3), _pspec(1), _pspec(1)),
        out_shape=(S((Bl, T, F), BF16), S((Bl, T, F), BF16), S((3, F), F32), S((3, F), F32), S((1, F), F32),
                   S((1, F), F32)),
        compiler_params=_cparams("parallel", "arbitrary"), name=name)(ug3, uv3, dact3, wc, wc, bc, bc)


def _lru_gates(xc, rp, ip, br_ref, bi_ref, lam_ref):
    r = jax.nn.sigmoid(rp + br_ref[...])
    i = jax.nn.sigmoid(ip + bi_ref[...])
    lam = lam_ref[...]
    sp = jnp.maximum(-lam, 0.0) + jnp.log1p(jnp.exp(-jnp.abs(lam)))
    log_a = (-LRU_C) * r * sp
    a = jnp.exp(log_a)
    z = 2.0 * log_a
    one_m_a2 = jnp.where(z > -0.05, -z * (1.0 + z * (0.5 + z * (1.0 / 6.0 + z * (1.0 / 24.0)))), 1.0 - a * a)
    mult = jnp.sqrt(one_m_a2)
    return r, i, sp, a, mult


def _rglru_fwd(xc3, gates3, proj3, br, bi, lam, name):
    Bl, T, C = xc3.shape
    nsteps = int(math.log2(T))
    assert 1 << nsteps == T

    def body(xc_ref, rp_ref, ip_ref, ug_ref, br_ref, bi_ref, lam_ref, y_ref, h_ref):
        row = lax.broadcasted_iota(jnp.int32, (T, CT), 0)
        xc = xc_ref[0]
        r, i, sp, a, mult = _lru_gates(xc, rp_ref[0], ip_ref[0], br_ref, bi_ref, lam_ref)
        b = mult * (i * xc)
        for st in range(nsteps):
            s = 1 << st
            a_sh = jnp.where(row >= s, pltpu.roll(a, s, axis=0), 1.0)
            b = a * _shift_down(b, s, row) + b
            a = a * a_sh
        h_ref[0] = b
        y_ref[0] = (b * _gelu(ug_ref[0])[0]).astype(BF16)

    return pl.pallas_call(
        body, grid=(C // CT, Bl),
        in_specs=[_cspec(T), _cspec(T), _cspec(T, C // CT), _cspec(T), _pspec(1), _pspec(1), _pspec(1)],
        out_specs=(_cspec(T), _cspec(T)), out_shape=(S((Bl, T, C), BF16), S((Bl, T, C), F32)),
        compiler_params=_cparams("parallel", "arbitrary"), name=name)(xc3, gates3, gates3, proj3, br, bi, lam)


def _rglru_bwd(dy3, xc3, gates3, proj3, h3, br, bi, lam, name):
    Bl, T, C = xc3.shape
    nsteps = int(math.log2(T))

    def body(dy_ref, xc_ref, rp_ref, ip_ref, ug_ref, h_ref, br_ref, bi_ref, lam_ref,
             dxc_ref, drp_ref, dip_ref, dug_ref, dbr_ref, dbi_ref, dlam_ref):
        row = lax.broadcasted_iota(jnp.int32, (T, CT), 0)
        xc = xc_ref[0]
        r, i, sp, a, mult = _lru_gates(xc, rp_ref[0], ip_ref[0], br_ref, bi_ref, lam_ref)
        h = h_ref[0]
        dy = dy_ref[0]
        ug = ug_ref[0]
        gel, t = _gelu(ug)
        dug_ref[0] = (dy * h * _dgelu(ug, t)).astype(BF16)
        gacc = dy * gel
        an = _shift_up(a, 1, row)
        for st in range(nsteps):
            s = 1 << st
            an_sh = jnp.where(row < T - s, pltpu.roll(an, T - s, axis=0), 1.0)
            gacc = an * _shift_up(gacc, s, row) + gacc
            an = an * an_sh
        da = gacc * _shift_down(h, 1, row)
        ix = i * xc
        d_mult = gacc * ix
        d_i = gacc * mult * xc
        dxc_ref[0] = gacc * mult * i
        d_log_a = da * a - d_mult * (a * a) / mult
        d_r = d_log_a * ((-LRU_C) * sp)
        d_sp = jnp.sum(d_log_a * ((-LRU_C) * r), axis=0, keepdims=True)
        drp = d_r * r * (1.0 - r)
        dip = d_i * i * (1.0 - i)
        drp_ref[0] = drp.astype(BF16)
        dip_ref[0] = dip.astype(BF16)

        @pl.when(pl.program_id(1) == 0)
        def _():
            dbr_ref[...] = jnp.zeros_like(dbr_ref)
            dbi_ref[...] = jnp.zeros_like(dbi_ref)
            dlam_ref[...] = jnp.zeros_like(dlam_ref)

        dbr_ref[...] += jnp.sum(drp, axis=0, keepdims=True)
        dbi_ref[...] += jnp.sum(dip, axis=0, keepdims=True)
        dlam_ref[...] += d_sp * (-jax.nn.sigmoid(-lam_ref[...]))

    vec = S((1, C), F32)
    act = S((Bl, T, C), BF16)
    return pl.pallas_call(
        body, grid=(C // CT, Bl),
        in_specs=[_cspec(T), _cspec(T), _cspec(T), _cspec(T, C // CT), _cspec(T), _cspec(T)] + [_pspec(1)] * 3,
        out_specs=(_cspec(T), _cspec(T), _cspec(T), _cspec(T), _pspec(1), _pspec(1), _pspec(1)),
        out_shape=(S((Bl, T, C), F32), act, act, act, vec, vec, vec),
        compiler_params=_cparams("parallel", "arbitrary"), name=name)(dy3, xc3, gates3, gates3, proj3, h3, br, bi, lam)


NT = (((1,), (1,)), ((), ()))
TN = (((0,), (0,)), ((), ()))


def _hs(h):
    return slice(h * HEAD, (h + 1) * HEAD)


def _mem_softmax(qb, kb):
    s = lax.dot_general(qb, kb, NT, preferred_element_type=F32) * (HEAD ** -0.5)
    e = jnp.exp(s - jnp.max(s, axis=-1, keepdims=True))
    return e / jnp.sum(e, axis=-1, keepdims=True)


def _mem_attn_fwd(proj3, q_off, mkv3, name):
    Bl, T, _ = proj3.shape
    M = mkv3.shape[1]
    tq = _tile(T, (512, 256, 128))

    def body(q_ref, k_ref, v_ref, o_ref):
        q = q_ref[0].astype(BF16)
        k = k_ref[0].astype(BF16)
        v = v_ref[0].astype(BF16)
        outs = []
        for h in range(MEM_HEADS):
            p = _mem_softmax(q[:, _hs(h)], k[:, _hs(h)])
            outs.append(jnp.dot(p.astype(BF16), v[:, _hs(h)], preferred_element_type=F32))
        o_ref[0] = jnp.concatenate(outs, axis=-1).astype(BF16)

    return pl.pallas_call(
        body, grid=(Bl, T // tq),
        in_specs=[pl.BlockSpec((1, tq, MEM_W), lambda b, t: (b, t, q_off // MEM_W)),
                  pl.BlockSpec((1, M, MEM_W), lambda b, t: (b, 0, 0)),
                  pl.BlockSpec((1, M, MEM_W), lambda b, t: (b, 0, 1))],
        out_specs=pl.BlockSpec((1, tq, MEM_W), lambda b, t: (b, t, 0)),
        out_shape=S((Bl, T, MEM_W), BF16), compiler_params=_cparams("parallel", "parallel"), name=name)(
            proj3, mkv3, mkv3)


def _mem_attn_bwd(proj3, q_off, mkv3, do3, name):
    Bl, T, _ = proj3.shape
    M = mkv3.shape[1]
    tq = _tile(T, (512, 256, 128))
    scale = HEAD ** -0.5

    def body(q_ref, k_ref, v_ref, do_ref, dq_ref, dkv_ref):
        q = q_ref[0].astype(BF16)
        k = k_ref[0].astype(BF16)
        v = v_ref[0].astype(BF16)
        do = do_ref[0].astype(BF16)
        dqs, dks, dvs = [], [], []
        for h in range(MEM_HEADS):
            qh, kh, vh, doh = q[:, _hs(h)], k[:, _hs(h)], v[:, _hs(h)], do[:, _hs(h)]
            p = _mem_softmax(qh, kh)
            dvs.append(lax.dot_general(p.astype(BF16), doh, TN, preferred_element_type=F32))
            dp = lax.dot_general(doh, vh, NT, preferred_element_type=F32)
            ds = (p * (dp - jnp.sum(p * dp, axis=-1, keepdims=True)) * scale).astype(BF16)
            dqs.append(jnp.dot(ds, kh, preferred_element_type=F32))
            dks.append(lax.dot_general(ds, qh, TN, preferred_element_type=F32))
        dq_ref[0] = jnp.concatenate(dqs, axis=-1).astype(BF16)

        @pl.when(pl.program_id(1) == 0)
        def _():
            dkv_ref[...] = jnp.zeros_like(dkv_ref)

        dkv_ref[0] += jnp.concatenate(dks + dvs, axis=-1)

    return pl.pallas_call(
        body, grid=(Bl, T // tq),
        in_specs=[pl.BlockSpec((1, tq, MEM_W), lambda b, t: (b, t, q_off // MEM_W)),
                  pl.BlockSpec((1, M, MEM_W), lambda b, t: (b, 0, 0)),
                  pl.BlockSpec((1, M, MEM_W), lambda b, t: (b, 0, 1)),
                  pl.BlockSpec((1, tq, MEM_W), lambda b, t: (b, t, 0))],
        out_specs=(pl.BlockSpec((1, tq, MEM_W), lambda b, t: (b, t, 0)),
                   pl.BlockSpec((1, M, 2 * MEM_W), lambda b, t: (b, 0, 0))),
        out_shape=(S((Bl, T, MEM_W), BF16), S((Bl, M, 2 * MEM_W), F32)),
        compiler_params=_cparams("parallel", "arbitrary"), name=name)(proj3, mkv3, mkv3, do3)


def _swa_probs(qh, kph, kch, sink, slope, has_prev):
    qi = lax.broadcasted_iota(jnp.int32, (WIN, WIN), 0)
    kj = lax.broadcasted_iota(jnp.int32, (WIN, WIN), 1)
    scale = HEAD ** -0.5
    sp = lax.dot_general(qh, kph, NT, preferred_element_type=F32) * scale
    sc = lax.dot_general(qh, kch, NT, preferred_element_type=F32) * scale
    dist_p = (qi + WIN - kj).astype(F32)
    dist_c = (qi - kj).astype(F32)
    neg = -jnp.inf
    sp = jnp.where(kj > qi + jnp.where(has_prev, 0, WIN), sp - slope * dist_p, neg)
    sc = jnp.where(kj <= qi, sc - slope * dist_c, neg)
    m = jnp.maximum(jnp.maximum(jnp.max(sp, axis=-1, keepdims=True), jnp.max(sc, axis=-1, keepdims=True)), sink)
    ep = jnp.exp(sp - m)
    ec = jnp.exp(sc - m)
    es = jnp.exp(sink - m)
    inv = 1.0 / (jnp.sum(ep, axis=-1, keepdims=True) + jnp.sum(ec, axis=-1, keepdims=True) + es)
    return ep * inv, ec * inv, es * inv


def _swa_specs(nb):
    prev = lambda n: jnp.maximum(n - 1, 0)
    q = pl.BlockSpec((1, WIN, MIX_W), lambda b, n: (b, n, 0))
    kp = pl.BlockSpec((1, WIN, MEM_W), lambda b, n: (b, prev(n), 0))
    kc = pl.BlockSpec((1, WIN, MEM_W), lambda b, n: (b, n, 0))
    vp = pl.BlockSpec((1, WIN, MEM_W), lambda b, n: (b, prev(n), 1))
    vc = pl.BlockSpec((1, WIN, MEM_W), lambda b, n: (b, n, 1))
    sm = pl.BlockSpec(memory_space=pltpu.SMEM)
    return q, kp, kc, vp, vc, sm


def _swa_fwd(proj3, kv3, sinks, name):
    Bl, T, _ = proj3.shape
    nb = T // WIN
    q_s, kp_s, kc_s, vp_s, vc_s, sm = _swa_specs(nb)

    def body(q_ref, kp_ref, kc_ref, vp_ref, vc_ref, sink_ref, o_ref):
        has_prev = pl.program_id(1) > 0
        q = q_ref[0].astype(BF16)
        kp, kc = kp_ref[0].astype(BF16), kc_ref[0].astype(BF16)
        vp, vc = vp_ref[0].astype(BF16), vc_ref[0].astype(BF16)
        outs = []
        for h in range(SWA_HEADS):
            kvs = _hs(h // SWA_GROUP)
            pp, pc, _ = _swa_probs(q[:, _hs(h)], kp[:, kvs], kc[:, kvs], sink_ref[h], SLOPES[h], has_prev)
            outs.append(jnp.dot(pp.astype(BF16), vp[:, kvs], preferred_element_type=F32)
                        + jnp.dot(pc.astype(BF16), vc[:, kvs], preferred_element_type=F32))
        o_ref[0] = jnp.concatenate(outs, axis=-1).astype(BF16)

    return pl.pallas_call(
        body, grid=(Bl, nb), in_specs=[q_s, kp_s, kc_s, vp_s, vc_s, sm], out_specs=q_s,
        out_shape=S((Bl, T, MIX_W), BF16), compiler_params=_cparams("parallel", "parallel"), name=name)(
            proj3, kv3, kv3, kv3, kv3, sinks)


def _swa_bwd(proj3, kv3, sinks, do3, name):
    Bl, T, _ = proj3.shape
    nb = T // WIN
    q_s, kp_s, kc_s, vp_s, vc_s, sm = _swa_specs(nb)
    kv_s = pl.BlockSpec((1, WIN, 2 * MEM_W), lambda b, n: (b, n, 0))
    sk_s = pl.BlockSpec((8, LANES), lambda b, n: (0, 0))
    scale = HEAD ** -0.5

    def body(q_ref, kp_ref, kc_ref, vp_ref, vc_ref, sink_ref, do_ref, dq_ref, dkc_ref, dkp_ref, dsk_ref):
        has_prev = pl.program_id(1) > 0
        q = q_ref[0].astype(BF16)
        kp, kc = kp_ref[0].astype(BF16), kc_ref[0].astype(BF16)
        vp, vc = vp_ref[0].astype(BF16), vc_ref[0].astype(BF16)
        do = do_ref[0].astype(BF16)
        lane = lax.broadcasted_iota(jnp.int32, (8, LANES), 1)
        srow = lax.broadcasted_iota(jnp.int32, (8, LANES), 0)
        dsk = jnp.zeros((8, LANES), F32)
        dqs = []
        nkv = SWA_HEADS // SWA_GROUP
        dkc, dkp, dvc, dvp = [None] * nkv, [None] * nkv, [None] * nkv, [None] * nkv
        for h in range(SWA_HEADS):
            kvh = h // SWA_GROUP
            kvs = _hs(kvh)
            qh, doh = q[:, _hs(h)], do[:, _hs(h)]
            pp, pc, ps = _swa_probs(qh, kp[:, kvs], kc[:, kvs], sink_ref[h], SLOPES[h], has_prev)
            dpp = lax.dot_general(doh, vp[:, kvs], NT, preferred_element_type=F32)
            dpc = lax.dot_general(doh, vc[:, kvs], NT, preferred_element_type=F32)
            delta = jnp.sum(pp * dpp, axis=-1, keepdims=True) + jnp.sum(pc * dpc, axis=-1, keepdims=True)
            dsp = (pp * (dpp - delta) * scale).astype(BF16)
            dsc = (pc * (dpc - delta) * scale).astype(BF16)
            dqs.append(jnp.dot(dsp, kp[:, kvs], preferred_element_type=F32)
                       + jnp.dot(dsc, kc[:, kvs], preferred_element_type=F32))
            parts = (lax.dot_general(dsc, qh, TN, preferred_element_type=F32),
                     lax.dot_general(dsp, qh, TN, preferred_element_type=F32),
                     lax.dot_general(pc.astype(BF16), doh, TN, preferred_element_type=F32),
                     lax.dot_general(pp.astype(BF16), doh, TN, preferred_element_type=F32))
            for acc, part in zip((dkc, dkp, dvc, dvp), parts):
                acc[kvh] = part if acc[kvh] is None else acc[kvh] + part
            dsk = dsk + jnp.where((lane == h) & (srow == 0), -jnp.sum(ps * delta), 0.0)
        dq_ref[0] = jnp.concatenate(dqs, axis=-1).astype(BF16)
        dkc_ref[0] = jnp.concatenate(dkc + dvc, axis=-1)
        dkp_ref[0] = jnp.concatenate(dkp + dvp, axis=-1)

        @pl.when((pl.program_id(0) == 0) & (pl.program_id(1) == 0))
        def _():
            dsk_ref[...] = jnp.zeros_like(dsk_ref)

        dsk_ref[...] += dsk

    return pl.pallas_call(
        body, grid=(Bl, nb), in_specs=[q_s, kp_s, kc_s, vp_s, vc_s, sm, q_s], out_specs=(q_s, kv_s, kv_s, sk_s),
        out_shape=(S((Bl, T, MIX_W), BF16), S((Bl, T, 2 * MEM_W), F32), S((Bl, T, 2 * MEM_W), F32), S((8, LANES), F32)),
        compiler_params=_cparams("arbitrary", "arbitrary"), name=name)(proj3, kv3, kv3, kv3, kv3, sinks, do3)


def _kv_grad_combine(parts, name):
    Bl, T, W = parts[0][0].shape
    nb = T // WIN
    nl = len(parts)

    def body(*refs):
        o_ref = refs[-1]
        has_next = jnp.where(pl.program_id(1) == nb - 1, 0.0, 1.0)
        acc = None
        for l in range(nl):
            c = refs[2 * l][0] + has_next * refs[2 * l + 1][0]
            acc = c if acc is None else acc + c
        o_ref[0] = acc.astype(BF16)

    cur = pl.BlockSpec((1, WIN, W), lambda b, n: (b, n, 0))
    nxt = pl.BlockSpec((1, WIN, W), lambda b, n: (b, jnp.minimum(n + 1, nb - 1), 0))
    return pl.pallas_call(
        body, grid=(Bl, nb), in_specs=[cur, nxt] * nl, out_specs=cur, out_shape=S((Bl, T, W), BF16),
        compiler_params=_cparams("parallel", "parallel"), name=name)(*[a for pr in parts for a in pr])


def _loss_bwd(y, target, name="loss"):
    N, D = y.shape
    tm = _tile(N, (512, 256, 128))

    def body(y_ref, t_ref, dy_ref, l_ref):
        e = y_ref[...] - t_ref[...]
        dy_ref[...] = e * (1.0 / D)

        @pl.when(pl.program_id(0) == 0)
        def _():
            l_ref[...] = jnp.zeros_like(l_ref)

        l_ref[...] += jnp.sum(e * e, axis=0, keepdims=True) * (0.5 / D)

    row = pl.BlockSpec((tm, D), lambda i: (i, 0))
    vec = pl.BlockSpec((1, D), lambda i: (0, 0))
    return pl.pallas_call(
        body, grid=(N // tm,), in_specs=[row, row], out_specs=(row, vec), out_shape=(S((N, D), F32), S((1, D), F32)),
        compiler_params=_cparams("arbitrary"), name=name)(y, target)


def _all_gather(x, name):
    R, C = x.shape

    def body(x_ref, out_ref, send_sems, recv_sems, local_sem):
        mx, my, mc = lax.axis_index("x"), lax.axis_index("y"), lax.axis_index("c")
        me, sibling = (mx, my, mc), (mx, my, 1 - mc)
        chips = [(1 - mx, my), (mx, 1 - my), (1 - mx, 1 - my)]

        def rows(px, py, pc):
            return out_ref.at[4 * px + 2 * py + pc]

        def copy(kk, block, to, src=None):
            return pltpu.make_async_remote_copy(
                src_ref=rows(*block) if src is None else src, dst_ref=rows(*block), send_sem=send_sems.at[kk],
                recv_sem=recv_sems.at[kk], device_id=to, device_id_type=MESH)

        mine = pltpu.make_async_copy(x_ref, rows(*me), local_sem)
        mine.start()
        first = [copy(0, me, sibling, src=x_ref)]
        first += [copy(1 + j, me, (*chip, mc), src=x_ref) for j, chip in enumerate(chips)]
        for cp in first:
            cp.start()
        passed = [copy(4 + j, (*chip, mc), sibling) for j, chip in enumerate(chips)]
        for j, chip in enumerate(chips):
            copy(1 + j, (*chip, mc), me).wait_recv()
            passed[j].start()
        copy(0, sibling, me).wait_recv()
        for j, chip in enumerate(chips):
            copy(4 + j, (*chip, 1 - mc), me).wait_recv()
        for cp in first + passed:
            cp.wait_send()
        mine.wait()

    return pl.pallas_call(
        body, out_shape=S((N_DEV, R, C), x.dtype), in_specs=[ANY], out_specs=ANY,
        scratch_shapes=[pltpu.SemaphoreType.DMA((7,)), pltpu.SemaphoreType.DMA((7,)), pltpu.SemaphoreType.DMA(())],
        name=name)(x)


def _ag_weights(shards, row_sharded, name):
    n = len(shards)

    def full_shape(a, rows):
        if rows:
            return a.shape[:-2] + (N_DEV * a.shape[-2],) + a.shape[-1:]
        return (N_DEV,) + a.shape

    def body(*refs):
        x_refs, o_refs = refs[:n], refs[n:2 * n]
        send_sems, recv_sems, local_sems = refs[2 * n:]
        mx, my, mc = lax.axis_index("x"), lax.axis_index("y"), lax.axis_index("c")
        me, sibling = (mx, my, mc), (mx, my, 1 - mc)
        chips = [(1 - mx, my), (mx, 1 - my), (1 - mx, 1 - my)]

        def dst(t, px, py, pc):
            d = 4 * px + 2 * py + pc
            if not row_sharded[t]:
                return o_refs[t].at[d]
            r = shards[t].shape[-2]
            idx = (slice(None),) * (shards[t].ndim - 2) + (pl.ds(pl.multiple_of(d * r, 16), r), slice(None))
            return o_refs[t].at[idx]

        def copy(kk, t, block, to, src=None):
            return pltpu.make_async_remote_copy(
                src_ref=dst(t, *block) if src is None else src, dst_ref=dst(t, *block),
                send_sem=send_sems.at[kk * n + t], recv_sem=recv_sems.at[kk * n + t], device_id=to,
                device_id_type=MESH)

        mine = [pltpu.make_async_copy(x_refs[t], dst(t, *me), local_sems.at[t]) for t in range(n)]
        for cp in mine:
            cp.start()
        first = []
        for t in range(n):
            first.append(copy(0, t, me, sibling, src=x_refs[t]))
            first += [copy(1 + j, t, me, (*chip, mc), src=x_refs[t]) for j, chip in enumerate(chips)]
        for cp in first:
            cp.start()
        passed = []
        for j, chip in enumerate(chips):
            for t in range(n):
                copy(1 + j, t, (*chip, mc), me).wait_recv()
                cp = copy(4 + j, t, (*chip, mc), sibling)
                cp.start()
                passed.append(cp)
        for t in range(n):
            copy(0, t, sibling, me).wait_recv()
            for j, chip in enumerate(chips):
                copy(4 + j, t, (*chip, 1 - mc), me).wait_recv()
        for cp in first + passed:
            cp.wait_send()
        for cp in mine:
            cp.wait()

    return pl.pallas_call(
        body, out_shape=tuple(S(full_shape(a, r), a.dtype) for a, r in zip(shards, row_sharded)),
        in_specs=[ANY] * n, out_specs=tuple([ANY] * n),
        scratch_shapes=[pltpu.SemaphoreType.DMA((7 * n,)), pltpu.SemaphoreType.DMA((7 * n,)),
                        pltpu.SemaphoreType.DMA((n,))],
        name=name)(*shards)


def _rs_sibling(gs, name):
    n = len(gs)

    def body(*refs):
        g_refs, o_refs = refs[:n], refs[n:2 * n]
        send_sems, recv_sems = refs[2 * n:]
        mx, my, mc = lax.axis_index("x"), lax.axis_index("y"), lax.axis_index("c")
        copies = [pltpu.make_async_remote_copy(
            src_ref=g_refs[t].at[:, 2 * j + (1 - mc)], dst_ref=o_refs[t].at[j], send_sem=send_sems.at[j * n + t],
            recv_sem=recv_sems.at[j * n + t], device_id=(mx, my, 1 - mc), device_id_type=MESH)
            for t in range(n) for j in range(4)]
        for cp in copies:
            cp.start()
        for cp in copies:
            cp.wait_recv()
        for cp in copies:
            cp.wait_send()

    return pl.pallas_call(
        body, out_shape=tuple(S((4, g.shape[0]) + g.shape[2:], g.dtype) for g in gs), in_specs=[ANY] * n,
        out_specs=tuple([ANY] * n),
        scratch_shapes=[pltpu.SemaphoreType.DMA((4 * n,)), pltpu.SemaphoreType.DMA((4 * n,))], name=name)(*gs)


def _rs_chips(ps, name):
    n = len(ps)

    def body(*refs):
        p_refs, o_refs = refs[:n], refs[n:2 * n]
        send_sems, recv_sems = refs[2 * n:]
        mx, my, mc = lax.axis_index("x"), lax.axis_index("y"), lax.axis_index("c")
        chips = [(1 - mx, my), (mx, 1 - my), (1 - mx, 1 - my)]
        copies = [pltpu.make_async_remote_copy(
            src_ref=p_refs[t].at[2 * cx + cy], dst_ref=o_refs[t].at[j], send_sem=send_sems.at[j * n + t],
            recv_sem=recv_sems.at[j * n + t], device_id=(cx, cy, mc), device_id_type=MESH)
            for t in range(n) for j, (cx, cy) in enumerate(chips)]
        for cp in copies:
            cp.start()
        for cp in copies:
            cp.wait_recv()
        for cp in copies:
            cp.wait_send()

    return pl.pallas_call(
        body, out_shape=tuple(S((3,) + p.shape[1:], p.dtype) for p in ps), in_specs=[ANY] * n,
        out_specs=tuple([ANY] * n),
        scratch_shapes=[pltpu.SemaphoreType.DMA((3 * n,)), pltpu.SemaphoreType.DMA((3 * n,))], name=name)(*ps)


FLIPS = [(fx, fy, fc) for fx in (0, 1) for fy in (0, 1) for fc in (0, 1)][1:]
HBM = pl.BlockSpec(memory_space=pltpu.HBM)
SEM = pl.BlockSpec(memory_space=pltpu.SEMAPHORE)
EFFECT = pltpu.SideEffectType.DATAFLOW_SIDE_EFFECTING


def _hbm(a):
    return pltpu.with_memory_space_constraint(a, pltpu.HBM)


def _split_copies(gather, row_sharded, s_refs, l_refs, send_sems, recv_sems):
    n = len(s_refs)
    mx, my, mc = lax.axis_index("x"), lax.axis_index("y"), lax.axis_index("c")
    me = 4 * mx + 2 * my + mc
    copies = []
    for k, (fx, fy, fc) in enumerate(FLIPS):
        px, py, pc = (1 - mx if fx else mx), (1 - my if fy else my), (1 - mc if fc else mc)
        for t in range(n):
            if gather:
                src = s_refs[t]
                if row_sharded[t]:
                    r = src.shape[0]
                    dst = l_refs[t].at[pl.ds(pl.multiple_of(me * r, 16), r), :]
                else:
                    dst = l_refs[t].at[me]
            else:
                src = s_refs[t].at[:, 4 * px + 2 * py + pc]
                dst = l_refs[t].at[k]
            copies.append(pltpu.make_async_remote_copy(
                src_ref=src, dst_ref=dst, send_sem=send_sems.at[k * n + t], recv_sem=recv_sems.at[k * n + t],
                device_id=(px, py, pc), device_id_type=MESH))
    return copies


def _split_start(gather, row_sharded, srcs, lands, after, name):
    n = len(srcs)

    def body(*refs):
        s_refs, l_refs = refs[:n], refs[n:2 * n]
        send_sems, recv_sems = refs[2 * n + 1], refs[2 * n + 2]
        token = refs[-1]
        for cp in _split_copies(gather, row_sharded, s_refs, l_refs, send_sems, recv_sems):
            cp.start()
        token[...] = jnp.zeros_like(token)

    outs = pl.pallas_call(
        body, name=name,
        out_shape=(pltpu.SemaphoreType.DMA((7 * n,)), pltpu.SemaphoreType.DMA((7 * n,)))
        + tuple(pltpu.HBM(a.shape, a.dtype) for a in srcs) + tuple(pltpu.HBM(a.shape, a.dtype) for a in lands)
        + (S((8, LANES), F32),),
        in_specs=[HBM] * (2 * n) + [ANY],
        out_specs=(SEM, SEM) + (HBM,) * (2 * n) + (pl.BlockSpec(memory_space=pltpu.VMEM),),
        input_output_aliases={i: 2 + i for i in range(2 * n)},
        compiler_params=pltpu.CompilerParams(has_side_effects=EFFECT),
    )(*[_hbm(a) for a in srcs], *[_hbm(a) for a in lands], after)
    return outs[0], outs[1], list(outs[2:2 + n]), list(outs[2 + n:2 + 2 * n]), outs[-1]


def _split_wait(gather, row_sharded, send_sems, recv_sems, srcs, lands, after, name):
    n = len(srcs)

    def body(*refs):
        s_refs, l_refs = refs[:n], refs[n:2 * n]
        ssem, rsem = refs[2 * n], refs[2 * n + 1]
        copies = _split_copies(gather, row_sharded, s_refs, l_refs, ssem, rsem)
        for cp in copies:
            cp.wait_send()
        for cp in copies:
            cp.wait_recv()

    outs = pl.pallas_call(
        body, name=name,
        out_shape=tuple(pltpu.HBM(a.shape, a.dtype) for a in srcs) + tuple(pltpu.HBM(a.shape, a.dtype) for a in lands),
        in_specs=[HBM] * (2 * n) + [SEM, SEM, ANY], out_specs=(HBM,) * (2 * n),
        input_output_aliases={i: i for i in range(2 * n)},
        compiler_params=pltpu.CompilerParams(has_side_effects=EFFECT),
    )(*srcs, *lands, send_sems, recv_sems, after)
    return list(outs[n:])


def _place_own(shards, row_sharded, name):
    n = len(shards)

    def full_shape(a, rows):
        return (N_DEV * a.shape[0],) + a.shape[1:] if rows else (N_DEV,) + a.shape

    def body(*refs):
        x_refs, o_refs, sems = refs[:n], refs[n:2 * n], refs[2 * n]
        me = 4 * lax.axis_index("x") + 2 * lax.axis_index("y") + lax.axis_index("c")
        copies = []
        for t in range(n):
            if row_sharded[t]:
                r = shards[t].shape[0]
                dst = o_refs[t].at[pl.ds(pl.multiple_of(me * r, 16), r), :]
            else:
                dst = o_refs[t].at[me]
            copies.append(pltpu.make_async_copy(x_refs[t], dst, sems.at[t]))
        for cp in copies:
            cp.start()
        for cp in copies:
            cp.wait()

    return list(pl.pallas_call(
        body, out_shape=tuple(S(full_shape(a, r), a.dtype) for a, r in zip(shards, row_sharded)),
        in_specs=[ANY] * n, out_specs=tuple([ANY] * n), scratch_shapes=[pltpu.SemaphoreType.DMA((n,))],
        name=name)(*shards))


def _rows_tile(b):
    return _tile(b, (512, 256, 128)) if b > 512 else b


def _pair_sum(g, got, name):
    A, _, B, C = g.shape
    tb = _rows_tile(B)
    core = lax.axis_index("c").astype(jnp.int32).reshape(1)

    def body(c_ref, g_ref, r_ref, o_ref):
        o_ref[...] = (g_ref[...].astype(F32) + r_ref[...].astype(F32)).astype(o_ref.dtype)

    return pl.pallas_call(
        body,
        grid_spec=pltpu.PrefetchScalarGridSpec(
            num_scalar_prefetch=1, grid=(4, A, B // tb),
            in_specs=[pl.BlockSpec((1, 1, tb, C), lambda j, a, i, c_ref: (a, 2 * j + c_ref[0], i, 0)),
                      pl.BlockSpec((1, 1, tb, C), lambda j, a, i, c_ref: (j, a, i, 0))],
            out_specs=pl.BlockSpec((1, 1, tb, C), lambda j, a, i, c_ref: (j, a, i, 0))),
        out_shape=S((4, A, B, C), g.dtype), compiler_params=_cparams("parallel", "parallel", "parallel"),
        name=name)(core, g, got)


def _cols_to_natural(g8, name):
    _, L, K, c = g8.shape
    tk = _tile(K, (256, 128))

    def body(x_ref, o_ref):
        o_ref[...] = jnp.concatenate([x_ref[d] for d in range(N_DEV)], axis=-1)

    return pl.pallas_call(
        body, grid=(L, K // tk), in_specs=[pl.BlockSpec((N_DEV, None, tk, c), lambda l, i: (0, l, i, 0))],
        out_specs=pl.BlockSpec((None, tk, N_DEV * c), lambda l, i: (l, i, 0)),
        out_shape=S((L, K, N_DEV * c), g8.dtype), compiler_params=_cparams("parallel", "parallel"), name=name)(g8)


def _natural_to_cols(g, name):
    L, K, c8 = g.shape
    c = c8 // N_DEV
    tk = _tile(K, (256, 128))

    def body(x_ref, o_ref):
        xv = x_ref[...]
        for d in range(N_DEV):
            o_ref[d] = xv[:, d * c:(d + 1) * c]

    return pl.pallas_call(
        body, grid=(L, K // tk), in_specs=[pl.BlockSpec((None, tk, c8), lambda l, i: (l, i, 0))],
        out_specs=pl.BlockSpec((N_DEV, None, tk, c), lambda l, i: (0, l, i, 0)),
        out_shape=S((N_DEV, L, K, c), g.dtype), compiler_params=_cparams("parallel", "parallel"), name=name)(g)


def _adamw_math(w, g, m, v):
    m = ADAM_B1 * m + (1.0 - ADAM_B1) * g
    v = ADAM_B2 * v + (1.0 - ADAM_B2) * (g * g)
    m_hat = m / (1.0 - ADAM_B1 ** ADAM_STEP)
    v_hat = v / (1.0 - ADAM_B2 ** ADAM_STEP)
    delta = -ADAM_LR * (m_hat / (jnp.sqrt(v_hat) + ADAM_EPS) + ADAM_WD * w)
    return delta, m, v


def _adamw_sharded(p, got, w, m, v, name):
    A, B, C = w.shape
    tb = _rows_tile(B)
    chip = (2 * lax.axis_index("x") + lax.axis_index("y")).astype(jnp.int32).reshape(1)

    def body(c_ref, p_ref, got_ref, w_ref, m_ref, v_ref, g_out, d_out, m_out, v_out):
        g = p_ref[0].astype(F32)
        for j in range(3):
            g = g + got_ref[j].astype(F32)
        d, mn, vn = _adamw_math(w_ref[...], g, m_ref[...], v_ref[...])
        g_out[...] = g
        d_out[...] = d
        m_out[...] = mn
        v_out[...] = vn

    blk = pl.BlockSpec((1, tb, C), lambda a, i, c_ref: (a, i, 0))
    return pl.pallas_call(
        body,
        grid_spec=pltpu.PrefetchScalarGridSpec(
            num_scalar_prefetch=1, grid=(A, B // tb),
            in_specs=[pl.BlockSpec((1, 1, tb, C), lambda a, i, c_ref: (c_ref[0], a, i, 0)),
                      pl.BlockSpec((3, 1, tb, C), lambda a, i, c_ref: (0, a, i, 0)), blk, blk, blk],
            out_specs=(blk, blk, blk, blk)),
        out_shape=(S((A, B, C), F32),) * 4, compiler_params=_cparams("parallel", "parallel"), name=name)(
            chip, p, got, w, m, v)


def _adamw_layers(owns, gots, w, m, v, name):
    L, B, C = w.shape
    per_row = 2 * L * len(FLIPS) * C * owns[0].dtype.itemsize
    tb = max(t for t in range(16, B + 1, 16) if B % t == 0 and (t * per_row <= 16 * 1024 * 1024 or t == 16))
    me = (4 * lax.axis_index("x") + 2 * lax.axis_index("y") + lax.axis_index("c")).astype(jnp.int32).reshape(1)

    def body(me_ref, *refs):
        own_refs, got_refs = refs[:L], refs[L:2 * L]
        w_ref, m_ref, v_ref = refs[2 * L:2 * L + 3]
        g_out, d_out, m_out, v_out = refs[2 * L + 3:]
        layer = pl.program_id(0)
        for kk in range(L):
            @pl.when(layer == kk)
            def _():
                g = own_refs[kk][0].astype(F32)
                for s in range(len(FLIPS)):
                    g = g + got_refs[kk][s].astype(F32)
                d, mn, vn = _adamw_math(w_ref[...], g, m_ref[...], v_ref[...])
                g_out[...] = g
                d_out[...] = d
                m_out[...] = mn
                v_out[...] = vn

    def row(kk, layer, i):
        return jnp.where(layer == kk, i, 0)

    blk = pl.BlockSpec((1, tb, C), lambda layer, i, me_ref: (layer, i, 0))
    own_specs = [pl.BlockSpec((1, 1, tb, C), lambda layer, i, me_ref, kk=kk: (0, me_ref[0], row(kk, layer, i), 0))
                 for kk in range(L)]
    got_specs = [pl.BlockSpec((len(FLIPS), 1, tb, C), lambda layer, i, me_ref, kk=kk: (0, 0, row(kk, layer, i), 0))
                 for kk in range(L)]
    return pl.pallas_call(
        body,
        grid_spec=pltpu.PrefetchScalarGridSpec(
            num_scalar_prefetch=1, grid=(L, B // tb), in_specs=own_specs + got_specs + [blk, blk, blk],
            out_specs=(blk, blk, blk, blk)),
        out_shape=(S((L, B, C), F32),) * 4, compiler_params=_cparams("arbitrary", "arbitrary"), name=name)(
            me, *owns, *gots, w, m, v)


def _adamw_replicated(parts, w, m, v, name):
    R, C = w.shape
    rb = _tile(R, (512, 256, 128, 64, 32, 16, 8))

    def body(p_ref, w_ref, m_ref, v_ref, g_out, d_out, m_out, v_out):
        g = p_ref[0]
        for j in range(1, N_DEV):
            g = g + p_ref[j]
        d, mn, vn = _adamw_math(w_ref[...], g, m_ref[...], v_ref[...])
        g_out[...] = g
        d_out[...] = d
        m_out[...] = mn
        v_out[...] = vn

    blk = pl.BlockSpec((rb, C), lambda i: (i, 0))
    return pl.pallas_call(
        body, grid=(R // rb,), in_specs=[pl.BlockSpec((N_DEV, rb, C), lambda i: (0, i, 0)), blk, blk, blk],
        out_specs=(blk, blk, blk, blk), out_shape=(S((R, C), F32),) * 4, compiler_params=_cparams("parallel"),
        name=name)(parts, w, m, v)


def _pack(arrs, rows_mult, dtype):
    flat = jnp.concatenate([a.reshape(-1).astype(dtype) for a in arrs])
    n = flat.shape[0]
    per = rows_mult * LANES
    tot = -(-n // per) * per
    return jnp.pad(flat, (0, tot - n)).reshape(tot // LANES, LANES)


def _unpack(blob, shapes):
    flat = blob.reshape(-1)
    out, off = [], 0
    for shp in shapes:
        n = int(np.prod(shp))
        out.append(flat[off:off + n].reshape(shp))
        off += n
    return out


def _small_to_natural(g8):
    t = jnp.moveaxis(g8, 0, -2)
    return t.reshape(t.shape[:-2] + (N_DEV * t.shape[-1],))


def _small_to_cols(g):
    t = g.reshape(g.shape[:-1] + (N_DEV, g.shape[-1] // N_DEV))
    return jnp.moveaxis(t, -2, 0)


def _block_diag(w):
    nb, bs, _ = w.shape
    eye = jnp.eye(nb, dtype=w.dtype)
    return (eye[:, None, :, None] * w[:, :, None, :]).reshape(nb * bs, nb * bs)


def _diag_blocks(d, nb, bs):
    d4 = d.reshape(nb, bs, nb, bs)
    return jnp.stack([d4[i, :, i, :] for i in range(nb)])


def kernel(x, mem, g_mix_pre, g_mix_post, g_ffn_pre, g_ffn_post, g_mem, w_mem_kv, w_mix_out, w_ffn_up, w_ffn_conv, b_ffn_conv, w_ffn_down, w_in_a, w_conv_a, b_conv_a, w_rg_r, b_rg_r, w_rg_i, b_rg_i, lru_lambda, w_in_b, sinks_b, g_kv, w_kv, loss_target, m_g_mix_pre, m_g_mix_post, m_g_ffn_pre, m_g_ffn_post, m_g_mem, m_w_mem_kv, m_w_mix_out, m_w_ffn_up, m_w_ffn_conv, m_b_ffn_conv, m_w_ffn_down, m_w_in_a, m_w_conv_a, m_b_conv_a, m_w_rg_r, m_b_rg_r, m_w_rg_i, m_b_rg_i, m_lru_lambda, m_w_in_b, m_sinks_b, m_g_kv, m_w_kv, v_g_mix_pre, v_g_mix_post, v_g_ffn_pre, v_g_ffn_post, v_g_mem, v_w_mem_kv, v_w_mix_out, v_w_ffn_up, v_w_ffn_conv, v_b_ffn_conv, v_w_ffn_down, v_w_in_a, v_w_conv_a, v_b_conv_a, v_w_rg_r, v_b_rg_r, v_w_rg_i, v_b_rg_i, v_lru_lambda, v_w_in_b, v_sinks_b, v_g_kv, v_w_kv):
    w_loc = dict(g_mix_pre=g_mix_pre, g_mix_post=g_mix_post, g_ffn_pre=g_ffn_pre, g_ffn_post=g_ffn_post, g_mem=g_mem,
                 w_mem_kv=w_mem_kv, w_mix_out=w_mix_out, w_ffn_up=w_ffn_up, w_ffn_conv=w_ffn_conv,
                 b_ffn_conv=b_ffn_conv, w_ffn_down=w_ffn_down, w_in_a=w_in_a, w_conv_a=w_conv_a, b_conv_a=b_conv_a,
                 w_rg_r=w_rg_r, b_rg_r=b_rg_r, w_rg_i=w_rg_i, b_rg_i=b_rg_i, lru_lambda=lru_lambda, w_in_b=w_in_b,
                 sinks_b=sinks_b, g_kv=g_kv, w_kv=w_kv)
    m_loc = dict(g_mix_pre=m_g_mix_pre, g_mix_post=m_g_mix_post, g_ffn_pre=m_g_ffn_pre, g_ffn_post=m_g_ffn_post,
                 g_mem=m_g_mem, w_mem_kv=m_w_mem_kv, w_mix_out=m_w_mix_out, w_ffn_up=m_w_ffn_up,
                 w_ffn_conv=m_w_ffn_conv, b_ffn_conv=m_b_ffn_conv, w_ffn_down=m_w_ffn_down, w_in_a=m_w_in_a,
                 w_conv_a=m_w_conv_a, b_conv_a=m_b_conv_a, w_rg_r=m_w_rg_r, b_rg_r=m_b_rg_r, w_rg_i=m_w_rg_i,
                 b_rg_i=m_b_rg_i, lru_lambda=m_lru_lambda, w_in_b=m_w_in_b, sinks_b=m_sinks_b, g_kv=m_g_kv,
                 w_kv=m_w_kv)
    v_loc = dict(g_mix_pre=v_g_mix_pre, g_mix_post=v_g_mix_post, g_ffn_pre=v_g_ffn_pre, g_ffn_post=v_g_ffn_post,
                 g_mem=v_g_mem, w_mem_kv=v_w_mem_kv, w_mix_out=v_w_mix_out, w_ffn_up=v_w_ffn_up,
                 w_ffn_conv=v_w_ffn_conv, b_ffn_conv=v_b_ffn_conv, w_ffn_down=v_w_ffn_down, w_in_a=v_w_in_a,
                 w_conv_a=v_w_conv_a, b_conv_a=v_b_conv_a, w_rg_r=v_w_rg_r, b_rg_r=v_b_rg_r, w_rg_i=v_w_rg_i,
                 b_rg_i=v_b_rg_i, lru_lambda=v_lru_lambda, w_in_b=v_w_in_b, sinks_b=v_sinks_b, g_kv=v_g_kv,
                 w_kv=v_w_kv)

    Bl, T, D = x.shape
    Ml = mem.shape[1]
    N = Bl * T
    depth = g_mix_pre.shape[0]
    n_a = w_in_a.shape[0]
    F = w_ffn_down.shape[1] * N_DEV
    row_sharded = {n: ax == w_loc[n].ndim - 2 for n, ax in SHARDED}

    def layer_keys(l):
        keys = [("w_mem_kv", l), ("w_mix_out", l), ("w_ffn_up", l), ("w_ffn_down", l)]
        keys.append(("w_in_a", l) if l < n_a else ("w_in_b", l - n_a))
        if l == n_a:
            keys.append(("w_kv", None))
        return keys

    def shard_of(key):
        n, i = key
        return (w_loc[n] if i is None else w_loc[n][i]).astype(BF16)

    def unshard_cols(key, a):
        if row_sharded[key[0]]:
            return a
        nat = _cols_to_natural(a.reshape((N_DEV, 1) + a.shape[1:]), name=f"unshard_{key[0]}_{key[1]}")
        return nat.reshape(nat.shape[1:])

    W = {}
    keys0 = layer_keys(0)
    got0 = _ag_weights([shard_of(kk) for kk in keys0] + [w_loc[n] for n in SMALL_SHARDED],
                       [row_sharded[kk[0]] for kk in keys0] + [False] * len(SMALL_SHARDED), name="ag_weights_0")
    for kk, a in zip(keys0, got0):
        W[kk] = unshard_cols(kk, a)
    for n, a in zip(SMALL_SHARDED, got0[len(keys0):]):
        W[n] = _small_to_natural(a)

    def gather_start(l):
        keys = layer_keys(l)
        rows = [row_sharded[kk[0]] for kk in keys]
        shards = [shard_of(kk) for kk in keys]
        lands = _place_own(shards, rows, name=f"ag_place_{l}")
        return (keys, rows) + _split_start(True, rows, shards, lands, W[("w_mem_kv", l - 1)], name=f"ag_start_{l}")

    def gather_wait(l, pending, after):
        keys, rows, ssem, rsem, srcs, lands, _ = pending
        for kk, a in zip(keys, _split_wait(True, rows, ssem, rsem, srcs, lands, after, name=f"ag_wait_{l}")):
            W[kk] = unshard_cols(kk, a)

    nblk, bsz = w_rg_r.shape[1], w_rg_r.shape[2]
    wbd = [jnp.concatenate([_block_diag(w_rg_r[j]), _block_diag(w_rg_i[j])], axis=1).astype(BF16) for j in range(n_a)]

    def vec(a):
        return a.reshape(1, -1)

    x2 = x.reshape(N, D)
    mem2 = mem.reshape(Bl * Ml, D)
    saved = []
    kvn = kv3 = x_kv = None
    xs = x2
    for l in range(depth):
        sv = {"x0": xs}
        g_pre = vec(g_mix_pre[l])
        if l + 1 < depth:
            pending = gather_start(l + 1)
            g_pre = g_pre + pending[-1][0, 0]
        h1 = _rms_fwd(xs, g_pre, BF16, name=f"rms_mixpre_{l}")
        memn = _rms_fwd(mem2, vec(g_mem[l]), BF16, name=f"rms_mem_{l}")
        mkv3 = _mm(memn, W[("w_mem_kv", l)], name=f"mm_memkv_{l}").reshape(Bl, Ml, 2 * MEM_W)
        if l < n_a:
            j = l
            proj = _mm(h1, W[("w_in_a", j)], name=f"mm_in_{l}")
            proj3 = proj.reshape(Bl, T, -1)
            xc3 = _conv_fwd_call(proj3, MIX_W, MIX_W, W["w_conv_a"][j], vec(W["b_conv_a"][j]), name=f"conv_a_{l}")
            gates3 = _mm(xc3.reshape(N, MIX_W), wbd[j], name=f"mm_gates_{l}").reshape(Bl, T, 2 * MIX_W)
            y_main3, hs3 = _rglru_fwd(xc3, gates3, proj3, vec(b_rg_r[j]), vec(b_rg_i[j]), vec(W["lru_lambda"][j]),
                                      name=f"rglru_fwd_{l}")
            q_off = 2 * MIX_W
            sv.update(xc3=xc3, gates3=gates3, hs3=hs3)
        else:
            j = l - n_a
            if l == n_a:
                x_kv = xs
                kvn = _rms_fwd(xs, vec(g_kv), BF16, name="rms_kv")
                kv3 = _mm(kvn, W[("w_kv", None)], name="mm_kv").reshape(Bl, T, 2 * MEM_W)
            proj = _mm(h1, W[("w_in_b", j)], name=f"mm_in_{l}")
            proj3 = proj.reshape(Bl, T, -1)
            y_main3 = _swa_fwd(proj3, kv3, sinks_b[j], name=f"swa_fwd_{l}")
            q_off = MIX_W
        y_mem3 = _mem_attn_fwd(proj3, q_off, mkv3, name=f"memattn_fwd_{l}")
        y_main = y_main3.reshape(N, MIX_W)
        y_mem = y_mem3.reshape(N, MEM_W)
        y = _mm(y_main, W[("w_mix_out", l)], n=D, k=MIX_W, name=f"mm_mixout_main_{l}")
        y = _mm(y_mem, W[("w_mix_out", l)], n=D, k=MEM_W, b_off=(MIX_W, 0), add=y, name=f"mm_mixout_mem_{l}")
        x1 = _rms_fwd(y, vec(g_mix_post[l]), F32, res=xs, name=f"rms_mixpost_{l}")
        h2 = _rms_fwd(x1, vec(g_ffn_pre[l]), BF16, name=f"rms_ffnpre_{l}")
        ug = _mm(h2, W[("w_ffn_up", l)], n=F, name=f"mm_up_g_{l}")
        uv = _mm(h2, W[("w_ffn_up", l)], n=F, b_off=(0, F), name=f"mm_up_v_{l}")
        ug3, uv3 = ug.reshape(Bl, T, F), uv.reshape(Bl, T, F)
        act3 = _ffn_mid_fwd(ug3, uv3, W["w_ffn_conv"][l], vec(b_ffn_conv[l]), name=f"ffn_mid_fwd_{l}")
        act = act3.reshape(N, F)
        f = _mm(act, W[("w_ffn_down", l)], name=f"mm_down_{l}")
        x_next = _rms_fwd(f, vec(g_ffn_post[l]), F32, res=x1, name=f"rms_ffnpost_{l}")
        if l + 1 < depth:
            gather_wait(l + 1, pending, x_next)
        sv.update(h1=h1, memn=memn, mkv3=mkv3, proj3=proj3, q_off=q_off, y_main=y_main, y_mem=y_mem, y=y, x1=x1,
                  h2=h2, ug3=ug3, uv3=uv3, act=act, f=f)
        saved.append(sv)
        xs = x_next

    dxs, loss_vec = _loss_bwd(xs, loss_target.reshape(N, D))
    loss = lax.psum(jnp.sum(loss_vec), ("x", "y", "c"))

    G = {n: [None] * w_loc[n].shape[0] for n in REPL + SMALL_SHARDED if n != "g_kv"}
    GW = {}

    def dw(key, off, a, b_, nm):
        GW[key] = _mm(a, b_, ta=True, out_dtype=BF16, into=(GW.get(key), (1,) + W[key].shape, 0, off), name=nm)

    def grad_blocks(key):
        g = GW[key]
        if row_sharded[key[0]]:
            return g.reshape(1, N_DEV, g.shape[1] // N_DEV, g.shape[2])
        t = _natural_to_cols(g, name=f"shard_grad_{key[0]}_{key[1]}")
        return t.reshape(1, N_DEV, t.shape[2], t.shape[3])

    reduces = []

    def reduce_start(keys, after, tag):
        srcs = [grad_blocks(kk) for kk in keys]
        lands = [lax.empty((len(FLIPS),) + s.shape[:1] + s.shape[2:], s.dtype) for s in srcs]
        started = _split_start(False, None, srcs, lands, after, name=f"rs_start_{tag}")
        reduces.append((keys, tag) + started)
        return started[-1][0, 0]

    kv_parts = []
    tok = None
    for l in reversed(range(depth)):
        sv = saved[l]
        proj3 = sv["proj3"]
        g_post = vec(g_ffn_post[l]) if tok is None else vec(g_ffn_post[l]) + tok
        df, dg = _rms_bwd(sv["f"], g_post, dxs, out_dtype=BF16, name=f"rmsb_ffnpost_{l}")
        G["g_ffn_post"][l] = dg[0]
        dact = _mm(df, W[("w_ffn_down", l)], tb=True, name=f"mmb_down_dx_{l}")
        dw(("w_ffn_down", l), (0, 0), sv["act"], df, f"mmb_down_dw_{l}")
        dug3, duv3, dwg, dwv, dbg, dbv = _ffn_mid_bwd(sv["ug3"], sv["uv3"], dact.reshape(Bl, T, F),
                                                      W["w_ffn_conv"][l], vec(b_ffn_conv[l]), name=f"ffn_mid_bwd_{l}")
        G["w_ffn_conv"][l] = jnp.concatenate([dwg, dwv], axis=1)
        G["b_ffn_conv"][l] = jnp.concatenate([dbg, dbv], axis=1)[0]
        dug, duv = dug3.reshape(N, F), duv3.reshape(N, F)
        dh2 = _mm(dug, W[("w_ffn_up", l)], tb=True, n=D, k=F, name=f"mmb_up_dx_g_{l}")
        dh2 = _mm(duv, W[("w_ffn_up", l)], tb=True, n=D, k=F, b_off=(0, F), add=dh2, name=f"mmb_up_dx_v_{l}")
        dw(("w_ffn_up", l), (0, 0), sv["h2"], dug, f"mmb_up_dw_g_{l}")
        dw(("w_ffn_up", l), (0, F), sv["h2"], duv, f"mmb_up_dw_v_{l}")
        tok = reduce_start([("w_ffn_down", l), ("w_ffn_up", l)], dh2, f"ffn_{l}")
        dx1, dg = _rms_bwd(sv["x1"], vec(g_ffn_pre[l]) + tok, dh2, add=dxs, name=f"rmsb_ffnpre_{l}")
        G["g_ffn_pre"][l] = dg[0]
        dy, dg = _rms_bwd(sv["y"], vec(g_mix_post[l]), dx1, out_dtype=BF16, name=f"rmsb_mixpost_{l}")
        G["g_mix_post"][l] = dg[0]
        dy_main = _mm(dy, W[("w_mix_out", l)], tb=True, n=MIX_W, k=D, name=f"mmb_mixout_dmain_{l}")
        dy_mem = _mm(dy, W[("w_mix_out", l)], tb=True, n=MEM_W, k=D, b_off=(MIX_W, 0),
                     name=f"mmb_mixout_dmem_{l}")
        dw(("w_mix_out", l), (0, 0), sv["y_main"], dy, f"mmb_mixout_dw_main_{l}")
        dw(("w_mix_out", l), (MIX_W, 0), sv["y_mem"], dy, f"mmb_mixout_dw_mem_{l}")
        dq_mem3, dmkv3 = _mem_attn_bwd(proj3, sv["q_off"], sv["mkv3"], dy_mem.reshape(Bl, T, MEM_W),
                                       name=f"memattn_bwd_{l}")
        dq_mem = dq_mem3.reshape(N, MEM_W)
        dmkv = dmkv3.reshape(Bl * Ml, 2 * MEM_W)
        dw(("w_mem_kv", l), (0, 0), sv["memn"], dmkv, f"mmb_memkv_dw_{l}")
        dmemn = _mm(dmkv, W[("w_mem_kv", l)], tb=True, name=f"mmb_memkv_dx_{l}")
        _, dg = _rms_bwd(mem2, vec(g_mem[l]), dmemn, name=f"rmsb_mem_{l}")
        G["g_mem"][l] = dg[0]
        dy_main3 = dy_main.reshape(Bl, T, MIX_W)
        if l < n_a:
            j = l
            dxc3, drp3, dip3, dugate3, dbr, dbi, dlam = _rglru_bwd(
                dy_main3, sv["xc3"], sv["gates3"], proj3, sv["hs3"], vec(b_rg_r[j]), vec(b_rg_i[j]),
                vec(W["lru_lambda"][j]), name=f"rglru_bwd_{l}")
            G["b_rg_r"][j] = dbr.reshape(nblk, bsz)
            G["b_rg_i"][j] = dbi.reshape(nblk, bsz)
            G["lru_lambda"][j] = dlam[0]
            drp, dip = drp3.reshape(N, MIX_W), dip3.reshape(N, MIX_W)
            xc2 = sv["xc3"].reshape(N, MIX_W)
            G["w_rg_r"][j] = _diag_blocks(_mm(xc2, drp, ta=True, name=f"mmb_gates_dw_r_{l}"), nblk, bsz)
            G["w_rg_i"][j] = _diag_blocks(_mm(xc2, dip, ta=True, name=f"mmb_gates_dw_i_{l}"), nblk, bsz)
            dxc = _mm(drp, wbd[j], tb=True, n=MIX_W, k=MIX_W, add=dxc3.reshape(N, MIX_W), name=f"mmb_gates_dx_r_{l}")
            dxc = _mm(dip, wbd[j], tb=True, n=MIX_W, k=MIX_W, b_off=(0, MIX_W), add=dxc, name=f"mmb_gates_dx_i_{l}")
            dux3, dwc, dbc = _conv_bwd_call(dxc.reshape(Bl, T, MIX_W), proj3, MIX_W, MIX_W, W["w_conv_a"][j],
                                            name=f"conv_a_bwd_{l}")
            G["w_conv_a"][j] = dwc
            G["b_conv_a"][j] = dbc[0]
            pieces = [(dugate3.reshape(N, MIX_W), 0), (dux3.reshape(N, MIX_W), MIX_W), (dq_mem, 2 * MIX_W)]
            in_key = ("w_in_a", j)
        else:
            j = l - n_a
            dq3, dkc, dkp, dsk = _swa_bwd(proj3, kv3, sinks_b[j], dy_main3, name=f"swa_bwd_{l}")
            kv_parts.append((dkc, dkp))
            G["sinks_b"][j] = dsk[0, :SWA_HEADS]
            pieces = [(dq3.reshape(N, MIX_W), 0), (dq_mem, MIX_W)]
            in_key = ("w_in_b", j)
        dh1 = None
        for pi, (piece, off) in enumerate(pieces):
            dh1 = _mm(piece, W[in_key], tb=True, n=D, k=piece.shape[1], b_off=(0, off), add=dh1,
                      name=f"mmb_in_dx_{pi}_{l}")
            dw(in_key, (0, off), sv["h1"], piece, f"mmb_in_dw_{pi}_{l}")
        dxs, dg = _rms_bwd(sv["x0"], vec(g_mix_pre[l]), dh1, add=dx1, name=f"rmsb_mixpre_{l}")
        G["g_mix_pre"][l] = dg[0]
        mix_keys = [("w_mix_out", l), ("w_mem_kv", l), in_key]
        if l == n_a:
            dkv = _kv_grad_combine(kv_parts, name="kv_grad_combine").reshape(N, 2 * MEM_W)
            dw(("w_kv", None), (0, 0), kvn, dkv, "mmb_kv_dw")
            dkvn = _mm(dkv, W[("w_kv", None)], tb=True, name="mmb_kv_dx")
            dxs, dg = _rms_bwd(x_kv, vec(g_kv), dkvn, add=dxs, name="rmsb_kv")
            G["g_kv"] = dg[0]
            mix_keys.append(("w_kv", None))
        tok = reduce_start(mix_keys, dxs, f"mix_{l}")
    grad_x = dxs.reshape(Bl, T, D)
    Gf = {n: (jnp.stack(g) if isinstance(g, list) else g) for n, g in G.items()}

    parts = {}
    for keys, tag, ssem, rsem, srcs, lands, _ in reduces:
        for kk, s, g7 in zip(keys, srcs, _split_wait(False, None, ssem, rsem, srcs, lands, dxs, name=f"rs_wait_{tag}")):
            parts[kk] = (s, g7)
    g4 = []
    for n in SMALL_SHARDED:
        t = _small_to_cols(Gf[n]).astype(BF16)
        g4.append(t.reshape(1, N_DEV, -1, t.shape[-1]))
    got = _rs_sibling(g4, name="rs_sibling")
    psum4 = [_pair_sum(g, r, name=f"rs_pair_sum_{n}") for n, g, r in zip(SMALL_SHARDED, g4, got)]
    got2 = _rs_chips(psum4, name="rs_chips")
    r_blob = _pack([Gf[n].astype(F32) for n in REPL], REPL_ROWS, F32)
    r_parts = _all_gather(r_blob, name="ag_repl_grads")

    res = [{} for _ in range(4)]
    for n, p4, g2 in zip(SMALL_SHARDED, psum4, got2):
        shp3 = p4.shape[1:]
        outs = _adamw_sharded(p4, g2, w_loc[n].reshape(shp3), m_loc[n].reshape(shp3), v_loc[n].reshape(shp3),
                              name=f"adamw_{n}")
        for k in range(4):
            res[k][n] = outs[k].reshape(w_loc[n].shape)
    for n, _ in SHARDED:
        if n in SMALL_SHARDED:
            continue
        idx = [None] if w_loc[n].ndim == 2 else list(range(w_loc[n].shape[0]))
        shp3 = (len(idx),) + w_loc[n].shape[-2:]
        outs = _adamw_layers([parts[(n, i)][0] for i in idx], [parts[(n, i)][1] for i in idx], w_loc[n].reshape(shp3),
                             m_loc[n].reshape(shp3), v_loc[n].reshape(shp3), name=f"adamw_{n}")
        for k in range(4):
            res[k][n] = outs[k].reshape(w_loc[n].shape)
    outs_rp = _adamw_replicated(r_parts, _pack([w_loc[n] for n in REPL], REPL_ROWS, F32),
                                _pack([m_loc[n] for n in REPL], REPL_ROWS, F32),
                                _pack([v_loc[n] for n in REPL], REPL_ROWS, F32),
                                name="adamw_replicated")
    rp_shapes = [w_loc[n].shape for n in REPL]
    for k in range(4):
        res[k].update(zip(REPL, _unpack(outs_rp[k], rp_shapes)))
    out = [loss, grad_x]
    for k in range(4):
        out += [res[k][n] for n in WEIGHTS]
    return tuple(out)
```

```python
import functools
import math

import numpy as np
import jax
import jax.numpy as jnp
from jax import lax
from jax.experimental import pallas as pl
from jax.experimental.pallas import tpu as pltpu

F32 = jnp.float32
BF16 = jnp.bfloat16
S = jax.ShapeDtypeStruct
MESH = pl.DeviceIdType.MESH
ANY = pl.BlockSpec(memory_space=pl.ANY)

HEAD = 64
MEM_HEADS = 4
MEM_W = MEM_HEADS * HEAD
SWA_HEADS = 12
SWA_GROUP = 3
MIX_W = SWA_HEADS * HEAD
WIN = 128
LRU_C = 8.0
EPS = 1e-6
ADAM_LR, ADAM_B1, ADAM_B2, ADAM_EPS, ADAM_WD, ADAM_STEP = 0.001, 0.9, 0.999, 1e-08, 0.01, 10
GELU_C0 = math.sqrt(2.0 / math.pi)
GELU_C1 = 0.044715
N_DEV = 8
LANES = 128
CT = 128
VMEM_LIMIT = 48 * 1024 * 1024
MM_VMEM_BUDGET = 36 * 1024 * 1024
REPL_ROWS = 256

SHARDED = (("w_mem_kv", 1), ("w_mix_out", 1), ("w_ffn_up", 2), ("w_ffn_conv", 2), ("w_ffn_down", 1), ("w_in_a", 2),
           ("w_conv_a", 2), ("b_conv_a", 1), ("lru_lambda", 1), ("w_in_b", 1), ("w_kv", 0))
SMALL_SHARDED = ("w_ffn_conv", "w_conv_a", "b_conv_a", "lru_lambda")
TRANSPOSED = ("w_ffn_up", "w_in_a")
REPL = ("g_mix_pre", "g_mix_post", "g_ffn_pre", "g_ffn_post", "g_mem", "b_ffn_conv", "w_rg_r", "b_rg_r", "w_rg_i",
        "b_rg_i", "sinks_b", "g_kv")
WEIGHTS = ("g_mix_pre", "g_mix_post", "g_ffn_pre", "g_ffn_post", "g_mem", "w_mem_kv", "w_mix_out", "w_ffn_up",
           "w_ffn_conv", "b_ffn_conv", "w_ffn_down", "w_in_a", "w_conv_a", "b_conv_a", "w_rg_r", "b_rg_r", "w_rg_i",
           "b_rg_i", "lru_lambda", "w_in_b", "sinks_b", "g_kv", "w_kv")


def _alibi_slopes(n):
    def pow2(m):
        start = 2.0 ** (-8.0 / m)
        return [start ** (i + 1) for i in range(m)]
    c = 2 ** int(math.floor(math.log2(n)))
    s = pow2(c)
    if c != n:
        s = s + pow2(2 * c)[0::2][: n - c]
    return [float(v) for v in np.asarray(s, dtype=np.float32)]


SLOPES = _alibi_slopes(SWA_HEADS)


def _tile(n, cands):
    for c in cands:
        if n % c == 0:
            return c
    return n


def _cparams(*sem):
    return pltpu.CompilerParams(dimension_semantics=sem, vmem_limit_bytes=VMEM_LIMIT)


def _mm_tiles(M, N, K, a_bytes, b_bytes, o_bytes, add_bytes, offsets):
    m_off, n_offs, k_off = offsets
    tms = [c for c in (1024, 512, 256, 128) if M % c == 0 and m_off % c == 0] or [M]
    tns = [c for c in (1408, 1024, 896, 768, 512, 384, 256, 128)
           if N % c == 0 and all(o % c == 0 for o in n_offs)] or [N]
    tks = [c for c in (K, 2048, 1408, 1024, 512, 256, 128) if c <= K and K % c == 0 and k_off % c == 0]
    best = None
    for tk in tks:
        fits = []
        for tm in tms:
            for tn in tns:
                need = 2 * (tm * tk * a_bytes + tk * tn * b_bytes + tm * tn * (o_bytes + add_bytes))
                need += tm * tn * 4 * (2 if tk < K else 1)
                need += (tm * tk * 2 if a_bytes != 2 else 0) + (tk * tn * 2 if b_bytes != 2 else 0)
                if need <= MM_VMEM_BUDGET:
                    fits.append((tm * tn, min(tm, 512), tm, tn))
        if fits:
            _, _, tm, tn = max(fits)
            best = (tm, tn, tk)
            break
    assert best is not None, (M, N, K)
    return best


def _mm(a, b, *, ta=False, tb=False, n=None, k=None, b_off=(0, 0), out_dtype=F32, add=None, into=None, after=None,
        name="mm"):
    if ta:
        K, M = a.shape
    else:
        M, K = a.shape
    if tb:
        N = b.shape[-2] if n is None else n
    else:
        N = b.shape[-1] if n is None else n
    assert k is None or k == K
    ro, co = b_off
    n_off, k_off = (ro, co) if tb else (co, ro)
    oro, oco = (0, 0) if into is None else into[3]
    tm, tn, tk = _mm_tiles(M, N, K, a.dtype.itemsize, b.dtype.itemsize, jnp.dtype(out_dtype).itemsize,
                           0 if add is None else add.dtype.itemsize, (oro, (n_off, oco), k_off))
    nk = K // tk
    if tb:
        b_spec = pl.BlockSpec((tn, tk), lambda i, j, kk: (j + ro // tn, kk + co // tk))
        b_dims = (1,)
    else:
        b_spec = pl.BlockSpec((tk, tn), lambda i, j, kk: (kk + ro // tk, j + co // tn))
        b_dims = (0,)
    if ta:
        a_spec = pl.BlockSpec((tk, tm), lambda i, j, kk: (kk, i))
        a_dims = (0,)
    else:
        a_spec = pl.BlockSpec((tm, tk), lambda i, j, kk: (i, kk))
        a_dims = (1,)
    dims = ((a_dims, b_dims), ((), ()))
    add_spec = pl.BlockSpec((tm, tn), lambda i, j, kk: (i, j))
    has_add = add is not None
    if into is None:
        o_spec, o_shape, buf = add_spec, (M, N), None
    else:
        buf, o_shape, ol, _ = into
        assert not has_add
        o_spec = pl.BlockSpec((None, tm, tn), lambda i, j, kk: (ol, i + oro // tm, j + oco // tn))
    has_buf = buf is not None

    def body(*refs):
        refs = list(refs)
        acc_ref = refs.pop() if nk > 1 else None
        o_ref = refs.pop()
        a_ref, b_ref = refs[0], refs[1]
        add_ref = refs[2] if has_add else None
        part = lax.dot_general(a_ref[...].astype(BF16), b_ref[...].astype(BF16), dims, preferred_element_type=F32)

        def finish(r):
            if has_add:
                r = r + add_ref[...].astype(F32)
            o_ref[...] = r.astype(out_dtype)

        if nk == 1:
            finish(part)
        else:
            kk = pl.program_id(2)

            @pl.when(kk == 0)
            def _():
                acc_ref[...] = part

            @pl.when(kk > 0)
            def _():
                acc_ref[...] += part

            @pl.when(kk == nk - 1)
            def _():
                finish(acc_ref[...])

    in_specs = [a_spec, b_spec] + ([add_spec] if has_add else []) + ([ANY] if has_buf else [])
    args = (a, b) + ((add,) if has_add else ()) + ((buf,) if has_buf else ())
    if after is not None:
        in_specs, args = in_specs + [ANY], args + (after,)
    return pl.pallas_call(
        body, grid=(M // tm, N // tn, nk), in_specs=in_specs, out_specs=o_spec,
        out_shape=S(o_shape, out_dtype), scratch_shapes=[pltpu.VMEM((tm, tn), F32)] if nk > 1 else [],
        input_output_aliases={2: 0} if has_buf else {},
        compiler_params=_cparams("parallel", "parallel", "arbitrary"), name=name)(*args)


def _rms_fwd(x, g, out_dtype, res=None, name="rms_fwd"):
    N, D = x.shape
    tm = _tile(N, (512, 256, 128))
    has_res = res is not None

    def body(*refs):
        if has_res:
            x_ref, g_ref, r_ref, o_ref = refs
        else:
            x_ref, g_ref, o_ref = refs
        xv = x_ref[...].astype(F32)
        y = xv * lax.rsqrt(jnp.mean(xv * xv, axis=-1, keepdims=True) + EPS) * g_ref[...]
        if has_res:
            y = y + r_ref[...]
        o_ref[...] = y.astype(out_dtype)

    row = pl.BlockSpec((tm, D), lambda i: (i, 0))
    vec = pl.BlockSpec((1, D), lambda i: (0, 0))
    return pl.pallas_call(
        body, grid=(N // tm,), in_specs=[row, vec] + ([row] if has_res else []), out_specs=row,
        out_shape=S((N, D), out_dtype), compiler_params=_cparams("parallel"), name=name)(
            *((x, g) + ((res,) if has_res else ())))


def _rms_bwd(x, g, dy, add=None, out_dtype=F32, name="rms_bwd"):
    N, D = x.shape
    tm = _tile(N, (512, 256, 128))
    has_add = add is not None

    def body(*refs):
        if has_add:
            x_ref, g_ref, dy_ref, add_ref, dx_ref, dg_ref = refs
        else:
            x_ref, g_ref, dy_ref, dx_ref, dg_ref = refs
        xv = x_ref[...].astype(F32)
        dyv = dy_ref[...].astype(F32)
        r = lax.rsqrt(jnp.mean(xv * xv, axis=-1, keepdims=True) + EPS)
        u = dyv * g_ref[...]
        dx = r * u - xv * (r * r * r * jnp.mean(u * xv, axis=-1, keepdims=True))
        if has_add:
            dx = dx + add_ref[...]
        dx_ref[...] = dx.astype(out_dtype)

        @pl.when(pl.program_id(0) == 0)
        def _():
            dg_ref[...] = jnp.zeros_like(dg_ref)

        dg_ref[...] += jnp.sum(dyv * xv * r, axis=0, keepdims=True)

    row = pl.BlockSpec((tm, D), lambda i: (i, 0))
    vec = pl.BlockSpec((1, D), lambda i: (0, 0))
    return pl.pallas_call(
        body, grid=(N // tm,), in_specs=[row, vec, row] + ([row] if has_add else []), out_specs=(row, vec),
        out_shape=(S((N, D), out_dtype), S((1, D), F32)), compiler_params=_cparams("arbitrary"), name=name)(
            *((x, g, dy) + ((add,) if has_add else ())))


def _shift_down(x, s, row):
    return jnp.where(row >= s, pltpu.roll(x, s, axis=0), 0.0)


def _shift_up(x, s, row):
    T = x.shape[0]
    return jnp.where(row < T - s, pltpu.roll(x, T - s, axis=0), 0.0)


def _conv(x, w_ref, b_ref, row):
    W = w_ref.shape[0]
    y = x * w_ref[W - 1:W, :] + b_ref[...]
    for s in range(1, W):
        y = y + _shift_down(x, s, row) * w_ref[W - 1 - s:W - s, :]
    return y


def _conv_bwd(dy, x, w_ref, row):
    W = w_ref.shape[0]
    dx = dy * w_ref[W - 1:W, :]
    dws = [None] * W
    dws[W - 1] = jnp.sum(dy * x, axis=0, keepdims=True)
    for s in range(1, W):
        dx = dx + _shift_up(dy, s, row) * w_ref[W - 1 - s:W - s, :]
        dws[W - 1 - s] = jnp.sum(dy * _shift_down(x, s, row), axis=0, keepdims=True)
    return dx, jnp.concatenate(dws, axis=0), jnp.sum(dy, axis=0, keepdims=True)


def _gelu(g):
    t = jnp.tanh(GELU_C0 * (g + GELU_C1 * g * g * g))
    return 0.5 * g * (1.0 + t), t


def _dgelu(g, t):
    return 0.5 * (1.0 + t) + 0.5 * g * (1.0 - t * t) * (GELU_C0 * (1.0 + 3.0 * GELU_C1 * g * g))


def _cspec(T, off=0):
    return pl.BlockSpec((1, T, CT), lambda j, b: (b, 0, j + off))


def _pspec(rows, off=0):
    return pl.BlockSpec((rows, CT), lambda j, b: (0, j + off))


def _conv_fwd_call(x3, x_off, C, w, b, name):
    Bl, T, _ = x3.shape
    W = w.shape[0]

    def body(x_ref, w_ref, b_ref, o_ref):
        row = lax.broadcasted_iota(jnp.int32, (T, CT), 0)
        o_ref[0] = _conv(x_ref[0], w_ref, b_ref, row)

    return pl.pallas_call(
        body, grid=(C // CT, Bl), in_specs=[_cspec(T, x_off // CT), _pspec(W), _pspec(1)], out_specs=_cspec(T),
        out_shape=S((Bl, T, C), F32), compiler_params=_cparams("parallel", "arbitrary"), name=name)(x3, w, b)


def _conv_bwd_call(dy3, x3, x_off, C, w, name):
    Bl, T, _ = x3.shape
    W = w.shape[0]

    def body(dy_ref, x_ref, w_ref, dx_ref, dw_ref, db_ref):
        row = lax.broadcasted_iota(jnp.int32, (T, CT), 0)
        dx, dw, db = _conv_bwd(dy_ref[0], x_ref[0], w_ref, row)
        dx_ref[0] = dx.astype(BF16)

        @pl.when(pl.program_id(1) == 0)
        def _():
            dw_ref[...] = jnp.zeros_like(dw_ref)
            db_ref[...] = jnp.zeros_like(db_ref)

        dw_ref[...] += dw
        db_ref[...] += db

    return pl.pallas_call(
        body, grid=(C // CT, Bl), in_specs=[_cspec(T), _cspec(T, x_off // CT), _pspec(W)],
        out_specs=(_cspec(T), _pspec(W), _pspec(1)),
        out_shape=(S((Bl, T, C), BF16), S((W, C), F32), S((1, C), F32)),
        compiler_params=_cparams("parallel", "arbitrary"), name=name)(dy3, x3, w)


def _ffn_mid_fwd(ug3, uv3, wc, bc, name):
    Bl, T, F = ug3.shape
    nf = F // CT

    def body(ug_ref, uv_ref, wg_ref, wv_ref, bg_ref, bv_ref, o_ref):
        row = lax.broadcasted_iota(jnp.int32, (T, CT), 0)
        g = _conv(ug_ref[0], wg_ref, bg_ref, row)
        v = _conv(uv_ref[0], wv_ref, bv_ref, row)
        o_ref[0] = (_gelu(g)[0] * v).astype(BF16)

    return pl.pallas_call(
        body, grid=(nf, Bl),
        in_specs=[_cspec(T), _cspec(T), _pspec(3), _pspec(3, nf), _pspec(1), _pspec(1, nf)], out_specs=_cspec(T),
        out_shape=S((Bl, T, F), BF16), compiler_params=_cparams("parallel", "arbitrary"), name=name)(
            ug3, uv3, wc, wc, bc, bc)


def _ffn_mid_bwd(ug3, uv3, dact3, wc, bc, name):
    Bl, T, F = ug3.shape
    nf = F // CT

    def body(ug_ref, uv_ref, da_ref, wg_ref, wv_ref, bg_ref, bv_ref, dug_ref, duv_ref, dwg_ref, dwv_ref, dbg_ref,
             dbv_ref):
        row = lax.broadcasted_iota(jnp.int32, (T, CT), 0)
        ug = ug_ref[0]
        uv = uv_ref[0]
        g = _conv(ug, wg_ref, bg_ref, row)
        v = _conv(uv, wv_ref, bv_ref, row)
        da = da_ref[0]
        gel, t = _gelu(g)
        dg = da * v * _dgelu(g, t)
        dv = da * gel
        dug, dwg, dbg = _conv_bwd(dg, ug, wg_ref, row)
        duv, dwv, dbv = _conv_bwd(dv, uv, wv_ref, row)
        dug_ref[0] = dug.astype(BF16)
        duv_ref[0] = duv.astype(BF16)

        @pl.when(pl.program_id(1) == 0)
        def _():
            dwg_ref[...] = jnp.zeros_like(dwg_ref)
            dwv_ref[...] = jnp.zeros_like(dwv_ref)
            dbg_ref[...] = jnp.zeros_like(dbg_ref)
            dbv_ref[...] = jnp.zeros_like(dbv_ref)

        dwg_ref[...] += dwg
        dwv_ref[...] += dwv
        dbg_ref[...] += dbg
        dbv_ref[...] += dbv

    return pl.pallas_call(
        body, grid=(nf, Bl),
        in_specs=[_cspec(T), _cspec(T), _cspec(T), _pspec(3), _pspec(3, nf), _pspec(1), _pspec(1, nf)],
        out_specs=(_cspec(T), _cspec(T), _pspec(3), _pspec(3), _pspec(1), _pspec(1)),
        out_shape=(S((Bl, T, F), BF16), S((Bl, T, F), BF16), S((3, F), F32), S((3, F), F32), S((1, F), F32),
                   S((1, F), F32)),
        compiler_params=_cparams("parallel", "arbitrary"), name=name)(ug3, uv3, dact3, wc, wc, bc, bc)


def _lru_gates(xc, rp, ip, br_ref, bi_ref, lam_ref):
    r = jax.nn.sigmoid(rp + br_ref[...])
    i = jax.nn.sigmoid(ip + bi_ref[...])
    lam = lam_ref[...]
    sp = jnp.maximum(-lam, 0.0) + jnp.log1p(jnp.exp(-jnp.abs(lam)))
    log_a = (-LRU_C) * r * sp
    a = jnp.exp(log_a)
    z = 2.0 * log_a
    one_m_a2 = jnp.where(z > -0.05, -z * (1.0 + z * (0.5 + z * (1.0 / 6.0 + z * (1.0 / 24.0)))), 1.0 - a * a)
    mult = jnp.sqrt(one_m_a2)
    return r, i, sp, a, mult


def _rglru_fwd(xc3, gates3, proj3, br, bi, lam, name):
    Bl, T, C = xc3.shape
    nsteps = int(math.log2(T))
    assert 1 << nsteps == T

    def body(xc_ref, rp_ref, ip_ref, ug_ref, br_ref, bi_ref, lam_ref, y_ref, h_ref):
        row = lax.broadcasted_iota(jnp.int32, (T, CT), 0)
        xc = xc_ref[0]
        r, i, sp, a, mult = _lru_gates(xc, rp_ref[0], ip_ref[0], br_ref, bi_ref, lam_ref)
        b = mult * (i * xc)
        for st in range(nsteps):
            s = 1 << st
            a_sh = jnp.where(row >= s, pltpu.roll(a, s, axis=0), 1.0)
            b = a * _shift_down(b, s, row) + b
            a = a * a_sh
        h_ref[0] = b
        y_ref[0] = (b * _gelu(ug_ref[0])[0]).astype(BF16)

    return pl.pallas_call(
        body, grid=(C // CT, Bl),
        in_specs=[_cspec(T), _cspec(T), _cspec(T, C // CT), _cspec(T), _pspec(1), _pspec(1), _pspec(1)],
        out_specs=(_cspec(T), _cspec(T)), out_shape=(S((Bl, T, C), BF16), S((Bl, T, C), F32)),
        compiler_params=_cparams("parallel", "arbitrary"), name=name)(xc3, gates3, gates3, proj3, br, bi, lam)


def _rglru_bwd(dy3, xc3, gates3, proj3, h3, br, bi, lam, name):
    Bl, T, C = xc3.shape
    nsteps = int(math.log2(T))

    def body(dy_ref, xc_ref, rp_ref, ip_ref, ug_ref, h_ref, br_ref, bi_ref, lam_ref,
             dxc_ref, drp_ref, dip_ref, dug_ref, dbr_ref, dbi_ref, dlam_ref):
        row = lax.broadcasted_iota(jnp.int32, (T, CT), 0)
        xc = xc_ref[0]
        r, i, sp, a, mult = _lru_gates(xc, rp_ref[0], ip_ref[0], br_ref, bi_ref, lam_ref)
        h = h_ref[0]
        dy = dy_ref[0]
        ug = ug_ref[0]
        gel, t = _gelu(ug)
        dug_ref[0] = (dy * h * _dgelu(ug, t)).astype(BF16)
        gacc = dy * gel
        an = _shift_up(a, 1, row)
        for st in range(nsteps):
            s = 1 << st
            an_sh = jnp.where(row < T - s, pltpu.roll(an, T - s, axis=0), 1.0)
            gacc = an * _shift_up(gacc, s, row) + gacc
            an = an * an_sh
        da = gacc * _shift_down(h, 1, row)
        ix = i * xc
        d_mult = gacc * ix
        d_i = gacc * mult * xc
        dxc_ref[0] = gacc * mult * i
        d_log_a = da * a - d_mult * (a * a) / mult
        d_r = d_log_a * ((-LRU_C) * sp)
        d_sp = jnp.sum(d_log_a * ((-LRU_C) * r), axis=0, keepdims=True)
        drp = d_r * r * (1.0 - r)
        dip = d_i * i * (1.0 - i)
        drp_ref[0] = drp.astype(BF16)
        dip_ref[0] = dip.astype(BF16)

        @pl.when(pl.program_id(1) == 0)
        def _():
            dbr_ref[...] = jnp.zeros_like(dbr_ref)
            dbi_ref[...] = jnp.zeros_like(dbi_ref)
            dlam_ref[...] = jnp.zeros_like(dlam_ref)

        dbr_ref[...] += jnp.sum(drp, axis=0, keepdims=True)
        dbi_ref[...] += jnp.sum(dip, axis=0, keepdims=True)
        dlam_ref[...] += d_sp * (-jax.nn.sigmoid(-lam_ref[...]))

    vec = S((1, C), F32)
    act = S((Bl, T, C), BF16)
    return pl.pallas_call(
        body, grid=(C // CT, Bl),
        in_specs=[_cspec(T), _cspec(T), _cspec(T), _cspec(T, C // CT), _cspec(T), _cspec(T)] + [_pspec(1)] * 3,
        out_specs=(_cspec(T), _cspec(T), _cspec(T), _cspec(T), _pspec(1), _pspec(1), _pspec(1)),
        out_shape=(S((Bl, T, C), F32), act, act, act, vec, vec, vec),
        compiler_params=_cparams("parallel", "arbitrary"), name=name)(dy3, xc3, gates3, gates3, proj3, h3, br, bi, lam)


NT = (((1,), (1,)), ((), ()))
TN = (((0,), (0,)), ((), ()))


def _hs(h):
    return slice(h * HEAD, (h + 1) * HEAD)


def _mem_softmax(qb, kb):
    s = lax.dot_general(qb, kb, NT, preferred_element_type=F32) * (HEAD ** -0.5)
    e = jnp.exp(s - jnp.max(s, axis=-1, keepdims=True))
    return e / jnp.sum(e, axis=-1, keepdims=True)


def _mem_attn_fwd(proj3, q_off, mkv3, name):
    Bl, T, _ = proj3.shape
    M = mkv3.shape[1]
    tq = _tile(T, (512, 256, 128))

    def body(q_ref, k_ref, v_ref, o_ref):
        q = q_ref[0].astype(BF16)
        k = k_ref[0].astype(BF16)
        v = v_ref[0].astype(BF16)
        outs = []
        for h in range(MEM_HEADS):
            p = _mem_softmax(q[:, _hs(h)], k[:, _hs(h)])
            outs.append(jnp.dot(p.astype(BF16), v[:, _hs(h)], preferred_element_type=F32))
        o_ref[0] = jnp.concatenate(outs, axis=-1).astype(BF16)

    return pl.pallas_call(
        body, grid=(Bl, T // tq),
        in_specs=[pl.BlockSpec((1, tq, MEM_W), lambda b, t: (b, t, q_off // MEM_W)),
                  pl.BlockSpec((1, M, MEM_W), lambda b, t: (b, 0, 0)),
                  pl.BlockSpec((1, M, MEM_W), lambda b, t: (b, 0, 1))],
        out_specs=pl.BlockSpec((1, tq, MEM_W), lambda b, t: (b, t, 0)),
        out_shape=S((Bl, T, MEM_W), BF16), compiler_params=_cparams("parallel", "parallel"), name=name)(
            proj3, mkv3, mkv3)


def _mem_attn_bwd(proj3, q_off, mkv3, do3, name):
    Bl, T, _ = proj3.shape
    M = mkv3.shape[1]
    tq = _tile(T, (512, 256, 128))
    scale = HEAD ** -0.5

    def body(q_ref, k_ref, v_ref, do_ref, dq_ref, dkv_ref):
        q = q_ref[0].astype(BF16)
        k = k_ref[0].astype(BF16)
        v = v_ref[0].astype(BF16)
        do = do_ref[0].astype(BF16)
        dqs, dks, dvs = [], [], []
        for h in range(MEM_HEADS):
            qh, kh, vh, doh = q[:, _hs(h)], k[:, _hs(h)], v[:, _hs(h)], do[:, _hs(h)]
            p = _mem_softmax(qh, kh)
            dvs.append(lax.dot_general(p.astype(BF16), doh, TN, preferred_element_type=F32))
            dp = lax.dot_general(doh, vh, NT, preferred_element_type=F32)
            ds = (p * (dp - jnp.sum(p * dp, axis=-1, keepdims=True)) * scale).astype(BF16)
            dqs.append(jnp.dot(ds, kh, preferred_element_type=F32))
            dks.append(lax.dot_general(ds, qh, TN, preferred_element_type=F32))
        dq_ref[0] = jnp.concatenate(dqs, axis=-1).astype(BF16)

        @pl.when(pl.program_id(1) == 0)
        def _():
            dkv_ref[...] = jnp.zeros_like(dkv_ref)

        dkv_ref[0] += jnp.concatenate(dks + dvs, axis=-1)

    return pl.pallas_call(
        body, grid=(Bl, T // tq),
        in_specs=[pl.BlockSpec((1, tq, MEM_W), lambda b, t: (b, t, q_off // MEM_W)),
                  pl.BlockSpec((1, M, MEM_W), lambda b, t: (b, 0, 0)),
                  pl.BlockSpec((1, M, MEM_W), lambda b, t: (b, 0, 1)),
                  pl.BlockSpec((1, tq, MEM_W), lambda b, t: (b, t, 0))],
        out_specs=(pl.BlockSpec((1, tq, MEM_W), lambda b, t: (b, t, 0)),
                   pl.BlockSpec((1, M, 2 * MEM_W), lambda b, t: (b, 0, 0))),
        out_shape=(S((Bl, T, MEM_W), BF16), S((Bl, M, 2 * MEM_W), F32)),
        compiler_params=_cparams("parallel", "arbitrary"), name=name)(proj3, mkv3, mkv3, do3)


def _swa_probs(qh, kph, kch, sink, slope, has_prev):
    qi = lax.broadcasted_iota(jnp.int32, (WIN, WIN), 0)
    kj = lax.broadcasted_iota(jnp.int32, (WIN, WIN), 1)
    scale = HEAD ** -0.5
    sp = lax.dot_general(qh, kph, NT, preferred_element_type=F32) * scale
    sc = lax.dot_general(qh, kch, NT, preferred_element_type=F32) * scale
    dist_p = (qi + WIN - kj).astype(F32)
    dist_c = (qi - kj).astype(F32)
    neg = -jnp.inf
    sp = jnp.where(kj > qi + jnp.where(has_prev, 0, WIN), sp - slope * dist_p, neg)
    sc = jnp.where(kj <= qi, sc - slope * dist_c, neg)
    m = jnp.maximum(jnp.maximum(jnp.max(sp, axis=-1, keepdims=True), jnp.max(sc, axis=-1, keepdims=True)), sink)
    ep = jnp.exp(sp - m)
    ec = jnp.exp(sc - m)
    es = jnp.exp(sink - m)
    inv = 1.0 / (jnp.sum(ep, axis=-1, keepdims=True) + jnp.sum(ec, axis=-1, keepdims=True) + es)
    return ep * inv, ec * inv, es * inv


def _swa_specs(nb):
    prev = lambda n: jnp.maximum(n - 1, 0)
    q = pl.BlockSpec((1, WIN, MIX_W), lambda b, n: (b, n, 0))
    kp = pl.BlockSpec((1, WIN, MEM_W), lambda b, n: (b, prev(n), 0))
    kc = pl.BlockSpec((1, WIN, MEM_W), lambda b, n: (b, n, 0))
    vp = pl.BlockSpec((1, WIN, MEM_W), lambda b, n: (b, prev(n), 1))
    vc = pl.BlockSpec((1, WIN, MEM_W), lambda b, n: (b, n, 1))
    sm = pl.BlockSpec(memory_space=pltpu.SMEM)
    return q, kp, kc, vp, vc, sm


def _swa_fwd(proj3, kv3, sinks, name):
    Bl, T, _ = proj3.shape
    nb = T // WIN
    q_s, kp_s, kc_s, vp_s, vc_s, sm = _swa_specs(nb)

    def body(q_ref, kp_ref, kc_ref, vp_ref, vc_ref, sink_ref, o_ref):
        has_prev = pl.program_id(1) > 0
        q = q_ref[0].astype(BF16)
        kp, kc = kp_ref[0].astype(BF16), kc_ref[0].astype(BF16)
        vp, vc = vp_ref[0].astype(BF16), vc_ref[0].astype(BF16)
        outs = []
        for h in range(SWA_HEADS):
            kvs = _hs(h // SWA_GROUP)
            pp, pc, _ = _swa_probs(q[:, _hs(h)], kp[:, kvs], kc[:, kvs], sink_ref[h], SLOPES[h], has_prev)
            outs.append(jnp.dot(pp.astype(BF16), vp[:, kvs], preferred_element_type=F32)
                        + jnp.dot(pc.astype(BF16), vc[:, kvs], preferred_element_type=F32))
        o_ref[0] = jnp.concatenate(outs, axis=-1).astype(BF16)

    return pl.pallas_call(
        body, grid=(Bl, nb), in_specs=[q_s, kp_s, kc_s, vp_s, vc_s, sm], out_specs=q_s,
        out_shape=S((Bl, T, MIX_W), BF16), compiler_params=_cparams("parallel", "parallel"), name=name)(
            proj3, kv3, kv3, kv3, kv3, sinks)


def _swa_bwd(proj3, kv3, sinks, do3, name):
    Bl, T, _ = proj3.shape
    nb = T // WIN
    q_s, kp_s, kc_s, vp_s, vc_s, sm = _swa_specs(nb)
    kv_s = pl.BlockSpec((1, WIN, 2 * MEM_W), lambda b, n: (b, n, 0))
    sk_s = pl.BlockSpec((8, LANES), lambda b, n: (0, 0))
    scale = HEAD ** -0.5

    def body(q_ref, kp_ref, kc_ref, vp_ref, vc_ref, sink_ref, do_ref, dq_ref, dkc_ref, dkp_ref, dsk_ref):
        has_prev = pl.program_id(1) > 0
        q = q_ref[0].astype(BF16)
        kp, kc = kp_ref[0].astype(BF16), kc_ref[0].astype(BF16)
        vp, vc = vp_ref[0].astype(BF16), vc_ref[0].astype(BF16)
        do = do_ref[0].astype(BF16)
        lane = lax.broadcasted_iota(jnp.int32, (8, LANES), 1)
        srow = lax.broadcasted_iota(jnp.int32, (8, LANES), 0)
        dsk = jnp.zeros((8, LANES), F32)
        dqs = []
        nkv = SWA_HEADS // SWA_GROUP
        dkc, dkp, dvc, dvp = [None] * nkv, [None] * nkv, [None] * nkv, [None] * nkv
        for h in range(SWA_HEADS):
            kvh = h // SWA_GROUP
            kvs = _hs(kvh)
            qh, doh = q[:, _hs(h)], do[:, _hs(h)]
            pp, pc, ps = _swa_probs(qh, kp[:, kvs], kc[:, kvs], sink_ref[h], SLOPES[h], has_prev)
            dpp = lax.dot_general(doh, vp[:, kvs], NT, preferred_element_type=F32)
            dpc = lax.dot_general(doh, vc[:, kvs], NT, preferred_element_type=F32)
            delta = jnp.sum(pp * dpp, axis=-1, keepdims=True) + jnp.sum(pc * dpc, axis=-1, keepdims=True)
            dsp = (pp * (dpp - delta) * scale).astype(BF16)
            dsc = (pc * (dpc - delta) * scale).astype(BF16)
            dqs.append(jnp.dot(dsp, kp[:, kvs], preferred_element_type=F32)
                       + jnp.dot(dsc, kc[:, kvs], preferred_element_type=F32))
            parts = (lax.dot_general(dsc, qh, TN, preferred_element_type=F32),
                     lax.dot_general(dsp, qh, TN, preferred_element_type=F32),
                     lax.dot_general(pc.astype(BF16), doh, TN, preferred_element_type=F32),
                     lax.dot_general(pp.astype(BF16), doh, TN, preferred_element_type=F32))
            for acc, part in zip((dkc, dkp, dvc, dvp), parts):
                acc[kvh] = part if acc[kvh] is None else acc[kvh] + part
            dsk = dsk + jnp.where((lane == h) & (srow == 0), -jnp.sum(ps * delta), 0.0)
        dq_ref[0] = jnp.concatenate(dqs, axis=-1).astype(BF16)
        dkc_ref[0] = jnp.concatenate(dkc + dvc, axis=-1)
        dkp_ref[0] = jnp.concatenate(dkp + dvp, axis=-1)

        @pl.when((pl.program_id(0) == 0) & (pl.program_id(1) == 0))
        def _():
            dsk_ref[...] = jnp.zeros_like(dsk_ref)

        dsk_ref[...] += dsk

    return pl.pallas_call(
        body, grid=(Bl, nb), in_specs=[q_s, kp_s, kc_s, vp_s, vc_s, sm, q_s], out_specs=(q_s, kv_s, kv_s, sk_s),
        out_shape=(S((Bl, T, MIX_W), BF16), S((Bl, T, 2 * MEM_W), F32), S((Bl, T, 2 * MEM_W), F32), S((8, LANES), F32)),
        compiler_params=_cparams("arbitrary", "arbitrary"), name=name)(proj3, kv3, kv3, kv3, kv3, sinks, do3)


def _kv_grad_combine(parts, name):
    Bl, T, W = parts[0][0].shape
    nb = T // WIN
    nl = len(parts)

    def body(*refs):
        o_ref = refs[-1]
        has_next = jnp.where(pl.program_id(1) == nb - 1, 0.0, 1.0)
        acc = None
        for l in range(nl):
            c = refs[2 * l][0] + has_next * refs[2 * l + 1][0]
            acc = c if acc is None else acc + c
        o_ref[0] = acc.astype(BF16)

    cur = pl.BlockSpec((1, WIN, W), lambda b, n: (b, n, 0))
    nxt = pl.BlockSpec((1, WIN, W), lambda b, n: (b, jnp.minimum(n + 1, nb - 1), 0))
    return pl.pallas_call(
        body, grid=(Bl, nb), in_specs=[cur, nxt] * nl, out_specs=cur, out_shape=S((Bl, T, W), BF16),
        compiler_params=_cparams("parallel", "parallel"), name=name)(*[a for pr in parts for a in pr])


def _loss_bwd(y, target, name="loss"):
    N, D = y.shape
    tm = _tile(N, (512, 256, 128))

    def body(y_ref, t_ref, dy_ref, l_ref):
        e = y_ref[...] - t_ref[...]
        dy_ref[...] = e * (1.0 / D)

        @pl.when(pl.program_id(0) == 0)
        def _():
            l_ref[...] = jnp.zeros_like(l_ref)

        l_ref[...] += jnp.sum(e * e, axis=0, keepdims=True) * (0.5 / D)

    row = pl.BlockSpec((tm, D), lambda i: (i, 0))
    vec = pl.BlockSpec((1, D), lambda i: (0, 0))
    return pl.pallas_call(
        body, grid=(N // tm,), in_specs=[row, row], out_specs=(row, vec), out_shape=(S((N, D), F32), S((1, D), F32)),
        compiler_params=_cparams("arbitrary"), name=name)(y, target)


def _all_gather(x, name):
    R, C = x.shape

    def body(x_ref, out_ref, send_sems, recv_sems, local_sem):
        mx, my, mc = lax.axis_index("x"), lax.axis_index("y"), lax.axis_index("c")
        me, sibling = (mx, my, mc), (mx, my, 1 - mc)
        chips = [(1 - mx, my), (mx, 1 - my), (1 - mx, 1 - my)]

        def rows(px, py, pc):
            return out_ref.at[4 * px + 2 * py + pc]

        def copy(kk, block, to, src=None):
            return pltpu.make_async_remote_copy(
                src_ref=rows(*block) if src is None else src, dst_ref=rows(*block), send_sem=send_sems.at[kk],
                recv_sem=recv_sems.at[kk], device_id=to, device_id_type=MESH)

        mine = pltpu.make_async_copy(x_ref, rows(*me), local_sem)
        mine.start()
        first = [copy(0, me, sibling, src=x_ref)]
        first += [copy(1 + j, me, (*chip, mc), src=x_ref) for j, chip in enumerate(chips)]
        for cp in first:
            cp.start()
        passed = [copy(4 + j, (*chip, mc), sibling) for j, chip in enumerate(chips)]
        for j, chip in enumerate(chips):
            copy(1 + j, (*chip, mc), me).wait_recv()
            passed[j].start()
        copy(0, sibling, me).wait_recv()
        for j, chip in enumerate(chips):
            copy(4 + j, (*chip, 1 - mc), me).wait_recv()
        for cp in first + passed:
            cp.wait_send()
        mine.wait()

    return pl.pallas_call(
        body, out_shape=S((N_DEV, R, C), x.dtype), in_specs=[ANY], out_specs=ANY,
        scratch_shapes=[pltpu.SemaphoreType.DMA((7,)), pltpu.SemaphoreType.DMA((7,)), pltpu.SemaphoreType.DMA(())],
        name=name)(x)


def _ag_weights(shards, row_sharded, name):
    n = len(shards)

    def full_shape(a, rows):
        if rows:
            return a.shape[:-2] + (N_DEV * a.shape[-2],) + a.shape[-1:]
        return (N_DEV,) + a.shape

    def body(*refs):
        x_refs, o_refs = refs[:n], refs[n:2 * n]
        send_sems, recv_sems, local_sems = refs[2 * n:]
        mx, my, mc = lax.axis_index("x"), lax.axis_index("y"), lax.axis_index("c")
        me, sibling = (mx, my, mc), (mx, my, 1 - mc)
        chips = [(1 - mx, my), (mx, 1 - my), (1 - mx, 1 - my)]

        def dst(t, px, py, pc):
            d = 4 * px + 2 * py + pc
            if not row_sharded[t]:
                return o_refs[t].at[d]
            r = shards[t].shape[-2]
            idx = (slice(None),) * (shards[t].ndim - 2) + (pl.ds(pl.multiple_of(d * r, 16), r), slice(None))
            return o_refs[t].at[idx]

        def copy(kk, t, block, to, src=None):
            return pltpu.make_async_remote_copy(
                src_ref=dst(t, *block) if src is None else src, dst_ref=dst(t, *block),
                send_sem=send_sems.at[kk * n + t], recv_sem=recv_sems.at[kk * n + t], device_id=to,
                device_id_type=MESH)

        mine = [pltpu.make_async_copy(x_refs[t], dst(t, *me), local_sems.at[t]) for t in range(n)]
        for cp in mine:
            cp.start()
        first = []
        for t in range(n):
            first.append(copy(0, t, me, sibling, src=x_refs[t]))
            first += [copy(1 + j, t, me, (*chip, mc), src=x_refs[t]) for j, chip in enumerate(chips)]
        for cp in first:
            cp.start()
        passed = []
        for j, chip in enumerate(chips):
            for t in range(n):
                copy(1 + j, t, (*chip, mc), me).wait_recv()
                cp = copy(4 + j, t, (*chip, mc), sibling)
                cp.start()
                passed.append(cp)
        for t in range(n):
            copy(0, t, sibling, me).wait_recv()
            for j, chip in enumerate(chips):
                copy(4 + j, t, (*chip, 1 - mc), me).wait_recv()
        for cp in first + passed:
            cp.wait_send()
        for cp in mine:
            cp.wait()

    return pl.pallas_call(
        body, out_shape=tuple(S(full_shape(a, r), a.dtype) for a, r in zip(shards, row_sharded)),
        in_specs=[ANY] * n, out_specs=tuple([ANY] * n),
        scratch_shapes=[pltpu.SemaphoreType.DMA((7 * n,)), pltpu.SemaphoreType.DMA((7 * n,)),
                        pltpu.SemaphoreType.DMA((n,))],
        name=name)(*shards)


def _rs_sibling(gs, name):
    n = len(gs)

    def body(*refs):
        g_refs, o_refs = refs[:n], refs[n:2 * n]
        send_sems, recv_sems = refs[2 * n:]
        mx, my, mc = lax.axis_index("x"), lax.axis_index("y"), lax.axis_index("c")
        copies = [pltpu.make_async_remote_copy(
            src_ref=g_refs[t].at[:, 2 * j + (1 - mc)], dst_ref=o_refs[t].at[j], send_sem=send_sems.at[j * n + t],
            recv_sem=recv_sems.at[j * n + t], device_id=(mx, my, 1 - mc), device_id_type=MESH)
            for t in range(n) for j in range(4)]
        for cp in copies:
            cp.start()
        for cp in copies:
            cp.wait_recv()
        for cp in copies:
            cp.wait_send()

    return pl.pallas_call(
        body, out_shape=tuple(S((4, g.shape[0]) + g.shape[2:], g.dtype) for g in gs), in_specs=[ANY] * n,
        out_specs=tuple([ANY] * n),
        scratch_shapes=[pltpu.SemaphoreType.DMA((4 * n,)), pltpu.SemaphoreType.DMA((4 * n,))], name=name)(*gs)


def _rs_chips(ps, name):
    n = len(ps)

    def body(*refs):
        p_refs, o_refs = refs[:n], refs[n:2 * n]
        send_sems, recv_sems = refs[2 * n:]
        mx, my, mc = lax.axis_index("x"), lax.axis_index("y"), lax.axis_index("c")
        chips = [(1 - mx, my), (mx, 1 - my), (1 - mx, 1 - my)]
        copies = [pltpu.make_async_remote_copy(
            src_ref=p_refs[t].at[2 * cx + cy], dst_ref=o_refs[t].at[j], send_sem=send_sems.at[j * n + t],
            recv_sem=recv_sems.at[j * n + t], device_id=(cx, cy, mc), device_id_type=MESH)
            for t in range(n) for j, (cx, cy) in enumerate(chips)]
        for cp in copies:
            cp.start()
        for cp in copies:
            cp.wait_recv()
        for cp in copies:
            cp.wait_send()

    return pl.pallas_call(
        body, out_shape=tuple(S((3,) + p.shape[1:], p.dtype) for p in ps), in_specs=[ANY] * n,
        out_specs=tuple([ANY] * n),
        scratch_shapes=[pltpu.SemaphoreType.DMA((3 * n,)), pltpu.SemaphoreType.DMA((3 * n,))], name=name)(*ps)


FLIPS = [(fx, fy, fc) for fx in (0, 1) for fy in (0, 1) for fc in (0, 1)][1:]
HBM = pl.BlockSpec(memory_space=pltpu.HBM)
SEM = pl.BlockSpec(memory_space=pltpu.SEMAPHORE)
EFFECT = pltpu.SideEffectType.DATAFLOW_SIDE_EFFECTING


def _hbm(a):
    return pltpu.with_memory_space_constraint(a, pltpu.HBM)


def _flips(gather):
    return [(0, 0, 0)] + FLIPS if gather else FLIPS


def _split_copies(gather, s_refs, l_refs, send_sems, recv_sems):
    n = len(s_refs)
    mx, my, mc = lax.axis_index("x"), lax.axis_index("y"), lax.axis_index("c")
    me = 4 * mx + 2 * my + mc
    copies = []
    for k, (fx, fy, fc) in enumerate(_flips(gather)):
        px, py, pc = (1 - mx if fx else mx), (1 - my if fy else my), (1 - mc if fc else mc)
        for t in range(n):
            if gather:
                src = s_refs[t]
                r = src.shape[0]
                dst = l_refs[t].at[pl.ds(pl.multiple_of(me * r, 16), r), :]
            else:
                src = s_refs[t].at[:, 4 * px + 2 * py + pc]
                dst = l_refs[t].at[k]
            copies.append(pltpu.make_async_remote_copy(
                src_ref=src, dst_ref=dst, send_sem=send_sems.at[k * n + t], recv_sem=recv_sems.at[k * n + t],
                device_id=(px, py, pc), device_id_type=MESH))
    return copies


def _split_start(gather, srcs, lands, after, name):
    n = len(srcs)
    n_sem = len(_flips(gather)) * n

    def body(*refs):
        s_refs, l_refs = refs[:n], refs[n:2 * n]
        send_sems, recv_sems = refs[2 * n + 1], refs[2 * n + 2]
        token = refs[-1]
        for cp in _split_copies(gather, s_refs, l_refs, send_sems, recv_sems):
            cp.start()
        token[...] = jnp.zeros_like(token)

    outs = pl.pallas_call(
        body, name=name,
        out_shape=(pltpu.SemaphoreType.DMA((n_sem,)), pltpu.SemaphoreType.DMA((n_sem,)))
        + tuple(pltpu.HBM(a.shape, a.dtype) for a in srcs) + tuple(pltpu.HBM(a.shape, a.dtype) for a in lands)
        + (S((8, LANES), F32),),
        in_specs=[HBM] * (2 * n) + [ANY],
        out_specs=(SEM, SEM) + (HBM,) * (2 * n) + (pl.BlockSpec(memory_space=pltpu.VMEM),),
        input_output_aliases={i: 2 + i for i in range(2 * n)},
        compiler_params=pltpu.CompilerParams(has_side_effects=EFFECT),
    )(*[_hbm(a) for a in srcs], *[_hbm(a) for a in lands], after)
    return outs[0], outs[1], list(outs[2:2 + n]), list(outs[2 + n:2 + 2 * n]), outs[-1]


def _split_wait(gather, send_sems, recv_sems, srcs, lands, after, name):
    n = len(srcs)

    def body(*refs):
        s_refs, l_refs = refs[:n], refs[n:2 * n]
        ssem, rsem = refs[2 * n], refs[2 * n + 1]
        copies = _split_copies(gather, s_refs, l_refs, ssem, rsem)
        for cp in copies:
            cp.wait_send()
        for cp in copies:
            cp.wait_recv()

    outs = pl.pallas_call(
        body, name=name,
        out_shape=tuple(pltpu.HBM(a.shape, a.dtype) for a in srcs) + tuple(pltpu.HBM(a.shape, a.dtype) for a in lands),
        in_specs=[HBM] * (2 * n) + [SEM, SEM, ANY], out_specs=(HBM,) * (2 * n),
        input_output_aliases={i: i for i in range(2 * n)},
        compiler_params=pltpu.CompilerParams(has_side_effects=EFFECT),
    )(*srcs, *lands, send_sems, recv_sems, after)
    return list(outs[n:])


def _rows_tile(b):
    return _tile(b, (512, 256, 128)) if b > 512 else b


def _pair_sum(g, got, name):
    A, _, B, C = g.shape
    tb = _rows_tile(B)
    core = lax.axis_index("c").astype(jnp.int32).reshape(1)

    def body(c_ref, g_ref, r_ref, o_ref):
        o_ref[...] = (g_ref[...].astype(F32) + r_ref[...].astype(F32)).astype(o_ref.dtype)

    return pl.pallas_call(
        body,
        grid_spec=pltpu.PrefetchScalarGridSpec(
            num_scalar_prefetch=1, grid=(4, A, B // tb),
            in_specs=[pl.BlockSpec((1, 1, tb, C), lambda j, a, i, c_ref: (a, 2 * j + c_ref[0], i, 0)),
                      pl.BlockSpec((1, 1, tb, C), lambda j, a, i, c_ref: (j, a, i, 0))],
            out_specs=pl.BlockSpec((1, 1, tb, C), lambda j, a, i, c_ref: (j, a, i, 0))),
        out_shape=S((4, A, B, C), g.dtype), compiler_params=_cparams("parallel", "parallel", "parallel"),
        name=name)(core, g, got)


def _adamw_math(w, g, m, v):
    m = ADAM_B1 * m + (1.0 - ADAM_B1) * g
    v = ADAM_B2 * v + (1.0 - ADAM_B2) * (g * g)
    m_hat = m / (1.0 - ADAM_B1 ** ADAM_STEP)
    v_hat = v / (1.0 - ADAM_B2 ** ADAM_STEP)
    delta = -ADAM_LR * (m_hat / (jnp.sqrt(v_hat) + ADAM_EPS) + ADAM_WD * w)
    return delta, m, v


def _adamw_sharded(p, got, w, m, v, name):
    A, B, C = w.shape
    tb = _rows_tile(B)
    chip = (2 * lax.axis_index("x") + lax.axis_index("y")).astype(jnp.int32).reshape(1)

    def body(c_ref, p_ref, got_ref, w_ref, m_ref, v_ref, g_out, d_out, m_out, v_out):
        g = p_ref[0].astype(F32)
        for j in range(3):
            g = g + got_ref[j].astype(F32)
        d, mn, vn = _adamw_math(w_ref[...], g, m_ref[...], v_ref[...])
        g_out[...] = g
        d_out[...] = d
        m_out[...] = mn
        v_out[...] = vn

    blk = pl.BlockSpec((1, tb, C), lambda a, i, c_ref: (a, i, 0))
    return pl.pallas_call(
        body,
        grid_spec=pltpu.PrefetchScalarGridSpec(
            num_scalar_prefetch=1, grid=(A, B // tb),
            in_specs=[pl.BlockSpec((1, 1, tb, C), lambda a, i, c_ref: (c_ref[0], a, i, 0)),
                      pl.BlockSpec((3, 1, tb, C), lambda a, i, c_ref: (0, a, i, 0)), blk, blk, blk],
            out_specs=(blk, blk, blk, blk)),
        out_shape=(S((A, B, C), F32),) * 4, compiler_params=_cparams("parallel", "parallel"), name=name)(
            chip, p, got, w, m, v)


def _adamw_layers(owns, gots, w, m, v, name):
    L, B, C = w.shape
    per_row = 2 * L * len(FLIPS) * C * owns[0].dtype.itemsize
    tb = max(t for t in range(16, B + 1, 16) if B % t == 0 and (t * per_row <= 16 * 1024 * 1024 or t == 16))
    me = (4 * lax.axis_index("x") + 2 * lax.axis_index("y") + lax.axis_index("c")).astype(jnp.int32).reshape(1)

    def body(me_ref, *refs):
        own_refs, got_refs = refs[:L], refs[L:2 * L]
        w_ref, m_ref, v_ref = refs[2 * L:2 * L + 3]
        g_out, d_out, m_out, v_out = refs[2 * L + 3:]
        layer = pl.program_id(0)
        for kk in range(L):
            @pl.when(layer == kk)
            def _():
                g = own_refs[kk][0].astype(F32)
                for s in range(len(FLIPS)):
                    g = g + got_refs[kk][s].astype(F32)
                d, mn, vn = _adamw_math(w_ref[...], g, m_ref[...], v_ref[...])
                g_out[...] = g
                d_out[...] = d
                m_out[...] = mn
                v_out[...] = vn

    def row(kk, layer, i):
        return jnp.where(layer == kk, i, 0)

    blk = pl.BlockSpec((1, tb, C), lambda layer, i, me_ref: (layer, i, 0))
    own_specs = [pl.BlockSpec((1, 1, tb, C), lambda layer, i, me_ref, kk=kk: (0, me_ref[0], row(kk, layer, i), 0))
                 for kk in range(L)]
    got_specs = [pl.BlockSpec((len(FLIPS), 1, tb, C), lambda layer, i, me_ref, kk=kk: (0, 0, row(kk, layer, i), 0))
                 for kk in range(L)]
    return pl.pallas_call(
        body,
        grid_spec=pltpu.PrefetchScalarGridSpec(
            num_scalar_prefetch=1, grid=(L, B // tb), in_specs=own_specs + got_specs + [blk, blk, blk],
            out_specs=(blk, blk, blk, blk)),
        out_shape=(S((L, B, C), F32),) * 4, compiler_params=_cparams("arbitrary", "arbitrary"), name=name)(
            me, *owns, *gots, w, m, v)


def _adamw_replicated(parts, w, m, v, name):
    R, C = w.shape
    rb = _tile(R, (512, 256, 128, 64, 32, 16, 8))

    def body(p_ref, w_ref, m_ref, v_ref, g_out, d_out, m_out, v_out):
        g = p_ref[0]
        for j in range(1, N_DEV):
            g = g + p_ref[j]
        d, mn, vn = _adamw_math(w_ref[...], g, m_ref[...], v_ref[...])
        g_out[...] = g
        d_out[...] = d
        m_out[...] = mn
        v_out[...] = vn

    blk = pl.BlockSpec((rb, C), lambda i: (i, 0))
    return pl.pallas_call(
        body, grid=(R // rb,), in_specs=[pl.BlockSpec((N_DEV, rb, C), lambda i: (0, i, 0)), blk, blk, blk],
        out_specs=(blk, blk, blk, blk), out_shape=(S((R, C), F32),) * 4, compiler_params=_cparams("parallel"),
        name=name)(parts, w, m, v)


def _pack(arrs, rows_mult, dtype):
    flat = jnp.concatenate([a.reshape(-1).astype(dtype) for a in arrs])
    n = flat.shape[0]
    per = rows_mult * LANES
    tot = -(-n // per) * per
    return jnp.pad(flat, (0, tot - n)).reshape(tot // LANES, LANES)


def _unpack(blob, shapes):
    flat = blob.reshape(-1)
    out, off = [], 0
    for shp in shapes:
        n = int(np.prod(shp))
        out.append(flat[off:off + n].reshape(shp))
        off += n
    return out


def _small_to_natural(g8):
    t = jnp.moveaxis(g8, 0, -2)
    return t.reshape(t.shape[:-2] + (N_DEV * t.shape[-1],))


def _small_to_cols(g):
    t = g.reshape(g.shape[:-1] + (N_DEV, g.shape[-1] // N_DEV))
    return jnp.moveaxis(t, -2, 0)


def _block_diag(w):
    nb, bs, _ = w.shape
    eye = jnp.eye(nb, dtype=w.dtype)
    return (eye[:, None, :, None] * w[:, :, None, :]).reshape(nb * bs, nb * bs)


def _diag_blocks(d, nb, bs):
    d4 = d.reshape(nb, bs, nb, bs)
    return jnp.stack([d4[i, :, i, :] for i in range(nb)])


def kernel(x, mem, g_mix_pre, g_mix_post, g_ffn_pre, g_ffn_post, g_mem, w_mem_kv, w_mix_out, w_ffn_up, w_ffn_conv, b_ffn_conv, w_ffn_down, w_in_a, w_conv_a, b_conv_a, w_rg_r, b_rg_r, w_rg_i, b_rg_i, lru_lambda, w_in_b, sinks_b, g_kv, w_kv, loss_target, m_g_mix_pre, m_g_mix_post, m_g_ffn_pre, m_g_ffn_post, m_g_mem, m_w_mem_kv, m_w_mix_out, m_w_ffn_up, m_w_ffn_conv, m_b_ffn_conv, m_w_ffn_down, m_w_in_a, m_w_conv_a, m_b_conv_a, m_w_rg_r, m_b_rg_r, m_w_rg_i, m_b_rg_i, m_lru_lambda, m_w_in_b, m_sinks_b, m_g_kv, m_w_kv, v_g_mix_pre, v_g_mix_post, v_g_ffn_pre, v_g_ffn_post, v_g_mem, v_w_mem_kv, v_w_mix_out, v_w_ffn_up, v_w_ffn_conv, v_b_ffn_conv, v_w_ffn_down, v_w_in_a, v_w_conv_a, v_b_conv_a, v_w_rg_r, v_b_rg_r, v_w_rg_i, v_b_rg_i, v_lru_lambda, v_w_in_b, v_sinks_b, v_g_kv, v_w_kv):
    w_loc = dict(g_mix_pre=g_mix_pre, g_mix_post=g_mix_post, g_ffn_pre=g_ffn_pre, g_ffn_post=g_ffn_post, g_mem=g_mem,
                 w_mem_kv=w_mem_kv, w_mix_out=w_mix_out, w_ffn_up=w_ffn_up, w_ffn_conv=w_ffn_conv,
                 b_ffn_conv=b_ffn_conv, w_ffn_down=w_ffn_down, w_in_a=w_in_a, w_conv_a=w_conv_a, b_conv_a=b_conv_a,
                 w_rg_r=w_rg_r, b_rg_r=b_rg_r, w_rg_i=w_rg_i, b_rg_i=b_rg_i, lru_lambda=lru_lambda, w_in_b=w_in_b,
                 sinks_b=sinks_b, g_kv=g_kv, w_kv=w_kv)
    m_loc = dict(g_mix_pre=m_g_mix_pre, g_mix_post=m_g_mix_post, g_ffn_pre=m_g_ffn_pre, g_ffn_post=m_g_ffn_post,
                 g_mem=m_g_mem, w_mem_kv=m_w_mem_kv, w_mix_out=m_w_mix_out, w_ffn_up=m_w_ffn_up,
                 w_ffn_conv=m_w_ffn_conv, b_ffn_conv=m_b_ffn_conv, w_ffn_down=m_w_ffn_down, w_in_a=m_w_in_a,
                 w_conv_a=m_w_conv_a, b_conv_a=m_b_conv_a, w_rg_r=m_w_rg_r, b_rg_r=m_b_rg_r, w_rg_i=m_w_rg_i,
                 b_rg_i=m_b_rg_i, lru_lambda=m_lru_lambda, w_in_b=m_w_in_b, sinks_b=m_sinks_b, g_kv=m_g_kv,
                 w_kv=m_w_kv)
    v_loc = dict(g_mix_pre=v_g_mix_pre, g_mix_post=v_g_mix_post, g_ffn_pre=v_g_ffn_pre, g_ffn_post=v_g_ffn_post,
                 g_mem=v_g_mem, w_mem_kv=v_w_mem_kv, w_mix_out=v_w_mix_out, w_ffn_up=v_w_ffn_up,
                 w_ffn_conv=v_w_ffn_conv, b_ffn_conv=v_b_ffn_conv, w_ffn_down=v_w_ffn_down, w_in_a=v_w_in_a,
                 w_conv_a=v_w_conv_a, b_conv_a=v_b_conv_a, w_rg_r=v_w_rg_r, b_rg_r=v_b_rg_r, w_rg_i=v_w_rg_i,
                 b_rg_i=v_b_rg_i, lru_lambda=v_lru_lambda, w_in_b=v_w_in_b, sinks_b=v_sinks_b, g_kv=v_g_kv,
                 w_kv=v_w_kv)

    Bl, T, D = x.shape
    Ml = mem.shape[1]
    N = Bl * T
    depth = g_mix_pre.shape[0]
    n_a = w_in_a.shape[0]
    F = w_ffn_down.shape[1] * N_DEV
    def as_rows(n, a):
        return jnp.swapaxes(a, -1, -2) if n in TRANSPOSED else a

    def layer_keys(l):
        keys = [("w_mem_kv", l), ("w_mix_out", l), ("w_ffn_up", l), ("w_ffn_down", l)]
        keys.append(("w_in_a", l) if l < n_a else ("w_in_b", l - n_a))
        if l == n_a:
            keys.append(("w_kv", None))
        return keys

    def shard_of(key):
        n, i = key
        return as_rows(n, w_loc[n] if i is None else w_loc[n][i]).astype(BF16)

    W = {}
    keys0 = layer_keys(0)
    got0 = _ag_weights([shard_of(kk) for kk in keys0] + [w_loc[n] for n in SMALL_SHARDED],
                       [True] * len(keys0) + [False] * len(SMALL_SHARDED), name="ag_weights_0")
    W.update(zip(keys0, got0))
    for n, a in zip(SMALL_SHARDED, got0[len(keys0):]):
        W[n] = _small_to_natural(a)

    def gather_start(l):
        keys = layer_keys(l)
        shards = [shard_of(kk) for kk in keys]
        lands = [lax.empty((N_DEV * s.shape[0],) + s.shape[1:], s.dtype) for s in shards]
        return (keys,) + _split_start(True, shards, lands, W[("w_mem_kv", l - 1)], name=f"ag_start_{l}")

    def gather_wait(l, pending, after):
        keys, ssem, rsem, srcs, lands, _ = pending
        W.update(zip(keys, _split_wait(True, ssem, rsem, srcs, lands, after, name=f"ag_wait_{l}")))

    nblk, bsz = w_rg_r.shape[1], w_rg_r.shape[2]
    wbd = [jnp.concatenate([_block_diag(w_rg_r[j]), _block_diag(w_rg_i[j])], axis=1).astype(BF16) for j in range(n_a)]

    def vec(a):
        return a.reshape(1, -1)

    x2 = x.reshape(N, D)
    mem2 = mem.reshape(Bl * Ml, D)
    saved = []
    kvn = kv3 = x_kv = None
    xs = x2
    for l in range(depth):
        sv = {"x0": xs}
        g_pre = vec(g_mix_pre[l])
        if l + 1 < depth:
            pending = gather_start(l + 1)
            g_pre = g_pre + pending[-1][0, 0]
        h1 = _rms_fwd(xs, g_pre, BF16, name=f"rms_mixpre_{l}")
        memn = _rms_fwd(mem2, vec(g_mem[l]), BF16, name=f"rms_mem_{l}")
        mkv3 = _mm(memn, W[("w_mem_kv", l)], name=f"mm_memkv_{l}").reshape(Bl, Ml, 2 * MEM_W)
        if l < n_a:
            j = l
            proj = _mm(h1, W[("w_in_a", j)], tb=True, name=f"mm_in_{l}")
            proj3 = proj.reshape(Bl, T, -1)
            xc3 = _conv_fwd_call(proj3, MIX_W, MIX_W, W["w_conv_a"][j], vec(W["b_conv_a"][j]), name=f"conv_a_{l}")
            gates3 = _mm(xc3.reshape(N, MIX_W), wbd[j], name=f"mm_gates_{l}").reshape(Bl, T, 2 * MIX_W)
            y_main3, hs3 = _rglru_fwd(xc3, gates3, proj3, vec(b_rg_r[j]), vec(b_rg_i[j]), vec(W["lru_lambda"][j]),
                                      name=f"rglru_fwd_{l}")
            q_off = 2 * MIX_W
            sv.update(xc3=xc3, gates3=gates3, hs3=hs3)
        else:
            j = l - n_a
            if l == n_a:
                x_kv = xs
                kvn = _rms_fwd(xs, vec(g_kv), BF16, name="rms_kv")
                kv3 = _mm(kvn, W[("w_kv", None)], name="mm_kv").reshape(Bl, T, 2 * MEM_W)
            proj = _mm(h1, W[("w_in_b", j)], name=f"mm_in_{l}")
            proj3 = proj.reshape(Bl, T, -1)
            y_main3 = _swa_fwd(proj3, kv3, sinks_b[j], name=f"swa_fwd_{l}")
            q_off = MIX_W
        y_mem3 = _mem_attn_fwd(proj3, q_off, mkv3, name=f"memattn_fwd_{l}")
        y_main = y_main3.reshape(N, MIX_W)
        y_mem = y_mem3.reshape(N, MEM_W)
        y = _mm(y_main, W[("w_mix_out", l)], n=D, k=MIX_W, name=f"mm_mixout_main_{l}")
        y = _mm(y_mem, W[("w_mix_out", l)], n=D, k=MEM_W, b_off=(MIX_W, 0), add=y, name=f"mm_mixout_mem_{l}")
        x1 = _rms_fwd(y, vec(g_mix_post[l]), F32, res=xs, name=f"rms_mixpost_{l}")
        h2 = _rms_fwd(x1, vec(g_ffn_pre[l]), BF16, name=f"rms_ffnpre_{l}")
        ug = _mm(h2, W[("w_ffn_up", l)], tb=True, n=F, name=f"mm_up_g_{l}")
        uv = _mm(h2, W[("w_ffn_up", l)], tb=True, n=F, b_off=(F, 0), name=f"mm_up_v_{l}")
        ug3, uv3 = ug.reshape(Bl, T, F), uv.reshape(Bl, T, F)
        act3 = _ffn_mid_fwd(ug3, uv3, W["w_ffn_conv"][l], vec(b_ffn_conv[l]), name=f"ffn_mid_fwd_{l}")
        act = act3.reshape(N, F)
        f = _mm(act, W[("w_ffn_down", l)], name=f"mm_down_{l}")
        x_next = _rms_fwd(f, vec(g_ffn_post[l]), F32, res=x1, name=f"rms_ffnpost_{l}")
        if l + 1 < depth:
            gather_wait(l + 1, pending, x_next)
        sv.update(h1=h1, memn=memn, mkv3=mkv3, proj3=proj3, q_off=q_off, y_main=y_main, y_mem=y_mem, y=y, x1=x1,
                  h2=h2, ug3=ug3, uv3=uv3, act=act, f=f)
        saved.append(sv)
        xs = x_next

    dxs, loss_vec = _loss_bwd(xs, loss_target.reshape(N, D))
    loss = lax.psum(jnp.sum(loss_vec), ("x", "y", "c"))

    G = {n: [None] * w_loc[n].shape[0] for n in REPL + SMALL_SHARDED if n != "g_kv"}
    GW = {}

    def dw(key, off, a, b_, nm):
        GW[key] = _mm(a, b_, ta=True, out_dtype=BF16, into=(GW.get(key), (1,) + W[key].shape, 0, off), name=nm)

    def grad_blocks(key):
        g = GW[key]
        return g.reshape(1, N_DEV, g.shape[1] // N_DEV, g.shape[2])

    reduces = []

    def reduce_start(keys, after, tag):
        srcs = [grad_blocks(kk) for kk in keys]
        lands = [lax.empty((len(FLIPS),) + s.shape[:1] + s.shape[2:], s.dtype) for s in srcs]
        started = _split_start(False, srcs, lands, after, name=f"rs_start_{tag}")
        reduces.append((keys, tag) + started)
        return started[-1]

    kv_parts = []
    for l in reversed(range(depth)):
        sv = saved[l]
        proj3 = sv["proj3"]
        df, dg = _rms_bwd(sv["f"], vec(g_ffn_post[l]), dxs, out_dtype=BF16, name=f"rmsb_ffnpost_{l}")
        G["g_ffn_post"][l] = dg[0]
        dact = _mm(df, W[("w_ffn_down", l)], tb=True, name=f"mmb_down_dx_{l}")
        dw(("w_ffn_down", l), (0, 0), sv["act"], df, f"mmb_down_dw_{l}")
        dug3, duv3, dwg, dwv, dbg, dbv = _ffn_mid_bwd(sv["ug3"], sv["uv3"], dact.reshape(Bl, T, F),
                                                      W["w_ffn_conv"][l], vec(b_ffn_conv[l]), name=f"ffn_mid_bwd_{l}")
        G["w_ffn_conv"][l] = jnp.concatenate([dwg, dwv], axis=1)
        G["b_ffn_conv"][l] = jnp.concatenate([dbg, dbv], axis=1)[0]
        dug, duv = dug3.reshape(N, F), duv3.reshape(N, F)
        dw(("w_ffn_up", l), (0, 0), dug, sv["h2"], f"mmb_up_dw_g_{l}")
        dw(("w_ffn_up", l), (F, 0), duv, sv["h2"], f"mmb_up_dw_v_{l}")
        tok = reduce_start([("w_ffn_down", l), ("w_ffn_up", l)], dug, f"ffn_{l}")
        dh2 = _mm(dug, W[("w_ffn_up", l)], n=D, k=F, after=tok, name=f"mmb_up_dx_g_{l}")
        dh2 = _mm(duv, W[("w_ffn_up", l)], n=D, k=F, b_off=(F, 0), add=dh2, name=f"mmb_up_dx_v_{l}")
        dx1, dg = _rms_bwd(sv["x1"], vec(g_ffn_pre[l]), dh2, add=dxs, name=f"rmsb_ffnpre_{l}")
        G["g_ffn_pre"][l] = dg[0]
        dy, dg = _rms_bwd(sv["y"], vec(g_mix_post[l]), dx1, out_dtype=BF16, name=f"rmsb_mixpost_{l}")
        G["g_mix_post"][l] = dg[0]
        dy_main = _mm(dy, W[("w_mix_out", l)], tb=True, n=MIX_W, k=D, name=f"mmb_mixout_dmain_{l}")
        dy_mem = _mm(dy, W[("w_mix_out", l)], tb=True, n=MEM_W, k=D, b_off=(MIX_W, 0),
                     name=f"mmb_mixout_dmem_{l}")
        dw(("w_mix_out", l), (0, 0), sv["y_main"], dy, f"mmb_mixout_dw_main_{l}")
        dw(("w_mix_out", l), (MIX_W, 0), sv["y_mem"], dy, f"mmb_mixout_dw_mem_{l}")
        dq_mem3, dmkv3 = _mem_attn_bwd(proj3, sv["q_off"], sv["mkv3"], dy_mem.reshape(Bl, T, MEM_W),
                                       name=f"memattn_bwd_{l}")
        dq_mem = dq_mem3.reshape(N, MEM_W)
        dmkv = dmkv3.reshape(Bl * Ml, 2 * MEM_W)
        dw(("w_mem_kv", l), (0, 0), sv["memn"], dmkv, f"mmb_memkv_dw_{l}")
        dmemn = _mm(dmkv, W[("w_mem_kv", l)], tb=True, name=f"mmb_memkv_dx_{l}")
        _, dg = _rms_bwd(mem2, vec(g_mem[l]), dmemn, name=f"rmsb_mem_{l}")
        G["g_mem"][l] = dg[0]
        dy_main3 = dy_main.reshape(Bl, T, MIX_W)
        if l < n_a:
            j = l
            dxc3, drp3, dip3, dugate3, dbr, dbi, dlam = _rglru_bwd(
                dy_main3, sv["xc3"], sv["gates3"], proj3, sv["hs3"], vec(b_rg_r[j]), vec(b_rg_i[j]),
                vec(W["lru_lambda"][j]), name=f"rglru_bwd_{l}")
            G["b_rg_r"][j] = dbr.reshape(nblk, bsz)
            G["b_rg_i"][j] = dbi.reshape(nblk, bsz)
            G["lru_lambda"][j] = dlam[0]
            drp, dip = drp3.reshape(N, MIX_W), dip3.reshape(N, MIX_W)
            xc2 = sv["xc3"].reshape(N, MIX_W)
            G["w_rg_r"][j] = _diag_blocks(_mm(xc2, drp, ta=True, name=f"mmb_gates_dw_r_{l}"), nblk, bsz)
            G["w_rg_i"][j] = _diag_blocks(_mm(xc2, dip, ta=True, name=f"mmb_gates_dw_i_{l}"), nblk, bsz)
            dxc = _mm(drp, wbd[j], tb=True, n=MIX_W, k=MIX_W, add=dxc3.reshape(N, MIX_W), name=f"mmb_gates_dx_r_{l}")
            dxc = _mm(dip, wbd[j], tb=True, n=MIX_W, k=MIX_W, b_off=(0, MIX_W), add=dxc, name=f"mmb_gates_dx_i_{l}")
            dux3, dwc, dbc = _conv_bwd_call(dxc.reshape(Bl, T, MIX_W), proj3, MIX_W, MIX_W, W["w_conv_a"][j],
                                            name=f"conv_a_bwd_{l}")
            G["w_conv_a"][j] = dwc
            G["b_conv_a"][j] = dbc[0]
            pieces = [(dugate3.reshape(N, MIX_W), 0), (dux3.reshape(N, MIX_W), MIX_W), (dq_mem, 2 * MIX_W)]
            in_key = ("w_in_a", j)
        else:
            j = l - n_a
            dq3, dkc, dkp, dsk = _swa_bwd(proj3, kv3, sinks_b[j], dy_main3, name=f"swa_bwd_{l}")
            kv_parts.append((dkc, dkp))
            G["sinks_b"][j] = dsk[0, :SWA_HEADS]
            pieces = [(dq3.reshape(N, MIX_W), 0), (dq_mem, MIX_W)]
            in_key = ("w_in_b", j)
        in_t = in_key[0] in TRANSPOSED
        for pi, (piece, off) in enumerate(pieces):
            if in_t:
                dw(in_key, (off, 0), piece, sv["h1"], f"mmb_in_dw_{pi}_{l}")
            else:
                dw(in_key, (0, off), sv["h1"], piece, f"mmb_in_dw_{pi}_{l}")
        tok = reduce_start([("w_mix_out", l), ("w_mem_kv", l), in_key], dy, f"mix_{l}")
        dh1 = None
        for pi, (piece, off) in enumerate(pieces):
            dh1 = _mm(piece, W[in_key], tb=not in_t, n=D, k=piece.shape[1], b_off=(off, 0) if in_t else (0, off),
                      add=dh1, after=tok if pi == 0 else None, name=f"mmb_in_dx_{pi}_{l}")
        dxs, dg = _rms_bwd(sv["x0"], vec(g_mix_pre[l]), dh1, add=dx1, name=f"rmsb_mixpre_{l}")
        G["g_mix_pre"][l] = dg[0]
        if l == n_a:
            dkv = _kv_grad_combine(kv_parts, name="kv_grad_combine").reshape(N, 2 * MEM_W)
            dw(("w_kv", None), (0, 0), kvn, dkv, "mmb_kv_dw")
            tok = reduce_start([("w_kv", None)], dkv, "kv")
            dkvn = _mm(dkv, W[("w_kv", None)], tb=True, after=tok, name="mmb_kv_dx")
            dxs, dg = _rms_bwd(x_kv, vec(g_kv), dkvn, add=dxs, name="rmsb_kv")
            G["g_kv"] = dg[0]
    grad_x = dxs.reshape(Bl, T, D)
    Gf = {n: (jnp.stack(g) if isinstance(g, list) else g) for n, g in G.items()}

    parts = {}
    for keys, tag, ssem, rsem, srcs, lands, _ in reduces:
        for kk, s, g7 in zip(keys, srcs, _split_wait(False, ssem, rsem, srcs, lands, dxs, name=f"rs_wait_{tag}")):
            parts[kk] = (s, g7)
    g4 = []
    for n in SMALL_SHARDED:
        t = _small_to_cols(Gf[n]).astype(BF16)
        g4.append(t.reshape(1, N_DEV, -1, t.shape[-1]))
    got = _rs_sibling(g4, name="rs_sibling")
    psum4 = [_pair_sum(g, r, name=f"rs_pair_sum_{n}") for n, g, r in zip(SMALL_SHARDED, g4, got)]
    got2 = _rs_chips(psum4, name="rs_chips")
    r_blob = _pack([Gf[n].astype(F32) for n in REPL], REPL_ROWS, F32)
    r_parts = _all_gather(r_blob, name="ag_repl_grads")

    res = [{} for _ in range(4)]
    for n, p4, g2 in zip(SMALL_SHARDED, psum4, got2):
        shp3 = p4.shape[1:]
        outs = _adamw_sharded(p4, g2, w_loc[n].reshape(shp3), m_loc[n].reshape(shp3), v_loc[n].reshape(shp3),
                              name=f"adamw_{n}")
        for k in range(4):
            res[k][n] = outs[k].reshape(w_loc[n].shape)
    for n, _ in SHARDED:
        if n in SMALL_SHARDED:
            continue
        idx = [None] if w_loc[n].ndim == 2 else list(range(w_loc[n].shape[0]))
        wmv = [as_rows(n, a[n]) for a in (w_loc, m_loc, v_loc)]
        shp3 = (len(idx),) + wmv[0].shape[-2:]
        outs = _adamw_layers([parts[(n, i)][0] for i in idx], [parts[(n, i)][1] for i in idx],
                             *[a.reshape(shp3) for a in wmv], name=f"adamw_{n}")
        for k in range(4):
            res[k][n] = as_rows(n, outs[k].reshape(wmv[0].shape))
    outs_rp = _adamw_replicated(r_parts, _pack([w_loc[n] for n in REPL], REPL_ROWS, F32),
                                _pack([m_loc[n] for n in REPL], REPL_ROWS, F32),
                                _pack([v_loc[n] for n in REPL], REPL_ROWS, F32),
                                name="adamw_replicated")
    rp_shapes = [w_loc[n].shape for n in REPL]
    for k in range(4):
        res[k].update(zip(REPL, _unpack(outs_rp[k], rp_shapes)))
    out = [loss, grad_x]
    for k in range(4):
        out += [res[k][n] for n in WEIGHTS]
    return tuple(out)
```

```python
import functools
import math

import numpy as np
import jax
import jax.numpy as jnp
from jax import lax
from jax.experimental import pallas as pl
from jax.experimental.pallas import tpu as pltpu

F32 = jnp.float32
BF16 = jnp.bfloat16
S = jax.ShapeDtypeStruct
MESH = pl.DeviceIdType.MESH
ANY = pl.BlockSpec(memory_space=pl.ANY)

HEAD = 64
MEM_HEADS = 4
MEM_W = MEM_HEADS * HEAD
SWA_HEADS = 12
SWA_GROUP = 3
MIX_W = SWA_HEADS * HEAD
WIN = 128
LRU_C = 8.0
EPS = 1e-6
ADAM_LR, ADAM_B1, ADAM_B2, ADAM_EPS, ADAM_WD, ADAM_STEP = 0.001, 0.9, 0.999, 1e-08, 0.01, 10
GELU_C0 = math.sqrt(2.0 / math.pi)
GELU_C1 = 0.044715
N_DEV = 8
LANES = 128
CT = 128
VMEM_LIMIT = 48 * 1024 * 1024
MM_VMEM_BUDGET = 36 * 1024 * 1024
REPL_ROWS = 256

SHARDED = (("w_mem_kv", 1), ("w_mix_out", 1), ("w_ffn_up", 2), ("w_ffn_conv", 2), ("w_ffn_down", 1), ("w_in_a", 2),
           ("w_conv_a", 2), ("b_conv_a", 1), ("lru_lambda", 1), ("w_in_b", 1), ("w_kv", 0))
SMALL_SHARDED = ("w_ffn_conv", "w_conv_a", "b_conv_a", "lru_lambda")
TRANSPOSED = ("w_ffn_up", "w_in_a")
REPL = ("g_mix_pre", "g_mix_post", "g_ffn_pre", "g_ffn_post", "g_mem", "b_ffn_conv", "w_rg_r", "b_rg_r", "w_rg_i",
        "b_rg_i", "sinks_b", "g_kv")
WEIGHTS = ("g_mix_pre", "g_mix_post", "g_ffn_pre", "g_ffn_post", "g_mem", "w_mem_kv", "w_mix_out", "w_ffn_up",
           "w_ffn_conv", "b_ffn_conv", "w_ffn_down", "w_in_a", "w_conv_a", "b_conv_a", "w_rg_r", "b_rg_r", "w_rg_i",
           "b_rg_i", "lru_lambda", "w_in_b", "sinks_b", "g_kv", "w_kv")


def _alibi_slopes(n):
    def pow2(m):
        start = 2.0 ** (-8.0 / m)
        return [start ** (i + 1) for i in range(m)]
    c = 2 ** int(math.floor(math.log2(n)))
    s = pow2(c)
    if c != n:
        s = s + pow2(2 * c)[0::2][: n - c]
    return [float(v) for v in np.asarray(s, dtype=np.float32)]


SLOPES = _alibi_slopes(SWA_HEADS)


def _tile(n, cands):
    for c in cands:
        if n % c == 0:
            return c
    return n


def _cparams(*sem):
    return pltpu.CompilerParams(dimension_semantics=sem, vmem_limit_bytes=VMEM_LIMIT)


def _mm_tiles(M, N, K, a_bytes, b_bytes, o_bytes, add_bytes, offsets):
    m_off, n_offs, k_off = offsets
    tms = [c for c in (1024, 512, 256, 128) if M % c == 0 and m_off % c == 0] or [M]
    tns = [c for c in (1408, 1024, 896, 768, 512, 384, 256, 128)
           if N % c == 0 and all(o % c == 0 for o in n_offs)] or [N]
    tks = [c for c in (K, 2048, 1408, 1024, 512, 256, 128) if c <= K and K % c == 0 and k_off % c == 0]
    best = None
    for tk in tks:
        fits = []
        for tm in tms:
            for tn in tns:
                need = 2 * (tm * tk * a_bytes + tk * tn * b_bytes + tm * tn * (o_bytes + add_bytes))
                need += tm * tn * 4 * (2 if tk < K else 1)
                need += (tm * tk * 2 if a_bytes != 2 else 0) + (tk * tn * 2 if b_bytes != 2 else 0)
                if need <= MM_VMEM_BUDGET:
                    fits.append((tm * tn, min(tm, 512), tm, tn))
        if fits:
            _, _, tm, tn = max(fits)
            best = (tm, tn, tk)
            break
    assert best is not None, (M, N, K)
    return best


def _mm(a, b, *, ta=False, tb=False, n=None, k=None, b_off=(0, 0), out_dtype=F32, add=None, into=None, after=None,
        name="mm"):
    if ta:
        K, M = a.shape
    else:
        M, K = a.shape
    if tb:
        N = b.shape[-2] if n is None else n
    else:
        N = b.shape[-1] if n is None else n
    assert k is None or k == K
    ro, co = b_off
    n_off, k_off = (ro, co) if tb else (co, ro)
    oro, oco = (0, 0) if into is None else into[3]
    tm, tn, tk = _mm_tiles(M, N, K, a.dtype.itemsize, b.dtype.itemsize, jnp.dtype(out_dtype).itemsize,
                           0 if add is None else add.dtype.itemsize, (oro, (n_off, oco), k_off))
    nk = K // tk
    if tb:
        b_spec = pl.BlockSpec((tn, tk), lambda i, j, kk: (j + ro // tn, kk + co // tk))
        b_dims = (1,)
    else:
        b_spec = pl.BlockSpec((tk, tn), lambda i, j, kk: (kk + ro // tk, j + co // tn))
        b_dims = (0,)
    if ta:
        a_spec = pl.BlockSpec((tk, tm), lambda i, j, kk: (kk, i))
        a_dims = (0,)
    else:
        a_spec = pl.BlockSpec((tm, tk), lambda i, j, kk: (i, kk))
        a_dims = (1,)
    dims = ((a_dims, b_dims), ((), ()))
    add_spec = pl.BlockSpec((tm, tn), lambda i, j, kk: (i, j))
    has_add = add is not None
    if into is None:
        o_spec, o_shape, buf = add_spec, (M, N), None
    else:
        buf, o_shape, ol, _ = into
        assert not has_add
        o_spec = pl.BlockSpec((None, tm, tn), lambda i, j, kk: (ol, i + oro // tm, j + oco // tn))
    has_buf = buf is not None

    def body(*refs):
        refs = list(refs)
        acc_ref = refs.pop() if nk > 1 else None
        o_ref = refs.pop()
        a_ref, b_ref = refs[0], refs[1]
        add_ref = refs[2] if has_add else None
        part = lax.dot_general(a_ref[...].astype(BF16), b_ref[...].astype(BF16), dims, preferred_element_type=F32)

        def finish(r):
            if has_add:
                r = r + add_ref[...].astype(F32)
            o_ref[...] = r.astype(out_dtype)

        if nk == 1:
            finish(part)
        else:
            kk = pl.program_id(2)

            @pl.when(kk == 0)
            def _():
                acc_ref[...] = part

            @pl.when(kk > 0)
            def _():
                acc_ref[...] += part

            @pl.when(kk == nk - 1)
            def _():
                finish(acc_ref[...])

    in_specs = [a_spec, b_spec] + ([add_spec] if has_add else []) + ([ANY] if has_buf else [])
    args = (a, b) + ((add,) if has_add else ()) + ((buf,) if has_buf else ())
    if after is not None:
        in_specs, args = in_specs + [ANY], args + (after,)
    return pl.pallas_call(
        body, grid=(M // tm, N // tn, nk), in_specs=in_specs, out_specs=o_spec,
        out_shape=S(o_shape, out_dtype), scratch_shapes=[pltpu.VMEM((tm, tn), F32)] if nk > 1 else [],
        input_output_aliases={2: 0} if has_buf else {},
        compiler_params=_cparams("parallel", "parallel", "arbitrary"), name=name)(*args)


def _rms_fwd(x, g, out_dtype, res=None, name="rms_fwd"):
    N, D = x.shape
    tm = _tile(N, (512, 256, 128))
    has_res = res is not None

    def body(*refs):
        if has_res:
            x_ref, g_ref, r_ref, o_ref = refs
        else:
            x_ref, g_ref, o_ref = refs
        xv = x_ref[...].astype(F32)
        y = xv * lax.rsqrt(jnp.mean(xv * xv, axis=-1, keepdims=True) + EPS) * g_ref[...]
        if has_res:
            y = y + r_ref[...]
        o_ref[...] = y.astype(out_dtype)

    row = pl.BlockSpec((tm, D), lambda i: (i, 0))
    vec = pl.BlockSpec((1, D), lambda i: (0, 0))
    return pl.pallas_call(
        body, grid=(N // tm,), in_specs=[row, vec] + ([row] if has_res else []), out_specs=row,
        out_shape=S((N, D), out_dtype), compiler_params=_cparams("parallel"), name=name)(
            *((x, g) + ((res,) if has_res else ())))


def _rms_bwd(x, g, dy, add=None, out_dtype=F32, name="rms_bwd"):
    N, D = x.shape
    tm = _tile(N, (512, 256, 128))
    has_add = add is not None

    def body(*refs):
        if has_add:
            x_ref, g_ref, dy_ref, add_ref, dx_ref, dg_ref = refs
        else:
            x_ref, g_ref, dy_ref, dx_ref, dg_ref = refs
        xv = x_ref[...].astype(F32)
        dyv = dy_ref[...].astype(F32)
        r = lax.rsqrt(jnp.mean(xv * xv, axis=-1, keepdims=True) + EPS)
        u = dyv * g_ref[...]
        dx = r * u - xv * (r * r * r * jnp.mean(u * xv, axis=-1, keepdims=True))
        if has_add:
            dx = dx + add_ref[...]
        dx_ref[...] = dx.astype(out_dtype)

        @pl.when(pl.program_id(0) == 0)
        def _():
            dg_ref[...] = jnp.zeros_like(dg_ref)

        dg_ref[...] += jnp.sum(dyv * xv * r, axis=0, keepdims=True)

    row = pl.BlockSpec((tm, D), lambda i: (i, 0))
    vec = pl.BlockSpec((1, D), lambda i: (0, 0))
    return pl.pallas_call(
        body, grid=(N // tm,), in_specs=[row, vec, row] + ([row] if has_add else []), out_specs=(row, vec),
        out_shape=(S((N, D), out_dtype), S((1, D), F32)), compiler_params=_cparams("arbitrary"), name=name)(
            *((x, g, dy) + ((add,) if has_add else ())))


def _shift_down(x, s, row):
    return jnp.where(row >= s, pltpu.roll(x, s, axis=0), 0.0)


def _shift_up(x, s, row):
    T = x.shape[0]
    return jnp.where(row < T - s, pltpu.roll(x, T - s, axis=0), 0.0)


def _conv(x, w_ref, b_ref, row):
    W = w_ref.shape[0]
    y = x * w_ref[W - 1:W, :] + b_ref[...]
    for s in range(1, W):
        y = y + _shift_down(x, s, row) * w_ref[W - 1 - s:W - s, :]
    return y


def _conv_bwd(dy, x, w_ref, row):
    W = w_ref.shape[0]
    dx = dy * w_ref[W - 1:W, :]
    dws = [None] * W
    dws[W - 1] = jnp.sum(dy * x, axis=0, keepdims=True)
    for s in range(1, W):
        dx = dx + _shift_up(dy, s, row) * w_ref[W - 1 - s:W - s, :]
        dws[W - 1 - s] = jnp.sum(dy * _shift_down(x, s, row), axis=0, keepdims=True)
    return dx, jnp.concatenate(dws, axis=0), jnp.sum(dy, axis=0, keepdims=True)


def _gelu(g):
    t = jnp.tanh(GELU_C0 * (g + GELU_C1 * g * g * g))
    return 0.5 * g * (1.0 + t), t


def _dgelu(g, t):
    return 0.5 * (1.0 + t) + 0.5 * g * (1.0 - t * t) * (GELU_C0 * (1.0 + 3.0 * GELU_C1 * g * g))


def _cspec(T, off=0):
    return pl.BlockSpec((1, T, CT), lambda j, b: (b, 0, j + off))


def _pspec(rows, off=0):
    return pl.BlockSpec((rows, CT), lambda j, b: (0, j + off))


def _conv_fwd_call(x3, x_off, C, w, b, name):
    Bl, T, _ = x3.shape
    W = w.shape[0]

    def body(x_ref, w_ref, b_ref, o_ref):
        row = lax.broadcasted_iota(jnp.int32, (T, CT), 0)
        o_ref[0] = _conv(x_ref[0], w_ref, b_ref, row)

    return pl.pallas_call(
        body, grid=(C // CT, Bl), in_specs=[_cspec(T, x_off // CT), _pspec(W), _pspec(1)], out_specs=_cspec(T),
        out_shape=S((Bl, T, C), F32), compiler_params=_cparams("parallel", "arbitrary"), name=name)(x3, w, b)


def _conv_bwd_call(dy3, x3, x_off, C, w, name):
    Bl, T, _ = x3.shape
    W = w.shape[0]

    def body(dy_ref, x_ref, w_ref, dx_ref, dw_ref, db_ref):
        row = lax.broadcasted_iota(jnp.int32, (T, CT), 0)
        dx, dw, db = _conv_bwd(dy_ref[0], x_ref[0], w_ref, row)
        dx_ref[0] = dx.astype(BF16)

        @pl.when(pl.program_id(1) == 0)
        def _():
            dw_ref[...] = jnp.zeros_like(dw_ref)
            db_ref[...] = jnp.zeros_like(db_ref)

        dw_ref[...] += dw
        db_ref[...] += db

    return pl.pallas_call(
        body, grid=(C // CT, Bl), in_specs=[_cspec(T), _cspec(T, x_off // CT), _pspec(W)],
        out_specs=(_cspec(T), _pspec(W), _pspec(1)),
        out_shape=(S((Bl, T, C), BF16), S((W, C), F32), S((1, C), F32)),
        compiler_params=_cparams("parallel", "arbitrary"), name=name)(dy3, x3, w)


def _ffn_mid_fwd(ug3, uv3, wc, bc, name):
    Bl, T, F = ug3.shape
    nf = F // CT

    def body(ug_ref, uv_ref, wg_ref, wv_ref, bg_ref, bv_ref, o_ref):
        row = lax.broadcasted_iota(jnp.int32, (T, CT), 0)
        g = _conv(ug_ref[0], wg_ref, bg_ref, row)
        v = _conv(uv_ref[0], wv_ref, bv_ref, row)
        o_ref[0] = (_gelu(g)[0] * v).astype(BF16)

    return pl.pallas_call(
        body, grid=(nf, Bl),
        in_specs=[_cspec(T), _cspec(T), _pspec(3), _pspec(3, nf), _pspec(1), _pspec(1, nf)], out_specs=_cspec(T),
        out_shape=S((Bl, T, F), BF16), compiler_params=_cparams("parallel", "arbitrary"), name=name)(
            ug3, uv3, wc, wc, bc, bc)


def _ffn_mid_bwd(ug3, uv3, dact3, wc, bc, name):
    Bl, T, F = ug3.shape
    nf = F // CT

    def body(ug_ref, uv_ref, da_ref, wg_ref, wv_ref, bg_ref, bv_ref, dug_ref, duv_ref, dwg_ref, dwv_ref, dbg_ref,
             dbv_ref):
        row = lax.broadcasted_iota(jnp.int32, (T, CT), 0)
        ug = ug_ref[0]
        uv = uv_ref[0]
        g = _conv(ug, wg_ref, bg_ref, row)
        v = _conv(uv, wv_ref, bv_ref, row)
        da = da_ref[0]
        gel, t = _gelu(g)
        dg = da * v * _dgelu(g, t)
        dv = da * gel
        dug, dwg, dbg = _conv_bwd(dg, ug, wg_ref, row)
        duv, dwv, dbv = _conv_bwd(dv, uv, wv_ref, row)
        dug_ref[0] = dug.astype(BF16)
        duv_ref[0] = duv.astype(BF16)

        @pl.when(pl.program_id(1) == 0)
        def _():
            dwg_ref[...] = jnp.zeros_like(dwg_ref)
            dwv_ref[...] = jnp.zeros_like(dwv_ref)
            dbg_ref[...] = jnp.zeros_like(dbg_ref)
            dbv_ref[...] = jnp.zeros_like(dbv_ref)

        dwg_ref[...] += dwg
        dwv_ref[...] += dwv
        dbg_ref[...] += dbg
        dbv_ref[...] += dbv

    return pl.pallas_call(
        body, grid=(nf, Bl),
        in_specs=[_cspec(T), _cspec(T), _cspec(T), _pspec(3), _pspec(3, nf), _pspec(1), _pspec(1, nf)],
        out_specs=(_cspec(T), _cspec(T), _pspec(3), _pspec(3), _pspec(1), _pspec(1)),
        out_shape=(S((Bl, T, F), BF16), S((Bl, T, F), BF16), S((3, F), F32), S((3, F), F32), S((1, F), F32),
                   S((1, F), F32)),
        compiler_params=_cparams("parallel", "arbitrary"), name=name)(ug3, uv3, dact3, wc, wc, bc, bc)


def _lru_gates(xc, rp, ip, br_ref, bi_ref, lam_ref):
    r = jax.nn.sigmoid(rp + br_ref[...])
    i = jax.nn.sigmoid(ip + bi_ref[...])
    lam = lam_ref[...]
    sp = jnp.maximum(-lam, 0.0) + jnp.log1p(jnp.exp(-jnp.abs(lam)))
    log_a = (-LRU_C) * r * sp
    a = jnp.exp(log_a)
    z = 2.0 * log_a
    one_m_a2 = jnp.where(z > -0.05, -z * (1.0 + z * (0.5 + z * (1.0 / 6.0 + z * (1.0 / 24.0)))), 1.0 - a * a)
    mult = jnp.sqrt(one_m_a2)
    return r, i, sp, a, mult


def _rglru_fwd(xc3, gates3, proj3, br, bi, lam, name):
    Bl, T, C = xc3.shape
    nsteps = int(math.log2(T))
    assert 1 << nsteps == T

    def body(xc_ref, rp_ref, ip_ref, ug_ref, br_ref, bi_ref, lam_ref, y_ref, h_ref):
        row = lax.broadcasted_iota(jnp.int32, (T, CT), 0)
        xc = xc_ref[0]
        r, i, sp, a, mult = _lru_gates(xc, rp_ref[0], ip_ref[0], br_ref, bi_ref, lam_ref)
        b = mult * (i * xc)
        for st in range(nsteps):
            s = 1 << st
            a_sh = jnp.where(row >= s, pltpu.roll(a, s, axis=0), 1.0)
            b = a * _shift_down(b, s, row) + b
            a = a * a_sh
        h_ref[0] = b
        y_ref[0] = (b * _gelu(ug_ref[0])[0]).astype(BF16)

    return pl.pallas_call(
        body, grid=(C // CT, Bl),
        in_specs=[_cspec(T), _cspec(T), _cspec(T, C // CT), _cspec(T), _pspec(1), _pspec(1), _pspec(1)],
        out_specs=(_cspec(T), _cspec(T)), out_shape=(S((Bl, T, C), BF16), S((Bl, T, C), F32)),
        compiler_params=_cparams("parallel", "arbitrary"), name=name)(xc3, gates3, gates3, proj3, br, bi, lam)


def _rglru_bwd(dy3, xc3, gates3, proj3, h3, br, bi, lam, name):
    Bl, T, C = xc3.shape
    nsteps = int(math.log2(T))

    def body(dy_ref, xc_ref, rp_ref, ip_ref, ug_ref, h_ref, br_ref, bi_ref, lam_ref,
             dxc_ref, drp_ref, dip_ref, dug_ref, dbr_ref, dbi_ref, dlam_ref):
        row = lax.broadcasted_iota(jnp.int32, (T, CT), 0)
        xc = xc_ref[0]
        r, i, sp, a, mult = _lru_gates(xc, rp_ref[0], ip_ref[0], br_ref, bi_ref, lam_ref)
        h = h_ref[0]
        dy = dy_ref[0]
        ug = ug_ref[0]
        gel, t = _gelu(ug)
        dug_ref[0] = (dy * h * _dgelu(ug, t)).astype(BF16)
        gacc = dy * gel
        an = _shift_up(a, 1, row)
        for st in range(nsteps):
            s = 1 << st
            an_sh = jnp.where(row < T - s, pltpu.roll(an, T - s, axis=0), 1.0)
            gacc = an * _shift_up(gacc, s, row) + gacc
            an = an * an_sh
        da = gacc * _shift_down(h, 1, row)
        ix = i * xc
        d_mult = gacc * ix
        d_i = gacc * mult * xc
        dxc_ref[0] = gacc * mult * i
        d_log_a = da * a - d_mult * (a * a) / mult
        d_r = d_log_a * ((-LRU_C) * sp)
        d_sp = jnp.sum(d_log_a * ((-LRU_C) * r), axis=0, keepdims=True)
        drp = d_r * r * (1.0 - r)
        dip = d_i * i * (1.0 - i)
        drp_ref[0] = drp.astype(BF16)
        dip_ref[0] = dip.astype(BF16)

        @pl.when(pl.program_id(1) == 0)
        def _():
            dbr_ref[...] = jnp.zeros_like(dbr_ref)
            dbi_ref[...] = jnp.zeros_like(dbi_ref)
            dlam_ref[...] = jnp.zeros_like(dlam_ref)

        dbr_ref[...] += jnp.sum(drp, axis=0, keepdims=True)
        dbi_ref[...] += jnp.sum(dip, axis=0, keepdims=True)
        dlam_ref[...] += d_sp * (-jax.nn.sigmoid(-lam_ref[...]))

    vec = S((1, C), F32)
    act = S((Bl, T, C), BF16)
    return pl.pallas_call(
        body, grid=(C // CT, Bl),
        in_specs=[_cspec(T), _cspec(T), _cspec(T), _cspec(T, C // CT), _cspec(T), _cspec(T)] + [_pspec(1)] * 3,
        out_specs=(_cspec(T), _cspec(T), _cspec(T), _cspec(T), _pspec(1), _pspec(1), _pspec(1)),
        out_shape=(S((Bl, T, C), F32), act, act, act, vec, vec, vec),
        compiler_params=_cparams("parallel", "arbitrary"), name=name)(dy3, xc3, gates3, gates3, proj3, h3, br, bi, lam)


NT = (((1,), (1,)), ((), ()))
TN = (((0,), (0,)), ((), ()))


def _hs(h):
    return slice(h * HEAD, (h + 1) * HEAD)


def _mem_softmax(qb, kb):
    s = lax.dot_general(qb, kb, NT, preferred_element_type=F32) * (HEAD ** -0.5)
    e = jnp.exp(s - jnp.max(s, axis=-1, keepdims=True))
    return e / jnp.sum(e, axis=-1, keepdims=True)


def _mem_attn_fwd(proj3, q_off, mkv3, name):
    Bl, T, _ = proj3.shape
    M = mkv3.shape[1]
    tq = _tile(T, (512, 256, 128))

    def body(q_ref, k_ref, v_ref, o_ref):
        q = q_ref[0].astype(BF16)
        k = k_ref[0].astype(BF16)
        v = v_ref[0].astype(BF16)
        outs = []
        for h in range(MEM_HEADS):
            p = _mem_softmax(q[:, _hs(h)], k[:, _hs(h)])
            outs.append(jnp.dot(p.astype(BF16), v[:, _hs(h)], preferred_element_type=F32))
        o_ref[0] = jnp.concatenate(outs, axis=-1).astype(BF16)

    return pl.pallas_call(
        body, grid=(Bl, T // tq),
        in_specs=[pl.BlockSpec((1, tq, MEM_W), lambda b, t: (b, t, q_off // MEM_W)),
                  pl.BlockSpec((1, M, MEM_W), lambda b, t: (b, 0, 0)),
                  pl.BlockSpec((1, M, MEM_W), lambda b, t: (b, 0, 1))],
        out_specs=pl.BlockSpec((1, tq, MEM_W), lambda b, t: (b, t, 0)),
        out_shape=S((Bl, T, MEM_W), BF16), compiler_params=_cparams("parallel", "parallel"), name=name)(
            proj3, mkv3, mkv3)


def _mem_attn_bwd(proj3, q_off, mkv3, do3, name):
    Bl, T, _ = proj3.shape
    M = mkv3.shape[1]
    tq = _tile(T, (512, 256, 128))
    scale = HEAD ** -0.5

    def body(q_ref, k_ref, v_ref, do_ref, dq_ref, dkv_ref):
        q = q_ref[0].astype(BF16)
        k = k_ref[0].astype(BF16)
        v = v_ref[0].astype(BF16)
        do = do_ref[0].astype(BF16)
        dqs, dks, dvs = [], [], []
        for h in range(MEM_HEADS):
            qh, kh, vh, doh = q[:, _hs(h)], k[:, _hs(h)], v[:, _hs(h)], do[:, _hs(h)]
            p = _mem_softmax(qh, kh)
            dvs.append(lax.dot_general(p.astype(BF16), doh, TN, preferred_element_type=F32))
            dp = lax.dot_general(doh, vh, NT, preferred_element_type=F32)
            ds = (p * (dp - jnp.sum(p * dp, axis=-1, keepdims=True)) * scale).astype(BF16)
            dqs.append(jnp.dot(ds, kh, preferred_element_type=F32))
            dks.append(lax.dot_general(ds, qh, TN, preferred_element_type=F32))
        dq_ref[0] = jnp.concatenate(dqs, axis=-1).astype(BF16)

        @pl.when(pl.program_id(1) == 0)
        def _():
            dkv_ref[...] = jnp.zeros_like(dkv_ref)

        dkv_ref[0] += jnp.concatenate(dks + dvs, axis=-1)

    return pl.pallas_call(
        body, grid=(Bl, T // tq),
        in_specs=[pl.BlockSpec((1, tq, MEM_W), lambda b, t: (b, t, q_off // MEM_W)),
                  pl.BlockSpec((1, M, MEM_W), lambda b, t: (b, 0, 0)),
                  pl.BlockSpec((1, M, MEM_W), lambda b, t: (b, 0, 1)),
                  pl.BlockSpec((1, tq, MEM_W), lambda b, t: (b, t, 0))],
        out_specs=(pl.BlockSpec((1, tq, MEM_W), lambda b, t: (b, t, 0)),
                   pl.BlockSpec((1, M, 2 * MEM_W), lambda b, t: (b, 0, 0))),
        out_shape=(S((Bl, T, MEM_W), BF16), S((Bl, M, 2 * MEM_W), F32)),
        compiler_params=_cparams("parallel", "arbitrary"), name=name)(proj3, mkv3, mkv3, do3)


GROUP_ROWS = SWA_GROUP * WIN


def _group_rows(x, kvh):
    return jnp.concatenate([x[:, _hs(SWA_GROUP * kvh + g)] for g in range(SWA_GROUP)], axis=0)


def _group_col(vals):
    grp = lax.shift_right_logical(lax.broadcasted_iota(jnp.int32, (GROUP_ROWS, 1), 0), WIN.bit_length() - 1)
    col = jnp.full((GROUP_ROWS, 1), vals[-1], F32)
    for g in range(SWA_GROUP - 2, -1, -1):
        col = jnp.where(grp == g, vals[g], col)
    return col


def _swa_probs(qh, kph, kch, sink, slope, has_prev):
    qi = jnp.bitwise_and(lax.broadcasted_iota(jnp.int32, (GROUP_ROWS, WIN), 0), WIN - 1)
    kj = lax.broadcasted_iota(jnp.int32, (GROUP_ROWS, WIN), 1)
    scale = HEAD ** -0.5
    sp = lax.dot_general(qh, kph, NT, preferred_element_type=F32) * scale
    sc = lax.dot_general(qh, kch, NT, preferred_element_type=F32) * scale
    dist_p = (qi + WIN - kj).astype(F32)
    dist_c = (qi - kj).astype(F32)
    neg = -jnp.inf
    sp = jnp.where(kj > qi + jnp.where(has_prev, 0, WIN), sp - slope * dist_p, neg)
    sc = jnp.where(kj <= qi, sc - slope * dist_c, neg)
    m = jnp.maximum(jnp.maximum(jnp.max(sp, axis=-1, keepdims=True), jnp.max(sc, axis=-1, keepdims=True)), sink)
    ep = jnp.exp(sp - m)
    ec = jnp.exp(sc - m)
    es = jnp.exp(sink - m)
    inv = 1.0 / (jnp.sum(ep, axis=-1, keepdims=True) + jnp.sum(ec, axis=-1, keepdims=True) + es)
    return ep * inv, ec * inv, es * inv


def _swa_specs(nb):
    prev = lambda n: jnp.maximum(n - 1, 0)
    q = pl.BlockSpec((1, WIN, MIX_W), lambda b, n: (b, n, 0))
    kp = pl.BlockSpec((1, WIN, MEM_W), lambda b, n: (b, prev(n), 0))
    kc = pl.BlockSpec((1, WIN, MEM_W), lambda b, n: (b, n, 0))
    vp = pl.BlockSpec((1, WIN, MEM_W), lambda b, n: (b, prev(n), 1))
    vc = pl.BlockSpec((1, WIN, MEM_W), lambda b, n: (b, n, 1))
    sm = pl.BlockSpec(memory_space=pltpu.SMEM)
    return q, kp, kc, vp, vc, sm


def _swa_fwd(proj3, kv3, sinks, name):
    Bl, T, _ = proj3.shape
    nb = T // WIN
    q_s, kp_s, kc_s, vp_s, vc_s, sm = _swa_specs(nb)

    def body(q_ref, kp_ref, kc_ref, vp_ref, vc_ref, sink_ref, o_ref):
        has_prev = pl.program_id(1) > 0
        q = q_ref[0].astype(BF16)
        kp, kc = kp_ref[0].astype(BF16), kc_ref[0].astype(BF16)
        vp, vc = vp_ref[0].astype(BF16), vc_ref[0].astype(BF16)
        outs = []
        for kvh in range(SWA_HEADS // SWA_GROUP):
            kvs = _hs(kvh)
            heads = range(SWA_GROUP * kvh, SWA_GROUP * (kvh + 1))
            pp, pc, _ = _swa_probs(_group_rows(q, kvh), kp[:, kvs], kc[:, kvs], _group_col([sink_ref[h] for h in heads]),
                                   _group_col([SLOPES[h] for h in heads]), has_prev)
            og = (jnp.dot(pp.astype(BF16), vp[:, kvs], preferred_element_type=F32)
                  + jnp.dot(pc.astype(BF16), vc[:, kvs], preferred_element_type=F32))
            outs += [og[g * WIN:(g + 1) * WIN] for g in range(SWA_GROUP)]
        o_ref[0] = jnp.concatenate(outs, axis=-1).astype(BF16)

    return pl.pallas_call(
        body, grid=(Bl, nb), in_specs=[q_s, kp_s, kc_s, vp_s, vc_s, sm], out_specs=q_s,
        out_shape=S((Bl, T, MIX_W), BF16), compiler_params=_cparams("parallel", "parallel"), name=name)(
            proj3, kv3, kv3, kv3, kv3, sinks)


def _swa_bwd(proj3, kv3, sinks, do3, name):
    Bl, T, _ = proj3.shape
    nb = T // WIN
    q_s, kp_s, kc_s, vp_s, vc_s, sm = _swa_specs(nb)
    kv_s = pl.BlockSpec((1, WIN, 2 * MEM_W), lambda b, n: (b, n, 0))
    sk_s = pl.BlockSpec((8, LANES), lambda b, n: (0, 0))
    scale = HEAD ** -0.5

    def body(q_ref, kp_ref, kc_ref, vp_ref, vc_ref, sink_ref, do_ref, dq_ref, dkc_ref, dkp_ref, dsk_ref):
        has_prev = pl.program_id(1) > 0
        q = q_ref[0].astype(BF16)
        kp, kc = kp_ref[0].astype(BF16), kc_ref[0].astype(BF16)
        vp, vc = vp_ref[0].astype(BF16), vc_ref[0].astype(BF16)
        do = do_ref[0].astype(BF16)
        lane = lax.broadcasted_iota(jnp.int32, (8, LANES), 1)
        srow = lax.broadcasted_iota(jnp.int32, (8, LANES), 0)
        dsk = jnp.zeros((8, LANES), F32)
        dqs = []
        dkc, dkp, dvc, dvp = [], [], [], []
        grp = lax.shift_right_logical(lax.broadcasted_iota(jnp.int32, (GROUP_ROWS, 1), 0), WIN.bit_length() - 1)
        for kvh in range(SWA_HEADS // SWA_GROUP):
            kvs = _hs(kvh)
            heads = range(SWA_GROUP * kvh, SWA_GROUP * (kvh + 1))
            qg, dog = _group_rows(q, kvh), _group_rows(do, kvh)
            pp, pc, ps = _swa_probs(qg, kp[:, kvs], kc[:, kvs], _group_col([sink_ref[h] for h in heads]),
                                    _group_col([SLOPES[h] for h in heads]), has_prev)
            dpp = lax.dot_general(dog, vp[:, kvs], NT, preferred_element_type=F32)
            dpc = lax.dot_general(dog, vc[:, kvs], NT, preferred_element_type=F32)
            delta = jnp.sum(pp * dpp, axis=-1, keepdims=True) + jnp.sum(pc * dpc, axis=-1, keepdims=True)
            dsp = (pp * (dpp - delta) * scale).astype(BF16)
            dsc = (pc * (dpc - delta) * scale).astype(BF16)
            dqg = (jnp.dot(dsp, kp[:, kvs], preferred_element_type=F32)
                   + jnp.dot(dsc, kc[:, kvs], preferred_element_type=F32))
            dqs += [dqg[g * WIN:(g + 1) * WIN] for g in range(SWA_GROUP)]
            dkc.append(lax.dot_general(dsc, qg, TN, preferred_element_type=F32))
            dkp.append(lax.dot_general(dsp, qg, TN, preferred_element_type=F32))
            dvc.append(lax.dot_general(pc.astype(BF16), dog, TN, preferred_element_type=F32))
            dvp.append(lax.dot_general(pp.astype(BF16), dog, TN, preferred_element_type=F32))
            dsink = ps * delta
            for g, h in enumerate(heads):
                dsk = dsk + jnp.where((lane == h) & (srow == 0), -jnp.sum(jnp.where(grp == g, dsink, 0.0)), 0.0)
        dq_ref[0] = jnp.concatenate(dqs, axis=-1).astype(BF16)
        dkc_ref[0] = jnp.concatenate(dkc + dvc, axis=-1)
        dkp_ref[0] = jnp.concatenate(dkp + dvp, axis=-1)

        @pl.when((pl.program_id(0) == 0) & (pl.program_id(1) == 0))
        def _():
            dsk_ref[...] = jnp.zeros_like(dsk_ref)

        dsk_ref[...] += dsk

    return pl.pallas_call(
        body, grid=(Bl, nb), in_specs=[q_s, kp_s, kc_s, vp_s, vc_s, sm, q_s], out_specs=(q_s, kv_s, kv_s, sk_s),
        out_shape=(S((Bl, T, MIX_W), BF16), S((Bl, T, 2 * MEM_W), F32), S((Bl, T, 2 * MEM_W), F32), S((8, LANES), F32)),
        compiler_params=_cparams("arbitrary", "arbitrary"), name=name)(proj3, kv3, kv3, kv3, kv3, sinks, do3)


def _kv_grad_combine(parts, name):
    Bl, T, W = parts[0][0].shape
    nb = T // WIN
    nl = len(parts)

    def body(*refs):
        o_ref = refs[-1]
        has_next = jnp.where(pl.program_id(1) == nb - 1, 0.0, 1.0)
        acc = None
        for l in range(nl):
            c = refs[2 * l][0] + has_next * refs[2 * l + 1][0]
            acc = c if acc is None else acc + c
        o_ref[0] = acc.astype(BF16)

    cur = pl.BlockSpec((1, WIN, W), lambda b, n: (b, n, 0))
    nxt = pl.BlockSpec((1, WIN, W), lambda b, n: (b, jnp.minimum(n + 1, nb - 1), 0))
    return pl.pallas_call(
        body, grid=(Bl, nb), in_specs=[cur, nxt] * nl, out_specs=cur, out_shape=S((Bl, T, W), BF16),
        compiler_params=_cparams("parallel", "parallel"), name=name)(*[a for pr in parts for a in pr])


def _loss_bwd(y, target, name="loss"):
    N, D = y.shape
    tm = _tile(N, (512, 256, 128))

    def body(y_ref, t_ref, dy_ref, l_ref):
        e = y_ref[...] - t_ref[...]
        dy_ref[...] = e * (1.0 / D)

        @pl.when(pl.program_id(0) == 0)
        def _():
            l_ref[...] = jnp.zeros_like(l_ref)

        l_ref[...] += jnp.sum(e * e, axis=0, keepdims=True) * (0.5 / D)

    row = pl.BlockSpec((tm, D), lambda i: (i, 0))
    vec = pl.BlockSpec((1, D), lambda i: (0, 0))
    return pl.pallas_call(
        body, grid=(N // tm,), in_specs=[row, row], out_specs=(row, vec), out_shape=(S((N, D), F32), S((1, D), F32)),
        compiler_params=_cparams("arbitrary"), name=name)(y, target)


def _all_gather(x, name):
    R, C = x.shape

    def body(x_ref, out_ref, send_sems, recv_sems, local_sem):
        mx, my, mc = lax.axis_index("x"), lax.axis_index("y"), lax.axis_index("c")
        me, sibling = (mx, my, mc), (mx, my, 1 - mc)
        chips = [(1 - mx, my), (mx, 1 - my), (1 - mx, 1 - my)]

        def rows(px, py, pc):
            return out_ref.at[4 * px + 2 * py + pc]

        def copy(kk, block, to, src=None):
            return pltpu.make_async_remote_copy(
                src_ref=rows(*block) if src is None else src, dst_ref=rows(*block), send_sem=send_sems.at[kk],
                recv_sem=recv_sems.at[kk], device_id=to, device_id_type=MESH)

        mine = pltpu.make_async_copy(x_ref, rows(*me), local_sem)
        mine.start()
        first = [copy(0, me, sibling, src=x_ref)]
        first += [copy(1 + j, me, (*chip, mc), src=x_ref) for j, chip in enumerate(chips)]
        for cp in first:
            cp.start()
        passed = [copy(4 + j, (*chip, mc), sibling) for j, chip in enumerate(chips)]
        for j, chip in enumerate(chips):
            copy(1 + j, (*chip, mc), me).wait_recv()
            passed[j].start()
        copy(0, sibling, me).wait_recv()
        for j, chip in enumerate(chips):
            copy(4 + j, (*chip, 1 - mc), me).wait_recv()
        for cp in first + passed:
            cp.wait_send()
        mine.wait()

    return pl.pallas_call(
        body, out_shape=S((N_DEV, R, C), x.dtype), in_specs=[ANY], out_specs=ANY,
        scratch_shapes=[pltpu.SemaphoreType.DMA((7,)), pltpu.SemaphoreType.DMA((7,)), pltpu.SemaphoreType.DMA(())],
        name=name)(x)


def _ag_weights(shards, row_sharded, name):
    n = len(shards)

    def full_shape(a, rows):
        if rows:
            return a.shape[:-2] + (N_DEV * a.shape[-2],) + a.shape[-1:]
        return (N_DEV,) + a.shape

    def body(*refs):
        x_refs, o_refs = refs[:n], refs[n:2 * n]
        send_sems, recv_sems, local_sems = refs[2 * n:]
        mx, my, mc = lax.axis_index("x"), lax.axis_index("y"), lax.axis_index("c")
        me, sibling = (mx, my, mc), (mx, my, 1 - mc)
        chips = [(1 - mx, my), (mx, 1 - my), (1 - mx, 1 - my)]

        def dst(t, px, py, pc):
            d = 4 * px + 2 * py + pc
            if not row_sharded[t]:
                return o_refs[t].at[d]
            r = shards[t].shape[-2]
            idx = (slice(None),) * (shards[t].ndim - 2) + (pl.ds(pl.multiple_of(d * r, 16), r), slice(None))
            return o_refs[t].at[idx]

        def copy(kk, t, block, to, src=None):
            return pltpu.make_async_remote_copy(
                src_ref=dst(t, *block) if src is None else src, dst_ref=dst(t, *block),
                send_sem=send_sems.at[kk * n + t], recv_sem=recv_sems.at[kk * n + t], device_id=to,
                device_id_type=MESH)

        mine = [pltpu.make_async_copy(x_refs[t], dst(t, *me), local_sems.at[t]) for t in range(n)]
        for cp in mine:
            cp.start()
        first = []
        for t in range(n):
            first.append(copy(0, t, me, sibling, src=x_refs[t]))
            first += [copy(1 + j, t, me, (*chip, mc), src=x_refs[t]) for j, chip in enumerate(chips)]
        for cp in first:
            cp.start()
        passed = []
        for j, chip in enumerate(chips):
            for t in range(n):
                copy(1 + j, t, (*chip, mc), me).wait_recv()
                cp = copy(4 + j, t, (*chip, mc), sibling)
                cp.start()
                passed.append(cp)
        for t in range(n):
            copy(0, t, sibling, me).wait_recv()
            for j, chip in enumerate(chips):
                copy(4 + j, t, (*chip, 1 - mc), me).wait_recv()
        for cp in first + passed:
            cp.wait_send()
        for cp in mine:
            cp.wait()

    return pl.pallas_call(
        body, out_shape=tuple(S(full_shape(a, r), a.dtype) for a, r in zip(shards, row_sharded)),
        in_specs=[ANY] * n, out_specs=tuple([ANY] * n),
        scratch_shapes=[pltpu.SemaphoreType.DMA((7 * n,)), pltpu.SemaphoreType.DMA((7 * n,)),
                        pltpu.SemaphoreType.DMA((n,))],
        name=name)(*shards)


def _rs_sibling(gs, name):
    n = len(gs)

    def body(*refs):
        g_refs, o_refs = refs[:n], refs[n:2 * n]
        send_sems, recv_sems = refs[2 * n:]
        mx, my, mc = lax.axis_index("x"), lax.axis_index("y"), lax.axis_index("c")
        copies = [pltpu.make_async_remote_copy(
            src_ref=g_refs[t].at[:, 2 * j + (1 - mc)], dst_ref=o_refs[t].at[j], send_sem=send_sems.at[j * n + t],
            recv_sem=recv_sems.at[j * n + t], device_id=(mx, my, 1 - mc), device_id_type=MESH)
            for t in range(n) for j in range(4)]
        for cp in copies:
            cp.start()
        for cp in copies:
            cp.wait_recv()
        for cp in copies:
            cp.wait_send()

    return pl.pallas_call(
        body, out_shape=tuple(S((4, g.shape[0]) + g.shape[2:], g.dtype) for g in gs), in_specs=[ANY] * n,
        out_specs=tuple([ANY] * n),
        scratch_shapes=[pltpu.SemaphoreType.DMA((4 * n,)), pltpu.SemaphoreType.DMA((4 * n,))], name=name)(*gs)


def _rs_chips(ps, name):
    n = len(ps)

    def body(*refs):
        p_refs, o_refs = refs[:n], refs[n:2 * n]
        send_sems, recv_sems = refs[2 * n:]
        mx, my, mc = lax.axis_index("x"), lax.axis_index("y"), lax.axis_index("c")
        chips = [(1 - mx, my), (mx, 1 - my), (1 - mx, 1 - my)]
        copies = [pltpu.make_async_remote_copy(
            src_ref=p_refs[t].at[2 * cx + cy], dst_ref=o_refs[t].at[j], send_sem=send_sems.at[j * n + t],
            recv_sem=recv_sems.at[j * n + t], device_id=(cx, cy, mc), device_id_type=MESH)
            for t in range(n) for j, (cx, cy) in enumerate(chips)]
        for cp in copies:
            cp.start()
        for cp in copies:
            cp.wait_recv()
        for cp in copies:
            cp.wait_send()

    return pl.pallas_call(
        body, out_shape=tuple(S((3,) + p.shape[1:], p.dtype) for p in ps), in_specs=[ANY] * n,
        out_specs=tuple([ANY] * n),
        scratch_shapes=[pltpu.SemaphoreType.DMA((3 * n,)), pltpu.SemaphoreType.DMA((3 * n,))], name=name)(*ps)


FLIPS = [(fx, fy, fc) for fx in (0, 1) for fy in (0, 1) for fc in (0, 1)][1:]
HBM = pl.BlockSpec(memory_space=pltpu.HBM)
SEM = pl.BlockSpec(memory_space=pltpu.SEMAPHORE)
EFFECT = pltpu.SideEffectType.DATAFLOW_SIDE_EFFECTING


def _hbm(a):
    return pltpu.with_memory_space_constraint(a, pltpu.HBM)


def _flips(gather):
    return [(0, 0, 0)] + FLIPS if gather else FLIPS


def _split_copies(gather, s_refs, l_refs, send_sems, recv_sems):
    n = len(s_refs)
    mx, my, mc = lax.axis_index("x"), lax.axis_index("y"), lax.axis_index("c")
    me = 4 * mx + 2 * my + mc
    copies = []
    for k, (fx, fy, fc) in enumerate(_flips(gather)):
        px, py, pc = (1 - mx if fx else mx), (1 - my if fy else my), (1 - mc if fc else mc)
        for t in range(n):
            if gather:
                src = s_refs[t]
                r = src.shape[0]
                dst = l_refs[t].at[pl.ds(pl.multiple_of(me * r, 16), r), :]
            else:
                src = s_refs[t].at[:, 4 * px + 2 * py + pc]
                dst = l_refs[t].at[k]
            copies.append(pltpu.make_async_remote_copy(
                src_ref=src, dst_ref=dst, send_sem=send_sems.at[k * n + t], recv_sem=recv_sems.at[k * n + t],
                device_id=(px, py, pc), device_id_type=MESH))
    return copies


def _split_start(gather, srcs, lands, after, name):
    n = len(srcs)
    n_sem = len(_flips(gather)) * n

    def body(*refs):
        s_refs, l_refs = refs[:n], refs[n:2 * n]
        send_sems, recv_sems = refs[2 * n + 1], refs[2 * n + 2]
        token = refs[-1]
        for cp in _split_copies(gather, s_refs, l_refs, send_sems, recv_sems):
            cp.start()
        token[...] = jnp.zeros_like(token)

    outs = pl.pallas_call(
        body, name=name,
        out_shape=(pltpu.SemaphoreType.DMA((n_sem,)), pltpu.SemaphoreType.DMA((n_sem,)))
        + tuple(pltpu.HBM(a.shape, a.dtype) for a in srcs) + tuple(pltpu.HBM(a.shape, a.dtype) for a in lands)
        + (S((8, LANES), F32),),
        in_specs=[HBM] * (2 * n) + [ANY],
        out_specs=(SEM, SEM) + (HBM,) * (2 * n) + (pl.BlockSpec(memory_space=pltpu.VMEM),),
        input_output_aliases={i: 2 + i for i in range(2 * n)},
        compiler_params=pltpu.CompilerParams(has_side_effects=EFFECT),
    )(*[_hbm(a) for a in srcs], *[_hbm(a) for a in lands], after)
    return outs[0], outs[1], list(outs[2:2 + n]), list(outs[2 + n:2 + 2 * n]), outs[-1]


def _split_wait(gather, send_sems, recv_sems, srcs, lands, after, name):
    n = len(srcs)

    def body(*refs):
        s_refs, l_refs = refs[:n], refs[n:2 * n]
        ssem, rsem = refs[2 * n], refs[2 * n + 1]
        copies = _split_copies(gather, s_refs, l_refs, ssem, rsem)
        for cp in copies:
            cp.wait_send()
        for cp in copies:
            cp.wait_recv()

    outs = pl.pallas_call(
        body, name=name,
        out_shape=tuple(pltpu.HBM(a.shape, a.dtype) for a in srcs) + tuple(pltpu.HBM(a.shape, a.dtype) for a in lands),
        in_specs=[HBM] * (2 * n) + [SEM, SEM, ANY], out_specs=(HBM,) * (2 * n),
        input_output_aliases={i: i for i in range(2 * n)},
        compiler_params=pltpu.CompilerParams(has_side_effects=EFFECT),
    )(*srcs, *lands, send_sems, recv_sems, after)
    return list(outs[n:])


def _rows_tile(b):
    return _tile(b, (512, 256, 128)) if b > 512 else b


def _pair_sum(g, got, name):
    A, _, B, C = g.shape
    tb = _rows_tile(B)
    core = lax.axis_index("c").astype(jnp.int32).reshape(1)

    def body(c_ref, g_ref, r_ref, o_ref):
        o_ref[...] = (g_ref[...].astype(F32) + r_ref[...].astype(F32)).astype(o_ref.dtype)

    return pl.pallas_call(
        body,
        grid_spec=pltpu.PrefetchScalarGridSpec(
            num_scalar_prefetch=1, grid=(4, A, B // tb),
            in_specs=[pl.BlockSpec((1, 1, tb, C), lambda j, a, i, c_ref: (a, 2 * j + c_ref[0], i, 0)),
                      pl.BlockSpec((1, 1, tb, C), lambda j, a, i, c_ref: (j, a, i, 0))],
            out_specs=pl.BlockSpec((1, 1, tb, C), lambda j, a, i, c_ref: (j, a, i, 0))),
        out_shape=S((4, A, B, C), g.dtype), compiler_params=_cparams("parallel", "parallel", "parallel"),
        name=name)(core, g, got)


def _adamw_math(w, g, m, v):
    m = ADAM_B1 * m + (1.0 - ADAM_B1) * g
    v = ADAM_B2 * v + (1.0 - ADAM_B2) * (g * g)
    m_hat = m / (1.0 - ADAM_B1 ** ADAM_STEP)
    v_hat = v / (1.0 - ADAM_B2 ** ADAM_STEP)
    delta = -ADAM_LR * (m_hat / (jnp.sqrt(v_hat) + ADAM_EPS) + ADAM_WD * w)
    return delta, m, v


def _adamw_sharded(p, got, w, m, v, name):
    A, B, C = w.shape
    tb = _rows_tile(B)
    chip = (2 * lax.axis_index("x") + lax.axis_index("y")).astype(jnp.int32).reshape(1)

    def body(c_ref, p_ref, got_ref, w_ref, m_ref, v_ref, g_out, d_out, m_out, v_out):
        g = p_ref[0].astype(F32)
        for j in range(3):
            g = g + got_ref[j].astype(F32)
        d, mn, vn = _adamw_math(w_ref[...], g, m_ref[...], v_ref[...])
        g_out[...] = g
        d_out[...] = d
        m_out[...] = mn
        v_out[...] = vn

    blk = pl.BlockSpec((1, tb, C), lambda a, i, c_ref: (a, i, 0))
    return pl.pallas_call(
        body,
        grid_spec=pltpu.PrefetchScalarGridSpec(
            num_scalar_prefetch=1, grid=(A, B // tb),
            in_specs=[pl.BlockSpec((1, 1, tb, C), lambda a, i, c_ref: (c_ref[0], a, i, 0)),
                      pl.BlockSpec((3, 1, tb, C), lambda a, i, c_ref: (0, a, i, 0)), blk, blk, blk],
            out_specs=(blk, blk, blk, blk)),
        out_shape=(S((A, B, C), F32),) * 4, compiler_params=_cparams("parallel", "parallel"), name=name)(
            chip, p, got, w, m, v)


def _adamw_layers(owns, gots, w, m, v, name):
    L, B, C = w.shape
    per_row = 2 * L * len(FLIPS) * C * owns[0].dtype.itemsize
    tb = max(t for t in range(16, B + 1, 16) if B % t == 0 and (t * per_row <= 16 * 1024 * 1024 or t == 16))
    me = (4 * lax.axis_index("x") + 2 * lax.axis_index("y") + lax.axis_index("c")).astype(jnp.int32).reshape(1)

    def body(me_ref, *refs):
        own_refs, got_refs = refs[:L], refs[L:2 * L]
        w_ref, m_ref, v_ref = refs[2 * L:2 * L + 3]
        g_out, d_out, m_out, v_out = refs[2 * L + 3:]
        layer = pl.program_id(0)
        for kk in range(L):
            @pl.when(layer == kk)
            def _():
                g = own_refs[kk][0].astype(F32)
                for s in range(len(FLIPS)):
                    g = g + got_refs[kk][s].astype(F32)
                d, mn, vn = _adamw_math(w_ref[...], g, m_ref[...], v_ref[...])
                g_out[...] = g
                d_out[...] = d
                m_out[...] = mn
                v_out[...] = vn

    def row(kk, layer, i):
        return jnp.where(layer == kk, i, 0)

    blk = pl.BlockSpec((1, tb, C), lambda layer, i, me_ref: (layer, i, 0))
    own_specs = [pl.BlockSpec((1, 1, tb, C), lambda layer, i, me_ref, kk=kk: (0, me_ref[0], row(kk, layer, i), 0))
                 for kk in range(L)]
    got_specs = [pl.BlockSpec((len(FLIPS), 1, tb, C), lambda layer, i, me_ref, kk=kk: (0, 0, row(kk, layer, i), 0))
                 for kk in range(L)]
    return pl.pallas_call(
        body,
        grid_spec=pltpu.PrefetchScalarGridSpec(
            num_scalar_prefetch=1, grid=(L, B // tb), in_specs=own_specs + got_specs + [blk, blk, blk],
            out_specs=(blk, blk, blk, blk)),
        out_shape=(S((L, B, C), F32),) * 4, compiler_params=_cparams("arbitrary", "arbitrary"), name=name)(
            me, *owns, *gots, w, m, v)


def _adamw_replicated(parts, w, m, v, name):
    R, C = w.shape
    rb = _tile(R, (512, 256, 128, 64, 32, 16, 8))

    def body(p_ref, w_ref, m_ref, v_ref, g_out, d_out, m_out, v_out):
        g = p_ref[0]
        for j in range(1, N_DEV):
            g = g + p_ref[j]
        d, mn, vn = _adamw_math(w_ref[...], g, m_ref[...], v_ref[...])
        g_out[...] = g
        d_out[...] = d
        m_out[...] = mn
        v_out[...] = vn

    blk = pl.BlockSpec((rb, C), lambda i: (i, 0))
    return pl.pallas_call(
        body, grid=(R // rb,), in_specs=[pl.BlockSpec((N_DEV, rb, C), lambda i: (0, i, 0)), blk, blk, blk],
        out_specs=(blk, blk, blk, blk), out_shape=(S((R, C), F32),) * 4, compiler_params=_cparams("parallel"),
        name=name)(parts, w, m, v)


def _pack(arrs, rows_mult, dtype):
    flat = jnp.concatenate([a.reshape(-1).astype(dtype) for a in arrs])
    n = flat.shape[0]
    per = rows_mult * LANES
    tot = -(-n // per) * per
    return jnp.pad(flat, (0, tot - n)).reshape(tot // LANES, LANES)


def _unpack(blob, shapes):
    flat = blob.reshape(-1)
    out, off = [], 0
    for shp in shapes:
        n = int(np.prod(shp))
        out.append(flat[off:off + n].reshape(shp))
        off += n
    return out


def _small_to_natural(g8):
    t = jnp.moveaxis(g8, 0, -2)
    return t.reshape(t.shape[:-2] + (N_DEV * t.shape[-1],))


def _small_to_cols(g):
    t = g.reshape(g.shape[:-1] + (N_DEV, g.shape[-1] // N_DEV))
    return jnp.moveaxis(t, -2, 0)


def _block_diag(w):
    nb, bs, _ = w.shape
    eye = jnp.eye(nb, dtype=w.dtype)
    return (eye[:, None, :, None] * w[:, :, None, :]).reshape(nb * bs, nb * bs)


def _diag_blocks(d, nb, bs):
    d4 = d.reshape(nb, bs, nb, bs)
    return jnp.stack([d4[i, :, i, :] for i in range(nb)])


def kernel(x, mem, g_mix_pre, g_mix_post, g_ffn_pre, g_ffn_post, g_mem, w_mem_kv, w_mix_out, w_ffn_up, w_ffn_conv, b_ffn_conv, w_ffn_down, w_in_a, w_conv_a, b_conv_a, w_rg_r, b_rg_r, w_rg_i, b_rg_i, lru_lambda, w_in_b, sinks_b, g_kv, w_kv, loss_target, m_g_mix_pre, m_g_mix_post, m_g_ffn_pre, m_g_ffn_post, m_g_mem, m_w_mem_kv, m_w_mix_out, m_w_ffn_up, m_w_ffn_conv, m_b_ffn_conv, m_w_ffn_down, m_w_in_a, m_w_conv_a, m_b_conv_a, m_w_rg_r, m_b_rg_r, m_w_rg_i, m_b_rg_i, m_lru_lambda, m_w_in_b, m_sinks_b, m_g_kv, m_w_kv, v_g_mix_pre, v_g_mix_post, v_g_ffn_pre, v_g_ffn_post, v_g_mem, v_w_mem_kv, v_w_mix_out, v_w_ffn_up, v_w_ffn_conv, v_b_ffn_conv, v_w_ffn_down, v_w_in_a, v_w_conv_a, v_b_conv_a, v_w_rg_r, v_b_rg_r, v_w_rg_i, v_b_rg_i, v_lru_lambda, v_w_in_b, v_sinks_b, v_g_kv, v_w_kv):
    w_loc = dict(g_mix_pre=g_mix_pre, g_mix_post=g_mix_post, g_ffn_pre=g_ffn_pre, g_ffn_post=g_ffn_post, g_mem=g_mem,
                 w_mem_kv=w_mem_kv, w_mix_out=w_mix_out, w_ffn_up=w_ffn_up, w_ffn_conv=w_ffn_conv,
                 b_ffn_conv=b_ffn_conv, w_ffn_down=w_ffn_down, w_in_a=w_in_a, w_conv_a=w_conv_a, b_conv_a=b_conv_a,
                 w_rg_r=w_rg_r, b_rg_r=b_rg_r, w_rg_i=w_rg_i, b_rg_i=b_rg_i, lru_lambda=lru_lambda, w_in_b=w_in_b,
                 sinks_b=sinks_b, g_kv=g_kv, w_kv=w_kv)
    m_loc = dict(g_mix_pre=m_g_mix_pre, g_mix_post=m_g_mix_post, g_ffn_pre=m_g_ffn_pre, g_ffn_post=m_g_ffn_post,
                 g_mem=m_g_mem, w_mem_kv=m_w_mem_kv, w_mix_out=m_w_mix_out, w_ffn_up=m_w_ffn_up,
                 w_ffn_conv=m_w_ffn_conv, b_ffn_conv=m_b_ffn_conv, w_ffn_down=m_w_ffn_down, w_in_a=m_w_in_a,
                 w_conv_a=m_w_conv_a, b_conv_a=m_b_conv_a, w_rg_r=m_w_rg_r, b_rg_r=m_b_rg_r, w_rg_i=m_w_rg_i,
                 b_rg_i=m_b_rg_i, lru_lambda=m_lru_lambda, w_in_b=m_w_in_b, sinks_b=m_sinks_b, g_kv=m_g_kv,
                 w_kv=m_w_kv)
    v_loc = dict(g_mix_pre=v_g_mix_pre, g_mix_post=v_g_mix_post, g_ffn_pre=v_g_ffn_pre, g_ffn_post=v_g_ffn_post,
                 g_mem=v_g_mem, w_mem_kv=v_w_mem_kv, w_mix_out=v_w_mix_out, w_ffn_up=v_w_ffn_up,
                 w_ffn_conv=v_w_ffn_conv, b_ffn_conv=v_b_ffn_conv, w_ffn_down=v_w_ffn_down, w_in_a=v_w_in_a,
                 w_conv_a=v_w_conv_a, b_conv_a=v_b_conv_a, w_rg_r=v_w_rg_r, b_rg_r=v_b_rg_r, w_rg_i=v_w_rg_i,
                 b_rg_i=v_b_rg_i, lru_lambda=v_lru_lambda, w_in_b=v_w_in_b, sinks_b=v_sinks_b, g_kv=v_g_kv,
                 w_kv=v_w_kv)

    Bl, T, D = x.shape
    Ml = mem.shape[1]
    N = Bl * T
    depth = g_mix_pre.shape[0]
    n_a = w_in_a.shape[0]
    F = w_ffn_down.shape[1] * N_DEV
    def as_rows(n, a):
        return jnp.swapaxes(a, -1, -2) if n in TRANSPOSED else a

    def layer_keys(l):
        keys = [("w_mem_kv", l), ("w_mix_out", l), ("w_ffn_up", l), ("w_ffn_down", l)]
        keys.append(("w_in_a", l) if l < n_a else ("w_in_b", l - n_a))
        if l == n_a:
            keys.append(("w_kv", None))
        return keys

    def shard_of(key):
        n, i = key
        return as_rows(n, w_loc[n] if i is None else w_loc[n][i]).astype(BF16)

    W = {}
    ffn_names = ("w_ffn_up", "w_ffn_down")
    keys0 = [kk for kk in layer_keys(0) if kk[0] not in ffn_names]
    got0 = _ag_weights([shard_of(kk) for kk in keys0] + [w_loc[n] for n in SMALL_SHARDED],
                       [True] * len(keys0) + [False] * len(SMALL_SHARDED), name="ag_weights_0")
    W.update(zip(keys0, got0))
    for n, a in zip(SMALL_SHARDED, got0[len(keys0):]):
        W[n] = _small_to_natural(a)

    def gather_start(keys, after, tag):
        shards = [shard_of(kk) for kk in keys]
        lands = [lax.empty((N_DEV * s.shape[0],) + s.shape[1:], s.dtype) for s in shards]
        return (keys, tag) + _split_start(True, shards, lands, after, name=f"ag_start_{tag}")

    def gather_wait(pending, after):
        keys, tag, ssem, rsem, srcs, lands, _ = pending
        W.update(zip(keys, _split_wait(True, ssem, rsem, srcs, lands, after, name=f"ag_wait_{tag}")))

    pending_ffn0 = gather_start([kk for kk in layer_keys(0) if kk[0] in ffn_names], got0[0], "ffn_0")

    nblk, bsz = w_rg_r.shape[1], w_rg_r.shape[2]
    wbd = [jnp.concatenate([_block_diag(w_rg_r[j]), _block_diag(w_rg_i[j])], axis=1).astype(BF16) for j in range(n_a)]

    def vec(a):
        return a.reshape(1, -1)

    x2 = x.reshape(N, D)
    mem2 = mem.reshape(Bl * Ml, D)
    saved = []
    kvn = kv3 = x_kv = None
    xs = x2
    for l in range(depth):
        sv = {"x0": xs}
        g_pre = vec(g_mix_pre[l])
        if l + 1 < depth:
            pending = gather_start(layer_keys(l + 1), pending_ffn0[-1] if l == 0 else W[("w_mem_kv", l)], l + 1)
            g_pre = g_pre + pending[-1][0, 0]
        h1 = _rms_fwd(xs, g_pre, BF16, name=f"rms_mixpre_{l}")
        memn = _rms_fwd(mem2, vec(g_mem[l]), BF16, name=f"rms_mem_{l}")
        mkv3 = _mm(memn, W[("w_mem_kv", l)], name=f"mm_memkv_{l}").reshape(Bl, Ml, 2 * MEM_W)
        if l < n_a:
            j = l
            proj = _mm(h1, W[("w_in_a", j)], tb=True, name=f"mm_in_{l}")
            proj3 = proj.reshape(Bl, T, -1)
            xc3 = _conv_fwd_call(proj3, MIX_W, MIX_W, W["w_conv_a"][j], vec(W["b_conv_a"][j]), name=f"conv_a_{l}")
            gates3 = _mm(xc3.reshape(N, MIX_W), wbd[j], name=f"mm_gates_{l}").reshape(Bl, T, 2 * MIX_W)
            y_main3, hs3 = _rglru_fwd(xc3, gates3, proj3, vec(b_rg_r[j]), vec(b_rg_i[j]), vec(W["lru_lambda"][j]),
                                      name=f"rglru_fwd_{l}")
            q_off = 2 * MIX_W
            sv.update(xc3=xc3, gates3=gates3, hs3=hs3)
        else:
            j = l - n_a
            if l == n_a:
                x_kv = xs
                kvn = _rms_fwd(xs, vec(g_kv), BF16, name="rms_kv")
                kv3 = _mm(kvn, W[("w_kv", None)], name="mm_kv").reshape(Bl, T, 2 * MEM_W)
            proj = _mm(h1, W[("w_in_b", j)], name=f"mm_in_{l}")
            proj3 = proj.reshape(Bl, T, -1)
            y_main3 = _swa_fwd(proj3, kv3, sinks_b[j], name=f"swa_fwd_{l}")
            q_off = MIX_W
        y_mem3 = _mem_attn_fwd(proj3, q_off, mkv3, name=f"memattn_fwd_{l}")
        y_main = y_main3.reshape(N, MIX_W)
        y_mem = y_mem3.reshape(N, MEM_W)
        y = _mm(y_main, W[("w_mix_out", l)], n=D, k=MIX_W, name=f"mm_mixout_main_{l}")
        y = _mm(y_mem, W[("w_mix_out", l)], n=D, k=MEM_W, b_off=(MIX_W, 0), add=y, name=f"mm_mixout_mem_{l}")
        x1 = _rms_fwd(y, vec(g_mix_post[l]), F32, res=xs, name=f"rms_mixpost_{l}")
        h2 = _rms_fwd(x1, vec(g_ffn_pre[l]), BF16, name=f"rms_ffnpre_{l}")
        if l == 0:
            gather_wait(pending_ffn0, h2)
        ug = _mm(h2, W[("w_ffn_up", l)], tb=True, n=F, name=f"mm_up_g_{l}")
        uv = _mm(h2, W[("w_ffn_up", l)], tb=True, n=F, b_off=(F, 0), name=f"mm_up_v_{l}")
        ug3, uv3 = ug.reshape(Bl, T, F), uv.reshape(Bl, T, F)
        act3 = _ffn_mid_fwd(ug3, uv3, W["w_ffn_conv"][l], vec(b_ffn_conv[l]), name=f"ffn_mid_fwd_{l}")
        act = act3.reshape(N, F)
        f = _mm(act, W[("w_ffn_down", l)], name=f"mm_down_{l}")
        x_next = _rms_fwd(f, vec(g_ffn_post[l]), F32, res=x1, name=f"rms_ffnpost_{l}")
        if l + 1 < depth:
            gather_wait(pending, x_next)
        sv.update(h1=h1, memn=memn, mkv3=mkv3, proj3=proj3, q_off=q_off, y_main=y_main, y_mem=y_mem, y=y, x1=x1,
                  h2=h2, ug3=ug3, uv3=uv3, act=act, f=f)
        saved.append(sv)
        xs = x_next

    dxs, loss_vec = _loss_bwd(xs, loss_target.reshape(N, D))
    loss = lax.psum(jnp.sum(loss_vec), ("x", "y", "c"))

    G = {n: [None] * w_loc[n].shape[0] for n in REPL + SMALL_SHARDED if n != "g_kv"}
    GW = {}

    def dw(key, off, a, b_, nm):
        GW[key] = _mm(a, b_, ta=True, out_dtype=BF16, into=(GW.get(key), (1,) + W[key].shape, 0, off), name=nm)

    def grad_blocks(key):
        g = GW[key]
        return g.reshape(1, N_DEV, g.shape[1] // N_DEV, g.shape[2])

    reduces = []

    def reduce_start(keys, after, tag):
        srcs = [grad_blocks(kk) for kk in keys]
        lands = [lax.empty((len(FLIPS),) + s.shape[:1] + s.shape[2:], s.dtype) for s in srcs]
        started = _split_start(False, srcs, lands, after, name=f"rs_start_{tag}")
        reduces.append((keys, tag) + started)
        return started[-1]

    kv_parts = []
    for l in reversed(range(depth)):
        sv = saved[l]
        proj3 = sv["proj3"]
        df, dg = _rms_bwd(sv["f"], vec(g_ffn_post[l]), dxs, out_dtype=BF16, name=f"rmsb_ffnpost_{l}")
        G["g_ffn_post"][l] = dg[0]
        dact = _mm(df, W[("w_ffn_down", l)], tb=True, name=f"mmb_down_dx_{l}")
        dw(("w_ffn_down", l), (0, 0), sv["act"], df, f"mmb_down_dw_{l}")
        dug3, duv3, dwg, dwv, dbg, dbv = _ffn_mid_bwd(sv["ug3"], sv["uv3"], dact.reshape(Bl, T, F),
                                                      W["w_ffn_conv"][l], vec(b_ffn_conv[l]), name=f"ffn_mid_bwd_{l}")
        G["w_ffn_conv"][l] = jnp.concatenate([dwg, dwv], axis=1)
        G["b_ffn_conv"][l] = jnp.concatenate([dbg, dbv], axis=1)[0]
        dug, duv = dug3.reshape(N, F), duv3.reshape(N, F)
        dw(("w_ffn_up", l), (0, 0), dug, sv["h2"], f"mmb_up_dw_g_{l}")
        dw(("w_ffn_up", l), (F, 0), duv, sv["h2"], f"mmb_up_dw_v_{l}")
        tok = reduce_start([("w_ffn_down", l), ("w_ffn_up", l)], dug, f"ffn_{l}")
        dh2 = _mm(dug, W[("w_ffn_up", l)], n=D, k=F, after=tok, name=f"mmb_up_dx_g_{l}")
        dh2 = _mm(duv, W[("w_ffn_up", l)], n=D, k=F, b_off=(F, 0), add=dh2, name=f"mmb_up_dx_v_{l}")
        dx1, dg = _rms_bwd(sv["x1"], vec(g_ffn_pre[l]), dh2, add=dxs, name=f"rmsb_ffnpre_{l}")
        G["g_ffn_pre"][l] = dg[0]
        dy, dg = _rms_bwd(sv["y"], vec(g_mix_post[l]), dx1, out_dtype=BF16, name=f"rmsb_mixpost_{l}")
        G["g_mix_post"][l] = dg[0]
        dy_main = _mm(dy, W[("w_mix_out", l)], tb=True, n=MIX_W, k=D, name=f"mmb_mixout_dmain_{l}")
        dy_mem = _mm(dy, W[("w_mix_out", l)], tb=True, n=MEM_W, k=D, b_off=(MIX_W, 0),
                     name=f"mmb_mixout_dmem_{l}")
        dw(("w_mix_out", l), (0, 0), sv["y_main"], dy, f"mmb_mixout_dw_main_{l}")
        dw(("w_mix_out", l), (MIX_W, 0), sv["y_mem"], dy, f"mmb_mixout_dw_mem_{l}")
        dq_mem3, dmkv3 = _mem_attn_bwd(proj3, sv["q_off"], sv["mkv3"], dy_mem.reshape(Bl, T, MEM_W),
                                       name=f"memattn_bwd_{l}")
        dq_mem = dq_mem3.reshape(N, MEM_W)
        dmkv = dmkv3.reshape(Bl * Ml, 2 * MEM_W)
        dw(("w_mem_kv", l), (0, 0), sv["memn"], dmkv, f"mmb_memkv_dw_{l}")
        dmemn = _mm(dmkv, W[("w_mem_kv", l)], tb=True, name=f"mmb_memkv_dx_{l}")
        _, dg = _rms_bwd(mem2, vec(g_mem[l]), dmemn, name=f"rmsb_mem_{l}")
        G["g_mem"][l] = dg[0]
        dy_main3 = dy_main.reshape(Bl, T, MIX_W)
        if l < n_a:
            j = l
            dxc3, drp3, dip3, dugate3, dbr, dbi, dlam = _rglru_bwd(
                dy_main3, sv["xc3"], sv["gates3"], proj3, sv["hs3"], vec(b_rg_r[j]), vec(b_rg_i[j]),
                vec(W["lru_lambda"][j]), name=f"rglru_bwd_{l}")
            G["b_rg_r"][j] = dbr.reshape(nblk, bsz)
            G["b_rg_i"][j] = dbi.reshape(nblk, bsz)
            G["lru_lambda"][j] = dlam[0]
            drp, dip = drp3.reshape(N, MIX_W), dip3.reshape(N, MIX_W)
            xc2 = sv["xc3"].reshape(N, MIX_W)
            G["w_rg_r"][j] = _diag_blocks(_mm(xc2, drp, ta=True, name=f"mmb_gates_dw_r_{l}"), nblk, bsz)
            G["w_rg_i"][j] = _diag_blocks(_mm(xc2, dip, ta=True, name=f"mmb_gates_dw_i_{l}"), nblk, bsz)
            dxc = _mm(drp, wbd[j], tb=True, n=MIX_W, k=MIX_W, add=dxc3.reshape(N, MIX_W), name=f"mmb_gates_dx_r_{l}")
            dxc = _mm(dip, wbd[j], tb=True, n=MIX_W, k=MIX_W, b_off=(0, MIX_W), add=dxc, name=f"mmb_gates_dx_i_{l}")
            dux3, dwc, dbc = _conv_bwd_call(dxc.reshape(Bl, T, MIX_W), proj3, MIX_W, MIX_W, W["w_conv_a"][j],
                                            name=f"conv_a_bwd_{l}")
            G["w_conv_a"][j] = dwc
            G["b_conv_a"][j] = dbc[0]
            pieces = [(dugate3.reshape(N, MIX_W), 0), (dux3.reshape(N, MIX_W), MIX_W), (dq_mem, 2 * MIX_W)]
            in_key = ("w_in_a", j)
        else:
            j = l - n_a
            dq3, dkc, dkp, dsk = _swa_bwd(proj3, kv3, sinks_b[j], dy_main3, name=f"swa_bwd_{l}")
            kv_parts.append((dkc, dkp))
            G["sinks_b"][j] = dsk[0, :SWA_HEADS]
            pieces = [(dq3.reshape(N, MIX_W), 0), (dq_mem, MIX_W)]
            in_key = ("w_in_b", j)
        in_t = in_key[0] in TRANSPOSED
        for pi, (piece, off) in enumerate(pieces):
            if in_t:
                dw(in_key, (off, 0), piece, sv["h1"], f"mmb_in_dw_{pi}_{l}")
            else:
                dw(in_key, (0, off), sv["h1"], piece, f"mmb_in_dw_{pi}_{l}")
        tok = reduce_start([("w_mix_out", l), ("w_mem_kv", l), in_key], dy, f"mix_{l}")
        dh1 = None
        for pi, (piece, off) in enumerate(pieces):
            dh1 = _mm(piece, W[in_key], tb=not in_t, n=D, k=piece.shape[1], b_off=(off, 0) if in_t else (0, off),
                      add=dh1, after=tok if pi == 0 else None, name=f"mmb_in_dx_{pi}_{l}")
        dxs, dg = _rms_bwd(sv["x0"], vec(g_mix_pre[l]), dh1, add=dx1, name=f"rmsb_mixpre_{l}")
        G["g_mix_pre"][l] = dg[0]
        if l == n_a:
            dkv = _kv_grad_combine(kv_parts, name="kv_grad_combine").reshape(N, 2 * MEM_W)
            dw(("w_kv", None), (0, 0), kvn, dkv, "mmb_kv_dw")
            tok = reduce_start([("w_kv", None)], dkv, "kv")
            dkvn = _mm(dkv, W[("w_kv", None)], tb=True, after=tok, name="mmb_kv_dx")
            dxs, dg = _rms_bwd(x_kv, vec(g_kv), dkvn, add=dxs, name="rmsb_kv")
            G["g_kv"] = dg[0]
    grad_x = dxs.reshape(Bl, T, D)
    Gf = {n: (jnp.stack(g) if isinstance(g, list) else g) for n, g in G.items()}

    parts = {}
    for keys, tag, ssem, rsem, srcs, lands, _ in reduces:
        for kk, s, g7 in zip(keys, srcs, _split_wait(False, ssem, rsem, srcs, lands, dxs, name=f"rs_wait_{tag}")):
            parts[kk] = (s, g7)
    g4 = []
    for n in SMALL_SHARDED:
        t = _small_to_cols(Gf[n]).astype(BF16)
        g4.append(t.reshape(1, N_DEV, -1, t.shape[-1]))
    got = _rs_sibling(g4, name="rs_sibling")
    psum4 = [_pair_sum(g, r, name=f"rs_pair_sum_{n}") for n, g, r in zip(SMALL_SHARDED, g4, got)]
    got2 = _rs_chips(psum4, name="rs_chips")
    r_blob = _pack([Gf[n].astype(F32) for n in REPL], REPL_ROWS, F32)
    r_parts = _all_gather(r_blob, name="ag_repl_grads")

    res = [{} for _ in range(4)]
    for n, p4, g2 in zip(SMALL_SHARDED, psum4, got2):
        shp3 = p4.shape[1:]
        outs = _adamw_sharded(p4, g2, w_loc[n].reshape(shp3), m_loc[n].reshape(shp3), v_loc[n].reshape(shp3),
                              name=f"adamw_{n}")
        for k in range(4):
            res[k][n] = outs[k].reshape(w_loc[n].shape)
    for n, _ in SHARDED:
        if n in SMALL_SHARDED:
            continue
        idx = [None] if w_loc[n].ndim == 2 else list(range(w_loc[n].shape[0]))
        wmv = [as_rows(n, a[n]) for a in (w_loc, m_loc, v_loc)]
        shp3 = (len(idx),) + wmv[0].shape[-2:]
        outs = _adamw_layers([parts[(n, i)][0] for i in idx], [parts[(n, i)][1] for i in idx],
                             *[a.reshape(shp3) for a in wmv], name=f"adamw_{n}")
        for k in range(4):
            res[k][n] = as_rows(n, outs[k].reshape(wmv[0].shape))
    outs_rp = _adamw_replicated(r_parts, _pack([w_loc[n] for n in REPL], REPL_ROWS, F32),
                                _pack([m_loc[n] for n in REPL], REPL_ROWS, F32),
                                _pack([v_loc[n] for n in REPL], REPL_ROWS, F32),
                                name="adamw_replicated")
    rp_shapes = [w_loc[n].shape for n in REPL]
    for k in range(4):
        res[k].update(zip(REPL, _unpack(outs_rp[k], rp_shapes)))
    out = [loss, grad_x]
    for k in range(4):
        out += [res[k][n] for n in WEIGHTS]
    return tuple(out)
```

```python
import functools
import math

import numpy as np
import jax
import jax.numpy as jnp
from jax import lax
from jax.experimental import pallas as pl
from jax.experimental.pallas import tpu as pltpu

F32 = jnp.float32
BF16 = jnp.bfloat16
S = jax.ShapeDtypeStruct
MESH = pl.DeviceIdType.MESH
ANY = pl.BlockSpec(memory_space=pl.ANY)

HEAD = 64
MEM_HEADS = 4
MEM_W = MEM_HEADS * HEAD
SWA_HEADS = 12
SWA_GROUP = 3
MIX_W = SWA_HEADS * HEAD
WIN = 128
LRU_C = 8.0
EPS = 1e-6
ADAM_LR, ADAM_B1, ADAM_B2, ADAM_EPS, ADAM_WD, ADAM_STEP = 0.001, 0.9, 0.999, 1e-08, 0.01, 10
GELU_C0 = math.sqrt(2.0 / math.pi)
GELU_C1 = 0.044715
N_DEV = 8
LANES = 128
CT = 128
VMEM_LIMIT = 48 * 1024 * 1024
MM_VMEM_BUDGET = 36 * 1024 * 1024
REPL_ROWS = 256

SHARDED = (("w_mem_kv", 1), ("w_mix_out", 1), ("w_ffn_up", 2), ("w_ffn_conv", 2), ("w_ffn_down", 1), ("w_in_a", 2),
           ("w_conv_a", 2), ("b_conv_a", 1), ("lru_lambda", 1), ("w_in_b", 1), ("w_kv", 0))
SMALL_SHARDED = ("w_ffn_conv", "w_conv_a", "b_conv_a", "lru_lambda")
TRANSPOSED = ("w_ffn_up", "w_in_a")
REPL = ("g_mix_pre", "g_mix_post", "g_ffn_pre", "g_ffn_post", "g_mem", "b_ffn_conv", "w_rg_r", "b_rg_r", "w_rg_i",
        "b_rg_i", "sinks_b", "g_kv")
WEIGHTS = ("g_mix_pre", "g_mix_post", "g_ffn_pre", "g_ffn_post", "g_mem", "w_mem_kv", "w_mix_out", "w_ffn_up",
           "w_ffn_conv", "b_ffn_conv", "w_ffn_down", "w_in_a", "w_conv_a", "b_conv_a", "w_rg_r", "b_rg_r", "w_rg_i",
           "b_rg_i", "lru_lambda", "w_in_b", "sinks_b", "g_kv", "w_kv")


def _alibi_slopes(n):
    def pow2(m):
        start = 2.0 ** (-8.0 / m)
        return [start ** (i + 1) for i in range(m)]
    c = 2 ** int(math.floor(math.log2(n)))
    s = pow2(c)
    if c != n:
        s = s + pow2(2 * c)[0::2][: n - c]
    return [float(v) for v in np.asarray(s, dtype=np.float32)]


SLOPES = _alibi_slopes(SWA_HEADS)


def _tile(n, cands):
    for c in cands:
        if n % c == 0:
            return c
    return n


def _cparams(*sem):
    return pltpu.CompilerParams(dimension_semantics=sem, vmem_limit_bytes=VMEM_LIMIT)


def _mm_tiles(M, N, K, a_bytes, b_bytes, o_bytes, add_bytes, offsets):
    m_off, n_offs, k_off = offsets
    tms = [c for c in (1024, 512, 256, 128) if M % c == 0 and m_off % c == 0] or [M]
    tns = [c for c in (1408, 1024, 896, 768, 512, 384, 256, 128)
           if N % c == 0 and all(o % c == 0 for o in n_offs)] or [N]
    tks = [c for c in (K, 2048, 1408, 1024, 512, 256, 128) if c <= K and K % c == 0 and k_off % c == 0]
    best = None
    for tk in tks:
        fits = []
        for tm in tms:
            for tn in tns:
                need = 2 * (tm * tk * a_bytes + tk * tn * b_bytes + tm * tn * (o_bytes + add_bytes))
                need += tm * tn * 4 * (2 if tk < K else 1)
                need += (tm * tk * 2 if a_bytes != 2 else 0) + (tk * tn * 2 if b_bytes != 2 else 0)
                if need <= MM_VMEM_BUDGET:
                    fits.append((tm * tn, min(tm, 512), tm, tn))
        if fits:
            _, _, tm, tn = max(fits)
            best = (tm, tn, tk)
            break
    assert best is not None, (M, N, K)
    return best


def _mm(a, b, *, ta=False, tb=False, n=None, k=None, b_off=(0, 0), out_dtype=F32, add=None, into=None, after=None,
        name="mm"):
    if ta:
        K, M = a.shape
    else:
        M, K = a.shape
    if tb:
        N = b.shape[-2] if n is None else n
    else:
        N = b.shape[-1] if n is None else n
    assert k is None or k == K
    ro, co = b_off
    n_off, k_off = (ro, co) if tb else (co, ro)
    oro, oco = (0, 0) if into is None else into[3]
    tm, tn, tk = _mm_tiles(M, N, K, a.dtype.itemsize, b.dtype.itemsize, jnp.dtype(out_dtype).itemsize,
                           0 if add is None else add.dtype.itemsize, (oro, (n_off, oco), k_off))
    nk = K // tk
    if tb:
        b_spec = pl.BlockSpec((tn, tk), lambda i, j, kk: (j + ro // tn, kk + co // tk))
        b_dims = (1,)
    else:
        b_spec = pl.BlockSpec((tk, tn), lambda i, j, kk: (kk + ro // tk, j + co // tn))
        b_dims = (0,)
    if ta:
        a_spec = pl.BlockSpec((tk, tm), lambda i, j, kk: (kk, i))
        a_dims = (0,)
    else:
        a_spec = pl.BlockSpec((tm, tk), lambda i, j, kk: (i, kk))
        a_dims = (1,)
    dims = ((a_dims, b_dims), ((), ()))
    add_spec = pl.BlockSpec((tm, tn), lambda i, j, kk: (i, j))
    has_add = add is not None
    if into is None:
        o_spec, o_shape, buf = add_spec, (M, N), None
    else:
        buf, o_shape, ol, _ = into
        assert not has_add
        o_spec = pl.BlockSpec((None, tm, tn), lambda i, j, kk: (ol, i + oro // tm, j + oco // tn))
    has_buf = buf is not None

    def body(*refs):
        refs = list(refs)
        acc_ref = refs.pop() if nk > 1 else None
        o_ref = refs.pop()
        a_ref, b_ref = refs[0], refs[1]
        add_ref = refs[2] if has_add else None
        part = lax.dot_general(a_ref[...].astype(BF16), b_ref[...].astype(BF16), dims, preferred_element_type=F32)

        def finish(r):
            if has_add:
                r = r + add_ref[...].astype(F32)
            o_ref[...] = r.astype(out_dtype)

        if nk == 1:
            finish(part)
        else:
            kk = pl.program_id(2)

            @pl.when(kk == 0)
            def _():
                acc_ref[...] = part

            @pl.when(kk > 0)
            def _():
                acc_ref[...] += part

            @pl.when(kk == nk - 1)
            def _():
                finish(acc_ref[...])

    in_specs = [a_spec, b_spec] + ([add_spec] if has_add else []) + ([ANY] if has_buf else [])
    args = (a, b) + ((add,) if has_add else ()) + ((buf,) if has_buf else ())
    if after is not None:
        in_specs, args = in_specs + [ANY], args + (after,)
    return pl.pallas_call(
        body, grid=(M // tm, N // tn, nk), in_specs=in_specs, out_specs=o_spec,
        out_shape=S(o_shape, out_dtype), scratch_shapes=[pltpu.VMEM((tm, tn), F32)] if nk > 1 else [],
        input_output_aliases={2: 0} if has_buf else {},
        compiler_params=_cparams("parallel", "parallel", "arbitrary"), name=name)(*args)


def _rms_fwd(x, g, out_dtype, res=None, name="rms_fwd"):
    N, D = x.shape
    tm = _tile(N, (512, 256, 128))
    has_res = res is not None

    def body(*refs):
        if has_res:
            x_ref, g_ref, r_ref, o_ref = refs
        else:
            x_ref, g_ref, o_ref = refs
        xv = x_ref[...].astype(F32)
        y = xv * lax.rsqrt(jnp.mean(xv * xv, axis=-1, keepdims=True) + EPS) * g_ref[...]
        if has_res:
            y = y + r_ref[...]
        o_ref[...] = y.astype(out_dtype)

    row = pl.BlockSpec((tm, D), lambda i: (i, 0))
    vec = pl.BlockSpec((1, D), lambda i: (0, 0))
    return pl.pallas_call(
        body, grid=(N // tm,), in_specs=[row, vec] + ([row] if has_res else []), out_specs=row,
        out_shape=S((N, D), out_dtype), compiler_params=_cparams("parallel"), name=name)(
            *((x, g) + ((res,) if has_res else ())))


def _rms_bwd(x, g, dy, add=None, out_dtype=F32, name="rms_bwd"):
    N, D = x.shape
    tm = _tile(N, (512, 256, 128))
    has_add = add is not None

    def body(*refs):
        if has_add:
            x_ref, g_ref, dy_ref, add_ref, dx_ref, dg_ref = refs
        else:
            x_ref, g_ref, dy_ref, dx_ref, dg_ref = refs
        xv = x_ref[...].astype(F32)
        dyv = dy_ref[...].astype(F32)
        r = lax.rsqrt(jnp.mean(xv * xv, axis=-1, keepdims=True) + EPS)
        u = dyv * g_ref[...]
        dx = r * u - xv * (r * r * r * jnp.mean(u * xv, axis=-1, keepdims=True))
        if has_add:
            dx = dx + add_ref[...]
        dx_ref[...] = dx.astype(out_dtype)

        @pl.when(pl.program_id(0) == 0)
        def _():
            dg_ref[...] = jnp.zeros_like(dg_ref)

        dg_ref[...] += jnp.sum(dyv * xv * r, axis=0, keepdims=True)

    row = pl.BlockSpec((tm, D), lambda i: (i, 0))
    vec = pl.BlockSpec((1, D), lambda i: (0, 0))
    return pl.pallas_call(
        body, grid=(N // tm,), in_specs=[row, vec, row] + ([row] if has_add else []), out_specs=(row, vec),
        out_shape=(S((N, D), out_dtype), S((1, D), F32)), compiler_params=_cparams("arbitrary"), name=name)(
            *((x, g, dy) + ((add,) if has_add else ())))


def _shift_down(x, s, row):
    return jnp.where(row >= s, pltpu.roll(x, s, axis=0), 0.0)


def _shift_up(x, s, row):
    T = x.shape[0]
    return jnp.where(row < T - s, pltpu.roll(x, T - s, axis=0), 0.0)


SLAB = 16


def _conv_wrap(x, w_ref, b_ref):
    W = w_ref.shape[0]
    y = x * w_ref[W - 1:W, :] + b_ref[...]
    for s in range(1, W):
        y = y + pltpu.roll(x, s, axis=0) * w_ref[W - 1 - s:W - s, :]
    return y


def _conv_head(x_head, w_ref, b_ref):
    row = lax.broadcasted_iota(jnp.int32, x_head.shape, 0)
    W = w_ref.shape[0]
    y = x_head * w_ref[W - 1:W, :] + b_ref[...]
    for s in range(1, W):
        y = y + _shift_down(x_head, s, row) * w_ref[W - 1 - s:W - s, :]
    return y


def _conv_bwd_wrap(dy, x, w_ref):
    W = w_ref.shape[0]
    T = dy.shape[0]
    dx = dy * w_ref[W - 1:W, :]
    dws = [None] * W
    dws[W - 1] = jnp.sum(dy * x, axis=0, keepdims=True)
    for s in range(1, W):
        up = pltpu.roll(dy, T - s, axis=0)
        dx = dx + up * w_ref[W - 1 - s:W - s, :]
        dws[W - 1 - s] = jnp.sum(up * x, axis=0, keepdims=True)
    return dx, jnp.concatenate(dws, axis=0), jnp.sum(dy, axis=0, keepdims=True)


def _conv_bwd_fix(dy_head, dy_tail, x_tail, w_ref):
    row = lax.broadcasted_iota(jnp.int32, dy_tail.shape, 0)
    W = w_ref.shape[0]
    dx = dy_tail * w_ref[W - 1:W, :]
    extra = [jnp.zeros((1, dy_tail.shape[1]), F32)] * W
    for s in range(1, W):
        dx = dx + _shift_up(dy_tail, s, row) * w_ref[W - 1 - s:W - s, :]
        extra[W - 1 - s] = jnp.sum(jnp.where(row < s, dy_head * pltpu.roll(x_tail, s, axis=0), 0.0), axis=0,
                                   keepdims=True)
    return dx, jnp.concatenate(extra, axis=0)


def _gelu(g):
    t = jnp.tanh(GELU_C0 * (g + GELU_C1 * g * g * g))
    return 0.5 * g * (1.0 + t), t


def _dgelu(g, t):
    return 0.5 * (1.0 + t) + 0.5 * g * (1.0 - t * t) * (GELU_C0 * (1.0 + 3.0 * GELU_C1 * g * g))


def _cspec(T, off=0):
    return pl.BlockSpec((1, T, CT), lambda j, b: (b, 0, j + off))


def _pspec(rows, off=0):
    return pl.BlockSpec((rows, CT), lambda j, b: (0, j + off))


def _conv_fwd_call(x3, x_off, C, w, b, name):
    Bl, T, _ = x3.shape
    W = w.shape[0]

    def body(x_ref, w_ref, b_ref, o_ref):
        o_ref[0] = _conv_wrap(x_ref[0], w_ref, b_ref)
        o_ref[0, 0:SLAB, :] = _conv_head(x_ref[0, 0:SLAB, :], w_ref, b_ref)

    return pl.pallas_call(
        body, grid=(C // CT, Bl), in_specs=[_cspec(T, x_off // CT), _pspec(W), _pspec(1)], out_specs=_cspec(T),
        out_shape=S((Bl, T, C), F32), compiler_params=_cparams("parallel", "arbitrary"), name=name)(x3, w, b)


def _conv_bwd_call(dy3, x3, x_off, C, w, name):
    Bl, T, _ = x3.shape
    W = w.shape[0]

    def body(dy_ref, x_ref, w_ref, dx_ref, dw_ref, db_ref):
        dx, dw, db = _conv_bwd_wrap(dy_ref[0], x_ref[0], w_ref)
        dx_tail, dw_extra = _conv_bwd_fix(dy_ref[0, 0:SLAB, :], dy_ref[0, T - SLAB:T, :], x_ref[0, T - SLAB:T, :],
                                          w_ref)
        dx_ref[0] = dx.astype(BF16)
        dx_ref[0, T - SLAB:T, :] = dx_tail.astype(BF16)

        @pl.when(pl.program_id(1) == 0)
        def _():
            dw_ref[...] = jnp.zeros_like(dw_ref)
            db_ref[...] = jnp.zeros_like(db_ref)

        dw_ref[...] += dw - dw_extra
        db_ref[...] += db

    return pl.pallas_call(
        body, grid=(C // CT, Bl), in_specs=[_cspec(T), _cspec(T, x_off // CT), _pspec(W)],
        out_specs=(_cspec(T), _pspec(W), _pspec(1)),
        out_shape=(S((Bl, T, C), BF16), S((W, C), F32), S((1, C), F32)),
        compiler_params=_cparams("parallel", "arbitrary"), name=name)(dy3, x3, w)


def _ffn_mid_fwd(ug3, uv3, wc, bc, name):
    Bl, T, F = ug3.shape
    nf = F // CT

    def body(ug_ref, uv_ref, wg_ref, wv_ref, bg_ref, bv_ref, o_ref, g_ref, v_ref):
        g = _conv_wrap(ug_ref[0], wg_ref, bg_ref)
        v = _conv_wrap(uv_ref[0], wv_ref, bv_ref)
        g_ref[0] = g
        v_ref[0] = v
        o_ref[0] = (_gelu(g)[0] * v).astype(BF16)
        g = _conv_head(ug_ref[0, 0:SLAB, :], wg_ref, bg_ref)
        v = _conv_head(uv_ref[0, 0:SLAB, :], wv_ref, bv_ref)
        g_ref[0, 0:SLAB, :] = g
        v_ref[0, 0:SLAB, :] = v
        o_ref[0, 0:SLAB, :] = (_gelu(g)[0] * v).astype(BF16)

    return pl.pallas_call(
        body, grid=(nf, Bl),
        in_specs=[_cspec(T), _cspec(T), _pspec(3), _pspec(3, nf), _pspec(1), _pspec(1, nf)],
        out_specs=(_cspec(T), _cspec(T), _cspec(T)),
        out_shape=(S((Bl, T, F), BF16), S((Bl, T, F), F32), S((Bl, T, F), F32)),
        compiler_params=_cparams("parallel", "arbitrary"), name=name)(ug3, uv3, wc, wc, bc, bc)


def _ffn_mid_bwd(ug3, uv3, g3, v3, dact3, wc, name):
    Bl, T, F = ug3.shape
    nf = F // CT

    def body(ug_ref, uv_ref, g_ref, v_ref, da_ref, wg_ref, wv_ref, dug_ref, duv_ref, dwg_ref, dwv_ref, dbg_ref,
             dbv_ref):
        g = g_ref[0]
        da = da_ref[0]
        gel, t = _gelu(g)
        dg = da * v_ref[0] * _dgelu(g, t)
        dv = da * gel
        tail = slice(T - SLAB, T)
        dug, dwg, dbg = _conv_bwd_wrap(dg, ug_ref[0], wg_ref)
        dug_tail, dwg_extra = _conv_bwd_fix(dg[0:SLAB], dg[tail], ug_ref[0, tail, :], wg_ref)
        duv, dwv, dbv = _conv_bwd_wrap(dv, uv_ref[0], wv_ref)
        duv_tail, dwv_extra = _conv_bwd_fix(dv[0:SLAB], dv[tail], uv_ref[0, tail, :], wv_ref)
        dug_ref[0] = dug.astype(BF16)
        duv_ref[0] = duv.astype(BF16)
        dug_ref[0, tail, :] = dug_tail.astype(BF16)
        duv_ref[0, tail, :] = duv_tail.astype(BF16)

        @pl.when(pl.program_id(1) == 0)
        def _():
            dwg_ref[...] = jnp.zeros_like(dwg_ref)
            dwv_ref[...] = jnp.zeros_like(dwv_ref)
            dbg_ref[...] = jnp.zeros_like(dbg_ref)
            dbv_ref[...] = jnp.zeros_like(dbv_ref)

        dwg_ref[...] += dwg - dwg_extra
        dwv_ref[...] += dwv - dwv_extra
        dbg_ref[...] += dbg
        dbv_ref[...] += dbv

    return pl.pallas_call(
        body, grid=(nf, Bl),
        in_specs=[_cspec(T)] * 5 + [_pspec(3), _pspec(3, nf)],
        out_specs=(_cspec(T), _cspec(T), _pspec(3), _pspec(3), _pspec(1), _pspec(1)),
        out_shape=(S((Bl, T, F), BF16), S((Bl, T, F), BF16), S((3, F), F32), S((3, F), F32), S((1, F), F32),
                   S((1, F), F32)),
        compiler_params=_cparams("parallel", "arbitrary"), name=name)(ug3, uv3, g3, v3, dact3, wc, wc)


def _lru_gates(xc, rp, ip, br_ref, bi_ref, lam_ref):
    r = jax.nn.sigmoid(rp + br_ref[...])
    i = jax.nn.sigmoid(ip + bi_ref[...])
    lam = lam_ref[...]
    sp = jnp.maximum(-lam, 0.0) + jnp.log1p(jnp.exp(-jnp.abs(lam)))
    log_a = (-LRU_C) * r * sp
    a = jnp.exp(log_a)
    z = 2.0 * log_a
    one_m_a2 = jnp.where(z > -0.05, -z * (1.0 + z * (0.5 + z * (1.0 / 6.0 + z * (1.0 / 24.0)))), 1.0 - a * a)
    mult = jnp.sqrt(one_m_a2)
    return r, i, sp, a, mult


def _rglru_fwd(xc3, gates3, proj3, br, bi, lam, name):
    Bl, T, C = xc3.shape
    nsteps = int(math.log2(T))
    assert 1 << nsteps == T

    def body(xc_ref, rp_ref, ip_ref, ug_ref, br_ref, bi_ref, lam_ref, y_ref, h_ref):
        row = lax.broadcasted_iota(jnp.int32, (T, CT), 0)
        xc = xc_ref[0]
        r, i, sp, a, mult = _lru_gates(xc, rp_ref[0], ip_ref[0], br_ref, bi_ref, lam_ref)
        b = mult * (i * xc)
        for st in range(nsteps):
            s = 1 << st
            a_sh = jnp.where(row >= s, pltpu.roll(a, s, axis=0), 1.0)
            b = a * _shift_down(b, s, row) + b
            a = a * a_sh
        h_ref[0] = b
        y_ref[0] = (b * _gelu(ug_ref[0])[0]).astype(BF16)

    return pl.pallas_call(
        body, grid=(C // CT, Bl),
        in_specs=[_cspec(T), _cspec(T), _cspec(T, C // CT), _cspec(T), _pspec(1), _pspec(1), _pspec(1)],
        out_specs=(_cspec(T), _cspec(T)), out_shape=(S((Bl, T, C), BF16), S((Bl, T, C), F32)),
        compiler_params=_cparams("parallel", "arbitrary"), name=name)(xc3, gates3, gates3, proj3, br, bi, lam)


def _rglru_bwd(dy3, xc3, gates3, proj3, h3, br, bi, lam, name):
    Bl, T, C = xc3.shape
    nsteps = int(math.log2(T))

    def body(dy_ref, xc_ref, rp_ref, ip_ref, ug_ref, h_ref, br_ref, bi_ref, lam_ref,
             dxc_ref, drp_ref, dip_ref, dug_ref, dbr_ref, dbi_ref, dlam_ref):
        row = lax.broadcasted_iota(jnp.int32, (T, CT), 0)
        xc = xc_ref[0]
        r, i, sp, a, mult = _lru_gates(xc, rp_ref[0], ip_ref[0], br_ref, bi_ref, lam_ref)
        h = h_ref[0]
        dy = dy_ref[0]
        ug = ug_ref[0]
        gel, t = _gelu(ug)
        dug_ref[0] = (dy * h * _dgelu(ug, t)).astype(BF16)
        gacc = dy * gel
        an = _shift_up(a, 1, row)
        for st in range(nsteps):
            s = 1 << st
            an_sh = jnp.where(row < T - s, pltpu.roll(an, T - s, axis=0), 1.0)
            gacc = an * _shift_up(gacc, s, row) + gacc
            an = an * an_sh
        da = gacc * _shift_down(h, 1, row)
        ix = i * xc
        d_mult = gacc * ix
        d_i = gacc * mult * xc
        dxc_ref[0] = gacc * mult * i
        d_log_a = da * a - d_mult * (a * a) / mult
        d_r = d_log_a * ((-LRU_C) * sp)
        d_sp = jnp.sum(d_log_a * ((-LRU_C) * r), axis=0, keepdims=True)
        drp = d_r * r * (1.0 - r)
        dip = d_i * i * (1.0 - i)
        drp_ref[0] = drp.astype(BF16)
        dip_ref[0] = dip.astype(BF16)

        @pl.when(pl.program_id(1) == 0)
        def _():
            dbr_ref[...] = jnp.zeros_like(dbr_ref)
            dbi_ref[...] = jnp.zeros_like(dbi_ref)
            dlam_ref[...] = jnp.zeros_like(dlam_ref)

        dbr_ref[...] += jnp.sum(drp, axis=0, keepdims=True)
        dbi_ref[...] += jnp.sum(dip, axis=0, keepdims=True)
        dlam_ref[...] += d_sp * (-jax.nn.sigmoid(-lam_ref[...]))

    vec = S((1, C), F32)
    act = S((Bl, T, C), BF16)
    return pl.pallas_call(
        body, grid=(C // CT, Bl),
        in_specs=[_cspec(T), _cspec(T), _cspec(T), _cspec(T, C // CT), _cspec(T), _cspec(T)] + [_pspec(1)] * 3,
        out_specs=(_cspec(T), _cspec(T), _cspec(T), _cspec(T), _pspec(1), _pspec(1), _pspec(1)),
        out_shape=(S((Bl, T, C), F32), act, act, act, vec, vec, vec),
        compiler_params=_cparams("parallel", "arbitrary"), name=name)(dy3, xc3, gates3, gates3, proj3, h3, br, bi, lam)


NT = (((1,), (1,)), ((), ()))
TN = (((0,), (0,)), ((), ()))


def _hs(h):
    return slice(h * HEAD, (h + 1) * HEAD)


def _mem_softmax(qb, kb):
    s = lax.dot_general(qb, kb, NT, preferred_element_type=F32) * (HEAD ** -0.5)
    e = jnp.exp(s - jnp.max(s, axis=-1, keepdims=True))
    return e / jnp.sum(e, axis=-1, keepdims=True)


def _mem_attn_fwd(proj3, q_off, mkv3, name):
    Bl, T, _ = proj3.shape
    M = mkv3.shape[1]
    tq = _tile(T, (512, 256, 128))

    def body(q_ref, k_ref, v_ref, o_ref):
        q = q_ref[0].astype(BF16)
        k = k_ref[0].astype(BF16)
        v = v_ref[0].astype(BF16)
        outs = []
        for h in range(MEM_HEADS):
            p = _mem_softmax(q[:, _hs(h)], k[:, _hs(h)])
            outs.append(jnp.dot(p.astype(BF16), v[:, _hs(h)], preferred_element_type=F32))
        o_ref[0] = jnp.concatenate(outs, axis=-1).astype(BF16)

    return pl.pallas_call(
        body, grid=(Bl, T // tq),
        in_specs=[pl.BlockSpec((1, tq, MEM_W), lambda b, t: (b, t, q_off // MEM_W)),
                  pl.BlockSpec((1, M, MEM_W), lambda b, t: (b, 0, 0)),
                  pl.BlockSpec((1, M, MEM_W), lambda b, t: (b, 0, 1))],
        out_specs=pl.BlockSpec((1, tq, MEM_W), lambda b, t: (b, t, 0)),
        out_shape=S((Bl, T, MEM_W), BF16), compiler_params=_cparams("parallel", "parallel"), name=name)(
            proj3, mkv3, mkv3)


def _mem_attn_bwd(proj3, q_off, mkv3, do3, name):
    Bl, T, _ = proj3.shape
    M = mkv3.shape[1]
    tq = _tile(T, (512, 256, 128))
    scale = HEAD ** -0.5

    def body(q_ref, k_ref, v_ref, do_ref, dq_ref, dkv_ref):
        q = q_ref[0].astype(BF16)
        k = k_ref[0].astype(BF16)
        v = v_ref[0].astype(BF16)
        do = do_ref[0].astype(BF16)
        dqs, dks, dvs = [], [], []
        for h in range(MEM_HEADS):
            qh, kh, vh, doh = q[:, _hs(h)], k[:, _hs(h)], v[:, _hs(h)], do[:, _hs(h)]
            p = _mem_softmax(qh, kh)
            dvs.append(lax.dot_general(p.astype(BF16), doh, TN, preferred_element_type=F32))
            dp = lax.dot_general(doh, vh, NT, preferred_element_type=F32)
            ds = (p * (dp - jnp.sum(p * dp, axis=-1, keepdims=True)) * scale).astype(BF16)
            dqs.append(jnp.dot(ds, kh, preferred_element_type=F32))
            dks.append(lax.dot_general(ds, qh, TN, preferred_element_type=F32))
        dq_ref[0] = jnp.concatenate(dqs, axis=-1).astype(BF16)

        @pl.when(pl.program_id(1) == 0)
        def _():
            dkv_ref[...] = jnp.zeros_like(dkv_ref)

        dkv_ref[0] += jnp.concatenate(dks + dvs, axis=-1)

    return pl.pallas_call(
        body, grid=(Bl, T // tq),
        in_specs=[pl.BlockSpec((1, tq, MEM_W), lambda b, t: (b, t, q_off // MEM_W)),
                  pl.BlockSpec((1, M, MEM_W), lambda b, t: (b, 0, 0)),
                  pl.BlockSpec((1, M, MEM_W), lambda b, t: (b, 0, 1)),
                  pl.BlockSpec((1, tq, MEM_W), lambda b, t: (b, t, 0))],
        out_specs=(pl.BlockSpec((1, tq, MEM_W), lambda b, t: (b, t, 0)),
                   pl.BlockSpec((1, M, 2 * MEM_W), lambda b, t: (b, 0, 0))),
        out_shape=(S((Bl, T, MEM_W), BF16), S((Bl, M, 2 * MEM_W), F32)),
        compiler_params=_cparams("parallel", "arbitrary"), name=name)(proj3, mkv3, mkv3, do3)


GROUP_ROWS = SWA_GROUP * WIN


def _group_rows(x, kvh):
    return jnp.concatenate([x[:, _hs(SWA_GROUP * kvh + g)] for g in range(SWA_GROUP)], axis=0)


def _group_col(vals):
    grp = lax.shift_right_logical(lax.broadcasted_iota(jnp.int32, (GROUP_ROWS, 1), 0), WIN.bit_length() - 1)
    col = jnp.full((GROUP_ROWS, 1), vals[-1], F32)
    for g in range(SWA_GROUP - 2, -1, -1):
        col = jnp.where(grp == g, vals[g], col)
    return col


def _swa_probs(qh, kph, kch, sink, slope, has_prev):
    qi = jnp.bitwise_and(lax.broadcasted_iota(jnp.int32, (GROUP_ROWS, WIN), 0), WIN - 1)
    kj = lax.broadcasted_iota(jnp.int32, (GROUP_ROWS, WIN), 1)
    scale = HEAD ** -0.5
    sp = lax.dot_general(qh, kph, NT, preferred_element_type=F32) * scale
    sc = lax.dot_general(qh, kch, NT, preferred_element_type=F32) * scale
    dist_p = (qi + WIN - kj).astype(F32)
    dist_c = (qi - kj).astype(F32)
    neg = -jnp.inf
    sp = jnp.where(kj > qi + jnp.where(has_prev, 0, WIN), sp - slope * dist_p, neg)
    sc = jnp.where(kj <= qi, sc - slope * dist_c, neg)
    m = jnp.maximum(jnp.maximum(jnp.max(sp, axis=-1, keepdims=True), jnp.max(sc, axis=-1, keepdims=True)), sink)
    ep = jnp.exp(sp - m)
    ec = jnp.exp(sc - m)
    es = jnp.exp(sink - m)
    inv = 1.0 / (jnp.sum(ep, axis=-1, keepdims=True) + jnp.sum(ec, axis=-1, keepdims=True) + es)
    return ep * inv, ec * inv, es * inv


def _swa_specs(nb):
    prev = lambda n: jnp.maximum(n - 1, 0)
    q = pl.BlockSpec((1, WIN, MIX_W), lambda b, n: (b, n, 0))
    kp = pl.BlockSpec((1, WIN, MEM_W), lambda b, n: (b, prev(n), 0))
    kc = pl.BlockSpec((1, WIN, MEM_W), lambda b, n: (b, n, 0))
    vp = pl.BlockSpec((1, WIN, MEM_W), lambda b, n: (b, prev(n), 1))
    vc = pl.BlockSpec((1, WIN, MEM_W), lambda b, n: (b, n, 1))
    sm = pl.BlockSpec(memory_space=pltpu.SMEM)
    return q, kp, kc, vp, vc, sm


def _swa_fwd(proj3, kv3, sinks, name):
    Bl, T, _ = proj3.shape
    nb = T // WIN
    q_s, kp_s, kc_s, vp_s, vc_s, sm = _swa_specs(nb)

    def body(q_ref, kp_ref, kc_ref, vp_ref, vc_ref, sink_ref, o_ref):
        has_prev = pl.program_id(1) > 0
        q = q_ref[0].astype(BF16)
        kp, kc = kp_ref[0].astype(BF16), kc_ref[0].astype(BF16)
        vp, vc = vp_ref[0].astype(BF16), vc_ref[0].astype(BF16)
        outs = []
        for kvh in range(SWA_HEADS // SWA_GROUP):
            kvs = _hs(kvh)
            heads = range(SWA_GROUP * kvh, SWA_GROUP * (kvh + 1))
            pp, pc, _ = _swa_probs(_group_rows(q, kvh), kp[:, kvs], kc[:, kvs], _group_col([sink_ref[h] for h in heads]),
                                   _group_col([SLOPES[h] for h in heads]), has_prev)
            og = (jnp.dot(pp.astype(BF16), vp[:, kvs], preferred_element_type=F32)
                  + jnp.dot(pc.astype(BF16), vc[:, kvs], preferred_element_type=F32))
            outs += [og[g * WIN:(g + 1) * WIN] for g in range(SWA_GROUP)]
        o_ref[0] = jnp.concatenate(outs, axis=-1).astype(BF16)

    return pl.pallas_call(
        body, grid=(Bl, nb), in_specs=[q_s, kp_s, kc_s, vp_s, vc_s, sm], out_specs=q_s,
        out_shape=S((Bl, T, MIX_W), BF16), compiler_params=_cparams("parallel", "parallel"), name=name)(
            proj3, kv3, kv3, kv3, kv3, sinks)


def _swa_bwd(proj3, kv3, sinks, do3, name):
    Bl, T, _ = proj3.shape
    nb = T // WIN
    q_s, kp_s, kc_s, vp_s, vc_s, sm = _swa_specs(nb)
    kv_s = pl.BlockSpec((1, WIN, 2 * MEM_W), lambda b, n: (b, n, 0))
    sk_s = pl.BlockSpec((8, LANES), lambda b, n: (0, 0))
    scale = HEAD ** -0.5

    def body(q_ref, kp_ref, kc_ref, vp_ref, vc_ref, sink_ref, do_ref, dq_ref, dkc_ref, dkp_ref, dsk_ref):
        has_prev = pl.program_id(1) > 0
        q = q_ref[0].astype(BF16)
        kp, kc = kp_ref[0].astype(BF16), kc_ref[0].astype(BF16)
        vp, vc = vp_ref[0].astype(BF16), vc_ref[0].astype(BF16)
        do = do_ref[0].astype(BF16)
        lane = lax.broadcasted_iota(jnp.int32, (8, LANES), 1)
        srow = lax.broadcasted_iota(jnp.int32, (8, LANES), 0)
        dsk = jnp.zeros((8, LANES), F32)
        dqs = []
        dkc, dkp, dvc, dvp = [], [], [], []
        grp = lax.shift_right_logical(lax.broadcasted_iota(jnp.int32, (GROUP_ROWS, 1), 0), WIN.bit_length() - 1)
        for kvh in range(SWA_HEADS // SWA_GROUP):
            kvs = _hs(kvh)
            heads = range(SWA_GROUP * kvh, SWA_GROUP * (kvh + 1))
            qg, dog = _group_rows(q, kvh), _group_rows(do, kvh)
            pp, pc, ps = _swa_probs(qg, kp[:, kvs], kc[:, kvs], _group_col([sink_ref[h] for h in heads]),
                                    _group_col([SLOPES[h] for h in heads]), has_prev)
            dpp = lax.dot_general(dog, vp[:, kvs], NT, preferred_element_type=F32)
            dpc = lax.dot_general(dog, vc[:, kvs], NT, preferred_element_type=F32)
            delta = jnp.sum(pp * dpp, axis=-1, keepdims=True) + jnp.sum(pc * dpc, axis=-1, keepdims=True)
            dsp = (pp * (dpp - delta) * scale).astype(BF16)
            dsc = (pc * (dpc - delta) * scale).astype(BF16)
            dqg = (jnp.dot(dsp, kp[:, kvs], preferred_element_type=F32)
                   + jnp.dot(dsc, kc[:, kvs], preferred_element_type=F32))
            dqs += [dqg[g * WIN:(g + 1) * WIN] for g in range(SWA_GROUP)]
            dkc.append(lax.dot_general(dsc, qg, TN, preferred_element_type=F32))
            dkp.append(lax.dot_general(dsp, qg, TN, preferred_element_type=F32))
            dvc.append(lax.dot_general(pc.astype(BF16), dog, TN, preferred_element_type=F32))
            dvp.append(lax.dot_general(pp.astype(BF16), dog, TN, preferred_element_type=F32))
            dsink = ps * delta
            for g, h in enumerate(heads):
                dsk = dsk + jnp.where((lane == h) & (srow == 0), -jnp.sum(jnp.where(grp == g, dsink, 0.0)), 0.0)
        dq_ref[0] = jnp.concatenate(dqs, axis=-1).astype(BF16)
        dkc_ref[0] = jnp.concatenate(dkc + dvc, axis=-1)
        dkp_ref[0] = jnp.concatenate(dkp + dvp, axis=-1)

        @pl.when((pl.program_id(0) == 0) & (pl.program_id(1) == 0))
        def _():
            dsk_ref[...] = jnp.zeros_like(dsk_ref)

        dsk_ref[...] += dsk

    return pl.pallas_call(
        body, grid=(Bl, nb), in_specs=[q_s, kp_s, kc_s, vp_s, vc_s, sm, q_s], out_specs=(q_s, kv_s, kv_s, sk_s),
        out_shape=(S((Bl, T, MIX_W), BF16), S((Bl, T, 2 * MEM_W), F32), S((Bl, T, 2 * MEM_W), F32), S((8, LANES), F32)),
        compiler_params=_cparams("arbitrary", "arbitrary"), name=name)(proj3, kv3, kv3, kv3, kv3, sinks, do3)


def _kv_grad_combine(parts, name):
    Bl, T, W = parts[0][0].shape
    nb = T // WIN
    nl = len(parts)

    def body(*refs):
        o_ref = refs[-1]
        has_next = jnp.where(pl.program_id(1) == nb - 1, 0.0, 1.0)
        acc = None
        for l in range(nl):
            c = refs[2 * l][0] + has_next * refs[2 * l + 1][0]
            acc = c if acc is None else acc + c
        o_ref[0] = acc.astype(BF16)

    cur = pl.BlockSpec((1, WIN, W), lambda b, n: (b, n, 0))
    nxt = pl.BlockSpec((1, WIN, W), lambda b, n: (b, jnp.minimum(n + 1, nb - 1), 0))
    return pl.pallas_call(
        body, grid=(Bl, nb), in_specs=[cur, nxt] * nl, out_specs=cur, out_shape=S((Bl, T, W), BF16),
        compiler_params=_cparams("parallel", "parallel"), name=name)(*[a for pr in parts for a in pr])


def _loss_bwd(y, target, name="loss"):
    N, D = y.shape
    tm = _tile(N, (512, 256, 128))

    def body(y_ref, t_ref, dy_ref, l_ref):
        e = y_ref[...] - t_ref[...]
        dy_ref[...] = e * (1.0 / D)

        @pl.when(pl.program_id(0) == 0)
        def _():
            l_ref[...] = jnp.zeros_like(l_ref)

        l_ref[...] += jnp.sum(e * e, axis=0, keepdims=True) * (0.5 / D)

    row = pl.BlockSpec((tm, D), lambda i: (i, 0))
    vec = pl.BlockSpec((1, D), lambda i: (0, 0))
    return pl.pallas_call(
        body, grid=(N // tm,), in_specs=[row, row], out_specs=(row, vec), out_shape=(S((N, D), F32), S((1, D), F32)),
        compiler_params=_cparams("arbitrary"), name=name)(y, target)


def _all_gather(x, name):
    R, C = x.shape

    def body(x_ref, out_ref, send_sems, recv_sems, local_sem):
        mx, my, mc = lax.axis_index("x"), lax.axis_index("y"), lax.axis_index("c")
        me, sibling = (mx, my, mc), (mx, my, 1 - mc)
        chips = [(1 - mx, my), (mx, 1 - my), (1 - mx, 1 - my)]

        def rows(px, py, pc):
            return out_ref.at[4 * px + 2 * py + pc]

        def copy(kk, block, to, src=None):
            return pltpu.make_async_remote_copy(
                src_ref=rows(*block) if src is None else src, dst_ref=rows(*block), send_sem=send_sems.at[kk],
                recv_sem=recv_sems.at[kk], device_id=to, device_id_type=MESH)

        mine = pltpu.make_async_copy(x_ref, rows(*me), local_sem)
        mine.start()
        first = [copy(0, me, sibling, src=x_ref)]
        first += [copy(1 + j, me, (*chip, mc), src=x_ref) for j, chip in enumerate(chips)]
        for cp in first:
            cp.start()
        passed = [copy(4 + j, (*chip, mc), sibling) for j, chip in enumerate(chips)]
        for j, chip in enumerate(chips):
            copy(1 + j, (*chip, mc), me).wait_recv()
            passed[j].start()
        copy(0, sibling, me).wait_recv()
        for j, chip in enumerate(chips):
            copy(4 + j, (*chip, 1 - mc), me).wait_recv()
        for cp in first + passed:
            cp.wait_send()
        mine.wait()

    return pl.pallas_call(
        body, out_shape=S((N_DEV, R, C), x.dtype), in_specs=[ANY], out_specs=ANY,
        scratch_shapes=[pltpu.SemaphoreType.DMA((7,)), pltpu.SemaphoreType.DMA((7,)), pltpu.SemaphoreType.DMA(())],
        name=name)(x)


def _ag_weights(shards, row_sharded, name):
    n = len(shards)

    def full_shape(a, rows):
        if rows:
            return a.shape[:-2] + (N_DEV * a.shape[-2],) + a.shape[-1:]
        return (N_DEV,) + a.shape

    def body(*refs):
        x_refs, o_refs = refs[:n], refs[n:2 * n]
        send_sems, recv_sems, local_sems = refs[2 * n:]
        mx, my, mc = lax.axis_index("x"), lax.axis_index("y"), lax.axis_index("c")
        me, sibling = (mx, my, mc), (mx, my, 1 - mc)
        chips = [(1 - mx, my), (mx, 1 - my), (1 - mx, 1 - my)]

        def dst(t, px, py, pc):
            d = 4 * px + 2 * py + pc
            if not row_sharded[t]:
                return o_refs[t].at[d]
            r = shards[t].shape[-2]
            idx = (slice(None),) * (shards[t].ndim - 2) + (pl.ds(pl.multiple_of(d * r, 16), r), slice(None))
            return o_refs[t].at[idx]

        def copy(kk, t, block, to, src=None):
            return pltpu.make_async_remote_copy(
                src_ref=dst(t, *block) if src is None else src, dst_ref=dst(t, *block),
                send_sem=send_sems.at[kk * n + t], recv_sem=recv_sems.at[kk * n + t], device_id=to,
                device_id_type=MESH)

        mine = [pltpu.make_async_copy(x_refs[t], dst(t, *me), local_sems.at[t]) for t in range(n)]
        for cp in mine:
            cp.start()
        first = []
        for t in range(n):
            first.append(copy(0, t, me, sibling, src=x_refs[t]))
            first += [copy(1 + j, t, me, (*chip, mc), src=x_refs[t]) for j, chip in enumerate(chips)]
        for cp in first:
            cp.start()
        passed = []
        for j, chip in enumerate(chips):
            for t in range(n):
                copy(1 + j, t, (*chip, mc), me).wait_recv()
                cp = copy(4 + j, t, (*chip, mc), sibling)
                cp.start()
                passed.append(cp)
        for t in range(n):
            copy(0, t, sibling, me).wait_recv()
            for j, chip in enumerate(chips):
                copy(4 + j, t, (*chip, 1 - mc), me).wait_recv()
        for cp in first + passed:
            cp.wait_send()
        for cp in mine:
            cp.wait()

    return pl.pallas_call(
        body, out_shape=tuple(S(full_shape(a, r), a.dtype) for a, r in zip(shards, row_sharded)),
        in_specs=[ANY] * n, out_specs=tuple([ANY] * n),
        scratch_shapes=[pltpu.SemaphoreType.DMA((7 * n,)), pltpu.SemaphoreType.DMA((7 * n,)),
                        pltpu.SemaphoreType.DMA((n,))],
        name=name)(*shards)


def _rs_sibling(gs, name):
    n = len(gs)

    def body(*refs):
        g_refs, o_refs = refs[:n], refs[n:2 * n]
        send_sems, recv_sems = refs[2 * n:]
        mx, my, mc = lax.axis_index("x"), lax.axis_index("y"), lax.axis_index("c")
        copies = [pltpu.make_async_remote_copy(
            src_ref=g_refs[t].at[:, 2 * j + (1 - mc)], dst_ref=o_refs[t].at[j], send_sem=send_sems.at[j * n + t],
            recv_sem=recv_sems.at[j * n + t], device_id=(mx, my, 1 - mc), device_id_type=MESH)
            for t in range(n) for j in range(4)]
        for cp in copies:
            cp.start()
        for cp in copies:
            cp.wait_recv()
        for cp in copies:
            cp.wait_send()

    return pl.pallas_call(
        body, out_shape=tuple(S((4, g.shape[0]) + g.shape[2:], g.dtype) for g in gs), in_specs=[ANY] * n,
        out_specs=tuple([ANY] * n),
        scratch_shapes=[pltpu.SemaphoreType.DMA((4 * n,)), pltpu.SemaphoreType.DMA((4 * n,))], name=name)(*gs)


def _rs_chips(ps, name):
    n = len(ps)

    def body(*refs):
        p_refs, o_refs = refs[:n], refs[n:2 * n]
        send_sems, recv_sems = refs[2 * n:]
        mx, my, mc = lax.axis_index("x"), lax.axis_index("y"), lax.axis_index("c")
        chips = [(1 - mx, my), (mx, 1 - my), (1 - mx, 1 - my)]
        copies = [pltpu.make_async_remote_copy(
            src_ref=p_refs[t].at[2 * cx + cy], dst_ref=o_refs[t].at[j], send_sem=send_sems.at[j * n + t],
            recv_sem=recv_sems.at[j * n + t], device_id=(cx, cy, mc), device_id_type=MESH)
            for t in range(n) for j, (cx, cy) in enumerate(chips)]
        for cp in copies:
            cp.start()
        for cp in copies:
            cp.wait_recv()
        for cp in copies:
            cp.wait_send()

    return pl.pallas_call(
        body, out_shape=tuple(S((3,) + p.shape[1:], p.dtype) for p in ps), in_specs=[ANY] * n,
        out_specs=tuple([ANY] * n),
        scratch_shapes=[pltpu.SemaphoreType.DMA((3 * n,)), pltpu.SemaphoreType.DMA((3 * n,))], name=name)(*ps)


FLIPS = [(fx, fy, fc) for fx in (0, 1) for fy in (0, 1) for fc in (0, 1)][1:]
HBM = pl.BlockSpec(memory_space=pltpu.HBM)
SEM = pl.BlockSpec(memory_space=pltpu.SEMAPHORE)
EFFECT = pltpu.SideEffectType.DATAFLOW_SIDE_EFFECTING


def _hbm(a):
    return pltpu.with_memory_space_constraint(a, pltpu.HBM)


def _flips(gather):
    return [(0, 0, 0)] + FLIPS if gather else FLIPS


def _split_copies(gather, s_refs, l_refs, send_sems, recv_sems):
    n = len(s_refs)
    mx, my, mc = lax.axis_index("x"), lax.axis_index("y"), lax.axis_index("c")
    me = 4 * mx + 2 * my + mc
    copies = []
    for k, (fx, fy, fc) in enumerate(_flips(gather)):
        px, py, pc = (1 - mx if fx else mx), (1 - my if fy else my), (1 - mc if fc else mc)
        for t in range(n):
            if gather:
                src = s_refs[t]
                r = src.shape[0]
                dst = l_refs[t].at[pl.ds(pl.multiple_of(me * r, 16), r), :]
            else:
                src = s_refs[t].at[:, 4 * px + 2 * py + pc]
                dst = l_refs[t].at[k]
            copies.append(pltpu.make_async_remote_copy(
                src_ref=src, dst_ref=dst, send_sem=send_sems.at[k * n + t], recv_sem=recv_sems.at[k * n + t],
                device_id=(px, py, pc), device_id_type=MESH))
    return copies


def _split_start(gather, srcs, lands, after, name):
    n = len(srcs)
    n_sem = len(_flips(gather)) * n

    def body(*refs):
        s_refs, l_refs = refs[:n], refs[n:2 * n]
        send_sems, recv_sems = refs[2 * n + 1], refs[2 * n + 2]
        token = refs[-1]
        for cp in _split_copies(gather, s_refs, l_refs, send_sems, recv_sems):
            cp.start()
        token[...] = jnp.zeros_like(token)

    outs = pl.pallas_call(
        body, name=name,
        out_shape=(pltpu.SemaphoreType.DMA((n_sem,)), pltpu.SemaphoreType.DMA((n_sem,)))
        + tuple(pltpu.HBM(a.shape, a.dtype) for a in srcs) + tuple(pltpu.HBM(a.shape, a.dtype) for a in lands)
        + (S((8, LANES), F32),),
        in_specs=[HBM] * (2 * n) + [ANY],
        out_specs=(SEM, SEM) + (HBM,) * (2 * n) + (pl.BlockSpec(memory_space=pltpu.VMEM),),
        input_output_aliases={i: 2 + i for i in range(2 * n)},
        compiler_params=pltpu.CompilerParams(has_side_effects=EFFECT),
    )(*[_hbm(a) for a in srcs], *[_hbm(a) for a in lands], after)
    return outs[0], outs[1], list(outs[2:2 + n]), list(outs[2 + n:2 + 2 * n]), outs[-1]


def _split_wait(gather, send_sems, recv_sems, srcs, lands, after, name):
    n = len(srcs)

    def body(*refs):
        s_refs, l_refs = refs[:n], refs[n:2 * n]
        ssem, rsem = refs[2 * n], refs[2 * n + 1]
        copies = _split_copies(gather, s_refs, l_refs, ssem, rsem)
        for cp in copies:
            cp.wait_send()
        for cp in copies:
            cp.wait_recv()

    outs = pl.pallas_call(
        body, name=name,
        out_shape=tuple(pltpu.HBM(a.shape, a.dtype) for a in srcs) + tuple(pltpu.HBM(a.shape, a.dtype) for a in lands),
        in_specs=[HBM] * (2 * n) + [SEM, SEM, ANY], out_specs=(HBM,) * (2 * n),
        input_output_aliases={i: i for i in range(2 * n)},
        compiler_params=pltpu.CompilerParams(has_side_effects=EFFECT),
    )(*srcs, *lands, send_sems, recv_sems, after)
    return list(outs[n:])


def _rows_tile(b):
    return _tile(b, (512, 256, 128)) if b > 512 else b


def _pair_sum(g, got, name):
    A, _, B, C = g.shape
    tb = _rows_tile(B)
    core = lax.axis_index("c").astype(jnp.int32).reshape(1)

    def body(c_ref, g_ref, r_ref, o_ref):
        o_ref[...] = (g_ref[...].astype(F32) + r_ref[...].astype(F32)).astype(o_ref.dtype)

    return pl.pallas_call(
        body,
        grid_spec=pltpu.PrefetchScalarGridSpec(
            num_scalar_prefetch=1, grid=(4, A, B // tb),
            in_specs=[pl.BlockSpec((1, 1, tb, C), lambda j, a, i, c_ref: (a, 2 * j + c_ref[0], i, 0)),
                      pl.BlockSpec((1, 1, tb, C), lambda j, a, i, c_ref: (j, a, i, 0))],
            out_specs=pl.BlockSpec((1, 1, tb, C), lambda j, a, i, c_ref: (j, a, i, 0))),
        out_shape=S((4, A, B, C), g.dtype), compiler_params=_cparams("parallel", "parallel", "parallel"),
        name=name)(core, g, got)


def _adamw_math(w, g, m, v):
    m = ADAM_B1 * m + (1.0 - ADAM_B1) * g
    v = ADAM_B2 * v + (1.0 - ADAM_B2) * (g * g)
    m_hat = m / (1.0 - ADAM_B1 ** ADAM_STEP)
    v_hat = v / (1.0 - ADAM_B2 ** ADAM_STEP)
    delta = -ADAM_LR * (m_hat / (jnp.sqrt(v_hat) + ADAM_EPS) + ADAM_WD * w)
    return delta, m, v


def _adamw_sharded(p, got, w, m, v, name):
    A, B, C = w.shape
    tb = _rows_tile(B)
    chip = (2 * lax.axis_index("x") + lax.axis_index("y")).astype(jnp.int32).reshape(1)

    def body(c_ref, p_ref, got_ref, w_ref, m_ref, v_ref, g_out, d_out, m_out, v_out):
        g = p_ref[0].astype(F32)
        for j in range(3):
            g = g + got_ref[j].astype(F32)
        d, mn, vn = _adamw_math(w_ref[...], g, m_ref[...], v_ref[...])
        g_out[...] = g
        d_out[...] = d
        m_out[...] = mn
        v_out[...] = vn

    blk = pl.BlockSpec((1, tb, C), lambda a, i, c_ref: (a, i, 0))
    return pl.pallas_call(
        body,
        grid_spec=pltpu.PrefetchScalarGridSpec(
            num_scalar_prefetch=1, grid=(A, B // tb),
            in_specs=[pl.BlockSpec((1, 1, tb, C), lambda a, i, c_ref: (c_ref[0], a, i, 0)),
                      pl.BlockSpec((3, 1, tb, C), lambda a, i, c_ref: (0, a, i, 0)), blk, blk, blk],
            out_specs=(blk, blk, blk, blk)),
        out_shape=(S((A, B, C), F32),) * 4, compiler_params=_cparams("parallel", "parallel"), name=name)(
            chip, p, got, w, m, v)


def _adamw_layers(owns, gots, w, m, v, name):
    L, B, C = w.shape
    per_row = 2 * L * len(FLIPS) * C * owns[0].dtype.itemsize
    tb = max(t for t in range(16, B + 1, 16) if B % t == 0 and (t * per_row <= 16 * 1024 * 1024 or t == 16))
    me = (4 * lax.axis_index("x") + 2 * lax.axis_index("y") + lax.axis_index("c")).astype(jnp.int32).reshape(1)

    def body(me_ref, *refs):
        own_refs, got_refs = refs[:L], refs[L:2 * L]
        w_ref, m_ref, v_ref = refs[2 * L:2 * L + 3]
        g_out, d_out, m_out, v_out = refs[2 * L + 3:]
        layer = pl.program_id(0)
        for kk in range(L):
            @pl.when(layer == kk)
            def _():
                g = own_refs[kk][0].astype(F32)
                for s in range(len(FLIPS)):
                    g = g + got_refs[kk][s].astype(F32)
                d, mn, vn = _adamw_math(w_ref[...], g, m_ref[...], v_ref[...])
                g_out[...] = g
                d_out[...] = d
                m_out[...] = mn
                v_out[...] = vn

    def row(kk, layer, i):
        return jnp.where(layer == kk, i, 0)

    blk = pl.BlockSpec((1, tb, C), lambda layer, i, me_ref: (layer, i, 0))
    own_specs = [pl.BlockSpec((1, 1, tb, C), lambda layer, i, me_ref, kk=kk: (0, me_ref[0], row(kk, layer, i), 0))
                 for kk in range(L)]
    got_specs = [pl.BlockSpec((len(FLIPS), 1, tb, C), lambda layer, i, me_ref, kk=kk: (0, 0, row(kk, layer, i), 0))
                 for kk in range(L)]
    return pl.pallas_call(
        body,
        grid_spec=pltpu.PrefetchScalarGridSpec(
            num_scalar_prefetch=1, grid=(L, B // tb), in_specs=own_specs + got_specs + [blk, blk, blk],
            out_specs=(blk, blk, blk, blk)),
        out_shape=(S((L, B, C), F32),) * 4, compiler_params=_cparams("arbitrary", "arbitrary"), name=name)(
            me, *owns, *gots, w, m, v)


def _adamw_replicated(parts, w, m, v, name):
    R, C = w.shape
    rb = _tile(R, (512, 256, 128, 64, 32, 16, 8))

    def body(p_ref, w_ref, m_ref, v_ref, g_out, d_out, m_out, v_out):
        g = p_ref[0]
        for j in range(1, N_DEV):
            g = g + p_ref[j]
        d, mn, vn = _adamw_math(w_ref[...], g, m_ref[...], v_ref[...])
        g_out[...] = g
        d_out[...] = d
        m_out[...] = mn
        v_out[...] = vn

    blk = pl.BlockSpec((rb, C), lambda i: (i, 0))
    return pl.pallas_call(
        body, grid=(R // rb,), in_specs=[pl.BlockSpec((N_DEV, rb, C), lambda i: (0, i, 0)), blk, blk, blk],
        out_specs=(blk, blk, blk, blk), out_shape=(S((R, C), F32),) * 4, compiler_params=_cparams("parallel"),
        name=name)(parts, w, m, v)


def _pack(arrs, rows_mult, dtype):
    flat = jnp.concatenate([a.reshape(-1).astype(dtype) for a in arrs])
    n = flat.shape[0]
    per = rows_mult * LANES
    tot = -(-n // per) * per
    return jnp.pad(flat, (0, tot - n)).reshape(tot // LANES, LANES)


def _unpack(blob, shapes):
    flat = blob.reshape(-1)
    out, off = [], 0
    for shp in shapes:
        n = int(np.prod(shp))
        out.append(flat[off:off + n].reshape(shp))
        off += n
    return out


def _small_to_natural(g8):
    t = jnp.moveaxis(g8, 0, -2)
    return t.reshape(t.shape[:-2] + (N_DEV * t.shape[-1],))


def _small_to_cols(g):
    t = g.reshape(g.shape[:-1] + (N_DEV, g.shape[-1] // N_DEV))
    return jnp.moveaxis(t, -2, 0)


def _block_diag(w):
    nb, bs, _ = w.shape
    eye = jnp.eye(nb, dtype=w.dtype)
    return (eye[:, None, :, None] * w[:, :, None, :]).reshape(nb * bs, nb * bs)


def _diag_blocks(d, nb, bs):
    d4 = d.reshape(nb, bs, nb, bs)
    return jnp.stack([d4[i, :, i, :] for i in range(nb)])


def kernel(x, mem, g_mix_pre, g_mix_post, g_ffn_pre, g_ffn_post, g_mem, w_mem_kv, w_mix_out, w_ffn_up, w_ffn_conv, b_ffn_conv, w_ffn_down, w_in_a, w_conv_a, b_conv_a, w_rg_r, b_rg_r, w_rg_i, b_rg_i, lru_lambda, w_in_b, sinks_b, g_kv, w_kv, loss_target, m_g_mix_pre, m_g_mix_post, m_g_ffn_pre, m_g_ffn_post, m_g_mem, m_w_mem_kv, m_w_mix_out, m_w_ffn_up, m_w_ffn_conv, m_b_ffn_conv, m_w_ffn_down, m_w_in_a, m_w_conv_a, m_b_conv_a, m_w_rg_r, m_b_rg_r, m_w_rg_i, m_b_rg_i, m_lru_lambda, m_w_in_b, m_sinks_b, m_g_kv, m_w_kv, v_g_mix_pre, v_g_mix_post, v_g_ffn_pre, v_g_ffn_post, v_g_mem, v_w_mem_kv, v_w_mix_out, v_w_ffn_up, v_w_ffn_conv, v_b_ffn_conv, v_w_ffn_down, v_w_in_a, v_w_conv_a, v_b_conv_a, v_w_rg_r, v_b_rg_r, v_w_rg_i, v_b_rg_i, v_lru_lambda, v_w_in_b, v_sinks_b, v_g_kv, v_w_kv):
    w_loc = dict(g_mix_pre=g_mix_pre, g_mix_post=g_mix_post, g_ffn_pre=g_ffn_pre, g_ffn_post=g_ffn_post, g_mem=g_mem,
                 w_mem_kv=w_mem_kv, w_mix_out=w_mix_out, w_ffn_up=w_ffn_up, w_ffn_conv=w_ffn_conv,
                 b_ffn_conv=b_ffn_conv, w_ffn_down=w_ffn_down, w_in_a=w_in_a, w_conv_a=w_conv_a, b_conv_a=b_conv_a,
                 w_rg_r=w_rg_r, b_rg_r=b_rg_r, w_rg_i=w_rg_i, b_rg_i=b_rg_i, lru_lambda=lru_lambda, w_in_b=w_in_b,
                 sinks_b=sinks_b, g_kv=g_kv, w_kv=w_kv)
    m_loc = dict(g_mix_pre=m_g_mix_pre, g_mix_post=m_g_mix_post, g_ffn_pre=m_g_ffn_pre, g_ffn_post=m_g_ffn_post,
                 g_mem=m_g_mem, w_mem_kv=m_w_mem_kv, w_mix_out=m_w_mix_out, w_ffn_up=m_w_ffn_up,
                 w_ffn_conv=m_w_ffn_conv, b_ffn_conv=m_b_ffn_conv, w_ffn_down=m_w_ffn_down, w_in_a=m_w_in_a,
                 w_conv_a=m_w_conv_a, b_conv_a=m_b_conv_a, w_rg_r=m_w_rg_r, b_rg_r=m_b_rg_r, w_rg_i=m_w_rg_i,
                 b_rg_i=m_b_rg_i, lru_lambda=m_lru_lambda, w_in_b=m_w_in_b, sinks_b=m_sinks_b, g_kv=m_g_kv,
                 w_kv=m_w_kv)
    v_loc = dict(g_mix_pre=v_g_mix_pre, g_mix_post=v_g_mix_post, g_ffn_pre=v_g_ffn_pre, g_ffn_post=v_g_ffn_post,
                 g_mem=v_g_mem, w_mem_kv=v_w_mem_kv, w_mix_out=v_w_mix_out, w_ffn_up=v_w_ffn_up,
                 w_ffn_conv=v_w_ffn_conv, b_ffn_conv=v_b_ffn_conv, w_ffn_down=v_w_ffn_down, w_in_a=v_w_in_a,
                 w_conv_a=v_w_conv_a, b_conv_a=v_b_conv_a, w_rg_r=v_w_rg_r, b_rg_r=v_b_rg_r, w_rg_i=v_w_rg_i,
                 b_rg_i=v_b_rg_i, lru_lambda=v_lru_lambda, w_in_b=v_w_in_b, sinks_b=v_sinks_b, g_kv=v_g_kv,
                 w_kv=v_w_kv)

    Bl, T, D = x.shape
    Ml = mem.shape[1]
    N = Bl * T
    depth = g_mix_pre.shape[0]
    n_a = w_in_a.shape[0]
    F = w_ffn_down.shape[1] * N_DEV
    def as_rows(n, a):
        return jnp.swapaxes(a, -1, -2) if n in TRANSPOSED else a

    def layer_keys(l):
        keys = [("w_mem_kv", l), ("w_mix_out", l), ("w_ffn_up", l), ("w_ffn_down", l)]
        keys.append(("w_in_a", l) if l < n_a else ("w_in_b", l - n_a))
        if l == n_a:
            keys.append(("w_kv", None))
        return keys

    def shard_of(key):
        n, i = key
        return as_rows(n, w_loc[n] if i is None else w_loc[n][i]).astype(BF16)

    W = {}
    ffn_names = ("w_ffn_up", "w_ffn_down")
    keys0 = [kk for kk in layer_keys(0) if kk[0] not in ffn_names]
    got0 = _ag_weights([shard_of(kk) for kk in keys0] + [w_loc[n] for n in SMALL_SHARDED],
                       [True] * len(keys0) + [False] * len(SMALL_SHARDED), name="ag_weights_0")
    W.update(zip(keys0, got0))
    for n, a in zip(SMALL_SHARDED, got0[len(keys0):]):
        W[n] = _small_to_natural(a)

    def gather_start(keys, after, tag):
        shards = [shard_of(kk) for kk in keys]
        lands = [lax.empty((N_DEV * s.shape[0],) + s.shape[1:], s.dtype) for s in shards]
        return (keys, tag) + _split_start(True, shards, lands, after, name=f"ag_start_{tag}")

    def gather_wait(pending, after):
        keys, tag, ssem, rsem, srcs, lands, _ = pending
        W.update(zip(keys, _split_wait(True, ssem, rsem, srcs, lands, after, name=f"ag_wait_{tag}")))

    pending_ffn0 = gather_start([kk for kk in layer_keys(0) if kk[0] in ffn_names], got0[0], "ffn_0")

    nblk, bsz = w_rg_r.shape[1], w_rg_r.shape[2]
    wbd = [jnp.concatenate([_block_diag(w_rg_r[j]), _block_diag(w_rg_i[j])], axis=1).astype(BF16) for j in range(n_a)]

    def vec(a):
        return a.reshape(1, -1)

    x2 = x.reshape(N, D)
    mem2 = mem.reshape(Bl * Ml, D)
    saved = []
    kvn = kv3 = x_kv = None
    xs = x2
    for l in range(depth):
        sv = {"x0": xs}
        g_pre = vec(g_mix_pre[l])
        if l + 1 < depth:
            pending = gather_start(layer_keys(l + 1), pending_ffn0[-1] if l == 0 else W[("w_mem_kv", l)], l + 1)
            g_pre = g_pre + pending[-1][0, 0]
        h1 = _rms_fwd(xs, g_pre, BF16, name=f"rms_mixpre_{l}")
        memn = _rms_fwd(mem2, vec(g_mem[l]), BF16, name=f"rms_mem_{l}")
        mkv3 = _mm(memn, W[("w_mem_kv", l)], name=f"mm_memkv_{l}").reshape(Bl, Ml, 2 * MEM_W)
        if l < n_a:
            j = l
            proj = _mm(h1, W[("w_in_a", j)], tb=True, name=f"mm_in_{l}")
            proj3 = proj.reshape(Bl, T, -1)
            xc3 = _conv_fwd_call(proj3, MIX_W, MIX_W, W["w_conv_a"][j], vec(W["b_conv_a"][j]), name=f"conv_a_{l}")
            gates3 = _mm(xc3.reshape(N, MIX_W), wbd[j], name=f"mm_gates_{l}").reshape(Bl, T, 2 * MIX_W)
            y_main3, hs3 = _rglru_fwd(xc3, gates3, proj3, vec(b_rg_r[j]), vec(b_rg_i[j]), vec(W["lru_lambda"][j]),
                                      name=f"rglru_fwd_{l}")
            q_off = 2 * MIX_W
            sv.update(xc3=xc3, gates3=gates3, hs3=hs3)
        else:
            j = l - n_a
            if l == n_a:
                x_kv = xs
                kvn = _rms_fwd(xs, vec(g_kv), BF16, name="rms_kv")
                kv3 = _mm(kvn, W[("w_kv", None)], name="mm_kv").reshape(Bl, T, 2 * MEM_W)
            proj = _mm(h1, W[("w_in_b", j)], name=f"mm_in_{l}")
            proj3 = proj.reshape(Bl, T, -1)
            y_main3 = _swa_fwd(proj3, kv3, sinks_b[j], name=f"swa_fwd_{l}")
            q_off = MIX_W
        y_mem3 = _mem_attn_fwd(proj3, q_off, mkv3, name=f"memattn_fwd_{l}")
        y_main = y_main3.reshape(N, MIX_W)
        y_mem = y_mem3.reshape(N, MEM_W)
        y = _mm(y_main, W[("w_mix_out", l)], n=D, k=MIX_W, name=f"mm_mixout_main_{l}")
        y = _mm(y_mem, W[("w_mix_out", l)], n=D, k=MEM_W, b_off=(MIX_W, 0), add=y, name=f"mm_mixout_mem_{l}")
        x1 = _rms_fwd(y, vec(g_mix_post[l]), F32, res=xs, name=f"rms_mixpost_{l}")
        h2 = _rms_fwd(x1, vec(g_ffn_pre[l]), BF16, name=f"rms_ffnpre_{l}")
        if l == 0:
            gather_wait(pending_ffn0, h2)
        ug = _mm(h2, W[("w_ffn_up", l)], tb=True, n=F, name=f"mm_up_g_{l}")
        uv = _mm(h2, W[("w_ffn_up", l)], tb=True, n=F, b_off=(F, 0), name=f"mm_up_v_{l}")
        ug3, uv3 = ug.reshape(Bl, T, F), uv.reshape(Bl, T, F)
        act3, g3, v3 = _ffn_mid_fwd(ug3, uv3, W["w_ffn_conv"][l], vec(b_ffn_conv[l]), name=f"ffn_mid_fwd_{l}")
        act = act3.reshape(N, F)
        f = _mm(act, W[("w_ffn_down", l)], name=f"mm_down_{l}")
        x_next = _rms_fwd(f, vec(g_ffn_post[l]), F32, res=x1, name=f"rms_ffnpost_{l}")
        if l + 1 < depth:
            gather_wait(pending, x_next)
        sv.update(h1=h1, memn=memn, mkv3=mkv3, proj3=proj3, q_off=q_off, y_main=y_main, y_mem=y_mem, y=y, x1=x1,
                  h2=h2, ug3=ug3, uv3=uv3, g3=g3, v3=v3, act=act, f=f)
        saved.append(sv)
        xs = x_next

    dxs, loss_vec = _loss_bwd(xs, loss_target.reshape(N, D))
    loss = lax.psum(jnp.sum(loss_vec), ("x", "y", "c"))

    G = {n: [None] * w_loc[n].shape[0] for n in REPL + SMALL_SHARDED if n != "g_kv"}
    GW = {}

    def dw(key, off, a, b_, nm):
        GW[key] = _mm(a, b_, ta=True, out_dtype=BF16, into=(GW.get(key), (1,) + W[key].shape, 0, off), name=nm)

    def grad_blocks(key):
        g = GW[key]
        return g.reshape(1, N_DEV, g.shape[1] // N_DEV, g.shape[2])

    reduces = []

    def reduce_start(keys, after, tag):
        srcs = [grad_blocks(kk) for kk in keys]
        lands = [lax.empty((len(FLIPS),) + s.shape[:1] + s.shape[2:], s.dtype) for s in srcs]
        started = _split_start(False, srcs, lands, after, name=f"rs_start_{tag}")
        reduces.append((keys, tag) + started)
        return started[-1]

    kv_parts = []
    for l in reversed(range(depth)):
        sv = saved[l]
        proj3 = sv["proj3"]
        df, dg = _rms_bwd(sv["f"], vec(g_ffn_post[l]), dxs, out_dtype=BF16, name=f"rmsb_ffnpost_{l}")
        G["g_ffn_post"][l] = dg[0]
        dact = _mm(df, W[("w_ffn_down", l)], tb=True, name=f"mmb_down_dx_{l}")
        dw(("w_ffn_down", l), (0, 0), sv["act"], df, f"mmb_down_dw_{l}")
        dug3, duv3, dwg, dwv, dbg, dbv = _ffn_mid_bwd(sv["ug3"], sv["uv3"], sv["g3"], sv["v3"], dact.reshape(Bl, T, F),
                                                      W["w_ffn_conv"][l], name=f"ffn_mid_bwd_{l}")
        G["w_ffn_conv"][l] = jnp.concatenate([dwg, dwv], axis=1)
        G["b_ffn_conv"][l] = jnp.concatenate([dbg, dbv], axis=1)[0]
        dug, duv = dug3.reshape(N, F), duv3.reshape(N, F)
        dw(("w_ffn_up", l), (0, 0), dug, sv["h2"], f"mmb_up_dw_g_{l}")
        dw(("w_ffn_up", l), (F, 0), duv, sv["h2"], f"mmb_up_dw_v_{l}")
        tok = reduce_start([("w_ffn_down", l), ("w_ffn_up", l)], dug, f"ffn_{l}")
        dh2 = _mm(dug, W[("w_ffn_up", l)], n=D, k=F, after=tok, name=f"mmb_up_dx_g_{l}")
        dh2 = _mm(duv, W[("w_ffn_up", l)], n=D, k=F, b_off=(F, 0), add=dh2, name=f"mmb_up_dx_v_{l}")
        dx1, dg = _rms_bwd(sv["x1"], vec(g_ffn_pre[l]), dh2, add=dxs, name=f"rmsb_ffnpre_{l}")
        G["g_ffn_pre"][l] = dg[0]
        dy, dg = _rms_bwd(sv["y"], vec(g_mix_post[l]), dx1, out_dtype=BF16, name=f"rmsb_mixpost_{l}")
        G["g_mix_post"][l] = dg[0]
        dy_main = _mm(dy, W[("w_mix_out", l)], tb=True, n=MIX_W, k=D, name=f"mmb_mixout_dmain_{l}")
        dy_mem = _mm(dy, W[("w_mix_out", l)], tb=True, n=MEM_W, k=D, b_off=(MIX_W, 0),
                     name=f"mmb_mixout_dmem_{l}")
        dw(("w_mix_out", l), (0, 0), sv["y_main"], dy, f"mmb_mixout_dw_main_{l}")
        dw(("w_mix_out", l), (MIX_W, 0), sv["y_mem"], dy, f"mmb_mixout_dw_mem_{l}")
        dq_mem3, dmkv3 = _mem_attn_bwd(proj3, sv["q_off"], sv["mkv3"], dy_mem.reshape(Bl, T, MEM_W),
                                       name=f"memattn_bwd_{l}")
        dq_mem = dq_mem3.reshape(N, MEM_W)
        dmkv = dmkv3.reshape(Bl * Ml, 2 * MEM_W)
        dw(("w_mem_kv", l), (0, 0), sv["memn"], dmkv, f"mmb_memkv_dw_{l}")
        dmemn = _mm(dmkv, W[("w_mem_kv", l)], tb=True, name=f"mmb_memkv_dx_{l}")
        _, dg = _rms_bwd(mem2, vec(g_mem[l]), dmemn, name=f"rmsb_mem_{l}")
        G["g_mem"][l] = dg[0]
        dy_main3 = dy_main.reshape(Bl, T, MIX_W)
        if l < n_a:
            j = l
            dxc3, drp3, dip3, dugate3, dbr, dbi, dlam = _rglru_bwd(
                dy_main3, sv["xc3"], sv["gates3"], proj3, sv["hs3"], vec(b_rg_r[j]), vec(b_rg_i[j]),
                vec(W["lru_lambda"][j]), name=f"rglru_bwd_{l}")
            G["b_rg_r"][j] = dbr.reshape(nblk, bsz)
            G["b_rg_i"][j] = dbi.reshape(nblk, bsz)
            G["lru_lambda"][j] = dlam[0]
            drp, dip = drp3.reshape(N, MIX_W), dip3.reshape(N, MIX_W)
            xc2 = sv["xc3"].reshape(N, MIX_W)
            G["w_rg_r"][j] = _diag_blocks(_mm(xc2, drp, ta=True, name=f"mmb_gates_dw_r_{l}"), nblk, bsz)
            G["w_rg_i"][j] = _diag_blocks(_mm(xc2, dip, ta=True, name=f"mmb_gates_dw_i_{l}"), nblk, bsz)
            dxc = _mm(drp, wbd[j], tb=True, n=MIX_W, k=MIX_W, add=dxc3.reshape(N, MIX_W), name=f"mmb_gates_dx_r_{l}")
            dxc = _mm(dip, wbd[j], tb=True, n=MIX_W, k=MIX_W, b_off=(0, MIX_W), add=dxc, name=f"mmb_gates_dx_i_{l}")
            dux3, dwc, dbc = _conv_bwd_call(dxc.reshape(Bl, T, MIX_W), proj3, MIX_W, MIX_W, W["w_conv_a"][j],
                                            name=f"conv_a_bwd_{l}")
            G["w_conv_a"][j] = dwc
            G["b_conv_a"][j] = dbc[0]
            pieces = [(dugate3.reshape(N, MIX_W), 0), (dux3.reshape(N, MIX_W), MIX_W), (dq_mem, 2 * MIX_W)]
            in_key = ("w_in_a", j)
        else:
            j = l - n_a
            dq3, dkc, dkp, dsk = _swa_bwd(proj3, kv3, sinks_b[j], dy_main3, name=f"swa_bwd_{l}")
            kv_parts.append((dkc, dkp))
            G["sinks_b"][j] = dsk[0, :SWA_HEADS]
            pieces = [(dq3.reshape(N, MIX_W), 0), (dq_mem, MIX_W)]
            in_key = ("w_in_b", j)
        in_t = in_key[0] in TRANSPOSED
        for pi, (piece, off) in enumerate(pieces):
            if in_t:
                dw(in_key, (off, 0), piece, sv["h1"], f"mmb_in_dw_{pi}_{l}")
            else:
                dw(in_key, (0, off), sv["h1"], piece, f"mmb_in_dw_{pi}_{l}")
        tok = reduce_start([("w_mix_out", l), ("w_mem_kv", l), in_key], dy, f"mix_{l}")
        dh1 = None
        for pi, (piece, off) in enumerate(pieces):
            dh1 = _mm(piece, W[in_key], tb=not in_t, n=D, k=piece.shape[1], b_off=(off, 0) if in_t else (0, off),
                      add=dh1, after=tok if pi == 0 else None, name=f"mmb_in_dx_{pi}_{l}")
        dxs, dg = _rms_bwd(sv["x0"], vec(g_mix_pre[l]), dh1, add=dx1, name=f"rmsb_mixpre_{l}")
        G["g_mix_pre"][l] = dg[0]
        if l == n_a:
            dkv = _kv_grad_combine(kv_parts, name="kv_grad_combine").reshape(N, 2 * MEM_W)
            dw(("w_kv", None), (0, 0), kvn, dkv, "mmb_kv_dw")
            tok = reduce_start([("w_kv", None)], dkv, "kv")
            dkvn = _mm(dkv, W[("w_kv", None)], tb=True, after=tok, name="mmb_kv_dx")
            dxs, dg = _rms_bwd(x_kv, vec(g_kv), dkvn, add=dxs, name="rmsb_kv")
            G["g_kv"] = dg[0]
    grad_x = dxs.reshape(Bl, T, D)
    Gf = {n: (jnp.stack(g) if isinstance(g, list) else g) for n, g in G.items()}

    parts = {}
    for keys, tag, ssem, rsem, srcs, lands, _ in reduces:
        for kk, s, g7 in zip(keys, srcs, _split_wait(False, ssem, rsem, srcs, lands, dxs, name=f"rs_wait_{tag}")):
            parts[kk] = (s, g7)
    g4 = []
    for n in SMALL_SHARDED:
        t = _small_to_cols(Gf[n]).astype(BF16)
        g4.append(t.reshape(1, N_DEV, -1, t.shape[-1]))
    got = _rs_sibling(g4, name="rs_sibling")
    psum4 = [_pair_sum(g, r, name=f"rs_pair_sum_{n}") for n, g, r in zip(SMALL_SHARDED, g4, got)]
    got2 = _rs_chips(psum4, name="rs_chips")
    r_blob = _pack([Gf[n].astype(F32) for n in REPL], REPL_ROWS, F32)
    r_parts = _all_gather(r_blob, name="ag_repl_grads")

    res = [{} for _ in range(4)]
    for n, p4, g2 in zip(SMALL_SHARDED, psum4, got2):
        shp3 = p4.shape[1:]
        outs = _adamw_sharded(p4, g2, w_loc[n].reshape(shp3), m_loc[n].reshape(shp3), v_loc[n].reshape(shp3),
                              name=f"adamw_{n}")
        for k in range(4):
            res[k][n] = outs[k].reshape(w_loc[n].shape)
    for n, _ in SHARDED:
        if n in SMALL_SHARDED:
            continue
        idx = [None] if w_loc[n].ndim == 2 else list(range(w_loc[n].shape[0]))
        wmv = [as_rows(n, a[n]) for a in (w_loc, m_loc, v_loc)]
        shp3 = (len(idx),) + wmv[0].shape[-2:]
        outs = _adamw_layers([parts[(n, i)][0] for i in idx], [parts[(n, i)][1] for i in idx],
                             *[a.reshape(shp3) for a in wmv], name=f"adamw_{n}")
        for k in range(4):
            res[k][n] = as_rows(n, outs[k].reshape(wmv[0].shape))
    outs_rp = _adamw_replicated(r_parts, _pack([w_loc[n] for n in REPL], REPL_ROWS, F32),
                                _pack([m_loc[n] for n in REPL], REPL_ROWS, F32),
                                _pack([v_loc[n] for n in REPL], REPL_ROWS, F32),
                                name="adamw_replicated")
    rp_shapes = [w_loc[n].shape for n in REPL]
    for k in range(4):
        res[k].update(zip(REPL, _unpack(outs_rp[k], rp_shapes)))
    out = [loss, grad_x]
    for k in range(4):
        out += [res[k][n] for n in WEIGHTS]
    return tuple(out)
```

```python
import functools
import math

import numpy as np
import jax
import jax.numpy as jnp
from jax import lax
from jax.experimental import pallas as pl
from jax.experimental.pallas import tpu as pltpu

F32 = jnp.float32
BF16 = jnp.bfloat16
S = jax.ShapeDtypeStruct
MESH = pl.DeviceIdType.MESH
ANY = pl.BlockSpec(memory_space=pl.ANY)

HEAD = 64
MEM_HEADS = 4
MEM_W = MEM_HEADS * HEAD
SWA_HEADS = 12
SWA_GROUP = 3
MIX_W = SWA_HEADS * HEAD
WIN = 128
LRU_C = 8.0
EPS = 1e-6
ADAM_LR, ADAM_B1, ADAM_B2, ADAM_EPS, ADAM_WD, ADAM_STEP = 0.001, 0.9, 0.999, 1e-08, 0.01, 10
GELU_C0 = math.sqrt(2.0 / math.pi)
GELU_C1 = 0.044715
N_DEV = 8
LANES = 128
CT = 128
VMEM_LIMIT = 48 * 1024 * 1024
MM_VMEM_BUDGET = 36 * 1024 * 1024
REPL_ROWS = 256

SHARDED = (("w_mem_kv", 1), ("w_mix_out", 1), ("w_ffn_up", 2), ("w_ffn_conv", 2), ("w_ffn_down", 1), ("w_in_a", 2),
           ("w_conv_a", 2), ("b_conv_a", 1), ("lru_lambda", 1), ("w_in_b", 1), ("w_kv", 0))
SMALL_SHARDED = ("w_ffn_conv", "w_conv_a", "b_conv_a", "lru_lambda")
TRANSPOSED = ("w_ffn_up", "w_in_a")
REPL = ("g_mix_pre", "g_mix_post", "g_ffn_pre", "g_ffn_post", "g_mem", "b_ffn_conv", "w_rg_r", "b_rg_r", "w_rg_i",
        "b_rg_i", "sinks_b", "g_kv")
WEIGHTS = ("g_mix_pre", "g_mix_post", "g_ffn_pre", "g_ffn_post", "g_mem", "w_mem_kv", "w_mix_out", "w_ffn_up",
           "w_ffn_conv", "b_ffn_conv", "w_ffn_down", "w_in_a", "w_conv_a", "b_conv_a", "w_rg_r", "b_rg_r", "w_rg_i",
           "b_rg_i", "lru_lambda", "w_in_b", "sinks_b", "g_kv", "w_kv")


def _alibi_slopes(n):
    def pow2(m):
        start = 2.0 ** (-8.0 / m)
        return [start ** (i + 1) for i in range(m)]
    c = 2 ** int(math.floor(math.log2(n)))
    s = pow2(c)
    if c != n:
        s = s + pow2(2 * c)[0::2][: n - c]
    return [float(v) for v in np.asarray(s, dtype=np.float32)]


SLOPES = _alibi_slopes(SWA_HEADS)


def _tile(n, cands):
    for c in cands:
        if n % c == 0:
            return c
    return n


def _cparams(*sem):
    return pltpu.CompilerParams(dimension_semantics=sem, vmem_limit_bytes=VMEM_LIMIT)


def _mm_tiles(M, N, K, a_bytes, b_bytes, o_bytes, add_bytes, offsets):
    m_off, n_offs, k_off = offsets
    tms = [c for c in (1024, 512, 256, 128) if M % c == 0 and m_off % c == 0] or [M]
    tns = [c for c in (1408, 1024, 896, 768, 512, 384, 256, 128)
           if N % c == 0 and all(o % c == 0 for o in n_offs)] or [N]
    tks = [c for c in (K, 2048, 1408, 1024, 512, 256, 128) if c <= K and K % c == 0 and k_off % c == 0]
    best = None
    for tk in tks:
        fits = []
        for tm in tms:
            for tn in tns:
                need = 2 * (tm * tk * a_bytes + tk * tn * b_bytes + tm * tn * (o_bytes + add_bytes))
                need += tm * tn * 4 * (2 if tk < K else 1)
                need += (tm * tk * 2 if a_bytes != 2 else 0) + (tk * tn * 2 if b_bytes != 2 else 0)
                if need <= MM_VMEM_BUDGET:
                    fits.append((tm * tn, min(tm, 512), tm, tn))
        if fits:
            _, _, tm, tn = max(fits)
            best = (tm, tn, tk)
            break
    assert best is not None, (M, N, K)
    return best


def _mm(a, b, *, ta=False, tb=False, n=None, k=None, b_off=(0, 0), out_dtype=F32, add=None, into=None, after=None,
        name="mm"):
    if ta:
        K, M = a.shape
    else:
        M, K = a.shape
    if tb:
        N = b.shape[-2] if n is None else n
    else:
        N = b.shape[-1] if n is None else n
    assert k is None or k == K
    ro, co = b_off
    n_off, k_off = (ro, co) if tb else (co, ro)
    oro, oco = (0, 0) if into is None else into[3]
    tm, tn, tk = _mm_tiles(M, N, K, a.dtype.itemsize, b.dtype.itemsize, jnp.dtype(out_dtype).itemsize,
                           0 if add is None else add.dtype.itemsize, (oro, (n_off, oco), k_off))
    nk = K // tk
    if tb:
        b_spec = pl.BlockSpec((tn, tk), lambda i, j, kk: (j + ro // tn, kk + co // tk))
        b_dims = (1,)
    else:
        b_spec = pl.BlockSpec((tk, tn), lambda i, j, kk: (kk + ro // tk, j + co // tn))
        b_dims = (0,)
    if ta:
        a_spec = pl.BlockSpec((tk, tm), lambda i, j, kk: (kk, i))
        a_dims = (0,)
    else:
        a_spec = pl.BlockSpec((tm, tk), lambda i, j, kk: (i, kk))
        a_dims = (1,)
    dims = ((a_dims, b_dims), ((), ()))
    add_spec = pl.BlockSpec((tm, tn), lambda i, j, kk: (i, j))
    has_add = add is not None
    if into is None:
        o_spec, o_shape, buf = add_spec, (M, N), None
    else:
        buf, o_shape, ol, _ = into
        assert not has_add
        o_spec = pl.BlockSpec((None, tm, tn), lambda i, j, kk: (ol, i + oro // tm, j + oco // tn))
    has_buf = buf is not None

    def body(*refs):
        refs = list(refs)
        acc_ref = refs.pop() if nk > 1 else None
        o_ref = refs.pop()
        a_ref, b_ref = refs[0], refs[1]
        add_ref = refs[2] if has_add else None
        part = lax.dot_general(a_ref[...].astype(BF16), b_ref[...].astype(BF16), dims, preferred_element_type=F32)

        def finish(r):
            if has_add:
                r = r + add_ref[...].astype(F32)
            o_ref[...] = r.astype(out_dtype)

        if nk == 1:
            finish(part)
        else:
            kk = pl.program_id(2)

            @pl.when(kk == 0)
            def _():
                acc_ref[...] = part

            @pl.when(kk > 0)
            def _():
                acc_ref[...] += part

            @pl.when(kk == nk - 1)
            def _():
                finish(acc_ref[...])

    in_specs = [a_spec, b_spec] + ([add_spec] if has_add else []) + ([ANY] if has_buf else [])
    args = (a, b) + ((add,) if has_add else ()) + ((buf,) if has_buf else ())
    if after is not None:
        in_specs, args = in_specs + [ANY], args + (after,)
    return pl.pallas_call(
        body, grid=(M // tm, N // tn, nk), in_specs=in_specs, out_specs=o_spec,
        out_shape=S(o_shape, out_dtype), scratch_shapes=[pltpu.VMEM((tm, tn), F32)] if nk > 1 else [],
        input_output_aliases={2: 0} if has_buf else {},
        compiler_params=_cparams("parallel", "parallel", "arbitrary"), name=name)(*args)


def _rms_fwd(x, g, out_dtype, res=None, name="rms_fwd"):
    N, D = x.shape
    tm = _tile(N, (512, 256, 128))
    has_res = res is not None

    def body(*refs):
        if has_res:
            x_ref, g_ref, r_ref, o_ref = refs
        else:
            x_ref, g_ref, o_ref = refs
        xv = x_ref[...].astype(F32)
        y = xv * lax.rsqrt(jnp.mean(xv * xv, axis=-1, keepdims=True) + EPS) * g_ref[...]
        if has_res:
            y = y + r_ref[...]
        o_ref[...] = y.astype(out_dtype)

    row = pl.BlockSpec((tm, D), lambda i: (i, 0))
    vec = pl.BlockSpec((1, D), lambda i: (0, 0))
    return pl.pallas_call(
        body, grid=(N // tm,), in_specs=[row, vec] + ([row] if has_res else []), out_specs=row,
        out_shape=S((N, D), out_dtype), compiler_params=_cparams("parallel"), name=name)(
            *((x, g) + ((res,) if has_res else ())))


def _rms_bwd(x, g, dy, add=None, out_dtype=F32, name="rms_bwd"):
    N, D = x.shape
    tm = _tile(N, (512, 256, 128))
    has_add = add is not None

    def body(*refs):
        if has_add:
            x_ref, g_ref, dy_ref, add_ref, dx_ref, dg_ref = refs
        else:
            x_ref, g_ref, dy_ref, dx_ref, dg_ref = refs
        xv = x_ref[...].astype(F32)
        dyv = dy_ref[...].astype(F32)
        r = lax.rsqrt(jnp.mean(xv * xv, axis=-1, keepdims=True) + EPS)
        u = dyv * g_ref[...]
        dx = r * u - xv * (r * r * r * jnp.mean(u * xv, axis=-1, keepdims=True))
        if has_add:
            dx = dx + add_ref[...]
        dx_ref[...] = dx.astype(out_dtype)

        @pl.when(pl.program_id(0) == 0)
        def _():
            dg_ref[...] = jnp.zeros_like(dg_ref)

        dg_ref[...] += jnp.sum(dyv * xv * r, axis=0, keepdims=True)

    row = pl.BlockSpec((tm, D), lambda i: (i, 0))
    vec = pl.BlockSpec((1, D), lambda i: (0, 0))
    return pl.pallas_call(
        body, grid=(N // tm,), in_specs=[row, vec, row] + ([row] if has_add else []), out_specs=(row, vec),
        out_shape=(S((N, D), out_dtype), S((1, D), F32)), compiler_params=_cparams("arbitrary"), name=name)(
            *((x, g, dy) + ((add,) if has_add else ())))


def _shift_down(x, s, row):
    return jnp.where(row >= s, pltpu.roll(x, s, axis=0), 0.0)


def _shift_up(x, s, row):
    T = x.shape[0]
    return jnp.where(row < T - s, pltpu.roll(x, T - s, axis=0), 0.0)


SLAB = 16


def _conv_wrap(x, w_ref, b_ref):
    W = w_ref.shape[0]
    y = x * w_ref[W - 1:W, :] + b_ref[...]
    for s in range(1, W):
        y = y + pltpu.roll(x, s, axis=0) * w_ref[W - 1 - s:W - s, :]
    return y


def _conv_head(x_head, w_ref, b_ref):
    row = lax.broadcasted_iota(jnp.int32, x_head.shape, 0)
    W = w_ref.shape[0]
    y = x_head * w_ref[W - 1:W, :] + b_ref[...]
    for s in range(1, W):
        y = y + _shift_down(x_head, s, row) * w_ref[W - 1 - s:W - s, :]
    return y


def _conv_bwd_wrap(dy, x, w_ref):
    W = w_ref.shape[0]
    T = dy.shape[0]
    dx = dy * w_ref[W - 1:W, :]
    dws = [None] * W
    dws[W - 1] = jnp.sum(dy * x, axis=0, keepdims=True)
    for s in range(1, W):
        up = pltpu.roll(dy, T - s, axis=0)
        dx = dx + up * w_ref[W - 1 - s:W - s, :]
        dws[W - 1 - s] = jnp.sum(up * x, axis=0, keepdims=True)
    return dx, jnp.concatenate(dws, axis=0), jnp.sum(dy, axis=0, keepdims=True)


def _conv_bwd_fix(dy_head, dy_tail, x_tail, w_ref):
    row = lax.broadcasted_iota(jnp.int32, dy_tail.shape, 0)
    W = w_ref.shape[0]
    dx = dy_tail * w_ref[W - 1:W, :]
    extra = [jnp.zeros((1, dy_tail.shape[1]), F32)] * W
    for s in range(1, W):
        dx = dx + _shift_up(dy_tail, s, row) * w_ref[W - 1 - s:W - s, :]
        extra[W - 1 - s] = jnp.sum(jnp.where(row < s, dy_head * pltpu.roll(x_tail, s, axis=0), 0.0), axis=0,
                                   keepdims=True)
    return dx, jnp.concatenate(extra, axis=0)


def _gelu(g):
    t = jnp.tanh(GELU_C0 * (g + GELU_C1 * g * g * g))
    return 0.5 * g * (1.0 + t), t


def _dgelu(g, t):
    return 0.5 * (1.0 + t) + 0.5 * g * (1.0 - t * t) * (GELU_C0 * (1.0 + 3.0 * GELU_C1 * g * g))


def _cspec(T, off=0):
    return pl.BlockSpec((1, T, CT), lambda j, b: (b, 0, j + off))


def _pspec(rows, off=0):
    return pl.BlockSpec((rows, CT), lambda j, b: (0, j + off))


def _conv_fwd_call(x3, x_off, C, w, b, name):
    Bl, T, _ = x3.shape
    W = w.shape[0]

    def body(x_ref, w_ref, b_ref, o_ref):
        o_ref[0] = _conv_wrap(x_ref[0], w_ref, b_ref)
        o_ref[0, 0:SLAB, :] = _conv_head(x_ref[0, 0:SLAB, :], w_ref, b_ref)

    return pl.pallas_call(
        body, grid=(C // CT, Bl), in_specs=[_cspec(T, x_off // CT), _pspec(W), _pspec(1)], out_specs=_cspec(T),
        out_shape=S((Bl, T, C), F32), compiler_params=_cparams("parallel", "arbitrary"), name=name)(x3, w, b)


def _conv_bwd_call(dy3, x3, x_off, C, w, name):
    Bl, T, _ = x3.shape
    W = w.shape[0]

    def body(dy_ref, x_ref, w_ref, dx_ref, dw_ref, db_ref):
        dx, dw, db = _conv_bwd_wrap(dy_ref[0], x_ref[0], w_ref)
        dx_tail, dw_extra = _conv_bwd_fix(dy_ref[0, 0:SLAB, :], dy_ref[0, T - SLAB:T, :], x_ref[0, T - SLAB:T, :],
                                          w_ref)
        dx_ref[0] = dx.astype(BF16)
        dx_ref[0, T - SLAB:T, :] = dx_tail.astype(BF16)

        @pl.when(pl.program_id(1) == 0)
        def _():
            dw_ref[...] = jnp.zeros_like(dw_ref)
            db_ref[...] = jnp.zeros_like(db_ref)

        dw_ref[...] += dw - dw_extra
        db_ref[...] += db

    return pl.pallas_call(
        body, grid=(C // CT, Bl), in_specs=[_cspec(T), _cspec(T, x_off // CT), _pspec(W)],
        out_specs=(_cspec(T), _pspec(W), _pspec(1)),
        out_shape=(S((Bl, T, C), BF16), S((W, C), F32), S((1, C), F32)),
        compiler_params=_cparams("parallel", "arbitrary"), name=name)(dy3, x3, w)


def _ffn_mid_fwd(ug3, uv3, wc, bc, name):
    Bl, T, F = ug3.shape
    nf = F // CT

    def body(ug_ref, uv_ref, wg_ref, wv_ref, bg_ref, bv_ref, o_ref):
        g = _conv_wrap(ug_ref[0], wg_ref, bg_ref)
        v = _conv_wrap(uv_ref[0], wv_ref, bv_ref)
        o_ref[0] = (_gelu(g)[0] * v).astype(BF16)
        g = _conv_head(ug_ref[0, 0:SLAB, :], wg_ref, bg_ref)
        v = _conv_head(uv_ref[0, 0:SLAB, :], wv_ref, bv_ref)
        o_ref[0, 0:SLAB, :] = (_gelu(g)[0] * v).astype(BF16)

    return pl.pallas_call(
        body, grid=(nf, Bl),
        in_specs=[_cspec(T), _cspec(T), _pspec(3), _pspec(3, nf), _pspec(1), _pspec(1, nf)], out_specs=_cspec(T),
        out_shape=S((Bl, T, F), BF16), compiler_params=_cparams("parallel", "arbitrary"), name=name)(
            ug3, uv3, wc, wc, bc, bc)


def _ffn_mid_bwd(ug3, uv3, dact3, wc, bc, name):
    Bl, T, F = ug3.shape
    nf = F // CT

    def gate_grads(g, v, da):
        gel, t = _gelu(g)
        return da * v * _dgelu(g, t), da * gel

    def body(ug_ref, uv_ref, da_ref, wg_ref, wv_ref, bg_ref, bv_ref, dug_ref, duv_ref, dwg_ref, dwv_ref, dbg_ref,
             dbv_ref, dg_scr, dv_scr):
        dg, dv = gate_grads(_conv_wrap(ug_ref[0], wg_ref, bg_ref), _conv_wrap(uv_ref[0], wv_ref, bv_ref), da_ref[0])
        dg_scr[...] = dg
        dv_scr[...] = dv
        dg_head, dv_head = gate_grads(_conv_head(ug_ref[0, 0:SLAB, :], wg_ref, bg_ref),
                                      _conv_head(uv_ref[0, 0:SLAB, :], wv_ref, bv_ref), da_ref[0, 0:SLAB, :])
        dg_scr[0:SLAB, :] = dg_head
        dv_scr[0:SLAB, :] = dv_head
        dg = dg_scr[...]
        dv = dv_scr[...]
        tail = slice(T - SLAB, T)
        dug, dwg, dbg = _conv_bwd_wrap(dg, ug_ref[0], wg_ref)
        dug_tail, dwg_extra = _conv_bwd_fix(dg_head, dg[tail], ug_ref[0, tail, :], wg_ref)
        duv, dwv, dbv = _conv_bwd_wrap(dv, uv_ref[0], wv_ref)
        duv_tail, dwv_extra = _conv_bwd_fix(dv_head, dv[tail], uv_ref[0, tail, :], wv_ref)
        dug_ref[0] = dug.astype(BF16)
        duv_ref[0] = duv.astype(BF16)
        dug_ref[0, tail, :] = dug_tail.astype(BF16)
        duv_ref[0, tail, :] = duv_tail.astype(BF16)

        @pl.when(pl.program_id(1) == 0)
        def _():
            dwg_ref[...] = jnp.zeros_like(dwg_ref)
            dwv_ref[...] = jnp.zeros_like(dwv_ref)
            dbg_ref[...] = jnp.zeros_like(dbg_ref)
            dbv_ref[...] = jnp.zeros_like(dbv_ref)

        dwg_ref[...] += dwg - dwg_extra
        dwv_ref[...] += dwv - dwv_extra
        dbg_ref[...] += dbg
        dbv_ref[...] += dbv

    return pl.pallas_call(
        body, grid=(nf, Bl),
        in_specs=[_cspec(T), _cspec(T), _cspec(T), _pspec(3), _pspec(3, nf), _pspec(1), _pspec(1, nf)],
        out_specs=(_cspec(T), _cspec(T), _pspec(3), _pspec(3), _pspec(1), _pspec(1)),
        out_shape=(S((Bl, T, F), BF16), S((Bl, T, F), BF16), S((3, F), F32), S((3, F), F32), S((1, F), F32),
                   S((1, F), F32)),
        scratch_shapes=[pltpu.VMEM((T, CT), F32), pltpu.VMEM((T, CT), F32)],
        compiler_params=_cparams("parallel", "arbitrary"), name=name)(ug3, uv3, dact3, wc, wc, bc, bc)


def _lru_gates(xc, rp, ip, br_ref, bi_ref, lam_ref):
    r = jax.nn.sigmoid(rp + br_ref[...])
    i = jax.nn.sigmoid(ip + bi_ref[...])
    lam = lam_ref[...]
    sp = jnp.maximum(-lam, 0.0) + jnp.log1p(jnp.exp(-jnp.abs(lam)))
    log_a = (-LRU_C) * r * sp
    a = jnp.exp(log_a)
    z = 2.0 * log_a
    one_m_a2 = jnp.where(z > -0.05, -z * (1.0 + z * (0.5 + z * (1.0 / 6.0 + z * (1.0 / 24.0)))), 1.0 - a * a)
    mult = jnp.sqrt(one_m_a2)
    return r, i, sp, a, mult


def _rglru_fwd(xc3, gates3, proj3, br, bi, lam, name):
    Bl, T, C = xc3.shape
    nsteps = int(math.log2(T))
    assert 1 << nsteps == T

    def body(xc_ref, rp_ref, ip_ref, ug_ref, br_ref, bi_ref, lam_ref, y_ref, h_ref):
        row = lax.broadcasted_iota(jnp.int32, (T, CT), 0)
        xc = xc_ref[0]
        r, i, sp, a, mult = _lru_gates(xc, rp_ref[0], ip_ref[0], br_ref, bi_ref, lam_ref)
        b = mult * (i * xc)
        for st in range(nsteps):
            s = 1 << st
            a_sh = jnp.where(row >= s, pltpu.roll(a, s, axis=0), 1.0)
            b = a * _shift_down(b, s, row) + b
            a = a * a_sh
        h_ref[0] = b
        y_ref[0] = (b * _gelu(ug_ref[0])[0]).astype(BF16)

    return pl.pallas_call(
        body, grid=(C // CT, Bl),
        in_specs=[_cspec(T), _cspec(T), _cspec(T, C // CT), _cspec(T), _pspec(1), _pspec(1), _pspec(1)],
        out_specs=(_cspec(T), _cspec(T)), out_shape=(S((Bl, T, C), BF16), S((Bl, T, C), F32)),
        compiler_params=_cparams("parallel", "arbitrary"), name=name)(xc3, gates3, gates3, proj3, br, bi, lam)


def _rglru_bwd(dy3, xc3, gates3, proj3, h3, br, bi, lam, name):
    Bl, T, C = xc3.shape
    nsteps = int(math.log2(T))

    def body(dy_ref, xc_ref, rp_ref, ip_ref, ug_ref, h_ref, br_ref, bi_ref, lam_ref,
             dxc_ref, drp_ref, dip_ref, dug_ref, dbr_ref, dbi_ref, dlam_ref):
        row = lax.broadcasted_iota(jnp.int32, (T, CT), 0)
        xc = xc_ref[0]
        r, i, sp, a, mult = _lru_gates(xc, rp_ref[0], ip_ref[0], br_ref, bi_ref, lam_ref)
        h = h_ref[0]
        dy = dy_ref[0]
        ug = ug_ref[0]
        gel, t = _gelu(ug)
        dug_ref[0] = (dy * h * _dgelu(ug, t)).astype(BF16)
        gacc = dy * gel
        an = _shift_up(a, 1, row)
        for st in range(nsteps):
            s = 1 << st
            an_sh = jnp.where(row < T - s, pltpu.roll(an, T - s, axis=0), 1.0)
            gacc = an * _shift_up(gacc, s, row) + gacc
            an = an * an_sh
        da = gacc * _shift_down(h, 1, row)
        ix = i * xc
        d_mult = gacc * ix
        d_i = gacc * mult * xc
        dxc_ref[0] = gacc * mult * i
        d_log_a = da * a - d_mult * (a * a) / mult
        d_r = d_log_a * ((-LRU_C) * sp)
        d_sp = jnp.sum(d_log_a * ((-LRU_C) * r), axis=0, keepdims=True)
        drp = d_r * r * (1.0 - r)
        dip = d_i * i * (1.0 - i)
        drp_ref[0] = drp.astype(BF16)
        dip_ref[0] = dip.astype(BF16)

        @pl.when(pl.program_id(1) == 0)
        def _():
            dbr_ref[...] = jnp.zeros_like(dbr_ref)
            dbi_ref[...] = jnp.zeros_like(dbi_ref)
            dlam_ref[...] = jnp.zeros_like(dlam_ref)

        dbr_ref[...] += jnp.sum(drp, axis=0, keepdims=True)
        dbi_ref[...] += jnp.sum(dip, axis=0, keepdims=True)
        dlam_ref[...] += d_sp * (-jax.nn.sigmoid(-lam_ref[...]))

    vec = S((1, C), F32)
    act = S((Bl, T, C), BF16)
    return pl.pallas_call(
        body, grid=(C // CT, Bl),
        in_specs=[_cspec(T), _cspec(T), _cspec(T), _cspec(T, C // CT), _cspec(T), _cspec(T)] + [_pspec(1)] * 3,
        out_specs=(_cspec(T), _cspec(T), _cspec(T), _cspec(T), _pspec(1), _pspec(1), _pspec(1)),
        out_shape=(S((Bl, T, C), F32), act, act, act, vec, vec, vec),
        compiler_params=_cparams("parallel", "arbitrary"), name=name)(dy3, xc3, gates3, gates3, proj3, h3, br, bi, lam)


NT = (((1,), (1,)), ((), ()))
TN = (((0,), (0,)), ((), ()))


def _hs(h):
    return slice(h * HEAD, (h + 1) * HEAD)


def _mem_softmax(qb, kb):
    s = lax.dot_general(qb, kb, NT, preferred_element_type=F32) * (HEAD ** -0.5)
    e = jnp.exp(s - jnp.max(s, axis=-1, keepdims=True))
    return e / jnp.sum(e, axis=-1, keepdims=True)


def _mem_attn_fwd(proj3, q_off, mkv3, name):
    Bl, T, _ = proj3.shape
    M = mkv3.shape[1]
    tq = _tile(T, (512, 256, 128))

    def body(q_ref, k_ref, v_ref, o_ref):
        q = q_ref[0].astype(BF16)
        k = k_ref[0].astype(BF16)
        v = v_ref[0].astype(BF16)
        outs = []
        for h in range(MEM_HEADS):
            p = _mem_softmax(q[:, _hs(h)], k[:, _hs(h)])
            outs.append(jnp.dot(p.astype(BF16), v[:, _hs(h)], preferred_element_type=F32))
        o_ref[0] = jnp.concatenate(outs, axis=-1).astype(BF16)

    return pl.pallas_call(
        body, grid=(Bl, T // tq),
        in_specs=[pl.BlockSpec((1, tq, MEM_W), lambda b, t: (b, t, q_off // MEM_W)),
                  pl.BlockSpec((1, M, MEM_W), lambda b, t: (b, 0, 0)),
                  pl.BlockSpec((1, M, MEM_W), lambda b, t: (b, 0, 1))],
        out_specs=pl.BlockSpec((1, tq, MEM_W), lambda b, t: (b, t, 0)),
        out_shape=S((Bl, T, MEM_W), BF16), compiler_params=_cparams("parallel", "parallel"), name=name)(
            proj3, mkv3, mkv3)


def _mem_attn_bwd(proj3, q_off, mkv3, do3, name):
    Bl, T, _ = proj3.shape
    M = mkv3.shape[1]
    tq = _tile(T, (512, 256, 128))
    scale = HEAD ** -0.5

    def body(q_ref, k_ref, v_ref, do_ref, dq_ref, dkv_ref):
        q = q_ref[0].astype(BF16)
        k = k_ref[0].astype(BF16)
        v = v_ref[0].astype(BF16)
        do = do_ref[0].astype(BF16)
        dqs, dks, dvs = [], [], []
        for h in range(MEM_HEADS):
            qh, kh, vh, doh = q[:, _hs(h)], k[:, _hs(h)], v[:, _hs(h)], do[:, _hs(h)]
            p = _mem_softmax(qh, kh)
            dvs.append(lax.dot_general(p.astype(BF16), doh, TN, preferred_element_type=F32))
            dp = lax.dot_general(doh, vh, NT, preferred_element_type=F32)
            ds = (p * (dp - jnp.sum(p * dp, axis=-1, keepdims=True)) * scale).astype(BF16)
            dqs.append(jnp.dot(ds, kh, preferred_element_type=F32))
            dks.append(lax.dot_general(ds, qh, TN, preferred_element_type=F32))
        dq_ref[0] = jnp.concatenate(dqs, axis=-1).astype(BF16)

        @pl.when(pl.program_id(1) == 0)
        def _():
            dkv_ref[...] = jnp.zeros_like(dkv_ref)

        dkv_ref[0] += jnp.concatenate(dks + dvs, axis=-1)

    return pl.pallas_call(
        body, grid=(Bl, T // tq),
        in_specs=[pl.BlockSpec((1, tq, MEM_W), lambda b, t: (b, t, q_off // MEM_W)),
                  pl.BlockSpec((1, M, MEM_W), lambda b, t: (b, 0, 0)),
                  pl.BlockSpec((1, M, MEM_W), lambda b, t: (b, 0, 1)),
                  pl.BlockSpec((1, tq, MEM_W), lambda b, t: (b, t, 0))],
        out_specs=(pl.BlockSpec((1, tq, MEM_W), lambda b, t: (b, t, 0)),
                   pl.BlockSpec((1, M, 2 * MEM_W), lambda b, t: (b, 0, 0))),
        out_shape=(S((Bl, T, MEM_W), BF16), S((Bl, M, 2 * MEM_W), F32)),
        compiler_params=_cparams("parallel", "arbitrary"), name=name)(proj3, mkv3, mkv3, do3)


GROUP_ROWS = SWA_GROUP * WIN


def _group_rows(x, kvh):
    return jnp.concatenate([x[:, _hs(SWA_GROUP * kvh + g)] for g in range(SWA_GROUP)], axis=0)


def _group_col(vals):
    grp = lax.shift_right_logical(lax.broadcasted_iota(jnp.int32, (GROUP_ROWS, 1), 0), WIN.bit_length() - 1)
    col = jnp.full((GROUP_ROWS, 1), vals[-1], F32)
    for g in range(SWA_GROUP - 2, -1, -1):
        col = jnp.where(grp == g, vals[g], col)
    return col


def _swa_probs(qh, kph, kch, sink, slope, has_prev):
    qi = jnp.bitwise_and(lax.broadcasted_iota(jnp.int32, (GROUP_ROWS, WIN), 0), WIN - 1)
    kj = lax.broadcasted_iota(jnp.int32, (GROUP_ROWS, WIN), 1)
    scale = HEAD ** -0.5
    sp = lax.dot_general(qh, kph, NT, preferred_element_type=F32) * scale
    sc = lax.dot_general(qh, kch, NT, preferred_element_type=F32) * scale
    dist_p = (qi + WIN - kj).astype(F32)
    dist_c = (qi - kj).astype(F32)
    neg = -jnp.inf
    sp = jnp.where(kj > qi + jnp.where(has_prev, 0, WIN), sp - slope * dist_p, neg)
    sc = jnp.where(kj <= qi, sc - slope * dist_c, neg)
    m = jnp.maximum(jnp.maximum(jnp.max(sp, axis=-1, keepdims=True), jnp.max(sc, axis=-1, keepdims=True)), sink)
    ep = jnp.exp(sp - m)
    ec = jnp.exp(sc - m)
    es = jnp.exp(sink - m)
    inv = 1.0 / (jnp.sum(ep, axis=-1, keepdims=True) + jnp.sum(ec, axis=-1, keepdims=True) + es)
    return ep * inv, ec * inv, es * inv


def _swa_specs(nb):
    prev = lambda n: jnp.maximum(n - 1, 0)
    q = pl.BlockSpec((1, WIN, MIX_W), lambda b, n: (b, n, 0))
    kp = pl.BlockSpec((1, WIN, MEM_W), lambda b, n: (b, prev(n), 0))
    kc = pl.BlockSpec((1, WIN, MEM_W), lambda b, n: (b, n, 0))
    vp = pl.BlockSpec((1, WIN, MEM_W), lambda b, n: (b, prev(n), 1))
    vc = pl.BlockSpec((1, WIN, MEM_W), lambda b, n: (b, n, 1))
    sm = pl.BlockSpec(memory_space=pltpu.SMEM)
    return q, kp, kc, vp, vc, sm


def _swa_fwd(proj3, kv3, sinks, name):
    Bl, T, _ = proj3.shape
    nb = T // WIN
    q_s, kp_s, kc_s, vp_s, vc_s, sm = _swa_specs(nb)

    def body(q_ref, kp_ref, kc_ref, vp_ref, vc_ref, sink_ref, o_ref):
        has_prev = pl.program_id(1) > 0
        q = q_ref[0].astype(BF16)
        kp, kc = kp_ref[0].astype(BF16), kc_ref[0].astype(BF16)
        vp, vc = vp_ref[0].astype(BF16), vc_ref[0].astype(BF16)
        outs = []
        for kvh in range(SWA_HEADS // SWA_GROUP):
            kvs = _hs(kvh)
            heads = range(SWA_GROUP * kvh, SWA_GROUP * (kvh + 1))
            pp, pc, _ = _swa_probs(_group_rows(q, kvh), kp[:, kvs], kc[:, kvs], _group_col([sink_ref[h] for h in heads]),
                                   _group_col([SLOPES[h] for h in heads]), has_prev)
            og = (jnp.dot(pp.astype(BF16), vp[:, kvs], preferred_element_type=F32)
                  + jnp.dot(pc.astype(BF16), vc[:, kvs], preferred_element_type=F32))
            outs += [og[g * WIN:(g + 1) * WIN] for g in range(SWA_GROUP)]
        o_ref[0] = jnp.concatenate(outs, axis=-1).astype(BF16)

    return pl.pallas_call(
        body, grid=(Bl, nb), in_specs=[q_s, kp_s, kc_s, vp_s, vc_s, sm], out_specs=q_s,
        out_shape=S((Bl, T, MIX_W), BF16), compiler_params=_cparams("parallel", "parallel"), name=name)(
            proj3, kv3, kv3, kv3, kv3, sinks)


def _swa_bwd(proj3, kv3, sinks, do3, name):
    Bl, T, _ = proj3.shape
    nb = T // WIN
    q_s, kp_s, kc_s, vp_s, vc_s, sm = _swa_specs(nb)
    kv_s = pl.BlockSpec((1, WIN, 2 * MEM_W), lambda b, n: (b, n, 0))
    sk_s = pl.BlockSpec((8, LANES), lambda b, n: (0, 0))
    scale = HEAD ** -0.5

    def body(q_ref, kp_ref, kc_ref, vp_ref, vc_ref, sink_ref, do_ref, dq_ref, dkc_ref, dkp_ref, dsk_ref):
        has_prev = pl.program_id(1) > 0
        q = q_ref[0].astype(BF16)
        kp, kc = kp_ref[0].astype(BF16), kc_ref[0].astype(BF16)
        vp, vc = vp_ref[0].astype(BF16), vc_ref[0].astype(BF16)
        do = do_ref[0].astype(BF16)
        lane = lax.broadcasted_iota(jnp.int32, (8, LANES), 1)
        srow = lax.broadcasted_iota(jnp.int32, (8, LANES), 0)
        dsk = jnp.zeros((8, LANES), F32)
        dqs = []
        dkc, dkp, dvc, dvp = [], [], [], []
        grp = lax.shift_right_logical(lax.broadcasted_iota(jnp.int32, (GROUP_ROWS, 1), 0), WIN.bit_length() - 1)
        for kvh in range(SWA_HEADS // SWA_GROUP):
            kvs = _hs(kvh)
            heads = range(SWA_GROUP * kvh, SWA_GROUP * (kvh + 1))
            qg, dog = _group_rows(q, kvh), _group_rows(do, kvh)
            pp, pc, ps = _swa_probs(qg, kp[:, kvs], kc[:, kvs], _group_col([sink_ref[h] for h in heads]),
                                    _group_col([SLOPES[h] for h in heads]), has_prev)
            dpp = lax.dot_general(dog, vp[:, kvs], NT, preferred_element_type=F32)
            dpc = lax.dot_general(dog, vc[:, kvs], NT, preferred_element_type=F32)
            delta = jnp.sum(pp * dpp, axis=-1, keepdims=True) + jnp.sum(pc * dpc, axis=-1, keepdims=True)
            dsp = (pp * (dpp - delta) * scale).astype(BF16)
            dsc = (pc * (dpc - delta) * scale).astype(BF16)
            dqg = (jnp.dot(dsp, kp[:, kvs], preferred_element_type=F32)
                   + jnp.dot(dsc, kc[:, kvs], preferred_element_type=F32))
            dqs += [dqg[g * WIN:(g + 1) * WIN] for g in range(SWA_GROUP)]
            dkc.append(lax.dot_general(dsc, qg, TN, preferred_element_type=F32))
            dkp.append(lax.dot_general(dsp, qg, TN, preferred_element_type=F32))
            dvc.append(lax.dot_general(pc.astype(BF16), dog, TN, preferred_element_type=F32))
            dvp.append(lax.dot_general(pp.astype(BF16), dog, TN, preferred_element_type=F32))
            dsink = ps * delta
            for g, h in enumerate(heads):
                dsk = dsk + jnp.where((lane == h) & (srow == 0), -jnp.sum(jnp.where(grp == g, dsink, 0.0)), 0.0)
        dq_ref[0] = jnp.concatenate(dqs, axis=-1).astype(BF16)
        dkc_ref[0] = jnp.concatenate(dkc + dvc, axis=-1)
        dkp_ref[0] = jnp.concatenate(dkp + dvp, axis=-1)

        @pl.when((pl.program_id(0) == 0) & (pl.program_id(1) == 0))
        def _():
            dsk_ref[...] = jnp.zeros_like(dsk_ref)

        dsk_ref[...] += dsk

    return pl.pallas_call(
        body, grid=(Bl, nb), in_specs=[q_s, kp_s, kc_s, vp_s, vc_s, sm, q_s], out_specs=(q_s, kv_s, kv_s, sk_s),
        out_shape=(S((Bl, T, MIX_W), BF16), S((Bl, T, 2 * MEM_W), F32), S((Bl, T, 2 * MEM_W), F32), S((8, LANES), F32)),
        compiler_params=_cparams("arbitrary", "arbitrary"), name=name)(proj3, kv3, kv3, kv3, kv3, sinks, do3)


def _kv_grad_combine(parts, name):
    Bl, T, W = parts[0][0].shape
    nb = T // WIN
    nl = len(parts)

    def body(*refs):
        o_ref = refs[-1]
        has_next = jnp.where(pl.program_id(1) == nb - 1, 0.0, 1.0)
        acc = None
        for l in range(nl):
            c = refs[2 * l][0] + has_next * refs[2 * l + 1][0]
            acc = c if acc is None else acc + c
        o_ref[0] = acc.astype(BF16)

    cur = pl.BlockSpec((1, WIN, W), lambda b, n: (b, n, 0))
    nxt = pl.BlockSpec((1, WIN, W), lambda b, n: (b, jnp.minimum(n + 1, nb - 1), 0))
    return pl.pallas_call(
        body, grid=(Bl, nb), in_specs=[cur, nxt] * nl, out_specs=cur, out_shape=S((Bl, T, W), BF16),
        compiler_params=_cparams("parallel", "parallel"), name=name)(*[a for pr in parts for a in pr])


def _loss_bwd(y, target, name="loss"):
    N, D = y.shape
    tm = _tile(N, (512, 256, 128))

    def body(y_ref, t_ref, dy_ref, l_ref):
        e = y_ref[...] - t_ref[...]
        dy_ref[...] = e * (1.0 / D)

        @pl.when(pl.program_id(0) == 0)
        def _():
            l_ref[...] = jnp.zeros_like(l_ref)

        l_ref[...] += jnp.sum(e * e, axis=0, keepdims=True) * (0.5 / D)

    row = pl.BlockSpec((tm, D), lambda i: (i, 0))
    vec = pl.BlockSpec((1, D), lambda i: (0, 0))
    return pl.pallas_call(
        body, grid=(N // tm,), in_specs=[row, row], out_specs=(row, vec), out_shape=(S((N, D), F32), S((1, D), F32)),
        compiler_params=_cparams("arbitrary"), name=name)(y, target)


def _all_gather(x, name):
    R, C = x.shape

    def body(x_ref, out_ref, send_sems, recv_sems, local_sem):
        mx, my, mc = lax.axis_index("x"), lax.axis_index("y"), lax.axis_index("c")
        me, sibling = (mx, my, mc), (mx, my, 1 - mc)
        chips = [(1 - mx, my), (mx, 1 - my), (1 - mx, 1 - my)]

        def rows(px, py, pc):
            return out_ref.at[4 * px + 2 * py + pc]

        def copy(kk, block, to, src=None):
            return pltpu.make_async_remote_copy(
                src_ref=rows(*block) if src is None else src, dst_ref=rows(*block), send_sem=send_sems.at[kk],
                recv_sem=recv_sems.at[kk], device_id=to, device_id_type=MESH)

        mine = pltpu.make_async_copy(x_ref, rows(*me), local_sem)
        mine.start()
        first = [copy(0, me, sibling, src=x_ref)]
        first += [copy(1 + j, me, (*chip, mc), src=x_ref) for j, chip in enumerate(chips)]
        for cp in first:
            cp.start()
        passed = [copy(4 + j, (*chip, mc), sibling) for j, chip in enumerate(chips)]
        for j, chip in enumerate(chips):
            copy(1 + j, (*chip, mc), me).wait_recv()
            passed[j].start()
        copy(0, sibling, me).wait_recv()
        for j, chip in enumerate(chips):
            copy(4 + j, (*chip, 1 - mc), me).wait_recv()
        for cp in first + passed:
            cp.wait_send()
        mine.wait()

    return pl.pallas_call(
        body, out_shape=S((N_DEV, R, C), x.dtype), in_specs=[ANY], out_specs=ANY,
        scratch_shapes=[pltpu.SemaphoreType.DMA((7,)), pltpu.SemaphoreType.DMA((7,)), pltpu.SemaphoreType.DMA(())],
        name=name)(x)


def _ag_weights(shards, row_sharded, name):
    n = len(shards)

    def full_shape(a, rows):
        if rows:
            return a.shape[:-2] + (N_DEV * a.shape[-2],) + a.shape[-1:]
        return (N_DEV,) + a.shape

    def body(*refs):
        x_refs, o_refs = refs[:n], refs[n:2 * n]
        send_sems, recv_sems, local_sems = refs[2 * n:]
        mx, my, mc = lax.axis_index("x"), lax.axis_index("y"), lax.axis_index("c")
        me, sibling = (mx, my, mc), (mx, my, 1 - mc)
        chips = [(1 - mx, my), (mx, 1 - my), (1 - mx, 1 - my)]

        def dst(t, px, py, pc):
            d = 4 * px + 2 * py + pc
            if not row_sharded[t]:
                return o_refs[t].at[d]
            r = shards[t].shape[-2]
            idx = (slice(None),) * (shards[t].ndim - 2) + (pl.ds(pl.multiple_of(d * r, 16), r), slice(None))
            return o_refs[t].at[idx]

        def copy(kk, t, block, to, src=None):
            return pltpu.make_async_remote_copy(
                src_ref=dst(t, *block) if src is None else src, dst_ref=dst(t, *block),
                send_sem=send_sems.at[kk * n + t], recv_sem=recv_sems.at[kk * n + t], device_id=to,
                device_id_type=MESH)

        mine = [pltpu.make_async_copy(x_refs[t], dst(t, *me), local_sems.at[t]) for t in range(n)]
        for cp in mine:
            cp.start()
        first = []
        for t in range(n):
            first.append(copy(0, t, me, sibling, src=x_refs[t]))
            first += [copy(1 + j, t, me, (*chip, mc), src=x_refs[t]) for j, chip in enumerate(chips)]
        for cp in first:
            cp.start()
        passed = []
        for j, chip in enumerate(chips):
            for t in range(n):
                copy(1 + j, t, (*chip, mc), me).wait_recv()
                cp = copy(4 + j, t, (*chip, mc), sibling)
                cp.start()
                passed.append(cp)
        for t in range(n):
            copy(0, t, sibling, me).wait_recv()
            for j, chip in enumerate(chips):
                copy(4 + j, t, (*chip, 1 - mc), me).wait_recv()
        for cp in first + passed:
            cp.wait_send()
        for cp in mine:
            cp.wait()

    return pl.pallas_call(
        body, out_shape=tuple(S(full_shape(a, r), a.dtype) for a, r in zip(shards, row_sharded)),
        in_specs=[ANY] * n, out_specs=tuple([ANY] * n),
        scratch_shapes=[pltpu.SemaphoreType.DMA((7 * n,)), pltpu.SemaphoreType.DMA((7 * n,)),
                        pltpu.SemaphoreType.DMA((n,))],
        name=name)(*shards)


def _rs_sibling(gs, name):
    n = len(gs)

    def body(*refs):
        g_refs, o_refs = refs[:n], refs[n:2 * n]
        send_sems, recv_sems = refs[2 * n:]
        mx, my, mc = lax.axis_index("x"), lax.axis_index("y"), lax.axis_index("c")
        copies = [pltpu.make_async_remote_copy(
            src_ref=g_refs[t].at[:, 2 * j + (1 - mc)], dst_ref=o_refs[t].at[j], send_sem=send_sems.at[j * n + t],
            recv_sem=recv_sems.at[j * n + t], device_id=(mx, my, 1 - mc), device_id_type=MESH)
            for t in range(n) for j in range(4)]
        for cp in copies:
            cp.start()
        for cp in copies:
            cp.wait_recv()
        for cp in copies:
            cp.wait_send()

    return pl.pallas_call(
        body, out_shape=tuple(S((4, g.shape[0]) + g.shape[2:], g.dtype) for g in gs), in_specs=[ANY] * n,
        out_specs=tuple([ANY] * n),
        scratch_shapes=[pltpu.SemaphoreType.DMA((4 * n,)), pltpu.SemaphoreType.DMA((4 * n,))], name=name)(*gs)


def _rs_chips(ps, name):
    n = len(ps)

    def body(*refs):
        p_refs, o_refs = refs[:n], refs[n:2 * n]
        send_sems, recv_sems = refs[2 * n:]
        mx, my, mc = lax.axis_index("x"), lax.axis_index("y"), lax.axis_index("c")
        chips = [(1 - mx, my), (mx, 1 - my), (1 - mx, 1 - my)]
        copies = [pltpu.make_async_remote_copy(
            src_ref=p_refs[t].at[2 * cx + cy], dst_ref=o_refs[t].at[j], send_sem=send_sems.at[j * n + t],
            recv_sem=recv_sems.at[j * n + t], device_id=(cx, cy, mc), device_id_type=MESH)
            for t in range(n) for j, (cx, cy) in enumerate(chips)]
        for cp in copies:
            cp.start()
        for cp in copies:
            cp.wait_recv()
        for cp in copies:
            cp.wait_send()

    return pl.pallas_call(
        body, out_shape=tuple(S((3,) + p.shape[1:], p.dtype) for p in ps), in_specs=[ANY] * n,
        out_specs=tuple([ANY] * n),
        scratch_shapes=[pltpu.SemaphoreType.DMA((3 * n,)), pltpu.SemaphoreType.DMA((3 * n,))], name=name)(*ps)


FLIPS = [(fx, fy, fc) for fx in (0, 1) for fy in (0, 1) for fc in (0, 1)][1:]
HBM = pl.BlockSpec(memory_space=pltpu.HBM)
SEM = pl.BlockSpec(memory_space=pltpu.SEMAPHORE)
EFFECT = pltpu.SideEffectType.DATAFLOW_SIDE_EFFECTING


def _hbm(a):
    return pltpu.with_memory_space_constraint(a, pltpu.HBM)


def _flips(gather):
    return [(0, 0, 0)] + FLIPS if gather else FLIPS


def _split_copies(gather, s_refs, l_refs, send_sems, recv_sems):
    n = len(s_refs)
    mx, my, mc = lax.axis_index("x"), lax.axis_index("y"), lax.axis_index("c")
    me = 4 * mx + 2 * my + mc
    copies = []
    for k, (fx, fy, fc) in enumerate(_flips(gather)):
        px, py, pc = (1 - mx if fx else mx), (1 - my if fy else my), (1 - mc if fc else mc)
        for t in range(n):
            if gather:
                src = s_refs[t]
                r = src.shape[0]
                dst = l_refs[t].at[pl.ds(pl.multiple_of(me * r, 16), r), :]
            else:
                src = s_refs[t].at[:, 4 * px + 2 * py + pc]
                dst = l_refs[t].at[k]
            copies.append(pltpu.make_async_remote_copy(
                src_ref=src, dst_ref=dst, send_sem=send_sems.at[k * n + t], recv_sem=recv_sems.at[k * n + t],
                device_id=(px, py, pc), device_id_type=MESH))
    return copies


def _split_start(gather, srcs, lands, after, name):
    n = len(srcs)
    n_sem = len(_flips(gather)) * n

    def body(*refs):
        s_refs, l_refs = refs[:n], refs[n:2 * n]
        send_sems, recv_sems = refs[2 * n + 1], refs[2 * n + 2]
        token = refs[-1]
        for cp in _split_copies(gather, s_refs, l_refs, send_sems, recv_sems):
            cp.start()
        token[...] = jnp.zeros_like(token)

    outs = pl.pallas_call(
        body, name=name,
        out_shape=(pltpu.SemaphoreType.DMA((n_sem,)), pltpu.SemaphoreType.DMA((n_sem,)))
        + tuple(pltpu.HBM(a.shape, a.dtype) for a in lands) + (S((8, LANES), F32),),
        in_specs=[HBM] * (2 * n) + [ANY],
        out_specs=(SEM, SEM) + (HBM,) * n + (pl.BlockSpec(memory_space=pltpu.VMEM),),
        input_output_aliases={n + i: 2 + i for i in range(n)},
        compiler_params=pltpu.CompilerParams(has_side_effects=EFFECT),
    )(*[_hbm(a) for a in srcs], *[_hbm(a) for a in lands], after)
    return outs[0], outs[1], list(srcs), list(outs[2:2 + n]), outs[-1]


def _split_wait(gather, send_sems, recv_sems, srcs, lands, after, name):
    n = len(srcs)

    def body(*refs):
        s_refs, l_refs = refs[:n], refs[n:2 * n]
        ssem, rsem = refs[2 * n], refs[2 * n + 1]
        copies = _split_copies(gather, s_refs, l_refs, ssem, rsem)
        for cp in copies:
            cp.wait_send()
        for cp in copies:
            cp.wait_recv()

    outs = pl.pallas_call(
        body, name=name, out_shape=tuple(pltpu.HBM(a.shape, a.dtype) for a in lands),
        in_specs=[HBM] * (2 * n) + [SEM, SEM, ANY], out_specs=(HBM,) * n,
        input_output_aliases={n + i: i for i in range(n)},
        compiler_params=pltpu.CompilerParams(has_side_effects=EFFECT),
    )(*[_hbm(a) for a in srcs], *lands, send_sems, recv_sems, after)
    return list(outs)


def _rows_tile(b):
    return _tile(b, (512, 256, 128)) if b > 512 else b


def _pair_sum(g, got, name):
    A, _, B, C = g.shape
    tb = _rows_tile(B)
    core = lax.axis_index("c").astype(jnp.int32).reshape(1)

    def body(c_ref, g_ref, r_ref, o_ref):
        o_ref[...] = (g_ref[...].astype(F32) + r_ref[...].astype(F32)).astype(o_ref.dtype)

    return pl.pallas_call(
        body,
        grid_spec=pltpu.PrefetchScalarGridSpec(
            num_scalar_prefetch=1, grid=(4, A, B // tb),
            in_specs=[pl.BlockSpec((1, 1, tb, C), lambda j, a, i, c_ref: (a, 2 * j + c_ref[0], i, 0)),
                      pl.BlockSpec((1, 1, tb, C), lambda j, a, i, c_ref: (j, a, i, 0))],
            out_specs=pl.BlockSpec((1, 1, tb, C), lambda j, a, i, c_ref: (j, a, i, 0))),
        out_shape=S((4, A, B, C), g.dtype), compiler_params=_cparams("parallel", "parallel", "parallel"),
        name=name)(core, g, got)


def _adamw_math(w, g, m, v):
    m = ADAM_B1 * m + (1.0 - ADAM_B1) * g
    v = ADAM_B2 * v + (1.0 - ADAM_B2) * (g * g)
    m_hat = m / (1.0 - ADAM_B1 ** ADAM_STEP)
    v_hat = v / (1.0 - ADAM_B2 ** ADAM_STEP)
    delta = -ADAM_LR * (m_hat / (jnp.sqrt(v_hat) + ADAM_EPS) + ADAM_WD * w)
    return delta, m, v


def _adamw_sharded(p, got, w, m, v, name):
    A, B, C = w.shape
    tb = _rows_tile(B)
    chip = (2 * lax.axis_index("x") + lax.axis_index("y")).astype(jnp.int32).reshape(1)

    def body(c_ref, p_ref, got_ref, w_ref, m_ref, v_ref, g_out, d_out, m_out, v_out):
        g = p_ref[0].astype(F32)
        for j in range(3):
            g = g + got_ref[j].astype(F32)
        d, mn, vn = _adamw_math(w_ref[...], g, m_ref[...], v_ref[...])
        g_out[...] = g
        d_out[...] = d
        m_out[...] = mn
        v_out[...] = vn

    blk = pl.BlockSpec((1, tb, C), lambda a, i, c_ref: (a, i, 0))
    return pl.pallas_call(
        body,
        grid_spec=pltpu.PrefetchScalarGridSpec(
            num_scalar_prefetch=1, grid=(A, B // tb),
            in_specs=[pl.BlockSpec((1, 1, tb, C), lambda a, i, c_ref: (c_ref[0], a, i, 0)),
                      pl.BlockSpec((3, 1, tb, C), lambda a, i, c_ref: (0, a, i, 0)), blk, blk, blk],
            out_specs=(blk, blk, blk, blk)),
        out_shape=(S((A, B, C), F32),) * 4, compiler_params=_cparams("parallel", "parallel"), name=name)(
            chip, p, got, w, m, v)


def _adamw_layers(owns, gots, w, m, v, name):
    L, B, C = w.shape
    per_row = 2 * L * len(FLIPS) * C * owns[0].dtype.itemsize
    tb = max([t for t in range(16, B + 1, 16) if B % t == 0 and (t * per_row <= 16 * 1024 * 1024 or t == 16)] or [B])
    me = (4 * lax.axis_index("x") + 2 * lax.axis_index("y") + lax.axis_index("c")).astype(jnp.int32).reshape(1)

    def body(me_ref, *refs):
        own_refs, got_refs = refs[:L], refs[L:2 * L]
        w_ref, m_ref, v_ref = refs[2 * L:2 * L + 3]
        g_out, d_out, m_out, v_out = refs[2 * L + 3:]
        layer = pl.program_id(0)
        for kk in range(L):
            @pl.when(layer == kk)
            def _():
                g = own_refs[kk][0].astype(F32)
                for s in range(len(FLIPS)):
                    g = g + got_refs[kk][s].astype(F32)
                d, mn, vn = _adamw_math(w_ref[...], g, m_ref[...], v_ref[...])
                g_out[...] = g
                d_out[...] = d
                m_out[...] = mn
                v_out[...] = vn

    def row(kk, layer, i):
        return jnp.where(layer == kk, i, 0)

    blk = pl.BlockSpec((1, tb, C), lambda layer, i, me_ref: (layer, i, 0))
    own_specs = [pl.BlockSpec((1, 1, tb, C), lambda layer, i, me_ref, kk=kk: (0, me_ref[0], row(kk, layer, i), 0))
                 for kk in range(L)]
    got_specs = [pl.BlockSpec((len(FLIPS), 1, tb, C), lambda layer, i, me_ref, kk=kk: (0, 0, row(kk, layer, i), 0))
                 for kk in range(L)]
    return pl.pallas_call(
        body,
        grid_spec=pltpu.PrefetchScalarGridSpec(
            num_scalar_prefetch=1, grid=(L, B // tb), in_specs=own_specs + got_specs + [blk, blk, blk],
            out_specs=(blk, blk, blk, blk)),
        out_shape=(S((L, B, C), F32),) * 4, compiler_params=_cparams("arbitrary", "arbitrary"), name=name)(
            me, *owns, *gots, w, m, v)


def _adamw_replicated(parts, w, m, v, name):
    R, C = w.shape
    rb = _tile(R, (512, 256, 128, 64, 32, 16, 8))

    def body(p_ref, w_ref, m_ref, v_ref, g_out, d_out, m_out, v_out):
        g = p_ref[0]
        for j in range(1, N_DEV):
            g = g + p_ref[j]
        d, mn, vn = _adamw_math(w_ref[...], g, m_ref[...], v_ref[...])
        g_out[...] = g
        d_out[...] = d
        m_out[...] = mn
        v_out[...] = vn

    blk = pl.BlockSpec((rb, C), lambda i: (i, 0))
    return pl.pallas_call(
        body, grid=(R // rb,), in_specs=[pl.BlockSpec((N_DEV, rb, C), lambda i: (0, i, 0)), blk, blk, blk],
        out_specs=(blk, blk, blk, blk), out_shape=(S((R, C), F32),) * 4, compiler_params=_cparams("parallel"),
        name=name)(parts, w, m, v)


def _pack(arrs, rows_mult, dtype):
    flat = jnp.concatenate([a.reshape(-1).astype(dtype) for a in arrs])
    n = flat.shape[0]
    per = rows_mult * LANES
    tot = -(-n // per) * per
    return jnp.pad(flat, (0, tot - n)).reshape(tot // LANES, LANES)


def _unpack(blob, shapes):
    flat = blob.reshape(-1)
    out, off = [], 0
    for shp in shapes:
        n = int(np.prod(shp))
        out.append(flat[off:off + n].reshape(shp))
        off += n
    return out


def _small_to_natural(g8):
    t = jnp.moveaxis(g8, 0, -2)
    return t.reshape(t.shape[:-2] + (N_DEV * t.shape[-1],))


def _small_to_cols(g):
    t = g.reshape(g.shape[:-1] + (N_DEV, g.shape[-1] // N_DEV))
    return jnp.moveaxis(t, -2, 0)


def _block_diag(w):
    nb, bs, _ = w.shape
    eye = jnp.eye(nb, dtype=w.dtype)
    return (eye[:, None, :, None] * w[:, :, None, :]).reshape(nb * bs, nb * bs)


def _diag_blocks(d, nb, bs):
    d4 = d.reshape(nb, bs, nb, bs)
    return jnp.stack([d4[i, :, i, :] for i in range(nb)])


def kernel(x, mem, g_mix_pre, g_mix_post, g_ffn_pre, g_ffn_post, g_mem, w_mem_kv, w_mix_out, w_ffn_up, w_ffn_conv, b_ffn_conv, w_ffn_down, w_in_a, w_conv_a, b_conv_a, w_rg_r, b_rg_r, w_rg_i, b_rg_i, lru_lambda, w_in_b, sinks_b, g_kv, w_kv, loss_target, m_g_mix_pre, m_g_mix_post, m_g_ffn_pre, m_g_ffn_post, m_g_mem, m_w_mem_kv, m_w_mix_out, m_w_ffn_up, m_w_ffn_conv, m_b_ffn_conv, m_w_ffn_down, m_w_in_a, m_w_conv_a, m_b_conv_a, m_w_rg_r, m_b_rg_r, m_w_rg_i, m_b_rg_i, m_lru_lambda, m_w_in_b, m_sinks_b, m_g_kv, m_w_kv, v_g_mix_pre, v_g_mix_post, v_g_ffn_pre, v_g_ffn_post, v_g_mem, v_w_mem_kv, v_w_mix_out, v_w_ffn_up, v_w_ffn_conv, v_b_ffn_conv, v_w_ffn_down, v_w_in_a, v_w_conv_a, v_b_conv_a, v_w_rg_r, v_b_rg_r, v_w_rg_i, v_b_rg_i, v_lru_lambda, v_w_in_b, v_sinks_b, v_g_kv, v_w_kv):
    w_loc = dict(g_mix_pre=g_mix_pre, g_mix_post=g_mix_post, g_ffn_pre=g_ffn_pre, g_ffn_post=g_ffn_post, g_mem=g_mem,
                 w_mem_kv=w_mem_kv, w_mix_out=w_mix_out, w_ffn_up=w_ffn_up, w_ffn_conv=w_ffn_conv,
                 b_ffn_conv=b_ffn_conv, w_ffn_down=w_ffn_down, w_in_a=w_in_a, w_conv_a=w_conv_a, b_conv_a=b_conv_a,
                 w_rg_r=w_rg_r, b_rg_r=b_rg_r, w_rg_i=w_rg_i, b_rg_i=b_rg_i, lru_lambda=lru_lambda, w_in_b=w_in_b,
                 sinks_b=sinks_b, g_kv=g_kv, w_kv=w_kv)
    m_loc = dict(g_mix_pre=m_g_mix_pre, g_mix_post=m_g_mix_post, g_ffn_pre=m_g_ffn_pre, g_ffn_post=m_g_ffn_post,
                 g_mem=m_g_mem, w_mem_kv=m_w_mem_kv, w_mix_out=m_w_mix_out, w_ffn_up=m_w_ffn_up,
                 w_ffn_conv=m_w_ffn_conv, b_ffn_conv=m_b_ffn_conv, w_ffn_down=m_w_ffn_down, w_in_a=m_w_in_a,
                 w_conv_a=m_w_conv_a, b_conv_a=m_b_conv_a, w_rg_r=m_w_rg_r, b_rg_r=m_b_rg_r, w_rg_i=m_w_rg_i,
                 b_rg_i=m_b_rg_i, lru_lambda=m_lru_lambda, w_in_b=m_w_in_b, sinks_b=m_sinks_b, g_kv=m_g_kv,
                 w_kv=m_w_kv)
    v_loc = dict(g_mix_pre=v_g_mix_pre, g_mix_post=v_g_mix_post, g_ffn_pre=v_g_ffn_pre, g_ffn_post=v_g_ffn_post,
                 g_mem=v_g_mem, w_mem_kv=v_w_mem_kv, w_mix_out=v_w_mix_out, w_ffn_up=v_w_ffn_up,
                 w_ffn_conv=v_w_ffn_conv, b_ffn_conv=v_b_ffn_conv, w_ffn_down=v_w_ffn_down, w_in_a=v_w_in_a,
                 w_conv_a=v_w_conv_a, b_conv_a=v_b_conv_a, w_rg_r=v_w_rg_r, b_rg_r=v_b_rg_r, w_rg_i=v_w_rg_i,
                 b_rg_i=v_b_rg_i, lru_lambda=v_lru_lambda, w_in_b=v_w_in_b, sinks_b=v_sinks_b, g_kv=v_g_kv,
                 w_kv=v_w_kv)

    Bl, T, D = x.shape
    Ml = mem.shape[1]
    N = Bl * T
    depth = g_mix_pre.shape[0]
    n_a = w_in_a.shape[0]
    F = w_ffn_down.shape[1] * N_DEV
    def as_rows(n, a):
        return jnp.swapaxes(a, -1, -2) if n in TRANSPOSED else a

    def layer_keys(l):
        keys = [("w_mem_kv", l), ("w_mix_out", l), ("w_ffn_up", l), ("w_ffn_down", l)]
        keys.append(("w_in_a", l) if l < n_a else ("w_in_b", l - n_a))
        if l == n_a:
            keys.append(("w_kv", None))
        return keys

    def shard_of(key):
        n, i = key
        return as_rows(n, w_loc[n] if i is None else w_loc[n][i]).astype(BF16)

    W = {}
    ffn_names = ("w_ffn_up", "w_ffn_down")
    keys0 = [kk for kk in layer_keys(0) if kk[0] not in ffn_names]
    got0 = _ag_weights([shard_of(kk) for kk in keys0] + [w_loc[n] for n in SMALL_SHARDED],
                       [True] * len(keys0) + [False] * len(SMALL_SHARDED), name="ag_weights_0")
    W.update(zip(keys0, got0))
    for n, a in zip(SMALL_SHARDED, got0[len(keys0):]):
        W[n] = _small_to_natural(a)

    def gather_start(keys, after, tag):
        shards = [shard_of(kk) for kk in keys]
        lands = [lax.empty((N_DEV * s.shape[0],) + s.shape[1:], s.dtype) for s in shards]
        return (keys, tag) + _split_start(True, shards, lands, after, name=f"ag_start_{tag}")

    def gather_wait(pending, after):
        keys, tag, ssem, rsem, srcs, lands, _ = pending
        W.update(zip(keys, _split_wait(True, ssem, rsem, srcs, lands, after, name=f"ag_wait_{tag}")))

    pending_ffn0 = gather_start([kk for kk in layer_keys(0) if kk[0] in ffn_names], got0[0], "ffn_0")

    nblk, bsz = w_rg_r.shape[1], w_rg_r.shape[2]
    wbd = [jnp.concatenate([_block_diag(w_rg_r[j]), _block_diag(w_rg_i[j])], axis=1).astype(BF16) for j in range(n_a)]

    def vec(a):
        return a.reshape(1, -1)

    x2 = x.reshape(N, D)
    mem2 = mem.reshape(Bl * Ml, D)
    saved = []
    kvn = kv3 = x_kv = None
    xs = x2
    for l in range(depth):
        sv = {"x0": xs}
        g_pre = vec(g_mix_pre[l])
        if l + 1 < depth:
            pending = gather_start(layer_keys(l + 1), pending_ffn0[-1] if l == 0 else W[("w_mem_kv", l)], l + 1)
            g_pre = g_pre + pending[-1][0, 0]
        h1 = _rms_fwd(xs, g_pre, BF16, name=f"rms_mixpre_{l}")
        memn = _rms_fwd(mem2, vec(g_mem[l]), BF16, name=f"rms_mem_{l}")
        mkv3 = _mm(memn, W[("w_mem_kv", l)], name=f"mm_memkv_{l}").reshape(Bl, Ml, 2 * MEM_W)
        if l < n_a:
            j = l
            proj = _mm(h1, W[("w_in_a", j)], tb=True, name=f"mm_in_{l}")
            proj3 = proj.reshape(Bl, T, -1)
            xc3 = _conv_fwd_call(proj3, MIX_W, MIX_W, W["w_conv_a"][j], vec(W["b_conv_a"][j]), name=f"conv_a_{l}")
            gates3 = _mm(xc3.reshape(N, MIX_W), wbd[j], name=f"mm_gates_{l}").reshape(Bl, T, 2 * MIX_W)
            y_main3, hs3 = _rglru_fwd(xc3, gates3, proj3, vec(b_rg_r[j]), vec(b_rg_i[j]), vec(W["lru_lambda"][j]),
                                      name=f"rglru_fwd_{l}")
            q_off = 2 * MIX_W
            sv.update(xc3=xc3, gates3=gates3, hs3=hs3)
        else:
            j = l - n_a
            if l == n_a:
                x_kv = xs
                kvn = _rms_fwd(xs, vec(g_kv), BF16, name="rms_kv")
                kv3 = _mm(kvn, W[("w_kv", None)], name="mm_kv").reshape(Bl, T, 2 * MEM_W)
            proj = _mm(h1, W[("w_in_b", j)], name=f"mm_in_{l}")
            proj3 = proj.reshape(Bl, T, -1)
            y_main3 = _swa_fwd(proj3, kv3, sinks_b[j], name=f"swa_fwd_{l}")
            q_off = MIX_W
        y_mem3 = _mem_attn_fwd(proj3, q_off, mkv3, name=f"memattn_fwd_{l}")
        y_main = y_main3.reshape(N, MIX_W)
        y_mem = y_mem3.reshape(N, MEM_W)
        y = _mm(y_main, W[("w_mix_out", l)], n=D, k=MIX_W, name=f"mm_mixout_main_{l}")
        y = _mm(y_mem, W[("w_mix_out", l)], n=D, k=MEM_W, b_off=(MIX_W, 0), add=y, name=f"mm_mixout_mem_{l}")
        x1 = _rms_fwd(y, vec(g_mix_post[l]), F32, res=xs, name=f"rms_mixpost_{l}")
        h2 = _rms_fwd(x1, vec(g_ffn_pre[l]), BF16, name=f"rms_ffnpre_{l}")
        if l == 0:
            gather_wait(pending_ffn0, h2)
        ug = _mm(h2, W[("w_ffn_up", l)], tb=True, n=F, name=f"mm_up_g_{l}")
        uv = _mm(h2, W[("w_ffn_up", l)], tb=True, n=F, b_off=(F, 0), name=f"mm_up_v_{l}")
        ug3, uv3 = ug.reshape(Bl, T, F), uv.reshape(Bl, T, F)
        act3 = _ffn_mid_fwd(ug3, uv3, W["w_ffn_conv"][l], vec(b_ffn_conv[l]), name=f"ffn_mid_fwd_{l}")
        act = act3.reshape(N, F)
        f = _mm(act, W[("w_ffn_down", l)], name=f"mm_down_{l}")
        x_next = _rms_fwd(f, vec(g_ffn_post[l]), F32, res=x1, name=f"rms_ffnpost_{l}")
        if l + 1 < depth:
            gather_wait(pending, x_next)
        sv.update(h1=h1, memn=memn, mkv3=mkv3, proj3=proj3, q_off=q_off, y_main=y_main, y_mem=y_mem, y=y, x1=x1,
                  h2=h2, ug3=ug3, uv3=uv3, act=act, f=f)
        saved.append(sv)
        xs = x_next

    dxs, loss_vec = _loss_bwd(xs, loss_target.reshape(N, D))
    loss = lax.psum(jnp.sum(loss_vec), ("x", "y", "c"))

    G = {n: [None] * w_loc[n].shape[0] for n in REPL + SMALL_SHARDED if n != "g_kv"}
    GW = {}

    def dw(key, off, a, b_, nm):
        GW[key] = _mm(a, b_, ta=True, out_dtype=BF16, into=(GW.get(key), (1,) + W[key].shape, 0, off), name=nm)

    def grad_blocks(key):
        g = GW[key]
        return g.reshape(1, N_DEV, g.shape[1] // N_DEV, g.shape[2])

    reduces = []

    def reduce_start(keys, after, tag):
        srcs = [grad_blocks(kk) for kk in keys]
        lands = [lax.empty((len(FLIPS),) + s.shape[:1] + s.shape[2:], s.dtype) for s in srcs]
        started = _split_start(False, srcs, lands, after, name=f"rs_start_{tag}")
        reduces.append((keys, tag) + started)
        return started[-1]

    kv_parts = []
    for l in reversed(range(depth)):
        sv = saved[l]
        proj3 = sv["proj3"]
        df, dg = _rms_bwd(sv["f"], vec(g_ffn_post[l]), dxs, out_dtype=BF16, name=f"rmsb_ffnpost_{l}")
        G["g_ffn_post"][l] = dg[0]
        dact = _mm(df, W[("w_ffn_down", l)], tb=True, name=f"mmb_down_dx_{l}")
        dw(("w_ffn_down", l), (0, 0), sv["act"], df, f"mmb_down_dw_{l}")
        dug3, duv3, dwg, dwv, dbg, dbv = _ffn_mid_bwd(sv["ug3"], sv["uv3"], dact.reshape(Bl, T, F),
                                                      W["w_ffn_conv"][l], vec(b_ffn_conv[l]), name=f"ffn_mid_bwd_{l}")
        G["w_ffn_conv"][l] = jnp.concatenate([dwg, dwv], axis=1)
        G["b_ffn_conv"][l] = jnp.concatenate([dbg, dbv], axis=1)[0]
        dug, duv = dug3.reshape(N, F), duv3.reshape(N, F)
        dw(("w_ffn_up", l), (0, 0), dug, sv["h2"], f"mmb_up_dw_g_{l}")
        dw(("w_ffn_up", l), (F, 0), duv, sv["h2"], f"mmb_up_dw_v_{l}")
        tok = reduce_start([("w_ffn_down", l), ("w_ffn_up", l)], dug, f"ffn_{l}")
        dh2 = _mm(dug, W[("w_ffn_up", l)], n=D, k=F, after=tok, name=f"mmb_up_dx_g_{l}")
        dh2 = _mm(duv, W[("w_ffn_up", l)], n=D, k=F, b_off=(F, 0), add=dh2, name=f"mmb_up_dx_v_{l}")
        dx1, dg = _rms_bwd(sv["x1"], vec(g_ffn_pre[l]), dh2, add=dxs, name=f"rmsb_ffnpre_{l}")
        G["g_ffn_pre"][l] = dg[0]
        dy, dg = _rms_bwd(sv["y"], vec(g_mix_post[l]), dx1, out_dtype=BF16, name=f"rmsb_mixpost_{l}")
        G["g_mix_post"][l] = dg[0]
        dy_main = _mm(dy, W[("w_mix_out", l)], tb=True, n=MIX_W, k=D, name=f"mmb_mixout_dmain_{l}")
        dy_mem = _mm(dy, W[("w_mix_out", l)], tb=True, n=MEM_W, k=D, b_off=(MIX_W, 0),
                     name=f"mmb_mixout_dmem_{l}")
        dw(("w_mix_out", l), (0, 0), sv["y_main"], dy, f"mmb_mixout_dw_main_{l}")
        dw(("w_mix_out", l), (MIX_W, 0), sv["y_mem"], dy, f"mmb_mixout_dw_mem_{l}")
        dq_mem3, dmkv3 = _mem_attn_bwd(proj3, sv["q_off"], sv["mkv3"], dy_mem.reshape(Bl, T, MEM_W),
                                       name=f"memattn_bwd_{l}")
        dq_mem = dq_mem3.reshape(N, MEM_W)
        dmkv = dmkv3.reshape(Bl * Ml, 2 * MEM_W)
        dw(("w_mem_kv", l), (0, 0), sv["memn"], dmkv, f"mmb_memkv_dw_{l}")
        dmemn = _mm(dmkv, W[("w_mem_kv", l)], tb=True, name=f"mmb_memkv_dx_{l}")
        _, dg = _rms_bwd(mem2, vec(g_mem[l]), dmemn, name=f"rmsb_mem_{l}")
        G["g_mem"][l] = dg[0]
        dy_main3 = dy_main.reshape(Bl, T, MIX_W)
        if l < n_a:
            j = l
            dxc3, drp3, dip3, dugate3, dbr, dbi, dlam = _rglru_bwd(
                dy_main3, sv["xc3"], sv["gates3"], proj3, sv["hs3"], vec(b_rg_r[j]), vec(b_rg_i[j]),
                vec(W["lru_lambda"][j]), name=f"rglru_bwd_{l}")
            G["b_rg_r"][j] = dbr.reshape(nblk, bsz)
            G["b_rg_i"][j] = dbi.reshape(nblk, bsz)
            G["lru_lambda"][j] = dlam[0]
            drp, dip = drp3.reshape(N, MIX_W), dip3.reshape(N, MIX_W)
            xc2 = sv["xc3"].reshape(N, MIX_W)
            G["w_rg_r"][j] = _diag_blocks(_mm(xc2, drp, ta=True, name=f"mmb_gates_dw_r_{l}"), nblk, bsz)
            G["w_rg_i"][j] = _diag_blocks(_mm(xc2, dip, ta=True, name=f"mmb_gates_dw_i_{l}"), nblk, bsz)
            dxc = _mm(drp, wbd[j], tb=True, n=MIX_W, k=MIX_W, add=dxc3.reshape(N, MIX_W), name=f"mmb_gates_dx_r_{l}")
            dxc = _mm(dip, wbd[j], tb=True, n=MIX_W, k=MIX_W, b_off=(0, MIX_W), add=dxc, name=f"mmb_gates_dx_i_{l}")
            dux3, dwc, dbc = _conv_bwd_call(dxc.reshape(Bl, T, MIX_W), proj3, MIX_W, MIX_W, W["w_conv_a"][j],
                                            name=f"conv_a_bwd_{l}")
            G["w_conv_a"][j] = dwc
            G["b_conv_a"][j] = dbc[0]
            pieces = [(dugate3.reshape(N, MIX_W), 0), (dux3.reshape(N, MIX_W), MIX_W), (dq_mem, 2 * MIX_W)]
            in_key = ("w_in_a", j)
        else:
            j = l - n_a
            dq3, dkc, dkp, dsk = _swa_bwd(proj3, kv3, sinks_b[j], dy_main3, name=f"swa_bwd_{l}")
            kv_parts.append((dkc, dkp))
            G["sinks_b"][j] = dsk[0, :SWA_HEADS]
            pieces = [(dq3.reshape(N, MIX_W), 0), (dq_mem, MIX_W)]
            in_key = ("w_in_b", j)
        in_t = in_key[0] in TRANSPOSED
        for pi, (piece, off) in enumerate(pieces):
            if in_t:
                dw(in_key, (off, 0), piece, sv["h1"], f"mmb_in_dw_{pi}_{l}")
            else:
                dw(in_key, (0, off), sv["h1"], piece, f"mmb_in_dw_{pi}_{l}")
        tok = reduce_start([("w_mix_out", l), ("w_mem_kv", l), in_key], dy, f"mix_{l}")
        dh1 = None
        for pi, (piece, off) in enumerate(pieces):
            dh1 = _mm(piece, W[in_key], tb=not in_t, n=D, k=piece.shape[1], b_off=(off, 0) if in_t else (0, off),
                      add=dh1, after=tok if pi == 0 else None, name=f"mmb_in_dx_{pi}_{l}")
        dxs, dg = _rms_bwd(sv["x0"], vec(g_mix_pre[l]), dh1, add=dx1, name=f"rmsb_mixpre_{l}")
        G["g_mix_pre"][l] = dg[0]
        if l == n_a:
            dkv = _kv_grad_combine(kv_parts, name="kv_grad_combine").reshape(N, 2 * MEM_W)
            dw(("w_kv", None), (0, 0), kvn, dkv, "mmb_kv_dw")
            tok = reduce_start([("w_kv", None)], dkv, "kv")
            dkvn = _mm(dkv, W[("w_kv", None)], tb=True, after=tok, name="mmb_kv_dx")
            dxs, dg = _rms_bwd(x_kv, vec(g_kv), dkvn, add=dxs, name="rmsb_kv")
            G["g_kv"] = dg[0]
    grad_x = dxs.reshape(Bl, T, D)
    Gf = {n: (jnp.stack(g) if isinstance(g, list) else g) for n, g in G.items()}

    small4 = []
    for n in SMALL_SHARDED:
        t = _small_to_cols(Gf[n]).astype(BF16)
        small4.append(t.reshape(1, N_DEV, -1, t.shape[-1]))
    small_lands = [lax.empty((len(FLIPS),) + s.shape[:1] + s.shape[2:], s.dtype) for s in small4]
    small_started = _split_start(False, small4, small_lands, dxs, name="rs_start_small")
    r_blob = _pack([Gf[n].astype(F32) for n in REPL], REPL_ROWS, F32)
    r_started = _split_start(True, [r_blob], [lax.empty((N_DEV * r_blob.shape[0], LANES), F32)], small_started[-1],
                             name="ag_start_repl_grads")
    parts = {}
    for keys, tag, ssem, rsem, srcs, lands, _ in reduces:
        for kk, s, g7 in zip(keys, srcs, _split_wait(False, ssem, rsem, srcs, lands, r_started[-1],
                                                     name=f"rs_wait_{tag}")):
            parts[kk] = (s, g7)

    res = [{} for _ in range(4)]
    for n, _ in SHARDED:
        if n in SMALL_SHARDED:
            continue
        idx = [None] if w_loc[n].ndim == 2 else list(range(w_loc[n].shape[0]))
        wmv = [as_rows(n, a[n]) for a in (w_loc, m_loc, v_loc)]
        shp3 = (len(idx),) + wmv[0].shape[-2:]
        outs = _adamw_layers([parts[(n, i)][0] for i in idx], [parts[(n, i)][1] for i in idx],
                             *[a.reshape(shp3) for a in wmv], name=f"adamw_{n}")
        for k in range(4):
            res[k][n] = as_rows(n, outs[k].reshape(wmv[0].shape))
    last = res[0]["w_kv"]
    small_got = _split_wait(False, *small_started[:4], last, name="rs_wait_small")
    for n, own, g7 in zip(SMALL_SHARDED, small4, small_got):
        shp3 = own.shape[:1] + own.shape[2:]
        outs = _adamw_layers([own], [g7], w_loc[n].reshape(shp3), m_loc[n].reshape(shp3), v_loc[n].reshape(shp3),
                             name=f"adamw_{n}")
        for k in range(4):
            res[k][n] = outs[k].reshape(w_loc[n].shape)
    r_parts = _split_wait(True, *r_started[:4], last, name="ag_wait_repl_grads")[0].reshape(N_DEV, -1, LANES)
    outs_rp = _adamw_replicated(r_parts, _pack([w_loc[n] for n in REPL], REPL_ROWS, F32),
                                _pack([m_loc[n] for n in REPL], REPL_ROWS, F32),
                                _pack([v_loc[n] for n in REPL], REPL_ROWS, F32),
                                name="adamw_replicated")
    rp_shapes = [w_loc[n].shape for n in REPL]
    for k in range(4):
        res[k].update(zip(REPL, _unpack(outs_rp[k], rp_shapes)))
    out = [loss, grad_x]
    for k in range(4):
        out += [res[k][n] for n in WEIGHTS]
    return tuple(out)
```

```python
import functools
import math

import numpy as np
import jax
import jax.numpy as jnp
from jax import lax
from jax.experimental import pallas as pl
from jax.experimental.pallas import tpu as pltpu

F32 = jnp.float32
BF16 = jnp.bfloat16
S = jax.ShapeDtypeStruct
MESH = pl.DeviceIdType.MESH
ANY = pl.BlockSpec(memory_space=pl.ANY)

HEAD = 64
MEM_HEADS = 4
MEM_W = MEM_HEADS * HEAD
SWA_HEADS = 12
SWA_GROUP = 3
MIX_W = SWA_HEADS * HEAD
WIN = 128
LRU_C = 8.0
EPS = 1e-6
ADAM_LR, ADAM_B1, ADAM_B2, ADAM_EPS, ADAM_WD, ADAM_STEP = 0.001, 0.9, 0.999, 1e-08, 0.01, 10
GELU_C0 = math.sqrt(2.0 / math.pi)
GELU_C1 = 0.044715
N_DEV = 8
LANES = 128
CT = 128
VMEM_LIMIT = 48 * 1024 * 1024
MM_VMEM_BUDGET = 36 * 1024 * 1024
REPL_ROWS = 256

SHARDED = (("w_mem_kv", 1), ("w_mix_out", 1), ("w_ffn_up", 2), ("w_ffn_conv", 2), ("w_ffn_down", 1), ("w_in_a", 2),
           ("w_conv_a", 2), ("b_conv_a", 1), ("lru_lambda", 1), ("w_in_b", 1), ("w_kv", 0))
SMALL_SHARDED = ("w_ffn_conv", "w_conv_a", "b_conv_a", "lru_lambda")
TRANSPOSED = ("w_ffn_up", "w_in_a")
REPL = ("g_mix_pre", "g_mix_post", "g_ffn_pre", "g_ffn_post", "g_mem", "b_ffn_conv", "w_rg_r", "b_rg_r", "w_rg_i",
        "b_rg_i", "sinks_b", "g_kv")
WEIGHTS = ("g_mix_pre", "g_mix_post", "g_ffn_pre", "g_ffn_post", "g_mem", "w_mem_kv", "w_mix_out", "w_ffn_up",
           "w_ffn_conv", "b_ffn_conv", "w_ffn_down", "w_in_a", "w_conv_a", "b_conv_a", "w_rg_r", "b_rg_r", "w_rg_i",
           "b_rg_i", "lru_lambda", "w_in_b", "sinks_b", "g_kv", "w_kv")


def _alibi_slopes(n):
    def pow2(m):
        start = 2.0 ** (-8.0 / m)
        return [start ** (i + 1) for i in range(m)]
    c = 2 ** int(math.floor(math.log2(n)))
    s = pow2(c)
    if c != n:
        s = s + pow2(2 * c)[0::2][: n - c]
    return [float(v) for v in np.asarray(s, dtype=np.float32)]


SLOPES = _alibi_slopes(SWA_HEADS)


def _tile(n, cands):
    for c in cands:
        if n % c == 0:
            return c
    return n


def _cparams(*sem):
    return pltpu.CompilerParams(dimension_semantics=sem, vmem_limit_bytes=VMEM_LIMIT)


def _mm_tiles(M, N, K, a_bytes, b_bytes, o_bytes, add_bytes, offsets):
    m_off, n_offs, k_off = offsets
    tms = [c for c in (1024, 512, 256, 128) if M % c == 0 and m_off % c == 0] or [M]
    tns = [c for c in (1408, 1024, 896, 768, 512, 384, 256, 128)
           if N % c == 0 and all(o % c == 0 for o in n_offs)] or [N]
    tks = [c for c in (K, 2048, 1408, 1024, 512, 256, 128) if c <= K and K % c == 0 and k_off % c == 0]
    best = None
    for tk in tks:
        fits = []
        for tm in tms:
            for tn in tns:
                need = 2 * (tm * tk * a_bytes + tk * tn * b_bytes + tm * tn * (o_bytes + add_bytes))
                need += tm * tn * 4 * (2 if tk < K else 1)
                need += (tm * tk * 2 if a_bytes != 2 else 0) + (tk * tn * 2 if b_bytes != 2 else 0)
                if need <= MM_VMEM_BUDGET:
                    fits.append((tm * tn, min(tm, 512), tm, tn))
        if fits:
            _, _, tm, tn = max(fits)
            best = (tm, tn, tk)
            break
    assert best is not None, (M, N, K)
    return best


def _mm(a, b, *, ta=False, tb=False, n=None, k=None, b_off=(0, 0), out_dtype=F32, add=None, into=None, after=None,
        name="mm"):
    if ta:
        K, M = a.shape
    else:
        M, K = a.shape
    if tb:
        N = b.shape[-2] if n is None else n
    else:
        N = b.shape[-1] if n is None else n
    assert k is None or k == K
    ro, co = b_off
    n_off, k_off = (ro, co) if tb else (co, ro)
    oro, oco = (0, 0) if into is None else into[3]
    tm, tn, tk = _mm_tiles(M, N, K, a.dtype.itemsize, b.dtype.itemsize, jnp.dtype(out_dtype).itemsize,
                           0 if add is None else add.dtype.itemsize, (oro, (n_off, oco), k_off))
    nk = K // tk
    if tb:
        b_spec = pl.BlockSpec((tn, tk), lambda i, j, kk: (j + ro // tn, kk + co // tk))
        b_dims = (1,)
    else:
        b_spec = pl.BlockSpec((tk, tn), lambda i, j, kk: (kk + ro // tk, j + co // tn))
        b_dims = (0,)
    if ta:
        a_spec = pl.BlockSpec((tk, tm), lambda i, j, kk: (kk, i))
        a_dims = (0,)
    else:
        a_spec = pl.BlockSpec((tm, tk), lambda i, j, kk: (i, kk))
        a_dims = (1,)
    dims = ((a_dims, b_dims), ((), ()))
    add_spec = pl.BlockSpec((tm, tn), lambda i, j, kk: (i, j))
    has_add = add is not None
    if into is None:
        o_spec, o_shape, buf = add_spec, (M, N), None
    else:
        buf, o_shape, ol, _ = into
        assert not has_add
        o_spec = pl.BlockSpec((None, tm, tn), lambda i, j, kk: (ol, i + oro // tm, j + oco // tn))
    has_buf = buf is not None

    def body(*refs):
        refs = list(refs)
        acc_ref = refs.pop() if nk > 1 else None
        o_ref = refs.pop()
        a_ref, b_ref = refs[0], refs[1]
        add_ref = refs[2] if has_add else None
        part = lax.dot_general(a_ref[...].astype(BF16), b_ref[...].astype(BF16), dims, preferred_element_type=F32)

        def finish(r):
            if has_add:
                r = r + add_ref[...].astype(F32)
            o_ref[...] = r.astype(out_dtype)

        if nk == 1:
            finish(part)
        else:
            kk = pl.program_id(2)

            @pl.when(kk == 0)
            def _():
                acc_ref[...] = part

            @pl.when(kk > 0)
            def _():
                acc_ref[...] += part

            @pl.when(kk == nk - 1)
            def _():
                finish(acc_ref[...])

    in_specs = [a_spec, b_spec] + ([add_spec] if has_add else []) + ([ANY] if has_buf else [])
    args = (a, b) + ((add,) if has_add else ()) + ((buf,) if has_buf else ())
    if after is not None:
        in_specs, args = in_specs + [ANY], args + (after,)
    return pl.pallas_call(
        body, grid=(M // tm, N // tn, nk), in_specs=in_specs, out_specs=o_spec,
        out_shape=S(o_shape, out_dtype), scratch_shapes=[pltpu.VMEM((tm, tn), F32)] if nk > 1 else [],
        input_output_aliases={2: 0} if has_buf else {},
        compiler_params=_cparams("parallel", "parallel", "arbitrary"), name=name)(*args)


def _rms_fwd(x, g, out_dtype, res=None, name="rms_fwd"):
    N, D = x.shape
    tm = _tile(N, (512, 256, 128))
    has_res = res is not None

    def body(*refs):
        if has_res:
            x_ref, g_ref, r_ref, o_ref = refs
        else:
            x_ref, g_ref, o_ref = refs
        xv = x_ref[...].astype(F32)
        y = xv * lax.rsqrt(jnp.mean(xv * xv, axis=-1, keepdims=True) + EPS) * g_ref[...]
        if has_res:
            y = y + r_ref[...]
        o_ref[...] = y.astype(out_dtype)

    row = pl.BlockSpec((tm, D), lambda i: (i, 0))
    vec = pl.BlockSpec((1, D), lambda i: (0, 0))
    return pl.pallas_call(
        body, grid=(N // tm,), in_specs=[row, vec] + ([row] if has_res else []), out_specs=row,
        out_shape=S((N, D), out_dtype), compiler_params=_cparams("parallel"), name=name)(
            *((x, g) + ((res,) if has_res else ())))


def _rms_bwd(x, g, dy, add=None, out_dtype=F32, name="rms_bwd"):
    N, D = x.shape
    tm = _tile(N, (512, 256, 128))
    has_add = add is not None

    def body(*refs):
        if has_add:
            x_ref, g_ref, dy_ref, add_ref, dx_ref, dg_ref = refs
        else:
            x_ref, g_ref, dy_ref, dx_ref, dg_ref = refs
        xv = x_ref[...].astype(F32)
        dyv = dy_ref[...].astype(F32)
        r = lax.rsqrt(jnp.mean(xv * xv, axis=-1, keepdims=True) + EPS)
        u = dyv * g_ref[...]
        dx = r * u - xv * (r * r * r * jnp.mean(u * xv, axis=-1, keepdims=True))
        if has_add:
            dx = dx + add_ref[...]
        dx_ref[...] = dx.astype(out_dtype)

        @pl.when(pl.program_id(0) == 0)
        def _():
            dg_ref[...] = jnp.zeros_like(dg_ref)

        dg_ref[...] += jnp.sum(dyv * xv * r, axis=0, keepdims=True)

    row = pl.BlockSpec((tm, D), lambda i: (i, 0))
    vec = pl.BlockSpec((1, D), lambda i: (0, 0))
    return pl.pallas_call(
        body, grid=(N // tm,), in_specs=[row, vec, row] + ([row] if has_add else []), out_specs=(row, vec),
        out_shape=(S((N, D), out_dtype), S((1, D), F32)), compiler_params=_cparams("arbitrary"), name=name)(
            *((x, g, dy) + ((add,) if has_add else ())))


def _shift_down(x, s, row):
    return jnp.where(row >= s, pltpu.roll(x, s, axis=0), 0.0)


def _shift_up(x, s, row):
    T = x.shape[0]
    return jnp.where(row < T - s, pltpu.roll(x, T - s, axis=0), 0.0)


SLAB = 16


def _conv_wrap(x, w_ref, b_ref):
    W = w_ref.shape[0]
    y = x * w_ref[W - 1:W, :] + b_ref[...]
    for s in range(1, W):
        y = y + pltpu.roll(x, s, axis=0) * w_ref[W - 1 - s:W - s, :]
    return y


def _conv_masked(x, w_ref, b_ref, row):
    W = w_ref.shape[0]
    y = x * w_ref[W - 1:W, :] + b_ref[...]
    for s in range(1, W):
        y = y + _shift_down(x, s, row) * w_ref[W - 1 - s:W - s, :]
    return y


def _conv_head(x_head, w_ref, b_ref):
    return _conv_masked(x_head, w_ref, b_ref, lax.broadcasted_iota(jnp.int32, x_head.shape, 0))


def _conv_bwd_masked(dy, x, w_ref, row):
    W = w_ref.shape[0]
    dx = dy * w_ref[W - 1:W, :]
    dws = [None] * W
    dws[W - 1] = jnp.sum(dy * x, axis=0, keepdims=True)
    for s in range(1, W):
        dx = dx + _shift_up(dy, s, row) * w_ref[W - 1 - s:W - s, :]
        dws[W - 1 - s] = jnp.sum(dy * _shift_down(x, s, row), axis=0, keepdims=True)
    return dx, jnp.concatenate(dws, axis=0), jnp.sum(dy, axis=0, keepdims=True)


def _conv_bwd_wrap(dy, x, w_ref):
    W = w_ref.shape[0]
    T = dy.shape[0]
    dx = dy * w_ref[W - 1:W, :]
    dws = [None] * W
    dws[W - 1] = jnp.sum(dy * x, axis=0, keepdims=True)
    for s in range(1, W):
        up = pltpu.roll(dy, T - s, axis=0)
        dx = dx + up * w_ref[W - 1 - s:W - s, :]
        dws[W - 1 - s] = jnp.sum(up * x, axis=0, keepdims=True)
    return dx, jnp.concatenate(dws, axis=0), jnp.sum(dy, axis=0, keepdims=True)


def _conv_bwd_fix(dy_head, dy_tail, x_tail, w_ref):
    row = lax.broadcasted_iota(jnp.int32, dy_tail.shape, 0)
    W = w_ref.shape[0]
    dx = dy_tail * w_ref[W - 1:W, :]
    extra = [jnp.zeros((1, dy_tail.shape[1]), F32)] * W
    for s in range(1, W):
        dx = dx + _shift_up(dy_tail, s, row) * w_ref[W - 1 - s:W - s, :]
        extra[W - 1 - s] = jnp.sum(jnp.where(row < s, dy_head * pltpu.roll(x_tail, s, axis=0), 0.0), axis=0,
                                   keepdims=True)
    return dx, jnp.concatenate(extra, axis=0)


def _gelu(g):
    t = jnp.tanh(GELU_C0 * (g + GELU_C1 * g * g * g))
    return 0.5 * g * (1.0 + t), t


def _dgelu(g, t):
    return 0.5 * (1.0 + t) + 0.5 * g * (1.0 - t * t) * (GELU_C0 * (1.0 + 3.0 * GELU_C1 * g * g))


def _cspec(T, off=0):
    return pl.BlockSpec((1, T, CT), lambda j, b: (b, 0, j + off))


def _pspec(rows, off=0):
    return pl.BlockSpec((rows, CT), lambda j, b: (0, j + off))


def _conv_fwd_call(x3, x_off, C, w, b, name):
    Bl, T, _ = x3.shape
    W = w.shape[0]

    def body(x_ref, w_ref, b_ref, o_ref):
        o_ref[0] = _conv_wrap(x_ref[0], w_ref, b_ref)
        o_ref[0, 0:SLAB, :] = _conv_head(x_ref[0, 0:SLAB, :], w_ref, b_ref)

    return pl.pallas_call(
        body, grid=(C // CT, Bl), in_specs=[_cspec(T, x_off // CT), _pspec(W), _pspec(1)], out_specs=_cspec(T),
        out_shape=S((Bl, T, C), F32), compiler_params=_cparams("parallel", "arbitrary"), name=name)(x3, w, b)


def _conv_bwd_call(dy3, x3, x_off, C, w, name):
    Bl, T, _ = x3.shape
    W = w.shape[0]

    def body(dy_ref, x_ref, w_ref, dx_ref, dw_ref, db_ref):
        dx, dw, db = _conv_bwd_wrap(dy_ref[0], x_ref[0], w_ref)
        dx_tail, dw_extra = _conv_bwd_fix(dy_ref[0, 0:SLAB, :], dy_ref[0, T - SLAB:T, :], x_ref[0, T - SLAB:T, :],
                                          w_ref)
        dx_ref[0] = dx.astype(BF16)
        dx_ref[0, T - SLAB:T, :] = dx_tail.astype(BF16)

        @pl.when(pl.program_id(1) == 0)
        def _():
            dw_ref[...] = jnp.zeros_like(dw_ref)
            db_ref[...] = jnp.zeros_like(db_ref)

        dw_ref[...] += dw - dw_extra
        db_ref[...] += db

    return pl.pallas_call(
        body, grid=(C // CT, Bl), in_specs=[_cspec(T), _cspec(T, x_off // CT), _pspec(W)],
        out_specs=(_cspec(T), _pspec(W), _pspec(1)),
        out_shape=(S((Bl, T, C), BF16), S((W, C), F32), S((1, C), F32)),
        compiler_params=_cparams("parallel", "arbitrary"), name=name)(dy3, x3, w)


def _ffn_mid_fwd(ug3, uv3, wc, bc, name):
    Bl, T, F = ug3.shape
    nf = F // CT

    def body(ug_ref, uv_ref, wg_ref, wv_ref, bg_ref, bv_ref, o_ref):
        g = _conv_wrap(ug_ref[0], wg_ref, bg_ref)
        v = _conv_wrap(uv_ref[0], wv_ref, bv_ref)
        o_ref[0] = (_gelu(g)[0] * v).astype(BF16)
        g = _conv_head(ug_ref[0, 0:SLAB, :], wg_ref, bg_ref)
        v = _conv_head(uv_ref[0, 0:SLAB, :], wv_ref, bv_ref)
        o_ref[0, 0:SLAB, :] = (_gelu(g)[0] * v).astype(BF16)

    return pl.pallas_call(
        body, grid=(nf, Bl),
        in_specs=[_cspec(T), _cspec(T), _pspec(3), _pspec(3, nf), _pspec(1), _pspec(1, nf)], out_specs=_cspec(T),
        out_shape=S((Bl, T, F), BF16), compiler_params=_cparams("parallel", "arbitrary"), name=name)(
            ug3, uv3, wc, wc, bc, bc)


def _ffn_mid_bwd(ug3, uv3, dact3, wc, bc, name):
    Bl, T, F = ug3.shape
    nf = F // CT

    def body(ug_ref, uv_ref, da_ref, wg_ref, wv_ref, bg_ref, bv_ref, dug_ref, duv_ref, dwg_ref, dwv_ref, dbg_ref,
             dbv_ref):
        row = lax.broadcasted_iota(jnp.int32, (T, CT), 0)
        ug = ug_ref[0]
        uv = uv_ref[0]
        g = _conv_masked(ug, wg_ref, bg_ref, row)
        v = _conv_masked(uv, wv_ref, bv_ref, row)
        da = da_ref[0]
        gel, t = _gelu(g)
        dg = da * v * _dgelu(g, t)
        dv = da * gel
        dug, dwg, dbg = _conv_bwd_masked(dg, ug, wg_ref, row)
        duv, dwv, dbv = _conv_bwd_masked(dv, uv, wv_ref, row)
        dug_ref[0] = dug.astype(BF16)
        duv_ref[0] = duv.astype(BF16)

        @pl.when(pl.program_id(1) == 0)
        def _():
            dwg_ref[...] = jnp.zeros_like(dwg_ref)
            dwv_ref[...] = jnp.zeros_like(dwv_ref)
            dbg_ref[...] = jnp.zeros_like(dbg_ref)
            dbv_ref[...] = jnp.zeros_like(dbv_ref)

        dwg_ref[...] += dwg
        dwv_ref[...] += dwv
        dbg_ref[...] += dbg
        dbv_ref[...] += dbv

    return pl.pallas_call(
        body, grid=(nf, Bl),
        in_specs=[_cspec(T), _cspec(T), _cspec(T), _pspec(3), _pspec(3, nf), _pspec(1), _pspec(1, nf)],
        out_specs=(_cspec(T), _cspec(T), _pspec(3), _pspec(3), _pspec(1), _pspec(1)),
        out_shape=(S((Bl, T, F), BF16), S((Bl, T, F), BF16), S((3, F), F32), S((3, F), F32), S((1, F), F32),
                   S((1, F), F32)),
        compiler_params=_cparams("parallel", "arbitrary"), name=name)(ug3, uv3, dact3, wc, wc, bc, bc)


def _lru_gates(xc, rp, ip, br_ref, bi_ref, lam_ref):
    r = jax.nn.sigmoid(rp + br_ref[...])
    i = jax.nn.sigmoid(ip + bi_ref[...])
    lam = lam_ref[...]
    sp = jnp.maximum(-lam, 0.0) + jnp.log1p(jnp.exp(-jnp.abs(lam)))
    log_a = (-LRU_C) * r * sp
    a = jnp.exp(log_a)
    z = 2.0 * log_a
    one_m_a2 = jnp.where(z > -0.05, -z * (1.0 + z * (0.5 + z * (1.0 / 6.0 + z * (1.0 / 24.0)))), 1.0 - a * a)
    mult = jnp.sqrt(one_m_a2)
    return r, i, sp, a, mult


def _rglru_fwd(xc3, gates3, proj3, br, bi, lam, name):
    Bl, T, C = xc3.shape
    nsteps = int(math.log2(T))
    assert 1 << nsteps == T

    def body(xc_ref, rp_ref, ip_ref, ug_ref, br_ref, bi_ref, lam_ref, y_ref, h_ref):
        row = lax.broadcasted_iota(jnp.int32, (T, CT), 0)
        xc = xc_ref[0]
        r, i, sp, a, mult = _lru_gates(xc, rp_ref[0], ip_ref[0], br_ref, bi_ref, lam_ref)
        b = mult * (i * xc)
        for st in range(nsteps):
            s = 1 << st
            a_sh = jnp.where(row >= s, pltpu.roll(a, s, axis=0), 1.0)
            b = a * _shift_down(b, s, row) + b
            a = a * a_sh
        h_ref[0] = b
        y_ref[0] = (b * _gelu(ug_ref[0])[0]).astype(BF16)

    return pl.pallas_call(
        body, grid=(C // CT, Bl),
        in_specs=[_cspec(T), _cspec(T), _cspec(T, C // CT), _cspec(T), _pspec(1), _pspec(1), _pspec(1)],
        out_specs=(_cspec(T), _cspec(T)), out_shape=(S((Bl, T, C), BF16), S((Bl, T, C), F32)),
        compiler_params=_cparams("parallel", "arbitrary"), name=name)(xc3, gates3, gates3, proj3, br, bi, lam)


def _rglru_bwd(dy3, xc3, gates3, proj3, h3, br, bi, lam, name):
    Bl, T, C = xc3.shape
    nsteps = int(math.log2(T))

    def body(dy_ref, xc_ref, rp_ref, ip_ref, ug_ref, h_ref, br_ref, bi_ref, lam_ref,
             dxc_ref, drp_ref, dip_ref, dug_ref, dbr_ref, dbi_ref, dlam_ref):
        row = lax.broadcasted_iota(jnp.int32, (T, CT), 0)
        xc = xc_ref[0]
        r, i, sp, a, mult = _lru_gates(xc, rp_ref[0], ip_ref[0], br_ref, bi_ref, lam_ref)
        h = h_ref[0]
        dy = dy_ref[0]
        ug = ug_ref[0]
        gel, t = _gelu(ug)
        dug_ref[0] = (dy * h * _dgelu(ug, t)).astype(BF16)
        gacc = dy * gel
        an = _shift_up(a, 1, row)
        for st in range(nsteps):
            s = 1 << st
            an_sh = jnp.where(row < T - s, pltpu.roll(an, T - s, axis=0), 1.0)
            gacc = an * _shift_up(gacc, s, row) + gacc
            an = an * an_sh
        da = gacc * _shift_down(h, 1, row)
        ix = i * xc
        d_mult = gacc * ix
        d_i = gacc * mult * xc
        dxc_ref[0] = gacc * mult * i
        d_log_a = da * a - d_mult * (a * a) / mult
        d_r = d_log_a * ((-LRU_C) * sp)
        d_sp = jnp.sum(d_log_a * ((-LRU_C) * r), axis=0, keepdims=True)
        drp = d_r * r * (1.0 - r)
        dip = d_i * i * (1.0 - i)
        drp_ref[0] = drp.astype(BF16)
        dip_ref[0] = dip.astype(BF16)

        @pl.when(pl.program_id(1) == 0)
        def _():
            dbr_ref[...] = jnp.zeros_like(dbr_ref)
            dbi_ref[...] = jnp.zeros_like(dbi_ref)
            dlam_ref[...] = jnp.zeros_like(dlam_ref)

        dbr_ref[...] += jnp.sum(drp, axis=0, keepdims=True)
        dbi_ref[...] += jnp.sum(dip, axis=0, keepdims=True)
        dlam_ref[...] += d_sp * (-jax.nn.sigmoid(-lam_ref[...]))

    vec = S((1, C), F32)
    act = S((Bl, T, C), BF16)
    return pl.pallas_call(
        body, grid=(C // CT, Bl),
        in_specs=[_cspec(T), _cspec(T), _cspec(T), _cspec(T, C // CT), _cspec(T), _cspec(T)] + [_pspec(1)] * 3,
        out_specs=(_cspec(T), _cspec(T), _cspec(T), _cspec(T), _pspec(1), _pspec(1), _pspec(1)),
        out_shape=(S((Bl, T, C), F32), act, act, act, vec, vec, vec),
        compiler_params=_cparams("parallel", "arbitrary"), name=name)(dy3, xc3, gates3, gates3, proj3, h3, br, bi, lam)


NT = (((1,), (1,)), ((), ()))
TN = (((0,), (0,)), ((), ()))


def _hs(h):
    return slice(h * HEAD, (h + 1) * HEAD)


def _mem_softmax(qb, kb):
    s = lax.dot_general(qb, kb, NT, preferred_element_type=F32) * (HEAD ** -0.5)
    e = jnp.exp(s - jnp.max(s, axis=-1, keepdims=True))
    return e / jnp.sum(e, axis=-1, keepdims=True)


def _mem_attn_fwd(proj3, q_off, mkv3, name):
    Bl, T, _ = proj3.shape
    M = mkv3.shape[1]
    tq = _tile(T, (512, 256, 128))

    def body(q_ref, k_ref, v_ref, o_ref):
        q = q_ref[0].astype(BF16)
        k = k_ref[0].astype(BF16)
        v = v_ref[0].astype(BF16)
        outs = []
        for h in range(MEM_HEADS):
            p = _mem_softmax(q[:, _hs(h)], k[:, _hs(h)])
            outs.append(jnp.dot(p.astype(BF16), v[:, _hs(h)], preferred_element_type=F32))
        o_ref[0] = jnp.concatenate(outs, axis=-1).astype(BF16)

    return pl.pallas_call(
        body, grid=(Bl, T // tq),
        in_specs=[pl.BlockSpec((1, tq, MEM_W), lambda b, t: (b, t, q_off // MEM_W)),
                  pl.BlockSpec((1, M, MEM_W), lambda b, t: (b, 0, 0)),
                  pl.BlockSpec((1, M, MEM_W), lambda b, t: (b, 0, 1))],
        out_specs=pl.BlockSpec((1, tq, MEM_W), lambda b, t: (b, t, 0)),
        out_shape=S((Bl, T, MEM_W), BF16), compiler_params=_cparams("parallel", "parallel"), name=name)(
            proj3, mkv3, mkv3)


def _mem_attn_bwd(proj3, q_off, mkv3, do3, name):
    Bl, T, _ = proj3.shape
    M = mkv3.shape[1]
    tq = _tile(T, (512, 256, 128))
    scale = HEAD ** -0.5

    def body(q_ref, k_ref, v_ref, do_ref, dq_ref, dkv_ref):
        q = q_ref[0].astype(BF16)
        k = k_ref[0].astype(BF16)
        v = v_ref[0].astype(BF16)
        do = do_ref[0].astype(BF16)
        dqs, dks, dvs = [], [], []
        for h in range(MEM_HEADS):
            qh, kh, vh, doh = q[:, _hs(h)], k[:, _hs(h)], v[:, _hs(h)], do[:, _hs(h)]
            p = _mem_softmax(qh, kh)
            dvs.append(lax.dot_general(p.astype(BF16), doh, TN, preferred_element_type=F32))
            dp = lax.dot_general(doh, vh, NT, preferred_element_type=F32)
            ds = (p * (dp - jnp.sum(p * dp, axis=-1, keepdims=True)) * scale).astype(BF16)
            dqs.append(jnp.dot(ds, kh, preferred_element_type=F32))
            dks.append(lax.dot_general(ds, qh, TN, preferred_element_type=F32))
        dq_ref[0] = jnp.concatenate(dqs, axis=-1).astype(BF16)

        @pl.when(pl.program_id(1) == 0)
        def _():
            dkv_ref[...] = jnp.zeros_like(dkv_ref)

        dkv_ref[0] += jnp.concatenate(dks + dvs, axis=-1)

    return pl.pallas_call(
        body, grid=(Bl, T // tq),
        in_specs=[pl.BlockSpec((1, tq, MEM_W), lambda b, t: (b, t, q_off // MEM_W)),
                  pl.BlockSpec((1, M, MEM_W), lambda b, t: (b, 0, 0)),
                  pl.BlockSpec((1, M, MEM_W), lambda b, t: (b, 0, 1)),
                  pl.BlockSpec((1, tq, MEM_W), lambda b, t: (b, t, 0))],
        out_specs=(pl.BlockSpec((1, tq, MEM_W), lambda b, t: (b, t, 0)),
                   pl.BlockSpec((1, M, 2 * MEM_W), lambda b, t: (b, 0, 0))),
        out_shape=(S((Bl, T, MEM_W), BF16), S((Bl, M, 2 * MEM_W), F32)),
        compiler_params=_cparams("parallel", "arbitrary"), name=name)(proj3, mkv3, mkv3, do3)


GROUP_ROWS = SWA_GROUP * WIN


def _group_rows(x, kvh):
    return jnp.concatenate([x[:, _hs(SWA_GROUP * kvh + g)] for g in range(SWA_GROUP)], axis=0)


def _group_col(vals):
    grp = lax.shift_right_logical(lax.broadcasted_iota(jnp.int32, (GROUP_ROWS, 1), 0), WIN.bit_length() - 1)
    col = jnp.full((GROUP_ROWS, 1), vals[-1], F32)
    for g in range(SWA_GROUP - 2, -1, -1):
        col = jnp.where(grp == g, vals[g], col)
    return col


def _swa_probs(qh, kph, kch, sink, slope, has_prev):
    qi = jnp.bitwise_and(lax.broadcasted_iota(jnp.int32, (GROUP_ROWS, WIN), 0), WIN - 1)
    kj = lax.broadcasted_iota(jnp.int32, (GROUP_ROWS, WIN), 1)
    scale = HEAD ** -0.5
    sp = lax.dot_general(qh, kph, NT, preferred_element_type=F32) * scale
    sc = lax.dot_general(qh, kch, NT, preferred_element_type=F32) * scale
    dist_p = (qi + WIN - kj).astype(F32)
    dist_c = (qi - kj).astype(F32)
    neg = -jnp.inf
    sp = jnp.where(kj > qi + jnp.where(has_prev, 0, WIN), sp - slope * dist_p, neg)
    sc = jnp.where(kj <= qi, sc - slope * dist_c, neg)
    m = jnp.maximum(jnp.maximum(jnp.max(sp, axis=-1, keepdims=True), jnp.max(sc, axis=-1, keepdims=True)), sink)
    ep = jnp.exp(sp - m)
    ec = jnp.exp(sc - m)
    es = jnp.exp(sink - m)
    inv = 1.0 / (jnp.sum(ep, axis=-1, keepdims=True) + jnp.sum(ec, axis=-1, keepdims=True) + es)
    return ep * inv, ec * inv, es * inv


def _swa_specs(nb):
    prev = lambda n: jnp.maximum(n - 1, 0)
    q = pl.BlockSpec((1, WIN, MIX_W), lambda b, n: (b, n, 0))
    kp = pl.BlockSpec((1, WIN, MEM_W), lambda b, n: (b, prev(n), 0))
    kc = pl.BlockSpec((1, WIN, MEM_W), lambda b, n: (b, n, 0))
    vp = pl.BlockSpec((1, WIN, MEM_W), lambda b, n: (b, prev(n), 1))
    vc = pl.BlockSpec((1, WIN, MEM_W), lambda b, n: (b, n, 1))
    sm = pl.BlockSpec(memory_space=pltpu.SMEM)
    return q, kp, kc, vp, vc, sm


def _swa_fwd(proj3, kv3, sinks, name):
    Bl, T, _ = proj3.shape
    nb = T // WIN
    q_s, kp_s, kc_s, vp_s, vc_s, sm = _swa_specs(nb)

    def body(q_ref, kp_ref, kc_ref, vp_ref, vc_ref, sink_ref, o_ref):
        has_prev = pl.program_id(1) > 0
        q = q_ref[0].astype(BF16)
        kp, kc = kp_ref[0].astype(BF16), kc_ref[0].astype(BF16)
        vp, vc = vp_ref[0].astype(BF16), vc_ref[0].astype(BF16)
        outs = []
        for kvh in range(SWA_HEADS // SWA_GROUP):
            kvs = _hs(kvh)
            heads = range(SWA_GROUP * kvh, SWA_GROUP * (kvh + 1))
            pp, pc, _ = _swa_probs(_group_rows(q, kvh), kp[:, kvs], kc[:, kvs], _group_col([sink_ref[h] for h in heads]),
                                   _group_col([SLOPES[h] for h in heads]), has_prev)
            og = (jnp.dot(pp.astype(BF16), vp[:, kvs], preferred_element_type=F32)
                  + jnp.dot(pc.astype(BF16), vc[:, kvs], preferred_element_type=F32))
            outs += [og[g * WIN:(g + 1) * WIN] for g in range(SWA_GROUP)]
        o_ref[0] = jnp.concatenate(outs, axis=-1).astype(BF16)

    return pl.pallas_call(
        body, grid=(Bl, nb), in_specs=[q_s, kp_s, kc_s, vp_s, vc_s, sm], out_specs=q_s,
        out_shape=S((Bl, T, MIX_W), BF16), compiler_params=_cparams("parallel", "parallel"), name=name)(
            proj3, kv3, kv3, kv3, kv3, sinks)


def _swa_bwd(proj3, kv3, sinks, do3, name):
    Bl, T, _ = proj3.shape
    nb = T // WIN
    q_s, kp_s, kc_s, vp_s, vc_s, sm = _swa_specs(nb)
    kv_s = pl.BlockSpec((1, WIN, 2 * MEM_W), lambda b, n: (b, n, 0))
    sk_s = pl.BlockSpec((8, LANES), lambda b, n: (0, 0))
    scale = HEAD ** -0.5

    def body(q_ref, kp_ref, kc_ref, vp_ref, vc_ref, sink_ref, do_ref, dq_ref, dkc_ref, dkp_ref, dsk_ref):
        has_prev = pl.program_id(1) > 0
        q = q_ref[0].astype(BF16)
        kp, kc = kp_ref[0].astype(BF16), kc_ref[0].astype(BF16)
        vp, vc = vp_ref[0].astype(BF16), vc_ref[0].astype(BF16)
        do = do_ref[0].astype(BF16)
        lane = lax.broadcasted_iota(jnp.int32, (8, LANES), 1)
        srow = lax.broadcasted_iota(jnp.int32, (8, LANES), 0)
        dsk = jnp.zeros((8, LANES), F32)
        dqs = []
        dkc, dkp, dvc, dvp = [], [], [], []
        grp = lax.shift_right_logical(lax.broadcasted_iota(jnp.int32, (GROUP_ROWS, 1), 0), WIN.bit_length() - 1)
        for kvh in range(SWA_HEADS // SWA_GROUP):
            kvs = _hs(kvh)
            heads = range(SWA_GROUP * kvh, SWA_GROUP * (kvh + 1))
            qg, dog = _group_rows(q, kvh), _group_rows(do, kvh)
            pp, pc, ps = _swa_probs(qg, kp[:, kvs], kc[:, kvs], _group_col([sink_ref[h] for h in heads]),
                                    _group_col([SLOPES[h] for h in heads]), has_prev)
            dpp = lax.dot_general(dog, vp[:, kvs], NT, preferred_element_type=F32)
            dpc = lax.dot_general(dog, vc[:, kvs], NT, preferred_element_type=F32)
            delta = jnp.sum(pp * dpp, axis=-1, keepdims=True) + jnp.sum(pc * dpc, axis=-1, keepdims=True)
            dsp = (pp * (dpp - delta) * scale).astype(BF16)
            dsc = (pc * (dpc - delta) * scale).astype(BF16)
            dqg = (jnp.dot(dsp, kp[:, kvs], preferred_element_type=F32)
                   + jnp.dot(dsc, kc[:, kvs], preferred_element_type=F32))
            dqs += [dqg[g * WIN:(g + 1) * WIN] for g in range(SWA_GROUP)]
            dkc.append(lax.dot_general(dsc, qg, TN, preferred_element_type=F32))
            dkp.append(lax.dot_general(dsp, qg, TN, preferred_element_type=F32))
            dvc.append(lax.dot_general(pc.astype(BF16), dog, TN, preferred_element_type=F32))
            dvp.append(lax.dot_general(pp.astype(BF16), dog, TN, preferred_element_type=F32))
            dsink = ps * delta
            for g, h in enumerate(heads):
                dsk = dsk + jnp.where((lane == h) & (srow == 0), -jnp.sum(jnp.where(grp == g, dsink, 0.0)), 0.0)
        dq_ref[0] = jnp.concatenate(dqs, axis=-1).astype(BF16)
        dkc_ref[0] = jnp.concatenate(dkc + dvc, axis=-1)
        dkp_ref[0] = jnp.concatenate(dkp + dvp, axis=-1)

        @pl.when((pl.program_id(0) == 0) & (pl.program_id(1) == 0))
        def _():
            dsk_ref[...] = jnp.zeros_like(dsk_ref)

        dsk_ref[...] += dsk

    return pl.pallas_call(
        body, grid=(Bl, nb), in_specs=[q_s, kp_s, kc_s, vp_s, vc_s, sm, q_s], out_specs=(q_s, kv_s, kv_s, sk_s),
        out_shape=(S((Bl, T, MIX_W), BF16), S((Bl, T, 2 * MEM_W), F32), S((Bl, T, 2 * MEM_W), F32), S((8, LANES), F32)),
        compiler_params=_cparams("arbitrary", "arbitrary"), name=name)(proj3, kv3, kv3, kv3, kv3, sinks, do3)


def _kv_grad_combine(parts, name):
    Bl, T, W = parts[0][0].shape
    nb = T // WIN
    nl = len(parts)

    def body(*refs):
        o_ref = refs[-1]
        has_next = jnp.where(pl.program_id(1) == nb - 1, 0.0, 1.0)
        acc = None
        for l in range(nl):
            c = refs[2 * l][0] + has_next * refs[2 * l + 1][0]
            acc = c if acc is None else acc + c
        o_ref[0] = acc.astype(BF16)

    cur = pl.BlockSpec((1, WIN, W), lambda b, n: (b, n, 0))
    nxt = pl.BlockSpec((1, WIN, W), lambda b, n: (b, jnp.minimum(n + 1, nb - 1), 0))
    return pl.pallas_call(
        body, grid=(Bl, nb), in_specs=[cur, nxt] * nl, out_specs=cur, out_shape=S((Bl, T, W), BF16),
        compiler_params=_cparams("parallel", "parallel"), name=name)(*[a for pr in parts for a in pr])


def _loss_bwd(y, target, name="loss"):
    N, D = y.shape
    tm = _tile(N, (512, 256, 128))

    def body(y_ref, t_ref, dy_ref, l_ref):
        e = y_ref[...] - t_ref[...]
        dy_ref[...] = e * (1.0 / D)

        @pl.when(pl.program_id(0) == 0)
        def _():
            l_ref[...] = jnp.zeros_like(l_ref)

        l_ref[...] += jnp.sum(e * e, axis=0, keepdims=True) * (0.5 / D)

    row = pl.BlockSpec((tm, D), lambda i: (i, 0))
    vec = pl.BlockSpec((1, D), lambda i: (0, 0))
    return pl.pallas_call(
        body, grid=(N // tm,), in_specs=[row, row], out_specs=(row, vec), out_shape=(S((N, D), F32), S((1, D), F32)),
        compiler_params=_cparams("arbitrary"), name=name)(y, target)


def _all_gather(x, name):
    R, C = x.shape

    def body(x_ref, out_ref, send_sems, recv_sems, local_sem):
        mx, my, mc = lax.axis_index("x"), lax.axis_index("y"), lax.axis_index("c")
        me, sibling = (mx, my, mc), (mx, my, 1 - mc)
        chips = [(1 - mx, my), (mx, 1 - my), (1 - mx, 1 - my)]

        def rows(px, py, pc):
            return out_ref.at[4 * px + 2 * py + pc]

        def copy(kk, block, to, src=None):
            return pltpu.make_async_remote_copy(
                src_ref=rows(*block) if src is None else src, dst_ref=rows(*block), send_sem=send_sems.at[kk],
                recv_sem=recv_sems.at[kk], device_id=to, device_id_type=MESH)

        mine = pltpu.make_async_copy(x_ref, rows(*me), local_sem)
        mine.start()
        first = [copy(0, me, sibling, src=x_ref)]
        first += [copy(1 + j, me, (*chip, mc), src=x_ref) for j, chip in enumerate(chips)]
        for cp in first:
            cp.start()
        passed = [copy(4 + j, (*chip, mc), sibling) for j, chip in enumerate(chips)]
        for j, chip in enumerate(chips):
            copy(1 + j, (*chip, mc), me).wait_recv()
            passed[j].start()
        copy(0, sibling, me).wait_recv()
        for j, chip in enumerate(chips):
            copy(4 + j, (*chip, 1 - mc), me).wait_recv()
        for cp in first + passed:
            cp.wait_send()
        mine.wait()

    return pl.pallas_call(
        body, out_shape=S((N_DEV, R, C), x.dtype), in_specs=[ANY], out_specs=ANY,
        scratch_shapes=[pltpu.SemaphoreType.DMA((7,)), pltpu.SemaphoreType.DMA((7,)), pltpu.SemaphoreType.DMA(())],
        name=name)(x)


def _ag_weights(shards, row_sharded, name):
    n = len(shards)

    def full_shape(a, rows):
        if rows:
            return a.shape[:-2] + (N_DEV * a.shape[-2],) + a.shape[-1:]
        return (N_DEV,) + a.shape

    def body(*refs):
        x_refs, o_refs = refs[:n], refs[n:2 * n]
        send_sems, recv_sems, local_sems = refs[2 * n:]
        mx, my, mc = lax.axis_index("x"), lax.axis_index("y"), lax.axis_index("c")
        me, sibling = (mx, my, mc), (mx, my, 1 - mc)
        chips = [(1 - mx, my), (mx, 1 - my), (1 - mx, 1 - my)]

        def dst(t, px, py, pc):
            d = 4 * px + 2 * py + pc
            if not row_sharded[t]:
                return o_refs[t].at[d]
            r = shards[t].shape[-2]
            idx = (slice(None),) * (shards[t].ndim - 2) + (pl.ds(pl.multiple_of(d * r, 16), r), slice(None))
            return o_refs[t].at[idx]

        def copy(kk, t, block, to, src=None):
            return pltpu.make_async_remote_copy(
                src_ref=dst(t, *block) if src is None else src, dst_ref=dst(t, *block),
                send_sem=send_sems.at[kk * n + t], recv_sem=recv_sems.at[kk * n + t], device_id=to,
                device_id_type=MESH)

        mine = [pltpu.make_async_copy(x_refs[t], dst(t, *me), local_sems.at[t]) for t in range(n)]
        for cp in mine:
            cp.start()
        first = []
        for t in range(n):
            first.append(copy(0, t, me, sibling, src=x_refs[t]))
            first += [copy(1 + j, t, me, (*chip, mc), src=x_refs[t]) for j, chip in enumerate(chips)]
        for cp in first:
            cp.start()
        passed = []
        for j, chip in enumerate(chips):
            for t in range(n):
                copy(1 + j, t, (*chip, mc), me).wait_recv()
                cp = copy(4 + j, t, (*chip, mc), sibling)
                cp.start()
                passed.append(cp)
        for t in range(n):
            copy(0, t, sibling, me).wait_recv()
            for j, chip in enumerate(chips):
                copy(4 + j, t, (*chip, 1 - mc), me).wait_recv()
        for cp in first + passed:
            cp.wait_send()
        for cp in mine:
            cp.wait()

    return pl.pallas_call(
        body, out_shape=tuple(S(full_shape(a, r), a.dtype) for a, r in zip(shards, row_sharded)),
        in_specs=[ANY] * n, out_specs=tuple([ANY] * n),
        scratch_shapes=[pltpu.SemaphoreType.DMA((7 * n,)), pltpu.SemaphoreType.DMA((7 * n,)),
                        pltpu.SemaphoreType.DMA((n,))],
        name=name)(*shards)


FLIPS = [(fx, fy, fc) for fx in (0, 1) for fy in (0, 1) for fc in (0, 1)][1:]
HBM = pl.BlockSpec(memory_space=pltpu.HBM)
SEM = pl.BlockSpec(memory_space=pltpu.SEMAPHORE)
EFFECT = pltpu.SideEffectType.DATAFLOW_SIDE_EFFECTING


def _hbm(a):
    return pltpu.with_memory_space_constraint(a, pltpu.HBM)


def _flips(gather):
    return [(0, 0, 0)] + FLIPS if gather else FLIPS


def _split_copies(gather, s_refs, l_refs, send_sems, recv_sems):
    n = len(s_refs)
    mx, my, mc = lax.axis_index("x"), lax.axis_index("y"), lax.axis_index("c")
    me = 4 * mx + 2 * my + mc
    copies = []
    for k, (fx, fy, fc) in enumerate(_flips(gather)):
        px, py, pc = (1 - mx if fx else mx), (1 - my if fy else my), (1 - mc if fc else mc)
        for t in range(n):
            if gather:
                src = s_refs[t]
                r = src.shape[0]
                dst = l_refs[t].at[pl.ds(pl.multiple_of(me * r, 16), r), :]
            else:
                src = s_refs[t].at[:, 4 * px + 2 * py + pc]
                dst = l_refs[t].at[k]
            copies.append(pltpu.make_async_remote_copy(
                src_ref=src, dst_ref=dst, send_sem=send_sems.at[k * n + t], recv_sem=recv_sems.at[k * n + t],
                device_id=(px, py, pc), device_id_type=MESH))
    return copies


def _split_start(gather, srcs, lands, after, name):
    n = len(srcs)
    n_sem = len(_flips(gather)) * n

    def body(*refs):
        s_refs, l_refs = refs[:n], refs[n:2 * n]
        send_sems, recv_sems = refs[2 * n + 1], refs[2 * n + 2]
        token = refs[-1]
        for cp in _split_copies(gather, s_refs, l_refs, send_sems, recv_sems):
            cp.start()
        token[...] = jnp.zeros_like(token)

    outs = pl.pallas_call(
        body, name=name,
        out_shape=(pltpu.SemaphoreType.DMA((n_sem,)), pltpu.SemaphoreType.DMA((n_sem,)))
        + tuple(pltpu.HBM(a.shape, a.dtype) for a in lands) + (S((8, LANES), F32),),
        in_specs=[HBM] * (2 * n) + [ANY],
        out_specs=(SEM, SEM) + (HBM,) * n + (pl.BlockSpec(memory_space=pltpu.VMEM),),
        input_output_aliases={n + i: 2 + i for i in range(n)},
        compiler_params=pltpu.CompilerParams(has_side_effects=EFFECT),
    )(*[_hbm(a) for a in srcs], *[_hbm(a) for a in lands], after)
    return outs[0], outs[1], list(srcs), list(outs[2:2 + n]), outs[-1]


def _split_wait(gather, send_sems, recv_sems, srcs, lands, after, name):
    n = len(srcs)

    def body(*refs):
        s_refs, l_refs = refs[:n], refs[n:2 * n]
        ssem, rsem = refs[2 * n], refs[2 * n + 1]
        copies = _split_copies(gather, s_refs, l_refs, ssem, rsem)
        for cp in copies:
            cp.wait_send()
        for cp in copies:
            cp.wait_recv()

    outs = pl.pallas_call(
        body, name=name, out_shape=tuple(pltpu.HBM(a.shape, a.dtype) for a in lands),
        in_specs=[HBM] * (2 * n) + [SEM, SEM, ANY], out_specs=(HBM,) * n,
        input_output_aliases={n + i: i for i in range(n)},
        compiler_params=pltpu.CompilerParams(has_side_effects=EFFECT),
    )(*[_hbm(a) for a in srcs], *lands, send_sems, recv_sems, after)
    return list(outs)


def _adamw_math(w, g, m, v):
    m = ADAM_B1 * m + (1.0 - ADAM_B1) * g
    v = ADAM_B2 * v + (1.0 - ADAM_B2) * (g * g)
    m_hat = m / (1.0 - ADAM_B1 ** ADAM_STEP)
    v_hat = v / (1.0 - ADAM_B2 ** ADAM_STEP)
    delta = -ADAM_LR * (m_hat / (jnp.sqrt(v_hat) + ADAM_EPS) + ADAM_WD * w)
    return delta, m, v


def _adamw_layers(owns, gots, w, m, v, name):
    L, B, C = w.shape
    per_row = 2 * L * len(FLIPS) * C * owns[0].dtype.itemsize
    tb = max([t for t in range(16, B + 1, 16) if B % t == 0 and (t * per_row <= 16 * 1024 * 1024 or t == 16)] or [B])
    me = (4 * lax.axis_index("x") + 2 * lax.axis_index("y") + lax.axis_index("c")).astype(jnp.int32).reshape(1)

    def body(me_ref, *refs):
        own_refs, got_refs = refs[:L], refs[L:2 * L]
        w_ref, m_ref, v_ref = refs[2 * L:2 * L + 3]
        g_out, d_out, m_out, v_out = refs[2 * L + 3:]
        layer = pl.program_id(0)
        for kk in range(L):
            @pl.when(layer == kk)
            def _():
                g = own_refs[kk][0].astype(F32)
                for s in range(len(FLIPS)):
                    g = g + got_refs[kk][s].astype(F32)
                d, mn, vn = _adamw_math(w_ref[...], g, m_ref[...], v_ref[...])
                g_out[...] = g
                d_out[...] = d
                m_out[...] = mn
                v_out[...] = vn

    def row(kk, layer, i):
        return jnp.where(layer == kk, i, 0)

    blk = pl.BlockSpec((1, tb, C), lambda layer, i, me_ref: (layer, i, 0))
    own_specs = [pl.BlockSpec((1, 1, tb, C), lambda layer, i, me_ref, kk=kk: (0, me_ref[0], row(kk, layer, i), 0))
                 for kk in range(L)]
    got_specs = [pl.BlockSpec((len(FLIPS), 1, tb, C), lambda layer, i, me_ref, kk=kk: (0, 0, row(kk, layer, i), 0))
                 for kk in range(L)]
    return pl.pallas_call(
        body,
        grid_spec=pltpu.PrefetchScalarGridSpec(
            num_scalar_prefetch=1, grid=(L, B // tb), in_specs=own_specs + got_specs + [blk, blk, blk],
            out_specs=(blk, blk, blk, blk)),
        out_shape=(S((L, B, C), F32),) * 4, compiler_params=_cparams("arbitrary", "arbitrary"), name=name)(
            me, *owns, *gots, w, m, v)


def _adamw_replicated(parts, w, m, v, name):
    R, C = w.shape
    rb = _tile(R, (512, 256, 128, 64, 32, 16, 8))

    def body(p_ref, w_ref, m_ref, v_ref, g_out, d_out, m_out, v_out):
        g = p_ref[0]
        for j in range(1, N_DEV):
            g = g + p_ref[j]
        d, mn, vn = _adamw_math(w_ref[...], g, m_ref[...], v_ref[...])
        g_out[...] = g
        d_out[...] = d
        m_out[...] = mn
        v_out[...] = vn

    blk = pl.BlockSpec((rb, C), lambda i: (i, 0))
    return pl.pallas_call(
        body, grid=(R // rb,), in_specs=[pl.BlockSpec((N_DEV, rb, C), lambda i: (0, i, 0)), blk, blk, blk],
        out_specs=(blk, blk, blk, blk), out_shape=(S((R, C), F32),) * 4, compiler_params=_cparams("parallel"),
        name=name)(parts, w, m, v)


def _pack(arrs, rows_mult, dtype):
    flat = jnp.concatenate([a.reshape(-1).astype(dtype) for a in arrs])
    n = flat.shape[0]
    per = rows_mult * LANES
    tot = -(-n // per) * per
    return jnp.pad(flat, (0, tot - n)).reshape(tot // LANES, LANES)


def _unpack(blob, shapes):
    flat = blob.reshape(-1)
    out, off = [], 0
    for shp in shapes:
        n = int(np.prod(shp))
        out.append(flat[off:off + n].reshape(shp))
        off += n
    return out


def _small_to_natural(g8):
    t = jnp.moveaxis(g8, 0, -2)
    return t.reshape(t.shape[:-2] + (N_DEV * t.shape[-1],))


def _small_to_cols(g):
    t = g.reshape(g.shape[:-1] + (N_DEV, g.shape[-1] // N_DEV))
    return jnp.moveaxis(t, -2, 0)


def _block_diag(w):
    nb, bs, _ = w.shape
    eye = jnp.eye(nb, dtype=w.dtype)
    return (eye[:, None, :, None] * w[:, :, None, :]).reshape(nb * bs, nb * bs)


def _diag_blocks(d, nb, bs):
    d4 = d.reshape(nb, bs, nb, bs)
    return jnp.stack([d4[i, :, i, :] for i in range(nb)])


def kernel(x, mem, g_mix_pre, g_mix_post, g_ffn_pre, g_ffn_post, g_mem, w_mem_kv, w_mix_out, w_ffn_up, w_ffn_conv, b_ffn_conv, w_ffn_down, w_in_a, w_conv_a, b_conv_a, w_rg_r, b_rg_r, w_rg_i, b_rg_i, lru_lambda, w_in_b, sinks_b, g_kv, w_kv, loss_target, m_g_mix_pre, m_g_mix_post, m_g_ffn_pre, m_g_ffn_post, m_g_mem, m_w_mem_kv, m_w_mix_out, m_w_ffn_up, m_w_ffn_conv, m_b_ffn_conv, m_w_ffn_down, m_w_in_a, m_w_conv_a, m_b_conv_a, m_w_rg_r, m_b_rg_r, m_w_rg_i, m_b_rg_i, m_lru_lambda, m_w_in_b, m_sinks_b, m_g_kv, m_w_kv, v_g_mix_pre, v_g_mix_post, v_g_ffn_pre, v_g_ffn_post, v_g_mem, v_w_mem_kv, v_w_mix_out, v_w_ffn_up, v_w_ffn_conv, v_b_ffn_conv, v_w_ffn_down, v_w_in_a, v_w_conv_a, v_b_conv_a, v_w_rg_r, v_b_rg_r, v_w_rg_i, v_b_rg_i, v_lru_lambda, v_w_in_b, v_sinks_b, v_g_kv, v_w_kv):
    w_loc = dict(g_mix_pre=g_mix_pre, g_mix_post=g_mix_post, g_ffn_pre=g_ffn_pre, g_ffn_post=g_ffn_post, g_mem=g_mem,
                 w_mem_kv=w_mem_kv, w_mix_out=w_mix_out, w_ffn_up=w_ffn_up, w_ffn_conv=w_ffn_conv,
                 b_ffn_conv=b_ffn_conv, w_ffn_down=w_ffn_down, w_in_a=w_in_a, w_conv_a=w_conv_a, b_conv_a=b_conv_a,
                 w_rg_r=w_rg_r, b_rg_r=b_rg_r, w_rg_i=w_rg_i, b_rg_i=b_rg_i, lru_lambda=lru_lambda, w_in_b=w_in_b,
                 sinks_b=sinks_b, g_kv=g_kv, w_kv=w_kv)
    m_loc = dict(g_mix_pre=m_g_mix_pre, g_mix_post=m_g_mix_post, g_ffn_pre=m_g_ffn_pre, g_ffn_post=m_g_ffn_post,
                 g_mem=m_g_mem, w_mem_kv=m_w_mem_kv, w_mix_out=m_w_mix_out, w_ffn_up=m_w_ffn_up,
                 w_ffn_conv=m_w_ffn_conv, b_ffn_conv=m_b_ffn_conv, w_ffn_down=m_w_ffn_down, w_in_a=m_w_in_a,
                 w_conv_a=m_w_conv_a, b_conv_a=m_b_conv_a, w_rg_r=m_w_rg_r, b_rg_r=m_b_rg_r, w_rg_i=m_w_rg_i,
                 b_rg_i=m_b_rg_i, lru_lambda=m_lru_lambda, w_in_b=m_w_in_b, sinks_b=m_sinks_b, g_kv=m_g_kv,
                 w_kv=m_w_kv)
    v_loc = dict(g_mix_pre=v_g_mix_pre, g_mix_post=v_g_mix_post, g_ffn_pre=v_g_ffn_pre, g_ffn_post=v_g_ffn_post,
                 g_mem=v_g_mem, w_mem_kv=v_w_mem_kv, w_mix_out=v_w_mix_out, w_ffn_up=v_w_ffn_up,
                 w_ffn_conv=v_w_ffn_conv, b_ffn_conv=v_b_ffn_conv, w_ffn_down=v_w_ffn_down, w_in_a=v_w_in_a,
                 w_conv_a=v_w_conv_a, b_conv_a=v_b_conv_a, w_rg_r=v_w_rg_r, b_rg_r=v_b_rg_r, w_rg_i=v_w_rg_i,
                 b_rg_i=v_b_rg_i, lru_lambda=v_lru_lambda, w_in_b=v_w_in_b, sinks_b=v_sinks_b, g_kv=v_g_kv,
                 w_kv=v_w_kv)

    Bl, T, D = x.shape
    Ml = mem.shape[1]
    N = Bl * T
    depth = g_mix_pre.shape[0]
    n_a = w_in_a.shape[0]
    F = w_ffn_down.shape[1] * N_DEV
    def as_rows(n, a):
        return jnp.swapaxes(a, -1, -2) if n in TRANSPOSED else a

    def layer_keys(l):
        keys = [("w_mem_kv", l), ("w_mix_out", l), ("w_ffn_up", l), ("w_ffn_down", l)]
        keys.append(("w_in_a", l) if l < n_a else ("w_in_b", l - n_a))
        if l == n_a:
            keys.append(("w_kv", None))
        return keys

    def shard_of(key):
        n, i = key
        return as_rows(n, w_loc[n] if i is None else w_loc[n][i]).astype(BF16)

    W = {}
    ffn_names = ("w_ffn_up", "w_ffn_down")
    keys0 = [kk for kk in layer_keys(0) if kk[0] not in ffn_names]
    got0 = _ag_weights([shard_of(kk) for kk in keys0] + [w_loc[n] for n in SMALL_SHARDED],
                       [True] * len(keys0) + [False] * len(SMALL_SHARDED), name="ag_weights_0")
    W.update(zip(keys0, got0))
    for n, a in zip(SMALL_SHARDED, got0[len(keys0):]):
        W[n] = _small_to_natural(a)

    def gather_start(keys, after, tag):
        shards = [shard_of(kk) for kk in keys]
        lands = [lax.empty((N_DEV * s.shape[0],) + s.shape[1:], s.dtype) for s in shards]
        return (keys, tag) + _split_start(True, shards, lands, after, name=f"ag_start_{tag}")

    def gather_wait(pending, after):
        keys, tag, ssem, rsem, srcs, lands, _ = pending
        W.update(zip(keys, _split_wait(True, ssem, rsem, srcs, lands, after, name=f"ag_wait_{tag}")))

    pending_ffn0 = gather_start([kk for kk in layer_keys(0) if kk[0] in ffn_names], got0[0], "ffn_0")

    nblk, bsz = w_rg_r.shape[1], w_rg_r.shape[2]
    wbd = [jnp.concatenate([_block_diag(w_rg_r[j]), _block_diag(w_rg_i[j])], axis=1).astype(BF16) for j in range(n_a)]

    def vec(a):
        return a.reshape(1, -1)

    x2 = x.reshape(N, D)
    mem2 = mem.reshape(Bl * Ml, D)
    saved = []
    kvn = kv3 = x_kv = None
    xs = x2
    for l in range(depth):
        sv = {"x0": xs}
        g_pre = vec(g_mix_pre[l])
        if l + 1 < depth:
            pending = gather_start(layer_keys(l + 1), pending_ffn0[-1] if l == 0 else W[("w_mem_kv", l)], l + 1)
            g_pre = g_pre + pending[-1][0, 0]
        h1 = _rms_fwd(xs, g_pre, BF16, name=f"rms_mixpre_{l}")
        memn = _rms_fwd(mem2, vec(g_mem[l]), BF16, name=f"rms_mem_{l}")
        mkv3 = _mm(memn, W[("w_mem_kv", l)], name=f"mm_memkv_{l}").reshape(Bl, Ml, 2 * MEM_W)
        if l < n_a:
            j = l
            proj = _mm(h1, W[("w_in_a", j)], tb=True, name=f"mm_in_{l}")
            proj3 = proj.reshape(Bl, T, -1)
            xc3 = _conv_fwd_call(proj3, MIX_W, MIX_W, W["w_conv_a"][j], vec(W["b_conv_a"][j]), name=f"conv_a_{l}")
            gates3 = _mm(xc3.reshape(N, MIX_W), wbd[j], name=f"mm_gates_{l}").reshape(Bl, T, 2 * MIX_W)
            y_main3, hs3 = _rglru_fwd(xc3, gates3, proj3, vec(b_rg_r[j]), vec(b_rg_i[j]), vec(W["lru_lambda"][j]),
                                      name=f"rglru_fwd_{l}")
            q_off = 2 * MIX_W
            sv.update(xc3=xc3, gates3=gates3, hs3=hs3)
        else:
            j = l - n_a
            if l == n_a:
                x_kv = xs
                kvn = _rms_fwd(xs, vec(g_kv), BF16, name="rms_kv")
                kv3 = _mm(kvn, W[("w_kv", None)], name="mm_kv").reshape(Bl, T, 2 * MEM_W)
            proj = _mm(h1, W[("w_in_b", j)], name=f"mm_in_{l}")
            proj3 = proj.reshape(Bl, T, -1)
            y_main3 = _swa_fwd(proj3, kv3, sinks_b[j], name=f"swa_fwd_{l}")
            q_off = MIX_W
        y_mem3 = _mem_attn_fwd(proj3, q_off, mkv3, name=f"memattn_fwd_{l}")
        y_main = y_main3.reshape(N, MIX_W)
        y_mem = y_mem3.reshape(N, MEM_W)
        y = _mm(y_main, W[("w_mix_out", l)], n=D, k=MIX_W, name=f"mm_mixout_main_{l}")
        y = _mm(y_mem, W[("w_mix_out", l)], n=D, k=MEM_W, b_off=(MIX_W, 0), add=y, name=f"mm_mixout_mem_{l}")
        x1 = _rms_fwd(y, vec(g_mix_post[l]), F32, res=xs, name=f"rms_mixpost_{l}")
        h2 = _rms_fwd(x1, vec(g_ffn_pre[l]), BF16, name=f"rms_ffnpre_{l}")
        if l == 0:
            gather_wait(pending_ffn0, h2)
        ug = _mm(h2, W[("w_ffn_up", l)], tb=True, n=F, name=f"mm_up_g_{l}")
        uv = _mm(h2, W[("w_ffn_up", l)], tb=True, n=F, b_off=(F, 0), name=f"mm_up_v_{l}")
        ug3, uv3 = ug.reshape(Bl, T, F), uv.reshape(Bl, T, F)
        act3 = _ffn_mid_fwd(ug3, uv3, W["w_ffn_conv"][l], vec(b_ffn_conv[l]), name=f"ffn_mid_fwd_{l}")
        act = act3.reshape(N, F)
        f = _mm(act, W[("w_ffn_down", l)], name=f"mm_down_{l}")
        x_next = _rms_fwd(f, vec(g_ffn_post[l]), F32, res=x1, name=f"rms_ffnpost_{l}")
        if l + 1 < depth:
            gather_wait(pending, x_next)
        sv.update(h1=h1, memn=memn, mkv3=mkv3, proj3=proj3, q_off=q_off, y_main=y_main, y_mem=y_mem, y=y, x1=x1,
                  h2=h2, ug3=ug3, uv3=uv3, act=act, f=f)
        saved.append(sv)
        xs = x_next

    dxs, loss_vec = _loss_bwd(xs, loss_target.reshape(N, D))
    loss = lax.psum(jnp.sum(loss_vec), ("x", "y", "c"))

    G = {n: [None] * w_loc[n].shape[0] for n in REPL + SMALL_SHARDED if n != "g_kv"}
    GW = {}

    def dw(key, off, a, b_, nm):
        GW[key] = _mm(a, b_, ta=True, out_dtype=BF16, into=(GW.get(key), (1,) + W[key].shape, 0, off), name=nm)

    def grad_blocks(key):
        g = GW[key]
        return g.reshape(1, N_DEV, g.shape[1] // N_DEV, g.shape[2])

    reduces = []

    def reduce_start(keys, after, tag):
        srcs = [grad_blocks(kk) for kk in keys]
        lands = [lax.empty((len(FLIPS),) + s.shape[:1] + s.shape[2:], s.dtype) for s in srcs]
        started = _split_start(False, srcs, lands, after, name=f"rs_start_{tag}")
        reduces.append((keys, tag) + started)
        return started[-1]

    kv_parts = []
    for l in reversed(range(depth)):
        sv = saved[l]
        proj3 = sv["proj3"]
        df, dg = _rms_bwd(sv["f"], vec(g_ffn_post[l]), dxs, out_dtype=BF16, name=f"rmsb_ffnpost_{l}")
        G["g_ffn_post"][l] = dg[0]
        dact = _mm(df, W[("w_ffn_down", l)], tb=True, name=f"mmb_down_dx_{l}")
        dw(("w_ffn_down", l), (0, 0), sv["act"], df, f"mmb_down_dw_{l}")
        dug3, duv3, dwg, dwv, dbg, dbv = _ffn_mid_bwd(sv["ug3"], sv["uv3"], dact.reshape(Bl, T, F),
                                                      W["w_ffn_conv"][l], vec(b_ffn_conv[l]), name=f"ffn_mid_bwd_{l}")
        G["w_ffn_conv"][l] = jnp.concatenate([dwg, dwv], axis=1)
        G["b_ffn_conv"][l] = jnp.concatenate([dbg, dbv], axis=1)[0]
        dug, duv = dug3.reshape(N, F), duv3.reshape(N, F)
        dw(("w_ffn_up", l), (0, 0), dug, sv["h2"], f"mmb_up_dw_g_{l}")
        dw(("w_ffn_up", l), (F, 0), duv, sv["h2"], f"mmb_up_dw_v_{l}")
        tok = reduce_start([("w_ffn_down", l), ("w_ffn_up", l)], dug, f"ffn_{l}")
        dh2 = _mm(dug, W[("w_ffn_up", l)], n=D, k=F, after=tok, name=f"mmb_up_dx_g_{l}")
        dh2 = _mm(duv, W[("w_ffn_up", l)], n=D, k=F, b_off=(F, 0), add=dh2, name=f"mmb_up_dx_v_{l}")
        dx1, dg = _rms_bwd(sv["x1"], vec(g_ffn_pre[l]), dh2, add=dxs, name=f"rmsb_ffnpre_{l}")
        G["g_ffn_pre"][l] = dg[0]
        dy, dg = _rms_bwd(sv["y"], vec(g_mix_post[l]), dx1, out_dtype=BF16, name=f"rmsb_mixpost_{l}")
        G["g_mix_post"][l] = dg[0]
        dy_main = _mm(dy, W[("w_mix_out", l)], tb=True, n=MIX_W, k=D, name=f"mmb_mixout_dmain_{l}")
        dy_mem = _mm(dy, W[("w_mix_out", l)], tb=True, n=MEM_W, k=D, b_off=(MIX_W, 0),
                     name=f"mmb_mixout_dmem_{l}")
        dw(("w_mix_out", l), (0, 0), sv["y_main"], dy, f"mmb_mixout_dw_main_{l}")
        dw(("w_mix_out", l), (MIX_W, 0), sv["y_mem"], dy, f"mmb_mixout_dw_mem_{l}")
        dq_mem3, dmkv3 = _mem_attn_bwd(proj3, sv["q_off"], sv["mkv3"], dy_mem.reshape(Bl, T, MEM_W),
                                       name=f"memattn_bwd_{l}")
        dq_mem = dq_mem3.reshape(N, MEM_W)
        dmkv = dmkv3.reshape(Bl * Ml, 2 * MEM_W)
        dw(("w_mem_kv", l), (0, 0), sv["memn"], dmkv, f"mmb_memkv_dw_{l}")
        dmemn = _mm(dmkv, W[("w_mem_kv", l)], tb=True, name=f"mmb_memkv_dx_{l}")
        _, dg = _rms_bwd(mem2, vec(g_mem[l]), dmemn, name=f"rmsb_mem_{l}")
        G["g_mem"][l] = dg[0]
        dy_main3 = dy_main.reshape(Bl, T, MIX_W)
        if l < n_a:
            j = l
            dxc3, drp3, dip3, dugate3, dbr, dbi, dlam = _rglru_bwd(
                dy_main3, sv["xc3"], sv["gates3"], proj3, sv["hs3"], vec(b_rg_r[j]), vec(b_rg_i[j]),
                vec(W["lru_lambda"][j]), name=f"rglru_bwd_{l}")
            G["b_rg_r"][j] = dbr.reshape(nblk, bsz)
            G["b_rg_i"][j] = dbi.reshape(nblk, bsz)
            G["lru_lambda"][j] = dlam[0]
            drp, dip = drp3.reshape(N, MIX_W), dip3.reshape(N, MIX_W)
            xc2 = sv["xc3"].reshape(N, MIX_W)
            G["w_rg_r"][j] = _diag_blocks(_mm(xc2, drp, ta=True, name=f"mmb_gates_dw_r_{l}"), nblk, bsz)
            G["w_rg_i"][j] = _diag_blocks(_mm(xc2, dip, ta=True, name=f"mmb_gates_dw_i_{l}"), nblk, bsz)
            dxc = _mm(drp, wbd[j], tb=True, n=MIX_W, k=MIX_W, add=dxc3.reshape(N, MIX_W), name=f"mmb_gates_dx_r_{l}")
            dxc = _mm(dip, wbd[j], tb=True, n=MIX_W, k=MIX_W, b_off=(0, MIX_W), add=dxc, name=f"mmb_gates_dx_i_{l}")
            dux3, dwc, dbc = _conv_bwd_call(dxc.reshape(Bl, T, MIX_W), proj3, MIX_W, MIX_W, W["w_conv_a"][j],
                                            name=f"conv_a_bwd_{l}")
            G["w_conv_a"][j] = dwc
            G["b_conv_a"][j] = dbc[0]
            pieces = [(dugate3.reshape(N, MIX_W), 0), (dux3.reshape(N, MIX_W), MIX_W), (dq_mem, 2 * MIX_W)]
            in_key = ("w_in_a", j)
        else:
            j = l - n_a
            dq3, dkc, dkp, dsk = _swa_bwd(proj3, kv3, sinks_b[j], dy_main3, name=f"swa_bwd_{l}")
            kv_parts.append((dkc, dkp))
            G["sinks_b"][j] = dsk[0, :SWA_HEADS]
            pieces = [(dq3.reshape(N, MIX_W), 0), (dq_mem, MIX_W)]
            in_key = ("w_in_b", j)
        in_t = in_key[0] in TRANSPOSED
        for pi, (piece, off) in enumerate(pieces):
            if in_t:
                dw(in_key, (off, 0), piece, sv["h1"], f"mmb_in_dw_{pi}_{l}")
            else:
                dw(in_key, (0, off), sv["h1"], piece, f"mmb_in_dw_{pi}_{l}")
        tok = reduce_start([("w_mix_out", l), ("w_mem_kv", l), in_key], dy, f"mix_{l}")
        dh1 = None
        for pi, (piece, off) in enumerate(pieces):
            dh1 = _mm(piece, W[in_key], tb=not in_t, n=D, k=piece.shape[1], b_off=(off, 0) if in_t else (0, off),
                      add=dh1, after=tok if pi == 0 else None, name=f"mmb_in_dx_{pi}_{l}")
        dxs, dg = _rms_bwd(sv["x0"], vec(g_mix_pre[l]), dh1, add=dx1, name=f"rmsb_mixpre_{l}")
        G["g_mix_pre"][l] = dg[0]
        if l == n_a:
            dkv = _kv_grad_combine(kv_parts, name="kv_grad_combine").reshape(N, 2 * MEM_W)
            dw(("w_kv", None), (0, 0), kvn, dkv, "mmb_kv_dw")
            tok = reduce_start([("w_kv", None)], dkv, "kv")
            dkvn = _mm(dkv, W[("w_kv", None)], tb=True, after=tok, name="mmb_kv_dx")
            dxs, dg = _rms_bwd(x_kv, vec(g_kv), dkvn, add=dxs, name="rmsb_kv")
            G["g_kv"] = dg[0]
    grad_x = dxs.reshape(Bl, T, D)
    Gf = {n: (jnp.stack(g) if isinstance(g, list) else g) for n, g in G.items()}

    small4 = []
    for n in SMALL_SHARDED:
        t = _small_to_cols(Gf[n]).astype(BF16)
        small4.append(t.reshape(1, N_DEV, -1, t.shape[-1]))
    small_lands = [lax.empty((len(FLIPS),) + s.shape[:1] + s.shape[2:], s.dtype) for s in small4]
    small_started = _split_start(False, small4, small_lands, dxs, name="rs_start_small")
    r_blob = _pack([Gf[n].astype(F32) for n in REPL], REPL_ROWS, F32)
    r_parts = _all_gather(r_blob, name="ag_repl_grads")
    parts = {}
    for keys, tag, ssem, rsem, srcs, lands, _ in reduces:
        for kk, s, g7 in zip(keys, srcs, _split_wait(False, ssem, rsem, srcs, lands, small_started[-1],
                                                     name=f"rs_wait_{tag}")):
            parts[kk] = (s, g7)

    res = [{} for _ in range(4)]
    for n, _ in SHARDED:
        if n in SMALL_SHARDED:
            continue
        idx = [None] if w_loc[n].ndim == 2 else list(range(w_loc[n].shape[0]))
        wmv = [as_rows(n, a[n]) for a in (w_loc, m_loc, v_loc)]
        shp3 = (len(idx),) + wmv[0].shape[-2:]
        outs = _adamw_layers([parts[(n, i)][0] for i in idx], [parts[(n, i)][1] for i in idx],
                             *[a.reshape(shp3) for a in wmv], name=f"adamw_{n}")
        for k in range(4):
            res[k][n] = as_rows(n, outs[k].reshape(wmv[0].shape))
    last = res[0]["w_kv"]
    small_got = _split_wait(False, *small_started[:4], last, name="rs_wait_small")
    for n, own, g7 in zip(SMALL_SHARDED, small4, small_got):
        shp3 = own.shape[:1] + own.shape[2:]
        outs = _adamw_layers([own], [g7], w_loc[n].reshape(shp3), m_loc[n].reshape(shp3), v_loc[n].reshape(shp3),
                             name=f"adamw_{n}")
        for k in range(4):
            res[k][n] = outs[k].reshape(w_loc[n].shape)
    outs_rp = _adamw_replicated(r_parts, _pack([w_loc[n] for n in REPL], REPL_ROWS, F32),
                                _pack([m_loc[n] for n in REPL], REPL_ROWS, F32),
                                _pack([v_loc[n] for n in REPL], REPL_ROWS, F32),
                                name="adamw_replicated")
    rp_shapes = [w_loc[n].shape for n in REPL]
    for k in range(4):
        res[k].update(zip(REPL, _unpack(outs_rp[k], rp_shapes)))
    out = [loss, grad_x]
    for k in range(4):
        out += [res[k][n] for n in WEIGHTS]
    return tuple(out)
```

```python
import functools
import math

import numpy as np
import jax
import jax.numpy as jnp
from jax import lax
from jax.experimental import pallas as pl
from jax.experimental.pallas import tpu as pltpu

F32 = jnp.float32
BF16 = jnp.bfloat16
S = jax.ShapeDtypeStruct
MESH = pl.DeviceIdType.MESH
ANY = pl.BlockSpec(memory_space=pl.ANY)

HEAD = 64
MEM_HEADS = 4
MEM_W = MEM_HEADS * HEAD
SWA_HEADS = 12
SWA_GROUP = 3
MIX_W = SWA_HEADS * HEAD
WIN = 128
LRU_C = 8.0
EPS = 1e-6
ADAM_LR, ADAM_B1, ADAM_B2, ADAM_EPS, ADAM_WD, ADAM_STEP = 0.001, 0.9, 0.999, 1e-08, 0.01, 10
GELU_C0 = math.sqrt(2.0 / math.pi)
GELU_C1 = 0.044715
N_DEV = 8
LANES = 128
CT = 128
VMEM_LIMIT = 48 * 1024 * 1024
MM_VMEM_BUDGET = 36 * 1024 * 1024
REPL_ROWS = 256

SHARDED = (("w_mem_kv", 1), ("w_mix_out", 1), ("w_ffn_up", 2), ("w_ffn_conv", 2), ("w_ffn_down", 1), ("w_in_a", 2),
           ("w_conv_a", 2), ("b_conv_a", 1), ("lru_lambda", 1), ("w_in_b", 1), ("w_kv", 0))
SMALL_SHARDED = ("w_ffn_conv", "w_conv_a", "b_conv_a", "lru_lambda")
TRANSPOSED = ("w_ffn_up", "w_in_a")
REPL = ("g_mix_pre", "g_mix_post", "g_ffn_pre", "g_ffn_post", "g_mem", "b_ffn_conv", "w_rg_r", "b_rg_r", "w_rg_i",
        "b_rg_i", "sinks_b", "g_kv")
WEIGHTS = ("g_mix_pre", "g_mix_post", "g_ffn_pre", "g_ffn_post", "g_mem", "w_mem_kv", "w_mix_out", "w_ffn_up",
           "w_ffn_conv", "b_ffn_conv", "w_ffn_down", "w_in_a", "w_conv_a", "b_conv_a", "w_rg_r", "b_rg_r", "w_rg_i",
           "b_rg_i", "lru_lambda", "w_in_b", "sinks_b", "g_kv", "w_kv")


def _alibi_slopes(n):
    def pow2(m):
        start = 2.0 ** (-8.0 / m)
        return [start ** (i + 1) for i in range(m)]
    c = 2 ** int(math.floor(math.log2(n)))
    s = pow2(c)
    if c != n:
        s = s + pow2(2 * c)[0::2][: n - c]
    return [float(v) for v in np.asarray(s, dtype=np.float32)]


SLOPES = _alibi_slopes(SWA_HEADS)


def _tile(n, cands):
    for c in cands:
        if n % c == 0:
            return c
    return n


def _cparams(*sem):
    return pltpu.CompilerParams(dimension_semantics=sem, vmem_limit_bytes=VMEM_LIMIT)


def _mm_tiles(M, N, K, a_bytes, b_bytes, o_bytes, add_bytes, offsets):
    m_off, n_offs, k_off = offsets
    tms = [c for c in (1024, 512, 256, 128) if M % c == 0 and m_off % c == 0] or [M]
    tns = [c for c in (1408, 1024, 896, 768, 512, 384, 256, 128)
           if N % c == 0 and all(o % c == 0 for o in n_offs)] or [N]
    tks = [c for c in (K, 2048, 1408, 1024, 512, 256, 128) if c <= K and K % c == 0 and k_off % c == 0]
    best = None
    for tk in tks:
        fits = []
        for tm in tms:
            for tn in tns:
                need = 2 * (tm * tk * a_bytes + tk * tn * b_bytes + tm * tn * (o_bytes + add_bytes))
                need += tm * tn * 4 * (2 if tk < K else 1)
                need += (tm * tk * 2 if a_bytes != 2 else 0) + (tk * tn * 2 if b_bytes != 2 else 0)
                if need <= MM_VMEM_BUDGET:
                    fits.append((tm * tn, min(tm, 512), tm, tn))
        if fits:
            _, _, tm, tn = max(fits)
            best = (tm, tn, tk)
            break
    assert best is not None, (M, N, K)
    return best


def _mm(a, b, *, ta=False, tb=False, n=None, k=None, b_off=(0, 0), out_dtype=F32, add=None, into=None, after=None,
        name="mm"):
    if ta:
        K, M = a.shape
    else:
        M, K = a.shape
    if tb:
        N = b.shape[-2] if n is None else n
    else:
        N = b.shape[-1] if n is None else n
    assert k is None or k == K
    ro, co = b_off
    n_off, k_off = (ro, co) if tb else (co, ro)
    oro, oco = (0, 0) if into is None else into[3]
    tm, tn, tk = _mm_tiles(M, N, K, a.dtype.itemsize, b.dtype.itemsize, jnp.dtype(out_dtype).itemsize,
                           0 if add is None else add.dtype.itemsize, (oro, (n_off, oco), k_off))
    nk = K // tk
    if tb:
        b_spec = pl.BlockSpec((tn, tk), lambda i, j, kk: (j + ro // tn, kk + co // tk))
        b_dims = (1,)
    else:
        b_spec = pl.BlockSpec((tk, tn), lambda i, j, kk: (kk + ro // tk, j + co // tn))
        b_dims = (0,)
    if ta:
        a_spec = pl.BlockSpec((tk, tm), lambda i, j, kk: (kk, i))
        a_dims = (0,)
    else:
        a_spec = pl.BlockSpec((tm, tk), lambda i, j, kk: (i, kk))
        a_dims = (1,)
    dims = ((a_dims, b_dims), ((), ()))
    add_spec = pl.BlockSpec((tm, tn), lambda i, j, kk: (i, j))
    has_add = add is not None
    if into is None:
        o_spec, o_shape, buf = add_spec, (M, N), None
    else:
        buf, o_shape, ol, _ = into
        assert not has_add
        o_spec = pl.BlockSpec((None, tm, tn), lambda i, j, kk: (ol, i + oro // tm, j + oco // tn))
    has_buf = buf is not None

    def body(*refs):
        refs = list(refs)
        acc_ref = refs.pop() if nk > 1 else None
        o_ref = refs.pop()
        a_ref, b_ref = refs[0], refs[1]
        add_ref = refs[2] if has_add else None
        part = lax.dot_general(a_ref[...].astype(BF16), b_ref[...].astype(BF16), dims, preferred_element_type=F32)

        def finish(r):
            if has_add:
                r = r + add_ref[...].astype(F32)
            o_ref[...] = r.astype(out_dtype)

        if nk == 1:
            finish(part)
        else:
            kk = pl.program_id(2)

            @pl.when(kk == 0)
            def _():
                acc_ref[...] = part

            @pl.when(kk > 0)
            def _():
                acc_ref[...] += part

            @pl.when(kk == nk - 1)
            def _():
                finish(acc_ref[...])

    in_specs = [a_spec, b_spec] + ([add_spec] if has_add else []) + ([ANY] if has_buf else [])
    args = (a, b) + ((add,) if has_add else ()) + ((buf,) if has_buf else ())
    if after is not None:
        in_specs, args = in_specs + [ANY], args + (after,)
    return pl.pallas_call(
        body, grid=(M // tm, N // tn, nk), in_specs=in_specs, out_specs=o_spec,
        out_shape=S(o_shape, out_dtype), scratch_shapes=[pltpu.VMEM((tm, tn), F32)] if nk > 1 else [],
        input_output_aliases={2: 0} if has_buf else {},
        compiler_params=_cparams("parallel", "parallel", "arbitrary"), name=name)(*args)


def _mm_sum(pieces, *, tb=False, n, out_dtype=F32, add=None, after=None, name="mm_sum"):
    M = pieces[0][0].shape[0]
    ks = [a.shape[1] for a, _, _ in pieces]
    a_bytes = max(a.dtype.itemsize for a, _, _ in pieces)
    b_bytes = max(b.dtype.itemsize for _, b, _ in pieces)
    n_offs = tuple(off[0] if tb else off[1] for _, _, off in pieces)
    for kp, (_, _, off) in zip(ks, pieces):
        assert (off[1] if tb else off[0]) % kp == 0
    tm, tn, tk = _mm_tiles(M, n, sum(ks), a_bytes, b_bytes, jnp.dtype(out_dtype).itemsize, 0, (0, n_offs, 0))
    assert tk == sum(ks)
    a_specs = [pl.BlockSpec((tm, kp), lambda i, j: (i, 0)) for kp in ks]
    if tb:
        b_specs = [pl.BlockSpec((tn, kp), lambda i, j, ro=off[0], co=off[1], kp=kp: (j + ro // tn, co // kp))
                   for kp, (_, _, off) in zip(ks, pieces)]
        dims = NT
    else:
        b_specs = [pl.BlockSpec((kp, tn), lambda i, j, ro=off[0], co=off[1], kp=kp: (ro // kp, j + co // tn))
                   for kp, (_, _, off) in zip(ks, pieces)]
        dims = (((1,), (0,)), ((), ()))
    npc = len(pieces)
    o_spec = pl.BlockSpec((tm, tn), lambda i, j: (i, j))

    def body(*refs):
        o_ref = refs[2 * npc + (add is not None) + (after is not None)]
        acc = refs[2 * npc][...].astype(F32) if add is not None else None
        for p in range(npc):
            part = lax.dot_general(refs[p][...].astype(BF16), refs[npc + p][...].astype(BF16), dims,
                                   preferred_element_type=F32)
            acc = part if acc is None else acc + part
        o_ref[...] = acc.astype(out_dtype)

    args = [a for a, _, _ in pieces] + [b for _, b, _ in pieces]
    in_specs = a_specs + b_specs
    if add is not None:
        in_specs, args = in_specs + [o_spec], args + [add]
    if after is not None:
        in_specs, args = in_specs + [ANY], args + [after]
    return pl.pallas_call(
        body, grid=(M // tm, n // tn), in_specs=in_specs, out_specs=o_spec,
        out_shape=S((M, n), out_dtype), compiler_params=_cparams("parallel", "parallel"), name=name)(*args)


def _rms_fwd(x, g, out_dtype, res=None, name="rms_fwd"):
    N, D = x.shape
    tm = _tile(N, (512, 256, 128))
    has_res = res is not None

    def body(*refs):
        if has_res:
            x_ref, g_ref, r_ref, o_ref = refs
        else:
            x_ref, g_ref, o_ref = refs
        xv = x_ref[...].astype(F32)
        y = xv * lax.rsqrt(jnp.mean(xv * xv, axis=-1, keepdims=True) + EPS) * g_ref[...]
        if has_res:
            y = y + r_ref[...]
        o_ref[...] = y.astype(out_dtype)

    row = pl.BlockSpec((tm, D), lambda i: (i, 0))
    vec = pl.BlockSpec((1, D), lambda i: (0, 0))
    return pl.pallas_call(
        body, grid=(N // tm,), in_specs=[row, vec] + ([row] if has_res else []), out_specs=row,
        out_shape=S((N, D), out_dtype), compiler_params=_cparams("parallel"), name=name)(
            *((x, g) + ((res,) if has_res else ())))


def _rms_bwd(x, g, dy, add=None, out_dtype=F32, name="rms_bwd"):
    N, D = x.shape
    tm = _tile(N, (512, 256, 128))
    has_add = add is not None

    def body(*refs):
        if has_add:
            x_ref, g_ref, dy_ref, add_ref, dx_ref, dg_ref = refs
        else:
            x_ref, g_ref, dy_ref, dx_ref, dg_ref = refs
        xv = x_ref[...].astype(F32)
        dyv = dy_ref[...].astype(F32)
        r = lax.rsqrt(jnp.mean(xv * xv, axis=-1, keepdims=True) + EPS)
        u = dyv * g_ref[...]
        dx = r * u - xv * (r * r * r * jnp.mean(u * xv, axis=-1, keepdims=True))
        if has_add:
            dx = dx + add_ref[...]
        dx_ref[...] = dx.astype(out_dtype)

        @pl.when(pl.program_id(0) == 0)
        def _():
            dg_ref[...] = jnp.zeros_like(dg_ref)

        dg_ref[...] += jnp.sum(dyv * xv * r, axis=0, keepdims=True)

    row = pl.BlockSpec((tm, D), lambda i: (i, 0))
    vec = pl.BlockSpec((1, D), lambda i: (0, 0))
    return pl.pallas_call(
        body, grid=(N // tm,), in_specs=[row, vec, row] + ([row] if has_add else []), out_specs=(row, vec),
        out_shape=(S((N, D), out_dtype), S((1, D), F32)), compiler_params=_cparams("arbitrary"), name=name)(
            *((x, g, dy) + ((add,) if has_add else ())))


def _shift_down(x, s, row):
    return jnp.where(row >= s, pltpu.roll(x, s, axis=0), 0.0)


def _shift_up(x, s, row):
    T = x.shape[0]
    return jnp.where(row < T - s, pltpu.roll(x, T - s, axis=0), 0.0)


SLAB = 16


def _conv_wrap(x, w_ref, b_ref):
    W = w_ref.shape[0]
    y = x * w_ref[W - 1:W, :] + b_ref[...]
    for s in range(1, W):
        y = y + pltpu.roll(x, s, axis=0) * w_ref[W - 1 - s:W - s, :]
    return y


def _conv_masked(x, w_ref, b_ref, row):
    W = w_ref.shape[0]
    y = x * w_ref[W - 1:W, :] + b_ref[...]
    for s in range(1, W):
        y = y + _shift_down(x, s, row) * w_ref[W - 1 - s:W - s, :]
    return y


def _conv_head(x_head, w_ref, b_ref):
    return _conv_masked(x_head, w_ref, b_ref, lax.broadcasted_iota(jnp.int32, x_head.shape, 0))


def _conv_bwd_masked(dy, x, w_ref, row):
    W = w_ref.shape[0]
    dx = dy * w_ref[W - 1:W, :]
    dws = [None] * W
    dws[W - 1] = jnp.sum(dy * x, axis=0, keepdims=True)
    for s in range(1, W):
        dx = dx + _shift_up(dy, s, row) * w_ref[W - 1 - s:W - s, :]
        dws[W - 1 - s] = jnp.sum(dy * _shift_down(x, s, row), axis=0, keepdims=True)
    return dx, jnp.concatenate(dws, axis=0), jnp.sum(dy, axis=0, keepdims=True)


def _conv_bwd_wrap(dy, x, w_ref):
    W = w_ref.shape[0]
    T = dy.shape[0]
    dx = dy * w_ref[W - 1:W, :]
    dws = [None] * W
    dws[W - 1] = jnp.sum(dy * x, axis=0, keepdims=True)
    for s in range(1, W):
        up = pltpu.roll(dy, T - s, axis=0)
        dx = dx + up * w_ref[W - 1 - s:W - s, :]
        dws[W - 1 - s] = jnp.sum(up * x, axis=0, keepdims=True)
    return dx, jnp.concatenate(dws, axis=0), jnp.sum(dy, axis=0, keepdims=True)


def _conv_bwd_fix(dy_head, dy_tail, x_tail, w_ref):
    row = lax.broadcasted_iota(jnp.int32, dy_tail.shape, 0)
    W = w_ref.shape[0]
    dx = dy_tail * w_ref[W - 1:W, :]
    extra = [jnp.zeros((1, dy_tail.shape[1]), F32)] * W
    for s in range(1, W):
        dx = dx + _shift_up(dy_tail, s, row) * w_ref[W - 1 - s:W - s, :]
        extra[W - 1 - s] = jnp.sum(jnp.where(row < s, dy_head * pltpu.roll(x_tail, s, axis=0), 0.0), axis=0,
                                   keepdims=True)
    return dx, jnp.concatenate(extra, axis=0)


def _gelu(g):
    t = jnp.tanh(GELU_C0 * (g + GELU_C1 * g * g * g))
    return 0.5 * g * (1.0 + t), t


def _dgelu(g, t):
    return 0.5 * (1.0 + t) + 0.5 * g * (1.0 - t * t) * (GELU_C0 * (1.0 + 3.0 * GELU_C1 * g * g))


def _cspec(T, off=0):
    return pl.BlockSpec((1, T, CT), lambda j, b: (b, 0, j + off))


def _pspec(rows, off=0):
    return pl.BlockSpec((rows, CT), lambda j, b: (0, j + off))


def _conv_fwd_call(x3, x_off, C, w, b, name):
    Bl, T, _ = x3.shape
    W = w.shape[0]

    def body(x_ref, w_ref, b_ref, o_ref):
        o_ref[0] = _conv_wrap(x_ref[0], w_ref, b_ref)
        o_ref[0, 0:SLAB, :] = _conv_head(x_ref[0, 0:SLAB, :], w_ref, b_ref)

    return pl.pallas_call(
        body, grid=(C // CT, Bl), in_specs=[_cspec(T, x_off // CT), _pspec(W), _pspec(1)], out_specs=_cspec(T),
        out_shape=S((Bl, T, C), F32), compiler_params=_cparams("parallel", "arbitrary"), name=name)(x3, w, b)


def _conv_bwd_call(dy3, x3, x_off, C, w, name):
    Bl, T, _ = x3.shape
    W = w.shape[0]

    def body(dy_ref, x_ref, w_ref, dx_ref, dw_ref, db_ref):
        dx, dw, db = _conv_bwd_wrap(dy_ref[0], x_ref[0], w_ref)
        dx_tail, dw_extra = _conv_bwd_fix(dy_ref[0, 0:SLAB, :], dy_ref[0, T - SLAB:T, :], x_ref[0, T - SLAB:T, :],
                                          w_ref)
        dx_ref[0] = dx.astype(BF16)
        dx_ref[0, T - SLAB:T, :] = dx_tail.astype(BF16)

        @pl.when(pl.program_id(1) == 0)
        def _():
            dw_ref[...] = jnp.zeros_like(dw_ref)
            db_ref[...] = jnp.zeros_like(db_ref)

        dw_ref[...] += dw - dw_extra
        db_ref[...] += db

    return pl.pallas_call(
        body, grid=(C // CT, Bl), in_specs=[_cspec(T), _cspec(T, x_off // CT), _pspec(W)],
        out_specs=(_cspec(T), _pspec(W), _pspec(1)),
        out_shape=(S((Bl, T, C), BF16), S((W, C), F32), S((1, C), F32)),
        compiler_params=_cparams("parallel", "arbitrary"), name=name)(dy3, x3, w)


def _ffn_mid_fwd(u3, wc, bc, name):
    Bl, T, F2 = u3.shape
    F = F2 // 2
    nf = F // CT

    def body(ug_ref, uv_ref, wg_ref, wv_ref, bg_ref, bv_ref, o_ref):
        g = _conv_wrap(ug_ref[0], wg_ref, bg_ref)
        v = _conv_wrap(uv_ref[0], wv_ref, bv_ref)
        o_ref[0] = (_gelu(g)[0] * v).astype(BF16)
        g = _conv_head(ug_ref[0, 0:SLAB, :], wg_ref, bg_ref)
        v = _conv_head(uv_ref[0, 0:SLAB, :], wv_ref, bv_ref)
        o_ref[0, 0:SLAB, :] = (_gelu(g)[0] * v).astype(BF16)

    return pl.pallas_call(
        body, grid=(nf, Bl),
        in_specs=[_cspec(T), _cspec(T, nf), _pspec(3), _pspec(3, nf), _pspec(1), _pspec(1, nf)], out_specs=_cspec(T),
        out_shape=S((Bl, T, F), BF16), compiler_params=_cparams("parallel", "arbitrary"), name=name)(
            u3, u3, wc, wc, bc, bc)


def _ffn_mid_bwd(u3, dact3, wc, bc, name):
    Bl, T, F2 = u3.shape
    F = F2 // 2
    nf = F // CT

    def body(ug_ref, uv_ref, da_ref, wg_ref, wv_ref, bg_ref, bv_ref, dug_ref, duv_ref, dwg_ref, dwv_ref, dbg_ref,
             dbv_ref):
        row = lax.broadcasted_iota(jnp.int32, (T, CT), 0)
        ug = ug_ref[0]
        uv = uv_ref[0]
        g = _conv_masked(ug, wg_ref, bg_ref, row)
        v = _conv_masked(uv, wv_ref, bv_ref, row)
        da = da_ref[0]
        gel, t = _gelu(g)
        dg = da * v * _dgelu(g, t)
        dv = da * gel
        dug, dwg, dbg = _conv_bwd_masked(dg, ug, wg_ref, row)
        duv, dwv, dbv = _conv_bwd_masked(dv, uv, wv_ref, row)
        dug_ref[0] = dug.astype(BF16)
        duv_ref[0] = duv.astype(BF16)

        @pl.when(pl.program_id(1) == 0)
        def _():
            dwg_ref[...] = jnp.zeros_like(dwg_ref)
            dwv_ref[...] = jnp.zeros_like(dwv_ref)
            dbg_ref[...] = jnp.zeros_like(dbg_ref)
            dbv_ref[...] = jnp.zeros_like(dbv_ref)

        dwg_ref[...] += dwg
        dwv_ref[...] += dwv
        dbg_ref[...] += dbg
        dbv_ref[...] += dbv

    return pl.pallas_call(
        body, grid=(nf, Bl),
        in_specs=[_cspec(T), _cspec(T, nf), _cspec(T), _pspec(3), _pspec(3, nf), _pspec(1), _pspec(1, nf)],
        out_specs=(_cspec(T), _cspec(T), _pspec(3), _pspec(3), _pspec(1), _pspec(1)),
        out_shape=(S((Bl, T, F), BF16), S((Bl, T, F), BF16), S((3, F), F32), S((3, F), F32), S((1, F), F32),
                   S((1, F), F32)),
        compiler_params=_cparams("parallel", "arbitrary"), name=name)(u3, u3, dact3, wc, wc, bc, bc)


def _lru_gates(xc, rp, ip, br_ref, bi_ref, lam_ref):
    r = jax.nn.sigmoid(rp + br_ref[...])
    i = jax.nn.sigmoid(ip + bi_ref[...])
    lam = lam_ref[...]
    sp = jnp.maximum(-lam, 0.0) + jnp.log1p(jnp.exp(-jnp.abs(lam)))
    log_a = (-LRU_C) * r * sp
    a = jnp.exp(log_a)
    z = 2.0 * log_a
    one_m_a2 = jnp.where(z > -0.05, -z * (1.0 + z * (0.5 + z * (1.0 / 6.0 + z * (1.0 / 24.0)))), 1.0 - a * a)
    mult = jnp.sqrt(one_m_a2)
    return r, i, sp, a, mult


def _rglru_fwd(xc3, gates3, proj3, br, bi, lam, name):
    Bl, T, C = xc3.shape
    nsteps = int(math.log2(T))
    assert 1 << nsteps == T

    def body(xc_ref, rp_ref, ip_ref, ug_ref, br_ref, bi_ref, lam_ref, y_ref, h_ref):
        row = lax.broadcasted_iota(jnp.int32, (T, CT), 0)
        xc = xc_ref[0]
        r, i, sp, a, mult = _lru_gates(xc, rp_ref[0], ip_ref[0], br_ref, bi_ref, lam_ref)
        b = mult * (i * xc)
        for st in range(nsteps):
            s = 1 << st
            a_sh = jnp.where(row >= s, pltpu.roll(a, s, axis=0), 1.0)
            b = a * _shift_down(b, s, row) + b
            a = a * a_sh
        h_ref[0] = b
        y_ref[0] = (b * _gelu(ug_ref[0])[0]).astype(BF16)

    return pl.pallas_call(
        body, grid=(C // CT, Bl),
        in_specs=[_cspec(T), _cspec(T), _cspec(T, C // CT), _cspec(T), _pspec(1), _pspec(1), _pspec(1)],
        out_specs=(_cspec(T), _cspec(T)), out_shape=(S((Bl, T, C), BF16), S((Bl, T, C), F32)),
        compiler_params=_cparams("parallel", "arbitrary"), name=name)(xc3, gates3, gates3, proj3, br, bi, lam)


def _rglru_bwd(dy3, xc3, gates3, proj3, h3, br, bi, lam, name):
    Bl, T, C = xc3.shape
    nsteps = int(math.log2(T))

    def body(dy_ref, xc_ref, rp_ref, ip_ref, ug_ref, h_ref, br_ref, bi_ref, lam_ref,
             dxc_ref, drp_ref, dip_ref, dug_ref, dbr_ref, dbi_ref, dlam_ref):
        row = lax.broadcasted_iota(jnp.int32, (T, CT), 0)
        xc = xc_ref[0]
        r, i, sp, a, mult = _lru_gates(xc, rp_ref[0], ip_ref[0], br_ref, bi_ref, lam_ref)
        h = h_ref[0]
        dy = dy_ref[0]
        ug = ug_ref[0]
        gel, t = _gelu(ug)
        dug_ref[0] = (dy * h * _dgelu(ug, t)).astype(BF16)
        gacc = dy * gel
        an = _shift_up(a, 1, row)
        for st in range(nsteps):
            s = 1 << st
            an_sh = jnp.where(row < T - s, pltpu.roll(an, T - s, axis=0), 1.0)
            gacc = an * _shift_up(gacc, s, row) + gacc
            an = an * an_sh
        da = gacc * _shift_down(h, 1, row)
        ix = i * xc
        d_mult = gacc * ix
        d_i = gacc * mult * xc
        dxc_ref[0] = gacc * mult * i
        d_log_a = da * a - d_mult * (a * a) / mult
        d_r = d_log_a * ((-LRU_C) * sp)
        d_sp = jnp.sum(d_log_a * ((-LRU_C) * r), axis=0, keepdims=True)
        drp = d_r * r * (1.0 - r)
        dip = d_i * i * (1.0 - i)
        drp_ref[0] = drp.astype(BF16)
        dip_ref[0] = dip.astype(BF16)

        @pl.when(pl.program_id(1) == 0)
        def _():
            dbr_ref[...] = jnp.zeros_like(dbr_ref)
            dbi_ref[...] = jnp.zeros_like(dbi_ref)
            dlam_ref[...] = jnp.zeros_like(dlam_ref)

        dbr_ref[...] += jnp.sum(drp, axis=0, keepdims=True)
        dbi_ref[...] += jnp.sum(dip, axis=0, keepdims=True)
        dlam_ref[...] += d_sp * (-jax.nn.sigmoid(-lam_ref[...]))

    vec = S((1, C), F32)
    act = S((Bl, T, C), BF16)
    return pl.pallas_call(
        body, grid=(C // CT, Bl),
        in_specs=[_cspec(T), _cspec(T), _cspec(T), _cspec(T, C // CT), _cspec(T), _cspec(T)] + [_pspec(1)] * 3,
        out_specs=(_cspec(T), _cspec(T), _cspec(T), _cspec(T), _pspec(1), _pspec(1), _pspec(1)),
        out_shape=(S((Bl, T, C), F32), act, act, act, vec, vec, vec),
        compiler_params=_cparams("parallel", "arbitrary"), name=name)(dy3, xc3, gates3, gates3, proj3, h3, br, bi, lam)


NT = (((1,), (1,)), ((), ()))
TN = (((0,), (0,)), ((), ()))


def _hs(h):
    return slice(h * HEAD, (h + 1) * HEAD)


def _head_rows(x):
    head = lax.shift_right_logical(lax.broadcasted_iota(jnp.int32, x.shape, 1), HEAD.bit_length() - 1)
    return jnp.concatenate([jnp.where(head == h, x, jnp.zeros_like(x)) for h in range(MEM_HEADS)], axis=0)


def _head_sum(xbd):
    M = xbd.shape[0] // MEM_HEADS
    head = lax.shift_right_logical(lax.broadcasted_iota(jnp.int32, (M, xbd.shape[1]), 1), HEAD.bit_length() - 1)
    out = jnp.zeros((M, xbd.shape[1]), xbd.dtype)
    for h in range(MEM_HEADS):
        out = jnp.where(head == h, xbd[h * M:(h + 1) * M], out)
    return out


def _mem_probs(q, kbd):
    M = kbd.shape[0] // MEM_HEADS
    s = lax.dot_general(q, kbd, NT, preferred_element_type=F32) * (HEAD ** -0.5)
    ps = []
    for h in range(MEM_HEADS):
        sh = s[:, h * M:(h + 1) * M]
        e = jnp.exp(sh - jnp.max(sh, axis=-1, keepdims=True))
        ps.append(e / jnp.sum(e, axis=-1, keepdims=True))
    return ps


def _mem_attn_fwd(proj3, q_off, mkv3, name):
    Bl, T, _ = proj3.shape
    M = mkv3.shape[1]
    tq = _tile(T, (512, 256, 128))

    def body(q_ref, k_ref, v_ref, o_ref):
        q = q_ref[0].astype(BF16)
        kbd = _head_rows(k_ref[0].astype(BF16))
        vbd = _head_rows(v_ref[0].astype(BF16))
        p = jnp.concatenate(_mem_probs(q, kbd), axis=-1).astype(BF16)
        o_ref[0] = jnp.dot(p, vbd, preferred_element_type=F32).astype(BF16)

    return pl.pallas_call(
        body, grid=(Bl, T // tq),
        in_specs=[pl.BlockSpec((1, tq, MEM_W), lambda b, t: (b, t, q_off // MEM_W)),
                  pl.BlockSpec((1, M, MEM_W), lambda b, t: (b, 0, 0)),
                  pl.BlockSpec((1, M, MEM_W), lambda b, t: (b, 0, 1))],
        out_specs=pl.BlockSpec((1, tq, MEM_W), lambda b, t: (b, t, 0)),
        out_shape=S((Bl, T, MEM_W), BF16), compiler_params=_cparams("parallel", "parallel"), name=name)(
            proj3, mkv3, mkv3)


def _mem_attn_bwd(proj3, q_off, mkv3, do3, name):
    Bl, T, _ = proj3.shape
    M = mkv3.shape[1]
    tq = _tile(T, (512, 256, 128))
    scale = HEAD ** -0.5

    def body(q_ref, k_ref, v_ref, do_ref, dq_ref, dkv_ref):
        q = q_ref[0].astype(BF16)
        kbd = _head_rows(k_ref[0].astype(BF16))
        vbd = _head_rows(v_ref[0].astype(BF16))
        do = do_ref[0].astype(BF16)
        ps = _mem_probs(q, kbd)
        dvbd = lax.dot_general(jnp.concatenate(ps, axis=-1).astype(BF16), do, TN, preferred_element_type=F32)
        dp = lax.dot_general(do, vbd, NT, preferred_element_type=F32)
        dss = []
        for h in range(MEM_HEADS):
            dph = dp[:, h * M:(h + 1) * M]
            dss.append(ps[h] * (dph - jnp.sum(ps[h] * dph, axis=-1, keepdims=True)) * scale)
        ds = jnp.concatenate(dss, axis=-1).astype(BF16)
        dq_ref[0] = jnp.dot(ds, kbd, preferred_element_type=F32).astype(BF16)
        dkbd = lax.dot_general(ds, q, TN, preferred_element_type=F32)

        @pl.when(pl.program_id(1) == 0)
        def _():
            dkv_ref[...] = jnp.zeros_like(dkv_ref)

        dkv_ref[0] += jnp.concatenate([_head_sum(dkbd), _head_sum(dvbd)], axis=-1)

    return pl.pallas_call(
        body, grid=(Bl, T // tq),
        in_specs=[pl.BlockSpec((1, tq, MEM_W), lambda b, t: (b, t, q_off // MEM_W)),
                  pl.BlockSpec((1, M, MEM_W), lambda b, t: (b, 0, 0)),
                  pl.BlockSpec((1, M, MEM_W), lambda b, t: (b, 0, 1)),
                  pl.BlockSpec((1, tq, MEM_W), lambda b, t: (b, t, 0))],
        out_specs=(pl.BlockSpec((1, tq, MEM_W), lambda b, t: (b, t, 0)),
                   pl.BlockSpec((1, M, 2 * MEM_W), lambda b, t: (b, 0, 0))),
        out_shape=(S((Bl, T, MEM_W), BF16), S((Bl, M, 2 * MEM_W), F32)),
        compiler_params=_cparams("parallel", "arbitrary"), name=name)(proj3, mkv3, mkv3, do3)


GROUP_ROWS = SWA_GROUP * WIN


def _group_rows(x, kvh):
    return jnp.concatenate([x[:, _hs(SWA_GROUP * kvh + g)] for g in range(SWA_GROUP)], axis=0)


def _group_col(vals):
    grp = lax.shift_right_logical(lax.broadcasted_iota(jnp.int32, (GROUP_ROWS, 1), 0), WIN.bit_length() - 1)
    col = jnp.full((GROUP_ROWS, 1), vals[-1], F32)
    for g in range(SWA_GROUP - 2, -1, -1):
        col = jnp.where(grp == g, vals[g], col)
    return col


def _swa_probs(qh, kph, kch, sink, slope, has_prev):
    qi = jnp.bitwise_and(lax.broadcasted_iota(jnp.int32, (GROUP_ROWS, WIN), 0), WIN - 1)
    kj = lax.broadcasted_iota(jnp.int32, (GROUP_ROWS, WIN), 1)
    scale = HEAD ** -0.5
    sp = lax.dot_general(qh, kph, NT, preferred_element_type=F32) * scale
    sc = lax.dot_general(qh, kch, NT, preferred_element_type=F32) * scale
    dist_p = (qi + WIN - kj).astype(F32)
    dist_c = (qi - kj).astype(F32)
    neg = -jnp.inf
    sp = jnp.where(kj > qi + jnp.where(has_prev, 0, WIN), sp - slope * dist_p, neg)
    sc = jnp.where(kj <= qi, sc - slope * dist_c, neg)
    m = jnp.maximum(jnp.maximum(jnp.max(sp, axis=-1, keepdims=True), jnp.max(sc, axis=-1, keepdims=True)), sink)
    ep = jnp.exp(sp - m)
    ec = jnp.exp(sc - m)
    es = jnp.exp(sink - m)
    inv = 1.0 / (jnp.sum(ep, axis=-1, keepdims=True) + jnp.sum(ec, axis=-1, keepdims=True) + es)
    return ep * inv, ec * inv, es * inv


def _swa_specs(nb):
    prev = lambda n: jnp.maximum(n - 1, 0)
    q = pl.BlockSpec((1, WIN, MIX_W), lambda b, n: (b, n, 0))
    kp = pl.BlockSpec((1, WIN, MEM_W), lambda b, n: (b, prev(n), 0))
    kc = pl.BlockSpec((1, WIN, MEM_W), lambda b, n: (b, n, 0))
    vp = pl.BlockSpec((1, WIN, MEM_W), lambda b, n: (b, prev(n), 1))
    vc = pl.BlockSpec((1, WIN, MEM_W), lambda b, n: (b, n, 1))
    sm = pl.BlockSpec(memory_space=pltpu.SMEM)
    return q, kp, kc, vp, vc, sm


def _swa_fwd(proj3, kv3, sinks, name):
    Bl, T, _ = proj3.shape
    nb = T // WIN
    q_s, kp_s, kc_s, vp_s, vc_s, sm = _swa_specs(nb)

    def body(q_ref, kp_ref, kc_ref, vp_ref, vc_ref, sink_ref, o_ref):
        has_prev = pl.program_id(1) > 0
        q = q_ref[0].astype(BF16)
        kp, kc = kp_ref[0].astype(BF16), kc_ref[0].astype(BF16)
        vp, vc = vp_ref[0].astype(BF16), vc_ref[0].astype(BF16)
        outs = []
        for kvh in range(SWA_HEADS // SWA_GROUP):
            kvs = _hs(kvh)
            heads = range(SWA_GROUP * kvh, SWA_GROUP * (kvh + 1))
            pp, pc, _ = _swa_probs(_group_rows(q, kvh), kp[:, kvs], kc[:, kvs], _group_col([sink_ref[h] for h in heads]),
                                   _group_col([SLOPES[h] for h in heads]), has_prev)
            og = (jnp.dot(pp.astype(BF16), vp[:, kvs], preferred_element_type=F32)
                  + jnp.dot(pc.astype(BF16), vc[:, kvs], preferred_element_type=F32))
            outs += [og[g * WIN:(g + 1) * WIN] for g in range(SWA_GROUP)]
        o_ref[0] = jnp.concatenate(outs, axis=-1).astype(BF16)

    return pl.pallas_call(
        body, grid=(Bl, nb), in_specs=[q_s, kp_s, kc_s, vp_s, vc_s, sm], out_specs=q_s,
        out_shape=S((Bl, T, MIX_W), BF16), compiler_params=_cparams("parallel", "parallel"), name=name)(
            proj3, kv3, kv3, kv3, kv3, sinks)


def _swa_bwd(proj3, kv3, sinks, do3, name):
    Bl, T, _ = proj3.shape
    nb = T // WIN
    q_s, kp_s, kc_s, vp_s, vc_s, sm = _swa_specs(nb)
    kv_s = pl.BlockSpec((1, WIN, 2 * MEM_W), lambda b, n: (b, n, 0))
    sk_s = pl.BlockSpec((8, LANES), lambda b, n: (0, 0))
    scale = HEAD ** -0.5

    def body(q_ref, kp_ref, kc_ref, vp_ref, vc_ref, sink_ref, do_ref, dq_ref, dkc_ref, dkp_ref, dsk_ref):
        has_prev = pl.program_id(1) > 0
        q = q_ref[0].astype(BF16)
        kp, kc = kp_ref[0].astype(BF16), kc_ref[0].astype(BF16)
        vp, vc = vp_ref[0].astype(BF16), vc_ref[0].astype(BF16)
        do = do_ref[0].astype(BF16)
        lane = lax.broadcasted_iota(jnp.int32, (8, LANES), 1)
        srow = lax.broadcasted_iota(jnp.int32, (8, LANES), 0)
        dsk = jnp.zeros((8, LANES), F32)
        dqs = []
        dkc, dkp, dvc, dvp = [], [], [], []
        grp = lax.shift_right_logical(lax.broadcasted_iota(jnp.int32, (GROUP_ROWS, 1), 0), WIN.bit_length() - 1)
        for kvh in range(SWA_HEADS // SWA_GROUP):
            kvs = _hs(kvh)
            heads = range(SWA_GROUP * kvh, SWA_GROUP * (kvh + 1))
            qg, dog = _group_rows(q, kvh), _group_rows(do, kvh)
            pp, pc, ps = _swa_probs(qg, kp[:, kvs], kc[:, kvs], _group_col([sink_ref[h] for h in heads]),
                                    _group_col([SLOPES[h] for h in heads]), has_prev)
            dpp = lax.dot_general(dog, vp[:, kvs], NT, preferred_element_type=F32)
            dpc = lax.dot_general(dog, vc[:, kvs], NT, preferred_element_type=F32)
            delta = jnp.sum(pp * dpp, axis=-1, keepdims=True) + jnp.sum(pc * dpc, axis=-1, keepdims=True)
            dsp = (pp * (dpp - delta) * scale).astype(BF16)
            dsc = (pc * (dpc - delta) * scale).astype(BF16)
            dqg = (jnp.dot(dsp, kp[:, kvs], preferred_element_type=F32)
                   + jnp.dot(dsc, kc[:, kvs], preferred_element_type=F32))
            dqs += [dqg[g * WIN:(g + 1) * WIN] for g in range(SWA_GROUP)]
            dkc.append(lax.dot_general(dsc, qg, TN, preferred_element_type=F32))
            dkp.append(lax.dot_general(dsp, qg, TN, preferred_element_type=F32))
            dvc.append(lax.dot_general(pc.astype(BF16), dog, TN, preferred_element_type=F32))
            dvp.append(lax.dot_general(pp.astype(BF16), dog, TN, preferred_element_type=F32))
            dsink = ps * delta
            for g, h in enumerate(heads):
                dsk = dsk + jnp.where((lane == h) & (srow == 0), -jnp.sum(jnp.where(grp == g, dsink, 0.0)), 0.0)
        dq_ref[0] = jnp.concatenate(dqs, axis=-1).astype(BF16)
        dkc_ref[0] = jnp.concatenate(dkc + dvc, axis=-1)
        dkp_ref[0] = jnp.concatenate(dkp + dvp, axis=-1)

        @pl.when((pl.program_id(0) == 0) & (pl.program_id(1) == 0))
        def _():
            dsk_ref[...] = jnp.zeros_like(dsk_ref)

        dsk_ref[...] += dsk

    return pl.pallas_call(
        body, grid=(Bl, nb), in_specs=[q_s, kp_s, kc_s, vp_s, vc_s, sm, q_s], out_specs=(q_s, kv_s, kv_s, sk_s),
        out_shape=(S((Bl, T, MIX_W), BF16), S((Bl, T, 2 * MEM_W), F32), S((Bl, T, 2 * MEM_W), F32), S((8, LANES), F32)),
        compiler_params=_cparams("arbitrary", "arbitrary"), name=name)(proj3, kv3, kv3, kv3, kv3, sinks, do3)


def _kv_grad_combine(parts, name):
    Bl, T, W = parts[0][0].shape
    nb = T // WIN
    nl = len(parts)

    def body(*refs):
        o_ref = refs[-1]
        has_next = jnp.where(pl.program_id(1) == nb - 1, 0.0, 1.0)
        acc = None
        for l in range(nl):
            c = refs[2 * l][0] + has_next * refs[2 * l + 1][0]
            acc = c if acc is None else acc + c
        o_ref[0] = acc.astype(BF16)

    cur = pl.BlockSpec((1, WIN, W), lambda b, n: (b, n, 0))
    nxt = pl.BlockSpec((1, WIN, W), lambda b, n: (b, jnp.minimum(n + 1, nb - 1), 0))
    return pl.pallas_call(
        body, grid=(Bl, nb), in_specs=[cur, nxt] * nl, out_specs=cur, out_shape=S((Bl, T, W), BF16),
        compiler_params=_cparams("parallel", "parallel"), name=name)(*[a for pr in parts for a in pr])


def _loss_bwd(y, target, name="loss"):
    N, D = y.shape
    tm = _tile(N, (512, 256, 128))

    def body(y_ref, t_ref, dy_ref, l_ref):
        e = y_ref[...] - t_ref[...]
        dy_ref[...] = e * (1.0 / D)

        @pl.when(pl.program_id(0) == 0)
        def _():
            l_ref[...] = jnp.zeros_like(l_ref)

        l_ref[...] += jnp.sum(e * e, axis=0, keepdims=True) * (0.5 / D)

    row = pl.BlockSpec((tm, D), lambda i: (i, 0))
    vec = pl.BlockSpec((1, D), lambda i: (0, 0))
    return pl.pallas_call(
        body, grid=(N // tm,), in_specs=[row, row], out_specs=(row, vec), out_shape=(S((N, D), F32), S((1, D), F32)),
        compiler_params=_cparams("arbitrary"), name=name)(y, target)


def _all_gather(x, name):
    R, C = x.shape

    def body(x_ref, out_ref, send_sems, recv_sems, local_sem):
        mx, my, mc = lax.axis_index("x"), lax.axis_index("y"), lax.axis_index("c")
        me, sibling = (mx, my, mc), (mx, my, 1 - mc)
        chips = [(1 - mx, my), (mx, 1 - my), (1 - mx, 1 - my)]

        def rows(px, py, pc):
            return out_ref.at[4 * px + 2 * py + pc]

        def copy(kk, block, to, src=None):
            return pltpu.make_async_remote_copy(
                src_ref=rows(*block) if src is None else src, dst_ref=rows(*block), send_sem=send_sems.at[kk],
                recv_sem=recv_sems.at[kk], device_id=to, device_id_type=MESH)

        mine = pltpu.make_async_copy(x_ref, rows(*me), local_sem)
        mine.start()
        first = [copy(0, me, sibling, src=x_ref)]
        first += [copy(1 + j, me, (*chip, mc), src=x_ref) for j, chip in enumerate(chips)]
        for cp in first:
            cp.start()
        passed = [copy(4 + j, (*chip, mc), sibling) for j, chip in enumerate(chips)]
        for j, chip in enumerate(chips):
            copy(1 + j, (*chip, mc), me).wait_recv()
            passed[j].start()
        copy(0, sibling, me).wait_recv()
        for j, chip in enumerate(chips):
            copy(4 + j, (*chip, 1 - mc), me).wait_recv()
        for cp in first + passed:
            cp.wait_send()
        mine.wait()

    return pl.pallas_call(
        body, out_shape=S((N_DEV, R, C), x.dtype), in_specs=[ANY], out_specs=ANY,
        scratch_shapes=[pltpu.SemaphoreType.DMA((7,)), pltpu.SemaphoreType.DMA((7,)), pltpu.SemaphoreType.DMA(())],
        name=name)(x)


def _ag_weights(shards, row_sharded, name):
    n = len(shards)

    def full_shape(a, rows):
        if rows:
            return a.shape[:-2] + (N_DEV * a.shape[-2],) + a.shape[-1:]
        return (N_DEV,) + a.shape

    def body(*refs):
        x_refs, o_refs = refs[:n], refs[n:2 * n]
        send_sems, recv_sems, local_sems = refs[2 * n:]
        mx, my, mc = lax.axis_index("x"), lax.axis_index("y"), lax.axis_index("c")
        me, sibling = (mx, my, mc), (mx, my, 1 - mc)
        chips = [(1 - mx, my), (mx, 1 - my), (1 - mx, 1 - my)]

        def dst(t, px, py, pc):
            d = 4 * px + 2 * py + pc
            if not row_sharded[t]:
                return o_refs[t].at[d]
            r = shards[t].shape[-2]
            idx = (slice(None),) * (shards[t].ndim - 2) + (pl.ds(pl.multiple_of(d * r, 16), r), slice(None))
            return o_refs[t].at[idx]

        def copy(kk, t, block, to, src=None):
            return pltpu.make_async_remote_copy(
                src_ref=dst(t, *block) if src is None else src, dst_ref=dst(t, *block),
                send_sem=send_sems.at[kk * n + t], recv_sem=recv_sems.at[kk * n + t], device_id=to,
                device_id_type=MESH)

        mine = [pltpu.make_async_copy(x_refs[t], dst(t, *me), local_sems.at[t]) for t in range(n)]
        for cp in mine:
            cp.start()
        first = []
        for t in range(n):
            first.append(copy(0, t, me, sibling, src=x_refs[t]))
            first += [copy(1 + j, t, me, (*chip, mc), src=x_refs[t]) for j, chip in enumerate(chips)]
        for cp in first:
            cp.start()
        passed = []
        for j, chip in enumerate(chips):
            for t in range(n):
                copy(1 + j, t, (*chip, mc), me).wait_recv()
                cp = copy(4 + j, t, (*chip, mc), sibling)
                cp.start()
                passed.append(cp)
        for t in range(n):
            copy(0, t, sibling, me).wait_recv()
            for j, chip in enumerate(chips):
                copy(4 + j, t, (*chip, 1 - mc), me).wait_recv()
        for cp in first + passed:
            cp.wait_send()
        for cp in mine:
            cp.wait()

    return pl.pallas_call(
        body, out_shape=tuple(S(full_shape(a, r), a.dtype) for a, r in zip(shards, row_sharded)),
        in_specs=[ANY] * n, out_specs=tuple([ANY] * n),
        scratch_shapes=[pltpu.SemaphoreType.DMA((7 * n,)), pltpu.SemaphoreType.DMA((7 * n,)),
                        pltpu.SemaphoreType.DMA((n,))],
        name=name)(*shards)


FLIPS = [(fx, fy, fc) for fx in (0, 1) for fy in (0, 1) for fc in (0, 1)][1:]
HBM = pl.BlockSpec(memory_space=pltpu.HBM)
SEM = pl.BlockSpec(memory_space=pltpu.SEMAPHORE)
EFFECT = pltpu.SideEffectType.DATAFLOW_SIDE_EFFECTING


def _hbm(a):
    return pltpu.with_memory_space_constraint(a, pltpu.HBM)


def _flips(gather):
    return [(0, 0, 0)] + FLIPS if gather else FLIPS


def _split_copies(gather, s_refs, l_refs, send_sems, recv_sems):
    n = len(s_refs)
    mx, my, mc = lax.axis_index("x"), lax.axis_index("y"), lax.axis_index("c")
    me = 4 * mx + 2 * my + mc
    copies = []
    for k, (fx, fy, fc) in enumerate(_flips(gather)):
        px, py, pc = (1 - mx if fx else mx), (1 - my if fy else my), (1 - mc if fc else mc)
        for t in range(n):
            if gather:
                src = s_refs[t]
                r = src.shape[0]
                dst = l_refs[t].at[pl.ds(pl.multiple_of(me * r, 16), r), :]
            else:
                src = s_refs[t].at[:, 4 * px + 2 * py + pc]
                dst = l_refs[t].at[k]
            copies.append(pltpu.make_async_remote_copy(
                src_ref=src, dst_ref=dst, send_sem=send_sems.at[k * n + t], recv_sem=recv_sems.at[k * n + t],
                device_id=(px, py, pc), device_id_type=MESH))
    return copies


def _split_start(gather, srcs, lands, after, name):
    n = len(srcs)
    n_sem = len(_flips(gather)) * n

    def body(*refs):
        s_refs, l_refs = refs[:n], refs[n:2 * n]
        send_sems, recv_sems = refs[2 * n + 1], refs[2 * n + 2]
        token = refs[-1]
        for cp in _split_copies(gather, s_refs, l_refs, send_sems, recv_sems):
            cp.start()
        token[...] = jnp.zeros_like(token)

    outs = pl.pallas_call(
        body, name=name,
        out_shape=(pltpu.SemaphoreType.DMA((n_sem,)), pltpu.SemaphoreType.DMA((n_sem,)))
        + tuple(pltpu.HBM(a.shape, a.dtype) for a in lands) + (S((8, LANES), F32),),
        in_specs=[HBM] * (2 * n) + [ANY],
        out_specs=(SEM, SEM) + (HBM,) * n + (pl.BlockSpec(memory_space=pltpu.VMEM),),
        input_output_aliases={n + i: 2 + i for i in range(n)},
        compiler_params=pltpu.CompilerParams(has_side_effects=EFFECT),
    )(*[_hbm(a) for a in srcs], *[_hbm(a) for a in lands], after)
    return outs[0], outs[1], list(srcs), list(outs[2:2 + n]), outs[-1]


def _split_wait(gather, send_sems, recv_sems, srcs, lands, after, name):
    n = len(srcs)

    def body(*refs):
        s_refs, l_refs = refs[:n], refs[n:2 * n]
        ssem, rsem = refs[2 * n], refs[2 * n + 1]
        copies = _split_copies(gather, s_refs, l_refs, ssem, rsem)
        for cp in copies:
            cp.wait_send()
        for cp in copies:
            cp.wait_recv()

    outs = pl.pallas_call(
        body, name=name, out_shape=tuple(pltpu.HBM(a.shape, a.dtype) for a in lands),
        in_specs=[HBM] * (2 * n) + [SEM, SEM, ANY], out_specs=(HBM,) * n,
        input_output_aliases={n + i: i for i in range(n)},
        compiler_params=pltpu.CompilerParams(has_side_effects=EFFECT),
    )(*[_hbm(a) for a in srcs], *lands, send_sems, recv_sems, after)
    return list(outs)


def _adamw_math(w, g, m, v):
    m = ADAM_B1 * m + (1.0 - ADAM_B1) * g
    v = ADAM_B2 * v + (1.0 - ADAM_B2) * (g * g)
    m_hat = m / (1.0 - ADAM_B1 ** ADAM_STEP)
    v_hat = v / (1.0 - ADAM_B2 ** ADAM_STEP)
    delta = -ADAM_LR * (m_hat / (jnp.sqrt(v_hat) + ADAM_EPS) + ADAM_WD * w)
    return delta, m, v


def _adamw_layers(owns, gots, w, m, v, name):
    L, B, C = w.shape
    per_row = 2 * L * len(FLIPS) * C * owns[0].dtype.itemsize
    tb = max([t for t in range(16, B + 1, 16) if B % t == 0 and (t * per_row <= 16 * 1024 * 1024 or t == 16)] or [B])
    me = (4 * lax.axis_index("x") + 2 * lax.axis_index("y") + lax.axis_index("c")).astype(jnp.int32).reshape(1)

    def body(me_ref, *refs):
        own_refs, got_refs = refs[:L], refs[L:2 * L]
        w_ref, m_ref, v_ref = refs[2 * L:2 * L + 3]
        g_out, d_out, m_out, v_out = refs[2 * L + 3:]
        layer = pl.program_id(0)
        for kk in range(L):
            @pl.when(layer == kk)
            def _():
                g = own_refs[kk][0].astype(F32)
                for s in range(len(FLIPS)):
                    g = g + got_refs[kk][s].astype(F32)
                d, mn, vn = _adamw_math(w_ref[...], g, m_ref[...], v_ref[...])
                g_out[...] = g
                d_out[...] = d
                m_out[...] = mn
                v_out[...] = vn

    def row(kk, layer, i):
        return jnp.where(layer == kk, i, 0)

    blk = pl.BlockSpec((1, tb, C), lambda layer, i, me_ref: (layer, i, 0))
    own_specs = [pl.BlockSpec((1, 1, tb, C), lambda layer, i, me_ref, kk=kk: (0, me_ref[0], row(kk, layer, i), 0))
                 for kk in range(L)]
    got_specs = [pl.BlockSpec((len(FLIPS), 1, tb, C), lambda layer, i, me_ref, kk=kk: (0, 0, row(kk, layer, i), 0))
                 for kk in range(L)]
    return pl.pallas_call(
        body,
        grid_spec=pltpu.PrefetchScalarGridSpec(
            num_scalar_prefetch=1, grid=(L, B // tb), in_specs=own_specs + got_specs + [blk, blk, blk],
            out_specs=(blk, blk, blk, blk)),
        out_shape=(S((L, B, C), F32),) * 4, compiler_params=_cparams("arbitrary", "arbitrary"), name=name)(
            me, *owns, *gots, w, m, v)


def _adamw_replicated(parts, w, m, v, name):
    R, C = w.shape
    rb = _tile(R, (512, 256, 128, 64, 32, 16, 8))

    def body(p_ref, w_ref, m_ref, v_ref, g_out, d_out, m_out, v_out):
        g = p_ref[0]
        for j in range(1, N_DEV):
            g = g + p_ref[j]
        d, mn, vn = _adamw_math(w_ref[...], g, m_ref[...], v_ref[...])
        g_out[...] = g
        d_out[...] = d
        m_out[...] = mn
        v_out[...] = vn

    blk = pl.BlockSpec((rb, C), lambda i: (i, 0))
    return pl.pallas_call(
        body, grid=(R // rb,), in_specs=[pl.BlockSpec((N_DEV, rb, C), lambda i: (0, i, 0)), blk, blk, blk],
        out_specs=(blk, blk, blk, blk), out_shape=(S((R, C), F32),) * 4, compiler_params=_cparams("parallel"),
        name=name)(parts, w, m, v)


def _pack(arrs, rows_mult, dtype):
    flat = jnp.concatenate([a.reshape(-1).astype(dtype) for a in arrs])
    n = flat.shape[0]
    per = rows_mult * LANES
    tot = -(-n // per) * per
    return jnp.pad(flat, (0, tot - n)).reshape(tot // LANES, LANES)


def _unpack(blob, shapes):
    flat = blob.reshape(-1)
    out, off = [], 0
    for shp in shapes:
        n = int(np.prod(shp))
        out.append(flat[off:off + n].reshape(shp))
        off += n
    return out


def _small_to_natural(g8):
    t = jnp.moveaxis(g8, 0, -2)
    return t.reshape(t.shape[:-2] + (N_DEV * t.shape[-1],))


def _small_to_cols(g):
    t = g.reshape(g.shape[:-1] + (N_DEV, g.shape[-1] // N_DEV))
    return jnp.moveaxis(t, -2, 0)


def _block_diag(w):
    nb, bs, _ = w.shape
    eye = jnp.eye(nb, dtype=w.dtype)
    return (eye[:, None, :, None] * w[:, :, None, :]).reshape(nb * bs, nb * bs)


def _diag_blocks(d, nb, bs):
    d4 = d.reshape(nb, bs, nb, bs)
    return jnp.stack([d4[i, :, i, :] for i in range(nb)])


def kernel(x, mem, g_mix_pre, g_mix_post, g_ffn_pre, g_ffn_post, g_mem, w_mem_kv, w_mix_out, w_ffn_up, w_ffn_conv, b_ffn_conv, w_ffn_down, w_in_a, w_conv_a, b_conv_a, w_rg_r, b_rg_r, w_rg_i, b_rg_i, lru_lambda, w_in_b, sinks_b, g_kv, w_kv, loss_target, m_g_mix_pre, m_g_mix_post, m_g_ffn_pre, m_g_ffn_post, m_g_mem, m_w_mem_kv, m_w_mix_out, m_w_ffn_up, m_w_ffn_conv, m_b_ffn_conv, m_w_ffn_down, m_w_in_a, m_w_conv_a, m_b_conv_a, m_w_rg_r, m_b_rg_r, m_w_rg_i, m_b_rg_i, m_lru_lambda, m_w_in_b, m_sinks_b, m_g_kv, m_w_kv, v_g_mix_pre, v_g_mix_post, v_g_ffn_pre, v_g_ffn_post, v_g_mem, v_w_mem_kv, v_w_mix_out, v_w_ffn_up, v_w_ffn_conv, v_b_ffn_conv, v_w_ffn_down, v_w_in_a, v_w_conv_a, v_b_conv_a, v_w_rg_r, v_b_rg_r, v_w_rg_i, v_b_rg_i, v_lru_lambda, v_w_in_b, v_sinks_b, v_g_kv, v_w_kv):
    w_loc = dict(g_mix_pre=g_mix_pre, g_mix_post=g_mix_post, g_ffn_pre=g_ffn_pre, g_ffn_post=g_ffn_post, g_mem=g_mem,
                 w_mem_kv=w_mem_kv, w_mix_out=w_mix_out, w_ffn_up=w_ffn_up, w_ffn_conv=w_ffn_conv,
                 b_ffn_conv=b_ffn_conv, w_ffn_down=w_ffn_down, w_in_a=w_in_a, w_conv_a=w_conv_a, b_conv_a=b_conv_a,
                 w_rg_r=w_rg_r, b_rg_r=b_rg_r, w_rg_i=w_rg_i, b_rg_i=b_rg_i, lru_lambda=lru_lambda, w_in_b=w_in_b,
                 sinks_b=sinks_b, g_kv=g_kv, w_kv=w_kv)
    m_loc = dict(g_mix_pre=m_g_mix_pre, g_mix_post=m_g_mix_post, g_ffn_pre=m_g_ffn_pre, g_ffn_post=m_g_ffn_post,
                 g_mem=m_g_mem, w_mem_kv=m_w_mem_kv, w_mix_out=m_w_mix_out, w_ffn_up=m_w_ffn_up,
                 w_ffn_conv=m_w_ffn_conv, b_ffn_conv=m_b_ffn_conv, w_ffn_down=m_w_ffn_down, w_in_a=m_w_in_a,
                 w_conv_a=m_w_conv_a, b_conv_a=m_b_conv_a, w_rg_r=m_w_rg_r, b_rg_r=m_b_rg_r, w_rg_i=m_w_rg_i,
                 b_rg_i=m_b_rg_i, lru_lambda=m_lru_lambda, w_in_b=m_w_in_b, sinks_b=m_sinks_b, g_kv=m_g_kv,
                 w_kv=m_w_kv)
    v_loc = dict(g_mix_pre=v_g_mix_pre, g_mix_post=v_g_mix_post, g_ffn_pre=v_g_ffn_pre, g_ffn_post=v_g_ffn_post,
                 g_mem=v_g_mem, w_mem_kv=v_w_mem_kv, w_mix_out=v_w_mix_out, w_ffn_up=v_w_ffn_up,
                 w_ffn_conv=v_w_ffn_conv, b_ffn_conv=v_b_ffn_conv, w_ffn_down=v_w_ffn_down, w_in_a=v_w_in_a,
                 w_conv_a=v_w_conv_a, b_conv_a=v_b_conv_a, w_rg_r=v_w_rg_r, b_rg_r=v_b_rg_r, w_rg_i=v_w_rg_i,
                 b_rg_i=v_b_rg_i, lru_lambda=v_lru_lambda, w_in_b=v_w_in_b, sinks_b=v_sinks_b, g_kv=v_g_kv,
                 w_kv=v_w_kv)

    Bl, T, D = x.shape
    Ml = mem.shape[1]
    N = Bl * T
    depth = g_mix_pre.shape[0]
    n_a = w_in_a.shape[0]
    F = w_ffn_down.shape[1] * N_DEV
    def as_rows(n, a):
        return jnp.swapaxes(a, -1, -2) if n in TRANSPOSED else a

    def layer_keys(l):
        keys = [("w_mem_kv", l), ("w_mix_out", l), ("w_ffn_up", l), ("w_ffn_down", l)]
        keys.append(("w_in_a", l) if l < n_a else ("w_in_b", l - n_a))
        if l == n_a:
            keys.append(("w_kv", None))
        return keys

    def shard_of(key):
        n, i = key
        return as_rows(n, w_loc[n] if i is None else w_loc[n][i]).astype(BF16)

    W = {}
    ffn_names = ("w_ffn_up", "w_ffn_down")
    keys0 = [kk for kk in layer_keys(0) if kk[0] not in ffn_names]
    got0 = _ag_weights([shard_of(kk) for kk in keys0] + [w_loc[n] for n in SMALL_SHARDED],
                       [True] * len(keys0) + [False] * len(SMALL_SHARDED), name="ag_weights_0")
    W.update(zip(keys0, got0))
    for n, a in zip(SMALL_SHARDED, got0[len(keys0):]):
        W[n] = _small_to_natural(a)

    def gather_start(keys, after, tag):
        shards = [shard_of(kk) for kk in keys]
        lands = [lax.empty((N_DEV * s.shape[0],) + s.shape[1:], s.dtype) for s in shards]
        return (keys, tag) + _split_start(True, shards, lands, after, name=f"ag_start_{tag}")

    def gather_wait(pending, after):
        keys, tag, ssem, rsem, srcs, lands, _ = pending
        W.update(zip(keys, _split_wait(True, ssem, rsem, srcs, lands, after, name=f"ag_wait_{tag}")))

    pending_ffn0 = gather_start([kk for kk in layer_keys(0) if kk[0] in ffn_names], got0[0], "ffn_0")

    nblk, bsz = w_rg_r.shape[1], w_rg_r.shape[2]
    wbd = [jnp.concatenate([_block_diag(w_rg_r[j]), _block_diag(w_rg_i[j])], axis=1).astype(BF16) for j in range(n_a)]

    def vec(a):
        return a.reshape(1, -1)

    x2 = x.reshape(N, D)
    mem2 = mem.reshape(Bl * Ml, D)
    saved = []
    kvn = kv3 = x_kv = None
    xs = x2
    for l in range(depth):
        sv = {"x0": xs}
        g_pre = vec(g_mix_pre[l])
        if l + 1 < depth:
            pending = gather_start(layer_keys(l + 1), pending_ffn0[-1] if l == 0 else W[("w_mem_kv", l)], l + 1)
            g_pre = g_pre + pending[-1][0, 0]
        h1 = _rms_fwd(xs, g_pre, BF16, name=f"rms_mixpre_{l}")
        memn = _rms_fwd(mem2, vec(g_mem[l]), BF16, name=f"rms_mem_{l}")
        mkv3 = _mm(memn, W[("w_mem_kv", l)], name=f"mm_memkv_{l}").reshape(Bl, Ml, 2 * MEM_W)
        if l < n_a:
            j = l
            proj = _mm(h1, W[("w_in_a", j)], tb=True, name=f"mm_in_{l}")
            proj3 = proj.reshape(Bl, T, -1)
            xc3 = _conv_fwd_call(proj3, MIX_W, MIX_W, W["w_conv_a"][j], vec(W["b_conv_a"][j]), name=f"conv_a_{l}")
            gates3 = _mm(xc3.reshape(N, MIX_W), wbd[j], name=f"mm_gates_{l}").reshape(Bl, T, 2 * MIX_W)
            y_main3, hs3 = _rglru_fwd(xc3, gates3, proj3, vec(b_rg_r[j]), vec(b_rg_i[j]), vec(W["lru_lambda"][j]),
                                      name=f"rglru_fwd_{l}")
            q_off = 2 * MIX_W
            sv.update(xc3=xc3, gates3=gates3, hs3=hs3)
        else:
            j = l - n_a
            if l == n_a:
                x_kv = xs
                kvn = _rms_fwd(xs, vec(g_kv), BF16, name="rms_kv")
                kv3 = _mm(kvn, W[("w_kv", None)], name="mm_kv").reshape(Bl, T, 2 * MEM_W)
            proj = _mm(h1, W[("w_in_b", j)], name=f"mm_in_{l}")
            proj3 = proj.reshape(Bl, T, -1)
            y_main3 = _swa_fwd(proj3, kv3, sinks_b[j], name=f"swa_fwd_{l}")
            q_off = MIX_W
        y_mem3 = _mem_attn_fwd(proj3, q_off, mkv3, name=f"memattn_fwd_{l}")
        y_main = y_main3.reshape(N, MIX_W)
        y_mem = y_mem3.reshape(N, MEM_W)
        y = _mm_sum([(y_main, W[("w_mix_out", l)], (0, 0)), (y_mem, W[("w_mix_out", l)], (MIX_W, 0))], n=D,
                    name=f"mm_mixout_{l}")
        x1 = _rms_fwd(y, vec(g_mix_post[l]), F32, res=xs, name=f"rms_mixpost_{l}")
        h2 = _rms_fwd(x1, vec(g_ffn_pre[l]), BF16, name=f"rms_ffnpre_{l}")
        if l == 0:
            gather_wait(pending_ffn0, h2)
        u3 = _mm(h2, W[("w_ffn_up", l)], tb=True, name=f"mm_up_{l}").reshape(Bl, T, 2 * F)
        act3 = _ffn_mid_fwd(u3, W["w_ffn_conv"][l], vec(b_ffn_conv[l]), name=f"ffn_mid_fwd_{l}")
        act = act3.reshape(N, F)
        f = _mm(act, W[("w_ffn_down", l)], name=f"mm_down_{l}")
        x_next = _rms_fwd(f, vec(g_ffn_post[l]), F32, res=x1, name=f"rms_ffnpost_{l}")
        if l + 1 < depth:
            gather_wait(pending, x_next)
        sv.update(h1=h1, memn=memn, mkv3=mkv3, proj3=proj3, q_off=q_off, y_main=y_main, y_mem=y_mem, y=y, x1=x1,
                  h2=h2, u3=u3, act=act, f=f)
        saved.append(sv)
        xs = x_next

    dxs, loss_vec = _loss_bwd(xs, loss_target.reshape(N, D))
    loss = lax.psum(jnp.sum(loss_vec), ("x", "y", "c"))

    G = {n: [None] * w_loc[n].shape[0] for n in REPL + SMALL_SHARDED if n != "g_kv"}
    GW = {}

    def dw(key, off, a, b_, nm):
        GW[key] = _mm(a, b_, ta=True, out_dtype=BF16, into=(GW.get(key), (1,) + W[key].shape, 0, off), name=nm)

    def grad_blocks(key):
        g = GW[key]
        return g.reshape(1, N_DEV, g.shape[1] // N_DEV, g.shape[2])

    reduces = []

    def reduce_start(keys, after, tag):
        srcs = [grad_blocks(kk) for kk in keys]
        lands = [lax.empty((len(FLIPS),) + s.shape[:1] + s.shape[2:], s.dtype) for s in srcs]
        started = _split_start(False, srcs, lands, after, name=f"rs_start_{tag}")
        reduces.append((keys, tag) + started)
        return started[-1]

    kv_parts = []
    for l in reversed(range(depth)):
        sv = saved[l]
        proj3 = sv["proj3"]
        df, dg = _rms_bwd(sv["f"], vec(g_ffn_post[l]), dxs, out_dtype=BF16, name=f"rmsb_ffnpost_{l}")
        G["g_ffn_post"][l] = dg[0]
        dact = _mm(df, W[("w_ffn_down", l)], tb=True, name=f"mmb_down_dx_{l}")
        dw(("w_ffn_down", l), (0, 0), sv["act"], df, f"mmb_down_dw_{l}")
        dug3, duv3, dwg, dwv, dbg, dbv = _ffn_mid_bwd(sv["u3"], dact.reshape(Bl, T, F),
                                                      W["w_ffn_conv"][l], vec(b_ffn_conv[l]), name=f"ffn_mid_bwd_{l}")
        G["w_ffn_conv"][l] = jnp.concatenate([dwg, dwv], axis=1)
        G["b_ffn_conv"][l] = jnp.concatenate([dbg, dbv], axis=1)[0]
        dug, duv = dug3.reshape(N, F), duv3.reshape(N, F)
        dw(("w_ffn_up", l), (0, 0), dug, sv["h2"], f"mmb_up_dw_g_{l}")
        dw(("w_ffn_up", l), (F, 0), duv, sv["h2"], f"mmb_up_dw_v_{l}")
        tok = reduce_start([("w_ffn_down", l), ("w_ffn_up", l)], dug, f"ffn_{l}")
        dh2 = _mm_sum([(dug, W[("w_ffn_up", l)], (0, 0)), (duv, W[("w_ffn_up", l)], (F, 0))], n=D, after=tok,
                      name=f"mmb_up_dx_{l}")
        dx1, dg = _rms_bwd(sv["x1"], vec(g_ffn_pre[l]), dh2, add=dxs, name=f"rmsb_ffnpre_{l}")
        G["g_ffn_pre"][l] = dg[0]
        dy, dg = _rms_bwd(sv["y"], vec(g_mix_post[l]), dx1, out_dtype=BF16, name=f"rmsb_mixpost_{l}")
        G["g_mix_post"][l] = dg[0]
        dy_main = _mm(dy, W[("w_mix_out", l)], tb=True, n=MIX_W, k=D, name=f"mmb_mixout_dmain_{l}")
        dy_mem = _mm(dy, W[("w_mix_out", l)], tb=True, n=MEM_W, k=D, b_off=(MIX_W, 0),
                     name=f"mmb_mixout_dmem_{l}")
        dw(("w_mix_out", l), (0, 0), sv["y_main"], dy, f"mmb_mixout_dw_main_{l}")
        dw(("w_mix_out", l), (MIX_W, 0), sv["y_mem"], dy, f"mmb_mixout_dw_mem_{l}")
        dq_mem3, dmkv3 = _mem_attn_bwd(proj3, sv["q_off"], sv["mkv3"], dy_mem.reshape(Bl, T, MEM_W),
                                       name=f"memattn_bwd_{l}")
        dq_mem = dq_mem3.reshape(N, MEM_W)
        dmkv = dmkv3.reshape(Bl * Ml, 2 * MEM_W)
        dw(("w_mem_kv", l), (0, 0), sv["memn"], dmkv, f"mmb_memkv_dw_{l}")
        dmemn = _mm(dmkv, W[("w_mem_kv", l)], tb=True, name=f"mmb_memkv_dx_{l}")
        _, dg = _rms_bwd(mem2, vec(g_mem[l]), dmemn, name=f"rmsb_mem_{l}")
        G["g_mem"][l] = dg[0]
        dy_main3 = dy_main.reshape(Bl, T, MIX_W)
        if l < n_a:
            j = l
            dxc3, drp3, dip3, dugate3, dbr, dbi, dlam = _rglru_bwd(
                dy_main3, sv["xc3"], sv["gates3"], proj3, sv["hs3"], vec(b_rg_r[j]), vec(b_rg_i[j]),
                vec(W["lru_lambda"][j]), name=f"rglru_bwd_{l}")
            G["b_rg_r"][j] = dbr.reshape(nblk, bsz)
            G["b_rg_i"][j] = dbi.reshape(nblk, bsz)
            G["lru_lambda"][j] = dlam[0]
            drp, dip = drp3.reshape(N, MIX_W), dip3.reshape(N, MIX_W)
            xc2 = sv["xc3"].reshape(N, MIX_W)
            G["w_rg_r"][j] = _diag_blocks(_mm(xc2, drp, ta=True, name=f"mmb_gates_dw_r_{l}"), nblk, bsz)
            G["w_rg_i"][j] = _diag_blocks(_mm(xc2, dip, ta=True, name=f"mmb_gates_dw_i_{l}"), nblk, bsz)
            dxc = _mm_sum([(drp, wbd[j], (0, 0)), (dip, wbd[j], (0, MIX_W))], tb=True, n=MIX_W,
                          add=dxc3.reshape(N, MIX_W), name=f"mmb_gates_dx_{l}")
            dux3, dwc, dbc = _conv_bwd_call(dxc.reshape(Bl, T, MIX_W), proj3, MIX_W, MIX_W, W["w_conv_a"][j],
                                            name=f"conv_a_bwd_{l}")
            G["w_conv_a"][j] = dwc
            G["b_conv_a"][j] = dbc[0]
            pieces = [(dugate3.reshape(N, MIX_W), 0), (dux3.reshape(N, MIX_W), MIX_W), (dq_mem, 2 * MIX_W)]
            in_key = ("w_in_a", j)
        else:
            j = l - n_a
            dq3, dkc, dkp, dsk = _swa_bwd(proj3, kv3, sinks_b[j], dy_main3, name=f"swa_bwd_{l}")
            kv_parts.append((dkc, dkp))
            G["sinks_b"][j] = dsk[0, :SWA_HEADS]
            pieces = [(dq3.reshape(N, MIX_W), 0), (dq_mem, MIX_W)]
            in_key = ("w_in_b", j)
        in_t = in_key[0] in TRANSPOSED
        for pi, (piece, off) in enumerate(pieces):
            if in_t:
                dw(in_key, (off, 0), piece, sv["h1"], f"mmb_in_dw_{pi}_{l}")
            else:
                dw(in_key, (0, off), sv["h1"], piece, f"mmb_in_dw_{pi}_{l}")
        tok = reduce_start([("w_mix_out", l), ("w_mem_kv", l), in_key], dy, f"mix_{l}")
        dh1 = _mm_sum([(piece, W[in_key], (off, 0) if in_t else (0, off)) for piece, off in pieces], tb=not in_t, n=D,
                      after=tok, name=f"mmb_in_dx_{l}")
        dxs, dg = _rms_bwd(sv["x0"], vec(g_mix_pre[l]), dh1, add=dx1, name=f"rmsb_mixpre_{l}")
        G["g_mix_pre"][l] = dg[0]
        if l == n_a:
            dkv = _kv_grad_combine(kv_parts, name="kv_grad_combine").reshape(N, 2 * MEM_W)
            dw(("w_kv", None), (0, 0), kvn, dkv, "mmb_kv_dw")
            tok = reduce_start([("w_kv", None)], dkv, "kv")
            dkvn = _mm(dkv, W[("w_kv", None)], tb=True, after=tok, name="mmb_kv_dx")
            dxs, dg = _rms_bwd(x_kv, vec(g_kv), dkvn, add=dxs, name="rmsb_kv")
            G["g_kv"] = dg[0]
    grad_x = dxs.reshape(Bl, T, D)
    Gf = {n: (jnp.stack(g) if isinstance(g, list) else g) for n, g in G.items()}

    small4 = []
    for n in SMALL_SHARDED:
        t = _small_to_cols(Gf[n]).astype(BF16)
        small4.append(t.reshape(1, N_DEV, -1, t.shape[-1]))
    small_lands = [lax.empty((len(FLIPS),) + s.shape[:1] + s.shape[2:], s.dtype) for s in small4]
    small_started = _split_start(False, small4, small_lands, dxs, name="rs_start_small")
    r_blob = _pack([Gf[n].astype(F32) for n in REPL], REPL_ROWS, F32)
    r_parts = _all_gather(r_blob, name="ag_repl_grads")
    parts = {}
    for keys, tag, ssem, rsem, srcs, lands, _ in reduces:
        for kk, s, g7 in zip(keys, srcs, _split_wait(False, ssem, rsem, srcs, lands, small_started[-1],
                                                     name=f"rs_wait_{tag}")):
            parts[kk] = (s, g7)

    res = [{} for _ in range(4)]
    for n, _ in SHARDED:
        if n in SMALL_SHARDED:
            continue
        idx = [None] if w_loc[n].ndim == 2 else list(range(w_loc[n].shape[0]))
        wmv = [as_rows(n, a[n]) for a in (w_loc, m_loc, v_loc)]
        shp3 = (len(idx),) + wmv[0].shape[-2:]
        outs = _adamw_layers([parts[(n, i)][0] for i in idx], [parts[(n, i)][1] for i in idx],
                             *[a.reshape(shp3) for a in wmv], name=f"adamw_{n}")
        for k in range(4):
            res[k][n] = as_rows(n, outs[k].reshape(wmv[0].shape))
    last = res[0]["w_kv"]
    small_got = _split_wait(False, *small_started[:4], last, name="rs_wait_small")
    for n, own, g7 in zip(SMALL_SHARDED, small4, small_got):
        shp3 = own.shape[:1] + own.shape[2:]
        outs = _adamw_layers([own], [g7], w_loc[n].reshape(shp3), m_loc[n].reshape(shp3), v_loc[n].reshape(shp3),
                             name=f"adamw_{n}")
        for k in range(4):
            res[k][n] = outs[k].reshape(w_loc[n].shape)
    outs_rp = _adamw_replicated(r_parts, _pack([w_loc[n] for n in REPL], REPL_ROWS, F32),
                                _pack([m_loc[n] for n in REPL], REPL_ROWS, F32),
                                _pack([v_loc[n] for n in REPL], REPL_ROWS, F32),
                                name="adamw_replicated")
    rp_shapes = [w_loc[n].shape for n in REPL]
    for k in range(4):
        res[k].update(zip(REPL, _unpack(outs_rp[k], rp_shapes)))
    out = [loss, grad_x]
    for k in range(4):
        out += [res[k][n] for n in WEIGHTS]
    return tuple(out)
```

```python
import functools
import math

import numpy as np
import jax
import jax.numpy as jnp
from jax import lax
from jax.experimental import pallas as pl
from jax.experimental.pallas import tpu as pltpu

F32 = jnp.float32
BF16 = jnp.bfloat16
S = jax.ShapeDtypeStruct
MESH = pl.DeviceIdType.MESH
ANY = pl.BlockSpec(memory_space=pl.ANY)

HEAD = 64
MEM_HEADS = 4
MEM_W = MEM_HEADS * HEAD
SWA_HEADS = 12
SWA_GROUP = 3
MIX_W = SWA_HEADS * HEAD
WIN = 128
LRU_C = 8.0
EPS = 1e-6
ADAM_LR, ADAM_B1, ADAM_B2, ADAM_EPS, ADAM_WD, ADAM_STEP = 0.001, 0.9, 0.999, 1e-08, 0.01, 10
GELU_C0 = math.sqrt(2.0 / math.pi)
GELU_C1 = 0.044715
N_DEV = 8
LANES = 128
CT = 128
VMEM_LIMIT = 48 * 1024 * 1024
MM_VMEM_BUDGET = 36 * 1024 * 1024
REPL_ROWS = 256

SHARDED = (("w_mem_kv", 1), ("w_mix_out", 1), ("w_ffn_up", 2), ("w_ffn_conv", 2), ("w_ffn_down", 1), ("w_in_a", 2),
           ("w_conv_a", 2), ("b_conv_a", 1), ("lru_lambda", 1), ("w_in_b", 1), ("w_kv", 0))
SMALL_SHARDED = ("w_ffn_conv", "w_conv_a", "b_conv_a", "lru_lambda")
TRANSPOSED = ("w_ffn_up", "w_in_a")
REPL = ("g_mix_pre", "g_mix_post", "g_ffn_pre", "g_ffn_post", "g_mem", "b_ffn_conv", "w_rg_r", "b_rg_r", "w_rg_i",
        "b_rg_i", "sinks_b", "g_kv")
WEIGHTS = ("g_mix_pre", "g_mix_post", "g_ffn_pre", "g_ffn_post", "g_mem", "w_mem_kv", "w_mix_out", "w_ffn_up",
           "w_ffn_conv", "b_ffn_conv", "w_ffn_down", "w_in_a", "w_conv_a", "b_conv_a", "w_rg_r", "b_rg_r", "w_rg_i",
           "b_rg_i", "lru_lambda", "w_in_b", "sinks_b", "g_kv", "w_kv")


def _alibi_slopes(n):
    def pow2(m):
        start = 2.0 ** (-8.0 / m)
        return [start ** (i + 1) for i in range(m)]
    c = 2 ** int(math.floor(math.log2(n)))
    s = pow2(c)
    if c != n:
        s = s + pow2(2 * c)[0::2][: n - c]
    return [float(v) for v in np.asarray(s, dtype=np.float32)]


SLOPES = _alibi_slopes(SWA_HEADS)


def _tile(n, cands):
    for c in cands:
        if n % c == 0:
            return c
    return n


def _cparams(*sem):
    return pltpu.CompilerParams(dimension_semantics=sem, vmem_limit_bytes=VMEM_LIMIT)


def _mm_tiles(M, N, K, a_bytes, b_bytes, o_bytes, add_bytes, offsets):
    m_off, n_offs, k_off = offsets
    tms = [c for c in (1024, 512, 256, 128) if M % c == 0 and m_off % c == 0] or [M]
    tns = [c for c in (1408, 1024, 896, 768, 512, 384, 256, 128)
           if N % c == 0 and all(o % c == 0 for o in n_offs)] or [N]
    tks = [c for c in (K, 2048, 1408, 1024, 512, 256, 128) if c <= K and K % c == 0 and k_off % c == 0]
    best = None
    for tk in tks:
        fits = []
        for tm in tms:
            for tn in tns:
                need = 2 * (tm * tk * a_bytes + tk * tn * b_bytes + tm * tn * (o_bytes + add_bytes))
                need += tm * tn * 4 * (2 if tk < K else 1)
                need += (tm * tk * 2 if a_bytes != 2 else 0) + (tk * tn * 2 if b_bytes != 2 else 0)
                if need <= MM_VMEM_BUDGET:
                    fits.append((tm * tn, min(tm, 512), tm, tn))
        if fits:
            _, _, tm, tn = max(fits)
            best = (tm, tn, tk)
            break
    assert best is not None, (M, N, K)
    return best


def _mm(a, b, *, ta=False, tb=False, n=None, k=None, b_off=(0, 0), out_dtype=F32, add=None, into=None, after=None,
        name="mm"):
    if ta:
        K, M = a.shape
    else:
        M, K = a.shape
    if tb:
        N = b.shape[-2] if n is None else n
    else:
        N = b.shape[-1] if n is None else n
    assert k is None or k == K
    ro, co = b_off
    n_off, k_off = (ro, co) if tb else (co, ro)
    oro, oco = (0, 0) if into is None else into[3]
    tm, tn, tk = _mm_tiles(M, N, K, a.dtype.itemsize, b.dtype.itemsize, jnp.dtype(out_dtype).itemsize,
                           0 if add is None else add.dtype.itemsize, (oro, (n_off, oco), k_off))
    nk = K // tk
    if tb:
        b_spec = pl.BlockSpec((tn, tk), lambda i, j, kk: (j + ro // tn, kk + co // tk))
        b_dims = (1,)
    else:
        b_spec = pl.BlockSpec((tk, tn), lambda i, j, kk: (kk + ro // tk, j + co // tn))
        b_dims = (0,)
    if ta:
        a_spec = pl.BlockSpec((tk, tm), lambda i, j, kk: (kk, i))
        a_dims = (0,)
    else:
        a_spec = pl.BlockSpec((tm, tk), lambda i, j, kk: (i, kk))
        a_dims = (1,)
    dims = ((a_dims, b_dims), ((), ()))
    add_spec = pl.BlockSpec((tm, tn), lambda i, j, kk: (i, j))
    has_add = add is not None
    if into is None:
        o_spec, o_shape, buf = add_spec, (M, N), None
    else:
        buf, o_shape, ol, _ = into
        assert not has_add
        o_spec = pl.BlockSpec((None, tm, tn), lambda i, j, kk: (ol, i + oro // tm, j + oco // tn))
    has_buf = buf is not None

    def body(*refs):
        refs = list(refs)
        acc_ref = refs.pop() if nk > 1 else None
        o_ref = refs.pop()
        a_ref, b_ref = refs[0], refs[1]
        add_ref = refs[2] if has_add else None
        part = lax.dot_general(a_ref[...].astype(BF16), b_ref[...].astype(BF16), dims, preferred_element_type=F32)

        def finish(r):
            if has_add:
                r = r + add_ref[...].astype(F32)
            o_ref[...] = r.astype(out_dtype)

        if nk == 1:
            finish(part)
        else:
            kk = pl.program_id(2)

            @pl.when(kk == 0)
            def _():
                acc_ref[...] = part

            @pl.when(kk > 0)
            def _():
                acc_ref[...] += part

            @pl.when(kk == nk - 1)
            def _():
                finish(acc_ref[...])

    in_specs = [a_spec, b_spec] + ([add_spec] if has_add else []) + ([ANY] if has_buf else [])
    args = (a, b) + ((add,) if has_add else ()) + ((buf,) if has_buf else ())
    if after is not None:
        in_specs, args = in_specs + [ANY], args + (after,)
    return pl.pallas_call(
        body, grid=(M // tm, N // tn, nk), in_specs=in_specs, out_specs=o_spec,
        out_shape=S(o_shape, out_dtype), scratch_shapes=[pltpu.VMEM((tm, tn), F32)] if nk > 1 else [],
        input_output_aliases={2: 0} if has_buf else {},
        compiler_params=_cparams("parallel", "parallel", "arbitrary"), name=name)(*args)


def _mm_sum(pieces, *, tb=False, n, out_dtype=F32, add=None, after=None, name="mm_sum"):
    M = pieces[0][0].shape[0]
    ks = [a.shape[1] for a, _, _ in pieces]
    a_bytes = max(a.dtype.itemsize for a, _, _ in pieces)
    b_bytes = max(b.dtype.itemsize for _, b, _ in pieces)
    n_offs = tuple(off[0] if tb else off[1] for _, _, off in pieces)
    for kp, (_, _, off) in zip(ks, pieces):
        assert (off[1] if tb else off[0]) % kp == 0
    tm, tn, tk = _mm_tiles(M, n, sum(ks), a_bytes, b_bytes, jnp.dtype(out_dtype).itemsize, 0, (0, n_offs, 0))
    assert tk == sum(ks)
    a_specs = [pl.BlockSpec((tm, kp), lambda i, j: (i, 0)) for kp in ks]
    if tb:
        b_specs = [pl.BlockSpec((tn, kp), lambda i, j, ro=off[0], co=off[1], kp=kp: (j + ro // tn, co // kp))
                   for kp, (_, _, off) in zip(ks, pieces)]
        dims = NT
    else:
        b_specs = [pl.BlockSpec((kp, tn), lambda i, j, ro=off[0], co=off[1], kp=kp: (ro // kp, j + co // tn))
                   for kp, (_, _, off) in zip(ks, pieces)]
        dims = (((1,), (0,)), ((), ()))
    npc = len(pieces)
    o_spec = pl.BlockSpec((tm, tn), lambda i, j: (i, j))

    def body(*refs):
        o_ref = refs[2 * npc + (add is not None) + (after is not None)]
        acc = refs[2 * npc][...].astype(F32) if add is not None else None
        for p in range(npc):
            part = lax.dot_general(refs[p][...].astype(BF16), refs[npc + p][...].astype(BF16), dims,
                                   preferred_element_type=F32)
            acc = part if acc is None else acc + part
        o_ref[...] = acc.astype(out_dtype)

    args = [a for a, _, _ in pieces] + [b for _, b, _ in pieces]
    in_specs = a_specs + b_specs
    if add is not None:
        in_specs, args = in_specs + [o_spec], args + [add]
    if after is not None:
        in_specs, args = in_specs + [ANY], args + [after]
    return pl.pallas_call(
        body, grid=(M // tm, n // tn), in_specs=in_specs, out_specs=o_spec,
        out_shape=S((M, n), out_dtype), compiler_params=_cparams("parallel", "parallel"), name=name)(*args)


def _rms_fwd(x, g, out_dtype, res=None, name="rms_fwd"):
    N, D = x.shape
    tm = _tile(N, (512, 256, 128))
    has_res = res is not None

    def body(*refs):
        if has_res:
            x_ref, g_ref, r_ref, o_ref = refs
        else:
            x_ref, g_ref, o_ref = refs
        xv = x_ref[...].astype(F32)
        y = xv * lax.rsqrt(jnp.mean(xv * xv, axis=-1, keepdims=True) + EPS) * g_ref[...]
        if has_res:
            y = y + r_ref[...]
        o_ref[...] = y.astype(out_dtype)

    row = pl.BlockSpec((tm, D), lambda i: (i, 0))
    vec = pl.BlockSpec((1, D), lambda i: (0, 0))
    return pl.pallas_call(
        body, grid=(N // tm,), in_specs=[row, vec] + ([row] if has_res else []), out_specs=row,
        out_shape=S((N, D), out_dtype), compiler_params=_cparams("parallel"), name=name)(
            *((x, g) + ((res,) if has_res else ())))


def _rms_bwd(x, g, dy, add=None, out_dtype=F32, name="rms_bwd"):
    N, D = x.shape
    tm = _tile(N, (512, 256, 128))
    has_add = add is not None

    def body(*refs):
        if has_add:
            x_ref, g_ref, dy_ref, add_ref, dx_ref, dg_ref = refs
        else:
            x_ref, g_ref, dy_ref, dx_ref, dg_ref = refs
        xv = x_ref[...].astype(F32)
        dyv = dy_ref[...].astype(F32)
        r = lax.rsqrt(jnp.mean(xv * xv, axis=-1, keepdims=True) + EPS)
        u = dyv * g_ref[...]
        dx = r * u - xv * (r * r * r * jnp.mean(u * xv, axis=-1, keepdims=True))
        if has_add:
            dx = dx + add_ref[...]
        dx_ref[...] = dx.astype(out_dtype)

        @pl.when(pl.program_id(0) == 0)
        def _():
            dg_ref[...] = jnp.zeros_like(dg_ref)

        dg_ref[...] += jnp.sum(dyv * xv * r, axis=0, keepdims=True)

    row = pl.BlockSpec((tm, D), lambda i: (i, 0))
    vec = pl.BlockSpec((1, D), lambda i: (0, 0))
    return pl.pallas_call(
        body, grid=(N // tm,), in_specs=[row, vec, row] + ([row] if has_add else []), out_specs=(row, vec),
        out_shape=(S((N, D), out_dtype), S((1, D), F32)), compiler_params=_cparams("arbitrary"), name=name)(
            *((x, g, dy) + ((add,) if has_add else ())))


def _shift_down(x, s, row):
    return jnp.where(row >= s, pltpu.roll(x, s, axis=0), 0.0)


def _shift_up(x, s, row):
    T = x.shape[0]
    return jnp.where(row < T - s, pltpu.roll(x, T - s, axis=0), 0.0)


SLAB = 16


def _conv_wrap(x, w_ref, b_ref):
    W = w_ref.shape[0]
    y = x * w_ref[W - 1:W, :] + b_ref[...]
    for s in range(1, W):
        y = y + pltpu.roll(x, s, axis=0) * w_ref[W - 1 - s:W - s, :]
    return y


def _conv_rows(x_ref, w_ref, b_ref, lo, hi):
    W = w_ref.shape[0]
    y = x_ref[lo:hi, :] * w_ref[W - 1:W, :] + b_ref[...]
    for s in range(1, W):
        y = y + x_ref[lo - s:hi - s, :] * w_ref[W - 1 - s:W - s, :]
    return y


def _taps(x_ref, W):
    T = x_ref.shape[0]
    head = x_ref[0:SLAB, :]
    row = lax.broadcasted_iota(jnp.int32, head.shape, 0)
    return [x_ref[...]] + [jnp.concatenate([_shift_down(head, s, row), x_ref[SLAB - s:T - s, :]], axis=0)
                           for s in range(1, W)]


def _conv_taps(xs, w_ref, b_ref):
    W = w_ref.shape[0]
    y = xs[0] * w_ref[W - 1:W, :] + b_ref[...]
    for s in range(1, W):
        y = y + xs[s] * w_ref[W - 1 - s:W - s, :]
    return y


def _conv_head(x_head, w_ref, b_ref):
    row = lax.broadcasted_iota(jnp.int32, x_head.shape, 0)
    return _conv_taps([x_head] + [_shift_down(x_head, s, row) for s in range(1, w_ref.shape[0])], w_ref, b_ref)


def _conv_bwd_taps(dy, xs, w_ref, row):
    W = w_ref.shape[0]
    dx = dy * w_ref[W - 1:W, :]
    dws = [None] * W
    dws[W - 1] = jnp.sum(dy * xs[0], axis=0, keepdims=True)
    for s in range(1, W):
        dx = dx + _shift_up(dy, s, row) * w_ref[W - 1 - s:W - s, :]
        dws[W - 1 - s] = jnp.sum(dy * xs[s], axis=0, keepdims=True)
    return dx, jnp.concatenate(dws, axis=0), jnp.sum(dy, axis=0, keepdims=True)


def _conv_bwd_wrap(dy, x, w_ref):
    W = w_ref.shape[0]
    T = dy.shape[0]
    dx = dy * w_ref[W - 1:W, :]
    dws = [None] * W
    dws[W - 1] = jnp.sum(dy * x, axis=0, keepdims=True)
    for s in range(1, W):
        up = pltpu.roll(dy, T - s, axis=0)
        dx = dx + up * w_ref[W - 1 - s:W - s, :]
        dws[W - 1 - s] = jnp.sum(up * x, axis=0, keepdims=True)
    return dx, jnp.concatenate(dws, axis=0), jnp.sum(dy, axis=0, keepdims=True)


def _conv_bwd_fix(dy_head, dy_tail, x_tail, w_ref):
    row = lax.broadcasted_iota(jnp.int32, dy_tail.shape, 0)
    W = w_ref.shape[0]
    dx = dy_tail * w_ref[W - 1:W, :]
    extra = [jnp.zeros((1, dy_tail.shape[1]), F32)] * W
    for s in range(1, W):
        dx = dx + _shift_up(dy_tail, s, row) * w_ref[W - 1 - s:W - s, :]
        extra[W - 1 - s] = jnp.sum(jnp.where(row < s, dy_head * pltpu.roll(x_tail, s, axis=0), 0.0), axis=0,
                                   keepdims=True)
    return dx, jnp.concatenate(extra, axis=0)


def _gelu(g):
    t = jnp.tanh(GELU_C0 * (g + GELU_C1 * g * g * g))
    return 0.5 * g * (1.0 + t), t


def _dgelu(g, t):
    return 0.5 * (1.0 + t) + 0.5 * g * (1.0 - t * t) * (GELU_C0 * (1.0 + 3.0 * GELU_C1 * g * g))


def _cspec(T, off=0):
    return pl.BlockSpec((1, T, CT), lambda j, b: (b, 0, j + off))


def _pspec(rows, off=0):
    return pl.BlockSpec((rows, CT), lambda j, b: (0, j + off))


def _conv_fwd_call(x3, x_off, C, w, b, name):
    Bl, T, _ = x3.shape
    W = w.shape[0]

    def body(x_ref, w_ref, b_ref, o_ref):
        o_ref[0] = _conv_wrap(x_ref[0], w_ref, b_ref)
        o_ref[0, 0:SLAB, :] = _conv_head(x_ref[0, 0:SLAB, :], w_ref, b_ref)

    return pl.pallas_call(
        body, grid=(C // CT, Bl), in_specs=[_cspec(T, x_off // CT), _pspec(W), _pspec(1)], out_specs=_cspec(T),
        out_shape=S((Bl, T, C), F32), compiler_params=_cparams("parallel", "arbitrary"), name=name)(x3, w, b)


def _conv_bwd_call(dy3, x3, x_off, C, w, name):
    Bl, T, _ = x3.shape
    W = w.shape[0]

    def body(dy_ref, x_ref, w_ref, dx_ref, dw_ref, db_ref):
        dx, dw, db = _conv_bwd_wrap(dy_ref[0], x_ref[0], w_ref)
        dx_tail, dw_extra = _conv_bwd_fix(dy_ref[0, 0:SLAB, :], dy_ref[0, T - SLAB:T, :], x_ref[0, T - SLAB:T, :],
                                          w_ref)
        dx_ref[0] = dx.astype(BF16)
        dx_ref[0, T - SLAB:T, :] = dx_tail.astype(BF16)

        @pl.when(pl.program_id(1) == 0)
        def _():
            dw_ref[...] = jnp.zeros_like(dw_ref)
            db_ref[...] = jnp.zeros_like(db_ref)

        dw_ref[...] += dw - dw_extra
        db_ref[...] += db

    return pl.pallas_call(
        body, grid=(C // CT, Bl), in_specs=[_cspec(T), _cspec(T, x_off // CT), _pspec(W)],
        out_specs=(_cspec(T), _pspec(W), _pspec(1)),
        out_shape=(S((Bl, T, C), BF16), S((W, C), F32), S((1, C), F32)),
        compiler_params=_cparams("parallel", "arbitrary"), name=name)(dy3, x3, w)


def _ffn_mid_fwd(u3, wc, bc, name):
    Bl, T, F2 = u3.shape
    F = F2 // 2
    nf = F // CT

    def body(ug_ref, uv_ref, wg_ref, wv_ref, bg_ref, bv_ref, o_ref):
        g = _conv_rows(ug_ref.at[0], wg_ref, bg_ref, SLAB, T)
        v = _conv_rows(uv_ref.at[0], wv_ref, bv_ref, SLAB, T)
        o_ref[0, SLAB:T, :] = (_gelu(g)[0] * v).astype(BF16)
        g = _conv_head(ug_ref[0, 0:SLAB, :], wg_ref, bg_ref)
        v = _conv_head(uv_ref[0, 0:SLAB, :], wv_ref, bv_ref)
        o_ref[0, 0:SLAB, :] = (_gelu(g)[0] * v).astype(BF16)

    return pl.pallas_call(
        body, grid=(nf, Bl),
        in_specs=[_cspec(T), _cspec(T, nf), _pspec(3), _pspec(3, nf), _pspec(1), _pspec(1, nf)], out_specs=_cspec(T),
        out_shape=S((Bl, T, F), BF16), compiler_params=_cparams("parallel", "arbitrary"), name=name)(
            u3, u3, wc, wc, bc, bc)


def _ffn_mid_bwd(u3, dact3, wc, bc, name):
    Bl, T, F2 = u3.shape
    F = F2 // 2
    nf = F // CT

    def body(ug_ref, uv_ref, da_ref, wg_ref, wv_ref, bg_ref, bv_ref, dug_ref, duv_ref, dwg_ref, dwv_ref, dbg_ref,
             dbv_ref):
        row = lax.broadcasted_iota(jnp.int32, (T, CT), 0)
        ugs = _taps(ug_ref.at[0], 3)
        uvs = _taps(uv_ref.at[0], 3)
        g = _conv_taps(ugs, wg_ref, bg_ref)
        v = _conv_taps(uvs, wv_ref, bv_ref)
        da = da_ref[0]
        gel, t = _gelu(g)
        dg = da * v * _dgelu(g, t)
        dv = da * gel
        dug, dwg, dbg = _conv_bwd_taps(dg, ugs, wg_ref, row)
        duv, dwv, dbv = _conv_bwd_taps(dv, uvs, wv_ref, row)
        dug_ref[0] = dug.astype(BF16)
        duv_ref[0] = duv.astype(BF16)

        @pl.when(pl.program_id(1) == 0)
        def _():
            dwg_ref[...] = jnp.zeros_like(dwg_ref)
            dwv_ref[...] = jnp.zeros_like(dwv_ref)
            dbg_ref[...] = jnp.zeros_like(dbg_ref)
            dbv_ref[...] = jnp.zeros_like(dbv_ref)

        dwg_ref[...] += dwg
        dwv_ref[...] += dwv
        dbg_ref[...] += dbg
        dbv_ref[...] += dbv

    return pl.pallas_call(
        body, grid=(nf, Bl),
        in_specs=[_cspec(T), _cspec(T, nf), _cspec(T), _pspec(3), _pspec(3, nf), _pspec(1), _pspec(1, nf)],
        out_specs=(_cspec(T), _cspec(T), _pspec(3), _pspec(3), _pspec(1), _pspec(1)),
        out_shape=(S((Bl, T, F), BF16), S((Bl, T, F), BF16), S((3, F), F32), S((3, F), F32), S((1, F), F32),
                   S((1, F), F32)),
        compiler_params=_cparams("parallel", "arbitrary"), name=name)(u3, u3, dact3, wc, wc, bc, bc)


def _lru_gates(xc, rp, ip, br_ref, bi_ref, lam_ref):
    r = jax.nn.sigmoid(rp + br_ref[...])
    i = jax.nn.sigmoid(ip + bi_ref[...])
    lam = lam_ref[...]
    sp = jnp.maximum(-lam, 0.0) + jnp.log1p(jnp.exp(-jnp.abs(lam)))
    log_a = (-LRU_C) * r * sp
    a = jnp.exp(log_a)
    z = 2.0 * log_a
    one_m_a2 = jnp.where(z > -0.05, -z * (1.0 + z * (0.5 + z * (1.0 / 6.0 + z * (1.0 / 24.0)))), 1.0 - a * a)
    mult = jnp.sqrt(one_m_a2)
    return r, i, sp, a, mult


def _rglru_fwd(xc3, gates3, proj3, br, bi, lam, name):
    Bl, T, C = xc3.shape
    nsteps = int(math.log2(T))
    assert 1 << nsteps == T

    def body(xc_ref, rp_ref, ip_ref, ug_ref, br_ref, bi_ref, lam_ref, y_ref, h_ref):
        row = lax.broadcasted_iota(jnp.int32, (T, CT), 0)
        xc = xc_ref[0]
        r, i, sp, a, mult = _lru_gates(xc, rp_ref[0], ip_ref[0], br_ref, bi_ref, lam_ref)
        b = mult * (i * xc)
        for st in range(nsteps):
            s = 1 << st
            a_sh = jnp.where(row >= s, pltpu.roll(a, s, axis=0), 1.0)
            b = a * _shift_down(b, s, row) + b
            a = a * a_sh
        h_ref[0] = b
        y_ref[0] = (b * _gelu(ug_ref[0])[0]).astype(BF16)

    return pl.pallas_call(
        body, grid=(C // CT, Bl),
        in_specs=[_cspec(T), _cspec(T), _cspec(T, C // CT), _cspec(T), _pspec(1), _pspec(1), _pspec(1)],
        out_specs=(_cspec(T), _cspec(T)), out_shape=(S((Bl, T, C), BF16), S((Bl, T, C), F32)),
        compiler_params=_cparams("parallel", "arbitrary"), name=name)(xc3, gates3, gates3, proj3, br, bi, lam)


def _rglru_bwd(dy3, xc3, gates3, proj3, h3, br, bi, lam, name):
    Bl, T, C = xc3.shape
    nsteps = int(math.log2(T))

    def body(dy_ref, xc_ref, rp_ref, ip_ref, ug_ref, h_ref, br_ref, bi_ref, lam_ref,
             dxc_ref, drp_ref, dip_ref, dug_ref, dbr_ref, dbi_ref, dlam_ref):
        row = lax.broadcasted_iota(jnp.int32, (T, CT), 0)
        xc = xc_ref[0]
        r, i, sp, a, mult = _lru_gates(xc, rp_ref[0], ip_ref[0], br_ref, bi_ref, lam_ref)
        h = h_ref[0]
        dy = dy_ref[0]
        ug = ug_ref[0]
        gel, t = _gelu(ug)
        dug_ref[0] = (dy * h * _dgelu(ug, t)).astype(BF16)
        gacc = dy * gel
        an = _shift_up(a, 1, row)
        for st in range(nsteps):
            s = 1 << st
            an_sh = jnp.where(row < T - s, pltpu.roll(an, T - s, axis=0), 1.0)
            gacc = an * _shift_up(gacc, s, row) + gacc
            an = an * an_sh
        da = gacc * _shift_down(h, 1, row)
        ix = i * xc
        d_mult = gacc * ix
        d_i = gacc * mult * xc
        dxc_ref[0] = gacc * mult * i
        d_log_a = da * a - d_mult * (a * a) / mult
        d_r = d_log_a * ((-LRU_C) * sp)
        d_sp = jnp.sum(d_log_a * ((-LRU_C) * r), axis=0, keepdims=True)
        drp = d_r * r * (1.0 - r)
        dip = d_i * i * (1.0 - i)
        drp_ref[0] = drp.astype(BF16)
        dip_ref[0] = dip.astype(BF16)

        @pl.when(pl.program_id(1) == 0)
        def _():
            dbr_ref[...] = jnp.zeros_like(dbr_ref)
            dbi_ref[...] = jnp.zeros_like(dbi_ref)
            dlam_ref[...] = jnp.zeros_like(dlam_ref)

        dbr_ref[...] += jnp.sum(drp, axis=0, keepdims=True)
        dbi_ref[...] += jnp.sum(dip, axis=0, keepdims=True)
        dlam_ref[...] += d_sp * (-jax.nn.sigmoid(-lam_ref[...]))

    vec = S((1, C), F32)
    act = S((Bl, T, C), BF16)
    return pl.pallas_call(
        body, grid=(C // CT, Bl),
        in_specs=[_cspec(T), _cspec(T), _cspec(T), _cspec(T, C // CT), _cspec(T), _cspec(T)] + [_pspec(1)] * 3,
        out_specs=(_cspec(T), _cspec(T), _cspec(T), _cspec(T), _pspec(1), _pspec(1), _pspec(1)),
        out_shape=(S((Bl, T, C), F32), act, act, act, vec, vec, vec),
        compiler_params=_cparams("parallel", "arbitrary"), name=name)(dy3, xc3, gates3, gates3, proj3, h3, br, bi, lam)


NT = (((1,), (1,)), ((), ()))
TN = (((0,), (0,)), ((), ()))


def _hs(h):
    return slice(h * HEAD, (h + 1) * HEAD)


def _head_rows(x):
    head = lax.shift_right_logical(lax.broadcasted_iota(jnp.int32, x.shape, 1), HEAD.bit_length() - 1)
    return jnp.concatenate([jnp.where(head == h, x, jnp.zeros_like(x)) for h in range(MEM_HEADS)], axis=0)


def _head_sum(xbd):
    M = xbd.shape[0] // MEM_HEADS
    head = lax.shift_right_logical(lax.broadcasted_iota(jnp.int32, (M, xbd.shape[1]), 1), HEAD.bit_length() - 1)
    out = jnp.zeros((M, xbd.shape[1]), xbd.dtype)
    for h in range(MEM_HEADS):
        out = jnp.where(head == h, xbd[h * M:(h + 1) * M], out)
    return out


def _mem_probs(q, kbd):
    M = kbd.shape[0] // MEM_HEADS
    s = lax.dot_general(q, kbd, NT, preferred_element_type=F32) * (HEAD ** -0.5)
    ps = []
    for h in range(MEM_HEADS):
        sh = s[:, h * M:(h + 1) * M]
        e = jnp.exp(sh - jnp.max(sh, axis=-1, keepdims=True))
        ps.append(e / jnp.sum(e, axis=-1, keepdims=True))
    return ps


def _mem_attn_fwd(proj3, q_off, mkv3, name):
    Bl, T, _ = proj3.shape
    M = mkv3.shape[1]
    tq = _tile(T, (512, 256, 128))

    def body(q_ref, k_ref, v_ref, o_ref):
        q = q_ref[0].astype(BF16)
        kbd = _head_rows(k_ref[0].astype(BF16))
        vbd = _head_rows(v_ref[0].astype(BF16))
        p = jnp.concatenate(_mem_probs(q, kbd), axis=-1).astype(BF16)
        o_ref[0] = jnp.dot(p, vbd, preferred_element_type=F32).astype(BF16)

    return pl.pallas_call(
        body, grid=(Bl, T // tq),
        in_specs=[pl.BlockSpec((1, tq, MEM_W), lambda b, t: (b, t, q_off // MEM_W)),
                  pl.BlockSpec((1, M, MEM_W), lambda b, t: (b, 0, 0)),
                  pl.BlockSpec((1, M, MEM_W), lambda b, t: (b, 0, 1))],
        out_specs=pl.BlockSpec((1, tq, MEM_W), lambda b, t: (b, t, 0)),
        out_shape=S((Bl, T, MEM_W), BF16), compiler_params=_cparams("parallel", "parallel"), name=name)(
            proj3, mkv3, mkv3)


def _mem_attn_bwd(proj3, q_off, mkv3, do3, name):
    Bl, T, _ = proj3.shape
    M = mkv3.shape[1]
    tq = _tile(T, (512, 256, 128))
    scale = HEAD ** -0.5

    def body(q_ref, k_ref, v_ref, do_ref, dq_ref, dkv_ref):
        q = q_ref[0].astype(BF16)
        kbd = _head_rows(k_ref[0].astype(BF16))
        vbd = _head_rows(v_ref[0].astype(BF16))
        do = do_ref[0].astype(BF16)
        ps = _mem_probs(q, kbd)
        dvbd = lax.dot_general(jnp.concatenate(ps, axis=-1).astype(BF16), do, TN, preferred_element_type=F32)
        dp = lax.dot_general(do, vbd, NT, preferred_element_type=F32)
        dss = []
        for h in range(MEM_HEADS):
            dph = dp[:, h * M:(h + 1) * M]
            dss.append(ps[h] * (dph - jnp.sum(ps[h] * dph, axis=-1, keepdims=True)) * scale)
        ds = jnp.concatenate(dss, axis=-1).astype(BF16)
        dq_ref[0] = jnp.dot(ds, kbd, preferred_element_type=F32).astype(BF16)
        dkbd = lax.dot_general(ds, q, TN, preferred_element_type=F32)

        @pl.when(pl.program_id(1) == 0)
        def _():
            dkv_ref[...] = jnp.zeros_like(dkv_ref)

        dkv_ref[0] += jnp.concatenate([_head_sum(dkbd), _head_sum(dvbd)], axis=-1)

    return pl.pallas_call(
        body, grid=(Bl, T // tq),
        in_specs=[pl.BlockSpec((1, tq, MEM_W), lambda b, t: (b, t, q_off // MEM_W)),
                  pl.BlockSpec((1, M, MEM_W), lambda b, t: (b, 0, 0)),
                  pl.BlockSpec((1, M, MEM_W), lambda b, t: (b, 0, 1)),
                  pl.BlockSpec((1, tq, MEM_W), lambda b, t: (b, t, 0))],
        out_specs=(pl.BlockSpec((1, tq, MEM_W), lambda b, t: (b, t, 0)),
                   pl.BlockSpec((1, M, 2 * MEM_W), lambda b, t: (b, 0, 0))),
        out_shape=(S((Bl, T, MEM_W), BF16), S((Bl, M, 2 * MEM_W), F32)),
        compiler_params=_cparams("parallel", "arbitrary"), name=name)(proj3, mkv3, mkv3, do3)


GROUP_ROWS = SWA_GROUP * WIN


def _group_rows(x, kvh):
    return jnp.concatenate([x[:, _hs(SWA_GROUP * kvh + g)] for g in range(SWA_GROUP)], axis=0)


def _group_col(vals):
    grp = lax.shift_right_logical(lax.broadcasted_iota(jnp.int32, (GROUP_ROWS, 1), 0), WIN.bit_length() - 1)
    col = jnp.full((GROUP_ROWS, 1), vals[-1], F32)
    for g in range(SWA_GROUP - 2, -1, -1):
        col = jnp.where(grp == g, vals[g], col)
    return col


def _swa_probs(qh, kph, kch, sink, slope, has_prev):
    qi = jnp.bitwise_and(lax.broadcasted_iota(jnp.int32, (GROUP_ROWS, WIN), 0), WIN - 1)
    kj = lax.broadcasted_iota(jnp.int32, (GROUP_ROWS, WIN), 1)
    scale = HEAD ** -0.5
    sp = lax.dot_general(qh, kph, NT, preferred_element_type=F32) * scale
    sc = lax.dot_general(qh, kch, NT, preferred_element_type=F32) * scale
    dist_p = (qi + WIN - kj).astype(F32)
    dist_c = (qi - kj).astype(F32)
    neg = -jnp.inf
    sp = jnp.where(kj > qi + jnp.where(has_prev, 0, WIN), sp - slope * dist_p, neg)
    sc = jnp.where(kj <= qi, sc - slope * dist_c, neg)
    m = jnp.maximum(jnp.maximum(jnp.max(sp, axis=-1, keepdims=True), jnp.max(sc, axis=-1, keepdims=True)), sink)
    ep = jnp.exp(sp - m)
    ec = jnp.exp(sc - m)
    es = jnp.exp(sink - m)
    inv = 1.0 / (jnp.sum(ep, axis=-1, keepdims=True) + jnp.sum(ec, axis=-1, keepdims=True) + es)
    return ep * inv, ec * inv, es * inv


def _swa_specs(nb):
    prev = lambda n: jnp.maximum(n - 1, 0)
    q = pl.BlockSpec((1, WIN, MIX_W), lambda b, n: (b, n, 0))
    kp = pl.BlockSpec((1, WIN, MEM_W), lambda b, n: (b, prev(n), 0))
    kc = pl.BlockSpec((1, WIN, MEM_W), lambda b, n: (b, n, 0))
    vp = pl.BlockSpec((1, WIN, MEM_W), lambda b, n: (b, prev(n), 1))
    vc = pl.BlockSpec((1, WIN, MEM_W), lambda b, n: (b, n, 1))
    sm = pl.BlockSpec(memory_space=pltpu.SMEM)
    return q, kp, kc, vp, vc, sm


def _swa_fwd(proj3, kv3, sinks, name):
    Bl, T, _ = proj3.shape
    nb = T // WIN
    q_s, kp_s, kc_s, vp_s, vc_s, sm = _swa_specs(nb)

    def body(q_ref, kp_ref, kc_ref, vp_ref, vc_ref, sink_ref, o_ref):
        has_prev = pl.program_id(1) > 0
        q = q_ref[0].astype(BF16)
        kp, kc = kp_ref[0].astype(BF16), kc_ref[0].astype(BF16)
        vp, vc = vp_ref[0].astype(BF16), vc_ref[0].astype(BF16)
        outs = []
        for kvh in range(SWA_HEADS // SWA_GROUP):
            kvs = _hs(kvh)
            heads = range(SWA_GROUP * kvh, SWA_GROUP * (kvh + 1))
            pp, pc, _ = _swa_probs(_group_rows(q, kvh), kp[:, kvs], kc[:, kvs], _group_col([sink_ref[h] for h in heads]),
                                   _group_col([SLOPES[h] for h in heads]), has_prev)
            og = (jnp.dot(pp.astype(BF16), vp[:, kvs], preferred_element_type=F32)
                  + jnp.dot(pc.astype(BF16), vc[:, kvs], preferred_element_type=F32))
            outs += [og[g * WIN:(g + 1) * WIN] for g in range(SWA_GROUP)]
        o_ref[0] = jnp.concatenate(outs, axis=-1).astype(BF16)

    return pl.pallas_call(
        body, grid=(Bl, nb), in_specs=[q_s, kp_s, kc_s, vp_s, vc_s, sm], out_specs=q_s,
        out_shape=S((Bl, T, MIX_W), BF16), compiler_params=_cparams("parallel", "parallel"), name=name)(
            proj3, kv3, kv3, kv3, kv3, sinks)


def _swa_bwd(proj3, kv3, sinks, do3, name):
    Bl, T, _ = proj3.shape
    nb = T // WIN
    q_s, kp_s, kc_s, vp_s, vc_s, sm = _swa_specs(nb)
    kv_s = pl.BlockSpec((1, WIN, 2 * MEM_W), lambda b, n: (b, n, 0))
    sk_s = pl.BlockSpec((8, LANES), lambda b, n: (0, 0))
    scale = HEAD ** -0.5

    def body(q_ref, kp_ref, kc_ref, vp_ref, vc_ref, sink_ref, do_ref, dq_ref, dkc_ref, dkp_ref, dsk_ref):
        has_prev = pl.program_id(1) > 0
        q = q_ref[0].astype(BF16)
        kp, kc = kp_ref[0].astype(BF16), kc_ref[0].astype(BF16)
        vp, vc = vp_ref[0].astype(BF16), vc_ref[0].astype(BF16)
        do = do_ref[0].astype(BF16)
        lane = lax.broadcasted_iota(jnp.int32, (8, LANES), 1)
        srow = lax.broadcasted_iota(jnp.int32, (8, LANES), 0)
        dsk = jnp.zeros((8, LANES), F32)
        dqs = []
        dkc, dkp, dvc, dvp = [], [], [], []
        grp = lax.shift_right_logical(lax.broadcasted_iota(jnp.int32, (GROUP_ROWS, 1), 0), WIN.bit_length() - 1)
        for kvh in range(SWA_HEADS // SWA_GROUP):
            kvs = _hs(kvh)
            heads = range(SWA_GROUP * kvh, SWA_GROUP * (kvh + 1))
            qg, dog = _group_rows(q, kvh), _group_rows(do, kvh)
            pp, pc, ps = _swa_probs(qg, kp[:, kvs], kc[:, kvs], _group_col([sink_ref[h] for h in heads]),
                                    _group_col([SLOPES[h] for h in heads]), has_prev)
            dpp = lax.dot_general(dog, vp[:, kvs], NT, preferred_element_type=F32)
            dpc = lax.dot_general(dog, vc[:, kvs], NT, preferred_element_type=F32)
            delta = jnp.sum(pp * dpp, axis=-1, keepdims=True) + jnp.sum(pc * dpc, axis=-1, keepdims=True)
            dsp = (pp * (dpp - delta) * scale).astype(BF16)
            dsc = (pc * (dpc - delta) * scale).astype(BF16)
            dqg = (jnp.dot(dsp, kp[:, kvs], preferred_element_type=F32)
                   + jnp.dot(dsc, kc[:, kvs], preferred_element_type=F32))
            dqs += [dqg[g * WIN:(g + 1) * WIN] for g in range(SWA_GROUP)]
            dkc.append(lax.dot_general(dsc, qg, TN, preferred_element_type=F32))
            dkp.append(lax.dot_general(dsp, qg, TN, preferred_element_type=F32))
            dvc.append(lax.dot_general(pc.astype(BF16), dog, TN, preferred_element_type=F32))
            dvp.append(lax.dot_general(pp.astype(BF16), dog, TN, preferred_element_type=F32))
            dsink = ps * delta
            for g, h in enumerate(heads):
                dsk = dsk + jnp.where((lane == h) & (srow == 0), -jnp.sum(jnp.where(grp == g, dsink, 0.0)), 0.0)
        dq_ref[0] = jnp.concatenate(dqs, axis=-1).astype(BF16)
        dkc_ref[0] = jnp.concatenate(dkc + dvc, axis=-1)
        dkp_ref[0] = jnp.concatenate(dkp + dvp, axis=-1)

        @pl.when((pl.program_id(0) == 0) & (pl.program_id(1) == 0))
        def _():
            dsk_ref[...] = jnp.zeros_like(dsk_ref)

        dsk_ref[...] += dsk

    return pl.pallas_call(
        body, grid=(Bl, nb), in_specs=[q_s, kp_s, kc_s, vp_s, vc_s, sm, q_s], out_specs=(q_s, kv_s, kv_s, sk_s),
        out_shape=(S((Bl, T, MIX_W), BF16), S((Bl, T, 2 * MEM_W), F32), S((Bl, T, 2 * MEM_W), F32), S((8, LANES), F32)),
        compiler_params=_cparams("arbitrary", "arbitrary"), name=name)(proj3, kv3, kv3, kv3, kv3, sinks, do3)


def _kv_grad_combine(parts, name):
    Bl, T, W = parts[0][0].shape
    nb = T // WIN
    nl = len(parts)

    def body(*refs):
        o_ref = refs[-1]
        has_next = jnp.where(pl.program_id(1) == nb - 1, 0.0, 1.0)
        acc = None
        for l in range(nl):
            c = refs[2 * l][0] + has_next * refs[2 * l + 1][0]
            acc = c if acc is None else acc + c
        o_ref[0] = acc.astype(BF16)

    cur = pl.BlockSpec((1, WIN, W), lambda b, n: (b, n, 0))
    nxt = pl.BlockSpec((1, WIN, W), lambda b, n: (b, jnp.minimum(n + 1, nb - 1), 0))
    return pl.pallas_call(
        body, grid=(Bl, nb), in_specs=[cur, nxt] * nl, out_specs=cur, out_shape=S((Bl, T, W), BF16),
        compiler_params=_cparams("parallel", "parallel"), name=name)(*[a for pr in parts for a in pr])


def _loss_bwd(y, target, name="loss"):
    N, D = y.shape
    tm = _tile(N, (512, 256, 128))

    def body(y_ref, t_ref, dy_ref, l_ref):
        e = y_ref[...] - t_ref[...]
        dy_ref[...] = e * (1.0 / D)

        @pl.when(pl.program_id(0) == 0)
        def _():
            l_ref[...] = jnp.zeros_like(l_ref)

        l_ref[...] += jnp.sum(e * e, axis=0, keepdims=True) * (0.5 / D)

    row = pl.BlockSpec((tm, D), lambda i: (i, 0))
    vec = pl.BlockSpec((1, D), lambda i: (0, 0))
    return pl.pallas_call(
        body, grid=(N // tm,), in_specs=[row, row], out_specs=(row, vec), out_shape=(S((N, D), F32), S((1, D), F32)),
        compiler_params=_cparams("arbitrary"), name=name)(y, target)


def _all_gather(x, name):
    R, C = x.shape

    def body(x_ref, out_ref, send_sems, recv_sems, local_sem):
        mx, my, mc = lax.axis_index("x"), lax.axis_index("y"), lax.axis_index("c")
        me, sibling = (mx, my, mc), (mx, my, 1 - mc)
        chips = [(1 - mx, my), (mx, 1 - my), (1 - mx, 1 - my)]

        def rows(px, py, pc):
            return out_ref.at[4 * px + 2 * py + pc]

        def copy(kk, block, to, src=None):
            return pltpu.make_async_remote_copy(
                src_ref=rows(*block) if src is None else src, dst_ref=rows(*block), send_sem=send_sems.at[kk],
                recv_sem=recv_sems.at[kk], device_id=to, device_id_type=MESH)

        mine = pltpu.make_async_copy(x_ref, rows(*me), local_sem)
        mine.start()
        first = [copy(0, me, sibling, src=x_ref)]
        first += [copy(1 + j, me, (*chip, mc), src=x_ref) for j, chip in enumerate(chips)]
        for cp in first:
            cp.start()
        passed = [copy(4 + j, (*chip, mc), sibling) for j, chip in enumerate(chips)]
        for j, chip in enumerate(chips):
            copy(1 + j, (*chip, mc), me).wait_recv()
            passed[j].start()
        copy(0, sibling, me).wait_recv()
        for j, chip in enumerate(chips):
            copy(4 + j, (*chip, 1 - mc), me).wait_recv()
        for cp in first + passed:
            cp.wait_send()
        mine.wait()

    return pl.pallas_call(
        body, out_shape=S((N_DEV, R, C), x.dtype), in_specs=[ANY], out_specs=ANY,
        scratch_shapes=[pltpu.SemaphoreType.DMA((7,)), pltpu.SemaphoreType.DMA((7,)), pltpu.SemaphoreType.DMA(())],
        name=name)(x)


def _ag_weights(shards, row_sharded, name):
    n = len(shards)

    def full_shape(a, rows):
        if rows:
            return a.shape[:-2] + (N_DEV * a.shape[-2],) + a.shape[-1:]
        return (N_DEV,) + a.shape

    def body(*refs):
        x_refs, o_refs = refs[:n], refs[n:2 * n]
        send_sems, recv_sems, local_sems = refs[2 * n:]
        mx, my, mc = lax.axis_index("x"), lax.axis_index("y"), lax.axis_index("c")
        me, sibling = (mx, my, mc), (mx, my, 1 - mc)
        chips = [(1 - mx, my), (mx, 1 - my), (1 - mx, 1 - my)]

        def dst(t, px, py, pc):
            d = 4 * px + 2 * py + pc
            if not row_sharded[t]:
                return o_refs[t].at[d]
            r = shards[t].shape[-2]
            idx = (slice(None),) * (shards[t].ndim - 2) + (pl.ds(pl.multiple_of(d * r, 16), r), slice(None))
            return o_refs[t].at[idx]

        def copy(kk, t, block, to, src=None):
            return pltpu.make_async_remote_copy(
                src_ref=dst(t, *block) if src is None else src, dst_ref=dst(t, *block),
                send_sem=send_sems.at[kk * n + t], recv_sem=recv_sems.at[kk * n + t], device_id=to,
                device_id_type=MESH)

        mine = [pltpu.make_async_copy(x_refs[t], dst(t, *me), local_sems.at[t]) for t in range(n)]
        for cp in mine:
            cp.start()
        first = []
        for t in range(n):
            first.append(copy(0, t, me, sibling, src=x_refs[t]))
            first += [copy(1 + j, t, me, (*chip, mc), src=x_refs[t]) for j, chip in enumerate(chips)]
        for cp in first:
            cp.start()
        passed = []
        for j, chip in enumerate(chips):
            for t in range(n):
                copy(1 + j, t, (*chip, mc), me).wait_recv()
                cp = copy(4 + j, t, (*chip, mc), sibling)
                cp.start()
                passed.append(cp)
        for t in range(n):
            copy(0, t, sibling, me).wait_recv()
            for j, chip in enumerate(chips):
                copy(4 + j, t, (*chip, 1 - mc), me).wait_recv()
        for cp in first + passed:
            cp.wait_send()
        for cp in mine:
            cp.wait()

    return pl.pallas_call(
        body, out_shape=tuple(S(full_shape(a, r), a.dtype) for a, r in zip(shards, row_sharded)),
        in_specs=[ANY] * n, out_specs=tuple([ANY] * n),
        scratch_shapes=[pltpu.SemaphoreType.DMA((7 * n,)), pltpu.SemaphoreType.DMA((7 * n,)),
                        pltpu.SemaphoreType.DMA((n,))],
        name=name)(*shards)


FLIPS = [(fx, fy, fc) for fx in (0, 1) for fy in (0, 1) for fc in (0, 1)][1:]
HBM = pl.BlockSpec(memory_space=pltpu.HBM)
SEM = pl.BlockSpec(memory_space=pltpu.SEMAPHORE)
EFFECT = pltpu.SideEffectType.DATAFLOW_SIDE_EFFECTING


def _hbm(a):
    return pltpu.with_memory_space_constraint(a, pltpu.HBM)


def _flips(gather):
    return [(0, 0, 0)] + FLIPS if gather else FLIPS


def _split_copies(gather, s_refs, l_refs, send_sems, recv_sems):
    n = len(s_refs)
    mx, my, mc = lax.axis_index("x"), lax.axis_index("y"), lax.axis_index("c")
    me = 4 * mx + 2 * my + mc
    copies = []
    for k, (fx, fy, fc) in enumerate(_flips(gather)):
        px, py, pc = (1 - mx if fx else mx), (1 - my if fy else my), (1 - mc if fc else mc)
        for t in range(n):
            if gather:
                src = s_refs[t]
                r = src.shape[0]
                dst = l_refs[t].at[pl.ds(pl.multiple_of(me * r, 16), r), :]
            else:
                src = s_refs[t].at[:, 4 * px + 2 * py + pc]
                dst = l_refs[t].at[k]
            copies.append(pltpu.make_async_remote_copy(
                src_ref=src, dst_ref=dst, send_sem=send_sems.at[k * n + t], recv_sem=recv_sems.at[k * n + t],
                device_id=(px, py, pc), device_id_type=MESH))
    return copies


def _split_start(gather, srcs, lands, after, name):
    n = len(srcs)
    n_sem = len(_flips(gather)) * n

    def body(*refs):
        s_refs, l_refs = refs[:n], refs[n:2 * n]
        send_sems, recv_sems = refs[2 * n + 1], refs[2 * n + 2]
        token = refs[-1]
        for cp in _split_copies(gather, s_refs, l_refs, send_sems, recv_sems):
            cp.start()
        token[...] = jnp.zeros_like(token)

    outs = pl.pallas_call(
        body, name=name,
        out_shape=(pltpu.SemaphoreType.DMA((n_sem,)), pltpu.SemaphoreType.DMA((n_sem,)))
        + tuple(pltpu.HBM(a.shape, a.dtype) for a in lands) + (S((8, LANES), F32),),
        in_specs=[HBM] * (2 * n) + [ANY],
        out_specs=(SEM, SEM) + (HBM,) * n + (pl.BlockSpec(memory_space=pltpu.VMEM),),
        input_output_aliases={n + i: 2 + i for i in range(n)},
        compiler_params=pltpu.CompilerParams(has_side_effects=EFFECT),
    )(*[_hbm(a) for a in srcs], *[_hbm(a) for a in lands], after)
    return outs[0], outs[1], list(srcs), list(outs[2:2 + n]), outs[-1]


def _split_wait(gather, send_sems, recv_sems, srcs, lands, after, name):
    n = len(srcs)

    def body(*refs):
        s_refs, l_refs = refs[:n], refs[n:2 * n]
        ssem, rsem = refs[2 * n], refs[2 * n + 1]
        copies = _split_copies(gather, s_refs, l_refs, ssem, rsem)
        for cp in copies:
            cp.wait_send()
        for cp in copies:
            cp.wait_recv()

    outs = pl.pallas_call(
        body, name=name, out_shape=tuple(pltpu.HBM(a.shape, a.dtype) for a in lands),
        in_specs=[HBM] * (2 * n) + [SEM, SEM, ANY], out_specs=(HBM,) * n,
        input_output_aliases={n + i: i for i in range(n)},
        compiler_params=pltpu.CompilerParams(has_side_effects=EFFECT),
    )(*[_hbm(a) for a in srcs], *lands, send_sems, recv_sems, after)
    return list(outs)


def _adamw_math(w, g, m, v):
    m = ADAM_B1 * m + (1.0 - ADAM_B1) * g
    v = ADAM_B2 * v + (1.0 - ADAM_B2) * (g * g)
    m_hat = m / (1.0 - ADAM_B1 ** ADAM_STEP)
    v_hat = v / (1.0 - ADAM_B2 ** ADAM_STEP)
    delta = -ADAM_LR * (m_hat / (jnp.sqrt(v_hat) + ADAM_EPS) + ADAM_WD * w)
    return delta, m, v


def _adamw_layers(owns, gots, w, m, v, name):
    L, B, C = w.shape
    per_row = 2 * L * len(FLIPS) * C * owns[0].dtype.itemsize
    tb = max([t for t in range(16, B + 1, 16) if B % t == 0 and (t * per_row <= 16 * 1024 * 1024 or t == 16)] or [B])
    me = (4 * lax.axis_index("x") + 2 * lax.axis_index("y") + lax.axis_index("c")).astype(jnp.int32).reshape(1)

    def body(me_ref, *refs):
        own_refs, got_refs = refs[:L], refs[L:2 * L]
        w_ref, m_ref, v_ref = refs[2 * L:2 * L + 3]
        g_out, d_out, m_out, v_out = refs[2 * L + 3:]
        layer = pl.program_id(0)
        for kk in range(L):
            @pl.when(layer == kk)
            def _():
                g = own_refs[kk][0].astype(F32)
                for s in range(len(FLIPS)):
                    g = g + got_refs[kk][s].astype(F32)
                d, mn, vn = _adamw_math(w_ref[...], g, m_ref[...], v_ref[...])
                g_out[...] = g
                d_out[...] = d
                m_out[...] = mn
                v_out[...] = vn

    def row(kk, layer, i):
        return jnp.where(layer == kk, i, 0)

    blk = pl.BlockSpec((1, tb, C), lambda layer, i, me_ref: (layer, i, 0))
    own_specs = [pl.BlockSpec((1, 1, tb, C), lambda layer, i, me_ref, kk=kk: (0, me_ref[0], row(kk, layer, i), 0))
                 for kk in range(L)]
    got_specs = [pl.BlockSpec((len(FLIPS), 1, tb, C), lambda layer, i, me_ref, kk=kk: (0, 0, row(kk, layer, i), 0))
                 for kk in range(L)]
    return pl.pallas_call(
        body,
        grid_spec=pltpu.PrefetchScalarGridSpec(
            num_scalar_prefetch=1, grid=(L, B // tb), in_specs=own_specs + got_specs + [blk, blk, blk],
            out_specs=(blk, blk, blk, blk)),
        out_shape=(S((L, B, C), F32),) * 4, compiler_params=_cparams("arbitrary", "arbitrary"), name=name)(
            me, *owns, *gots, w, m, v)


def _adamw_replicated(parts, w, m, v, name):
    R, C = w.shape
    rb = _tile(R, (512, 256, 128, 64, 32, 16, 8))

    def body(p_ref, w_ref, m_ref, v_ref, g_out, d_out, m_out, v_out):
        g = p_ref[0]
        for j in range(1, N_DEV):
            g = g + p_ref[j]
        d, mn, vn = _adamw_math(w_ref[...], g, m_ref[...], v_ref[...])
        g_out[...] = g
        d_out[...] = d
        m_out[...] = mn
        v_out[...] = vn

    blk = pl.BlockSpec((rb, C), lambda i: (i, 0))
    return pl.pallas_call(
        body, grid=(R // rb,), in_specs=[pl.BlockSpec((N_DEV, rb, C), lambda i: (0, i, 0)), blk, blk, blk],
        out_specs=(blk, blk, blk, blk), out_shape=(S((R, C), F32),) * 4, compiler_params=_cparams("parallel"),
        name=name)(parts, w, m, v)


def _pack(arrs, rows_mult, dtype):
    flat = jnp.concatenate([a.reshape(-1).astype(dtype) for a in arrs])
    n = flat.shape[0]
    per = rows_mult * LANES
    tot = -(-n // per) * per
    return jnp.pad(flat, (0, tot - n)).reshape(tot // LANES, LANES)


def _unpack(blob, shapes):
    flat = blob.reshape(-1)
    out, off = [], 0
    for shp in shapes:
        n = int(np.prod(shp))
        out.append(flat[off:off + n].reshape(shp))
        off += n
    return out


def _small_to_natural(g8):
    t = jnp.moveaxis(g8, 0, -2)
    return t.reshape(t.shape[:-2] + (N_DEV * t.shape[-1],))


def _small_to_cols(g):
    t = g.reshape(g.shape[:-1] + (N_DEV, g.shape[-1] // N_DEV))
    return jnp.moveaxis(t, -2, 0)


def _block_diag(w):
    nb, bs, _ = w.shape
    eye = jnp.eye(nb, dtype=w.dtype)
    return (eye[:, None, :, None] * w[:, :, None, :]).reshape(nb * bs, nb * bs)


def _diag_blocks(d, nb, bs):
    d4 = d.reshape(nb, bs, nb, bs)
    return jnp.stack([d4[i, :, i, :] for i in range(nb)])


def kernel(x, mem, g_mix_pre, g_mix_post, g_ffn_pre, g_ffn_post, g_mem, w_mem_kv, w_mix_out, w_ffn_up, w_ffn_conv, b_ffn_conv, w_ffn_down, w_in_a, w_conv_a, b_conv_a, w_rg_r, b_rg_r, w_rg_i, b_rg_i, lru_lambda, w_in_b, sinks_b, g_kv, w_kv, loss_target, m_g_mix_pre, m_g_mix_post, m_g_ffn_pre, m_g_ffn_post, m_g_mem, m_w_mem_kv, m_w_mix_out, m_w_ffn_up, m_w_ffn_conv, m_b_ffn_conv, m_w_ffn_down, m_w_in_a, m_w_conv_a, m_b_conv_a, m_w_rg_r, m_b_rg_r, m_w_rg_i, m_b_rg_i, m_lru_lambda, m_w_in_b, m_sinks_b, m_g_kv, m_w_kv, v_g_mix_pre, v_g_mix_post, v_g_ffn_pre, v_g_ffn_post, v_g_mem, v_w_mem_kv, v_w_mix_out, v_w_ffn_up, v_w_ffn_conv, v_b_ffn_conv, v_w_ffn_down, v_w_in_a, v_w_conv_a, v_b_conv_a, v_w_rg_r, v_b_rg_r, v_w_rg_i, v_b_rg_i, v_lru_lambda, v_w_in_b, v_sinks_b, v_g_kv, v_w_kv):
    w_loc = dict(g_mix_pre=g_mix_pre, g_mix_post=g_mix_post, g_ffn_pre=g_ffn_pre, g_ffn_post=g_ffn_post, g_mem=g_mem,
                 w_mem_kv=w_mem_kv, w_mix_out=w_mix_out, w_ffn_up=w_ffn_up, w_ffn_conv=w_ffn_conv,
                 b_ffn_conv=b_ffn_conv, w_ffn_down=w_ffn_down, w_in_a=w_in_a, w_conv_a=w_conv_a, b_conv_a=b_conv_a,
                 w_rg_r=w_rg_r, b_rg_r=b_rg_r, w_rg_i=w_rg_i, b_rg_i=b_rg_i, lru_lambda=lru_lambda, w_in_b=w_in_b,
                 sinks_b=sinks_b, g_kv=g_kv, w_kv=w_kv)
    m_loc = dict(g_mix_pre=m_g_mix_pre, g_mix_post=m_g_mix_post, g_ffn_pre=m_g_ffn_pre, g_ffn_post=m_g_ffn_post,
                 g_mem=m_g_mem, w_mem_kv=m_w_mem_kv, w_mix_out=m_w_mix_out, w_ffn_up=m_w_ffn_up,
                 w_ffn_conv=m_w_ffn_conv, b_ffn_conv=m_b_ffn_conv, w_ffn_down=m_w_ffn_down, w_in_a=m_w_in_a,
                 w_conv_a=m_w_conv_a, b_conv_a=m_b_conv_a, w_rg_r=m_w_rg_r, b_rg_r=m_b_rg_r, w_rg_i=m_w_rg_i,
                 b_rg_i=m_b_rg_i, lru_lambda=m_lru_lambda, w_in_b=m_w_in_b, sinks_b=m_sinks_b, g_kv=m_g_kv,
                 w_kv=m_w_kv)
    v_loc = dict(g_mix_pre=v_g_mix_pre, g_mix_post=v_g_mix_post, g_ffn_pre=v_g_ffn_pre, g_ffn_post=v_g_ffn_post,
                 g_mem=v_g_mem, w_mem_kv=v_w_mem_kv, w_mix_out=v_w_mix_out, w_ffn_up=v_w_ffn_up,
                 w_ffn_conv=v_w_ffn_conv, b_ffn_conv=v_b_ffn_conv, w_ffn_down=v_w_ffn_down, w_in_a=v_w_in_a,
                 w_conv_a=v_w_conv_a, b_conv_a=v_b_conv_a, w_rg_r=v_w_rg_r, b_rg_r=v_b_rg_r, w_rg_i=v_w_rg_i,
                 b_rg_i=v_b_rg_i, lru_lambda=v_lru_lambda, w_in_b=v_w_in_b, sinks_b=v_sinks_b, g_kv=v_g_kv,
                 w_kv=v_w_kv)

    Bl, T, D = x.shape
    Ml = mem.shape[1]
    N = Bl * T
    depth = g_mix_pre.shape[0]
    n_a = w_in_a.shape[0]
    F = w_ffn_down.shape[1] * N_DEV
    def as_rows(n, a):
        return jnp.swapaxes(a, -1, -2) if n in TRANSPOSED else a

    def layer_keys(l):
        keys = [("w_mem_kv", l), ("w_mix_out", l), ("w_ffn_up", l), ("w_ffn_down", l)]
        keys.append(("w_in_a", l) if l < n_a else ("w_in_b", l - n_a))
        if l == n_a:
            keys.append(("w_kv", None))
        return keys

    def shard_of(key):
        n, i = key
        return as_rows(n, w_loc[n] if i is None else w_loc[n][i]).astype(BF16)

    W = {}
    ffn_names = ("w_ffn_up", "w_ffn_down")
    keys0 = [kk for kk in layer_keys(0) if kk[0] not in ffn_names]
    got0 = _ag_weights([shard_of(kk) for kk in keys0] + [w_loc[n] for n in SMALL_SHARDED],
                       [True] * len(keys0) + [False] * len(SMALL_SHARDED), name="ag_weights_0")
    W.update(zip(keys0, got0))
    for n, a in zip(SMALL_SHARDED, got0[len(keys0):]):
        W[n] = _small_to_natural(a)

    def gather_start(keys, after, tag):
        shards = [shard_of(kk) for kk in keys]
        lands = [lax.empty((N_DEV * s.shape[0],) + s.shape[1:], s.dtype) for s in shards]
        return (keys, tag) + _split_start(True, shards, lands, after, name=f"ag_start_{tag}")

    def gather_wait(pending, after):
        keys, tag, ssem, rsem, srcs, lands, _ = pending
        W.update(zip(keys, _split_wait(True, ssem, rsem, srcs, lands, after, name=f"ag_wait_{tag}")))

    pending_ffn0 = gather_start([kk for kk in layer_keys(0) if kk[0] in ffn_names], got0[0], "ffn_0")

    nblk, bsz = w_rg_r.shape[1], w_rg_r.shape[2]
    wbd = [jnp.concatenate([_block_diag(w_rg_r[j]), _block_diag(w_rg_i[j])], axis=1).astype(BF16) for j in range(n_a)]

    def vec(a):
        return a.reshape(1, -1)

    x2 = x.reshape(N, D)
    mem2 = mem.reshape(Bl * Ml, D)
    saved = []
    kvn = kv3 = x_kv = None
    xs = x2
    for l in range(depth):
        sv = {"x0": xs}
        g_pre = vec(g_mix_pre[l])
        if l + 1 < depth:
            pending = gather_start(layer_keys(l + 1), pending_ffn0[-1] if l == 0 else W[("w_mem_kv", l)], l + 1)
            g_pre = g_pre + pending[-1][0, 0]
        h1 = _rms_fwd(xs, g_pre, BF16, name=f"rms_mixpre_{l}")
        memn = _rms_fwd(mem2, vec(g_mem[l]), BF16, name=f"rms_mem_{l}")
        mkv3 = _mm(memn, W[("w_mem_kv", l)], name=f"mm_memkv_{l}").reshape(Bl, Ml, 2 * MEM_W)
        if l < n_a:
            j = l
            proj = _mm(h1, W[("w_in_a", j)], tb=True, name=f"mm_in_{l}")
            proj3 = proj.reshape(Bl, T, -1)
            xc3 = _conv_fwd_call(proj3, MIX_W, MIX_W, W["w_conv_a"][j], vec(W["b_conv_a"][j]), name=f"conv_a_{l}")
            gates3 = _mm(xc3.reshape(N, MIX_W), wbd[j], name=f"mm_gates_{l}").reshape(Bl, T, 2 * MIX_W)
            y_main3, hs3 = _rglru_fwd(xc3, gates3, proj3, vec(b_rg_r[j]), vec(b_rg_i[j]), vec(W["lru_lambda"][j]),
                                      name=f"rglru_fwd_{l}")
            q_off = 2 * MIX_W
            sv.update(xc3=xc3, gates3=gates3, hs3=hs3)
        else:
            j = l - n_a
            if l == n_a:
                x_kv = xs
                kvn = _rms_fwd(xs, vec(g_kv), BF16, name="rms_kv")
                kv3 = _mm(kvn, W[("w_kv", None)], name="mm_kv").reshape(Bl, T, 2 * MEM_W)
            proj = _mm(h1, W[("w_in_b", j)], name=f"mm_in_{l}")
            proj3 = proj.reshape(Bl, T, -1)
            y_main3 = _swa_fwd(proj3, kv3, sinks_b[j], name=f"swa_fwd_{l}")
            q_off = MIX_W
        y_mem3 = _mem_attn_fwd(proj3, q_off, mkv3, name=f"memattn_fwd_{l}")
        y_main = y_main3.reshape(N, MIX_W)
        y_mem = y_mem3.reshape(N, MEM_W)
        y = _mm_sum([(y_main, W[("w_mix_out", l)], (0, 0)), (y_mem, W[("w_mix_out", l)], (MIX_W, 0))], n=D,
                    name=f"mm_mixout_{l}")
        x1 = _rms_fwd(y, vec(g_mix_post[l]), F32, res=xs, name=f"rms_mixpost_{l}")
        h2 = _rms_fwd(x1, vec(g_ffn_pre[l]), BF16, name=f"rms_ffnpre_{l}")
        if l == 0:
            gather_wait(pending_ffn0, h2)
        u3 = _mm(h2, W[("w_ffn_up", l)], tb=True, name=f"mm_up_{l}").reshape(Bl, T, 2 * F)
        act3 = _ffn_mid_fwd(u3, W["w_ffn_conv"][l], vec(b_ffn_conv[l]), name=f"ffn_mid_fwd_{l}")
        act = act3.reshape(N, F)
        f = _mm(act, W[("w_ffn_down", l)], name=f"mm_down_{l}")
        x_next = _rms_fwd(f, vec(g_ffn_post[l]), F32, res=x1, name=f"rms_ffnpost_{l}")
        if l + 1 < depth:
            gather_wait(pending, x_next)
        sv.update(h1=h1, memn=memn, mkv3=mkv3, proj3=proj3, q_off=q_off, y_main=y_main, y_mem=y_mem, y=y, x1=x1,
                  h2=h2, u3=u3, act=act, f=f)
        saved.append(sv)
        xs = x_next

    dxs, loss_vec = _loss_bwd(xs, loss_target.reshape(N, D))
    loss = lax.psum(jnp.sum(loss_vec), ("x", "y", "c"))

    G = {n: [None] * w_loc[n].shape[0] for n in REPL + SMALL_SHARDED if n != "g_kv"}
    GW = {}

    def dw(key, off, a, b_, nm):
        GW[key] = _mm(a, b_, ta=True, out_dtype=BF16, into=(GW.get(key), (1,) + W[key].shape, 0, off), name=nm)

    def grad_blocks(key):
        g = GW[key]
        return g.reshape(1, N_DEV, g.shape[1] // N_DEV, g.shape[2])

    reduces = []

    def reduce_start(keys, after, tag):
        srcs = [grad_blocks(kk) for kk in keys]
        lands = [lax.empty((len(FLIPS),) + s.shape[:1] + s.shape[2:], s.dtype) for s in srcs]
        started = _split_start(False, srcs, lands, after, name=f"rs_start_{tag}")
        reduces.append((keys, tag) + started)
        return started[-1]

    kv_parts = []
    for l in reversed(range(depth)):
        sv = saved[l]
        proj3 = sv["proj3"]
        df, dg = _rms_bwd(sv["f"], vec(g_ffn_post[l]), dxs, out_dtype=BF16, name=f"rmsb_ffnpost_{l}")
        G["g_ffn_post"][l] = dg[0]
        dact = _mm(df, W[("w_ffn_down", l)], tb=True, name=f"mmb_down_dx_{l}")
        dw(("w_ffn_down", l), (0, 0), sv["act"], df, f"mmb_down_dw_{l}")
        dug3, duv3, dwg, dwv, dbg, dbv = _ffn_mid_bwd(sv["u3"], dact.reshape(Bl, T, F),
                                                      W["w_ffn_conv"][l], vec(b_ffn_conv[l]), name=f"ffn_mid_bwd_{l}")
        G["w_ffn_conv"][l] = jnp.concatenate([dwg, dwv], axis=1)
        G["b_ffn_conv"][l] = jnp.concatenate([dbg, dbv], axis=1)[0]
        dug, duv = dug3.reshape(N, F), duv3.reshape(N, F)
        dw(("w_ffn_up", l), (0, 0), dug, sv["h2"], f"mmb_up_dw_g_{l}")
        dw(("w_ffn_up", l), (F, 0), duv, sv["h2"], f"mmb_up_dw_v_{l}")
        tok = reduce_start([("w_ffn_down", l), ("w_ffn_up", l)], dug, f"ffn_{l}")
        dh2 = _mm_sum([(dug, W[("w_ffn_up", l)], (0, 0)), (duv, W[("w_ffn_up", l)], (F, 0))], n=D, after=tok,
                      name=f"mmb_up_dx_{l}")
        dx1, dg = _rms_bwd(sv["x1"], vec(g_ffn_pre[l]), dh2, add=dxs, name=f"rmsb_ffnpre_{l}")
        G["g_ffn_pre"][l] = dg[0]
        dy, dg = _rms_bwd(sv["y"], vec(g_mix_post[l]), dx1, out_dtype=BF16, name=f"rmsb_mixpost_{l}")
        G["g_mix_post"][l] = dg[0]
        dy_main = _mm(dy, W[("w_mix_out", l)], tb=True, n=MIX_W, k=D, name=f"mmb_mixout_dmain_{l}")
        dy_mem = _mm(dy, W[("w_mix_out", l)], tb=True, n=MEM_W, k=D, b_off=(MIX_W, 0),
                     name=f"mmb_mixout_dmem_{l}")
        dw(("w_mix_out", l), (0, 0), sv["y_main"], dy, f"mmb_mixout_dw_main_{l}")
        dw(("w_mix_out", l), (MIX_W, 0), sv["y_mem"], dy, f"mmb_mixout_dw_mem_{l}")
        dq_mem3, dmkv3 = _mem_attn_bwd(proj3, sv["q_off"], sv["mkv3"], dy_mem.reshape(Bl, T, MEM_W),
                                       name=f"memattn_bwd_{l}")
        dq_mem = dq_mem3.reshape(N, MEM_W)
        dmkv = dmkv3.reshape(Bl * Ml, 2 * MEM_W)
        dw(("w_mem_kv", l), (0, 0), sv["memn"], dmkv, f"mmb_memkv_dw_{l}")
        dmemn = _mm(dmkv, W[("w_mem_kv", l)], tb=True, name=f"mmb_memkv_dx_{l}")
        _, dg = _rms_bwd(mem2, vec(g_mem[l]), dmemn, name=f"rmsb_mem_{l}")
        G["g_mem"][l] = dg[0]
        dy_main3 = dy_main.reshape(Bl, T, MIX_W)
        if l < n_a:
            j = l
            dxc3, drp3, dip3, dugate3, dbr, dbi, dlam = _rglru_bwd(
                dy_main3, sv["xc3"], sv["gates3"], proj3, sv["hs3"], vec(b_rg_r[j]), vec(b_rg_i[j]),
                vec(W["lru_lambda"][j]), name=f"rglru_bwd_{l}")
            G["b_rg_r"][j] = dbr.reshape(nblk, bsz)
            G["b_rg_i"][j] = dbi.reshape(nblk, bsz)
            G["lru_lambda"][j] = dlam[0]
            drp, dip = drp3.reshape(N, MIX_W), dip3.reshape(N, MIX_W)
            xc2 = sv["xc3"].reshape(N, MIX_W)
            G["w_rg_r"][j] = _diag_blocks(_mm(xc2, drp, ta=True, name=f"mmb_gates_dw_r_{l}"), nblk, bsz)
            G["w_rg_i"][j] = _diag_blocks(_mm(xc2, dip, ta=True, name=f"mmb_gates_dw_i_{l}"), nblk, bsz)
            dxc = _mm_sum([(drp, wbd[j], (0, 0)), (dip, wbd[j], (0, MIX_W))], tb=True, n=MIX_W,
                          add=dxc3.reshape(N, MIX_W), name=f"mmb_gates_dx_{l}")
            dux3, dwc, dbc = _conv_bwd_call(dxc.reshape(Bl, T, MIX_W), proj3, MIX_W, MIX_W, W["w_conv_a"][j],
                                            name=f"conv_a_bwd_{l}")
            G["w_conv_a"][j] = dwc
            G["b_conv_a"][j] = dbc[0]
            pieces = [(dugate3.reshape(N, MIX_W), 0), (dux3.reshape(N, MIX_W), MIX_W), (dq_mem, 2 * MIX_W)]
            in_key = ("w_in_a", j)
        else:
            j = l - n_a
            dq3, dkc, dkp, dsk = _swa_bwd(proj3, kv3, sinks_b[j], dy_main3, name=f"swa_bwd_{l}")
            kv_parts.append((dkc, dkp))
            G["sinks_b"][j] = dsk[0, :SWA_HEADS]
            pieces = [(dq3.reshape(N, MIX_W), 0), (dq_mem, MIX_W)]
            in_key = ("w_in_b", j)
        in_t = in_key[0] in TRANSPOSED
        for pi, (piece, off) in enumerate(pieces):
            if in_t:
                dw(in_key, (off, 0), piece, sv["h1"], f"mmb_in_dw_{pi}_{l}")
            else:
                dw(in_key, (0, off), sv["h1"], piece, f"mmb_in_dw_{pi}_{l}")
        tok = reduce_start([("w_mix_out", l), ("w_mem_kv", l), in_key], dy, f"mix_{l}")
        dh1 = _mm_sum([(piece, W[in_key], (off, 0) if in_t else (0, off)) for piece, off in pieces], tb=not in_t, n=D,
                      after=tok, name=f"mmb_in_dx_{l}")
        dxs, dg = _rms_bwd(sv["x0"], vec(g_mix_pre[l]), dh1, add=dx1, name=f"rmsb_mixpre_{l}")
        G["g_mix_pre"][l] = dg[0]
        if l == n_a:
            dkv = _kv_grad_combine(kv_parts, name="kv_grad_combine").reshape(N, 2 * MEM_W)
            dw(("w_kv", None), (0, 0), kvn, dkv, "mmb_kv_dw")
            tok = reduce_start([("w_kv", None)], dkv, "kv")
            dkvn = _mm(dkv, W[("w_kv", None)], tb=True, after=tok, name="mmb_kv_dx")
            dxs, dg = _rms_bwd(x_kv, vec(g_kv), dkvn, add=dxs, name="rmsb_kv")
            G["g_kv"] = dg[0]
    grad_x = dxs.reshape(Bl, T, D)
    Gf = {n: (jnp.stack(g) if isinstance(g, list) else g) for n, g in G.items()}

    small4 = []
    for n in SMALL_SHARDED:
        t = _small_to_cols(Gf[n]).astype(BF16)
        small4.append(t.reshape(1, N_DEV, -1, t.shape[-1]))
    small_lands = [lax.empty((len(FLIPS),) + s.shape[:1] + s.shape[2:], s.dtype) for s in small4]
    small_started = _split_start(False, small4, small_lands, dxs, name="rs_start_small")
    r_blob = _pack([Gf[n].astype(F32) for n in REPL], REPL_ROWS, F32)
    r_parts = _all_gather(r_blob, name="ag_repl_grads")
    parts = {}
    for keys, tag, ssem, rsem, srcs, lands, _ in reduces:
        for kk, s, g7 in zip(keys, srcs, _split_wait(False, ssem, rsem, srcs, lands, small_started[-1],
                                                     name=f"rs_wait_{tag}")):
            parts[kk] = (s, g7)

    res = [{} for _ in range(4)]
    for n, _ in SHARDED:
        if n in SMALL_SHARDED:
            continue
        idx = [None] if w_loc[n].ndim == 2 else list(range(w_loc[n].shape[0]))
        wmv = [as_rows(n, a[n]) for a in (w_loc, m_loc, v_loc)]
        shp3 = (len(idx),) + wmv[0].shape[-2:]
        outs = _adamw_layers([parts[(n, i)][0] for i in idx], [parts[(n, i)][1] for i in idx],
                             *[a.reshape(shp3) for a in wmv], name=f"adamw_{n}")
        for k in range(4):
            res[k][n] = as_rows(n, outs[k].reshape(wmv[0].shape))
    last = res[0]["w_kv"]
    small_got = _split_wait(False, *small_started[:4], last, name="rs_wait_small")
    for n, own, g7 in zip(SMALL_SHARDED, small4, small_got):
        shp3 = own.shape[:1] + own.shape[2:]
        outs = _adamw_layers([own], [g7], w_loc[n].reshape(shp3), m_loc[n].reshape(shp3), v_loc[n].reshape(shp3),
                             name=f"adamw_{n}")
        for k in range(4):
            res[k][n] = outs[k].reshape(w_loc[n].shape)
    outs_rp = _adamw_replicated(r_parts, _pack([w_loc[n] for n in REPL], REPL_ROWS, F32),
                                _pack([m_loc[n] for n in REPL], REPL_ROWS, F32),
                                _pack([v_loc[n] for n in REPL], REPL_ROWS, F32),
                                name="adamw_replicated")
    rp_shapes = [w_loc[n].shape for n in REPL]
    for k in range(4):
        res[k].update(zip(REPL, _unpack(outs_rp[k], rp_shapes)))
    out = [loss, grad_x]
    for k in range(4):
        out += [res[k][n] for n in WEIGHTS]
    return tuple(out)
```

```python
import functools
import math

import numpy as np
import jax
import jax.numpy as jnp
from jax import lax
from jax.experimental import pallas as pl
from jax.experimental.pallas import tpu as pltpu

F32 = jnp.float32
BF16 = jnp.bfloat16
S = jax.ShapeDtypeStruct
MESH = pl.DeviceIdType.MESH
ANY = pl.BlockSpec(memory_space=pl.ANY)

HEAD = 64
MEM_HEADS = 4
MEM_W = MEM_HEADS * HEAD
SWA_HEADS = 12
SWA_GROUP = 3
MIX_W = SWA_HEADS * HEAD
WIN = 128
LRU_C = 8.0
EPS = 1e-6
ADAM_LR, ADAM_B1, ADAM_B2, ADAM_EPS, ADAM_WD, ADAM_STEP = 0.001, 0.9, 0.999, 1e-08, 0.01, 10
GELU_C0 = math.sqrt(2.0 / math.pi)
GELU_C1 = 0.044715
N_DEV = 8
LANES = 128
CT = 128
VMEM_LIMIT = 48 * 1024 * 1024
MM_VMEM_BUDGET = 36 * 1024 * 1024
REPL_ROWS = 256

SHARDED = (("w_mem_kv", 1), ("w_mix_out", 1), ("w_ffn_up", 2), ("w_ffn_conv", 2), ("w_ffn_down", 1), ("w_in_a", 2),
           ("w_conv_a", 2), ("b_conv_a", 1), ("lru_lambda", 1), ("w_in_b", 1), ("w_kv", 0))
SMALL_SHARDED = ("w_ffn_conv", "w_conv_a", "b_conv_a", "lru_lambda")
TRANSPOSED = ("w_ffn_up", "w_in_a")
REPL = ("g_mix_pre", "g_mix_post", "g_ffn_pre", "g_ffn_post", "g_mem", "b_ffn_conv", "w_rg_r", "b_rg_r", "w_rg_i",
        "b_rg_i", "sinks_b", "g_kv")
WEIGHTS = ("g_mix_pre", "g_mix_post", "g_ffn_pre", "g_ffn_post", "g_mem", "w_mem_kv", "w_mix_out", "w_ffn_up",
           "w_ffn_conv", "b_ffn_conv", "w_ffn_down", "w_in_a", "w_conv_a", "b_conv_a", "w_rg_r", "b_rg_r", "w_rg_i",
           "b_rg_i", "lru_lambda", "w_in_b", "sinks_b", "g_kv", "w_kv")


def _alibi_slopes(n):
    def pow2(m):
        start = 2.0 ** (-8.0 / m)
        return [start ** (i + 1) for i in range(m)]
    c = 2 ** int(math.floor(math.log2(n)))
    s = pow2(c)
    if c != n:
        s = s + pow2(2 * c)[0::2][: n - c]
    return [float(v) for v in np.asarray(s, dtype=np.float32)]


SLOPES = _alibi_slopes(SWA_HEADS)


def _tile(n, cands):
    for c in cands:
        if n % c == 0:
            return c
    return n


def _cparams(*sem):
    return pltpu.CompilerParams(dimension_semantics=sem, vmem_limit_bytes=VMEM_LIMIT)


def _mm_tiles(M, N, K, a_bytes, b_bytes, o_bytes, add_bytes, offsets):
    m_off, n_offs, k_off = offsets
    tms = [c for c in (1024, 512, 256, 128) if M % c == 0 and m_off % c == 0] or [M]
    tns = [c for c in (1408, 1024, 896, 768, 512, 384, 256, 128)
           if N % c == 0 and all(o % c == 0 for o in n_offs)] or [N]
    tks = [c for c in (K, 2048, 1408, 1024, 512, 256, 128) if c <= K and K % c == 0 and k_off % c == 0]
    best = None
    for tk in tks:
        fits = []
        for tm in tms:
            for tn in tns:
                need = 2 * (tm * tk * a_bytes + tk * tn * b_bytes + tm * tn * (o_bytes + add_bytes))
                need += tm * tn * 4 * (2 if tk < K else 1)
                need += (tm * tk * 2 if a_bytes != 2 else 0) + (tk * tn * 2 if b_bytes != 2 else 0)
                if need <= MM_VMEM_BUDGET:
                    fits.append((tm * tn, min(tm, 512), tm, tn))
        if fits:
            _, _, tm, tn = max(fits)
            best = (tm, tn, tk)
            break
    assert best is not None, (M, N, K)
    return best


def _mm(a, b, *, ta=False, tb=False, n=None, k=None, b_off=(0, 0), out_dtype=F32, add=None, into=None, after=None,
        name="mm"):
    if ta:
        K, M = a.shape
    else:
        M, K = a.shape
    if tb:
        N = b.shape[-2] if n is None else n
    else:
        N = b.shape[-1] if n is None else n
    assert k is None or k == K
    ro, co = b_off
    n_off, k_off = (ro, co) if tb else (co, ro)
    oro, oco = (0, 0) if into is None else into[3]
    tm, tn, tk = _mm_tiles(M, N, K, a.dtype.itemsize, b.dtype.itemsize, jnp.dtype(out_dtype).itemsize,
                           0 if add is None else add.dtype.itemsize, (oro, (n_off, oco), k_off))
    nk = K // tk
    if tb:
        b_spec = pl.BlockSpec((tn, tk), lambda i, j, kk: (j + ro // tn, kk + co // tk))
        b_dims = (1,)
    else:
        b_spec = pl.BlockSpec((tk, tn), lambda i, j, kk: (kk + ro // tk, j + co // tn))
        b_dims = (0,)
    if ta:
        a_spec = pl.BlockSpec((tk, tm), lambda i, j, kk: (kk, i))
        a_dims = (0,)
    else:
        a_spec = pl.BlockSpec((tm, tk), lambda i, j, kk: (i, kk))
        a_dims = (1,)
    dims = ((a_dims, b_dims), ((), ()))
    add_spec = pl.BlockSpec((tm, tn), lambda i, j, kk: (i, j))
    has_add = add is not None
    if into is None:
        o_spec, o_shape, buf = add_spec, (M, N), None
    else:
        buf, o_shape, ol, _ = into
        assert not has_add
        o_spec = pl.BlockSpec((None, tm, tn), lambda i, j, kk: (ol, i + oro // tm, j + oco // tn))
    has_buf = buf is not None

    def body(*refs):
        refs = list(refs)
        acc_ref = refs.pop() if nk > 1 else None
        o_ref = refs.pop()
        a_ref, b_ref = refs[0], refs[1]
        add_ref = refs[2] if has_add else None
        part = lax.dot_general(a_ref[...].astype(BF16), b_ref[...].astype(BF16), dims, preferred_element_type=F32)

        def finish(r):
            if has_add:
                r = r + add_ref[...].astype(F32)
            o_ref[...] = r.astype(out_dtype)

        if nk == 1:
            finish(part)
        else:
            kk = pl.program_id(2)

            @pl.when(kk == 0)
            def _():
                acc_ref[...] = part

            @pl.when(kk > 0)
            def _():
                acc_ref[...] += part

            @pl.when(kk == nk - 1)
            def _():
                finish(acc_ref[...])

    in_specs = [a_spec, b_spec] + ([add_spec] if has_add else []) + ([ANY] if has_buf else [])
    args = (a, b) + ((add,) if has_add else ()) + ((buf,) if has_buf else ())
    if after is not None:
        in_specs, args = in_specs + [ANY], args + (after,)
    return pl.pallas_call(
        body, grid=(M // tm, N // tn, nk), in_specs=in_specs, out_specs=o_spec,
        out_shape=S(o_shape, out_dtype), scratch_shapes=[pltpu.VMEM((tm, tn), F32)] if nk > 1 else [],
        input_output_aliases={2: 0} if has_buf else {},
        compiler_params=_cparams("parallel", "parallel", "arbitrary"), name=name)(*args)


def _mm_sum(pieces, *, tb=False, n, out_dtype=F32, add=None, after=None, name="mm_sum"):
    M = pieces[0][0].shape[0]
    ks = [a.shape[1] for a, _, _ in pieces]
    a_bytes = max(a.dtype.itemsize for a, _, _ in pieces)
    b_bytes = max(b.dtype.itemsize for _, b, _ in pieces)
    n_offs = tuple(off[0] if tb else off[1] for _, _, off in pieces)
    for kp, (_, _, off) in zip(ks, pieces):
        assert (off[1] if tb else off[0]) % kp == 0
    tm, tn, tk = _mm_tiles(M, n, sum(ks), a_bytes, b_bytes, jnp.dtype(out_dtype).itemsize, 0, (0, n_offs, 0))
    assert tk == sum(ks)
    a_specs = [pl.BlockSpec((tm, kp), lambda i, j: (i, 0)) for kp in ks]
    if tb:
        b_specs = [pl.BlockSpec((tn, kp), lambda i, j, ro=off[0], co=off[1], kp=kp: (j + ro // tn, co // kp))
                   for kp, (_, _, off) in zip(ks, pieces)]
        dims = NT
    else:
        b_specs = [pl.BlockSpec((kp, tn), lambda i, j, ro=off[0], co=off[1], kp=kp: (ro // kp, j + co // tn))
                   for kp, (_, _, off) in zip(ks, pieces)]
        dims = (((1,), (0,)), ((), ()))
    npc = len(pieces)
    o_spec = pl.BlockSpec((tm, tn), lambda i, j: (i, j))

    def body(*refs):
        o_ref = refs[2 * npc + (add is not None) + (after is not None)]
        acc = refs[2 * npc][...].astype(F32) if add is not None else None
        for p in range(npc):
            part = lax.dot_general(refs[p][...].astype(BF16), refs[npc + p][...].astype(BF16), dims,
                                   preferred_element_type=F32)
            acc = part if acc is None else acc + part
        o_ref[...] = acc.astype(out_dtype)

    args = [a for a, _, _ in pieces] + [b for _, b, _ in pieces]
    in_specs = a_specs + b_specs
    if add is not None:
        in_specs, args = in_specs + [o_spec], args + [add]
    if after is not None:
        in_specs, args = in_specs + [ANY], args + [after]
    return pl.pallas_call(
        body, grid=(M // tm, n // tn), in_specs=in_specs, out_specs=o_spec,
        out_shape=S((M, n), out_dtype), compiler_params=_cparams("parallel", "parallel"), name=name)(*args)


def _rms_fwd(x, g, out_dtype, res=None, name="rms_fwd"):
    N, D = x.shape
    tm = _tile(N, (512, 256, 128))
    has_res = res is not None

    def body(*refs):
        if has_res:
            x_ref, g_ref, r_ref, o_ref = refs
        else:
            x_ref, g_ref, o_ref = refs
        xv = x_ref[...].astype(F32)
        y = xv * lax.rsqrt(jnp.mean(xv * xv, axis=-1, keepdims=True) + EPS) * g_ref[...]
        if has_res:
            y = y + r_ref[...]
        o_ref[...] = y.astype(out_dtype)

    row = pl.BlockSpec((tm, D), lambda i: (i, 0))
    vec = pl.BlockSpec((1, D), lambda i: (0, 0))
    return pl.pallas_call(
        body, grid=(N // tm,), in_specs=[row, vec] + ([row] if has_res else []), out_specs=row,
        out_shape=S((N, D), out_dtype), compiler_params=_cparams("parallel"), name=name)(
            *((x, g) + ((res,) if has_res else ())))


def _rms_bwd(x, g, dy, add=None, out_dtype=F32, name="rms_bwd"):
    N, D = x.shape
    tm = _tile(N, (512, 256, 128))
    has_add = add is not None

    def body(*refs):
        if has_add:
            x_ref, g_ref, dy_ref, add_ref, dx_ref, dg_ref = refs
        else:
            x_ref, g_ref, dy_ref, dx_ref, dg_ref = refs
        xv = x_ref[...].astype(F32)
        dyv = dy_ref[...].astype(F32)
        r = lax.rsqrt(jnp.mean(xv * xv, axis=-1, keepdims=True) + EPS)
        u = dyv * g_ref[...]
        dx = r * u - xv * (r * r * r * jnp.mean(u * xv, axis=-1, keepdims=True))
        if has_add:
            dx = dx + add_ref[...]
        dx_ref[...] = dx.astype(out_dtype)

        @pl.when(pl.program_id(0) == 0)
        def _():
            dg_ref[...] = jnp.zeros_like(dg_ref)

        dg_ref[...] += jnp.sum(dyv * xv * r, axis=0, keepdims=True)

    row = pl.BlockSpec((tm, D), lambda i: (i, 0))
    vec = pl.BlockSpec((1, D), lambda i: (0, 0))
    return pl.pallas_call(
        body, grid=(N // tm,), in_specs=[row, vec, row] + ([row] if has_add else []), out_specs=(row, vec),
        out_shape=(S((N, D), out_dtype), S((1, D), F32)), compiler_params=_cparams("arbitrary"), name=name)(
            *((x, g, dy) + ((add,) if has_add else ())))


def _rms_pair_fwd(y, g_post, res, gains, name):
    N, D = y.shape
    tm = _tile(N, (512, 256, 128))
    ng = len(gains)

    def body(*refs):
        y_ref, gp_ref, r_ref = refs[:3]
        g_refs = refs[3:3 + ng]
        x_ref = refs[3 + ng]
        h_refs = refs[4 + ng:]
        yv = y_ref[...]
        x = r_ref[...] + yv * lax.rsqrt(jnp.mean(yv * yv, axis=-1, keepdims=True) + EPS) * gp_ref[...]
        x_ref[...] = x
        xn = x * lax.rsqrt(jnp.mean(x * x, axis=-1, keepdims=True) + EPS)
        for g_ref, h_ref in zip(g_refs, h_refs):
            h_ref[...] = (xn * g_ref[...]).astype(BF16)

    row = pl.BlockSpec((tm, D), lambda i: (i, 0))
    vec = pl.BlockSpec((1, D), lambda i: (0, 0))
    return pl.pallas_call(
        body, grid=(N // tm,), in_specs=[row, vec, row] + [vec] * ng, out_specs=(row,) * (1 + ng),
        out_shape=(S((N, D), F32),) + (S((N, D), BF16),) * ng, compiler_params=_cparams("parallel"), name=name)(
            y, g_post, res, *gains)


def _rms_pair_bwd(xa, ga, dya, add, xb, gb, name):
    N, D = xa.shape
    tm = _tile(N, (512, 256, 128))

    def one(xv, g_ref, dyv):
        r = lax.rsqrt(jnp.mean(xv * xv, axis=-1, keepdims=True) + EPS)
        u = dyv * g_ref[...]
        dx = r * u - xv * (r * r * r * jnp.mean(u * xv, axis=-1, keepdims=True))
        return dx, jnp.sum(dyv * xv * r, axis=0, keepdims=True)

    def body(xa_ref, ga_ref, dya_ref, add_ref, xb_ref, gb_ref, da_ref, db_ref, dga_ref, dgb_ref):
        da, dga = one(xa_ref[...].astype(F32), ga_ref, dya_ref[...].astype(F32))
        da = da + add_ref[...]
        da_ref[...] = da
        db, dgb = one(xb_ref[...].astype(F32), gb_ref, da)
        db_ref[...] = db.astype(BF16)

        @pl.when(pl.program_id(0) == 0)
        def _():
            dga_ref[...] = jnp.zeros_like(dga_ref)
            dgb_ref[...] = jnp.zeros_like(dgb_ref)

        dga_ref[...] += dga
        dgb_ref[...] += dgb

    row = pl.BlockSpec((tm, D), lambda i: (i, 0))
    vec = pl.BlockSpec((1, D), lambda i: (0, 0))
    return pl.pallas_call(
        body, grid=(N // tm,), in_specs=[row, vec, row, row, row, vec], out_specs=(row, row, vec, vec),
        out_shape=(S((N, D), F32), S((N, D), BF16), S((1, D), F32), S((1, D), F32)),
        compiler_params=_cparams("arbitrary"), name=name)(xa, ga, dya, add, xb, gb)


def _shift_down(x, s, row):
    return jnp.where(row >= s, pltpu.roll(x, s, axis=0), 0.0)


def _shift_up(x, s, row):
    T = x.shape[0]
    return jnp.where(row < T - s, pltpu.roll(x, T - s, axis=0), 0.0)


SLAB = 16


def _conv_wrap(x, w_ref, b_ref):
    W = w_ref.shape[0]
    y = x * w_ref[W - 1:W, :] + b_ref[...]
    for s in range(1, W):
        y = y + pltpu.roll(x, s, axis=0) * w_ref[W - 1 - s:W - s, :]
    return y


def _conv_rows(x_ref, w_ref, b_ref, lo, hi):
    W = w_ref.shape[0]
    y = x_ref[lo:hi, :] * w_ref[W - 1:W, :] + b_ref[...]
    for s in range(1, W):
        y = y + x_ref[lo - s:hi - s, :] * w_ref[W - 1 - s:W - s, :]
    return y


def _taps(x_ref, W):
    T = x_ref.shape[0]
    head = x_ref[0:SLAB, :]
    row = lax.broadcasted_iota(jnp.int32, head.shape, 0)
    return [x_ref[...]] + [jnp.concatenate([_shift_down(head, s, row), x_ref[SLAB - s:T - s, :]], axis=0)
                           for s in range(1, W)]


def _conv_taps(xs, w_ref, b_ref):
    W = w_ref.shape[0]
    y = xs[0] * w_ref[W - 1:W, :] + b_ref[...]
    for s in range(1, W):
        y = y + xs[s] * w_ref[W - 1 - s:W - s, :]
    return y


def _conv_head(x_head, w_ref, b_ref):
    row = lax.broadcasted_iota(jnp.int32, x_head.shape, 0)
    return _conv_taps([x_head] + [_shift_down(x_head, s, row) for s in range(1, w_ref.shape[0])], w_ref, b_ref)


def _conv_bwd_taps(dy, xs, w_ref, row):
    W = w_ref.shape[0]
    dx = dy * w_ref[W - 1:W, :]
    dws = [None] * W
    dws[W - 1] = jnp.sum(dy * xs[0], axis=0, keepdims=True)
    for s in range(1, W):
        dx = dx + _shift_up(dy, s, row) * w_ref[W - 1 - s:W - s, :]
        dws[W - 1 - s] = jnp.sum(dy * xs[s], axis=0, keepdims=True)
    return dx, jnp.concatenate(dws, axis=0), jnp.sum(dy, axis=0, keepdims=True)


def _conv_bwd_wrap(dy, x, w_ref):
    W = w_ref.shape[0]
    T = dy.shape[0]
    dx = dy * w_ref[W - 1:W, :]
    dws = [None] * W
    dws[W - 1] = jnp.sum(dy * x, axis=0, keepdims=True)
    for s in range(1, W):
        up = pltpu.roll(dy, T - s, axis=0)
        dx = dx + up * w_ref[W - 1 - s:W - s, :]
        dws[W - 1 - s] = jnp.sum(up * x, axis=0, keepdims=True)
    return dx, jnp.concatenate(dws, axis=0), jnp.sum(dy, axis=0, keepdims=True)


def _conv_bwd_fix(dy_head, dy_tail, x_tail, w_ref):
    row = lax.broadcasted_iota(jnp.int32, dy_tail.shape, 0)
    W = w_ref.shape[0]
    dx = dy_tail * w_ref[W - 1:W, :]
    extra = [jnp.zeros((1, dy_tail.shape[1]), F32)] * W
    for s in range(1, W):
        dx = dx + _shift_up(dy_tail, s, row) * w_ref[W - 1 - s:W - s, :]
        extra[W - 1 - s] = jnp.sum(jnp.where(row < s, dy_head * pltpu.roll(x_tail, s, axis=0), 0.0), axis=0,
                                   keepdims=True)
    return dx, jnp.concatenate(extra, axis=0)


def _gelu(g):
    t = jnp.tanh(GELU_C0 * (g + GELU_C1 * g * g * g))
    return 0.5 * g * (1.0 + t), t


def _dgelu(g, t):
    return 0.5 * (1.0 + t) + 0.5 * g * (1.0 - t * t) * (GELU_C0 * (1.0 + 3.0 * GELU_C1 * g * g))


def _cspec(T, off=0, ct=CT):
    return pl.BlockSpec((1, T, ct), lambda j, b: (b, 0, j + off))


def _pspec(rows, off=0, ct=CT):
    return pl.BlockSpec((rows, ct), lambda j, b: (0, j + off))


def _conv_fwd_call(x3, x_off, C, w, b, name):
    Bl, T, _ = x3.shape
    W = w.shape[0]

    def body(x_ref, w_ref, b_ref, o_ref):
        o_ref[0] = _conv_wrap(x_ref[0], w_ref, b_ref)
        o_ref[0, 0:SLAB, :] = _conv_head(x_ref[0, 0:SLAB, :], w_ref, b_ref)

    return pl.pallas_call(
        body, grid=(C // CT, Bl), in_specs=[_cspec(T, x_off // CT), _pspec(W), _pspec(1)], out_specs=_cspec(T),
        out_shape=S((Bl, T, C), F32), compiler_params=_cparams("parallel", "arbitrary"), name=name)(x3, w, b)


def _conv_bwd_call(dy3, x3, x_off, C, w, name):
    Bl, T, _ = x3.shape
    W = w.shape[0]

    def body(dy_ref, x_ref, w_ref, dx_ref, dw_ref, db_ref):
        dx, dw, db = _conv_bwd_wrap(dy_ref[0], x_ref[0], w_ref)
        dx_tail, dw_extra = _conv_bwd_fix(dy_ref[0, 0:SLAB, :], dy_ref[0, T - SLAB:T, :], x_ref[0, T - SLAB:T, :],
                                          w_ref)
        dx_ref[0] = dx.astype(BF16)
        dx_ref[0, T - SLAB:T, :] = dx_tail.astype(BF16)

        @pl.when(pl.program_id(1) == 0)
        def _():
            dw_ref[...] = jnp.zeros_like(dw_ref)
            db_ref[...] = jnp.zeros_like(db_ref)

        dw_ref[...] += dw - dw_extra
        db_ref[...] += db

    return pl.pallas_call(
        body, grid=(C // CT, Bl), in_specs=[_cspec(T), _cspec(T, x_off // CT), _pspec(W)],
        out_specs=(_cspec(T), _pspec(W), _pspec(1)),
        out_shape=(S((Bl, T, C), BF16), S((W, C), F32), S((1, C), F32)),
        compiler_params=_cparams("parallel", "arbitrary"), name=name)(dy3, x3, w)


def _ffn_mid_fwd(u3, wc, bc, name):
    Bl, T, F2 = u3.shape
    F = F2 // 2
    nf = F // CT

    def body(ug_ref, uv_ref, wg_ref, wv_ref, bg_ref, bv_ref, o_ref):
        g = _conv_rows(ug_ref.at[0], wg_ref, bg_ref, SLAB, T)
        v = _conv_rows(uv_ref.at[0], wv_ref, bv_ref, SLAB, T)
        o_ref[0, SLAB:T, :] = (_gelu(g)[0] * v).astype(BF16)
        g = _conv_head(ug_ref[0, 0:SLAB, :], wg_ref, bg_ref)
        v = _conv_head(uv_ref[0, 0:SLAB, :], wv_ref, bv_ref)
        o_ref[0, 0:SLAB, :] = (_gelu(g)[0] * v).astype(BF16)

    return pl.pallas_call(
        body, grid=(nf, Bl),
        in_specs=[_cspec(T), _cspec(T, nf), _pspec(3), _pspec(3, nf), _pspec(1), _pspec(1, nf)], out_specs=_cspec(T),
        out_shape=S((Bl, T, F), BF16), compiler_params=_cparams("parallel", "arbitrary"), name=name)(
            u3, u3, wc, wc, bc, bc)


def _ffn_mid_bwd(u3, dact3, wc, bc, name):
    Bl, T, F2 = u3.shape
    F = F2 // 2
    nf = F // CT

    def body(ug_ref, uv_ref, da_ref, wg_ref, wv_ref, bg_ref, bv_ref, dug_ref, duv_ref, dwg_ref, dwv_ref, dbg_ref,
             dbv_ref):
        row = lax.broadcasted_iota(jnp.int32, (T, CT), 0)
        ugs = _taps(ug_ref.at[0], 3)
        uvs = _taps(uv_ref.at[0], 3)
        g = _conv_taps(ugs, wg_ref, bg_ref)
        v = _conv_taps(uvs, wv_ref, bv_ref)
        da = da_ref[0]
        gel, t = _gelu(g)
        dg = da * v * _dgelu(g, t)
        dv = da * gel
        dug, dwg, dbg = _conv_bwd_taps(dg, ugs, wg_ref, row)
        duv, dwv, dbv = _conv_bwd_taps(dv, uvs, wv_ref, row)
        dug_ref[0] = dug.astype(BF16)
        duv_ref[0] = duv.astype(BF16)

        @pl.when(pl.program_id(1) == 0)
        def _():
            dwg_ref[...] = jnp.zeros_like(dwg_ref)
            dwv_ref[...] = jnp.zeros_like(dwv_ref)
            dbg_ref[...] = jnp.zeros_like(dbg_ref)
            dbv_ref[...] = jnp.zeros_like(dbv_ref)

        dwg_ref[...] += dwg
        dwv_ref[...] += dwv
        dbg_ref[...] += dbg
        dbv_ref[...] += dbv

    return pl.pallas_call(
        body, grid=(nf, Bl),
        in_specs=[_cspec(T), _cspec(T, nf), _cspec(T), _pspec(3), _pspec(3, nf), _pspec(1), _pspec(1, nf)],
        out_specs=(_cspec(T), _cspec(T), _pspec(3), _pspec(3), _pspec(1), _pspec(1)),
        out_shape=(S((Bl, T, F), BF16), S((Bl, T, F), BF16), S((3, F), F32), S((3, F), F32), S((1, F), F32),
                   S((1, F), F32)),
        compiler_params=_cparams("parallel", "arbitrary"), name=name)(u3, u3, dact3, wc, wc, bc, bc)


def _lru_gates(xc, rp, ip, br_ref, bi_ref, lam_ref):
    r = jax.nn.sigmoid(rp + br_ref[...])
    i = jax.nn.sigmoid(ip + bi_ref[...])
    lam = lam_ref[...]
    sp = jnp.maximum(-lam, 0.0) + jnp.log1p(jnp.exp(-jnp.abs(lam)))
    log_a = (-LRU_C) * r * sp
    a = jnp.exp(log_a)
    z = 2.0 * log_a
    one_m_a2 = jnp.where(z > -0.05, -z * (1.0 + z * (0.5 + z * (1.0 / 6.0 + z * (1.0 / 24.0)))), 1.0 - a * a)
    mult = jnp.sqrt(one_m_a2)
    return r, i, sp, a, mult


def _rglru_fwd(xc3, gates3, proj3, br, bi, lam, name):
    Bl, T, C = xc3.shape
    nsteps = int(math.log2(T))
    assert 1 << nsteps == T

    def body(xc_ref, rp_ref, ip_ref, ug_ref, br_ref, bi_ref, lam_ref, y_ref, h_ref):
        row = lax.broadcasted_iota(jnp.int32, (T, CT), 0)
        xc = xc_ref[0]
        r, i, sp, a, mult = _lru_gates(xc, rp_ref[0], ip_ref[0], br_ref, bi_ref, lam_ref)
        b = mult * (i * xc)
        for st in range(nsteps):
            s = 1 << st
            a_sh = jnp.where(row >= s, pltpu.roll(a, s, axis=0), 1.0)
            b = a * _shift_down(b, s, row) + b
            a = a * a_sh
        h_ref[0] = b
        y_ref[0] = (b * _gelu(ug_ref[0])[0]).astype(BF16)

    return pl.pallas_call(
        body, grid=(C // CT, Bl),
        in_specs=[_cspec(T), _cspec(T), _cspec(T, C // CT), _cspec(T), _pspec(1), _pspec(1), _pspec(1)],
        out_specs=(_cspec(T), _cspec(T)), out_shape=(S((Bl, T, C), BF16), S((Bl, T, C), F32)),
        compiler_params=_cparams("parallel", "arbitrary"), name=name)(xc3, gates3, gates3, proj3, br, bi, lam)


def _rglru_bwd(dy3, xc3, gates3, proj3, h3, br, bi, lam, name):
    Bl, T, C = xc3.shape
    nsteps = int(math.log2(T))

    def body(dy_ref, xc_ref, rp_ref, ip_ref, ug_ref, h_ref, br_ref, bi_ref, lam_ref,
             dxc_ref, drp_ref, dip_ref, dug_ref, dbr_ref, dbi_ref, dlam_ref):
        row = lax.broadcasted_iota(jnp.int32, (T, CT), 0)
        xc = xc_ref[0]
        r, i, sp, a, mult = _lru_gates(xc, rp_ref[0], ip_ref[0], br_ref, bi_ref, lam_ref)
        h = h_ref[0]
        dy = dy_ref[0]
        ug = ug_ref[0]
        gel, t = _gelu(ug)
        dug_ref[0] = (dy * h * _dgelu(ug, t)).astype(BF16)
        gacc = dy * gel
        an = _shift_up(a, 1, row)
        for st in range(nsteps):
            s = 1 << st
            an_sh = jnp.where(row < T - s, pltpu.roll(an, T - s, axis=0), 1.0)
            gacc = an * _shift_up(gacc, s, row) + gacc
            an = an * an_sh
        da = gacc * _shift_down(h, 1, row)
        ix = i * xc
        d_mult = gacc * ix
        d_i = gacc * mult * xc
        dxc_ref[0] = gacc * mult * i
        d_log_a = da * a - d_mult * (a * a) / mult
        d_r = d_log_a * ((-LRU_C) * sp)
        d_sp = jnp.sum(d_log_a * ((-LRU_C) * r), axis=0, keepdims=True)
        drp = d_r * r * (1.0 - r)
        dip = d_i * i * (1.0 - i)
        drp_ref[0] = drp.astype(BF16)
        dip_ref[0] = dip.astype(BF16)

        @pl.when(pl.program_id(1) == 0)
        def _():
            dbr_ref[...] = jnp.zeros_like(dbr_ref)
            dbi_ref[...] = jnp.zeros_like(dbi_ref)
            dlam_ref[...] = jnp.zeros_like(dlam_ref)

        dbr_ref[...] += jnp.sum(drp, axis=0, keepdims=True)
        dbi_ref[...] += jnp.sum(dip, axis=0, keepdims=True)
        dlam_ref[...] += d_sp * (-jax.nn.sigmoid(-lam_ref[...]))

    vec = S((1, C), F32)
    act = S((Bl, T, C), BF16)
    return pl.pallas_call(
        body, grid=(C // CT, Bl),
        in_specs=[_cspec(T), _cspec(T), _cspec(T), _cspec(T, C // CT), _cspec(T), _cspec(T)] + [_pspec(1)] * 3,
        out_specs=(_cspec(T), _cspec(T), _cspec(T), _cspec(T), _pspec(1), _pspec(1), _pspec(1)),
        out_shape=(S((Bl, T, C), F32), act, act, act, vec, vec, vec),
        compiler_params=_cparams("parallel", "arbitrary"), name=name)(dy3, xc3, gates3, gates3, proj3, h3, br, bi, lam)


NT = (((1,), (1,)), ((), ()))
TN = (((0,), (0,)), ((), ()))


def _hs(h):
    return slice(h * HEAD, (h + 1) * HEAD)


def _head_rows(x):
    head = lax.shift_right_logical(lax.broadcasted_iota(jnp.int32, x.shape, 1), HEAD.bit_length() - 1)
    return jnp.concatenate([jnp.where(head == h, x, jnp.zeros_like(x)) for h in range(MEM_HEADS)], axis=0)


def _head_sum(xbd):
    M = xbd.shape[0] // MEM_HEADS
    head = lax.shift_right_logical(lax.broadcasted_iota(jnp.int32, (M, xbd.shape[1]), 1), HEAD.bit_length() - 1)
    out = jnp.zeros((M, xbd.shape[1]), xbd.dtype)
    for h in range(MEM_HEADS):
        out = jnp.where(head == h, xbd[h * M:(h + 1) * M], out)
    return out


def _mem_probs(q, kbd):
    M = kbd.shape[0] // MEM_HEADS
    s = lax.dot_general(q, kbd, NT, preferred_element_type=F32) * (HEAD ** -0.5)
    ps = []
    for h in range(MEM_HEADS):
        sh = s[:, h * M:(h + 1) * M]
        e = jnp.exp(sh - jnp.max(sh, axis=-1, keepdims=True))
        ps.append(e / jnp.sum(e, axis=-1, keepdims=True))
    return ps


def _mem_attn_fwd(proj3, q_off, mkv3, name):
    Bl, T, _ = proj3.shape
    M = mkv3.shape[1]
    tq = _tile(T, (512, 256, 128))

    def body(q_ref, k_ref, v_ref, o_ref):
        q = q_ref[0].astype(BF16)
        kbd = _head_rows(k_ref[0].astype(BF16))
        vbd = _head_rows(v_ref[0].astype(BF16))
        p = jnp.concatenate(_mem_probs(q, kbd), axis=-1).astype(BF16)
        o_ref[0] = jnp.dot(p, vbd, preferred_element_type=F32).astype(BF16)

    return pl.pallas_call(
        body, grid=(Bl, T // tq),
        in_specs=[pl.BlockSpec((1, tq, MEM_W), lambda b, t: (b, t, q_off // MEM_W)),
                  pl.BlockSpec((1, M, MEM_W), lambda b, t: (b, 0, 0)),
                  pl.BlockSpec((1, M, MEM_W), lambda b, t: (b, 0, 1))],
        out_specs=pl.BlockSpec((1, tq, MEM_W), lambda b, t: (b, t, 0)),
        out_shape=S((Bl, T, MEM_W), BF16), compiler_params=_cparams("parallel", "parallel"), name=name)(
            proj3, mkv3, mkv3)


def _mem_attn_bwd(proj3, q_off, mkv3, do3, name):
    Bl, T, _ = proj3.shape
    M = mkv3.shape[1]
    tq = _tile(T, (512, 256, 128))
    scale = HEAD ** -0.5

    def body(q_ref, k_ref, v_ref, do_ref, dq_ref, dkv_ref):
        q = q_ref[0].astype(BF16)
        kbd = _head_rows(k_ref[0].astype(BF16))
        vbd = _head_rows(v_ref[0].astype(BF16))
        do = do_ref[0].astype(BF16)
        ps = _mem_probs(q, kbd)
        dvbd = lax.dot_general(jnp.concatenate(ps, axis=-1).astype(BF16), do, TN, preferred_element_type=F32)
        dp = lax.dot_general(do, vbd, NT, preferred_element_type=F32)
        dss = []
        for h in range(MEM_HEADS):
            dph = dp[:, h * M:(h + 1) * M]
            dss.append(ps[h] * (dph - jnp.sum(ps[h] * dph, axis=-1, keepdims=True)) * scale)
        ds = jnp.concatenate(dss, axis=-1).astype(BF16)
        dq_ref[0] = jnp.dot(ds, kbd, preferred_element_type=F32).astype(BF16)
        dkbd = lax.dot_general(ds, q, TN, preferred_element_type=F32)

        @pl.when(pl.program_id(1) == 0)
        def _():
            dkv_ref[...] = jnp.zeros_like(dkv_ref)

        dkv_ref[0] += jnp.concatenate([_head_sum(dkbd), _head_sum(dvbd)], axis=-1)

    return pl.pallas_call(
        body, grid=(Bl, T // tq),
        in_specs=[pl.BlockSpec((1, tq, MEM_W), lambda b, t: (b, t, q_off // MEM_W)),
                  pl.BlockSpec((1, M, MEM_W), lambda b, t: (b, 0, 0)),
                  pl.BlockSpec((1, M, MEM_W), lambda b, t: (b, 0, 1)),
                  pl.BlockSpec((1, tq, MEM_W), lambda b, t: (b, t, 0))],
        out_specs=(pl.BlockSpec((1, tq, MEM_W), lambda b, t: (b, t, 0)),
                   pl.BlockSpec((1, M, 2 * MEM_W), lambda b, t: (b, 0, 0))),
        out_shape=(S((Bl, T, MEM_W), BF16), S((Bl, M, 2 * MEM_W), F32)),
        compiler_params=_cparams("parallel", "arbitrary"), name=name)(proj3, mkv3, mkv3, do3)


GROUP_ROWS = SWA_GROUP * WIN


def _group_rows(x, kvh):
    return jnp.concatenate([x[:, _hs(SWA_GROUP * kvh + g)] for g in range(SWA_GROUP)], axis=0)


def _group_col(vals):
    grp = lax.shift_right_logical(lax.broadcasted_iota(jnp.int32, (GROUP_ROWS, 1), 0), WIN.bit_length() - 1)
    col = jnp.full((GROUP_ROWS, 1), vals[-1], F32)
    for g in range(SWA_GROUP - 2, -1, -1):
        col = jnp.where(grp == g, vals[g], col)
    return col


def _swa_probs(qh, kph, kch, sink, slope, has_prev):
    qi = jnp.bitwise_and(lax.broadcasted_iota(jnp.int32, (GROUP_ROWS, WIN), 0), WIN - 1)
    kj = lax.broadcasted_iota(jnp.int32, (GROUP_ROWS, WIN), 1)
    scale = HEAD ** -0.5
    sp = lax.dot_general(qh, kph, NT, preferred_element_type=F32) * scale
    sc = lax.dot_general(qh, kch, NT, preferred_element_type=F32) * scale
    dist_p = (qi + WIN - kj).astype(F32)
    dist_c = (qi - kj).astype(F32)
    neg = -jnp.inf
    sp = jnp.where(kj > qi + jnp.where(has_prev, 0, WIN), sp - slope * dist_p, neg)
    sc = jnp.where(kj <= qi, sc - slope * dist_c, neg)
    m = jnp.maximum(jnp.maximum(jnp.max(sp, axis=-1, keepdims=True), jnp.max(sc, axis=-1, keepdims=True)), sink)
    ep = jnp.exp(sp - m)
    ec = jnp.exp(sc - m)
    es = jnp.exp(sink - m)
    inv = 1.0 / (jnp.sum(ep, axis=-1, keepdims=True) + jnp.sum(ec, axis=-1, keepdims=True) + es)
    return ep * inv, ec * inv, es * inv


def _swa_specs(nb):
    prev = lambda n: jnp.maximum(n - 1, 0)
    q = pl.BlockSpec((1, WIN, MIX_W), lambda b, n: (b, n, 0))
    kp = pl.BlockSpec((1, WIN, MEM_W), lambda b, n: (b, prev(n), 0))
    kc = pl.BlockSpec((1, WIN, MEM_W), lambda b, n: (b, n, 0))
    vp = pl.BlockSpec((1, WIN, MEM_W), lambda b, n: (b, prev(n), 1))
    vc = pl.BlockSpec((1, WIN, MEM_W), lambda b, n: (b, n, 1))
    sm = pl.BlockSpec(memory_space=pltpu.SMEM)
    return q, kp, kc, vp, vc, sm


def _swa_fwd(proj3, kv3, sinks, name):
    Bl, T, _ = proj3.shape
    nb = T // WIN
    q_s, kp_s, kc_s, vp_s, vc_s, sm = _swa_specs(nb)

    def body(q_ref, kp_ref, kc_ref, vp_ref, vc_ref, sink_ref, o_ref):
        has_prev = pl.program_id(1) > 0
        q = q_ref[0].astype(BF16)
        kp, kc = kp_ref[0].astype(BF16), kc_ref[0].astype(BF16)
        vp, vc = vp_ref[0].astype(BF16), vc_ref[0].astype(BF16)
        outs = []
        for kvh in range(SWA_HEADS // SWA_GROUP):
            kvs = _hs(kvh)
            heads = range(SWA_GROUP * kvh, SWA_GROUP * (kvh + 1))
            pp, pc, _ = _swa_probs(_group_rows(q, kvh), kp[:, kvs], kc[:, kvs], _group_col([sink_ref[h] for h in heads]),
                                   _group_col([SLOPES[h] for h in heads]), has_prev)
            og = (jnp.dot(pp.astype(BF16), vp[:, kvs], preferred_element_type=F32)
                  + jnp.dot(pc.astype(BF16), vc[:, kvs], preferred_element_type=F32))
            outs += [og[g * WIN:(g + 1) * WIN] for g in range(SWA_GROUP)]
        o_ref[0] = jnp.concatenate(outs, axis=-1).astype(BF16)

    return pl.pallas_call(
        body, grid=(Bl, nb), in_specs=[q_s, kp_s, kc_s, vp_s, vc_s, sm], out_specs=q_s,
        out_shape=S((Bl, T, MIX_W), BF16), compiler_params=_cparams("parallel", "parallel"), name=name)(
            proj3, kv3, kv3, kv3, kv3, sinks)


def _swa_bwd(proj3, kv3, sinks, do3, name):
    Bl, T, _ = proj3.shape
    nb = T // WIN
    q_s, kp_s, kc_s, vp_s, vc_s, sm = _swa_specs(nb)
    kv_s = pl.BlockSpec((1, WIN, 2 * MEM_W), lambda b, n: (b, n, 0))
    sk_s = pl.BlockSpec((8, LANES), lambda b, n: (0, 0))
    scale = HEAD ** -0.5

    def body(q_ref, kp_ref, kc_ref, vp_ref, vc_ref, sink_ref, do_ref, dq_ref, dkc_ref, dkp_ref, dsk_ref):
        has_prev = pl.program_id(1) > 0
        q = q_ref[0].astype(BF16)
        kp, kc = kp_ref[0].astype(BF16), kc_ref[0].astype(BF16)
        vp, vc = vp_ref[0].astype(BF16), vc_ref[0].astype(BF16)
        do = do_ref[0].astype(BF16)
        lane = lax.broadcasted_iota(jnp.int32, (8, LANES), 1)
        srow = lax.broadcasted_iota(jnp.int32, (8, LANES), 0)
        dsk = jnp.zeros((8, LANES), F32)
        dqs = []
        dkc, dkp, dvc, dvp = [], [], [], []
        grp = lax.shift_right_logical(lax.broadcasted_iota(jnp.int32, (GROUP_ROWS, 1), 0), WIN.bit_length() - 1)
        for kvh in range(SWA_HEADS // SWA_GROUP):
            kvs = _hs(kvh)
            heads = range(SWA_GROUP * kvh, SWA_GROUP * (kvh + 1))
            qg, dog = _group_rows(q, kvh), _group_rows(do, kvh)
            pp, pc, ps = _swa_probs(qg, kp[:, kvs], kc[:, kvs], _group_col([sink_ref[h] for h in heads]),
                                    _group_col([SLOPES[h] for h in heads]), has_prev)
            dpp = lax.dot_general(dog, vp[:, kvs], NT, preferred_element_type=F32)
            dpc = lax.dot_general(dog, vc[:, kvs], NT, preferred_element_type=F32)
            delta = jnp.sum(pp * dpp, axis=-1, keepdims=True) + jnp.sum(pc * dpc, axis=-1, keepdims=True)
            dsp = (pp * (dpp - delta) * scale).astype(BF16)
            dsc = (pc * (dpc - delta) * scale).astype(BF16)
            dqg = (jnp.dot(dsp, kp[:, kvs], preferred_element_type=F32)
                   + jnp.dot(dsc, kc[:, kvs], preferred_element_type=F32))
            dqs += [dqg[g * WIN:(g + 1) * WIN] for g in range(SWA_GROUP)]
            dkc.append(lax.dot_general(dsc, qg, TN, preferred_element_type=F32))
            dkp.append(lax.dot_general(dsp, qg, TN, preferred_element_type=F32))
            dvc.append(lax.dot_general(pc.astype(BF16), dog, TN, preferred_element_type=F32))
            dvp.append(lax.dot_general(pp.astype(BF16), dog, TN, preferred_element_type=F32))
            dsink = ps * delta
            for g, h in enumerate(heads):
                dsk = dsk + jnp.where((lane == h) & (srow == 0), -jnp.sum(jnp.where(grp == g, dsink, 0.0)), 0.0)
        dq_ref[0] = jnp.concatenate(dqs, axis=-1).astype(BF16)
        dkc_ref[0] = jnp.concatenate(dkc + dvc, axis=-1)
        dkp_ref[0] = jnp.concatenate(dkp + dvp, axis=-1)

        @pl.when((pl.program_id(0) == 0) & (pl.program_id(1) == 0))
        def _():
            dsk_ref[...] = jnp.zeros_like(dsk_ref)

        dsk_ref[...] += dsk

    return pl.pallas_call(
        body, grid=(Bl, nb), in_specs=[q_s, kp_s, kc_s, vp_s, vc_s, sm, q_s], out_specs=(q_s, kv_s, kv_s, sk_s),
        out_shape=(S((Bl, T, MIX_W), BF16), S((Bl, T, 2 * MEM_W), F32), S((Bl, T, 2 * MEM_W), F32), S((8, LANES), F32)),
        compiler_params=_cparams("arbitrary", "arbitrary"), name=name)(proj3, kv3, kv3, kv3, kv3, sinks, do3)


def _kv_grad_combine(parts, name):
    Bl, T, W = parts[0][0].shape
    nb = T // WIN
    nl = len(parts)

    def body(*refs):
        o_ref = refs[-1]
        has_next = jnp.where(pl.program_id(1) == nb - 1, 0.0, 1.0)
        acc = None
        for l in range(nl):
            c = refs[2 * l][0] + has_next * refs[2 * l + 1][0]
            acc = c if acc is None else acc + c
        o_ref[0] = acc.astype(BF16)

    cur = pl.BlockSpec((1, WIN, W), lambda b, n: (b, n, 0))
    nxt = pl.BlockSpec((1, WIN, W), lambda b, n: (b, jnp.minimum(n + 1, nb - 1), 0))
    return pl.pallas_call(
        body, grid=(Bl, nb), in_specs=[cur, nxt] * nl, out_specs=cur, out_shape=S((Bl, T, W), BF16),
        compiler_params=_cparams("parallel", "parallel"), name=name)(*[a for pr in parts for a in pr])


def _loss_bwd(y, target, name="loss"):
    N, D = y.shape
    tm = _tile(N, (512, 256, 128))

    def body(y_ref, t_ref, dy_ref, l_ref):
        e = y_ref[...] - t_ref[...]
        dy_ref[...] = e * (1.0 / D)

        @pl.when(pl.program_id(0) == 0)
        def _():
            l_ref[...] = jnp.zeros_like(l_ref)

        l_ref[...] += jnp.sum(e * e, axis=0, keepdims=True) * (0.5 / D)

    row = pl.BlockSpec((tm, D), lambda i: (i, 0))
    vec = pl.BlockSpec((1, D), lambda i: (0, 0))
    return pl.pallas_call(
        body, grid=(N // tm,), in_specs=[row, row], out_specs=(row, vec), out_shape=(S((N, D), F32), S((1, D), F32)),
        compiler_params=_cparams("arbitrary"), name=name)(y, target)


def _all_gather(x, name):
    R, C = x.shape

    def body(x_ref, out_ref, send_sems, recv_sems, local_sem):
        mx, my, mc = lax.axis_index("x"), lax.axis_index("y"), lax.axis_index("c")
        me, sibling = (mx, my, mc), (mx, my, 1 - mc)
        chips = [(1 - mx, my), (mx, 1 - my), (1 - mx, 1 - my)]

        def rows(px, py, pc):
            return out_ref.at[4 * px + 2 * py + pc]

        def copy(kk, block, to, src=None):
            return pltpu.make_async_remote_copy(
                src_ref=rows(*block) if src is None else src, dst_ref=rows(*block), send_sem=send_sems.at[kk],
                recv_sem=recv_sems.at[kk], device_id=to, device_id_type=MESH)

        mine = pltpu.make_async_copy(x_ref, rows(*me), local_sem)
        mine.start()
        first = [copy(0, me, sibling, src=x_ref)]
        first += [copy(1 + j, me, (*chip, mc), src=x_ref) for j, chip in enumerate(chips)]
        for cp in first:
            cp.start()
        passed = [copy(4 + j, (*chip, mc), sibling) for j, chip in enumerate(chips)]
        for j, chip in enumerate(chips):
            copy(1 + j, (*chip, mc), me).wait_recv()
            passed[j].start()
        copy(0, sibling, me).wait_recv()
        for j, chip in enumerate(chips):
            copy(4 + j, (*chip, 1 - mc), me).wait_recv()
        for cp in first + passed:
            cp.wait_send()
        mine.wait()

    return pl.pallas_call(
        body, out_shape=S((N_DEV, R, C), x.dtype), in_specs=[ANY], out_specs=ANY,
        scratch_shapes=[pltpu.SemaphoreType.DMA((7,)), pltpu.SemaphoreType.DMA((7,)), pltpu.SemaphoreType.DMA(())],
        name=name)(x)


def _ag_weights(shards, row_sharded, name):
    n = len(shards)

    def full_shape(a, rows):
        if rows:
            return a.shape[:-2] + (N_DEV * a.shape[-2],) + a.shape[-1:]
        return (N_DEV,) + a.shape

    def body(*refs):
        x_refs, o_refs = refs[:n], refs[n:2 * n]
        send_sems, recv_sems, local_sems = refs[2 * n:]
        mx, my, mc = lax.axis_index("x"), lax.axis_index("y"), lax.axis_index("c")
        me, sibling = (mx, my, mc), (mx, my, 1 - mc)
        chips = [(1 - mx, my), (mx, 1 - my), (1 - mx, 1 - my)]

        def dst(t, px, py, pc):
            d = 4 * px + 2 * py + pc
            if not row_sharded[t]:
                return o_refs[t].at[d]
            r = shards[t].shape[-2]
            idx = (slice(None),) * (shards[t].ndim - 2) + (pl.ds(pl.multiple_of(d * r, 16), r), slice(None))
            return o_refs[t].at[idx]

        def copy(kk, t, block, to, src=None):
            return pltpu.make_async_remote_copy(
                src_ref=dst(t, *block) if src is None else src, dst_ref=dst(t, *block),
                send_sem=send_sems.at[kk * n + t], recv_sem=recv_sems.at[kk * n + t], device_id=to,
                device_id_type=MESH)

        mine = [pltpu.make_async_copy(x_refs[t], dst(t, *me), local_sems.at[t]) for t in range(n)]
        for cp in mine:
            cp.start()
        first = []
        for t in range(n):
            first.append(copy(0, t, me, sibling, src=x_refs[t]))
            first += [copy(1 + j, t, me, (*chip, mc), src=x_refs[t]) for j, chip in enumerate(chips)]
        for cp in first:
            cp.start()
        passed = []
        for j, chip in enumerate(chips):
            for t in range(n):
                copy(1 + j, t, (*chip, mc), me).wait_recv()
                cp = copy(4 + j, t, (*chip, mc), sibling)
                cp.start()
                passed.append(cp)
        for t in range(n):
            copy(0, t, sibling, me).wait_recv()
            for j, chip in enumerate(chips):
                copy(4 + j, t, (*chip, 1 - mc), me).wait_recv()
        for cp in first + passed:
            cp.wait_send()
        for cp in mine:
            cp.wait()

    return pl.pallas_call(
        body, out_shape=tuple(S(full_shape(a, r), a.dtype) for a, r in zip(shards, row_sharded)),
        in_specs=[ANY] * n, out_specs=tuple([ANY] * n),
        scratch_shapes=[pltpu.SemaphoreType.DMA((7 * n,)), pltpu.SemaphoreType.DMA((7 * n,)),
                        pltpu.SemaphoreType.DMA((n,))],
        name=name)(*shards)


FLIPS = [(fx, fy, fc) for fx in (0, 1) for fy in (0, 1) for fc in (0, 1)][1:]
HBM = pl.BlockSpec(memory_space=pltpu.HBM)
SEM = pl.BlockSpec(memory_space=pltpu.SEMAPHORE)
EFFECT = pltpu.SideEffectType.DATAFLOW_SIDE_EFFECTING


def _hbm(a):
    return pltpu.with_memory_space_constraint(a, pltpu.HBM)


def _flips(gather):
    return [(0, 0, 0)] + FLIPS if gather else FLIPS


def _split_copies(gather, s_refs, l_refs, send_sems, recv_sems):
    n = len(s_refs)
    mx, my, mc = lax.axis_index("x"), lax.axis_index("y"), lax.axis_index("c")
    me = 4 * mx + 2 * my + mc
    copies = []
    for k, (fx, fy, fc) in enumerate(_flips(gather)):
        px, py, pc = (1 - mx if fx else mx), (1 - my if fy else my), (1 - mc if fc else mc)
        for t in range(n):
            if gather:
                src = s_refs[t]
                r = src.shape[0]
                dst = l_refs[t].at[pl.ds(pl.multiple_of(me * r, 16), r), :]
            else:
                src = s_refs[t].at[:, 4 * px + 2 * py + pc]
                dst = l_refs[t].at[k]
            copies.append(pltpu.make_async_remote_copy(
                src_ref=src, dst_ref=dst, send_sem=send_sems.at[k * n + t], recv_sem=recv_sems.at[k * n + t],
                device_id=(px, py, pc), device_id_type=MESH))
    return copies


def _split_start(gather, srcs, lands, after, name):
    n = len(srcs)
    n_sem = len(_flips(gather)) * n

    def body(*refs):
        s_refs, l_refs = refs[:n], refs[n:2 * n]
        send_sems, recv_sems = refs[2 * n + 1], refs[2 * n + 2]
        token = refs[-1]
        for cp in _split_copies(gather, s_refs, l_refs, send_sems, recv_sems):
            cp.start()
        token[...] = jnp.zeros_like(token)

    outs = pl.pallas_call(
        body, name=name,
        out_shape=(pltpu.SemaphoreType.DMA((n_sem,)), pltpu.SemaphoreType.DMA((n_sem,)))
        + tuple(pltpu.HBM(a.shape, a.dtype) for a in lands) + (S((8, LANES), F32),),
        in_specs=[HBM] * (2 * n) + [ANY],
        out_specs=(SEM, SEM) + (HBM,) * n + (pl.BlockSpec(memory_space=pltpu.VMEM),),
        input_output_aliases={n + i: 2 + i for i in range(n)},
        compiler_params=pltpu.CompilerParams(has_side_effects=EFFECT),
    )(*[_hbm(a) for a in srcs], *[_hbm(a) for a in lands], after)
    return outs[0], outs[1], list(srcs), list(outs[2:2 + n]), outs[-1]


def _split_wait(gather, send_sems, recv_sems, srcs, lands, after, name):
    n = len(srcs)

    def body(*refs):
        s_refs, l_refs = refs[:n], refs[n:2 * n]
        ssem, rsem = refs[2 * n], refs[2 * n + 1]
        copies = _split_copies(gather, s_refs, l_refs, ssem, rsem)
        for cp in copies:
            cp.wait_send()
        for cp in copies:
            cp.wait_recv()

    outs = pl.pallas_call(
        body, name=name, out_shape=tuple(pltpu.HBM(a.shape, a.dtype) for a in lands),
        in_specs=[HBM] * (2 * n) + [SEM, SEM, ANY], out_specs=(HBM,) * n,
        input_output_aliases={n + i: i for i in range(n)},
        compiler_params=pltpu.CompilerParams(has_side_effects=EFFECT),
    )(*[_hbm(a) for a in srcs], *lands, send_sems, recv_sems, after)
    return list(outs)


def _adamw_math(w, g, m, v):
    m = ADAM_B1 * m + (1.0 - ADAM_B1) * g
    v = ADAM_B2 * v + (1.0 - ADAM_B2) * (g * g)
    m_hat = m / (1.0 - ADAM_B1 ** ADAM_STEP)
    v_hat = v / (1.0 - ADAM_B2 ** ADAM_STEP)
    delta = -ADAM_LR * (m_hat / (jnp.sqrt(v_hat) + ADAM_EPS) + ADAM_WD * w)
    return delta, m, v


def _adamw_layers(owns, gots, w, m, v, name):
    L, B, C = w.shape
    per_row = 2 * L * len(FLIPS) * C * owns[0].dtype.itemsize
    tb = max([t for t in range(16, B + 1, 16) if B % t == 0 and (t * per_row <= 16 * 1024 * 1024 or t == 16)] or [B])
    me = (4 * lax.axis_index("x") + 2 * lax.axis_index("y") + lax.axis_index("c")).astype(jnp.int32).reshape(1)

    def body(me_ref, *refs):
        own_refs, got_refs = refs[:L], refs[L:2 * L]
        w_ref, m_ref, v_ref = refs[2 * L:2 * L + 3]
        g_out, d_out, m_out, v_out = refs[2 * L + 3:]
        layer = pl.program_id(0)
        for kk in range(L):
            @pl.when(layer == kk)
            def _():
                g = own_refs[kk][0].astype(F32)
                for s in range(len(FLIPS)):
                    g = g + got_refs[kk][s].astype(F32)
                d, mn, vn = _adamw_math(w_ref[...], g, m_ref[...], v_ref[...])
                g_out[...] = g
                d_out[...] = d
                m_out[...] = mn
                v_out[...] = vn

    def row(kk, layer, i):
        return jnp.where(layer == kk, i, 0)

    blk = pl.BlockSpec((1, tb, C), lambda layer, i, me_ref: (layer, i, 0))
    own_specs = [pl.BlockSpec((1, 1, tb, C), lambda layer, i, me_ref, kk=kk: (0, me_ref[0], row(kk, layer, i), 0))
                 for kk in range(L)]
    got_specs = [pl.BlockSpec((len(FLIPS), 1, tb, C), lambda layer, i, me_ref, kk=kk: (0, 0, row(kk, layer, i), 0))
                 for kk in range(L)]
    return pl.pallas_call(
        body,
        grid_spec=pltpu.PrefetchScalarGridSpec(
            num_scalar_prefetch=1, grid=(L, B // tb), in_specs=own_specs + got_specs + [blk, blk, blk],
            out_specs=(blk, blk, blk, blk)),
        out_shape=(S((L, B, C), F32),) * 4, compiler_params=_cparams("arbitrary", "arbitrary"), name=name)(
            me, *owns, *gots, w, m, v)


def _adamw_replicated(parts, w, m, v, name):
    R, C = w.shape
    rb = _tile(R, (512, 256, 128, 64, 32, 16, 8))

    def body(p_ref, w_ref, m_ref, v_ref, g_out, d_out, m_out, v_out):
        g = p_ref[0]
        for j in range(1, N_DEV):
            g = g + p_ref[j]
        d, mn, vn = _adamw_math(w_ref[...], g, m_ref[...], v_ref[...])
        g_out[...] = g
        d_out[...] = d
        m_out[...] = mn
        v_out[...] = vn

    blk = pl.BlockSpec((rb, C), lambda i: (i, 0))
    return pl.pallas_call(
        body, grid=(R // rb,), in_specs=[pl.BlockSpec((N_DEV, rb, C), lambda i: (0, i, 0)), blk, blk, blk],
        out_specs=(blk, blk, blk, blk), out_shape=(S((R, C), F32),) * 4, compiler_params=_cparams("parallel"),
        name=name)(parts, w, m, v)


def _pack(arrs, rows_mult, dtype):
    flat = jnp.concatenate([a.reshape(-1).astype(dtype) for a in arrs])
    n = flat.shape[0]
    per = rows_mult * LANES
    tot = -(-n // per) * per
    return jnp.pad(flat, (0, tot - n)).reshape(tot // LANES, LANES)


def _unpack(blob, shapes):
    flat = blob.reshape(-1)
    out, off = [], 0
    for shp in shapes:
        n = int(np.prod(shp))
        out.append(flat[off:off + n].reshape(shp))
        off += n
    return out


def _small_to_natural(g8):
    t = jnp.moveaxis(g8, 0, -2)
    return t.reshape(t.shape[:-2] + (N_DEV * t.shape[-1],))


def _small_to_cols(g):
    t = g.reshape(g.shape[:-1] + (N_DEV, g.shape[-1] // N_DEV))
    return jnp.moveaxis(t, -2, 0)


def _block_diag(w):
    nb, bs, _ = w.shape
    eye = jnp.eye(nb, dtype=w.dtype)
    return (eye[:, None, :, None] * w[:, :, None, :]).reshape(nb * bs, nb * bs)


def _diag_blocks(d, nb, bs):
    d4 = d.reshape(nb, bs, nb, bs)
    return jnp.stack([d4[i, :, i, :] for i in range(nb)])


def kernel(x, mem, g_mix_pre, g_mix_post, g_ffn_pre, g_ffn_post, g_mem, w_mem_kv, w_mix_out, w_ffn_up, w_ffn_conv, b_ffn_conv, w_ffn_down, w_in_a, w_conv_a, b_conv_a, w_rg_r, b_rg_r, w_rg_i, b_rg_i, lru_lambda, w_in_b, sinks_b, g_kv, w_kv, loss_target, m_g_mix_pre, m_g_mix_post, m_g_ffn_pre, m_g_ffn_post, m_g_mem, m_w_mem_kv, m_w_mix_out, m_w_ffn_up, m_w_ffn_conv, m_b_ffn_conv, m_w_ffn_down, m_w_in_a, m_w_conv_a, m_b_conv_a, m_w_rg_r, m_b_rg_r, m_w_rg_i, m_b_rg_i, m_lru_lambda, m_w_in_b, m_sinks_b, m_g_kv, m_w_kv, v_g_mix_pre, v_g_mix_post, v_g_ffn_pre, v_g_ffn_post, v_g_mem, v_w_mem_kv, v_w_mix_out, v_w_ffn_up, v_w_ffn_conv, v_b_ffn_conv, v_w_ffn_down, v_w_in_a, v_w_conv_a, v_b_conv_a, v_w_rg_r, v_b_rg_r, v_w_rg_i, v_b_rg_i, v_lru_lambda, v_w_in_b, v_sinks_b, v_g_kv, v_w_kv):
    w_loc = dict(g_mix_pre=g_mix_pre, g_mix_post=g_mix_post, g_ffn_pre=g_ffn_pre, g_ffn_post=g_ffn_post, g_mem=g_mem,
                 w_mem_kv=w_mem_kv, w_mix_out=w_mix_out, w_ffn_up=w_ffn_up, w_ffn_conv=w_ffn_conv,
                 b_ffn_conv=b_ffn_conv, w_ffn_down=w_ffn_down, w_in_a=w_in_a, w_conv_a=w_conv_a, b_conv_a=b_conv_a,
                 w_rg_r=w_rg_r, b_rg_r=b_rg_r, w_rg_i=w_rg_i, b_rg_i=b_rg_i, lru_lambda=lru_lambda, w_in_b=w_in_b,
                 sinks_b=sinks_b, g_kv=g_kv, w_kv=w_kv)
    m_loc = dict(g_mix_pre=m_g_mix_pre, g_mix_post=m_g_mix_post, g_ffn_pre=m_g_ffn_pre, g_ffn_post=m_g_ffn_post,
                 g_mem=m_g_mem, w_mem_kv=m_w_mem_kv, w_mix_out=m_w_mix_out, w_ffn_up=m_w_ffn_up,
                 w_ffn_conv=m_w_ffn_conv, b_ffn_conv=m_b_ffn_conv, w_ffn_down=m_w_ffn_down, w_in_a=m_w_in_a,
                 w_conv_a=m_w_conv_a, b_conv_a=m_b_conv_a, w_rg_r=m_w_rg_r, b_rg_r=m_b_rg_r, w_rg_i=m_w_rg_i,
                 b_rg_i=m_b_rg_i, lru_lambda=m_lru_lambda, w_in_b=m_w_in_b, sinks_b=m_sinks_b, g_kv=m_g_kv,
                 w_kv=m_w_kv)
    v_loc = dict(g_mix_pre=v_g_mix_pre, g_mix_post=v_g_mix_post, g_ffn_pre=v_g_ffn_pre, g_ffn_post=v_g_ffn_post,
                 g_mem=v_g_mem, w_mem_kv=v_w_mem_kv, w_mix_out=v_w_mix_out, w_ffn_up=v_w_ffn_up,
                 w_ffn_conv=v_w_ffn_conv, b_ffn_conv=v_b_ffn_conv, w_ffn_down=v_w_ffn_down, w_in_a=v_w_in_a,
                 w_conv_a=v_w_conv_a, b_conv_a=v_b_conv_a, w_rg_r=v_w_rg_r, b_rg_r=v_b_rg_r, w_rg_i=v_w_rg_i,
                 b_rg_i=v_b_rg_i, lru_lambda=v_lru_lambda, w_in_b=v_w_in_b, sinks_b=v_sinks_b, g_kv=v_g_kv,
                 w_kv=v_w_kv)

    Bl, T, D = x.shape
    Ml = mem.shape[1]
    N = Bl * T
    depth = g_mix_pre.shape[0]
    n_a = w_in_a.shape[0]
    F = w_ffn_down.shape[1] * N_DEV
    def as_rows(n, a):
        return jnp.swapaxes(a, -1, -2) if n in TRANSPOSED else a

    def layer_keys(l):
        keys = [("w_mem_kv", l), ("w_mix_out", l), ("w_ffn_up", l), ("w_ffn_down", l)]
        keys.append(("w_in_a", l) if l < n_a else ("w_in_b", l - n_a))
        if l == n_a:
            keys.append(("w_kv", None))
        return keys

    def shard_of(key):
        n, i = key
        return as_rows(n, w_loc[n] if i is None else w_loc[n][i]).astype(BF16)

    W = {}
    ffn_names = ("w_ffn_up", "w_ffn_down")
    keys0 = [kk for kk in layer_keys(0) if kk[0] not in ffn_names]
    got0 = _ag_weights([shard_of(kk) for kk in keys0] + [w_loc[n] for n in SMALL_SHARDED],
                       [True] * len(keys0) + [False] * len(SMALL_SHARDED), name="ag_weights_0")
    W.update(zip(keys0, got0))
    for n, a in zip(SMALL_SHARDED, got0[len(keys0):]):
        W[n] = _small_to_natural(a)

    def gather_start(keys, after, tag):
        shards = [shard_of(kk) for kk in keys]
        lands = [lax.empty((N_DEV * s.shape[0],) + s.shape[1:], s.dtype) for s in shards]
        return (keys, tag) + _split_start(True, shards, lands, after, name=f"ag_start_{tag}")

    def gather_wait(pending, after):
        keys, tag, ssem, rsem, srcs, lands, _ = pending
        W.update(zip(keys, _split_wait(True, ssem, rsem, srcs, lands, after, name=f"ag_wait_{tag}")))

    pending_ffn0 = gather_start([kk for kk in layer_keys(0) if kk[0] in ffn_names], got0[0], "ffn_0")

    nblk, bsz = w_rg_r.shape[1], w_rg_r.shape[2]
    wbd = [jnp.concatenate([_block_diag(w_rg_r[j]), _block_diag(w_rg_i[j])], axis=1).astype(BF16) for j in range(n_a)]

    def vec(a):
        return a.reshape(1, -1)

    x2 = x.reshape(N, D)
    mem2 = mem.reshape(Bl * Ml, D)
    saved = []
    kvn = kv3 = x_kv = None
    xs = x2
    h1 = _rms_fwd(xs, vec(g_mix_pre[0]), BF16, name="rms_mixpre_0")
    for l in range(depth):
        sv = {"x0": xs}
        tok = None
        if l + 1 < depth:
            pending = gather_start(layer_keys(l + 1), pending_ffn0[-1] if l == 0 else W[("w_mem_kv", l)], l + 1)
            tok = pending[-1]
        memn = _rms_fwd(mem2, vec(g_mem[l]), BF16, name=f"rms_mem_{l}")
        mkv3 = _mm(memn, W[("w_mem_kv", l)], after=tok, name=f"mm_memkv_{l}").reshape(Bl, Ml, 2 * MEM_W)
        if l < n_a:
            j = l
            proj = _mm(h1, W[("w_in_a", j)], tb=True, after=tok, name=f"mm_in_{l}")
            proj3 = proj.reshape(Bl, T, -1)
            xc3 = _conv_fwd_call(proj3, MIX_W, MIX_W, W["w_conv_a"][j], vec(W["b_conv_a"][j]), name=f"conv_a_{l}")
            gates3 = _mm(xc3.reshape(N, MIX_W), wbd[j], name=f"mm_gates_{l}").reshape(Bl, T, 2 * MIX_W)
            y_main3, hs3 = _rglru_fwd(xc3, gates3, proj3, vec(b_rg_r[j]), vec(b_rg_i[j]), vec(W["lru_lambda"][j]),
                                      name=f"rglru_fwd_{l}")
            q_off = 2 * MIX_W
            sv.update(xc3=xc3, gates3=gates3, hs3=hs3)
        else:
            j = l - n_a
            if l == n_a:
                x_kv = xs
                kv3 = _mm(kvn, W[("w_kv", None)], name="mm_kv").reshape(Bl, T, 2 * MEM_W)
            proj = _mm(h1, W[("w_in_b", j)], after=tok, name=f"mm_in_{l}")
            proj3 = proj.reshape(Bl, T, -1)
            y_main3 = _swa_fwd(proj3, kv3, sinks_b[j], name=f"swa_fwd_{l}")
            q_off = MIX_W
        y_mem3 = _mem_attn_fwd(proj3, q_off, mkv3, name=f"memattn_fwd_{l}")
        y_main = y_main3.reshape(N, MIX_W)
        y_mem = y_mem3.reshape(N, MEM_W)
        y = _mm_sum([(y_main, W[("w_mix_out", l)], (0, 0)), (y_mem, W[("w_mix_out", l)], (MIX_W, 0))], n=D,
                    name=f"mm_mixout_{l}")
        x1, h2 = _rms_pair_fwd(y, vec(g_mix_post[l]), xs, [vec(g_ffn_pre[l])], name=f"rms_mixpost_ffnpre_{l}")
        if l == 0:
            gather_wait(pending_ffn0, h2)
        u3 = _mm(h2, W[("w_ffn_up", l)], tb=True, name=f"mm_up_{l}").reshape(Bl, T, 2 * F)
        act3 = _ffn_mid_fwd(u3, W["w_ffn_conv"][l], vec(b_ffn_conv[l]), name=f"ffn_mid_fwd_{l}")
        act = act3.reshape(N, F)
        f = _mm(act, W[("w_ffn_down", l)], name=f"mm_down_{l}")
        sv.update(h1=h1, memn=memn, mkv3=mkv3, proj3=proj3, q_off=q_off, y_main=y_main, y_mem=y_mem, y=y, x1=x1,
                  h2=h2, u3=u3, act=act, f=f)
        saved.append(sv)
        if l + 1 < depth:
            gains = [vec(g_mix_pre[l + 1])] + ([vec(g_kv)] if l + 1 == n_a else [])
            xs, h1, *rest = _rms_pair_fwd(f, vec(g_ffn_post[l]), x1, gains, name=f"rms_ffnpost_mixpre_{l}")
            if rest:
                kvn = rest[0]
            gather_wait(pending, xs)
        else:
            xs = _rms_fwd(f, vec(g_ffn_post[l]), F32, res=x1, name=f"rms_ffnpost_{l}")

    dxs, loss_vec = _loss_bwd(xs, loss_target.reshape(N, D))
    loss = lax.psum(jnp.sum(loss_vec), ("x", "y", "c"))

    G = {n: [None] * w_loc[n].shape[0] for n in REPL + SMALL_SHARDED if n != "g_kv"}
    GW = {}

    def dw(key, off, a, b_, nm):
        GW[key] = _mm(a, b_, ta=True, out_dtype=BF16, into=(GW.get(key), (1,) + W[key].shape, 0, off), name=nm)

    def grad_blocks(key):
        g = GW[key]
        return g.reshape(1, N_DEV, g.shape[1] // N_DEV, g.shape[2])

    reduces = []

    def reduce_start(keys, after, tag):
        srcs = [grad_blocks(kk) for kk in keys]
        lands = [lax.empty((len(FLIPS),) + s.shape[:1] + s.shape[2:], s.dtype) for s in srcs]
        started = _split_start(False, srcs, lands, after, name=f"rs_start_{tag}")
        reduces.append((keys, tag) + started)
        return started[-1]

    kv_parts = []
    df = None
    for l in reversed(range(depth)):
        sv = saved[l]
        proj3 = sv["proj3"]
        if df is None:
            df, dg = _rms_bwd(sv["f"], vec(g_ffn_post[l]), dxs, out_dtype=BF16, name=f"rmsb_ffnpost_{l}")
            G["g_ffn_post"][l] = dg[0]
        dact = _mm(df, W[("w_ffn_down", l)], tb=True, name=f"mmb_down_dx_{l}")
        dw(("w_ffn_down", l), (0, 0), sv["act"], df, f"mmb_down_dw_{l}")
        dug3, duv3, dwg, dwv, dbg, dbv = _ffn_mid_bwd(sv["u3"], dact.reshape(Bl, T, F),
                                                      W["w_ffn_conv"][l], vec(b_ffn_conv[l]), name=f"ffn_mid_bwd_{l}")
        G["w_ffn_conv"][l] = jnp.concatenate([dwg, dwv], axis=1)
        G["b_ffn_conv"][l] = jnp.concatenate([dbg, dbv], axis=1)[0]
        dug, duv = dug3.reshape(N, F), duv3.reshape(N, F)
        dw(("w_ffn_up", l), (0, 0), dug, sv["h2"], f"mmb_up_dw_g_{l}")
        dw(("w_ffn_up", l), (F, 0), duv, sv["h2"], f"mmb_up_dw_v_{l}")
        tok = reduce_start([("w_ffn_down", l), ("w_ffn_up", l)], dug, f"ffn_{l}")
        dh2 = _mm_sum([(dug, W[("w_ffn_up", l)], (0, 0)), (duv, W[("w_ffn_up", l)], (F, 0))], n=D, after=tok,
                      name=f"mmb_up_dx_{l}")
        dx1, dy, dg, dg2 = _rms_pair_bwd(sv["x1"], vec(g_ffn_pre[l]), dh2, dxs, sv["y"], vec(g_mix_post[l]),
                                         name=f"rmsb_ffnpre_mixpost_{l}")
        G["g_ffn_pre"][l] = dg[0]
        G["g_mix_post"][l] = dg2[0]
        dy_main = _mm(dy, W[("w_mix_out", l)], tb=True, n=MIX_W, k=D, name=f"mmb_mixout_dmain_{l}")
        dy_mem = _mm(dy, W[("w_mix_out", l)], tb=True, n=MEM_W, k=D, b_off=(MIX_W, 0),
                     name=f"mmb_mixout_dmem_{l}")
        dw(("w_mix_out", l), (0, 0), sv["y_main"], dy, f"mmb_mixout_dw_main_{l}")
        dw(("w_mix_out", l), (MIX_W, 0), sv["y_mem"], dy, f"mmb_mixout_dw_mem_{l}")
        dq_mem3, dmkv3 = _mem_attn_bwd(proj3, sv["q_off"], sv["mkv3"], dy_mem.reshape(Bl, T, MEM_W),
                                       name=f"memattn_bwd_{l}")
        dq_mem = dq_mem3.reshape(N, MEM_W)
        dmkv = dmkv3.reshape(Bl * Ml, 2 * MEM_W)
        dw(("w_mem_kv", l), (0, 0), sv["memn"], dmkv, f"mmb_memkv_dw_{l}")
        dmemn = _mm(dmkv, W[("w_mem_kv", l)], tb=True, name=f"mmb_memkv_dx_{l}")
        _, dg = _rms_bwd(mem2, vec(g_mem[l]), dmemn, name=f"rmsb_mem_{l}")
        G["g_mem"][l] = dg[0]
        dy_main3 = dy_main.reshape(Bl, T, MIX_W)
        if l < n_a:
            j = l
            dxc3, drp3, dip3, dugate3, dbr, dbi, dlam = _rglru_bwd(
                dy_main3, sv["xc3"], sv["gates3"], proj3, sv["hs3"], vec(b_rg_r[j]), vec(b_rg_i[j]),
                vec(W["lru_lambda"][j]), name=f"rglru_bwd_{l}")
            G["b_rg_r"][j] = dbr.reshape(nblk, bsz)
            G["b_rg_i"][j] = dbi.reshape(nblk, bsz)
            G["lru_lambda"][j] = dlam[0]
            drp, dip = drp3.reshape(N, MIX_W), dip3.reshape(N, MIX_W)
            xc2 = sv["xc3"].reshape(N, MIX_W)
            G["w_rg_r"][j] = _diag_blocks(_mm(xc2, drp, ta=True, name=f"mmb_gates_dw_r_{l}"), nblk, bsz)
            G["w_rg_i"][j] = _diag_blocks(_mm(xc2, dip, ta=True, name=f"mmb_gates_dw_i_{l}"), nblk, bsz)
            dxc = _mm_sum([(drp, wbd[j], (0, 0)), (dip, wbd[j], (0, MIX_W))], tb=True, n=MIX_W,
                          add=dxc3.reshape(N, MIX_W), name=f"mmb_gates_dx_{l}")
            dux3, dwc, dbc = _conv_bwd_call(dxc.reshape(Bl, T, MIX_W), proj3, MIX_W, MIX_W, W["w_conv_a"][j],
                                            name=f"conv_a_bwd_{l}")
            G["w_conv_a"][j] = dwc
            G["b_conv_a"][j] = dbc[0]
            pieces = [(dugate3.reshape(N, MIX_W), 0), (dux3.reshape(N, MIX_W), MIX_W), (dq_mem, 2 * MIX_W)]
            in_key = ("w_in_a", j)
        else:
            j = l - n_a
            dq3, dkc, dkp, dsk = _swa_bwd(proj3, kv3, sinks_b[j], dy_main3, name=f"swa_bwd_{l}")
            kv_parts.append((dkc, dkp))
            G["sinks_b"][j] = dsk[0, :SWA_HEADS]
            pieces = [(dq3.reshape(N, MIX_W), 0), (dq_mem, MIX_W)]
            in_key = ("w_in_b", j)
        in_t = in_key[0] in TRANSPOSED
        for pi, (piece, off) in enumerate(pieces):
            if in_t:
                dw(in_key, (off, 0), piece, sv["h1"], f"mmb_in_dw_{pi}_{l}")
            else:
                dw(in_key, (0, off), sv["h1"], piece, f"mmb_in_dw_{pi}_{l}")
        tok = reduce_start([("w_mix_out", l), ("w_mem_kv", l), in_key], dy, f"mix_{l}")
        dh1 = _mm_sum([(piece, W[in_key], (off, 0) if in_t else (0, off)) for piece, off in pieces], tb=not in_t, n=D,
                      after=tok, name=f"mmb_in_dx_{l}")
        if l > 0 and l != n_a:
            dxs, df, dg, dg2 = _rms_pair_bwd(sv["x0"], vec(g_mix_pre[l]), dh1, dx1, saved[l - 1]["f"],
                                             vec(g_ffn_post[l - 1]), name=f"rmsb_mixpre_ffnpost_{l}")
            G["g_ffn_post"][l - 1] = dg2[0]
        else:
            dxs, dg = _rms_bwd(sv["x0"], vec(g_mix_pre[l]), dh1, add=dx1, name=f"rmsb_mixpre_{l}")
            df = None
        G["g_mix_pre"][l] = dg[0]
        if l == n_a:
            dkv = _kv_grad_combine(kv_parts, name="kv_grad_combine").reshape(N, 2 * MEM_W)
            dw(("w_kv", None), (0, 0), kvn, dkv, "mmb_kv_dw")
            tok = reduce_start([("w_kv", None)], dkv, "kv")
            dkvn = _mm(dkv, W[("w_kv", None)], tb=True, after=tok, name="mmb_kv_dx")
            dxs, dg = _rms_bwd(x_kv, vec(g_kv), dkvn, add=dxs, name="rmsb_kv")
            G["g_kv"] = dg[0]
    grad_x = dxs.reshape(Bl, T, D)
    Gf = {n: (jnp.stack(g) if isinstance(g, list) else g) for n, g in G.items()}

    small4 = []
    for n in SMALL_SHARDED:
        t = _small_to_cols(Gf[n]).astype(BF16)
        small4.append(t.reshape(1, N_DEV, -1, t.shape[-1]))
    small_lands = [lax.empty((len(FLIPS),) + s.shape[:1] + s.shape[2:], s.dtype) for s in small4]
    small_started = _split_start(False, small4, small_lands, dxs, name="rs_start_small")
    r_blob = _pack([Gf[n].astype(F32) for n in REPL], REPL_ROWS, F32)
    r_parts = _all_gather(r_blob, name="ag_repl_grads")
    parts = {}
    for keys, tag, ssem, rsem, srcs, lands, _ in reduces:
        for kk, s, g7 in zip(keys, srcs, _split_wait(False, ssem, rsem, srcs, lands, small_started[-1],
                                                     name=f"rs_wait_{tag}")):
            parts[kk] = (s, g7)

    res = [{} for _ in range(4)]
    for n, _ in SHARDED:
        if n in SMALL_SHARDED:
            continue
        idx = [None] if w_loc[n].ndim == 2 else list(range(w_loc[n].shape[0]))
        wmv = [as_rows(n, a[n]) for a in (w_loc, m_loc, v_loc)]
        shp3 = (len(idx),) + wmv[0].shape[-2:]
        outs = _adamw_layers([parts[(n, i)][0] for i in idx], [parts[(n, i)][1] for i in idx],
                             *[a.reshape(shp3) for a in wmv], name=f"adamw_{n}")
        for k in range(4):
            res[k][n] = as_rows(n, outs[k].reshape(wmv[0].shape))
    last = res[0]["w_kv"]
    small_got = _split_wait(False, *small_started[:4], last, name="rs_wait_small")
    for n, own, g7 in zip(SMALL_SHARDED, small4, small_got):
        shp3 = own.shape[:1] + own.shape[2:]
        outs = _adamw_layers([own], [g7], w_loc[n].reshape(shp3), m_loc[n].reshape(shp3), v_loc[n].reshape(shp3),
                             name=f"adamw_{n}")
        for k in range(4):
            res[k][n] = outs[k].reshape(w_loc[n].shape)
    outs_rp = _adamw_replicated(r_parts, _pack([w_loc[n] for n in REPL], REPL_ROWS, F32),
                                _pack([m_loc[n] for n in REPL], REPL_ROWS, F32),
                                _pack([v_loc[n] for n in REPL], REPL_ROWS, F32),
                                name="adamw_replicated")
    rp_shapes = [w_loc[n].shape for n in REPL]
    for k in range(4):
        res[k].update(zip(REPL, _unpack(outs_rp[k], rp_shapes)))
    out = [loss, grad_x]
    for k in range(4):
        out += [res[k][n] for n in WEIGHTS]
    return tuple(out)
```

```python
import functools
import math

import numpy as np
import jax
import jax.numpy as jnp
from jax import lax
from jax.experimental import pallas as pl
from jax.experimental.pallas import tpu as pltpu

F32 = jnp.float32
BF16 = jnp.bfloat16
S = jax.ShapeDtypeStruct
MESH = pl.DeviceIdType.MESH
ANY = pl.BlockSpec(memory_space=pl.ANY)

HEAD = 64
MEM_HEADS = 4
MEM_W = MEM_HEADS * HEAD
SWA_HEADS = 12
SWA_GROUP = 3
MIX_W = SWA_HEADS * HEAD
WIN = 128
LRU_C = 8.0
EPS = 1e-6
ADAM_LR, ADAM_B1, ADAM_B2, ADAM_EPS, ADAM_WD, ADAM_STEP = 0.001, 0.9, 0.999, 1e-08, 0.01, 10
GELU_C0 = math.sqrt(2.0 / math.pi)
GELU_C1 = 0.044715
N_DEV = 8
LANES = 128
CT = 128
VMEM_LIMIT = 48 * 1024 * 1024
MM_VMEM_BUDGET = 36 * 1024 * 1024
REPL_ROWS = 256

SHARDED = (("w_mem_kv", 1), ("w_mix_out", 1), ("w_ffn_up", 2), ("w_ffn_conv", 2), ("w_ffn_down", 1), ("w_in_a", 2),
           ("w_conv_a", 2), ("b_conv_a", 1), ("lru_lambda", 1), ("w_in_b", 1), ("w_kv", 0))
SMALL_SHARDED = ("w_ffn_conv", "w_conv_a", "b_conv_a", "lru_lambda")
TRANSPOSED = ("w_ffn_up", "w_in_a")
REPL = ("g_mix_pre", "g_mix_post", "g_ffn_pre", "g_ffn_post", "g_mem", "b_ffn_conv", "w_rg_r", "b_rg_r", "w_rg_i",
        "b_rg_i", "sinks_b", "g_kv")
WEIGHTS = ("g_mix_pre", "g_mix_post", "g_ffn_pre", "g_ffn_post", "g_mem", "w_mem_kv", "w_mix_out", "w_ffn_up",
           "w_ffn_conv", "b_ffn_conv", "w_ffn_down", "w_in_a", "w_conv_a", "b_conv_a", "w_rg_r", "b_rg_r", "w_rg_i",
           "b_rg_i", "lru_lambda", "w_in_b", "sinks_b", "g_kv", "w_kv")


def _alibi_slopes(n):
    def pow2(m):
        start = 2.0 ** (-8.0 / m)
        return [start ** (i + 1) for i in range(m)]
    c = 2 ** int(math.floor(math.log2(n)))
    s = pow2(c)
    if c != n:
        s = s + pow2(2 * c)[0::2][: n - c]
    return [float(v) for v in np.asarray(s, dtype=np.float32)]


SLOPES = _alibi_slopes(SWA_HEADS)


def _tile(n, cands):
    for c in cands:
        if n % c == 0:
            return c
    return n


def _cparams(*sem):
    return pltpu.CompilerParams(dimension_semantics=sem, vmem_limit_bytes=VMEM_LIMIT)


def _mm_tiles(M, N, K, a_bytes, b_bytes, o_bytes, add_bytes, offsets):
    m_off, n_offs, k_off = offsets
    tms = [c for c in (1024, 512, 256, 128) if M % c == 0 and m_off % c == 0] or [M]
    tns = [c for c in (1408, 1024, 896, 768, 512, 384, 256, 128)
           if N % c == 0 and all(o % c == 0 for o in n_offs)] or [N]
    tks = [c for c in (K, 2048, 1408, 1024, 512, 256, 128) if c <= K and K % c == 0 and k_off % c == 0]
    best = None
    for tk in tks:
        fits = []
        for tm in tms:
            for tn in tns:
                need = 2 * (tm * tk * a_bytes + tk * tn * b_bytes + tm * tn * (o_bytes + add_bytes))
                need += tm * tn * 4 * (2 if tk < K else 1)
                need += (tm * tk * 2 if a_bytes != 2 else 0) + (tk * tn * 2 if b_bytes != 2 else 0)
                if need <= MM_VMEM_BUDGET:
                    fits.append((tm * tn, min(tm, 512), tm, tn))
        if fits:
            _, _, tm, tn = max(fits)
            best = (tm, tn, tk)
            break
    assert best is not None, (M, N, K)
    return best


def _mm(a, b, *, ta=False, tb=False, n=None, k=None, b_off=(0, 0), out_dtype=F32, add=None, into=None, after=None,
        name="mm"):
    if ta:
        K, M = a.shape
    else:
        M, K = a.shape
    if tb:
        N = b.shape[-2] if n is None else n
    else:
        N = b.shape[-1] if n is None else n
    assert k is None or k == K
    ro, co = b_off
    n_off, k_off = (ro, co) if tb else (co, ro)
    oro, oco = (0, 0) if into is None else into[3]
    tm, tn, tk = _mm_tiles(M, N, K, a.dtype.itemsize, b.dtype.itemsize, jnp.dtype(out_dtype).itemsize,
                           0 if add is None else add.dtype.itemsize, (oro, (n_off, oco), k_off))
    nk = K // tk
    if tb:
        b_spec = pl.BlockSpec((tn, tk), lambda i, j, kk: (j + ro // tn, kk + co // tk))
        b_dims = (1,)
    else:
        b_spec = pl.BlockSpec((tk, tn), lambda i, j, kk: (kk + ro // tk, j + co // tn))
        b_dims = (0,)
    if ta:
        a_spec = pl.BlockSpec((tk, tm), lambda i, j, kk: (kk, i))
        a_dims = (0,)
    else:
        a_spec = pl.BlockSpec((tm, tk), lambda i, j, kk: (i, kk))
        a_dims = (1,)
    dims = ((a_dims, b_dims), ((), ()))
    add_spec = pl.BlockSpec((tm, tn), lambda i, j, kk: (i, j))
    has_add = add is not None
    if into is None:
        o_spec, o_shape, buf = add_spec, (M, N), None
    else:
        buf, o_shape, ol, _ = into
        assert not has_add
        o_spec = pl.BlockSpec((None, tm, tn), lambda i, j, kk: (ol, i + oro // tm, j + oco // tn))
    has_buf = buf is not None

    def body(*refs):
        refs = list(refs)
        acc_ref = refs.pop() if nk > 1 else None
        o_ref = refs.pop()
        a_ref, b_ref = refs[0], refs[1]
        add_ref = refs[2] if has_add else None
        part = lax.dot_general(a_ref[...].astype(BF16), b_ref[...].astype(BF16), dims, preferred_element_type=F32)

        def finish(r):
            if has_add:
                r = r + add_ref[...].astype(F32)
            o_ref[...] = r.astype(out_dtype)

        if nk == 1:
            finish(part)
        else:
            kk = pl.program_id(2)

            @pl.when(kk == 0)
            def _():
                acc_ref[...] = part

            @pl.when(kk > 0)
            def _():
                acc_ref[...] += part

            @pl.when(kk == nk - 1)
            def _():
                finish(acc_ref[...])

    in_specs = [a_spec, b_spec] + ([add_spec] if has_add else []) + ([ANY] if has_buf else [])
    args = (a, b) + ((add,) if has_add else ()) + ((buf,) if has_buf else ())
    if after is not None:
        in_specs, args = in_specs + [ANY], args + (after,)
    return pl.pallas_call(
        body, grid=(M // tm, N // tn, nk), in_specs=in_specs, out_specs=o_spec,
        out_shape=S(o_shape, out_dtype), scratch_shapes=[pltpu.VMEM((tm, tn), F32)] if nk > 1 else [],
        input_output_aliases={2: 0} if has_buf else {},
        compiler_params=_cparams("parallel", "parallel", "arbitrary"), name=name)(*args)


def _mm_sum(pieces, *, tb=False, n, out_dtype=F32, add=None, after=None, name="mm_sum"):
    M = pieces[0][0].shape[0]
    ks = [a.shape[1] for a, _, _ in pieces]
    a_bytes = max(a.dtype.itemsize for a, _, _ in pieces)
    b_bytes = max(b.dtype.itemsize for _, b, _ in pieces)
    n_offs = tuple(off[0] if tb else off[1] for _, _, off in pieces)
    for kp, (_, _, off) in zip(ks, pieces):
        assert (off[1] if tb else off[0]) % kp == 0
    tm, tn, tk = _mm_tiles(M, n, sum(ks), a_bytes, b_bytes, jnp.dtype(out_dtype).itemsize, 0, (0, n_offs, 0))
    assert tk == sum(ks)
    a_specs = [pl.BlockSpec((tm, kp), lambda i, j: (i, 0)) for kp in ks]
    if tb:
        b_specs = [pl.BlockSpec((tn, kp), lambda i, j, ro=off[0], co=off[1], kp=kp: (j + ro // tn, co // kp))
                   for kp, (_, _, off) in zip(ks, pieces)]
        dims = NT
    else:
        b_specs = [pl.BlockSpec((kp, tn), lambda i, j, ro=off[0], co=off[1], kp=kp: (ro // kp, j + co // tn))
                   for kp, (_, _, off) in zip(ks, pieces)]
        dims = (((1,), (0,)), ((), ()))
    npc = len(pieces)
    o_spec = pl.BlockSpec((tm, tn), lambda i, j: (i, j))

    def body(*refs):
        o_ref = refs[2 * npc + (add is not None) + (after is not None)]
        acc = refs[2 * npc][...].astype(F32) if add is not None else None
        for p in range(npc):
            part = lax.dot_general(refs[p][...].astype(BF16), refs[npc + p][...].astype(BF16), dims,
                                   preferred_element_type=F32)
            acc = part if acc is None else acc + part
        o_ref[...] = acc.astype(out_dtype)

    args = [a for a, _, _ in pieces] + [b for _, b, _ in pieces]
    in_specs = a_specs + b_specs
    if add is not None:
        in_specs, args = in_specs + [o_spec], args + [add]
    if after is not None:
        in_specs, args = in_specs + [ANY], args + [after]
    return pl.pallas_call(
        body, grid=(M // tm, n // tn), in_specs=in_specs, out_specs=o_spec,
        out_shape=S((M, n), out_dtype), compiler_params=_cparams("parallel", "parallel"), name=name)(*args)


def _rms_fwd(x, g, out_dtype, res=None, name="rms_fwd"):
    N, D = x.shape
    tm = _tile(N, (512, 256, 128))
    has_res = res is not None

    def body(*refs):
        if has_res:
            x_ref, g_ref, r_ref, o_ref = refs
        else:
            x_ref, g_ref, o_ref = refs
        xv = x_ref[...].astype(F32)
        y = xv * lax.rsqrt(jnp.mean(xv * xv, axis=-1, keepdims=True) + EPS) * g_ref[...]
        if has_res:
            y = y + r_ref[...]
        o_ref[...] = y.astype(out_dtype)

    row = pl.BlockSpec((tm, D), lambda i: (i, 0))
    vec = pl.BlockSpec((1, D), lambda i: (0, 0))
    return pl.pallas_call(
        body, grid=(N // tm,), in_specs=[row, vec] + ([row] if has_res else []), out_specs=row,
        out_shape=S((N, D), out_dtype), compiler_params=_cparams("parallel"), name=name)(
            *((x, g) + ((res,) if has_res else ())))


def _rms_bwd(x, g, dy, add=None, out_dtype=F32, name="rms_bwd"):
    N, D = x.shape
    tm = _tile(N, (512, 256, 128))
    has_add = add is not None

    def body(*refs):
        if has_add:
            x_ref, g_ref, dy_ref, add_ref, dx_ref, dg_ref = refs
        else:
            x_ref, g_ref, dy_ref, dx_ref, dg_ref = refs
        xv = x_ref[...].astype(F32)
        dyv = dy_ref[...].astype(F32)
        r = lax.rsqrt(jnp.mean(xv * xv, axis=-1, keepdims=True) + EPS)
        u = dyv * g_ref[...]
        dx = r * u - xv * (r * r * r * jnp.mean(u * xv, axis=-1, keepdims=True))
        if has_add:
            dx = dx + add_ref[...]
        dx_ref[...] = dx.astype(out_dtype)

        @pl.when(pl.program_id(0) == 0)
        def _():
            dg_ref[...] = jnp.zeros_like(dg_ref)

        dg_ref[...] += jnp.sum(dyv * xv * r, axis=0, keepdims=True)

    row = pl.BlockSpec((tm, D), lambda i: (i, 0))
    vec = pl.BlockSpec((1, D), lambda i: (0, 0))
    return pl.pallas_call(
        body, grid=(N // tm,), in_specs=[row, vec, row] + ([row] if has_add else []), out_specs=(row, vec),
        out_shape=(S((N, D), out_dtype), S((1, D), F32)), compiler_params=_cparams("arbitrary"), name=name)(
            *((x, g, dy) + ((add,) if has_add else ())))


def _rms_pair_fwd(y, g_post, res, gains, name):
    N, D = y.shape
    tm = _tile(N, (512, 256, 128))
    ng = len(gains)

    def body(*refs):
        y_ref, gp_ref, r_ref = refs[:3]
        g_refs = refs[3:3 + ng]
        x_ref = refs[3 + ng]
        h_refs = refs[4 + ng:]
        yv = y_ref[...]
        x = r_ref[...] + yv * lax.rsqrt(jnp.mean(yv * yv, axis=-1, keepdims=True) + EPS) * gp_ref[...]
        x_ref[...] = x
        xn = x * lax.rsqrt(jnp.mean(x * x, axis=-1, keepdims=True) + EPS)
        for g_ref, h_ref in zip(g_refs, h_refs):
            h_ref[...] = (xn * g_ref[...]).astype(BF16)

    row = pl.BlockSpec((tm, D), lambda i: (i, 0))
    vec = pl.BlockSpec((1, D), lambda i: (0, 0))
    return pl.pallas_call(
        body, grid=(N // tm,), in_specs=[row, vec, row] + [vec] * ng, out_specs=(row,) * (1 + ng),
        out_shape=(S((N, D), F32),) + (S((N, D), BF16),) * ng, compiler_params=_cparams("parallel"), name=name)(
            y, g_post, res, *gains)


def _rms_pair_bwd(xa, ga, dya, add, xb, gb, name):
    N, D = xa.shape
    tm = _tile(N, (512, 256, 128))

    def one(xv, g_ref, dyv):
        r = lax.rsqrt(jnp.mean(xv * xv, axis=-1, keepdims=True) + EPS)
        u = dyv * g_ref[...]
        dx = r * u - xv * (r * r * r * jnp.mean(u * xv, axis=-1, keepdims=True))
        return dx, jnp.sum(dyv * xv * r, axis=0, keepdims=True)

    def body(xa_ref, ga_ref, dya_ref, add_ref, xb_ref, gb_ref, da_ref, db_ref, dga_ref, dgb_ref):
        da, dga = one(xa_ref[...].astype(F32), ga_ref, dya_ref[...].astype(F32))
        da = da + add_ref[...]
        da_ref[...] = da
        db, dgb = one(xb_ref[...].astype(F32), gb_ref, da)
        db_ref[...] = db.astype(BF16)

        @pl.when(pl.program_id(0) == 0)
        def _():
            dga_ref[...] = jnp.zeros_like(dga_ref)
            dgb_ref[...] = jnp.zeros_like(dgb_ref)

        dga_ref[...] += dga
        dgb_ref[...] += dgb

    row = pl.BlockSpec((tm, D), lambda i: (i, 0))
    vec = pl.BlockSpec((1, D), lambda i: (0, 0))
    return pl.pallas_call(
        body, grid=(N // tm,), in_specs=[row, vec, row, row, row, vec], out_specs=(row, row, vec, vec),
        out_shape=(S((N, D), F32), S((N, D), BF16), S((1, D), F32), S((1, D), F32)),
        compiler_params=_cparams("arbitrary"), name=name)(xa, ga, dya, add, xb, gb)


def _shift_down(x, s, row):
    return jnp.where(row >= s, pltpu.roll(x, s, axis=0), 0.0)


def _shift_up(x, s, row):
    T = x.shape[0]
    return jnp.where(row < T - s, pltpu.roll(x, T - s, axis=0), 0.0)


SLAB = 16


def _conv_wrap(x, w_ref, b_ref):
    W = w_ref.shape[0]
    y = x * w_ref[W - 1:W, :] + b_ref[...]
    for s in range(1, W):
        y = y + pltpu.roll(x, s, axis=0) * w_ref[W - 1 - s:W - s, :]
    return y


def _conv_rows(x_ref, w_ref, b_ref, lo, hi):
    W = w_ref.shape[0]
    y = x_ref[lo:hi, :] * w_ref[W - 1:W, :] + b_ref[...]
    for s in range(1, W):
        y = y + x_ref[lo - s:hi - s, :] * w_ref[W - 1 - s:W - s, :]
    return y


def _taps(x_ref, W):
    T = x_ref.shape[0]
    head = x_ref[0:SLAB, :]
    row = lax.broadcasted_iota(jnp.int32, head.shape, 0)
    return [x_ref[...]] + [jnp.concatenate([_shift_down(head, s, row), x_ref[SLAB - s:T - s, :]], axis=0)
                           for s in range(1, W)]


def _conv_taps(xs, w_ref, b_ref):
    W = w_ref.shape[0]
    y = xs[0] * w_ref[W - 1:W, :] + b_ref[...]
    for s in range(1, W):
        y = y + xs[s] * w_ref[W - 1 - s:W - s, :]
    return y


def _conv_head(x_head, w_ref, b_ref):
    row = lax.broadcasted_iota(jnp.int32, x_head.shape, 0)
    return _conv_taps([x_head] + [_shift_down(x_head, s, row) for s in range(1, w_ref.shape[0])], w_ref, b_ref)


def _conv_bwd_taps(dy, xs, w_ref, row):
    W = w_ref.shape[0]
    dx = dy * w_ref[W - 1:W, :]
    dws = [None] * W
    dws[W - 1] = jnp.sum(dy * xs[0], axis=0, keepdims=True)
    for s in range(1, W):
        dx = dx + _shift_up(dy, s, row) * w_ref[W - 1 - s:W - s, :]
        dws[W - 1 - s] = jnp.sum(dy * xs[s], axis=0, keepdims=True)
    return dx, jnp.concatenate(dws, axis=0), jnp.sum(dy, axis=0, keepdims=True)


def _conv_bwd_wrap(dy, x, w_ref):
    W = w_ref.shape[0]
    T = dy.shape[0]
    dx = dy * w_ref[W - 1:W, :]
    dws = [None] * W
    dws[W - 1] = jnp.sum(dy * x, axis=0, keepdims=True)
    for s in range(1, W):
        up = pltpu.roll(dy, T - s, axis=0)
        dx = dx + up * w_ref[W - 1 - s:W - s, :]
        dws[W - 1 - s] = jnp.sum(up * x, axis=0, keepdims=True)
    return dx, jnp.concatenate(dws, axis=0), jnp.sum(dy, axis=0, keepdims=True)


def _conv_bwd_fix(dy_head, dy_tail, x_tail, w_ref):
    row = lax.broadcasted_iota(jnp.int32, dy_tail.shape, 0)
    W = w_ref.shape[0]
    dx = dy_tail * w_ref[W - 1:W, :]
    extra = [jnp.zeros((1, dy_tail.shape[1]), F32)] * W
    for s in range(1, W):
        dx = dx + _shift_up(dy_tail, s, row) * w_ref[W - 1 - s:W - s, :]
        extra[W - 1 - s] = jnp.sum(jnp.where(row < s, dy_head * pltpu.roll(x_tail, s, axis=0), 0.0), axis=0,
                                   keepdims=True)
    return dx, jnp.concatenate(extra, axis=0)


def _gelu(g):
    t = jnp.tanh(GELU_C0 * (g + GELU_C1 * g * g * g))
    return 0.5 * g * (1.0 + t), t


def _dgelu(g, t):
    return 0.5 * (1.0 + t) + 0.5 * g * (1.0 - t * t) * (GELU_C0 * (1.0 + 3.0 * GELU_C1 * g * g))


def _cspec(T, off=0, ct=CT):
    return pl.BlockSpec((1, T, ct), lambda j, b: (b, 0, j + off))


def _pspec(rows, off=0, ct=CT):
    return pl.BlockSpec((rows, ct), lambda j, b: (0, j + off))


def _conv_fwd_call(x3, x_off, C, w, b, name):
    Bl, T, _ = x3.shape
    W = w.shape[0]

    def body(x_ref, w_ref, b_ref, o_ref):
        o_ref[0] = _conv_wrap(x_ref[0], w_ref, b_ref)
        o_ref[0, 0:SLAB, :] = _conv_head(x_ref[0, 0:SLAB, :], w_ref, b_ref)

    return pl.pallas_call(
        body, grid=(C // CT, Bl), in_specs=[_cspec(T, x_off // CT), _pspec(W), _pspec(1)], out_specs=_cspec(T),
        out_shape=S((Bl, T, C), F32), compiler_params=_cparams("parallel", "arbitrary"), name=name)(x3, w, b)


def _conv_bwd_call(dy3, x3, x_off, C, w, name):
    Bl, T, _ = x3.shape
    W = w.shape[0]

    def body(dy_ref, x_ref, w_ref, dx_ref, dw_ref, db_ref):
        dx, dw, db = _conv_bwd_wrap(dy_ref[0], x_ref[0], w_ref)
        dx_tail, dw_extra = _conv_bwd_fix(dy_ref[0, 0:SLAB, :], dy_ref[0, T - SLAB:T, :], x_ref[0, T - SLAB:T, :],
                                          w_ref)
        dx_ref[0] = dx.astype(BF16)
        dx_ref[0, T - SLAB:T, :] = dx_tail.astype(BF16)

        @pl.when(pl.program_id(1) == 0)
        def _():
            dw_ref[...] = jnp.zeros_like(dw_ref)
            db_ref[...] = jnp.zeros_like(db_ref)

        dw_ref[...] += dw - dw_extra
        db_ref[...] += db

    return pl.pallas_call(
        body, grid=(C // CT, Bl), in_specs=[_cspec(T), _cspec(T, x_off // CT), _pspec(W)],
        out_specs=(_cspec(T), _pspec(W), _pspec(1)),
        out_shape=(S((Bl, T, C), BF16), S((W, C), F32), S((1, C), F32)),
        compiler_params=_cparams("parallel", "arbitrary"), name=name)(dy3, x3, w)


def _ffn_mid_fwd(u3, wc, bc, name):
    Bl, T, F2 = u3.shape
    F = F2 // 2
    nf = F // CT

    def body(ug_ref, uv_ref, wg_ref, wv_ref, bg_ref, bv_ref, o_ref):
        g = _conv_rows(ug_ref.at[0], wg_ref, bg_ref, SLAB, T)
        v = _conv_rows(uv_ref.at[0], wv_ref, bv_ref, SLAB, T)
        o_ref[0, SLAB:T, :] = (_gelu(g)[0] * v).astype(BF16)
        g = _conv_head(ug_ref[0, 0:SLAB, :], wg_ref, bg_ref)
        v = _conv_head(uv_ref[0, 0:SLAB, :], wv_ref, bv_ref)
        o_ref[0, 0:SLAB, :] = (_gelu(g)[0] * v).astype(BF16)

    return pl.pallas_call(
        body, grid=(nf, Bl),
        in_specs=[_cspec(T), _cspec(T, nf), _pspec(3), _pspec(3, nf), _pspec(1), _pspec(1, nf)], out_specs=_cspec(T),
        out_shape=S((Bl, T, F), BF16), compiler_params=_cparams("parallel", "arbitrary"), name=name)(
            u3, u3, wc, wc, bc, bc)


def _ffn_mid_bwd(u3, dact3, wc, bc, name):
    Bl, T, F2 = u3.shape
    F = F2 // 2
    nf = F // CT

    def body(ug_ref, uv_ref, da_ref, wg_ref, wv_ref, bg_ref, bv_ref, dug_ref, duv_ref, dwg_ref, dwv_ref, dbg_ref,
             dbv_ref):
        row = lax.broadcasted_iota(jnp.int32, (T, CT), 0)
        ugs = _taps(ug_ref.at[0], 3)
        uvs = _taps(uv_ref.at[0], 3)
        g = _conv_taps(ugs, wg_ref, bg_ref)
        v = _conv_taps(uvs, wv_ref, bv_ref)
        da = da_ref[0]
        gel, t = _gelu(g)
        dg = da * v * _dgelu(g, t)
        dv = da * gel
        dug, dwg, dbg = _conv_bwd_taps(dg, ugs, wg_ref, row)
        duv, dwv, dbv = _conv_bwd_taps(dv, uvs, wv_ref, row)
        dug_ref[0] = dug.astype(BF16)
        duv_ref[0] = duv.astype(BF16)

        @pl.when(pl.program_id(1) == 0)
        def _():
            dwg_ref[...] = jnp.zeros_like(dwg_ref)
            dwv_ref[...] = jnp.zeros_like(dwv_ref)
            dbg_ref[...] = jnp.zeros_like(dbg_ref)
            dbv_ref[...] = jnp.zeros_like(dbv_ref)

        dwg_ref[...] += dwg
        dwv_ref[...] += dwv
        dbg_ref[...] += dbg
        dbv_ref[...] += dbv

    return pl.pallas_call(
        body, grid=(nf, Bl),
        in_specs=[_cspec(T), _cspec(T, nf), _cspec(T), _pspec(3), _pspec(3, nf), _pspec(1), _pspec(1, nf)],
        out_specs=(_cspec(T), _cspec(T), _pspec(3), _pspec(3), _pspec(1), _pspec(1)),
        out_shape=(S((Bl, T, F), BF16), S((Bl, T, F), BF16), S((3, F), F32), S((3, F), F32), S((1, F), F32),
                   S((1, F), F32)),
        compiler_params=_cparams("parallel", "arbitrary"), name=name)(u3, u3, dact3, wc, wc, bc, bc)


def _lru_gates(xc, rp, ip, br_ref, bi_ref, lam_ref):
    r = jax.nn.sigmoid(rp + br_ref[...])
    i = jax.nn.sigmoid(ip + bi_ref[...])
    lam = lam_ref[...]
    sp = jnp.maximum(-lam, 0.0) + jnp.log1p(jnp.exp(-jnp.abs(lam)))
    log_a = (-LRU_C) * r * sp
    a = jnp.exp(log_a)
    z = 2.0 * log_a
    one_m_a2 = jnp.where(z > -0.05, -z * (1.0 + z * (0.5 + z * (1.0 / 6.0 + z * (1.0 / 24.0)))), 1.0 - a * a)
    mult = jnp.sqrt(one_m_a2)
    return r, i, sp, a, mult


def _rglru_fwd(xc3, gates3, proj3, br, bi, lam, name):
    Bl, T, C = xc3.shape
    nsteps = int(math.log2(T))
    assert 1 << nsteps == T

    def body(xc_ref, rp_ref, ip_ref, ug_ref, br_ref, bi_ref, lam_ref, y_ref, h_ref):
        row = lax.broadcasted_iota(jnp.int32, (T, CT), 0)
        xc = xc_ref[0]
        r, i, sp, a, mult = _lru_gates(xc, rp_ref[0], ip_ref[0], br_ref, bi_ref, lam_ref)
        b = mult * (i * xc)
        for st in range(nsteps):
            s = 1 << st
            a_sh = jnp.where(row >= s, pltpu.roll(a, s, axis=0), 1.0)
            b = a * _shift_down(b, s, row) + b
            a = a * a_sh
        h_ref[0] = b
        y_ref[0] = (b * _gelu(ug_ref[0])[0]).astype(BF16)

    return pl.pallas_call(
        body, grid=(C // CT, Bl),
        in_specs=[_cspec(T), _cspec(T), _cspec(T, C // CT), _cspec(T), _pspec(1), _pspec(1), _pspec(1)],
        out_specs=(_cspec(T), _cspec(T)), out_shape=(S((Bl, T, C), BF16), S((Bl, T, C), F32)),
        compiler_params=_cparams("parallel", "arbitrary"), name=name)(xc3, gates3, gates3, proj3, br, bi, lam)


def _rglru_bwd(dy3, xc3, gates3, proj3, h3, br, bi, lam, name):
    Bl, T, C = xc3.shape
    nsteps = int(math.log2(T))

    def body(dy_ref, xc_ref, rp_ref, ip_ref, ug_ref, h_ref, br_ref, bi_ref, lam_ref,
             dxc_ref, drp_ref, dip_ref, dug_ref, dbr_ref, dbi_ref, dlam_ref):
        row = lax.broadcasted_iota(jnp.int32, (T, CT), 0)
        xc = xc_ref[0]
        r, i, sp, a, mult = _lru_gates(xc, rp_ref[0], ip_ref[0], br_ref, bi_ref, lam_ref)
        h = h_ref[0]
        dy = dy_ref[0]
        ug = ug_ref[0]
        gel, t = _gelu(ug)
        dug_ref[0] = (dy * h * _dgelu(ug, t)).astype(BF16)
        gacc = dy * gel
        an = _shift_up(a, 1, row)
        for st in range(nsteps):
            s = 1 << st
            an_sh = jnp.where(row < T - s, pltpu.roll(an, T - s, axis=0), 1.0)
            gacc = an * _shift_up(gacc, s, row) + gacc
            an = an * an_sh
        da = gacc * _shift_down(h, 1, row)
        ix = i * xc
        d_mult = gacc * ix
        d_i = gacc * mult * xc
        dxc_ref[0] = gacc * mult * i
        d_log_a = da * a - d_mult * (a * a) / mult
        d_r = d_log_a * ((-LRU_C) * sp)
        d_sp = jnp.sum(d_log_a * ((-LRU_C) * r), axis=0, keepdims=True)
        drp = d_r * r * (1.0 - r)
        dip = d_i * i * (1.0 - i)
        drp_ref[0] = drp.astype(BF16)
        dip_ref[0] = dip.astype(BF16)

        @pl.when(pl.program_id(1) == 0)
        def _():
            dbr_ref[...] = jnp.zeros_like(dbr_ref)
            dbi_ref[...] = jnp.zeros_like(dbi_ref)
            dlam_ref[...] = jnp.zeros_like(dlam_ref)

        dbr_ref[...] += jnp.sum(drp, axis=0, keepdims=True)
        dbi_ref[...] += jnp.sum(dip, axis=0, keepdims=True)
        dlam_ref[...] += d_sp * (-jax.nn.sigmoid(-lam_ref[...]))

    vec = S((1, C), F32)
    act = S((Bl, T, C), BF16)
    return pl.pallas_call(
        body, grid=(C // CT, Bl),
        in_specs=[_cspec(T), _cspec(T), _cspec(T), _cspec(T, C // CT), _cspec(T), _cspec(T)] + [_pspec(1)] * 3,
        out_specs=(_cspec(T), _cspec(T), _cspec(T), _cspec(T), _pspec(1), _pspec(1), _pspec(1)),
        out_shape=(S((Bl, T, C), F32), act, act, act, vec, vec, vec),
        compiler_params=_cparams("parallel", "arbitrary"), name=name)(dy3, xc3, gates3, gates3, proj3, h3, br, bi, lam)


NT = (((1,), (1,)), ((), ()))
TN = (((0,), (0,)), ((), ()))


def _hs(h):
    return slice(h * HEAD, (h + 1) * HEAD)


def _head_rows(x):
    head = lax.shift_right_logical(lax.broadcasted_iota(jnp.int32, x.shape, 1), HEAD.bit_length() - 1)
    return jnp.concatenate([jnp.where(head == h, x, jnp.zeros_like(x)) for h in range(MEM_HEADS)], axis=0)


def _head_sum(xbd):
    M = xbd.shape[0] // MEM_HEADS
    head = lax.shift_right_logical(lax.broadcasted_iota(jnp.int32, (M, xbd.shape[1]), 1), HEAD.bit_length() - 1)
    out = jnp.zeros((M, xbd.shape[1]), xbd.dtype)
    for h in range(MEM_HEADS):
        out = jnp.where(head == h, xbd[h * M:(h + 1) * M], out)
    return out


def _mem_probs(q, kbd):
    M = kbd.shape[0] // MEM_HEADS
    s = lax.dot_general(q, kbd, NT, preferred_element_type=F32) * (HEAD ** -0.5)
    ps = []
    for h in range(MEM_HEADS):
        sh = s[:, h * M:(h + 1) * M]
        e = jnp.exp(sh - jnp.max(sh, axis=-1, keepdims=True))
        ps.append(e / jnp.sum(e, axis=-1, keepdims=True))
    return ps


def _mem_attn_fwd(proj3, q_off, mkv3, name):
    Bl, T, _ = proj3.shape
    M = mkv3.shape[1]
    tq = _tile(T, (512, 256, 128))

    def body(q_ref, k_ref, v_ref, o_ref):
        q = q_ref[0].astype(BF16)
        kbd = _head_rows(k_ref[0].astype(BF16))
        vbd = _head_rows(v_ref[0].astype(BF16))
        p = jnp.concatenate(_mem_probs(q, kbd), axis=-1).astype(BF16)
        o_ref[0] = jnp.dot(p, vbd, preferred_element_type=F32).astype(BF16)

    return pl.pallas_call(
        body, grid=(Bl, T // tq),
        in_specs=[pl.BlockSpec((1, tq, MEM_W), lambda b, t: (b, t, q_off // MEM_W)),
                  pl.BlockSpec((1, M, MEM_W), lambda b, t: (b, 0, 0)),
                  pl.BlockSpec((1, M, MEM_W), lambda b, t: (b, 0, 1))],
        out_specs=pl.BlockSpec((1, tq, MEM_W), lambda b, t: (b, t, 0)),
        out_shape=S((Bl, T, MEM_W), BF16), compiler_params=_cparams("parallel", "parallel"), name=name)(
            proj3, mkv3, mkv3)


def _mem_attn_bwd(proj3, q_off, mkv3, do3, name):
    Bl, T, _ = proj3.shape
    M = mkv3.shape[1]
    tq = _tile(T, (512, 256, 128))
    scale = HEAD ** -0.5

    def body(q_ref, k_ref, v_ref, do_ref, dq_ref, dkv_ref):
        q = q_ref[0].astype(BF16)
        kbd = _head_rows(k_ref[0].astype(BF16))
        vbd = _head_rows(v_ref[0].astype(BF16))
        do = do_ref[0].astype(BF16)
        ps = _mem_probs(q, kbd)
        dvbd = lax.dot_general(jnp.concatenate(ps, axis=-1).astype(BF16), do, TN, preferred_element_type=F32)
        dp = lax.dot_general(do, vbd, NT, preferred_element_type=F32)
        dss = []
        for h in range(MEM_HEADS):
            dph = dp[:, h * M:(h + 1) * M]
            dss.append(ps[h] * (dph - jnp.sum(ps[h] * dph, axis=-1, keepdims=True)) * scale)
        ds = jnp.concatenate(dss, axis=-1).astype(BF16)
        dq_ref[0] = jnp.dot(ds, kbd, preferred_element_type=F32).astype(BF16)
        dkbd = lax.dot_general(ds, q, TN, preferred_element_type=F32)

        @pl.when(pl.program_id(1) == 0)
        def _():
            dkv_ref[...] = jnp.zeros_like(dkv_ref)

        dkv_ref[0] += jnp.concatenate([_head_sum(dkbd), _head_sum(dvbd)], axis=-1)

    return pl.pallas_call(
        body, grid=(Bl, T // tq),
        in_specs=[pl.BlockSpec((1, tq, MEM_W), lambda b, t: (b, t, q_off // MEM_W)),
                  pl.BlockSpec((1, M, MEM_W), lambda b, t: (b, 0, 0)),
                  pl.BlockSpec((1, M, MEM_W), lambda b, t: (b, 0, 1)),
                  pl.BlockSpec((1, tq, MEM_W), lambda b, t: (b, t, 0))],
        out_specs=(pl.BlockSpec((1, tq, MEM_W), lambda b, t: (b, t, 0)),
                   pl.BlockSpec((1, M, 2 * MEM_W), lambda b, t: (b, 0, 0))),
        out_shape=(S((Bl, T, MEM_W), BF16), S((Bl, M, 2 * MEM_W), F32)),
        compiler_params=_cparams("parallel", "arbitrary"), name=name)(proj3, mkv3, mkv3, do3)


GROUP_ROWS = SWA_GROUP * WIN


def _group_rows(x, kvh):
    return jnp.concatenate([x[:, _hs(SWA_GROUP * kvh + g)] for g in range(SWA_GROUP)], axis=0)


def _group_col(vals):
    grp = lax.shift_right_logical(lax.broadcasted_iota(jnp.int32, (GROUP_ROWS, 1), 0), WIN.bit_length() - 1)
    col = jnp.full((GROUP_ROWS, 1), vals[-1], F32)
    for g in range(SWA_GROUP - 2, -1, -1):
        col = jnp.where(grp == g, vals[g], col)
    return col


def _swa_probs(qh, kph, kch, sink, slope, has_prev):
    qi = jnp.bitwise_and(lax.broadcasted_iota(jnp.int32, (GROUP_ROWS, WIN), 0), WIN - 1)
    kj = lax.broadcasted_iota(jnp.int32, (GROUP_ROWS, WIN), 1)
    scale = HEAD ** -0.5
    sp = lax.dot_general(qh, kph, NT, preferred_element_type=F32) * scale
    sc = lax.dot_general(qh, kch, NT, preferred_element_type=F32) * scale
    dist_p = (qi + WIN - kj).astype(F32)
    dist_c = (qi - kj).astype(F32)
    neg = -jnp.inf
    sp = jnp.where(kj > qi + jnp.where(has_prev, 0, WIN), sp - slope * dist_p, neg)
    sc = jnp.where(kj <= qi, sc - slope * dist_c, neg)
    m = jnp.maximum(jnp.maximum(jnp.max(sp, axis=-1, keepdims=True), jnp.max(sc, axis=-1, keepdims=True)), sink)
    ep = jnp.exp(sp - m)
    ec = jnp.exp(sc - m)
    es = jnp.exp(sink - m)
    inv = 1.0 / (jnp.sum(ep, axis=-1, keepdims=True) + jnp.sum(ec, axis=-1, keepdims=True) + es)
    return ep * inv, ec * inv, es * inv


def _swa_specs(nb):
    prev = lambda n: jnp.maximum(n - 1, 0)
    q = pl.BlockSpec((1, WIN, MIX_W), lambda b, n: (b, n, 0))
    kp = pl.BlockSpec((1, WIN, MEM_W), lambda b, n: (b, prev(n), 0))
    kc = pl.BlockSpec((1, WIN, MEM_W), lambda b, n: (b, n, 0))
    vp = pl.BlockSpec((1, WIN, MEM_W), lambda b, n: (b, prev(n), 1))
    vc = pl.BlockSpec((1, WIN, MEM_W), lambda b, n: (b, n, 1))
    sm = pl.BlockSpec(memory_space=pltpu.SMEM)
    return q, kp, kc, vp, vc, sm


def _swa_fwd(proj3, kv3, sinks, name):
    Bl, T, _ = proj3.shape
    nb = T // WIN
    q_s, kp_s, kc_s, vp_s, vc_s, sm = _swa_specs(nb)

    def body(q_ref, kp_ref, kc_ref, vp_ref, vc_ref, sink_ref, o_ref):
        has_prev = pl.program_id(1) > 0
        q = q_ref[0].astype(BF16)
        kp, kc = kp_ref[0].astype(BF16), kc_ref[0].astype(BF16)
        vp, vc = vp_ref[0].astype(BF16), vc_ref[0].astype(BF16)
        outs = []
        for kvh in range(SWA_HEADS // SWA_GROUP):
            kvs = _hs(kvh)
            heads = range(SWA_GROUP * kvh, SWA_GROUP * (kvh + 1))
            pp, pc, _ = _swa_probs(_group_rows(q, kvh), kp[:, kvs], kc[:, kvs], _group_col([sink_ref[h] for h in heads]),
                                   _group_col([SLOPES[h] for h in heads]), has_prev)
            og = (jnp.dot(pp.astype(BF16), vp[:, kvs], preferred_element_type=F32)
                  + jnp.dot(pc.astype(BF16), vc[:, kvs], preferred_element_type=F32))
            outs += [og[g * WIN:(g + 1) * WIN] for g in range(SWA_GROUP)]
        o_ref[0] = jnp.concatenate(outs, axis=-1).astype(BF16)

    return pl.pallas_call(
        body, grid=(Bl, nb), in_specs=[q_s, kp_s, kc_s, vp_s, vc_s, sm], out_specs=q_s,
        out_shape=S((Bl, T, MIX_W), BF16), compiler_params=_cparams("parallel", "parallel"), name=name)(
            proj3, kv3, kv3, kv3, kv3, sinks)


def _swa_bwd(proj3, kv3, sinks, do3, name):
    Bl, T, _ = proj3.shape
    nb = T // WIN
    q_s, kp_s, kc_s, vp_s, vc_s, sm = _swa_specs(nb)
    kv_s = pl.BlockSpec((1, WIN, 2 * MEM_W), lambda b, n: (b, n, 0))
    sk_s = pl.BlockSpec((8, LANES), lambda b, n: (0, 0))
    scale = HEAD ** -0.5

    def body(q_ref, kp_ref, kc_ref, vp_ref, vc_ref, sink_ref, do_ref, dq_ref, dkc_ref, dkp_ref, dsk_ref):
        has_prev = pl.program_id(1) > 0
        q = q_ref[0].astype(BF16)
        kp, kc = kp_ref[0].astype(BF16), kc_ref[0].astype(BF16)
        vp, vc = vp_ref[0].astype(BF16), vc_ref[0].astype(BF16)
        do = do_ref[0].astype(BF16)
        lane = lax.broadcasted_iota(jnp.int32, (8, LANES), 1)
        srow = lax.broadcasted_iota(jnp.int32, (8, LANES), 0)
        dsk = jnp.zeros((8, LANES), F32)
        dqs = []
        dkc, dkp, dvc, dvp = [], [], [], []
        grp = lax.shift_right_logical(lax.broadcasted_iota(jnp.int32, (GROUP_ROWS, 1), 0), WIN.bit_length() - 1)
        for kvh in range(SWA_HEADS // SWA_GROUP):
            kvs = _hs(kvh)
            heads = range(SWA_GROUP * kvh, SWA_GROUP * (kvh + 1))
            qg, dog = _group_rows(q, kvh), _group_rows(do, kvh)
            pp, pc, ps = _swa_probs(qg, kp[:, kvs], kc[:, kvs], _group_col([sink_ref[h] for h in heads]),
                                    _group_col([SLOPES[h] for h in heads]), has_prev)
            dpp = lax.dot_general(dog, vp[:, kvs], NT, preferred_element_type=F32)
            dpc = lax.dot_general(dog, vc[:, kvs], NT, preferred_element_type=F32)
            delta = jnp.sum(pp * dpp, axis=-1, keepdims=True) + jnp.sum(pc * dpc, axis=-1, keepdims=True)
            dsp = (pp * (dpp - delta) * scale).astype(BF16)
            dsc = (pc * (dpc - delta) * scale).astype(BF16)
            dqg = (jnp.dot(dsp, kp[:, kvs], preferred_element_type=F32)
                   + jnp.dot(dsc, kc[:, kvs], preferred_element_type=F32))
            dqs += [dqg[g * WIN:(g + 1) * WIN] for g in range(SWA_GROUP)]
            dkc.append(lax.dot_general(dsc, qg, TN, preferred_element_type=F32))
            dkp.append(lax.dot_general(dsp, qg, TN, preferred_element_type=F32))
            dvc.append(lax.dot_general(pc.astype(BF16), dog, TN, preferred_element_type=F32))
            dvp.append(lax.dot_general(pp.astype(BF16), dog, TN, preferred_element_type=F32))
            dsink = ps * delta
            for g, h in enumerate(heads):
                dsk = dsk + jnp.where((lane == h) & (srow == 0), -jnp.sum(jnp.where(grp == g, dsink, 0.0)), 0.0)
        dq_ref[0] = jnp.concatenate(dqs, axis=-1).astype(BF16)
        dkc_ref[0] = jnp.concatenate(dkc + dvc, axis=-1)
        dkp_ref[0] = jnp.concatenate(dkp + dvp, axis=-1)

        @pl.when((pl.program_id(0) == 0) & (pl.program_id(1) == 0))
        def _():
            dsk_ref[...] = jnp.zeros_like(dsk_ref)

        dsk_ref[...] += dsk

    return pl.pallas_call(
        body, grid=(Bl, nb), in_specs=[q_s, kp_s, kc_s, vp_s, vc_s, sm, q_s], out_specs=(q_s, kv_s, kv_s, sk_s),
        out_shape=(S((Bl, T, MIX_W), BF16), S((Bl, T, 2 * MEM_W), F32), S((Bl, T, 2 * MEM_W), F32), S((8, LANES), F32)),
        compiler_params=_cparams("arbitrary", "arbitrary"), name=name)(proj3, kv3, kv3, kv3, kv3, sinks, do3)


def _kv_grad_combine(parts, name):
    Bl, T, W = parts[0][0].shape
    nb = T // WIN
    nl = len(parts)

    def body(*refs):
        o_ref = refs[-1]
        has_next = jnp.where(pl.program_id(1) == nb - 1, 0.0, 1.0)
        acc = None
        for l in range(nl):
            c = refs[2 * l][0] + has_next * refs[2 * l + 1][0]
            acc = c if acc is None else acc + c
        o_ref[0] = acc.astype(BF16)

    cur = pl.BlockSpec((1, WIN, W), lambda b, n: (b, n, 0))
    nxt = pl.BlockSpec((1, WIN, W), lambda b, n: (b, jnp.minimum(n + 1, nb - 1), 0))
    return pl.pallas_call(
        body, grid=(Bl, nb), in_specs=[cur, nxt] * nl, out_specs=cur, out_shape=S((Bl, T, W), BF16),
        compiler_params=_cparams("parallel", "parallel"), name=name)(*[a for pr in parts for a in pr])


def _loss_bwd(y, target, name="loss"):
    N, D = y.shape
    tm = _tile(N, (512, 256, 128))

    def body(y_ref, t_ref, dy_ref, l_ref):
        e = y_ref[...] - t_ref[...]
        dy_ref[...] = e * (1.0 / D)

        @pl.when(pl.program_id(0) == 0)
        def _():
            l_ref[...] = jnp.zeros_like(l_ref)

        l_ref[...] += jnp.sum(e * e, axis=0, keepdims=True) * (0.5 / D)

    row = pl.BlockSpec((tm, D), lambda i: (i, 0))
    vec = pl.BlockSpec((1, D), lambda i: (0, 0))
    return pl.pallas_call(
        body, grid=(N // tm,), in_specs=[row, row], out_specs=(row, vec), out_shape=(S((N, D), F32), S((1, D), F32)),
        compiler_params=_cparams("arbitrary"), name=name)(y, target)


def _all_gather(x, name):
    R, C = x.shape

    def body(x_ref, out_ref, send_sems, recv_sems, local_sem):
        mx, my, mc = lax.axis_index("x"), lax.axis_index("y"), lax.axis_index("c")
        me, sibling = (mx, my, mc), (mx, my, 1 - mc)
        chips = [(1 - mx, my), (mx, 1 - my), (1 - mx, 1 - my)]

        def rows(px, py, pc):
            return out_ref.at[4 * px + 2 * py + pc]

        def copy(kk, block, to, src=None):
            return pltpu.make_async_remote_copy(
                src_ref=rows(*block) if src is None else src, dst_ref=rows(*block), send_sem=send_sems.at[kk],
                recv_sem=recv_sems.at[kk], device_id=to, device_id_type=MESH)

        mine = pltpu.make_async_copy(x_ref, rows(*me), local_sem)
        mine.start()
        first = [copy(0, me, sibling, src=x_ref)]
        first += [copy(1 + j, me, (*chip, mc), src=x_ref) for j, chip in enumerate(chips)]
        for cp in first:
            cp.start()
        passed = [copy(4 + j, (*chip, mc), sibling) for j, chip in enumerate(chips)]
        for j, chip in enumerate(chips):
            copy(1 + j, (*chip, mc), me).wait_recv()
            passed[j].start()
        copy(0, sibling, me).wait_recv()
        for j, chip in enumerate(chips):
            copy(4 + j, (*chip, 1 - mc), me).wait_recv()
        for cp in first + passed:
            cp.wait_send()
        mine.wait()

    return pl.pallas_call(
        body, out_shape=S((N_DEV, R, C), x.dtype), in_specs=[ANY], out_specs=ANY,
        scratch_shapes=[pltpu.SemaphoreType.DMA((7,)), pltpu.SemaphoreType.DMA((7,)), pltpu.SemaphoreType.DMA(())],
        name=name)(x)


def _ag_weights(shards, row_sharded, name):
    n = len(shards)

    def full_shape(a, rows):
        if rows:
            return a.shape[:-2] + (N_DEV * a.shape[-2],) + a.shape[-1:]
        return (N_DEV,) + a.shape

    def body(*refs):
        x_refs, o_refs = refs[:n], refs[n:2 * n]
        send_sems, recv_sems, local_sems = refs[2 * n:]
        mx, my, mc = lax.axis_index("x"), lax.axis_index("y"), lax.axis_index("c")
        me, sibling = (mx, my, mc), (mx, my, 1 - mc)
        chips = [(1 - mx, my), (mx, 1 - my), (1 - mx, 1 - my)]

        def dst(t, px, py, pc):
            d = 4 * px + 2 * py + pc
            if not row_sharded[t]:
                return o_refs[t].at[d]
            r = shards[t].shape[-2]
            idx = (slice(None),) * (shards[t].ndim - 2) + (pl.ds(pl.multiple_of(d * r, 16), r), slice(None))
            return o_refs[t].at[idx]

        def copy(kk, t, block, to, src=None):
            return pltpu.make_async_remote_copy(
                src_ref=dst(t, *block) if src is None else src, dst_ref=dst(t, *block),
                send_sem=send_sems.at[kk * n + t], recv_sem=recv_sems.at[kk * n + t], device_id=to,
                device_id_type=MESH)

        mine = [pltpu.make_async_copy(x_refs[t], dst(t, *me), local_sems.at[t]) for t in range(n)]
        for cp in mine:
            cp.start()
        first = []
        for t in range(n):
            first.append(copy(0, t, me, sibling, src=x_refs[t]))
            first += [copy(1 + j, t, me, (*chip, mc), src=x_refs[t]) for j, chip in enumerate(chips)]
        for cp in first:
            cp.start()
        passed = []
        for j, chip in enumerate(chips):
            for t in range(n):
                copy(1 + j, t, (*chip, mc), me).wait_recv()
                cp = copy(4 + j, t, (*chip, mc), sibling)
                cp.start()
                passed.append(cp)
        for t in range(n):
            copy(0, t, sibling, me).wait_recv()
            for j, chip in enumerate(chips):
                copy(4 + j, t, (*chip, 1 - mc), me).wait_recv()
        for cp in first + passed:
            cp.wait_send()
        for cp in mine:
            cp.wait()

    return pl.pallas_call(
        body, out_shape=tuple(S(full_shape(a, r), a.dtype) for a, r in zip(shards, row_sharded)),
        in_specs=[ANY] * n, out_specs=tuple([ANY] * n),
        scratch_shapes=[pltpu.SemaphoreType.DMA((7 * n,)), pltpu.SemaphoreType.DMA((7 * n,)),
                        pltpu.SemaphoreType.DMA((n,))],
        name=name)(*shards)


FLIPS = [(fx, fy, fc) for fx in (0, 1) for fy in (0, 1) for fc in (0, 1)][1:]
HBM = pl.BlockSpec(memory_space=pltpu.HBM)
SEM = pl.BlockSpec(memory_space=pltpu.SEMAPHORE)
EFFECT = pltpu.SideEffectType.DATAFLOW_SIDE_EFFECTING


def _hbm(a):
    return pltpu.with_memory_space_constraint(a, pltpu.HBM)


def _flips(gather):
    return [(0, 0, 0)] + FLIPS if gather else FLIPS


def _split_copies(gather, s_refs, l_refs, send_sems, recv_sems):
    n = len(s_refs)
    mx, my, mc = lax.axis_index("x"), lax.axis_index("y"), lax.axis_index("c")
    me = 4 * mx + 2 * my + mc
    copies = []
    for k, (fx, fy, fc) in enumerate(_flips(gather)):
        px, py, pc = (1 - mx if fx else mx), (1 - my if fy else my), (1 - mc if fc else mc)
        for t in range(n):
            if gather:
                src = s_refs[t]
                r = src.shape[0]
                dst = l_refs[t].at[pl.ds(pl.multiple_of(me * r, 16), r), :]
            else:
                src = s_refs[t].at[:, 4 * px + 2 * py + pc]
                dst = l_refs[t].at[k]
            copies.append(pltpu.make_async_remote_copy(
                src_ref=src, dst_ref=dst, send_sem=send_sems.at[k * n + t], recv_sem=recv_sems.at[k * n + t],
                device_id=(px, py, pc), device_id_type=MESH))
    return copies


def _split_start(gather, srcs, lands, after, name):
    n = len(srcs)
    n_sem = len(_flips(gather)) * n

    def body(*refs):
        s_refs, l_refs = refs[:n], refs[n:2 * n]
        send_sems, recv_sems = refs[2 * n + 1], refs[2 * n + 2]
        token = refs[-1]
        for cp in _split_copies(gather, s_refs, l_refs, send_sems, recv_sems):
            cp.start()
        token[...] = jnp.zeros_like(token)

    outs = pl.pallas_call(
        body, name=name,
        out_shape=(pltpu.SemaphoreType.DMA((n_sem,)), pltpu.SemaphoreType.DMA((n_sem,)))
        + tuple(pltpu.HBM(a.shape, a.dtype) for a in lands) + (S((8, LANES), F32),),
        in_specs=[HBM] * (2 * n) + [ANY],
        out_specs=(SEM, SEM) + (HBM,) * n + (pl.BlockSpec(memory_space=pltpu.VMEM),),
        input_output_aliases={n + i: 2 + i for i in range(n)},
        compiler_params=pltpu.CompilerParams(has_side_effects=EFFECT),
    )(*[_hbm(a) for a in srcs], *[_hbm(a) for a in lands], after)
    return outs[0], outs[1], list(srcs), list(outs[2:2 + n]), outs[-1]


def _split_wait(gather, send_sems, recv_sems, srcs, lands, after, name):
    n = len(srcs)

    def body(*refs):
        s_refs, l_refs = refs[:n], refs[n:2 * n]
        ssem, rsem = refs[2 * n], refs[2 * n + 1]
        copies = _split_copies(gather, s_refs, l_refs, ssem, rsem)
        for cp in copies:
            cp.wait_send()
        for cp in copies:
            cp.wait_recv()

    outs = pl.pallas_call(
        body, name=name, out_shape=tuple(pltpu.HBM(a.shape, a.dtype) for a in lands),
        in_specs=[HBM] * (2 * n) + [SEM, SEM, ANY], out_specs=(HBM,) * n,
        input_output_aliases={n + i: i for i in range(n)},
        compiler_params=pltpu.CompilerParams(has_side_effects=EFFECT),
    )(*[_hbm(a) for a in srcs], *lands, send_sems, recv_sems, after)
    return list(outs)


def _adamw_math(w, g, m, v):
    m = ADAM_B1 * m + (1.0 - ADAM_B1) * g
    v = ADAM_B2 * v + (1.0 - ADAM_B2) * (g * g)
    m_hat = m / (1.0 - ADAM_B1 ** ADAM_STEP)
    v_hat = v / (1.0 - ADAM_B2 ** ADAM_STEP)
    delta = -ADAM_LR * (m_hat / (jnp.sqrt(v_hat) + ADAM_EPS) + ADAM_WD * w)
    return delta, m, v


def _adamw_layers(owns, gots, w, m, v, name):
    L, B, C = w.shape
    per_row = 2 * L * len(FLIPS) * C * owns[0].dtype.itemsize
    tb = max([t for t in range(16, B + 1, 16) if B % t == 0 and (t * per_row <= 16 * 1024 * 1024 or t == 16)] or [B])
    me = (4 * lax.axis_index("x") + 2 * lax.axis_index("y") + lax.axis_index("c")).astype(jnp.int32).reshape(1)

    def body(me_ref, *refs):
        own_refs, got_refs = refs[:L], refs[L:2 * L]
        w_ref, m_ref, v_ref = refs[2 * L:2 * L + 3]
        g_out, d_out, m_out, v_out = refs[2 * L + 3:]
        layer = pl.program_id(0)
        for kk in range(L):
            @pl.when(layer == kk)
            def _():
                g = own_refs[kk][0].astype(F32)
                for s in range(len(FLIPS)):
                    g = g + got_refs[kk][s].astype(F32)
                d, mn, vn = _adamw_math(w_ref[...], g, m_ref[...], v_ref[...])
                g_out[...] = g
                d_out[...] = d
                m_out[...] = mn
                v_out[...] = vn

    def row(kk, layer, i):
        return jnp.where(layer == kk, i, 0)

    blk = pl.BlockSpec((1, tb, C), lambda layer, i, me_ref: (layer, i, 0))
    own_specs = [pl.BlockSpec((1, 1, tb, C), lambda layer, i, me_ref, kk=kk: (0, me_ref[0], row(kk, layer, i), 0))
                 for kk in range(L)]
    got_specs = [pl.BlockSpec((len(FLIPS), 1, tb, C), lambda layer, i, me_ref, kk=kk: (0, 0, row(kk, layer, i), 0))
                 for kk in range(L)]
    return pl.pallas_call(
        body,
        grid_spec=pltpu.PrefetchScalarGridSpec(
            num_scalar_prefetch=1, grid=(L, B // tb), in_specs=own_specs + got_specs + [blk, blk, blk],
            out_specs=(blk, blk, blk, blk)),
        out_shape=(S((L, B, C), F32),) * 4, compiler_params=_cparams("arbitrary", "arbitrary"), name=name)(
            me, *owns, *gots, w, m, v)


def _adamw_replicated(parts, w, m, v, name):
    R, C = w.shape
    rb = _tile(R, (512, 256, 128, 64, 32, 16, 8))

    def body(p_ref, w_ref, m_ref, v_ref, g_out, d_out, m_out, v_out):
        g = p_ref[0]
        for j in range(1, N_DEV):
            g = g + p_ref[j]
        d, mn, vn = _adamw_math(w_ref[...], g, m_ref[...], v_ref[...])
        g_out[...] = g
        d_out[...] = d
        m_out[...] = mn
        v_out[...] = vn

    blk = pl.BlockSpec((rb, C), lambda i: (i, 0))
    return pl.pallas_call(
        body, grid=(R // rb,), in_specs=[pl.BlockSpec((N_DEV, rb, C), lambda i: (0, i, 0)), blk, blk, blk],
        out_specs=(blk, blk, blk, blk), out_shape=(S((R, C), F32),) * 4, compiler_params=_cparams("parallel"),
        name=name)(parts, w, m, v)


def _pack(arrs, rows_mult, dtype):
    flat = jnp.concatenate([a.reshape(-1).astype(dtype) for a in arrs])
    n = flat.shape[0]
    per = rows_mult * LANES
    tot = -(-n // per) * per
    return jnp.pad(flat, (0, tot - n)).reshape(tot // LANES, LANES)


def _unpack(blob, shapes):
    flat = blob.reshape(-1)
    out, off = [], 0
    for shp in shapes:
        n = int(np.prod(shp))
        out.append(flat[off:off + n].reshape(shp))
        off += n
    return out


def _small_to_natural(g8):
    t = jnp.moveaxis(g8, 0, -2)
    return t.reshape(t.shape[:-2] + (N_DEV * t.shape[-1],))


def _small_to_cols(g):
    t = g.reshape(g.shape[:-1] + (N_DEV, g.shape[-1] // N_DEV))
    return jnp.moveaxis(t, -2, 0)


def _block_diag(w):
    nb, bs, _ = w.shape
    eye = jnp.eye(nb, dtype=w.dtype)
    return (eye[:, None, :, None] * w[:, :, None, :]).reshape(nb * bs, nb * bs)


def _diag_blocks(d, nb, bs):
    d4 = d.reshape(nb, bs, nb, bs)
    return jnp.stack([d4[i, :, i, :] for i in range(nb)])


def kernel(x, mem, g_mix_pre, g_mix_post, g_ffn_pre, g_ffn_post, g_mem, w_mem_kv, w_mix_out, w_ffn_up, w_ffn_conv, b_ffn_conv, w_ffn_down, w_in_a, w_conv_a, b_conv_a, w_rg_r, b_rg_r, w_rg_i, b_rg_i, lru_lambda, w_in_b, sinks_b, g_kv, w_kv, loss_target, m_g_mix_pre, m_g_mix_post, m_g_ffn_pre, m_g_ffn_post, m_g_mem, m_w_mem_kv, m_w_mix_out, m_w_ffn_up, m_w_ffn_conv, m_b_ffn_conv, m_w_ffn_down, m_w_in_a, m_w_conv_a, m_b_conv_a, m_w_rg_r, m_b_rg_r, m_w_rg_i, m_b_rg_i, m_lru_lambda, m_w_in_b, m_sinks_b, m_g_kv, m_w_kv, v_g_mix_pre, v_g_mix_post, v_g_ffn_pre, v_g_ffn_post, v_g_mem, v_w_mem_kv, v_w_mix_out, v_w_ffn_up, v_w_ffn_conv, v_b_ffn_conv, v_w_ffn_down, v_w_in_a, v_w_conv_a, v_b_conv_a, v_w_rg_r, v_b_rg_r, v_w_rg_i, v_b_rg_i, v_lru_lambda, v_w_in_b, v_sinks_b, v_g_kv, v_w_kv):
    w_loc = dict(g_mix_pre=g_mix_pre, g_mix_post=g_mix_post, g_ffn_pre=g_ffn_pre, g_ffn_post=g_ffn_post, g_mem=g_mem,
                 w_mem_kv=w_mem_kv, w_mix_out=w_mix_out, w_ffn_up=w_ffn_up, w_ffn_conv=w_ffn_conv,
                 b_ffn_conv=b_ffn_conv, w_ffn_down=w_ffn_down, w_in_a=w_in_a, w_conv_a=w_conv_a, b_conv_a=b_conv_a,
                 w_rg_r=w_rg_r, b_rg_r=b_rg_r, w_rg_i=w_rg_i, b_rg_i=b_rg_i, lru_lambda=lru_lambda, w_in_b=w_in_b,
                 sinks_b=sinks_b, g_kv=g_kv, w_kv=w_kv)
    m_loc = dict(g_mix_pre=m_g_mix_pre, g_mix_post=m_g_mix_post, g_ffn_pre=m_g_ffn_pre, g_ffn_post=m_g_ffn_post,
                 g_mem=m_g_mem, w_mem_kv=m_w_mem_kv, w_mix_out=m_w_mix_out, w_ffn_up=m_w_ffn_up,
                 w_ffn_conv=m_w_ffn_conv, b_ffn_conv=m_b_ffn_conv, w_ffn_down=m_w_ffn_down, w_in_a=m_w_in_a,
                 w_conv_a=m_w_conv_a, b_conv_a=m_b_conv_a, w_rg_r=m_w_rg_r, b_rg_r=m_b_rg_r, w_rg_i=m_w_rg_i,
                 b_rg_i=m_b_rg_i, lru_lambda=m_lru_lambda, w_in_b=m_w_in_b, sinks_b=m_sinks_b, g_kv=m_g_kv,
                 w_kv=m_w_kv)
    v_loc = dict(g_mix_pre=v_g_mix_pre, g_mix_post=v_g_mix_post, g_ffn_pre=v_g_ffn_pre, g_ffn_post=v_g_ffn_post,
                 g_mem=v_g_mem, w_mem_kv=v_w_mem_kv, w_mix_out=v_w_mix_out, w_ffn_up=v_w_ffn_up,
                 w_ffn_conv=v_w_ffn_conv, b_ffn_conv=v_b_ffn_conv, w_ffn_down=v_w_ffn_down, w_in_a=v_w_in_a,
                 w_conv_a=v_w_conv_a, b_conv_a=v_b_conv_a, w_rg_r=v_w_rg_r, b_rg_r=v_b_rg_r, w_rg_i=v_w_rg_i,
                 b_rg_i=v_b_rg_i, lru_lambda=v_lru_lambda, w_in_b=v_w_in_b, sinks_b=v_sinks_b, g_kv=v_g_kv,
                 w_kv=v_w_kv)

    Bl, T, D = x.shape
    Ml = mem.shape[1]
    N = Bl * T
    depth = g_mix_pre.shape[0]
    n_a = w_in_a.shape[0]
    F = w_ffn_down.shape[1] * N_DEV
    def as_rows(n, a):
        return jnp.swapaxes(a, -1, -2) if n in TRANSPOSED else a

    def mix_keys(l):
        keys = [("w_mem_kv", l), ("w_mix_out", l), ("w_in_a", l) if l < n_a else ("w_in_b", l - n_a)]
        return keys + ([("w_kv", None)] if l == n_a else [])

    def ffn_keys(l):
        return [("w_ffn_up", l), ("w_ffn_down", l)]

    def shard_of(key):
        n, i = key
        return as_rows(n, w_loc[n] if i is None else w_loc[n][i]).astype(BF16)

    W = {}
    keys0 = mix_keys(0)
    got0 = _ag_weights([shard_of(kk) for kk in keys0] + [w_loc[n] for n in SMALL_SHARDED],
                       [True] * len(keys0) + [False] * len(SMALL_SHARDED), name="ag_weights_0")
    W.update(zip(keys0, got0))
    for n, a in zip(SMALL_SHARDED, got0[len(keys0):]):
        W[n] = _small_to_natural(a)

    def gather_start(keys, after, tag):
        shards = [shard_of(kk) for kk in keys]
        lands = [lax.empty((N_DEV * s.shape[0],) + s.shape[1:], s.dtype) for s in shards]
        return (keys, tag) + _split_start(True, shards, lands, after, name=f"ag_start_{tag}")

    def gather_wait(pending, after):
        keys, tag, ssem, rsem, srcs, lands, _ = pending
        W.update(zip(keys, _split_wait(True, ssem, rsem, srcs, lands, after, name=f"ag_wait_{tag}")))

    pending_ffn = gather_start(ffn_keys(0), got0[0], "ffn_0")

    nblk, bsz = w_rg_r.shape[1], w_rg_r.shape[2]
    wbd = [jnp.concatenate([_block_diag(w_rg_r[j]), _block_diag(w_rg_i[j])], axis=1).astype(BF16) for j in range(n_a)]

    def vec(a):
        return a.reshape(1, -1)

    x2 = x.reshape(N, D)
    mem2 = mem.reshape(Bl * Ml, D)
    saved = []
    kvn = kv3 = x_kv = None
    xs = x2
    h1 = _rms_fwd(xs, vec(g_mix_pre[0]), BF16, name="rms_mixpre_0")
    for l in range(depth):
        sv = {"x0": xs}
        tok = None
        if l + 1 < depth:
            pending = gather_start(mix_keys(l + 1), pending_ffn[-1] if l == 0 else W[("w_mem_kv", l)], f"mix_{l + 1}")
            pending_next_ffn = gather_start(ffn_keys(l + 1), pending[-1], f"ffn_{l + 1}")
            tok = pending_next_ffn[-1]
        memn = _rms_fwd(mem2, vec(g_mem[l]), BF16, name=f"rms_mem_{l}")
        mkv3 = _mm(memn, W[("w_mem_kv", l)], after=tok, name=f"mm_memkv_{l}").reshape(Bl, Ml, 2 * MEM_W)
        if l < n_a:
            j = l
            proj = _mm(h1, W[("w_in_a", j)], tb=True, after=tok, name=f"mm_in_{l}")
            proj3 = proj.reshape(Bl, T, -1)
            xc3 = _conv_fwd_call(proj3, MIX_W, MIX_W, W["w_conv_a"][j], vec(W["b_conv_a"][j]), name=f"conv_a_{l}")
            gates3 = _mm(xc3.reshape(N, MIX_W), wbd[j], name=f"mm_gates_{l}").reshape(Bl, T, 2 * MIX_W)
            y_main3, hs3 = _rglru_fwd(xc3, gates3, proj3, vec(b_rg_r[j]), vec(b_rg_i[j]), vec(W["lru_lambda"][j]),
                                      name=f"rglru_fwd_{l}")
            q_off = 2 * MIX_W
            sv.update(xc3=xc3, gates3=gates3, hs3=hs3)
        else:
            j = l - n_a
            if l == n_a:
                x_kv = xs
                kv3 = _mm(kvn, W[("w_kv", None)], name="mm_kv").reshape(Bl, T, 2 * MEM_W)
            proj = _mm(h1, W[("w_in_b", j)], after=tok, name=f"mm_in_{l}")
            proj3 = proj.reshape(Bl, T, -1)
            y_main3 = _swa_fwd(proj3, kv3, sinks_b[j], name=f"swa_fwd_{l}")
            q_off = MIX_W
        y_mem3 = _mem_attn_fwd(proj3, q_off, mkv3, name=f"memattn_fwd_{l}")
        y_main = y_main3.reshape(N, MIX_W)
        y_mem = y_mem3.reshape(N, MEM_W)
        y = _mm_sum([(y_main, W[("w_mix_out", l)], (0, 0)), (y_mem, W[("w_mix_out", l)], (MIX_W, 0))], n=D,
                    name=f"mm_mixout_{l}")
        x1, h2 = _rms_pair_fwd(y, vec(g_mix_post[l]), xs, [vec(g_ffn_pre[l])], name=f"rms_mixpost_ffnpre_{l}")
        gather_wait(pending_ffn, h2)
        if l + 1 < depth:
            pending_ffn = pending_next_ffn
        u3 = _mm(h2, W[("w_ffn_up", l)], tb=True, name=f"mm_up_{l}").reshape(Bl, T, 2 * F)
        act3 = _ffn_mid_fwd(u3, W["w_ffn_conv"][l], vec(b_ffn_conv[l]), name=f"ffn_mid_fwd_{l}")
        act = act3.reshape(N, F)
        f = _mm(act, W[("w_ffn_down", l)], name=f"mm_down_{l}")
        sv.update(h1=h1, memn=memn, mkv3=mkv3, proj3=proj3, q_off=q_off, y_main=y_main, y_mem=y_mem, y=y, x1=x1,
                  h2=h2, u3=u3, act=act, f=f)
        saved.append(sv)
        if l + 1 < depth:
            gains = [vec(g_mix_pre[l + 1])] + ([vec(g_kv)] if l + 1 == n_a else [])
            xs, h1, *rest = _rms_pair_fwd(f, vec(g_ffn_post[l]), x1, gains, name=f"rms_ffnpost_mixpre_{l}")
            if rest:
                kvn = rest[0]
            gather_wait(pending, xs)
        else:
            xs = _rms_fwd(f, vec(g_ffn_post[l]), F32, res=x1, name=f"rms_ffnpost_{l}")

    dxs, loss_vec = _loss_bwd(xs, loss_target.reshape(N, D))
    loss = lax.psum(jnp.sum(loss_vec), ("x", "y", "c"))

    G = {n: [None] * w_loc[n].shape[0] for n in REPL + SMALL_SHARDED if n != "g_kv"}
    GW = {}

    def dw(key, off, a, b_, nm):
        GW[key] = _mm(a, b_, ta=True, out_dtype=BF16, into=(GW.get(key), (1,) + W[key].shape, 0, off), name=nm)

    def grad_blocks(key):
        g = GW[key]
        return g.reshape(1, N_DEV, g.shape[1] // N_DEV, g.shape[2])

    reduces = []

    def reduce_start(keys, after, tag):
        srcs = [grad_blocks(kk) for kk in keys]
        lands = [lax.empty((len(FLIPS),) + s.shape[:1] + s.shape[2:], s.dtype) for s in srcs]
        started = _split_start(False, srcs, lands, after, name=f"rs_start_{tag}")
        reduces.append((keys, tag) + started)
        return started[-1]

    kv_parts = []
    df = None
    for l in reversed(range(depth)):
        sv = saved[l]
        proj3 = sv["proj3"]
        if df is None:
            df, dg = _rms_bwd(sv["f"], vec(g_ffn_post[l]), dxs, out_dtype=BF16, name=f"rmsb_ffnpost_{l}")
            G["g_ffn_post"][l] = dg[0]
        dact = _mm(df, W[("w_ffn_down", l)], tb=True, name=f"mmb_down_dx_{l}")
        dw(("w_ffn_down", l), (0, 0), sv["act"], df, f"mmb_down_dw_{l}")
        dug3, duv3, dwg, dwv, dbg, dbv = _ffn_mid_bwd(sv["u3"], dact.reshape(Bl, T, F),
                                                      W["w_ffn_conv"][l], vec(b_ffn_conv[l]), name=f"ffn_mid_bwd_{l}")
        G["w_ffn_conv"][l] = jnp.concatenate([dwg, dwv], axis=1)
        G["b_ffn_conv"][l] = jnp.concatenate([dbg, dbv], axis=1)[0]
        dug, duv = dug3.reshape(N, F), duv3.reshape(N, F)
        dw(("w_ffn_up", l), (0, 0), dug, sv["h2"], f"mmb_up_dw_g_{l}")
        dw(("w_ffn_up", l), (F, 0), duv, sv["h2"], f"mmb_up_dw_v_{l}")
        tok = reduce_start([("w_ffn_down", l), ("w_ffn_up", l)], dug, f"ffn_{l}")
        dh2 = _mm_sum([(dug, W[("w_ffn_up", l)], (0, 0)), (duv, W[("w_ffn_up", l)], (F, 0))], n=D, after=tok,
                      name=f"mmb_up_dx_{l}")
        dx1, dy, dg, dg2 = _rms_pair_bwd(sv["x1"], vec(g_ffn_pre[l]), dh2, dxs, sv["y"], vec(g_mix_post[l]),
                                         name=f"rmsb_ffnpre_mixpost_{l}")
        G["g_ffn_pre"][l] = dg[0]
        G["g_mix_post"][l] = dg2[0]
        dy_main = _mm(dy, W[("w_mix_out", l)], tb=True, n=MIX_W, k=D, name=f"mmb_mixout_dmain_{l}")
        dy_mem = _mm(dy, W[("w_mix_out", l)], tb=True, n=MEM_W, k=D, b_off=(MIX_W, 0),
                     name=f"mmb_mixout_dmem_{l}")
        dw(("w_mix_out", l), (0, 0), sv["y_main"], dy, f"mmb_mixout_dw_main_{l}")
        dw(("w_mix_out", l), (MIX_W, 0), sv["y_mem"], dy, f"mmb_mixout_dw_mem_{l}")
        dq_mem3, dmkv3 = _mem_attn_bwd(proj3, sv["q_off"], sv["mkv3"], dy_mem.reshape(Bl, T, MEM_W),
                                       name=f"memattn_bwd_{l}")
        dq_mem = dq_mem3.reshape(N, MEM_W)
        dmkv = dmkv3.reshape(Bl * Ml, 2 * MEM_W)
        dw(("w_mem_kv", l), (0, 0), sv["memn"], dmkv, f"mmb_memkv_dw_{l}")
        dmemn = _mm(dmkv, W[("w_mem_kv", l)], tb=True, name=f"mmb_memkv_dx_{l}")
        _, dg = _rms_bwd(mem2, vec(g_mem[l]), dmemn, name=f"rmsb_mem_{l}")
        G["g_mem"][l] = dg[0]
        dy_main3 = dy_main.reshape(Bl, T, MIX_W)
        if l < n_a:
            j = l
            dxc3, drp3, dip3, dugate3, dbr, dbi, dlam = _rglru_bwd(
                dy_main3, sv["xc3"], sv["gates3"], proj3, sv["hs3"], vec(b_rg_r[j]), vec(b_rg_i[j]),
                vec(W["lru_lambda"][j]), name=f"rglru_bwd_{l}")
            G["b_rg_r"][j] = dbr.reshape(nblk, bsz)
            G["b_rg_i"][j] = dbi.reshape(nblk, bsz)
            G["lru_lambda"][j] = dlam[0]
            drp, dip = drp3.reshape(N, MIX_W), dip3.reshape(N, MIX_W)
            xc2 = sv["xc3"].reshape(N, MIX_W)
            G["w_rg_r"][j] = _diag_blocks(_mm(xc2, drp, ta=True, name=f"mmb_gates_dw_r_{l}"), nblk, bsz)
            G["w_rg_i"][j] = _diag_blocks(_mm(xc2, dip, ta=True, name=f"mmb_gates_dw_i_{l}"), nblk, bsz)
            dxc = _mm_sum([(drp, wbd[j], (0, 0)), (dip, wbd[j], (0, MIX_W))], tb=True, n=MIX_W,
                          add=dxc3.reshape(N, MIX_W), name=f"mmb_gates_dx_{l}")
            dux3, dwc, dbc = _conv_bwd_call(dxc.reshape(Bl, T, MIX_W), proj3, MIX_W, MIX_W, W["w_conv_a"][j],
                                            name=f"conv_a_bwd_{l}")
            G["w_conv_a"][j] = dwc
            G["b_conv_a"][j] = dbc[0]
            pieces = [(dugate3.reshape(N, MIX_W), 0), (dux3.reshape(N, MIX_W), MIX_W), (dq_mem, 2 * MIX_W)]
            in_key = ("w_in_a", j)
        else:
            j = l - n_a
            dq3, dkc, dkp, dsk = _swa_bwd(proj3, kv3, sinks_b[j], dy_main3, name=f"swa_bwd_{l}")
            kv_parts.append((dkc, dkp))
            G["sinks_b"][j] = dsk[0, :SWA_HEADS]
            pieces = [(dq3.reshape(N, MIX_W), 0), (dq_mem, MIX_W)]
            in_key = ("w_in_b", j)
        in_t = in_key[0] in TRANSPOSED
        for pi, (piece, off) in enumerate(pieces):
            if in_t:
                dw(in_key, (off, 0), piece, sv["h1"], f"mmb_in_dw_{pi}_{l}")
            else:
                dw(in_key, (0, off), sv["h1"], piece, f"mmb_in_dw_{pi}_{l}")
        tok = reduce_start([("w_mix_out", l), ("w_mem_kv", l), in_key], dy, f"mix_{l}")
        dh1 = _mm_sum([(piece, W[in_key], (off, 0) if in_t else (0, off)) for piece, off in pieces], tb=not in_t, n=D,
                      after=tok, name=f"mmb_in_dx_{l}")
        if l > 0 and l != n_a:
            dxs, df, dg, dg2 = _rms_pair_bwd(sv["x0"], vec(g_mix_pre[l]), dh1, dx1, saved[l - 1]["f"],
                                             vec(g_ffn_post[l - 1]), name=f"rmsb_mixpre_ffnpost_{l}")
            G["g_ffn_post"][l - 1] = dg2[0]
        else:
            dxs, dg = _rms_bwd(sv["x0"], vec(g_mix_pre[l]), dh1, add=dx1, name=f"rmsb_mixpre_{l}")
            df = None
        G["g_mix_pre"][l] = dg[0]
        if l == n_a:
            dkv = _kv_grad_combine(kv_parts, name="kv_grad_combine").reshape(N, 2 * MEM_W)
            dw(("w_kv", None), (0, 0), kvn, dkv, "mmb_kv_dw")
            tok = reduce_start([("w_kv", None)], dkv, "kv")
            dkvn = _mm(dkv, W[("w_kv", None)], tb=True, after=tok, name="mmb_kv_dx")
            dxs, dg = _rms_bwd(x_kv, vec(g_kv), dkvn, add=dxs, name="rmsb_kv")
            G["g_kv"] = dg[0]
    grad_x = dxs.reshape(Bl, T, D)
    Gf = {n: (jnp.stack(g) if isinstance(g, list) else g) for n, g in G.items()}

    small4 = []
    for n in SMALL_SHARDED:
        t = _small_to_cols(Gf[n]).astype(BF16)
        small4.append(t.reshape(1, N_DEV, -1, t.shape[-1]))
    small_lands = [lax.empty((len(FLIPS),) + s.shape[:1] + s.shape[2:], s.dtype) for s in small4]
    small_started = _split_start(False, small4, small_lands, dxs, name="rs_start_small")
    r_blob = _pack([Gf[n].astype(F32) for n in REPL], REPL_ROWS, F32)
    r_parts = _all_gather(r_blob, name="ag_repl_grads")
    parts = {}
    for keys, tag, ssem, rsem, srcs, lands, _ in reduces:
        for kk, s, g7 in zip(keys, srcs, _split_wait(False, ssem, rsem, srcs, lands, small_started[-1],
                                                     name=f"rs_wait_{tag}")):
            parts[kk] = (s, g7)

    res = [{} for _ in range(4)]
    for n, _ in SHARDED:
        if n in SMALL_SHARDED:
            continue
        idx = [None] if w_loc[n].ndim == 2 else list(range(w_loc[n].shape[0]))
        wmv = [as_rows(n, a[n]) for a in (w_loc, m_loc, v_loc)]
        shp3 = (len(idx),) + wmv[0].shape[-2:]
        outs = _adamw_layers([parts[(n, i)][0] for i in idx], [parts[(n, i)][1] for i in idx],
                             *[a.reshape(shp3) for a in wmv], name=f"adamw_{n}")
        for k in range(4):
            res[k][n] = as_rows(n, outs[k].reshape(wmv[0].shape))
    last = res[0]["w_kv"]
    small_got = _split_wait(False, *small_started[:4], last, name="rs_wait_small")
    for n, own, g7 in zip(SMALL_SHARDED, small4, small_got):
        shp3 = own.shape[:1] + own.shape[2:]
        outs = _adamw_layers([own], [g7], w_loc[n].reshape(shp3), m_loc[n].reshape(shp3), v_loc[n].reshape(shp3),
                             name=f"adamw_{n}")
        for k in range(4):
            res[k][n] = outs[k].reshape(w_loc[n].shape)
    outs_rp = _adamw_replicated(r_parts, _pack([w_loc[n] for n in REPL], REPL_ROWS, F32),
                                _pack([m_loc[n] for n in REPL], REPL_ROWS, F32),
                                _pack([v_loc[n] for n in REPL], REPL_ROWS, F32),
                                name="adamw_replicated")
    rp_shapes = [w_loc[n].shape for n in REPL]
    for k in range(4):
        res[k].update(zip(REPL, _unpack(outs_rp[k], rp_shapes)))
    out = [loss, grad_x]
    for k in range(4):
        out += [res[k][n] for n in WEIGHTS]
    return tuple(out)
```

```python
import functools
import math

import numpy as np
import jax
import jax.numpy as jnp
from jax import lax
from jax.experimental import pallas as pl
from jax.experimental.pallas import tpu as pltpu

F32 = jnp.float32
BF16 = jnp.bfloat16
S = jax.ShapeDtypeStruct
MESH = pl.DeviceIdType.MESH
ANY = pl.BlockSpec(memory_space=pl.ANY)

HEAD = 64
MEM_HEADS = 4
MEM_W = MEM_HEADS * HEAD
SWA_HEADS = 12
SWA_GROUP = 3
MIX_W = SWA_HEADS * HEAD
WIN = 128
LRU_C = 8.0
EPS = 1e-6
ADAM_LR, ADAM_B1, ADAM_B2, ADAM_EPS, ADAM_WD, ADAM_STEP = 0.001, 0.9, 0.999, 1e-08, 0.01, 10
GELU_C0 = math.sqrt(2.0 / math.pi)
GELU_C1 = 0.044715
N_DEV = 8
LANES = 128
CT = 128
VMEM_LIMIT = 48 * 1024 * 1024
MM_VMEM_BUDGET = 36 * 1024 * 1024
REPL_ROWS = 256

SHARDED = (("w_mem_kv", 1), ("w_mix_out", 1), ("w_ffn_up", 2), ("w_ffn_conv", 2), ("w_ffn_down", 1), ("w_in_a", 2),
           ("w_conv_a", 2), ("b_conv_a", 1), ("lru_lambda", 1), ("w_in_b", 1), ("w_kv", 0))
SMALL_SHARDED = ("w_ffn_conv", "w_conv_a", "b_conv_a", "lru_lambda")
TRANSPOSED = ("w_ffn_up", "w_in_a")
REPL = ("g_mix_pre", "g_mix_post", "g_ffn_pre", "g_ffn_post", "g_mem", "b_ffn_conv", "w_rg_r", "b_rg_r", "w_rg_i",
        "b_rg_i", "sinks_b", "g_kv")
WEIGHTS = ("g_mix_pre", "g_mix_post", "g_ffn_pre", "g_ffn_post", "g_mem", "w_mem_kv", "w_mix_out", "w_ffn_up",
           "w_ffn_conv", "b_ffn_conv", "w_ffn_down", "w_in_a", "w_conv_a", "b_conv_a", "w_rg_r", "b_rg_r", "w_rg_i",
           "b_rg_i", "lru_lambda", "w_in_b", "sinks_b", "g_kv", "w_kv")


def _alibi_slopes(n):
    def pow2(m):
        start = 2.0 ** (-8.0 / m)
        return [start ** (i + 1) for i in range(m)]
    c = 2 ** int(math.floor(math.log2(n)))
    s = pow2(c)
    if c != n:
        s = s + pow2(2 * c)[0::2][: n - c]
    return [float(v) for v in np.asarray(s, dtype=np.float32)]


SLOPES = _alibi_slopes(SWA_HEADS)


def _tile(n, cands):
    for c in cands:
        if n % c == 0:
            return c
    return n


def _cparams(*sem):
    return pltpu.CompilerParams(dimension_semantics=sem, vmem_limit_bytes=VMEM_LIMIT)


def _mm_tiles(M, N, K, a_bytes, b_bytes, o_bytes, add_bytes, offsets):
    m_off, n_offs, k_off = offsets
    tms = [c for c in (1024, 512, 256, 128) if M % c == 0 and m_off % c == 0] or [M]
    tns = [c for c in (1408, 1024, 896, 768, 512, 384, 256, 128)
           if N % c == 0 and all(o % c == 0 for o in n_offs)] or [N]
    tks = [c for c in (K, 2048, 1408, 1024, 512, 256, 128) if c <= K and K % c == 0 and k_off % c == 0]
    best = None
    for tk in tks:
        fits = []
        for tm in tms:
            for tn in tns:
                need = 2 * (tm * tk * a_bytes + tk * tn * b_bytes + tm * tn * (o_bytes + add_bytes))
                need += tm * tn * 4 * (2 if tk < K else 1)
                need += (tm * tk * 2 if a_bytes != 2 else 0) + (tk * tn * 2 if b_bytes != 2 else 0)
                if need <= MM_VMEM_BUDGET:
                    fits.append((tm * tn, min(tm, 512), tm, tn))
        if fits:
            _, _, tm, tn = max(fits)
            best = (tm, tn, tk)
            break
    assert best is not None, (M, N, K)
    return best


def _mm(a, b, *, ta=False, tb=False, n=None, k=None, b_off=(0, 0), out_dtype=F32, add=None, into=None, after=None,
        name="mm"):
    if ta:
        K, M = a.shape
    else:
        M, K = a.shape
    if tb:
        N = b.shape[-2] if n is None else n
    else:
        N = b.shape[-1] if n is None else n
    assert k is None or k == K
    ro, co = b_off
    n_off, k_off = (ro, co) if tb else (co, ro)
    oro, oco = (0, 0) if into is None else into[3]
    tm, tn, tk = _mm_tiles(M, N, K, a.dtype.itemsize, b.dtype.itemsize, jnp.dtype(out_dtype).itemsize,
                           0 if add is None else add.dtype.itemsize, (oro, (n_off, oco), k_off))
    nk = K // tk
    if tb:
        b_spec = pl.BlockSpec((tn, tk), lambda i, j, kk: (j + ro // tn, kk + co // tk))
        b_dims = (1,)
    else:
        b_spec = pl.BlockSpec((tk, tn), lambda i, j, kk: (kk + ro // tk, j + co // tn))
        b_dims = (0,)
    if ta:
        a_spec = pl.BlockSpec((tk, tm), lambda i, j, kk: (kk, i))
        a_dims = (0,)
    else:
        a_spec = pl.BlockSpec((tm, tk), lambda i, j, kk: (i, kk))
        a_dims = (1,)
    dims = ((a_dims, b_dims), ((), ()))
    add_spec = pl.BlockSpec((tm, tn), lambda i, j, kk: (i, j))
    has_add = add is not None
    if into is None:
        o_spec, o_shape, buf = add_spec, (M, N), None
    else:
        buf, o_shape, ol, _ = into
        assert not has_add
        o_spec = pl.BlockSpec((None, tm, tn), lambda i, j, kk: (ol, i + oro // tm, j + oco // tn))
    has_buf = buf is not None

    def body(*refs):
        refs = list(refs)
        acc_ref = refs.pop() if nk > 1 else None
        o_ref = refs.pop()
        a_ref, b_ref = refs[0], refs[1]
        add_ref = refs[2] if has_add else None
        part = lax.dot_general(a_ref[...].astype(BF16), b_ref[...].astype(BF16), dims, preferred_element_type=F32)

        def finish(r):
            if has_add:
                r = r + add_ref[...].astype(F32)
            o_ref[...] = r.astype(out_dtype)

        if nk == 1:
            finish(part)
        else:
            kk = pl.program_id(2)

            @pl.when(kk == 0)
            def _():
                acc_ref[...] = part

            @pl.when(kk > 0)
            def _():
                acc_ref[...] += part

            @pl.when(kk == nk - 1)
            def _():
                finish(acc_ref[...])

    in_specs = [a_spec, b_spec] + ([add_spec] if has_add else []) + ([ANY] if has_buf else [])
    args = (a, b) + ((add,) if has_add else ()) + ((buf,) if has_buf else ())
    if after is not None:
        in_specs, args = in_specs + [ANY], args + (after,)
    return pl.pallas_call(
        body, grid=(M // tm, N // tn, nk), in_specs=in_specs, out_specs=o_spec,
        out_shape=S(o_shape, out_dtype), scratch_shapes=[pltpu.VMEM((tm, tn), F32)] if nk > 1 else [],
        input_output_aliases={2: 0} if has_buf else {},
        compiler_params=_cparams("parallel", "parallel", "arbitrary"), name=name)(*args)


def _mm_sum(pieces, *, tb=False, n, out_dtype=F32, add=None, after=None, name="mm_sum"):
    M = pieces[0][0].shape[0]
    ks = [a.shape[1] for a, _, _ in pieces]
    a_bytes = max(a.dtype.itemsize for a, _, _ in pieces)
    b_bytes = max(b.dtype.itemsize for _, b, _ in pieces)
    n_offs = tuple(off[0] if tb else off[1] for _, _, off in pieces)
    for kp, (_, _, off) in zip(ks, pieces):
        assert (off[1] if tb else off[0]) % kp == 0
    tm, tn, tk = _mm_tiles(M, n, sum(ks), a_bytes, b_bytes, jnp.dtype(out_dtype).itemsize, 0, (0, n_offs, 0))
    assert tk == sum(ks)
    a_specs = [pl.BlockSpec((tm, kp), lambda i, j: (i, 0)) for kp in ks]
    if tb:
        b_specs = [pl.BlockSpec((tn, kp), lambda i, j, ro=off[0], co=off[1], kp=kp: (j + ro // tn, co // kp))
                   for kp, (_, _, off) in zip(ks, pieces)]
        dims = NT
    else:
        b_specs = [pl.BlockSpec((kp, tn), lambda i, j, ro=off[0], co=off[1], kp=kp: (ro // kp, j + co // tn))
                   for kp, (_, _, off) in zip(ks, pieces)]
        dims = (((1,), (0,)), ((), ()))
    npc = len(pieces)
    o_spec = pl.BlockSpec((tm, tn), lambda i, j: (i, j))

    def body(*refs):
        o_ref = refs[2 * npc + (add is not None) + (after is not None)]
        acc = refs[2 * npc][...].astype(F32) if add is not None else None
        for p in range(npc):
            part = lax.dot_general(refs[p][...].astype(BF16), refs[npc + p][...].astype(BF16), dims,
                                   preferred_element_type=F32)
            acc = part if acc is None else acc + part
        o_ref[...] = acc.astype(out_dtype)

    args = [a for a, _, _ in pieces] + [b for _, b, _ in pieces]
    in_specs = a_specs + b_specs
    if add is not None:
        in_specs, args = in_specs + [o_spec], args + [add]
    if after is not None:
        in_specs, args = in_specs + [ANY], args + [after]
    return pl.pallas_call(
        body, grid=(M // tm, n // tn), in_specs=in_specs, out_specs=o_spec,
        out_shape=S((M, n), out_dtype), compiler_params=_cparams("parallel", "parallel"), name=name)(*args)


def _rms_fwd(x, g, out_dtype, res=None, name="rms_fwd"):
    N, D = x.shape
    tm = _tile(N, (512, 256, 128))
    has_res = res is not None

    def body(*refs):
        if has_res:
            x_ref, g_ref, r_ref, o_ref = refs
        else:
            x_ref, g_ref, o_ref = refs
        xv = x_ref[...].astype(F32)
        y = xv * lax.rsqrt(jnp.mean(xv * xv, axis=-1, keepdims=True) + EPS) * g_ref[...]
        if has_res:
            y = y + r_ref[...]
        o_ref[...] = y.astype(out_dtype)

    row = pl.BlockSpec((tm, D), lambda i: (i, 0))
    vec = pl.BlockSpec((1, D), lambda i: (0, 0))
    return pl.pallas_call(
        body, grid=(N // tm,), in_specs=[row, vec] + ([row] if has_res else []), out_specs=row,
        out_shape=S((N, D), out_dtype), compiler_params=_cparams("parallel"), name=name)(
            *((x, g) + ((res,) if has_res else ())))


def _rms_bwd(x, g, dy, add=None, out_dtype=F32, name="rms_bwd"):
    N, D = x.shape
    tm = _tile(N, (512, 256, 128))
    has_add = add is not None

    def body(*refs):
        if has_add:
            x_ref, g_ref, dy_ref, add_ref, dx_ref, dg_ref = refs
        else:
            x_ref, g_ref, dy_ref, dx_ref, dg_ref = refs
        xv = x_ref[...].astype(F32)
        dyv = dy_ref[...].astype(F32)
        r = lax.rsqrt(jnp.mean(xv * xv, axis=-1, keepdims=True) + EPS)
        u = dyv * g_ref[...]
        dx = r * u - xv * (r * r * r * jnp.mean(u * xv, axis=-1, keepdims=True))
        if has_add:
            dx = dx + add_ref[...]
        dx_ref[...] = dx.astype(out_dtype)

        @pl.when(pl.program_id(0) == 0)
        def _():
            dg_ref[...] = jnp.zeros_like(dg_ref)

        dg_ref[...] += jnp.sum(dyv * xv * r, axis=0, keepdims=True)

    row = pl.BlockSpec((tm, D), lambda i: (i, 0))
    vec = pl.BlockSpec((1, D), lambda i: (0, 0))
    return pl.pallas_call(
        body, grid=(N // tm,), in_specs=[row, vec, row] + ([row] if has_add else []), out_specs=(row, vec),
        out_shape=(S((N, D), out_dtype), S((1, D), F32)), compiler_params=_cparams("arbitrary"), name=name)(
            *((x, g, dy) + ((add,) if has_add else ())))


def _rms_pair_fwd(y, g_post, res, gains, name):
    N, D = y.shape
    tm = _tile(N, (512, 256, 128))
    ng = len(gains)

    def body(*refs):
        y_ref, gp_ref, r_ref = refs[:3]
        g_refs = refs[3:3 + ng]
        x_ref = refs[3 + ng]
        h_refs = refs[4 + ng:]
        yv = y_ref[...]
        x = r_ref[...] + yv * lax.rsqrt(jnp.mean(yv * yv, axis=-1, keepdims=True) + EPS) * gp_ref[...]
        x_ref[...] = x
        xn = x * lax.rsqrt(jnp.mean(x * x, axis=-1, keepdims=True) + EPS)
        for g_ref, h_ref in zip(g_refs, h_refs):
            h_ref[...] = (xn * g_ref[...]).astype(BF16)

    row = pl.BlockSpec((tm, D), lambda i: (i, 0))
    vec = pl.BlockSpec((1, D), lambda i: (0, 0))
    return pl.pallas_call(
        body, grid=(N // tm,), in_specs=[row, vec, row] + [vec] * ng, out_specs=(row,) * (1 + ng),
        out_shape=(S((N, D), F32),) + (S((N, D), BF16),) * ng, compiler_params=_cparams("parallel"), name=name)(
            y, g_post, res, *gains)


def _rms_pair_bwd(xa, ga, dya, add, xb, gb, name):
    N, D = xa.shape
    tm = _tile(N, (512, 256, 128))

    def one(xv, g_ref, dyv):
        r = lax.rsqrt(jnp.mean(xv * xv, axis=-1, keepdims=True) + EPS)
        u = dyv * g_ref[...]
        dx = r * u - xv * (r * r * r * jnp.mean(u * xv, axis=-1, keepdims=True))
        return dx, jnp.sum(dyv * xv * r, axis=0, keepdims=True)

    def body(xa_ref, ga_ref, dya_ref, add_ref, xb_ref, gb_ref, da_ref, db_ref, dga_ref, dgb_ref):
        da, dga = one(xa_ref[...].astype(F32), ga_ref, dya_ref[...].astype(F32))
        da = da + add_ref[...]
        da_ref[...] = da
        db, dgb = one(xb_ref[...].astype(F32), gb_ref, da)
        db_ref[...] = db.astype(BF16)

        @pl.when(pl.program_id(0) == 0)
        def _():
            dga_ref[...] = jnp.zeros_like(dga_ref)
            dgb_ref[...] = jnp.zeros_like(dgb_ref)

        dga_ref[...] += dga
        dgb_ref[...] += dgb

    row = pl.BlockSpec((tm, D), lambda i: (i, 0))
    vec = pl.BlockSpec((1, D), lambda i: (0, 0))
    return pl.pallas_call(
        body, grid=(N // tm,), in_specs=[row, vec, row, row, row, vec], out_specs=(row, row, vec, vec),
        out_shape=(S((N, D), F32), S((N, D), BF16), S((1, D), F32), S((1, D), F32)),
        compiler_params=_cparams("arbitrary"), name=name)(xa, ga, dya, add, xb, gb)


def _shift_down(x, s, row):
    return jnp.where(row >= s, pltpu.roll(x, s, axis=0), 0.0)


def _shift_up(x, s, row):
    T = x.shape[0]
    return jnp.where(row < T - s, pltpu.roll(x, T - s, axis=0), 0.0)


SLAB = 16


def _conv_wrap(x, w_ref, b_ref):
    W = w_ref.shape[0]
    y = x * w_ref[W - 1:W, :] + b_ref[...]
    for s in range(1, W):
        y = y + pltpu.roll(x, s, axis=0) * w_ref[W - 1 - s:W - s, :]
    return y


def _conv_rows(x_ref, w_ref, b_ref, lo, hi):
    W = w_ref.shape[0]
    y = x_ref[lo:hi, :] * w_ref[W - 1:W, :] + b_ref[...]
    for s in range(1, W):
        y = y + x_ref[lo - s:hi - s, :] * w_ref[W - 1 - s:W - s, :]
    return y


def _taps(x_ref, W):
    T = x_ref.shape[0]
    head = x_ref[0:SLAB, :]
    row = lax.broadcasted_iota(jnp.int32, head.shape, 0)
    return [x_ref[...]] + [jnp.concatenate([_shift_down(head, s, row), x_ref[SLAB - s:T - s, :]], axis=0)
                           for s in range(1, W)]


def _conv_taps(xs, w_ref, b_ref):
    W = w_ref.shape[0]
    y = xs[0] * w_ref[W - 1:W, :] + b_ref[...]
    for s in range(1, W):
        y = y + xs[s] * w_ref[W - 1 - s:W - s, :]
    return y


def _conv_head(x_head, w_ref, b_ref):
    row = lax.broadcasted_iota(jnp.int32, x_head.shape, 0)
    return _conv_taps([x_head] + [_shift_down(x_head, s, row) for s in range(1, w_ref.shape[0])], w_ref, b_ref)


def _conv_bwd_taps(dy, xs, w_ref, row):
    W = w_ref.shape[0]
    dx = dy * w_ref[W - 1:W, :]
    dws = [None] * W
    dws[W - 1] = jnp.sum(dy * xs[0], axis=0, keepdims=True)
    for s in range(1, W):
        dx = dx + _shift_up(dy, s, row) * w_ref[W - 1 - s:W - s, :]
        dws[W - 1 - s] = jnp.sum(dy * xs[s], axis=0, keepdims=True)
    return dx, jnp.concatenate(dws, axis=0), jnp.sum(dy, axis=0, keepdims=True)


def _conv_bwd_wrap(dy, x, w_ref):
    W = w_ref.shape[0]
    T = dy.shape[0]
    dx = dy * w_ref[W - 1:W, :]
    dws = [None] * W
    dws[W - 1] = jnp.sum(dy * x, axis=0, keepdims=True)
    for s in range(1, W):
        up = pltpu.roll(dy, T - s, axis=0)
        dx = dx + up * w_ref[W - 1 - s:W - s, :]
        dws[W - 1 - s] = jnp.sum(up * x, axis=0, keepdims=True)
    return dx, jnp.concatenate(dws, axis=0), jnp.sum(dy, axis=0, keepdims=True)


def _conv_bwd_fix(dy_head, dy_tail, x_tail, w_ref):
    row = lax.broadcasted_iota(jnp.int32, dy_tail.shape, 0)
    W = w_ref.shape[0]
    dx = dy_tail * w_ref[W - 1:W, :]
    extra = [jnp.zeros((1, dy_tail.shape[1]), F32)] * W
    for s in range(1, W):
        dx = dx + _shift_up(dy_tail, s, row) * w_ref[W - 1 - s:W - s, :]
        extra[W - 1 - s] = jnp.sum(jnp.where(row < s, dy_head * pltpu.roll(x_tail, s, axis=0), 0.0), axis=0,
                                   keepdims=True)
    return dx, jnp.concatenate(extra, axis=0)


def _gelu(g):
    t = jnp.tanh(GELU_C0 * (g + GELU_C1 * g * g * g))
    return 0.5 * g * (1.0 + t), t


def _dgelu(g, t):
    return 0.5 * (1.0 + t) + 0.5 * g * (1.0 - t * t) * (GELU_C0 * (1.0 + 3.0 * GELU_C1 * g * g))


def _cspec(T, off=0, ct=CT):
    return pl.BlockSpec((1, T, ct), lambda j, b: (b, 0, j + off))


def _pspec(rows, off=0, ct=CT):
    return pl.BlockSpec((rows, ct), lambda j, b: (0, j + off))


def _conv_fwd_call(x3, x_off, C, w, b, name):
    Bl, T, _ = x3.shape
    W = w.shape[0]

    def body(x_ref, w_ref, b_ref, o_ref):
        o_ref[0] = _conv_wrap(x_ref[0], w_ref, b_ref)
        o_ref[0, 0:SLAB, :] = _conv_head(x_ref[0, 0:SLAB, :], w_ref, b_ref)

    return pl.pallas_call(
        body, grid=(C // CT, Bl), in_specs=[_cspec(T, x_off // CT), _pspec(W), _pspec(1)], out_specs=_cspec(T),
        out_shape=S((Bl, T, C), F32), compiler_params=_cparams("parallel", "arbitrary"), name=name)(x3, w, b)


def _conv_bwd_call(dy3, x3, x_off, C, w, name):
    Bl, T, _ = x3.shape
    W = w.shape[0]

    def body(dy_ref, x_ref, w_ref, dx_ref, dw_ref, db_ref):
        dx, dw, db = _conv_bwd_wrap(dy_ref[0], x_ref[0], w_ref)
        dx_tail, dw_extra = _conv_bwd_fix(dy_ref[0, 0:SLAB, :], dy_ref[0, T - SLAB:T, :], x_ref[0, T - SLAB:T, :],
                                          w_ref)
        dx_ref[0] = dx.astype(BF16)
        dx_ref[0, T - SLAB:T, :] = dx_tail.astype(BF16)

        @pl.when(pl.program_id(1) == 0)
        def _():
            dw_ref[...] = jnp.zeros_like(dw_ref)
            db_ref[...] = jnp.zeros_like(db_ref)

        dw_ref[...] += dw - dw_extra
        db_ref[...] += db

    return pl.pallas_call(
        body, grid=(C // CT, Bl), in_specs=[_cspec(T), _cspec(T, x_off // CT), _pspec(W)],
        out_specs=(_cspec(T), _pspec(W), _pspec(1)),
        out_shape=(S((Bl, T, C), BF16), S((W, C), F32), S((1, C), F32)),
        compiler_params=_cparams("parallel", "arbitrary"), name=name)(dy3, x3, w)


def _ffn_mid_fwd(u3, wc, bc, name):
    Bl, T, F2 = u3.shape
    F = F2 // 2
    nf = F // CT

    def body(ug_ref, uv_ref, wg_ref, wv_ref, bg_ref, bv_ref, o_ref):
        g = _conv_rows(ug_ref.at[0], wg_ref, bg_ref, SLAB, T)
        v = _conv_rows(uv_ref.at[0], wv_ref, bv_ref, SLAB, T)
        o_ref[0, SLAB:T, :] = (_gelu(g)[0] * v).astype(BF16)
        g = _conv_head(ug_ref[0, 0:SLAB, :], wg_ref, bg_ref)
        v = _conv_head(uv_ref[0, 0:SLAB, :], wv_ref, bv_ref)
        o_ref[0, 0:SLAB, :] = (_gelu(g)[0] * v).astype(BF16)

    return pl.pallas_call(
        body, grid=(nf, Bl),
        in_specs=[_cspec(T), _cspec(T, nf), _pspec(3), _pspec(3, nf), _pspec(1), _pspec(1, nf)], out_specs=_cspec(T),
        out_shape=S((Bl, T, F), BF16), compiler_params=_cparams("parallel", "arbitrary"), name=name)(
            u3, u3, wc, wc, bc, bc)


def _ffn_mid_bwd(u3, dact3, wc, bc, name):
    Bl, T, F2 = u3.shape
    F = F2 // 2
    nf = F // CT

    def body(ug_ref, uv_ref, da_ref, wg_ref, wv_ref, bg_ref, bv_ref, dug_ref, duv_ref, dwg_ref, dwv_ref, dbg_ref,
             dbv_ref):
        row = lax.broadcasted_iota(jnp.int32, (T, CT), 0)
        ugs = _taps(ug_ref.at[0], 3)
        uvs = _taps(uv_ref.at[0], 3)
        g = _conv_taps(ugs, wg_ref, bg_ref)
        v = _conv_taps(uvs, wv_ref, bv_ref)
        da = da_ref[0]
        gel, t = _gelu(g)
        dg = da * v * _dgelu(g, t)
        dv = da * gel
        dug, dwg, dbg = _conv_bwd_taps(dg, ugs, wg_ref, row)
        duv, dwv, dbv = _conv_bwd_taps(dv, uvs, wv_ref, row)
        dug_ref[0] = dug.astype(BF16)
        duv_ref[0] = duv.astype(BF16)

        @pl.when(pl.program_id(1) == 0)
        def _():
            dwg_ref[...] = jnp.zeros_like(dwg_ref)
            dwv_ref[...] = jnp.zeros_like(dwv_ref)
            dbg_ref[...] = jnp.zeros_like(dbg_ref)
            dbv_ref[...] = jnp.zeros_like(dbv_ref)

        dwg_ref[...] += dwg
        dwv_ref[...] += dwv
        dbg_ref[...] += dbg
        dbv_ref[...] += dbv

    return pl.pallas_call(
        body, grid=(nf, Bl),
        in_specs=[_cspec(T), _cspec(T, nf), _cspec(T), _pspec(3), _pspec(3, nf), _pspec(1), _pspec(1, nf)],
        out_specs=(_cspec(T), _cspec(T), _pspec(3), _pspec(3), _pspec(1), _pspec(1)),
        out_shape=(S((Bl, T, F), BF16), S((Bl, T, F), BF16), S((3, F), F32), S((3, F), F32), S((1, F), F32),
                   S((1, F), F32)),
        compiler_params=_cparams("parallel", "arbitrary"), name=name)(u3, u3, dact3, wc, wc, bc, bc)


def _lru_gates(xc, rp, ip, br_ref, bi_ref, lam_ref):
    r = jax.nn.sigmoid(rp + br_ref[...])
    i = jax.nn.sigmoid(ip + bi_ref[...])
    lam = lam_ref[...]
    sp = jnp.maximum(-lam, 0.0) + jnp.log1p(jnp.exp(-jnp.abs(lam)))
    log_a = (-LRU_C) * r * sp
    a = jnp.exp(log_a)
    z = 2.0 * log_a
    one_m_a2 = jnp.where(z > -0.05, -z * (1.0 + z * (0.5 + z * (1.0 / 6.0 + z * (1.0 / 24.0)))), 1.0 - a * a)
    mult = jnp.sqrt(one_m_a2)
    return r, i, sp, a, mult


SCAN_CHUNK = 64


def _scan_down(a, b):
    T = a.shape[0]
    ch = min(SCAN_CHUNK, T)
    row = lax.broadcasted_iota(jnp.int32, (ch, a.shape[1]), 0)
    outs, carry = [], None
    for c in range(T // ch):
        ac, bc = a[c * ch:(c + 1) * ch], b[c * ch:(c + 1) * ch]
        s = 1
        while s < ch:
            a_sh = jnp.where(row >= s, pltpu.roll(ac, s, axis=0), 1.0)
            bc = ac * _shift_down(bc, s, row) + bc
            ac = ac * a_sh
            s *= 2
        if carry is not None:
            bc = bc + ac * carry
        carry = bc[ch - 1:ch, :]
        outs.append(bc)
    return jnp.concatenate(outs, axis=0)


def _scan_up(an, g):
    T = an.shape[0]
    ch = min(SCAN_CHUNK, T)
    row = lax.broadcasted_iota(jnp.int32, (ch, an.shape[1]), 0)
    outs, carry = [], None
    for c in reversed(range(T // ch)):
        ac, gc = an[c * ch:(c + 1) * ch], g[c * ch:(c + 1) * ch]
        s = 1
        while s < ch:
            a_sh = jnp.where(row < ch - s, pltpu.roll(ac, ch - s, axis=0), 1.0)
            gc = ac * _shift_up(gc, s, row) + gc
            ac = ac * a_sh
            s *= 2
        if carry is not None:
            gc = gc + ac * carry
        carry = gc[0:1, :]
        outs.append(gc)
    return jnp.concatenate(outs[::-1], axis=0)


def _rglru_fwd(xc3, gates3, proj3, br, bi, lam, name):
    Bl, T, C = xc3.shape

    def body(xc_ref, rp_ref, ip_ref, ug_ref, br_ref, bi_ref, lam_ref, y_ref, h_ref):
        xc = xc_ref[0]
        r, i, sp, a, mult = _lru_gates(xc, rp_ref[0], ip_ref[0], br_ref, bi_ref, lam_ref)
        h = _scan_down(a, mult * (i * xc))
        h_ref[0] = h
        y_ref[0] = (h * _gelu(ug_ref[0])[0]).astype(BF16)

    return pl.pallas_call(
        body, grid=(C // CT, Bl),
        in_specs=[_cspec(T), _cspec(T), _cspec(T, C // CT), _cspec(T), _pspec(1), _pspec(1), _pspec(1)],
        out_specs=(_cspec(T), _cspec(T)), out_shape=(S((Bl, T, C), BF16), S((Bl, T, C), F32)),
        compiler_params=_cparams("parallel", "arbitrary"), name=name)(xc3, gates3, gates3, proj3, br, bi, lam)


def _rglru_bwd(dy3, xc3, gates3, proj3, h3, br, bi, lam, name):
    Bl, T, C = xc3.shape

    def body(dy_ref, xc_ref, rp_ref, ip_ref, ug_ref, h_ref, br_ref, bi_ref, lam_ref,
             dxc_ref, drp_ref, dip_ref, dug_ref, dbr_ref, dbi_ref, dlam_ref):
        row = lax.broadcasted_iota(jnp.int32, (T, CT), 0)
        xc = xc_ref[0]
        r, i, sp, a, mult = _lru_gates(xc, rp_ref[0], ip_ref[0], br_ref, bi_ref, lam_ref)
        h = h_ref[0]
        dy = dy_ref[0]
        ug = ug_ref[0]
        gel, t = _gelu(ug)
        dug_ref[0] = (dy * h * _dgelu(ug, t)).astype(BF16)
        gacc = _scan_up(_shift_up(a, 1, row), dy * gel)
        da = gacc * _shift_down(h, 1, row)
        ix = i * xc
        d_mult = gacc * ix
        d_i = gacc * mult * xc
        dxc_ref[0] = gacc * mult * i
        d_log_a = da * a - d_mult * (a * a) / mult
        d_r = d_log_a * ((-LRU_C) * sp)
        d_sp = jnp.sum(d_log_a * ((-LRU_C) * r), axis=0, keepdims=True)
        drp = d_r * r * (1.0 - r)
        dip = d_i * i * (1.0 - i)
        drp_ref[0] = drp.astype(BF16)
        dip_ref[0] = dip.astype(BF16)

        @pl.when(pl.program_id(1) == 0)
        def _():
            dbr_ref[...] = jnp.zeros_like(dbr_ref)
            dbi_ref[...] = jnp.zeros_like(dbi_ref)
            dlam_ref[...] = jnp.zeros_like(dlam_ref)

        dbr_ref[...] += jnp.sum(drp, axis=0, keepdims=True)
        dbi_ref[...] += jnp.sum(dip, axis=0, keepdims=True)
        dlam_ref[...] += d_sp * (-jax.nn.sigmoid(-lam_ref[...]))

    vec = S((1, C), F32)
    act = S((Bl, T, C), BF16)
    return pl.pallas_call(
        body, grid=(C // CT, Bl),
        in_specs=[_cspec(T), _cspec(T), _cspec(T), _cspec(T, C // CT), _cspec(T), _cspec(T)] + [_pspec(1)] * 3,
        out_specs=(_cspec(T), _cspec(T), _cspec(T), _cspec(T), _pspec(1), _pspec(1), _pspec(1)),
        out_shape=(S((Bl, T, C), F32), act, act, act, vec, vec, vec),
        compiler_params=_cparams("parallel", "arbitrary"), name=name)(dy3, xc3, gates3, gates3, proj3, h3, br, bi, lam)


NT = (((1,), (1,)), ((), ()))
TN = (((0,), (0,)), ((), ()))


def _hs(h):
    return slice(h * HEAD, (h + 1) * HEAD)


def _head_rows(x):
    head = lax.shift_right_logical(lax.broadcasted_iota(jnp.int32, x.shape, 1), HEAD.bit_length() - 1)
    return jnp.concatenate([jnp.where(head == h, x, jnp.zeros_like(x)) for h in range(MEM_HEADS)], axis=0)


def _head_sum(xbd):
    M = xbd.shape[0] // MEM_HEADS
    head = lax.shift_right_logical(lax.broadcasted_iota(jnp.int32, (M, xbd.shape[1]), 1), HEAD.bit_length() - 1)
    out = jnp.zeros((M, xbd.shape[1]), xbd.dtype)
    for h in range(MEM_HEADS):
        out = jnp.where(head == h, xbd[h * M:(h + 1) * M], out)
    return out


def _mem_probs(q, kbd):
    M = kbd.shape[0] // MEM_HEADS
    s = lax.dot_general(q, kbd, NT, preferred_element_type=F32) * (HEAD ** -0.5)
    ps = []
    for h in range(MEM_HEADS):
        sh = s[:, h * M:(h + 1) * M]
        e = jnp.exp(sh - jnp.max(sh, axis=-1, keepdims=True))
        ps.append(e / jnp.sum(e, axis=-1, keepdims=True))
    return ps


def _mem_attn_fwd(proj3, q_off, mkv3, name):
    Bl, T, _ = proj3.shape
    M = mkv3.shape[1]
    tq = _tile(T, (512, 256, 128))

    def body(q_ref, k_ref, v_ref, o_ref):
        q = q_ref[0].astype(BF16)
        kbd = _head_rows(k_ref[0].astype(BF16))
        vbd = _head_rows(v_ref[0].astype(BF16))
        p = jnp.concatenate(_mem_probs(q, kbd), axis=-1).astype(BF16)
        o_ref[0] = jnp.dot(p, vbd, preferred_element_type=F32).astype(BF16)

    return pl.pallas_call(
        body, grid=(Bl, T // tq),
        in_specs=[pl.BlockSpec((1, tq, MEM_W), lambda b, t: (b, t, q_off // MEM_W)),
                  pl.BlockSpec((1, M, MEM_W), lambda b, t: (b, 0, 0)),
                  pl.BlockSpec((1, M, MEM_W), lambda b, t: (b, 0, 1))],
        out_specs=pl.BlockSpec((1, tq, MEM_W), lambda b, t: (b, t, 0)),
        out_shape=S((Bl, T, MEM_W), BF16), compiler_params=_cparams("parallel", "parallel"), name=name)(
            proj3, mkv3, mkv3)


def _mem_attn_bwd(proj3, q_off, mkv3, do3, name):
    Bl, T, _ = proj3.shape
    M = mkv3.shape[1]
    tq = _tile(T, (512, 256, 128))
    scale = HEAD ** -0.5

    def body(q_ref, k_ref, v_ref, do_ref, dq_ref, dkv_ref):
        q = q_ref[0].astype(BF16)
        kbd = _head_rows(k_ref[0].astype(BF16))
        vbd = _head_rows(v_ref[0].astype(BF16))
        do = do_ref[0].astype(BF16)
        ps = _mem_probs(q, kbd)
        dvbd = lax.dot_general(jnp.concatenate(ps, axis=-1).astype(BF16), do, TN, preferred_element_type=F32)
        dp = lax.dot_general(do, vbd, NT, preferred_element_type=F32)
        dss = []
        for h in range(MEM_HEADS):
            dph = dp[:, h * M:(h + 1) * M]
            dss.append(ps[h] * (dph - jnp.sum(ps[h] * dph, axis=-1, keepdims=True)) * scale)
        ds = jnp.concatenate(dss, axis=-1).astype(BF16)
        dq_ref[0] = jnp.dot(ds, kbd, preferred_element_type=F32).astype(BF16)
        dkbd = lax.dot_general(ds, q, TN, preferred_element_type=F32)

        @pl.when(pl.program_id(1) == 0)
        def _():
            dkv_ref[...] = jnp.zeros_like(dkv_ref)

        dkv_ref[0] += jnp.concatenate([_head_sum(dkbd), _head_sum(dvbd)], axis=-1)

    return pl.pallas_call(
        body, grid=(Bl, T // tq),
        in_specs=[pl.BlockSpec((1, tq, MEM_W), lambda b, t: (b, t, q_off // MEM_W)),
                  pl.BlockSpec((1, M, MEM_W), lambda b, t: (b, 0, 0)),
                  pl.BlockSpec((1, M, MEM_W), lambda b, t: (b, 0, 1)),
                  pl.BlockSpec((1, tq, MEM_W), lambda b, t: (b, t, 0))],
        out_specs=(pl.BlockSpec((1, tq, MEM_W), lambda b, t: (b, t, 0)),
                   pl.BlockSpec((1, M, 2 * MEM_W), lambda b, t: (b, 0, 0))),
        out_shape=(S((Bl, T, MEM_W), BF16), S((Bl, M, 2 * MEM_W), F32)),
        compiler_params=_cparams("parallel", "arbitrary"), name=name)(proj3, mkv3, mkv3, do3)


GROUP_ROWS = SWA_GROUP * WIN


def _group_rows(x, kvh):
    return jnp.concatenate([x[:, _hs(SWA_GROUP * kvh + g)] for g in range(SWA_GROUP)], axis=0)


def _group_col(vals):
    grp = lax.shift_right_logical(lax.broadcasted_iota(jnp.int32, (GROUP_ROWS, 1), 0), WIN.bit_length() - 1)
    col = jnp.full((GROUP_ROWS, 1), vals[-1], F32)
    for g in range(SWA_GROUP - 2, -1, -1):
        col = jnp.where(grp == g, vals[g], col)
    return col


def _swa_probs(qh, kph, kch, sink, slope, has_prev):
    qi = jnp.bitwise_and(lax.broadcasted_iota(jnp.int32, (GROUP_ROWS, WIN), 0), WIN - 1)
    kj = lax.broadcasted_iota(jnp.int32, (GROUP_ROWS, WIN), 1)
    scale = HEAD ** -0.5
    sp = lax.dot_general(qh, kph, NT, preferred_element_type=F32) * scale
    sc = lax.dot_general(qh, kch, NT, preferred_element_type=F32) * scale
    dist_p = (qi + WIN - kj).astype(F32)
    dist_c = (qi - kj).astype(F32)
    neg = -jnp.inf
    sp = jnp.where(kj > qi + jnp.where(has_prev, 0, WIN), sp - slope * dist_p, neg)
    sc = jnp.where(kj <= qi, sc - slope * dist_c, neg)
    m = jnp.maximum(jnp.maximum(jnp.max(sp, axis=-1, keepdims=True), jnp.max(sc, axis=-1, keepdims=True)), sink)
    ep = jnp.exp(sp - m)
    ec = jnp.exp(sc - m)
    es = jnp.exp(sink - m)
    inv = 1.0 / (jnp.sum(ep, axis=-1, keepdims=True) + jnp.sum(ec, axis=-1, keepdims=True) + es)
    return ep * inv, ec * inv, es * inv


def _swa_specs(nb):
    prev = lambda n: jnp.maximum(n - 1, 0)
    q = pl.BlockSpec((1, WIN, MIX_W), lambda b, n: (b, n, 0))
    kp = pl.BlockSpec((1, WIN, MEM_W), lambda b, n: (b, prev(n), 0))
    kc = pl.BlockSpec((1, WIN, MEM_W), lambda b, n: (b, n, 0))
    vp = pl.BlockSpec((1, WIN, MEM_W), lambda b, n: (b, prev(n), 1))
    vc = pl.BlockSpec((1, WIN, MEM_W), lambda b, n: (b, n, 1))
    sm = pl.BlockSpec(memory_space=pltpu.SMEM)
    return q, kp, kc, vp, vc, sm


def _swa_fwd(proj3, kv3, sinks, name):
    Bl, T, _ = proj3.shape
    nb = T // WIN
    q_s, kp_s, kc_s, vp_s, vc_s, sm = _swa_specs(nb)

    def body(q_ref, kp_ref, kc_ref, vp_ref, vc_ref, sink_ref, o_ref):
        has_prev = pl.program_id(1) > 0
        q = q_ref[0].astype(BF16)
        kp, kc = kp_ref[0].astype(BF16), kc_ref[0].astype(BF16)
        vp, vc = vp_ref[0].astype(BF16), vc_ref[0].astype(BF16)
        outs = []
        for kvh in range(SWA_HEADS // SWA_GROUP):
            kvs = _hs(kvh)
            heads = range(SWA_GROUP * kvh, SWA_GROUP * (kvh + 1))
            pp, pc, _ = _swa_probs(_group_rows(q, kvh), kp[:, kvs], kc[:, kvs], _group_col([sink_ref[h] for h in heads]),
                                   _group_col([SLOPES[h] for h in heads]), has_prev)
            og = (jnp.dot(pp.astype(BF16), vp[:, kvs], preferred_element_type=F32)
                  + jnp.dot(pc.astype(BF16), vc[:, kvs], preferred_element_type=F32))
            outs += [og[g * WIN:(g + 1) * WIN] for g in range(SWA_GROUP)]
        o_ref[0] = jnp.concatenate(outs, axis=-1).astype(BF16)

    return pl.pallas_call(
        body, grid=(Bl, nb), in_specs=[q_s, kp_s, kc_s, vp_s, vc_s, sm], out_specs=q_s,
        out_shape=S((Bl, T, MIX_W), BF16), compiler_params=_cparams("parallel", "parallel"), name=name)(
            proj3, kv3, kv3, kv3, kv3, sinks)


def _swa_bwd(proj3, kv3, sinks, do3, name):
    Bl, T, _ = proj3.shape
    nb = T // WIN
    q_s, kp_s, kc_s, vp_s, vc_s, sm = _swa_specs(nb)
    kv_s = pl.BlockSpec((1, WIN, 2 * MEM_W), lambda b, n: (b, n, 0))
    sk_s = pl.BlockSpec((8, LANES), lambda b, n: (0, 0))
    scale = HEAD ** -0.5

    def body(q_ref, kp_ref, kc_ref, vp_ref, vc_ref, sink_ref, do_ref, dq_ref, dkc_ref, dkp_ref, dsk_ref):
        has_prev = pl.program_id(1) > 0
        q = q_ref[0].astype(BF16)
        kp, kc = kp_ref[0].astype(BF16), kc_ref[0].astype(BF16)
        vp, vc = vp_ref[0].astype(BF16), vc_ref[0].astype(BF16)
        do = do_ref[0].astype(BF16)
        lane = lax.broadcasted_iota(jnp.int32, (8, LANES), 1)
        srow = lax.broadcasted_iota(jnp.int32, (8, LANES), 0)
        dsk = jnp.zeros((8, LANES), F32)
        dqs = []
        dkc, dkp, dvc, dvp = [], [], [], []
        grp = lax.shift_right_logical(lax.broadcasted_iota(jnp.int32, (GROUP_ROWS, 1), 0), WIN.bit_length() - 1)
        for kvh in range(SWA_HEADS // SWA_GROUP):
            kvs = _hs(kvh)
            heads = range(SWA_GROUP * kvh, SWA_GROUP * (kvh + 1))
            qg, dog = _group_rows(q, kvh), _group_rows(do, kvh)
            pp, pc, ps = _swa_probs(qg, kp[:, kvs], kc[:, kvs], _group_col([sink_ref[h] for h in heads]),
                                    _group_col([SLOPES[h] for h in heads]), has_prev)
            dpp = lax.dot_general(dog, vp[:, kvs], NT, preferred_element_type=F32)
            dpc = lax.dot_general(dog, vc[:, kvs], NT, preferred_element_type=F32)
            delta = jnp.sum(pp * dpp, axis=-1, keepdims=True) + jnp.sum(pc * dpc, axis=-1, keepdims=True)
            dsp = (pp * (dpp - delta) * scale).astype(BF16)
            dsc = (pc * (dpc - delta) * scale).astype(BF16)
            dqg = (jnp.dot(dsp, kp[:, kvs], preferred_element_type=F32)
                   + jnp.dot(dsc, kc[:, kvs], preferred_element_type=F32))
            dqs += [dqg[g * WIN:(g + 1) * WIN] for g in range(SWA_GROUP)]
            dkc.append(lax.dot_general(dsc, qg, TN, preferred_element_type=F32))
            dkp.append(lax.dot_general(dsp, qg, TN, preferred_element_type=F32))
            dvc.append(lax.dot_general(pc.astype(BF16), dog, TN, preferred_element_type=F32))
            dvp.append(lax.dot_general(pp.astype(BF16), dog, TN, preferred_element_type=F32))
            dsink = ps * delta
            for g, h in enumerate(heads):
                dsk = dsk + jnp.where((lane == h) & (srow == 0), -jnp.sum(jnp.where(grp == g, dsink, 0.0)), 0.0)
        dq_ref[0] = jnp.concatenate(dqs, axis=-1).astype(BF16)
        dkc_ref[0] = jnp.concatenate(dkc + dvc, axis=-1)
        dkp_ref[0] = jnp.concatenate(dkp + dvp, axis=-1)

        @pl.when((pl.program_id(0) == 0) & (pl.program_id(1) == 0))
        def _():
            dsk_ref[...] = jnp.zeros_like(dsk_ref)

        dsk_ref[...] += dsk

    return pl.pallas_call(
        body, grid=(Bl, nb), in_specs=[q_s, kp_s, kc_s, vp_s, vc_s, sm, q_s], out_specs=(q_s, kv_s, kv_s, sk_s),
        out_shape=(S((Bl, T, MIX_W), BF16), S((Bl, T, 2 * MEM_W), F32), S((Bl, T, 2 * MEM_W), F32), S((8, LANES), F32)),
        compiler_params=_cparams("arbitrary", "arbitrary"), name=name)(proj3, kv3, kv3, kv3, kv3, sinks, do3)


def _kv_grad_combine(parts, name):
    Bl, T, W = parts[0][0].shape
    nb = T // WIN
    nl = len(parts)

    def body(*refs):
        o_ref = refs[-1]
        has_next = jnp.where(pl.program_id(1) == nb - 1, 0.0, 1.0)
        acc = None
        for l in range(nl):
            c = refs[2 * l][0] + has_next * refs[2 * l + 1][0]
            acc = c if acc is None else acc + c
        o_ref[0] = acc.astype(BF16)

    cur = pl.BlockSpec((1, WIN, W), lambda b, n: (b, n, 0))
    nxt = pl.BlockSpec((1, WIN, W), lambda b, n: (b, jnp.minimum(n + 1, nb - 1), 0))
    return pl.pallas_call(
        body, grid=(Bl, nb), in_specs=[cur, nxt] * nl, out_specs=cur, out_shape=S((Bl, T, W), BF16),
        compiler_params=_cparams("parallel", "parallel"), name=name)(*[a for pr in parts for a in pr])


def _loss_bwd(y, target, name="loss"):
    N, D = y.shape
    tm = _tile(N, (512, 256, 128))

    def body(y_ref, t_ref, dy_ref, l_ref):
        e = y_ref[...] - t_ref[...]
        dy_ref[...] = e * (1.0 / D)

        @pl.when(pl.program_id(0) == 0)
        def _():
            l_ref[...] = jnp.zeros_like(l_ref)

        l_ref[...] += jnp.sum(e * e, axis=0, keepdims=True) * (0.5 / D)

    row = pl.BlockSpec((tm, D), lambda i: (i, 0))
    vec = pl.BlockSpec((1, D), lambda i: (0, 0))
    return pl.pallas_call(
        body, grid=(N // tm,), in_specs=[row, row], out_specs=(row, vec), out_shape=(S((N, D), F32), S((1, D), F32)),
        compiler_params=_cparams("arbitrary"), name=name)(y, target)


def _all_gather(x, name):
    R, C = x.shape

    def body(x_ref, out_ref, send_sems, recv_sems, local_sem):
        mx, my, mc = lax.axis_index("x"), lax.axis_index("y"), lax.axis_index("c")
        me, sibling = (mx, my, mc), (mx, my, 1 - mc)
        chips = [(1 - mx, my), (mx, 1 - my), (1 - mx, 1 - my)]

        def rows(px, py, pc):
            return out_ref.at[4 * px + 2 * py + pc]

        def copy(kk, block, to, src=None):
            return pltpu.make_async_remote_copy(
                src_ref=rows(*block) if src is None else src, dst_ref=rows(*block), send_sem=send_sems.at[kk],
                recv_sem=recv_sems.at[kk], device_id=to, device_id_type=MESH)

        mine = pltpu.make_async_copy(x_ref, rows(*me), local_sem)
        mine.start()
        first = [copy(0, me, sibling, src=x_ref)]
        first += [copy(1 + j, me, (*chip, mc), src=x_ref) for j, chip in enumerate(chips)]
        for cp in first:
            cp.start()
        passed = [copy(4 + j, (*chip, mc), sibling) for j, chip in enumerate(chips)]
        for j, chip in enumerate(chips):
            copy(1 + j, (*chip, mc), me).wait_recv()
            passed[j].start()
        copy(0, sibling, me).wait_recv()
        for j, chip in enumerate(chips):
            copy(4 + j, (*chip, 1 - mc), me).wait_recv()
        for cp in first + passed:
            cp.wait_send()
        mine.wait()

    return pl.pallas_call(
        body, out_shape=S((N_DEV, R, C), x.dtype), in_specs=[ANY], out_specs=ANY,
        scratch_shapes=[pltpu.SemaphoreType.DMA((7,)), pltpu.SemaphoreType.DMA((7,)), pltpu.SemaphoreType.DMA(())],
        name=name)(x)


def _ag_weights(shards, row_sharded, name):
    n = len(shards)

    def full_shape(a, rows):
        if rows:
            return a.shape[:-2] + (N_DEV * a.shape[-2],) + a.shape[-1:]
        return (N_DEV,) + a.shape

    def body(*refs):
        x_refs, o_refs = refs[:n], refs[n:2 * n]
        send_sems, recv_sems, local_sems = refs[2 * n:]
        mx, my, mc = lax.axis_index("x"), lax.axis_index("y"), lax.axis_index("c")
        me, sibling = (mx, my, mc), (mx, my, 1 - mc)
        chips = [(1 - mx, my), (mx, 1 - my), (1 - mx, 1 - my)]

        def dst(t, px, py, pc):
            d = 4 * px + 2 * py + pc
            if not row_sharded[t]:
                return o_refs[t].at[d]
            r = shards[t].shape[-2]
            idx = (slice(None),) * (shards[t].ndim - 2) + (pl.ds(pl.multiple_of(d * r, 16), r), slice(None))
            return o_refs[t].at[idx]

        def copy(kk, t, block, to, src=None):
            return pltpu.make_async_remote_copy(
                src_ref=dst(t, *block) if src is None else src, dst_ref=dst(t, *block),
                send_sem=send_sems.at[kk * n + t], recv_sem=recv_sems.at[kk * n + t], device_id=to,
                device_id_type=MESH)

        mine = [pltpu.make_async_copy(x_refs[t], dst(t, *me), local_sems.at[t]) for t in range(n)]
        for cp in mine:
            cp.start()
        first = []
        for t in range(n):
            first.append(copy(0, t, me, sibling, src=x_refs[t]))
            first += [copy(1 + j, t, me, (*chip, mc), src=x_refs[t]) for j, chip in enumerate(chips)]
        for cp in first:
            cp.start()
        passed = []
        for j, chip in enumerate(chips):
            for t in range(n):
                copy(1 + j, t, (*chip, mc), me).wait_recv()
                cp = copy(4 + j, t, (*chip, mc), sibling)
                cp.start()
                passed.append(cp)
        for t in range(n):
            copy(0, t, sibling, me).wait_recv()
            for j, chip in enumerate(chips):
                copy(4 + j, t, (*chip, 1 - mc), me).wait_recv()
        for cp in first + passed:
            cp.wait_send()
        for cp in mine:
            cp.wait()

    return pl.pallas_call(
        body, out_shape=tuple(S(full_shape(a, r), a.dtype) for a, r in zip(shards, row_sharded)),
        in_specs=[ANY] * n, out_specs=tuple([ANY] * n),
        scratch_shapes=[pltpu.SemaphoreType.DMA((7 * n,)), pltpu.SemaphoreType.DMA((7 * n,)),
                        pltpu.SemaphoreType.DMA((n,))],
        name=name)(*shards)


FLIPS = [(fx, fy, fc) for fx in (0, 1) for fy in (0, 1) for fc in (0, 1)][1:]
HBM = pl.BlockSpec(memory_space=pltpu.HBM)
SEM = pl.BlockSpec(memory_space=pltpu.SEMAPHORE)
EFFECT = pltpu.SideEffectType.DATAFLOW_SIDE_EFFECTING


def _hbm(a):
    return pltpu.with_memory_space_constraint(a, pltpu.HBM)


def _flips(gather):
    return [(0, 0, 0)] + FLIPS if gather else FLIPS


def _split_copies(gather, s_refs, l_refs, send_sems, recv_sems):
    n = len(s_refs)
    mx, my, mc = lax.axis_index("x"), lax.axis_index("y"), lax.axis_index("c")
    me = 4 * mx + 2 * my + mc
    copies = []
    for k, (fx, fy, fc) in enumerate(_flips(gather)):
        px, py, pc = (1 - mx if fx else mx), (1 - my if fy else my), (1 - mc if fc else mc)
        for t in range(n):
            if gather:
                src = s_refs[t]
                r = src.shape[0]
                dst = l_refs[t].at[pl.ds(pl.multiple_of(me * r, 16), r), :]
            else:
                src = s_refs[t].at[:, 4 * px + 2 * py + pc]
                dst = l_refs[t].at[k]
            copies.append(pltpu.make_async_remote_copy(
                src_ref=src, dst_ref=dst, send_sem=send_sems.at[k * n + t], recv_sem=recv_sems.at[k * n + t],
                device_id=(px, py, pc), device_id_type=MESH))
    return copies


def _split_start(gather, srcs, lands, after, name):
    n = len(srcs)
    n_sem = len(_flips(gather)) * n

    def body(*refs):
        s_refs, l_refs = refs[:n], refs[n:2 * n]
        send_sems, recv_sems = refs[2 * n + 1], refs[2 * n + 2]
        token = refs[-1]
        for cp in _split_copies(gather, s_refs, l_refs, send_sems, recv_sems):
            cp.start()
        token[...] = jnp.zeros_like(token)

    outs = pl.pallas_call(
        body, name=name,
        out_shape=(pltpu.SemaphoreType.DMA((n_sem,)), pltpu.SemaphoreType.DMA((n_sem,)))
        + tuple(pltpu.HBM(a.shape, a.dtype) for a in lands) + (S((8, LANES), F32),),
        in_specs=[HBM] * (2 * n) + [ANY],
        out_specs=(SEM, SEM) + (HBM,) * n + (pl.BlockSpec(memory_space=pltpu.VMEM),),
        input_output_aliases={n + i: 2 + i for i in range(n)},
        compiler_params=pltpu.CompilerParams(has_side_effects=EFFECT),
    )(*[_hbm(a) for a in srcs], *[_hbm(a) for a in lands], after)
    return outs[0], outs[1], list(srcs), list(outs[2:2 + n]), outs[-1]


def _split_wait(gather, send_sems, recv_sems, srcs, lands, after, name):
    n = len(srcs)

    def body(*refs):
        s_refs, l_refs = refs[:n], refs[n:2 * n]
        ssem, rsem = refs[2 * n], refs[2 * n + 1]
        copies = _split_copies(gather, s_refs, l_refs, ssem, rsem)
        for cp in copies:
            cp.wait_send()
        for cp in copies:
            cp.wait_recv()

    outs = pl.pallas_call(
        body, name=name, out_shape=tuple(pltpu.HBM(a.shape, a.dtype) for a in lands),
        in_specs=[HBM] * (2 * n) + [SEM, SEM, ANY], out_specs=(HBM,) * n,
        input_output_aliases={n + i: i for i in range(n)},
        compiler_params=pltpu.CompilerParams(has_side_effects=EFFECT),
    )(*[_hbm(a) for a in srcs], *lands, send_sems, recv_sems, after)
    return list(outs)


def _adamw_math(w, g, m, v):
    m = ADAM_B1 * m + (1.0 - ADAM_B1) * g
    v = ADAM_B2 * v + (1.0 - ADAM_B2) * (g * g)
    m_hat = m / (1.0 - ADAM_B1 ** ADAM_STEP)
    v_hat = v / (1.0 - ADAM_B2 ** ADAM_STEP)
    delta = -ADAM_LR * (m_hat / (jnp.sqrt(v_hat) + ADAM_EPS) + ADAM_WD * w)
    return delta, m, v


def _adamw_layers(owns, gots, w, m, v, name):
    L, B, C = w.shape
    per_row = 2 * L * len(FLIPS) * C * owns[0].dtype.itemsize
    tb = max([t for t in range(16, B + 1, 16) if B % t == 0 and (t * per_row <= 16 * 1024 * 1024 or t == 16)] or [B])
    me = (4 * lax.axis_index("x") + 2 * lax.axis_index("y") + lax.axis_index("c")).astype(jnp.int32).reshape(1)

    def body(me_ref, *refs):
        own_refs, got_refs = refs[:L], refs[L:2 * L]
        w_ref, m_ref, v_ref = refs[2 * L:2 * L + 3]
        g_out, d_out, m_out, v_out = refs[2 * L + 3:]
        layer = pl.program_id(0)
        for kk in range(L):
            @pl.when(layer == kk)
            def _():
                g = own_refs[kk][0].astype(F32)
                for s in range(len(FLIPS)):
                    g = g + got_refs[kk][s].astype(F32)
                d, mn, vn = _adamw_math(w_ref[...], g, m_ref[...], v_ref[...])
                g_out[...] = g
                d_out[...] = d
                m_out[...] = mn
                v_out[...] = vn

    def row(kk, layer, i):
        return jnp.where(layer == kk, i, 0)

    blk = pl.BlockSpec((1, tb, C), lambda layer, i, me_ref: (layer, i, 0))
    own_specs = [pl.BlockSpec((1, 1, tb, C), lambda layer, i, me_ref, kk=kk: (0, me_ref[0], row(kk, layer, i), 0))
                 for kk in range(L)]
    got_specs = [pl.BlockSpec((len(FLIPS), 1, tb, C), lambda layer, i, me_ref, kk=kk: (0, 0, row(kk, layer, i), 0))
                 for kk in range(L)]
    return pl.pallas_call(
        body,
        grid_spec=pltpu.PrefetchScalarGridSpec(
            num_scalar_prefetch=1, grid=(L, B // tb), in_specs=own_specs + got_specs + [blk, blk, blk],
            out_specs=(blk, blk, blk, blk)),
        out_shape=(S((L, B, C), F32),) * 4, compiler_params=_cparams("arbitrary", "arbitrary"), name=name)(
            me, *owns, *gots, w, m, v)


def _adamw_replicated(parts, w, m, v, name):
    R, C = w.shape
    rb = _tile(R, (512, 256, 128, 64, 32, 16))

    def body(p_ref, w_ref, m_ref, v_ref, g_out, d_out, m_out, v_out):
        g = p_ref[0].astype(F32)
        for j in range(1, N_DEV):
            g = g + p_ref[j].astype(F32)
        d, mn, vn = _adamw_math(w_ref[...], g, m_ref[...], v_ref[...])
        g_out[...] = g
        d_out[...] = d
        m_out[...] = mn
        v_out[...] = vn

    blk = pl.BlockSpec((rb, C), lambda i: (i, 0))
    return pl.pallas_call(
        body, grid=(R // rb,), in_specs=[pl.BlockSpec((N_DEV, rb, C), lambda i: (0, i, 0)), blk, blk, blk],
        out_specs=(blk, blk, blk, blk), out_shape=(S((R, C), F32),) * 4, compiler_params=_cparams("parallel"),
        name=name)(parts, w, m, v)


def _pack(arrs, rows_mult, dtype):
    flat = jnp.concatenate([a.reshape(-1).astype(dtype) for a in arrs])
    n = flat.shape[0]
    per = rows_mult * LANES
    tot = -(-n // per) * per
    return jnp.pad(flat, (0, tot - n)).reshape(tot // LANES, LANES)


def _unpack(blob, shapes):
    flat = blob.reshape(-1)
    out, off = [], 0
    for shp in shapes:
        n = int(np.prod(shp))
        out.append(flat[off:off + n].reshape(shp))
        off += n
    return out


def _small_to_natural(g8):
    t = jnp.moveaxis(g8, 0, -2)
    return t.reshape(t.shape[:-2] + (N_DEV * t.shape[-1],))


def _small_to_cols(g):
    t = g.reshape(g.shape[:-1] + (N_DEV, g.shape[-1] // N_DEV))
    return jnp.moveaxis(t, -2, 0)


def _block_diag(w):
    nb, bs, _ = w.shape
    eye = jnp.eye(nb, dtype=w.dtype)
    return (eye[:, None, :, None] * w[:, :, None, :]).reshape(nb * bs, nb * bs)


def _diag_blocks(d, nb, bs):
    d4 = d.reshape(nb, bs, nb, bs)
    return jnp.stack([d4[i, :, i, :] for i in range(nb)])


def kernel(x, mem, g_mix_pre, g_mix_post, g_ffn_pre, g_ffn_post, g_mem, w_mem_kv, w_mix_out, w_ffn_up, w_ffn_conv, b_ffn_conv, w_ffn_down, w_in_a, w_conv_a, b_conv_a, w_rg_r, b_rg_r, w_rg_i, b_rg_i, lru_lambda, w_in_b, sinks_b, g_kv, w_kv, loss_target, m_g_mix_pre, m_g_mix_post, m_g_ffn_pre, m_g_ffn_post, m_g_mem, m_w_mem_kv, m_w_mix_out, m_w_ffn_up, m_w_ffn_conv, m_b_ffn_conv, m_w_ffn_down, m_w_in_a, m_w_conv_a, m_b_conv_a, m_w_rg_r, m_b_rg_r, m_w_rg_i, m_b_rg_i, m_lru_lambda, m_w_in_b, m_sinks_b, m_g_kv, m_w_kv, v_g_mix_pre, v_g_mix_post, v_g_ffn_pre, v_g_ffn_post, v_g_mem, v_w_mem_kv, v_w_mix_out, v_w_ffn_up, v_w_ffn_conv, v_b_ffn_conv, v_w_ffn_down, v_w_in_a, v_w_conv_a, v_b_conv_a, v_w_rg_r, v_b_rg_r, v_w_rg_i, v_b_rg_i, v_lru_lambda, v_w_in_b, v_sinks_b, v_g_kv, v_w_kv):
    w_loc = dict(g_mix_pre=g_mix_pre, g_mix_post=g_mix_post, g_ffn_pre=g_ffn_pre, g_ffn_post=g_ffn_post, g_mem=g_mem,
                 w_mem_kv=w_mem_kv, w_mix_out=w_mix_out, w_ffn_up=w_ffn_up, w_ffn_conv=w_ffn_conv,
                 b_ffn_conv=b_ffn_conv, w_ffn_down=w_ffn_down, w_in_a=w_in_a, w_conv_a=w_conv_a, b_conv_a=b_conv_a,
                 w_rg_r=w_rg_r, b_rg_r=b_rg_r, w_rg_i=w_rg_i, b_rg_i=b_rg_i, lru_lambda=lru_lambda, w_in_b=w_in_b,
                 sinks_b=sinks_b, g_kv=g_kv, w_kv=w_kv)
    m_loc = dict(g_mix_pre=m_g_mix_pre, g_mix_post=m_g_mix_post, g_ffn_pre=m_g_ffn_pre, g_ffn_post=m_g_ffn_post,
                 g_mem=m_g_mem, w_mem_kv=m_w_mem_kv, w_mix_out=m_w_mix_out, w_ffn_up=m_w_ffn_up,
                 w_ffn_conv=m_w_ffn_conv, b_ffn_conv=m_b_ffn_conv, w_ffn_down=m_w_ffn_down, w_in_a=m_w_in_a,
                 w_conv_a=m_w_conv_a, b_conv_a=m_b_conv_a, w_rg_r=m_w_rg_r, b_rg_r=m_b_rg_r, w_rg_i=m_w_rg_i,
                 b_rg_i=m_b_rg_i, lru_lambda=m_lru_lambda, w_in_b=m_w_in_b, sinks_b=m_sinks_b, g_kv=m_g_kv,
                 w_kv=m_w_kv)
    v_loc = dict(g_mix_pre=v_g_mix_pre, g_mix_post=v_g_mix_post, g_ffn_pre=v_g_ffn_pre, g_ffn_post=v_g_ffn_post,
                 g_mem=v_g_mem, w_mem_kv=v_w_mem_kv, w_mix_out=v_w_mix_out, w_ffn_up=v_w_ffn_up,
                 w_ffn_conv=v_w_ffn_conv, b_ffn_conv=v_b_ffn_conv, w_ffn_down=v_w_ffn_down, w_in_a=v_w_in_a,
                 w_conv_a=v_w_conv_a, b_conv_a=v_b_conv_a, w_rg_r=v_w_rg_r, b_rg_r=v_b_rg_r, w_rg_i=v_w_rg_i,
                 b_rg_i=v_b_rg_i, lru_lambda=v_lru_lambda, w_in_b=v_w_in_b, sinks_b=v_sinks_b, g_kv=v_g_kv,
                 w_kv=v_w_kv)

    Bl, T, D = x.shape
    Ml = mem.shape[1]
    N = Bl * T
    depth = g_mix_pre.shape[0]
    n_a = w_in_a.shape[0]
    F = w_ffn_down.shape[1] * N_DEV
    def as_rows(n, a):
        return jnp.swapaxes(a, -1, -2) if n in TRANSPOSED else a

    def mix_keys(l):
        keys = [("w_mem_kv", l), ("w_mix_out", l), ("w_in_a", l) if l < n_a else ("w_in_b", l - n_a)]
        return keys + ([("w_kv", None)] if l == n_a else [])

    def ffn_keys(l):
        return [("w_ffn_up", l), ("w_ffn_down", l)]

    def shard_of(key):
        n, i = key
        return as_rows(n, w_loc[n] if i is None else w_loc[n][i]).astype(BF16)

    W = {}
    keys0 = mix_keys(0)
    got0 = _ag_weights([shard_of(kk) for kk in keys0] + [w_loc[n] for n in SMALL_SHARDED],
                       [True] * len(keys0) + [False] * len(SMALL_SHARDED), name="ag_weights_0")
    W.update(zip(keys0, got0))
    for n, a in zip(SMALL_SHARDED, got0[len(keys0):]):
        W[n] = _small_to_natural(a)

    def gather_start(keys, after, tag):
        shards = [shard_of(kk) for kk in keys]
        lands = [lax.empty((N_DEV * s.shape[0],) + s.shape[1:], s.dtype) for s in shards]
        return (keys, tag) + _split_start(True, shards, lands, after, name=f"ag_start_{tag}")

    def gather_wait(pending, after):
        keys, tag, ssem, rsem, srcs, lands, _ = pending
        W.update(zip(keys, _split_wait(True, ssem, rsem, srcs, lands, after, name=f"ag_wait_{tag}")))

    pending_ffn = gather_start(ffn_keys(0), got0[0], "ffn_0")

    nblk, bsz = w_rg_r.shape[1], w_rg_r.shape[2]
    wbd = [jnp.concatenate([_block_diag(w_rg_r[j]), _block_diag(w_rg_i[j])], axis=1).astype(BF16) for j in range(n_a)]

    def vec(a):
        return a.reshape(1, -1)

    x2 = x.reshape(N, D)
    mem2 = mem.reshape(Bl * Ml, D)
    saved = []
    kvn = kv3 = x_kv = None
    xs = x2
    h1 = _rms_fwd(xs, vec(g_mix_pre[0]), BF16, name="rms_mixpre_0")
    for l in range(depth):
        sv = {"x0": xs}
        tok = None
        if l + 1 < depth:
            pending = gather_start(mix_keys(l + 1), pending_ffn[-1] if l == 0 else W[("w_mem_kv", l)], f"mix_{l + 1}")
            pending_next_ffn = gather_start(ffn_keys(l + 1), pending[-1], f"ffn_{l + 1}")
            tok = pending_next_ffn[-1]
        memn = _rms_fwd(mem2, vec(g_mem[l]), BF16, name=f"rms_mem_{l}")
        mkv3 = _mm(memn, W[("w_mem_kv", l)], after=tok, name=f"mm_memkv_{l}").reshape(Bl, Ml, 2 * MEM_W)
        if l < n_a:
            j = l
            proj = _mm(h1, W[("w_in_a", j)], tb=True, after=tok, name=f"mm_in_{l}")
            proj3 = proj.reshape(Bl, T, -1)
            xc3 = _conv_fwd_call(proj3, MIX_W, MIX_W, W["w_conv_a"][j], vec(W["b_conv_a"][j]), name=f"conv_a_{l}")
            gates3 = _mm(xc3.reshape(N, MIX_W), wbd[j], name=f"mm_gates_{l}").reshape(Bl, T, 2 * MIX_W)
            y_main3, hs3 = _rglru_fwd(xc3, gates3, proj3, vec(b_rg_r[j]), vec(b_rg_i[j]), vec(W["lru_lambda"][j]),
                                      name=f"rglru_fwd_{l}")
            q_off = 2 * MIX_W
            sv.update(xc3=xc3, gates3=gates3, hs3=hs3)
        else:
            j = l - n_a
            if l == n_a:
                x_kv = xs
                kv3 = _mm(kvn, W[("w_kv", None)], name="mm_kv").reshape(Bl, T, 2 * MEM_W)
            proj = _mm(h1, W[("w_in_b", j)], after=tok, name=f"mm_in_{l}")
            proj3 = proj.reshape(Bl, T, -1)
            y_main3 = _swa_fwd(proj3, kv3, sinks_b[j], name=f"swa_fwd_{l}")
            q_off = MIX_W
        y_mem3 = _mem_attn_fwd(proj3, q_off, mkv3, name=f"memattn_fwd_{l}")
        y_main = y_main3.reshape(N, MIX_W)
        y_mem = y_mem3.reshape(N, MEM_W)
        y = _mm_sum([(y_main, W[("w_mix_out", l)], (0, 0)), (y_mem, W[("w_mix_out", l)], (MIX_W, 0))], n=D,
                    name=f"mm_mixout_{l}")
        x1, h2 = _rms_pair_fwd(y, vec(g_mix_post[l]), xs, [vec(g_ffn_pre[l])], name=f"rms_mixpost_ffnpre_{l}")
        gather_wait(pending_ffn, h2)
        if l + 1 < depth:
            pending_ffn = pending_next_ffn
        u3 = _mm(h2, W[("w_ffn_up", l)], tb=True, name=f"mm_up_{l}").reshape(Bl, T, 2 * F)
        act3 = _ffn_mid_fwd(u3, W["w_ffn_conv"][l], vec(b_ffn_conv[l]), name=f"ffn_mid_fwd_{l}")
        act = act3.reshape(N, F)
        f = _mm(act, W[("w_ffn_down", l)], name=f"mm_down_{l}")
        sv.update(h1=h1, memn=memn, mkv3=mkv3, proj3=proj3, q_off=q_off, y_main=y_main, y_mem=y_mem, y=y, x1=x1,
                  h2=h2, u3=u3, act=act, f=f)
        saved.append(sv)
        if l + 1 < depth:
            gains = [vec(g_mix_pre[l + 1])] + ([vec(g_kv)] if l + 1 == n_a else [])
            xs, h1, *rest = _rms_pair_fwd(f, vec(g_ffn_post[l]), x1, gains, name=f"rms_ffnpost_mixpre_{l}")
            if rest:
                kvn = rest[0]
            gather_wait(pending, xs)
        else:
            xs = _rms_fwd(f, vec(g_ffn_post[l]), F32, res=x1, name=f"rms_ffnpost_{l}")

    dxs, loss_vec = _loss_bwd(xs, loss_target.reshape(N, D))
    loss = lax.psum(jnp.sum(loss_vec), ("x", "y", "c"))

    G = {n: [None] * w_loc[n].shape[0] for n in REPL + SMALL_SHARDED if n != "g_kv"}
    GW = {}

    def dw(key, off, a, b_, nm):
        GW[key] = _mm(a, b_, ta=True, out_dtype=BF16, into=(GW.get(key), (1,) + W[key].shape, 0, off), name=nm)

    def grad_blocks(key):
        g = GW[key]
        return g.reshape(1, N_DEV, g.shape[1] // N_DEV, g.shape[2])

    reduces = []

    def reduce_start(keys, after, tag):
        srcs = [grad_blocks(kk) for kk in keys]
        lands = [lax.empty((len(FLIPS),) + s.shape[:1] + s.shape[2:], s.dtype) for s in srcs]
        started = _split_start(False, srcs, lands, after, name=f"rs_start_{tag}")
        reduces.append((keys, tag) + started)
        return started[-1]

    kv_parts = []
    df = None
    for l in reversed(range(depth)):
        sv = saved[l]
        proj3 = sv["proj3"]
        if df is None:
            df, dg = _rms_bwd(sv["f"], vec(g_ffn_post[l]), dxs, out_dtype=BF16, name=f"rmsb_ffnpost_{l}")
            G["g_ffn_post"][l] = dg[0]
        dact = _mm(df, W[("w_ffn_down", l)], tb=True, name=f"mmb_down_dx_{l}")
        dw(("w_ffn_down", l), (0, 0), sv["act"], df, f"mmb_down_dw_{l}")
        dug3, duv3, dwg, dwv, dbg, dbv = _ffn_mid_bwd(sv["u3"], dact.reshape(Bl, T, F),
                                                      W["w_ffn_conv"][l], vec(b_ffn_conv[l]), name=f"ffn_mid_bwd_{l}")
        G["w_ffn_conv"][l] = jnp.concatenate([dwg, dwv], axis=1)
        G["b_ffn_conv"][l] = jnp.concatenate([dbg, dbv], axis=1)[0]
        dug, duv = dug3.reshape(N, F), duv3.reshape(N, F)
        dw(("w_ffn_up", l), (0, 0), dug, sv["h2"], f"mmb_up_dw_g_{l}")
        dw(("w_ffn_up", l), (F, 0), duv, sv["h2"], f"mmb_up_dw_v_{l}")
        tok = reduce_start([("w_ffn_down", l), ("w_ffn_up", l)], dug, f"ffn_{l}")
        dh2 = _mm_sum([(dug, W[("w_ffn_up", l)], (0, 0)), (duv, W[("w_ffn_up", l)], (F, 0))], n=D, after=tok,
                      name=f"mmb_up_dx_{l}")
        dx1, dy, dg, dg2 = _rms_pair_bwd(sv["x1"], vec(g_ffn_pre[l]), dh2, dxs, sv["y"], vec(g_mix_post[l]),
                                         name=f"rmsb_ffnpre_mixpost_{l}")
        G["g_ffn_pre"][l] = dg[0]
        G["g_mix_post"][l] = dg2[0]
        dy_main = _mm(dy, W[("w_mix_out", l)], tb=True, n=MIX_W, k=D, name=f"mmb_mixout_dmain_{l}")
        dy_mem = _mm(dy, W[("w_mix_out", l)], tb=True, n=MEM_W, k=D, b_off=(MIX_W, 0),
                     name=f"mmb_mixout_dmem_{l}")
        dw(("w_mix_out", l), (0, 0), sv["y_main"], dy, f"mmb_mixout_dw_main_{l}")
        dw(("w_mix_out", l), (MIX_W, 0), sv["y_mem"], dy, f"mmb_mixout_dw_mem_{l}")
        dq_mem3, dmkv3 = _mem_attn_bwd(proj3, sv["q_off"], sv["mkv3"], dy_mem.reshape(Bl, T, MEM_W),
                                       name=f"memattn_bwd_{l}")
        dq_mem = dq_mem3.reshape(N, MEM_W)
        dmkv = dmkv3.reshape(Bl * Ml, 2 * MEM_W)
        dw(("w_mem_kv", l), (0, 0), sv["memn"], dmkv, f"mmb_memkv_dw_{l}")
        dmemn = _mm(dmkv, W[("w_mem_kv", l)], tb=True, name=f"mmb_memkv_dx_{l}")
        _, dg = _rms_bwd(mem2, vec(g_mem[l]), dmemn, name=f"rmsb_mem_{l}")
        G["g_mem"][l] = dg[0]
        dy_main3 = dy_main.reshape(Bl, T, MIX_W)
        if l < n_a:
            j = l
            dxc3, drp3, dip3, dugate3, dbr, dbi, dlam = _rglru_bwd(
                dy_main3, sv["xc3"], sv["gates3"], proj3, sv["hs3"], vec(b_rg_r[j]), vec(b_rg_i[j]),
                vec(W["lru_lambda"][j]), name=f"rglru_bwd_{l}")
            G["b_rg_r"][j] = dbr.reshape(nblk, bsz)
            G["b_rg_i"][j] = dbi.reshape(nblk, bsz)
            G["lru_lambda"][j] = dlam[0]
            drp, dip = drp3.reshape(N, MIX_W), dip3.reshape(N, MIX_W)
            xc2 = sv["xc3"].reshape(N, MIX_W)
            G["w_rg_r"][j] = _diag_blocks(_mm(xc2, drp, ta=True, name=f"mmb_gates_dw_r_{l}"), nblk, bsz)
            G["w_rg_i"][j] = _diag_blocks(_mm(xc2, dip, ta=True, name=f"mmb_gates_dw_i_{l}"), nblk, bsz)
            dxc = _mm_sum([(drp, wbd[j], (0, 0)), (dip, wbd[j], (0, MIX_W))], tb=True, n=MIX_W,
                          add=dxc3.reshape(N, MIX_W), name=f"mmb_gates_dx_{l}")
            dux3, dwc, dbc = _conv_bwd_call(dxc.reshape(Bl, T, MIX_W), proj3, MIX_W, MIX_W, W["w_conv_a"][j],
                                            name=f"conv_a_bwd_{l}")
            G["w_conv_a"][j] = dwc
            G["b_conv_a"][j] = dbc[0]
            pieces = [(dugate3.reshape(N, MIX_W), 0), (dux3.reshape(N, MIX_W), MIX_W), (dq_mem, 2 * MIX_W)]
            in_key = ("w_in_a", j)
        else:
            j = l - n_a
            dq3, dkc, dkp, dsk = _swa_bwd(proj3, kv3, sinks_b[j], dy_main3, name=f"swa_bwd_{l}")
            kv_parts.append((dkc, dkp))
            G["sinks_b"][j] = dsk[0, :SWA_HEADS]
            pieces = [(dq3.reshape(N, MIX_W), 0), (dq_mem, MIX_W)]
            in_key = ("w_in_b", j)
        in_t = in_key[0] in TRANSPOSED
        for pi, (piece, off) in enumerate(pieces):
            if in_t:
                dw(in_key, (off, 0), piece, sv["h1"], f"mmb_in_dw_{pi}_{l}")
            else:
                dw(in_key, (0, off), sv["h1"], piece, f"mmb_in_dw_{pi}_{l}")
        tok = reduce_start([("w_mix_out", l), ("w_mem_kv", l), in_key], dy, f"mix_{l}")
        dh1 = _mm_sum([(piece, W[in_key], (off, 0) if in_t else (0, off)) for piece, off in pieces], tb=not in_t, n=D,
                      after=tok, name=f"mmb_in_dx_{l}")
        if l > 0 and l != n_a:
            dxs, df, dg, dg2 = _rms_pair_bwd(sv["x0"], vec(g_mix_pre[l]), dh1, dx1, saved[l - 1]["f"],
                                             vec(g_ffn_post[l - 1]), name=f"rmsb_mixpre_ffnpost_{l}")
            G["g_ffn_post"][l - 1] = dg2[0]
        else:
            dxs, dg = _rms_bwd(sv["x0"], vec(g_mix_pre[l]), dh1, add=dx1, name=f"rmsb_mixpre_{l}")
            df = None
        G["g_mix_pre"][l] = dg[0]
        if l == n_a:
            dkv = _kv_grad_combine(kv_parts, name="kv_grad_combine").reshape(N, 2 * MEM_W)
            dw(("w_kv", None), (0, 0), kvn, dkv, "mmb_kv_dw")
            tok = reduce_start([("w_kv", None)], dkv, "kv")
            dkvn = _mm(dkv, W[("w_kv", None)], tb=True, after=tok, name="mmb_kv_dx")
            dxs, dg = _rms_bwd(x_kv, vec(g_kv), dkvn, add=dxs, name="rmsb_kv")
            G["g_kv"] = dg[0]
    grad_x = dxs.reshape(Bl, T, D)
    Gf = {n: (jnp.stack(g) if isinstance(g, list) else g) for n, g in G.items()}

    small4 = []
    for n in SMALL_SHARDED:
        t = _small_to_cols(Gf[n]).astype(BF16)
        small4.append(t.reshape(1, N_DEV, -1, t.shape[-1]))
    small_lands = [lax.empty((len(FLIPS),) + s.shape[:1] + s.shape[2:], s.dtype) for s in small4]
    small_started = _split_start(False, small4, small_lands, dxs, name="rs_start_small")
    r_blob = _pack([Gf[n] for n in REPL], REPL_ROWS, BF16)
    r_parts = _all_gather(r_blob, name="ag_repl_grads")
    parts = {}
    for keys, tag, ssem, rsem, srcs, lands, _ in reduces:
        for kk, s, g7 in zip(keys, srcs, _split_wait(False, ssem, rsem, srcs, lands, small_started[-1],
                                                     name=f"rs_wait_{tag}")):
            parts[kk] = (s, g7)

    res = [{} for _ in range(4)]
    for n, _ in SHARDED:
        if n in SMALL_SHARDED:
            continue
        idx = [None] if w_loc[n].ndim == 2 else list(range(w_loc[n].shape[0]))
        wmv = [as_rows(n, a[n]) for a in (w_loc, m_loc, v_loc)]
        shp3 = (len(idx),) + wmv[0].shape[-2:]
        outs = _adamw_layers([parts[(n, i)][0] for i in idx], [parts[(n, i)][1] for i in idx],
                             *[a.reshape(shp3) for a in wmv], name=f"adamw_{n}")
        for k in range(4):
            res[k][n] = as_rows(n, outs[k].reshape(wmv[0].shape))
    last = res[0]["w_kv"]
    small_got = _split_wait(False, *small_started[:4], last, name="rs_wait_small")
    for n, own, g7 in zip(SMALL_SHARDED, small4, small_got):
        shp3 = own.shape[:1] + own.shape[2:]
        outs = _adamw_layers([own], [g7], w_loc[n].reshape(shp3), m_loc[n].reshape(shp3), v_loc[n].reshape(shp3),
                             name=f"adamw_{n}")
        for k in range(4):
            res[k][n] = outs[k].reshape(w_loc[n].shape)
    outs_rp = _adamw_replicated(r_parts, _pack([w_loc[n] for n in REPL], REPL_ROWS, F32),
                                _pack([m_loc[n] for n in REPL], REPL_ROWS, F32),
                                _pack([v_loc[n] for n in REPL], REPL_ROWS, F32),
                                name="adamw_replicated")
    rp_shapes = [w_loc[n].shape for n in REPL]
    for k in range(4):
        res[k].update(zip(REPL, _unpack(outs_rp[k], rp_shapes)))
    out = [loss, grad_x]
    for k in range(4):
        out += [res[k][n] for n in WEIGHTS]
    return tuple(out)
```

```python
import math

import numpy as np
import jax
import jax.numpy as jnp
from jax import lax
from jax.experimental import pallas as pl
from jax.experimental.pallas import tpu as pltpu

F32 = jnp.float32
BF16 = jnp.bfloat16
S = jax.ShapeDtypeStruct
MESH = pl.DeviceIdType.MESH
ANY = pl.BlockSpec(memory_space=pl.ANY)

HEAD = 64
MEM_HEADS = 4
MEM_W = MEM_HEADS * HEAD
SWA_HEADS = 12
SWA_GROUP = 3
MIX_W = SWA_HEADS * HEAD
WIN = 128
LRU_C = 8.0
EPS = 1e-6
ADAM_LR, ADAM_B1, ADAM_B2, ADAM_EPS, ADAM_WD, ADAM_STEP = 0.001, 0.9, 0.999, 1e-08, 0.01, 10
GELU_C0 = math.sqrt(2.0 / math.pi)
GELU_C1 = 0.044715
N_DEV = 8
LANES = 128
CT = 128
VMEM_LIMIT = 48 * 1024 * 1024
MM_VMEM_BUDGET = 36 * 1024 * 1024
REPL_ROWS = 256

SHARDED = (("w_mem_kv", 1), ("w_mix_out", 1), ("w_ffn_up", 2), ("w_ffn_conv", 2), ("w_ffn_down", 1), ("w_in_a", 2),
           ("w_conv_a", 2), ("b_conv_a", 1), ("lru_lambda", 1), ("w_in_b", 1), ("w_kv", 0))
SMALL_SHARDED = ("w_ffn_conv", "w_conv_a", "b_conv_a", "lru_lambda")
TRANSPOSED = ("w_ffn_up", "w_in_a")
REPL = ("g_mix_pre", "g_mix_post", "g_ffn_pre", "g_ffn_post", "g_mem", "b_ffn_conv", "w_rg_r", "b_rg_r", "w_rg_i",
        "b_rg_i", "sinks_b", "g_kv")
WEIGHTS = ("g_mix_pre", "g_mix_post", "g_ffn_pre", "g_ffn_post", "g_mem", "w_mem_kv", "w_mix_out", "w_ffn_up",
           "w_ffn_conv", "b_ffn_conv", "w_ffn_down", "w_in_a", "w_conv_a", "b_conv_a", "w_rg_r", "b_rg_r", "w_rg_i",
           "b_rg_i", "lru_lambda", "w_in_b", "sinks_b", "g_kv", "w_kv")


def _alibi_slopes(n):
    def pow2(m):
        start = 2.0 ** (-8.0 / m)
        return [start ** (i + 1) for i in range(m)]
    c = 2 ** int(math.floor(math.log2(n)))
    s = pow2(c)
    if c != n:
        s = s + pow2(2 * c)[0::2][: n - c]
    return [float(v) for v in np.asarray(s, dtype=np.float32)]


SLOPES = _alibi_slopes(SWA_HEADS)


def _tile(n, cands):
    for c in cands:
        if n % c == 0:
            return c
    return n


def _cparams(*sem):
    return pltpu.CompilerParams(dimension_semantics=sem, vmem_limit_bytes=VMEM_LIMIT)


def _mm_tiles(M, N, K, a_bytes, b_bytes, o_bytes, add_bytes, offsets):
    m_off, n_offs, k_off = offsets
    tms = [c for c in (1024, 512, 256, 128) if M % c == 0 and m_off % c == 0] or [M]
    tns = [c for c in (1408, 1024, 896, 768, 512, 384, 256, 128)
           if N % c == 0 and all(o % c == 0 for o in n_offs)] or [N]
    tks = [c for c in (K, 2048, 1408, 1024, 512, 256, 128) if c <= K and K % c == 0 and k_off % c == 0]
    best = None
    for tk in tks:
        fits = []
        for tm in tms:
            for tn in tns:
                need = 2 * (tm * tk * a_bytes + tk * tn * b_bytes + tm * tn * (o_bytes + add_bytes))
                need += tm * tn * 4 * (2 if tk < K else 1)
                need += (tm * tk * 2 if a_bytes != 2 else 0) + (tk * tn * 2 if b_bytes != 2 else 0)
                if need <= MM_VMEM_BUDGET:
                    fits.append((tm * tn, min(tm, 512), tm, tn))
        if fits:
            _, _, tm, tn = max(fits)
            best = (tm, tn, tk)
            break
    assert best is not None, (M, N, K)
    return best


def _mm(a, b, *, ta=False, tb=False, n=None, k=None, b_off=(0, 0), out_dtype=F32, add=None, into=None, after=None,
        name="mm"):
    if ta:
        K, M = a.shape
    else:
        M, K = a.shape
    if tb:
        N = b.shape[-2] if n is None else n
    else:
        N = b.shape[-1] if n is None else n
    assert k is None or k == K
    ro, co = b_off
    n_off, k_off = (ro, co) if tb else (co, ro)
    oro, oco = (0, 0) if into is None else into[3]
    tm, tn, tk = _mm_tiles(M, N, K, a.dtype.itemsize, b.dtype.itemsize, jnp.dtype(out_dtype).itemsize,
                           0 if add is None else add.dtype.itemsize, (oro, (n_off, oco), k_off))
    nk = K // tk
    if tb:
        b_spec = pl.BlockSpec((tn, tk), lambda i, j, kk: (j + ro // tn, kk + co // tk))
        b_dims = (1,)
    else:
        b_spec = pl.BlockSpec((tk, tn), lambda i, j, kk: (kk + ro // tk, j + co // tn))
        b_dims = (0,)
    if ta:
        a_spec = pl.BlockSpec((tk, tm), lambda i, j, kk: (kk, i))
        a_dims = (0,)
    else:
        a_spec = pl.BlockSpec((tm, tk), lambda i, j, kk: (i, kk))
        a_dims = (1,)
    dims = ((a_dims, b_dims), ((), ()))
    add_spec = pl.BlockSpec((tm, tn), lambda i, j, kk: (i, j))
    has_add = add is not None
    if into is None:
        o_spec, o_shape, buf = add_spec, (M, N), None
    else:
        buf, o_shape, ol, _ = into
        assert not has_add
        o_spec = pl.BlockSpec((None, tm, tn), lambda i, j, kk: (ol, i + oro // tm, j + oco // tn))
    has_buf = buf is not None

    def body(*refs):
        refs = list(refs)
        acc_ref = refs.pop() if nk > 1 else None
        o_ref = refs.pop()
        a_ref, b_ref = refs[0], refs[1]
        add_ref = refs[2] if has_add else None
        part = lax.dot_general(a_ref[...].astype(BF16), b_ref[...].astype(BF16), dims, preferred_element_type=F32)

        def finish(r):
            if has_add:
                r = r + add_ref[...].astype(F32)
            o_ref[...] = r.astype(out_dtype)

        if nk == 1:
            finish(part)
        else:
            kk = pl.program_id(2)

            @pl.when(kk == 0)
            def _():
                acc_ref[...] = part

            @pl.when(kk > 0)
            def _():
                acc_ref[...] += part

            @pl.when(kk == nk - 1)
            def _():
                finish(acc_ref[...])

    in_specs = [a_spec, b_spec] + ([add_spec] if has_add else []) + ([ANY] if has_buf else [])
    args = (a, b) + ((add,) if has_add else ()) + ((buf,) if has_buf else ())
    if after is not None:
        in_specs, args = in_specs + [ANY], args + (after,)
    return pl.pallas_call(
        body, grid=(M // tm, N // tn, nk), in_specs=in_specs, out_specs=o_spec,
        out_shape=S(o_shape, out_dtype), scratch_shapes=[pltpu.VMEM((tm, tn), F32)] if nk > 1 else [],
        input_output_aliases={2: 0} if has_buf else {},
        compiler_params=_cparams("parallel", "parallel", "arbitrary"), name=name)(*args)


def _mm_sum(pieces, *, tb=False, n, out_dtype=F32, add=None, after=None, name="mm_sum"):
    M = pieces[0][0].shape[0]
    ks = [a.shape[1] for a, _, _ in pieces]
    a_bytes = max(a.dtype.itemsize for a, _, _ in pieces)
    b_bytes = max(b.dtype.itemsize for _, b, _ in pieces)
    n_offs = tuple(off[0] if tb else off[1] for _, _, off in pieces)
    for kp, (_, _, off) in zip(ks, pieces):
        assert (off[1] if tb else off[0]) % kp == 0
    tm, tn, tk = _mm_tiles(M, n, sum(ks), a_bytes, b_bytes, jnp.dtype(out_dtype).itemsize, 0, (0, n_offs, 0))
    assert tk == sum(ks)
    a_specs = [pl.BlockSpec((tm, kp), lambda i, j: (i, 0)) for kp in ks]
    if tb:
        b_specs = [pl.BlockSpec((tn, kp), lambda i, j, ro=off[0], co=off[1], kp=kp: (j + ro // tn, co // kp))
                   for kp, (_, _, off) in zip(ks, pieces)]
        dims = NT
    else:
        b_specs = [pl.BlockSpec((kp, tn), lambda i, j, ro=off[0], co=off[1], kp=kp: (ro // kp, j + co // tn))
                   for kp, (_, _, off) in zip(ks, pieces)]
        dims = (((1,), (0,)), ((), ()))
    npc = len(pieces)
    o_spec = pl.BlockSpec((tm, tn), lambda i, j: (i, j))

    def body(*refs):
        o_ref = refs[2 * npc + (add is not None) + (after is not None)]
        acc = refs[2 * npc][...].astype(F32) if add is not None else None
        for p in range(npc):
            part = lax.dot_general(refs[p][...].astype(BF16), refs[npc + p][...].astype(BF16), dims,
                                   preferred_element_type=F32)
            acc = part if acc is None else acc + part
        o_ref[...] = acc.astype(out_dtype)

    args = [a for a, _, _ in pieces] + [b for _, b, _ in pieces]
    in_specs = a_specs + b_specs
    if add is not None:
        in_specs, args = in_specs + [o_spec], args + [add]
    if after is not None:
        in_specs, args = in_specs + [ANY], args + [after]
    return pl.pallas_call(
        body, grid=(M // tm, n // tn), in_specs=in_specs, out_specs=o_spec,
        out_shape=S((M, n), out_dtype), compiler_params=_cparams("parallel", "parallel"), name=name)(*args)


def _rms_fwd(x, g, out_dtype, res=None, name="rms_fwd"):
    N, D = x.shape
    tm = _tile(N, (512, 256, 128))
    has_res = res is not None

    def body(*refs):
        if has_res:
            x_ref, g_ref, r_ref, o_ref = refs
        else:
            x_ref, g_ref, o_ref = refs
        xv = x_ref[...].astype(F32)
        y = xv * lax.rsqrt(jnp.mean(xv * xv, axis=-1, keepdims=True) + EPS) * g_ref[...]
        if has_res:
            y = y + r_ref[...]
        o_ref[...] = y.astype(out_dtype)

    row = pl.BlockSpec((tm, D), lambda i: (i, 0))
    vec = pl.BlockSpec((1, D), lambda i: (0, 0))
    return pl.pallas_call(
        body, grid=(N // tm,), in_specs=[row, vec] + ([row] if has_res else []), out_specs=row,
        out_shape=S((N, D), out_dtype), compiler_params=_cparams("parallel"), name=name)(
            *((x, g) + ((res,) if has_res else ())))


def _rms_bwd(x, g, dy, add=None, out_dtype=F32, name="rms_bwd"):
    N, D = x.shape
    tm = _tile(N, (512, 256, 128))
    has_add = add is not None

    def body(*refs):
        if has_add:
            x_ref, g_ref, dy_ref, add_ref, dx_ref, dg_ref = refs
        else:
            x_ref, g_ref, dy_ref, dx_ref, dg_ref = refs
        xv = x_ref[...].astype(F32)
        dyv = dy_ref[...].astype(F32)
        r = lax.rsqrt(jnp.mean(xv * xv, axis=-1, keepdims=True) + EPS)
        u = dyv * g_ref[...]
        dx = r * u - xv * (r * r * r * jnp.mean(u * xv, axis=-1, keepdims=True))
        if has_add:
            dx = dx + add_ref[...]
        dx_ref[...] = dx.astype(out_dtype)

        @pl.when(pl.program_id(0) == 0)
        def _():
            dg_ref[...] = jnp.zeros_like(dg_ref)

        dg_ref[...] += jnp.sum(dyv * xv * r, axis=0, keepdims=True)

    row = pl.BlockSpec((tm, D), lambda i: (i, 0))
    vec = pl.BlockSpec((1, D), lambda i: (0, 0))
    return pl.pallas_call(
        body, grid=(N // tm,), in_specs=[row, vec, row] + ([row] if has_add else []), out_specs=(row, vec),
        out_shape=(S((N, D), out_dtype), S((1, D), F32)), compiler_params=_cparams("arbitrary"), name=name)(
            *((x, g, dy) + ((add,) if has_add else ())))


def _rms_pair_fwd(y, g_post, res, gains, name):
    N, D = y.shape
    tm = _tile(N, (512, 256, 128))
    ng = len(gains)

    def body(*refs):
        y_ref, gp_ref, r_ref = refs[:3]
        g_refs = refs[3:3 + ng]
        x_ref = refs[3 + ng]
        h_refs = refs[4 + ng:]
        yv = y_ref[...]
        x = r_ref[...] + yv * lax.rsqrt(jnp.mean(yv * yv, axis=-1, keepdims=True) + EPS) * gp_ref[...]
        x_ref[...] = x
        xn = x * lax.rsqrt(jnp.mean(x * x, axis=-1, keepdims=True) + EPS)
        for g_ref, h_ref in zip(g_refs, h_refs):
            h_ref[...] = (xn * g_ref[...]).astype(BF16)

    row = pl.BlockSpec((tm, D), lambda i: (i, 0))
    vec = pl.BlockSpec((1, D), lambda i: (0, 0))
    return pl.pallas_call(
        body, grid=(N // tm,), in_specs=[row, vec, row] + [vec] * ng, out_specs=(row,) * (1 + ng),
        out_shape=(S((N, D), F32),) + (S((N, D), BF16),) * ng, compiler_params=_cparams("parallel"), name=name)(
            y, g_post, res, *gains)


def _rms_pair_bwd(xa, ga, dya, add, xb, gb, name):
    N, D = xa.shape
    tm = _tile(N, (512, 256, 128))

    def one(xv, g_ref, dyv):
        r = lax.rsqrt(jnp.mean(xv * xv, axis=-1, keepdims=True) + EPS)
        u = dyv * g_ref[...]
        dx = r * u - xv * (r * r * r * jnp.mean(u * xv, axis=-1, keepdims=True))
        return dx, jnp.sum(dyv * xv * r, axis=0, keepdims=True)

    def body(xa_ref, ga_ref, dya_ref, add_ref, xb_ref, gb_ref, da_ref, db_ref, dga_ref, dgb_ref):
        da, dga = one(xa_ref[...].astype(F32), ga_ref, dya_ref[...].astype(F32))
        da = da + add_ref[...]
        da_ref[...] = da
        db, dgb = one(xb_ref[...].astype(F32), gb_ref, da)
        db_ref[...] = db.astype(BF16)

        @pl.when(pl.program_id(0) == 0)
        def _():
            dga_ref[...] = jnp.zeros_like(dga_ref)
            dgb_ref[...] = jnp.zeros_like(dgb_ref)

        dga_ref[...] += dga
        dgb_ref[...] += dgb

    row = pl.BlockSpec((tm, D), lambda i: (i, 0))
    vec = pl.BlockSpec((1, D), lambda i: (0, 0))
    return pl.pallas_call(
        body, grid=(N // tm,), in_specs=[row, vec, row, row, row, vec], out_specs=(row, row, vec, vec),
        out_shape=(S((N, D), F32), S((N, D), BF16), S((1, D), F32), S((1, D), F32)),
        compiler_params=_cparams("arbitrary"), name=name)(xa, ga, dya, add, xb, gb)


def _shift_down(x, s, row):
    return jnp.where(row >= s, pltpu.roll(x, s, axis=0), 0.0)


def _shift_up(x, s, row):
    T = x.shape[0]
    return jnp.where(row < T - s, pltpu.roll(x, T - s, axis=0), 0.0)


SLAB = 16


def _conv_rows(x_ref, w_ref, b_ref, lo, hi):
    W = w_ref.shape[0]
    y = x_ref[lo:hi, :] * w_ref[W - 1:W, :] + b_ref[...]
    for s in range(1, W):
        y = y + x_ref[lo - s:hi - s, :] * w_ref[W - 1 - s:W - s, :]
    return y


def _taps(x_ref, W):
    T = x_ref.shape[0]
    head = x_ref[0:SLAB, :]
    row = lax.broadcasted_iota(jnp.int32, head.shape, 0)
    return [x_ref[...]] + [jnp.concatenate([_shift_down(head, s, row), x_ref[SLAB - s:T - s, :]], axis=0)
                           for s in range(1, W)]


def _conv_taps(xs, w_ref, b_ref):
    W = w_ref.shape[0]
    y = xs[0] * w_ref[W - 1:W, :] + b_ref[...]
    for s in range(1, W):
        y = y + xs[s] * w_ref[W - 1 - s:W - s, :]
    return y


def _conv_head(x_head, w_ref, b_ref):
    row = lax.broadcasted_iota(jnp.int32, x_head.shape, 0)
    return _conv_taps([x_head] + [_shift_down(x_head, s, row) for s in range(1, w_ref.shape[0])], w_ref, b_ref)


def _conv_bwd_taps(dy, xs, w_ref, row):
    W = w_ref.shape[0]
    dx = dy * w_ref[W - 1:W, :]
    dws = [None] * W
    dws[W - 1] = jnp.sum(dy * xs[0], axis=0, keepdims=True)
    for s in range(1, W):
        dx = dx + _shift_up(dy, s, row) * w_ref[W - 1 - s:W - s, :]
        dws[W - 1 - s] = jnp.sum(dy * xs[s], axis=0, keepdims=True)
    return dx, jnp.concatenate(dws, axis=0), jnp.sum(dy, axis=0, keepdims=True)


def _gelu(g):
    t = jnp.tanh(GELU_C0 * (g + GELU_C1 * g * g * g))
    return 0.5 * g * (1.0 + t), t


def _dgelu(g, t):
    return 0.5 * (1.0 + t) + 0.5 * g * (1.0 - t * t) * (GELU_C0 * (1.0 + 3.0 * GELU_C1 * g * g))


def _cspec(T, off=0, ct=CT):
    return pl.BlockSpec((1, T, ct), lambda j, b: (b, 0, j + off))


def _pspec(rows, off=0, ct=CT):
    return pl.BlockSpec((rows, ct), lambda j, b: (0, j + off))


def _conv_fwd_call(x3, x_off, C, w, b, name):
    Bl, T, _ = x3.shape
    W = w.shape[0]

    def body(x_ref, w_ref, b_ref, o_ref):
        o_ref[0, SLAB:T, :] = _conv_rows(x_ref.at[0], w_ref, b_ref, SLAB, T)
        o_ref[0, 0:SLAB, :] = _conv_head(x_ref[0, 0:SLAB, :], w_ref, b_ref)

    return pl.pallas_call(
        body, grid=(C // CT, Bl), in_specs=[_cspec(T, x_off // CT), _pspec(W), _pspec(1)], out_specs=_cspec(T),
        out_shape=S((Bl, T, C), F32), compiler_params=_cparams("parallel", "arbitrary"), name=name)(x3, w, b)


def _conv_bwd_call(dy3, x3, x_off, C, w, name):
    Bl, T, _ = x3.shape
    W = w.shape[0]

    def body(dy_ref, x_ref, w_ref, dx_ref, dw_ref, db_ref):
        row = lax.broadcasted_iota(jnp.int32, (T, CT), 0)
        dx, dw, db = _conv_bwd_taps(dy_ref[0], _taps(x_ref.at[0], W), w_ref, row)
        dx_ref[0] = dx.astype(BF16)

        @pl.when(pl.program_id(1) == 0)
        def _():
            dw_ref[...] = jnp.zeros_like(dw_ref)
            db_ref[...] = jnp.zeros_like(db_ref)

        dw_ref[...] += dw
        db_ref[...] += db

    return pl.pallas_call(
        body, grid=(C // CT, Bl), in_specs=[_cspec(T), _cspec(T, x_off // CT), _pspec(W)],
        out_specs=(_cspec(T), _pspec(W), _pspec(1)),
        out_shape=(S((Bl, T, C), BF16), S((W, C), F32), S((1, C), F32)),
        compiler_params=_cparams("parallel", "arbitrary"), name=name)(dy3, x3, w)


def _ffn_mid_fwd(u3, wc, bc, name):
    Bl, T, F2 = u3.shape
    F = F2 // 2
    nf = F // CT

    def body(ug_ref, uv_ref, wg_ref, wv_ref, bg_ref, bv_ref, o_ref):
        g = _conv_rows(ug_ref.at[0], wg_ref, bg_ref, SLAB, T)
        v = _conv_rows(uv_ref.at[0], wv_ref, bv_ref, SLAB, T)
        o_ref[0, SLAB:T, :] = (_gelu(g)[0] * v).astype(BF16)
        g = _conv_head(ug_ref[0, 0:SLAB, :], wg_ref, bg_ref)
        v = _conv_head(uv_ref[0, 0:SLAB, :], wv_ref, bv_ref)
        o_ref[0, 0:SLAB, :] = (_gelu(g)[0] * v).astype(BF16)

    return pl.pallas_call(
        body, grid=(nf, Bl),
        in_specs=[_cspec(T), _cspec(T, nf), _pspec(3), _pspec(3, nf), _pspec(1), _pspec(1, nf)], out_specs=_cspec(T),
        out_shape=S((Bl, T, F), BF16), compiler_params=_cparams("parallel", "arbitrary"), name=name)(
            u3, u3, wc, wc, bc, bc)


def _ffn_mid_bwd(u3, dact3, wc, bc, name):
    Bl, T, F2 = u3.shape
    F = F2 // 2
    nf = F // CT

    def body(ug_ref, uv_ref, da_ref, wg_ref, wv_ref, bg_ref, bv_ref, dug_ref, duv_ref, dwg_ref, dwv_ref, dbg_ref,
             dbv_ref):
        row = lax.broadcasted_iota(jnp.int32, (T, CT), 0)
        ugs = _taps(ug_ref.at[0], 3)
        uvs = _taps(uv_ref.at[0], 3)
        g = _conv_taps(ugs, wg_ref, bg_ref)
        v = _conv_taps(uvs, wv_ref, bv_ref)
        da = da_ref[0]
        gel, t = _gelu(g)
        dg = da * v * _dgelu(g, t)
        dv = da * gel
        dug, dwg, dbg = _conv_bwd_taps(dg, ugs, wg_ref, row)
        duv, dwv, dbv = _conv_bwd_taps(dv, uvs, wv_ref, row)
        dug_ref[0] = dug.astype(BF16)
        duv_ref[0] = duv.astype(BF16)

        @pl.when(pl.program_id(1) == 0)
        def _():
            dwg_ref[...] = jnp.zeros_like(dwg_ref)
            dwv_ref[...] = jnp.zeros_like(dwv_ref)
            dbg_ref[...] = jnp.zeros_like(dbg_ref)
            dbv_ref[...] = jnp.zeros_like(dbv_ref)

        dwg_ref[...] += dwg
        dwv_ref[...] += dwv
        dbg_ref[...] += dbg
        dbv_ref[...] += dbv

    return pl.pallas_call(
        body, grid=(nf, Bl),
        in_specs=[_cspec(T), _cspec(T, nf), _cspec(T), _pspec(3), _pspec(3, nf), _pspec(1), _pspec(1, nf)],
        out_specs=(_cspec(T), _cspec(T), _pspec(3), _pspec(3), _pspec(1), _pspec(1)),
        out_shape=(S((Bl, T, F), BF16), S((Bl, T, F), BF16), S((3, F), F32), S((3, F), F32), S((1, F), F32),
                   S((1, F), F32)),
        compiler_params=_cparams("parallel", "arbitrary"), name=name)(u3, u3, dact3, wc, wc, bc, bc)


def _lru_gates(xc, rp, ip, br_ref, bi_ref, lam_ref):
    r = jax.nn.sigmoid(rp + br_ref[...])
    i = jax.nn.sigmoid(ip + bi_ref[...])
    lam = lam_ref[...]
    sp = jnp.maximum(-lam, 0.0) + jnp.log1p(jnp.exp(-jnp.abs(lam)))
    log_a = (-LRU_C) * r * sp
    a = jnp.exp(log_a)
    z = 2.0 * log_a
    one_m_a2 = jnp.where(z > -0.05, -z * (1.0 + z * (0.5 + z * (1.0 / 6.0 + z * (1.0 / 24.0)))), 1.0 - a * a)
    mult = jnp.sqrt(one_m_a2)
    return r, i, sp, a, mult


SCAN_CHUNK = 64


def _scan_down(a, b):
    T = a.shape[0]
    ch = min(SCAN_CHUNK, T)
    row = lax.broadcasted_iota(jnp.int32, (ch, a.shape[1]), 0)
    outs, carry = [], None
    for c in range(T // ch):
        ac, bc = a[c * ch:(c + 1) * ch], b[c * ch:(c + 1) * ch]
        s = 1
        while s < ch:
            a_sh = jnp.where(row >= s, pltpu.roll(ac, s, axis=0), 1.0)
            bc = ac * _shift_down(bc, s, row) + bc
            ac = ac * a_sh
            s *= 2
        if carry is not None:
            bc = bc + ac * carry
        carry = bc[ch - 1:ch, :]
        outs.append(bc)
    return jnp.concatenate(outs, axis=0)


def _scan_up(an, g):
    T = an.shape[0]
    ch = min(SCAN_CHUNK, T)
    row = lax.broadcasted_iota(jnp.int32, (ch, an.shape[1]), 0)
    outs, carry = [], None
    for c in reversed(range(T // ch)):
        ac, gc = an[c * ch:(c + 1) * ch], g[c * ch:(c + 1) * ch]
        s = 1
        while s < ch:
            a_sh = jnp.where(row < ch - s, pltpu.roll(ac, ch - s, axis=0), 1.0)
            gc = ac * _shift_up(gc, s, row) + gc
            ac = ac * a_sh
            s *= 2
        if carry is not None:
            gc = gc + ac * carry
        carry = gc[0:1, :]
        outs.append(gc)
    return jnp.concatenate(outs[::-1], axis=0)


def _rglru_fwd(xc3, gates3, proj3, br, bi, lam, name):
    Bl, T, C = xc3.shape

    def body(xc_ref, rp_ref, ip_ref, ug_ref, br_ref, bi_ref, lam_ref, y_ref, h_ref):
        xc = xc_ref[0]
        r, i, sp, a, mult = _lru_gates(xc, rp_ref[0], ip_ref[0], br_ref, bi_ref, lam_ref)
        h = _scan_down(a, mult * (i * xc))
        h_ref[0] = h
        y_ref[0] = (h * _gelu(ug_ref[0])[0]).astype(BF16)

    return pl.pallas_call(
        body, grid=(C // CT, Bl),
        in_specs=[_cspec(T), _cspec(T), _cspec(T, C // CT), _cspec(T), _pspec(1), _pspec(1), _pspec(1)],
        out_specs=(_cspec(T), _cspec(T)), out_shape=(S((Bl, T, C), BF16), S((Bl, T, C), F32)),
        compiler_params=_cparams("parallel", "arbitrary"), name=name)(xc3, gates3, gates3, proj3, br, bi, lam)


def _rglru_bwd(dy3, xc3, gates3, proj3, h3, br, bi, lam, name):
    Bl, T, C = xc3.shape

    def body(dy_ref, xc_ref, rp_ref, ip_ref, ug_ref, h_ref, br_ref, bi_ref, lam_ref,
             dxc_ref, drp_ref, dip_ref, dug_ref, dbr_ref, dbi_ref, dlam_ref):
        row = lax.broadcasted_iota(jnp.int32, (T, CT), 0)
        xc = xc_ref[0]
        r, i, sp, a, mult = _lru_gates(xc, rp_ref[0], ip_ref[0], br_ref, bi_ref, lam_ref)
        h = h_ref[0]
        dy = dy_ref[0]
        ug = ug_ref[0]
        gel, t = _gelu(ug)
        dug_ref[0] = (dy * h * _dgelu(ug, t)).astype(BF16)
        gacc = _scan_up(_shift_up(a, 1, row), dy * gel)
        da = gacc * _shift_down(h, 1, row)
        ix = i * xc
        d_mult = gacc * ix
        d_i = gacc * mult * xc
        dxc_ref[0] = gacc * mult * i
        d_log_a = da * a - d_mult * (a * a) / mult
        d_r = d_log_a * ((-LRU_C) * sp)
        d_sp = jnp.sum(d_log_a * ((-LRU_C) * r), axis=0, keepdims=True)
        drp = d_r * r * (1.0 - r)
        dip = d_i * i * (1.0 - i)
        drp_ref[0] = drp.astype(BF16)
        dip_ref[0] = dip.astype(BF16)

        @pl.when(pl.program_id(1) == 0)
        def _():
            dbr_ref[...] = jnp.zeros_like(dbr_ref)
            dbi_ref[...] = jnp.zeros_like(dbi_ref)
            dlam_ref[...] = jnp.zeros_like(dlam_ref)

        dbr_ref[...] += jnp.sum(drp, axis=0, keepdims=True)
        dbi_ref[...] += jnp.sum(dip, axis=0, keepdims=True)
        dlam_ref[...] += d_sp * (-jax.nn.sigmoid(-lam_ref[...]))

    vec = S((1, C), F32)
    act = S((Bl, T, C), BF16)
    return pl.pallas_call(
        body, grid=(C // CT, Bl),
        in_specs=[_cspec(T), _cspec(T), _cspec(T), _cspec(T, C // CT), _cspec(T), _cspec(T)] + [_pspec(1)] * 3,
        out_specs=(_cspec(T), _cspec(T), _cspec(T), _cspec(T), _pspec(1), _pspec(1), _pspec(1)),
        out_shape=(S((Bl, T, C), F32), act, act, act, vec, vec, vec),
        compiler_params=_cparams("parallel", "arbitrary"), name=name)(dy3, xc3, gates3, gates3, proj3, h3, br, bi, lam)


NT = (((1,), (1,)), ((), ()))
TN = (((0,), (0,)), ((), ()))


def _hs(h):
    return slice(h * HEAD, (h + 1) * HEAD)


def _head_rows(x):
    head = lax.shift_right_logical(lax.broadcasted_iota(jnp.int32, x.shape, 1), HEAD.bit_length() - 1)
    return jnp.concatenate([jnp.where(head == h, x, jnp.zeros_like(x)) for h in range(MEM_HEADS)], axis=0)


def _head_sum(xbd):
    M = xbd.shape[0] // MEM_HEADS
    head = lax.shift_right_logical(lax.broadcasted_iota(jnp.int32, (M, xbd.shape[1]), 1), HEAD.bit_length() - 1)
    out = jnp.zeros((M, xbd.shape[1]), xbd.dtype)
    for h in range(MEM_HEADS):
        out = jnp.where(head == h, xbd[h * M:(h + 1) * M], out)
    return out


def _mem_probs(q, kbd):
    M = kbd.shape[0] // MEM_HEADS
    s = lax.dot_general(q, kbd, NT, preferred_element_type=F32) * (HEAD ** -0.5)
    ps = []
    for h in range(MEM_HEADS):
        sh = s[:, h * M:(h + 1) * M]
        e = jnp.exp(sh - jnp.max(sh, axis=-1, keepdims=True))
        ps.append(e / jnp.sum(e, axis=-1, keepdims=True))
    return ps


def _mem_attn_fwd(proj3, q_off, mkv3, name):
    Bl, T, _ = proj3.shape
    M = mkv3.shape[1]
    tq = _tile(T, (512, 256, 128))

    def body(q_ref, k_ref, v_ref, o_ref):
        q = q_ref[0].astype(BF16)
        kbd = _head_rows(k_ref[0].astype(BF16))
        vbd = _head_rows(v_ref[0].astype(BF16))
        p = jnp.concatenate(_mem_probs(q, kbd), axis=-1).astype(BF16)
        o_ref[0] = jnp.dot(p, vbd, preferred_element_type=F32).astype(BF16)

    return pl.pallas_call(
        body, grid=(Bl, T // tq),
        in_specs=[pl.BlockSpec((1, tq, MEM_W), lambda b, t: (b, t, q_off // MEM_W)),
                  pl.BlockSpec((1, M, MEM_W), lambda b, t: (b, 0, 0)),
                  pl.BlockSpec((1, M, MEM_W), lambda b, t: (b, 0, 1))],
        out_specs=pl.BlockSpec((1, tq, MEM_W), lambda b, t: (b, t, 0)),
        out_shape=S((Bl, T, MEM_W), BF16), compiler_params=_cparams("parallel", "parallel"), name=name)(
            proj3, mkv3, mkv3)


def _mem_attn_bwd(proj3, q_off, mkv3, do3, name):
    Bl, T, _ = proj3.shape
    M = mkv3.shape[1]
    tq = _tile(T, (512, 256, 128))
    scale = HEAD ** -0.5

    def body(q_ref, k_ref, v_ref, do_ref, dq_ref, dkv_ref):
        q = q_ref[0].astype(BF16)
        kbd = _head_rows(k_ref[0].astype(BF16))
        vbd = _head_rows(v_ref[0].astype(BF16))
        do = do_ref[0].astype(BF16)
        ps = _mem_probs(q, kbd)
        dvbd = lax.dot_general(jnp.concatenate(ps, axis=-1).astype(BF16), do, TN, preferred_element_type=F32)
        dp = lax.dot_general(do, vbd, NT, preferred_element_type=F32)
        dss = []
        for h in range(MEM_HEADS):
            dph = dp[:, h * M:(h + 1) * M]
            dss.append(ps[h] * (dph - jnp.sum(ps[h] * dph, axis=-1, keepdims=True)) * scale)
        ds = jnp.concatenate(dss, axis=-1).astype(BF16)
        dq_ref[0] = jnp.dot(ds, kbd, preferred_element_type=F32).astype(BF16)
        dkbd = lax.dot_general(ds, q, TN, preferred_element_type=F32)

        @pl.when(pl.program_id(1) == 0)
        def _():
            dkv_ref[...] = jnp.zeros_like(dkv_ref)

        dkv_ref[0] += jnp.concatenate([_head_sum(dkbd), _head_sum(dvbd)], axis=-1)

    return pl.pallas_call(
        body, grid=(Bl, T // tq),
        in_specs=[pl.BlockSpec((1, tq, MEM_W), lambda b, t: (b, t, q_off // MEM_W)),
                  pl.BlockSpec((1, M, MEM_W), lambda b, t: (b, 0, 0)),
                  pl.BlockSpec((1, M, MEM_W), lambda b, t: (b, 0, 1)),
                  pl.BlockSpec((1, tq, MEM_W), lambda b, t: (b, t, 0))],
        out_specs=(pl.BlockSpec((1, tq, MEM_W), lambda b, t: (b, t, 0)),
                   pl.BlockSpec((1, M, 2 * MEM_W), lambda b, t: (b, 0, 0))),
        out_shape=(S((Bl, T, MEM_W), BF16), S((Bl, M, 2 * MEM_W), F32)),
        compiler_params=_cparams("parallel", "arbitrary"), name=name)(proj3, mkv3, mkv3, do3)


GROUP_ROWS = SWA_GROUP * WIN


def _group_rows(x, kvh):
    return jnp.concatenate([x[:, _hs(SWA_GROUP * kvh + g)] for g in range(SWA_GROUP)], axis=0)


def _group_col(vals):
    grp = lax.shift_right_logical(lax.broadcasted_iota(jnp.int32, (GROUP_ROWS, 1), 0), WIN.bit_length() - 1)
    col = jnp.full((GROUP_ROWS, 1), vals[-1], F32)
    for g in range(SWA_GROUP - 2, -1, -1):
        col = jnp.where(grp == g, vals[g], col)
    return col


def _swa_probs(qh, kph, kch, sink, slope, has_prev):
    qi = jnp.bitwise_and(lax.broadcasted_iota(jnp.int32, (GROUP_ROWS, WIN), 0), WIN - 1)
    kj = lax.broadcasted_iota(jnp.int32, (GROUP_ROWS, WIN), 1)
    scale = HEAD ** -0.5
    sp = lax.dot_general(qh, kph, NT, preferred_element_type=F32) * scale
    sc = lax.dot_general(qh, kch, NT, preferred_element_type=F32) * scale
    dist_p = (qi + WIN - kj).astype(F32)
    dist_c = (qi - kj).astype(F32)
    neg = -jnp.inf
    sp = jnp.where(kj > qi + jnp.where(has_prev, 0, WIN), sp - slope * dist_p, neg)
    sc = jnp.where(kj <= qi, sc - slope * dist_c, neg)
    m = jnp.maximum(jnp.maximum(jnp.max(sp, axis=-1, keepdims=True), jnp.max(sc, axis=-1, keepdims=True)), sink)
    ep = jnp.exp(sp - m)
    ec = jnp.exp(sc - m)
    es = jnp.exp(sink - m)
    inv = 1.0 / (jnp.sum(ep, axis=-1, keepdims=True) + jnp.sum(ec, axis=-1, keepdims=True) + es)
    return ep * inv, ec * inv, es * inv


def _swa_specs(nb):
    prev = lambda n: jnp.maximum(n - 1, 0)
    q = pl.BlockSpec((1, WIN, MIX_W), lambda b, n: (b, n, 0))
    kp = pl.BlockSpec((1, WIN, MEM_W), lambda b, n: (b, prev(n), 0))
    kc = pl.BlockSpec((1, WIN, MEM_W), lambda b, n: (b, n, 0))
    vp = pl.BlockSpec((1, WIN, MEM_W), lambda b, n: (b, prev(n), 1))
    vc = pl.BlockSpec((1, WIN, MEM_W), lambda b, n: (b, n, 1))
    sm = pl.BlockSpec(memory_space=pltpu.SMEM)
    return q, kp, kc, vp, vc, sm


def _swa_fwd(proj3, kv3, sinks, name):
    Bl, T, _ = proj3.shape
    nb = T // WIN
    q_s, kp_s, kc_s, vp_s, vc_s, sm = _swa_specs(nb)

    def body(q_ref, kp_ref, kc_ref, vp_ref, vc_ref, sink_ref, o_ref):
        has_prev = pl.program_id(1) > 0
        q = q_ref[0].astype(BF16)
        kp, kc = kp_ref[0].astype(BF16), kc_ref[0].astype(BF16)
        vp, vc = vp_ref[0].astype(BF16), vc_ref[0].astype(BF16)
        outs = []
        for kvh in range(SWA_HEADS // SWA_GROUP):
            kvs = _hs(kvh)
            heads = range(SWA_GROUP * kvh, SWA_GROUP * (kvh + 1))
            pp, pc, _ = _swa_probs(_group_rows(q, kvh), kp[:, kvs], kc[:, kvs], _group_col([sink_ref[h] for h in heads]),
                                   _group_col([SLOPES[h] for h in heads]), has_prev)
            og = (jnp.dot(pp.astype(BF16), vp[:, kvs], preferred_element_type=F32)
                  + jnp.dot(pc.astype(BF16), vc[:, kvs], preferred_element_type=F32))
            outs += [og[g * WIN:(g + 1) * WIN] for g in range(SWA_GROUP)]
        o_ref[0] = jnp.concatenate(outs, axis=-1).astype(BF16)

    return pl.pallas_call(
        body, grid=(Bl, nb), in_specs=[q_s, kp_s, kc_s, vp_s, vc_s, sm], out_specs=q_s,
        out_shape=S((Bl, T, MIX_W), BF16), compiler_params=_cparams("parallel", "parallel"), name=name)(
            proj3, kv3, kv3, kv3, kv3, sinks)


def _swa_bwd(proj3, kv3, sinks, do3, name):
    Bl, T, _ = proj3.shape
    nb = T // WIN
    q_s, kp_s, kc_s, vp_s, vc_s, sm = _swa_specs(nb)
    kv_s = pl.BlockSpec((1, WIN, 2 * MEM_W), lambda b, n: (b, n, 0))
    sk_s = pl.BlockSpec((8, LANES), lambda b, n: (0, 0))
    scale = HEAD ** -0.5

    def body(q_ref, kp_ref, kc_ref, vp_ref, vc_ref, sink_ref, do_ref, dq_ref, dkc_ref, dkp_ref, dsk_ref):
        has_prev = pl.program_id(1) > 0
        q = q_ref[0].astype(BF16)
        kp, kc = kp_ref[0].astype(BF16), kc_ref[0].astype(BF16)
        vp, vc = vp_ref[0].astype(BF16), vc_ref[0].astype(BF16)
        do = do_ref[0].astype(BF16)
        lane = lax.broadcasted_iota(jnp.int32, (8, LANES), 1)
        srow = lax.broadcasted_iota(jnp.int32, (8, LANES), 0)
        dsk = jnp.zeros((8, LANES), F32)
        dqs = []
        dkc, dkp, dvc, dvp = [], [], [], []
        grp = lax.shift_right_logical(lax.broadcasted_iota(jnp.int32, (GROUP_ROWS, 1), 0), WIN.bit_length() - 1)
        for kvh in range(SWA_HEADS // SWA_GROUP):
            kvs = _hs(kvh)
            heads = range(SWA_GROUP * kvh, SWA_GROUP * (kvh + 1))
            qg, dog = _group_rows(q, kvh), _group_rows(do, kvh)
            pp, pc, ps = _swa_probs(qg, kp[:, kvs], kc[:, kvs], _group_col([sink_ref[h] for h in heads]),
                                    _group_col([SLOPES[h] for h in heads]), has_prev)
            dpp = lax.dot_general(dog, vp[:, kvs], NT, preferred_element_type=F32)
            dpc = lax.dot_general(dog, vc[:, kvs], NT, preferred_element_type=F32)
            delta = jnp.sum(pp * dpp, axis=-1, keepdims=True) + jnp.sum(pc * dpc, axis=-1, keepdims=True)
            dsp = (pp * (dpp - delta) * scale).astype(BF16)
            dsc = (pc * (dpc - delta) * scale).astype(BF16)
            dqg = (jnp.dot(dsp, kp[:, kvs], preferred_element_type=F32)
                   + jnp.dot(dsc, kc[:, kvs], preferred_element_type=F32))
            dqs += [dqg[g * WIN:(g + 1) * WIN] for g in range(SWA_GROUP)]
            dkc.append(lax.dot_general(dsc, qg, TN, preferred_element_type=F32))
            dkp.append(lax.dot_general(dsp, qg, TN, preferred_element_type=F32))
            dvc.append(lax.dot_general(pc.astype(BF16), dog, TN, preferred_element_type=F32))
            dvp.append(lax.dot_general(pp.astype(BF16), dog, TN, preferred_element_type=F32))
            dsink = ps * delta
            for g, h in enumerate(heads):
                dsk = dsk + jnp.where((lane == h) & (srow == 0), -jnp.sum(jnp.where(grp == g, dsink, 0.0)), 0.0)
        dq_ref[0] = jnp.concatenate(dqs, axis=-1).astype(BF16)
        dkc_ref[0] = jnp.concatenate(dkc + dvc, axis=-1)
        dkp_ref[0] = jnp.concatenate(dkp + dvp, axis=-1)

        @pl.when((pl.program_id(0) == 0) & (pl.program_id(1) == 0))
        def _():
            dsk_ref[...] = jnp.zeros_like(dsk_ref)

        dsk_ref[...] += dsk

    return pl.pallas_call(
        body, grid=(Bl, nb), in_specs=[q_s, kp_s, kc_s, vp_s, vc_s, sm, q_s], out_specs=(q_s, kv_s, kv_s, sk_s),
        out_shape=(S((Bl, T, MIX_W), BF16), S((Bl, T, 2 * MEM_W), F32), S((Bl, T, 2 * MEM_W), F32), S((8, LANES), F32)),
        compiler_params=_cparams("arbitrary", "arbitrary"), name=name)(proj3, kv3, kv3, kv3, kv3, sinks, do3)


def _kv_grad_combine(parts, name):
    Bl, T, W = parts[0][0].shape
    nb = T // WIN
    nl = len(parts)

    def body(*refs):
        o_ref = refs[-1]
        has_next = jnp.where(pl.program_id(1) == nb - 1, 0.0, 1.0)
        acc = None
        for l in range(nl):
            c = refs[2 * l][0] + has_next * refs[2 * l + 1][0]
            acc = c if acc is None else acc + c
        o_ref[0] = acc.astype(BF16)

    cur = pl.BlockSpec((1, WIN, W), lambda b, n: (b, n, 0))
    nxt = pl.BlockSpec((1, WIN, W), lambda b, n: (b, jnp.minimum(n + 1, nb - 1), 0))
    return pl.pallas_call(
        body, grid=(Bl, nb), in_specs=[cur, nxt] * nl, out_specs=cur, out_shape=S((Bl, T, W), BF16),
        compiler_params=_cparams("parallel", "parallel"), name=name)(*[a for pr in parts for a in pr])


def _loss_bwd(y, target, name="loss"):
    N, D = y.shape
    tm = _tile(N, (512, 256, 128))

    def body(y_ref, t_ref, dy_ref, l_ref):
        e = y_ref[...] - t_ref[...]
        dy_ref[...] = e * (1.0 / D)

        @pl.when(pl.program_id(0) == 0)
        def _():
            l_ref[...] = jnp.zeros_like(l_ref)

        l_ref[...] += jnp.sum(e * e, axis=0, keepdims=True) * (0.5 / D)

    row = pl.BlockSpec((tm, D), lambda i: (i, 0))
    vec = pl.BlockSpec((1, D), lambda i: (0, 0))
    return pl.pallas_call(
        body, grid=(N // tm,), in_specs=[row, row], out_specs=(row, vec), out_shape=(S((N, D), F32), S((1, D), F32)),
        compiler_params=_cparams("arbitrary"), name=name)(y, target)


def _all_gather(x, name):
    R, C = x.shape

    def body(x_ref, out_ref, send_sems, recv_sems, local_sem):
        mx, my, mc = lax.axis_index("x"), lax.axis_index("y"), lax.axis_index("c")
        me, sibling = (mx, my, mc), (mx, my, 1 - mc)
        chips = [(1 - mx, my), (mx, 1 - my), (1 - mx, 1 - my)]

        def rows(px, py, pc):
            return out_ref.at[4 * px + 2 * py + pc]

        def copy(kk, block, to, src=None):
            return pltpu.make_async_remote_copy(
                src_ref=rows(*block) if src is None else src, dst_ref=rows(*block), send_sem=send_sems.at[kk],
                recv_sem=recv_sems.at[kk], device_id=to, device_id_type=MESH)

        mine = pltpu.make_async_copy(x_ref, rows(*me), local_sem)
        mine.start()
        first = [copy(0, me, sibling, src=x_ref)]
        first += [copy(1 + j, me, (*chip, mc), src=x_ref) for j, chip in enumerate(chips)]
        for cp in first:
            cp.start()
        passed = [copy(4 + j, (*chip, mc), sibling) for j, chip in enumerate(chips)]
        for j, chip in enumerate(chips):
            copy(1 + j, (*chip, mc), me).wait_recv()
            passed[j].start()
        copy(0, sibling, me).wait_recv()
        for j, chip in enumerate(chips):
            copy(4 + j, (*chip, 1 - mc), me).wait_recv()
        for cp in first + passed:
            cp.wait_send()
        mine.wait()

    return pl.pallas_call(
        body, out_shape=S((N_DEV, R, C), x.dtype), in_specs=[ANY], out_specs=ANY,
        scratch_shapes=[pltpu.SemaphoreType.DMA((7,)), pltpu.SemaphoreType.DMA((7,)), pltpu.SemaphoreType.DMA(())],
        name=name)(x)


def _ag_weights(shards, row_sharded, name):
    n = len(shards)

    def full_shape(a, rows):
        if rows:
            return a.shape[:-2] + (N_DEV * a.shape[-2],) + a.shape[-1:]
        return (N_DEV,) + a.shape

    def body(*refs):
        x_refs, o_refs = refs[:n], refs[n:2 * n]
        send_sems, recv_sems, local_sems = refs[2 * n:]
        mx, my, mc = lax.axis_index("x"), lax.axis_index("y"), lax.axis_index("c")
        me, sibling = (mx, my, mc), (mx, my, 1 - mc)
        chips = [(1 - mx, my), (mx, 1 - my), (1 - mx, 1 - my)]

        def dst(t, px, py, pc):
            d = 4 * px + 2 * py + pc
            if not row_sharded[t]:
                return o_refs[t].at[d]
            r = shards[t].shape[-2]
            idx = (slice(None),) * (shards[t].ndim - 2) + (pl.ds(pl.multiple_of(d * r, 16), r), slice(None))
            return o_refs[t].at[idx]

        def copy(kk, t, block, to, src=None):
            return pltpu.make_async_remote_copy(
                src_ref=dst(t, *block) if src is None else src, dst_ref=dst(t, *block),
                send_sem=send_sems.at[kk * n + t], recv_sem=recv_sems.at[kk * n + t], device_id=to,
                device_id_type=MESH)

        mine = [pltpu.make_async_copy(x_refs[t], dst(t, *me), local_sems.at[t]) for t in range(n)]
        for cp in mine:
            cp.start()
        first = []
        for t in range(n):
            first.append(copy(0, t, me, sibling, src=x_refs[t]))
            first += [copy(1 + j, t, me, (*chip, mc), src=x_refs[t]) for j, chip in enumerate(chips)]
        for cp in first:
            cp.start()
        passed = []
        for j, chip in enumerate(chips):
            for t in range(n):
                copy(1 + j, t, (*chip, mc), me).wait_recv()
                cp = copy(4 + j, t, (*chip, mc), sibling)
                cp.start()
                passed.append(cp)
        for t in range(n):
            copy(0, t, sibling, me).wait_recv()
            for j, chip in enumerate(chips):
                copy(4 + j, t, (*chip, 1 - mc), me).wait_recv()
        for cp in first + passed:
            cp.wait_send()
        for cp in mine:
            cp.wait()

    return pl.pallas_call(
        body, out_shape=tuple(S(full_shape(a, r), a.dtype) for a, r in zip(shards, row_sharded)),
        in_specs=[ANY] * n, out_specs=tuple([ANY] * n),
        scratch_shapes=[pltpu.SemaphoreType.DMA((7 * n,)), pltpu.SemaphoreType.DMA((7 * n,)),
                        pltpu.SemaphoreType.DMA((n,))],
        name=name)(*shards)


FLIPS = [(fx, fy, fc) for fx in (0, 1) for fy in (0, 1) for fc in (0, 1)][1:]
HBM = pl.BlockSpec(memory_space=pltpu.HBM)
SEM = pl.BlockSpec(memory_space=pltpu.SEMAPHORE)
EFFECT = pltpu.SideEffectType.DATAFLOW_SIDE_EFFECTING


def _hbm(a):
    return pltpu.with_memory_space_constraint(a, pltpu.HBM)


def _flips(gather):
    return [(0, 0, 0)] + FLIPS if gather else FLIPS


def _split_copies(gather, s_refs, l_refs, send_sems, recv_sems):
    n = len(s_refs)
    mx, my, mc = lax.axis_index("x"), lax.axis_index("y"), lax.axis_index("c")
    me = 4 * mx + 2 * my + mc
    copies = []
    for k, (fx, fy, fc) in enumerate(_flips(gather)):
        px, py, pc = (1 - mx if fx else mx), (1 - my if fy else my), (1 - mc if fc else mc)
        for t in range(n):
            if gather:
                src = s_refs[t]
                r = src.shape[0]
                dst = l_refs[t].at[pl.ds(pl.multiple_of(me * r, 16), r), :]
            else:
                src = s_refs[t].at[:, 4 * px + 2 * py + pc]
                dst = l_refs[t].at[k]
            copies.append(pltpu.make_async_remote_copy(
                src_ref=src, dst_ref=dst, send_sem=send_sems.at[k * n + t], recv_sem=recv_sems.at[k * n + t],
                device_id=(px, py, pc), device_id_type=MESH))
    return copies


def _split_start(gather, srcs, lands, after, name):
    n = len(srcs)
    n_sem = len(_flips(gather)) * n

    def body(*refs):
        s_refs, l_refs = refs[:n], refs[n:2 * n]
        send_sems, recv_sems = refs[2 * n + 1], refs[2 * n + 2]
        token = refs[-1]
        for cp in _split_copies(gather, s_refs, l_refs, send_sems, recv_sems):
            cp.start()
        token[...] = jnp.zeros_like(token)

    outs = pl.pallas_call(
        body, name=name,
        out_shape=(pltpu.SemaphoreType.DMA((n_sem,)), pltpu.SemaphoreType.DMA((n_sem,)))
        + tuple(pltpu.HBM(a.shape, a.dtype) for a in lands) + (S((8, LANES), F32),),
        in_specs=[HBM] * (2 * n) + [ANY],
        out_specs=(SEM, SEM) + (HBM,) * n + (pl.BlockSpec(memory_space=pltpu.VMEM),),
        input_output_aliases={n + i: 2 + i for i in range(n)},
        compiler_params=pltpu.CompilerParams(has_side_effects=EFFECT),
    )(*[_hbm(a) for a in srcs], *[_hbm(a) for a in lands], after)
    return outs[0], outs[1], list(srcs), list(outs[2:2 + n]), outs[-1]


def _split_wait(gather, send_sems, recv_sems, srcs, lands, after, name):
    n = len(srcs)

    def body(*refs):
        s_refs, l_refs = refs[:n], refs[n:2 * n]
        ssem, rsem = refs[2 * n], refs[2 * n + 1]
        copies = _split_copies(gather, s_refs, l_refs, ssem, rsem)
        for cp in copies:
            cp.wait_send()
        for cp in copies:
            cp.wait_recv()

    outs = pl.pallas_call(
        body, name=name, out_shape=tuple(pltpu.HBM(a.shape, a.dtype) for a in lands),
        in_specs=[HBM] * (2 * n) + [SEM, SEM, ANY], out_specs=(HBM,) * n,
        input_output_aliases={n + i: i for i in range(n)},
        compiler_params=pltpu.CompilerParams(has_side_effects=EFFECT),
    )(*[_hbm(a) for a in srcs], *lands, send_sems, recv_sems, after)
    return list(outs)


def _adamw_math(w, g, m, v):
    m = ADAM_B1 * m + (1.0 - ADAM_B1) * g
    v = ADAM_B2 * v + (1.0 - ADAM_B2) * (g * g)
    m_hat = m / (1.0 - ADAM_B1 ** ADAM_STEP)
    v_hat = v / (1.0 - ADAM_B2 ** ADAM_STEP)
    delta = -ADAM_LR * (m_hat / (jnp.sqrt(v_hat) + ADAM_EPS) + ADAM_WD * w)
    return delta, m, v


def _adamw_layers(owns, gots, w, m, v, name):
    L, B, C = w.shape
    per_row = 2 * L * len(FLIPS) * C * owns[0].dtype.itemsize
    tb = max([t for t in range(16, B + 1, 16) if B % t == 0 and (t * per_row <= 16 * 1024 * 1024 or t == 16)] or [B])
    me = (4 * lax.axis_index("x") + 2 * lax.axis_index("y") + lax.axis_index("c")).astype(jnp.int32).reshape(1)

    def body(me_ref, *refs):
        own_refs, got_refs = refs[:L], refs[L:2 * L]
        w_ref, m_ref, v_ref = refs[2 * L:2 * L + 3]
        g_out, d_out, m_out, v_out = refs[2 * L + 3:]
        layer = pl.program_id(0)
        for kk in range(L):
            @pl.when(layer == kk)
            def _():
                g = own_refs[kk][0].astype(F32)
                for s in range(len(FLIPS)):
                    g = g + got_refs[kk][s].astype(F32)
                d, mn, vn = _adamw_math(w_ref[...], g, m_ref[...], v_ref[...])
                g_out[...] = g
                d_out[...] = d
                m_out[...] = mn
                v_out[...] = vn

    def row(kk, layer, i):
        return jnp.where(layer == kk, i, 0)

    blk = pl.BlockSpec((1, tb, C), lambda layer, i, me_ref: (layer, i, 0))
    own_specs = [pl.BlockSpec((1, 1, tb, C), lambda layer, i, me_ref, kk=kk: (0, me_ref[0], row(kk, layer, i), 0))
                 for kk in range(L)]
    got_specs = [pl.BlockSpec((len(FLIPS), 1, tb, C), lambda layer, i, me_ref, kk=kk: (0, 0, row(kk, layer, i), 0))
                 for kk in range(L)]
    return pl.pallas_call(
        body,
        grid_spec=pltpu.PrefetchScalarGridSpec(
            num_scalar_prefetch=1, grid=(L, B // tb), in_specs=own_specs + got_specs + [blk, blk, blk],
            out_specs=(blk, blk, blk, blk)),
        out_shape=(S((L, B, C), F32),) * 4, compiler_params=_cparams("arbitrary", "arbitrary"), name=name)(
            me, *owns, *gots, w, m, v)


def _adamw_replicated(parts, w, m, v, name):
    R, C = w.shape
    rb = _tile(R, (512, 256, 128, 64, 32, 16))

    def body(p_ref, w_ref, m_ref, v_ref, g_out, d_out, m_out, v_out):
        g = p_ref[0].astype(F32)
        for j in range(1, N_DEV):
            g = g + p_ref[j].astype(F32)
        d, mn, vn = _adamw_math(w_ref[...], g, m_ref[...], v_ref[...])
        g_out[...] = g
        d_out[...] = d
        m_out[...] = mn
        v_out[...] = vn

    blk = pl.BlockSpec((rb, C), lambda i: (i, 0))
    return pl.pallas_call(
        body, grid=(R // rb,), in_specs=[pl.BlockSpec((N_DEV, rb, C), lambda i: (0, i, 0)), blk, blk, blk],
        out_specs=(blk, blk, blk, blk), out_shape=(S((R, C), F32),) * 4, compiler_params=_cparams("parallel"),
        name=name)(parts, w, m, v)


def _pack(arrs, rows_mult, dtype):
    flat = jnp.concatenate([a.reshape(-1).astype(dtype) for a in arrs])
    n = flat.shape[0]
    per = rows_mult * LANES
    tot = -(-n // per) * per
    return jnp.pad(flat, (0, tot - n)).reshape(tot // LANES, LANES)


def _unpack(blob, shapes):
    flat = blob.reshape(-1)
    out, off = [], 0
    for shp in shapes:
        n = int(np.prod(shp))
        out.append(flat[off:off + n].reshape(shp))
        off += n
    return out


def _small_to_natural(g8):
    t = jnp.moveaxis(g8, 0, -2)
    return t.reshape(t.shape[:-2] + (N_DEV * t.shape[-1],))


def _small_to_cols(g):
    t = g.reshape(g.shape[:-1] + (N_DEV, g.shape[-1] // N_DEV))
    return jnp.moveaxis(t, -2, 0)


def _block_diag(w):
    nb, bs, _ = w.shape
    eye = jnp.eye(nb, dtype=w.dtype)
    return (eye[:, None, :, None] * w[:, :, None, :]).reshape(nb * bs, nb * bs)


def _diag_blocks(d, nb, bs):
    d4 = d.reshape(nb, bs, nb, bs)
    return jnp.stack([d4[i, :, i, :] for i in range(nb)])


def kernel(x, mem, g_mix_pre, g_mix_post, g_ffn_pre, g_ffn_post, g_mem, w_mem_kv, w_mix_out, w_ffn_up, w_ffn_conv, b_ffn_conv, w_ffn_down, w_in_a, w_conv_a, b_conv_a, w_rg_r, b_rg_r, w_rg_i, b_rg_i, lru_lambda, w_in_b, sinks_b, g_kv, w_kv, loss_target, m_g_mix_pre, m_g_mix_post, m_g_ffn_pre, m_g_ffn_post, m_g_mem, m_w_mem_kv, m_w_mix_out, m_w_ffn_up, m_w_ffn_conv, m_b_ffn_conv, m_w_ffn_down, m_w_in_a, m_w_conv_a, m_b_conv_a, m_w_rg_r, m_b_rg_r, m_w_rg_i, m_b_rg_i, m_lru_lambda, m_w_in_b, m_sinks_b, m_g_kv, m_w_kv, v_g_mix_pre, v_g_mix_post, v_g_ffn_pre, v_g_ffn_post, v_g_mem, v_w_mem_kv, v_w_mix_out, v_w_ffn_up, v_w_ffn_conv, v_b_ffn_conv, v_w_ffn_down, v_w_in_a, v_w_conv_a, v_b_conv_a, v_w_rg_r, v_b_rg_r, v_w_rg_i, v_b_rg_i, v_lru_lambda, v_w_in_b, v_sinks_b, v_g_kv, v_w_kv):
    w_loc = dict(g_mix_pre=g_mix_pre, g_mix_post=g_mix_post, g_ffn_pre=g_ffn_pre, g_ffn_post=g_ffn_post, g_mem=g_mem,
                 w_mem_kv=w_mem_kv, w_mix_out=w_mix_out, w_ffn_up=w_ffn_up, w_ffn_conv=w_ffn_conv,
                 b_ffn_conv=b_ffn_conv, w_ffn_down=w_ffn_down, w_in_a=w_in_a, w_conv_a=w_conv_a, b_conv_a=b_conv_a,
                 w_rg_r=w_rg_r, b_rg_r=b_rg_r, w_rg_i=w_rg_i, b_rg_i=b_rg_i, lru_lambda=lru_lambda, w_in_b=w_in_b,
                 sinks_b=sinks_b, g_kv=g_kv, w_kv=w_kv)
    m_loc = dict(g_mix_pre=m_g_mix_pre, g_mix_post=m_g_mix_post, g_ffn_pre=m_g_ffn_pre, g_ffn_post=m_g_ffn_post,
                 g_mem=m_g_mem, w_mem_kv=m_w_mem_kv, w_mix_out=m_w_mix_out, w_ffn_up=m_w_ffn_up,
                 w_ffn_conv=m_w_ffn_conv, b_ffn_conv=m_b_ffn_conv, w_ffn_down=m_w_ffn_down, w_in_a=m_w_in_a,
                 w_conv_a=m_w_conv_a, b_conv_a=m_b_conv_a, w_rg_r=m_w_rg_r, b_rg_r=m_b_rg_r, w_rg_i=m_w_rg_i,
                 b_rg_i=m_b_rg_i, lru_lambda=m_lru_lambda, w_in_b=m_w_in_b, sinks_b=m_sinks_b, g_kv=m_g_kv,
                 w_kv=m_w_kv)
    v_loc = dict(g_mix_pre=v_g_mix_pre, g_mix_post=v_g_mix_post, g_ffn_pre=v_g_ffn_pre, g_ffn_post=v_g_ffn_post,
                 g_mem=v_g_mem, w_mem_kv=v_w_mem_kv, w_mix_out=v_w_mix_out, w_ffn_up=v_w_ffn_up,
                 w_ffn_conv=v_w_ffn_conv, b_ffn_conv=v_b_ffn_conv, w_ffn_down=v_w_ffn_down, w_in_a=v_w_in_a,
                 w_conv_a=v_w_conv_a, b_conv_a=v_b_conv_a, w_rg_r=v_w_rg_r, b_rg_r=v_b_rg_r, w_rg_i=v_w_rg_i,
                 b_rg_i=v_b_rg_i, lru_lambda=v_lru_lambda, w_in_b=v_w_in_b, sinks_b=v_sinks_b, g_kv=v_g_kv,
                 w_kv=v_w_kv)

    Bl, T, D = x.shape
    Ml = mem.shape[1]
    N = Bl * T
    depth = g_mix_pre.shape[0]
    n_a = w_in_a.shape[0]
    F = w_ffn_down.shape[1] * N_DEV
    def as_rows(n, a):
        return jnp.swapaxes(a, -1, -2) if n in TRANSPOSED else a

    def mix_keys(l):
        keys = [("w_mem_kv", l), ("w_mix_out", l), ("w_in_a", l) if l < n_a else ("w_in_b", l - n_a)]
        return keys + ([("w_kv", None)] if l == n_a else [])

    def ffn_keys(l):
        return [("w_ffn_up", l), ("w_ffn_down", l)]

    def shard_of(key):
        n, i = key
        return as_rows(n, w_loc[n] if i is None else w_loc[n][i]).astype(BF16)

    W = {}
    keys0 = mix_keys(0)
    got0 = _ag_weights([shard_of(kk) for kk in keys0] + [w_loc[n] for n in SMALL_SHARDED],
                       [True] * len(keys0) + [False] * len(SMALL_SHARDED), name="ag_weights_0")
    W.update(zip(keys0, got0))
    for n, a in zip(SMALL_SHARDED, got0[len(keys0):]):
        W[n] = _small_to_natural(a)

    def gather_start(keys, after, tag):
        shards = [shard_of(kk) for kk in keys]
        lands = [lax.empty((N_DEV * s.shape[0],) + s.shape[1:], s.dtype) for s in shards]
        return (keys, tag) + _split_start(True, shards, lands, after, name=f"ag_start_{tag}")

    def gather_wait(pending, after):
        keys, tag, ssem, rsem, srcs, lands, _ = pending
        W.update(zip(keys, _split_wait(True, ssem, rsem, srcs, lands, after, name=f"ag_wait_{tag}")))

    pending_ffn = gather_start(ffn_keys(0), got0[0], "ffn_0")

    nblk, bsz = w_rg_r.shape[1], w_rg_r.shape[2]
    wbd = [jnp.concatenate([_block_diag(w_rg_r[j]), _block_diag(w_rg_i[j])], axis=1).astype(BF16) for j in range(n_a)]

    def vec(a):
        return a.reshape(1, -1)

    x2 = x.reshape(N, D)
    mem2 = mem.reshape(Bl * Ml, D)
    saved = []
    kvn = kv3 = x_kv = None
    xs = x2
    h1 = _rms_fwd(xs, vec(g_mix_pre[0]), BF16, name="rms_mixpre_0")
    for l in range(depth):
        sv = {"x0": xs}
        tok = None
        if l + 1 < depth:
            pending = gather_start(mix_keys(l + 1), pending_ffn[-1] if l == 0 else W[("w_mem_kv", l)], f"mix_{l + 1}")
            pending_next_ffn = gather_start(ffn_keys(l + 1), pending[-1], f"ffn_{l + 1}")
            tok = pending_next_ffn[-1]
        memn = _rms_fwd(mem2, vec(g_mem[l]), BF16, name=f"rms_mem_{l}")
        mkv3 = _mm(memn, W[("w_mem_kv", l)], after=tok, name=f"mm_memkv_{l}").reshape(Bl, Ml, 2 * MEM_W)
        if l < n_a:
            j = l
            proj = _mm(h1, W[("w_in_a", j)], tb=True, after=tok, name=f"mm_in_{l}")
            proj3 = proj.reshape(Bl, T, -1)
            xc3 = _conv_fwd_call(proj3, MIX_W, MIX_W, W["w_conv_a"][j], vec(W["b_conv_a"][j]), name=f"conv_a_{l}")
            gates3 = _mm(xc3.reshape(N, MIX_W), wbd[j], name=f"mm_gates_{l}").reshape(Bl, T, 2 * MIX_W)
            y_main3, hs3 = _rglru_fwd(xc3, gates3, proj3, vec(b_rg_r[j]), vec(b_rg_i[j]), vec(W["lru_lambda"][j]),
                                      name=f"rglru_fwd_{l}")
            q_off = 2 * MIX_W
            sv.update(xc3=xc3, gates3=gates3, hs3=hs3)
        else:
            j = l - n_a
            if l == n_a:
                x_kv = xs
                kv3 = _mm(kvn, W[("w_kv", None)], name="mm_kv").reshape(Bl, T, 2 * MEM_W)
            proj = _mm(h1, W[("w_in_b", j)], after=tok, name=f"mm_in_{l}")
            proj3 = proj.reshape(Bl, T, -1)
            y_main3 = _swa_fwd(proj3, kv3, sinks_b[j], name=f"swa_fwd_{l}")
            q_off = MIX_W
        y_mem3 = _mem_attn_fwd(proj3, q_off, mkv3, name=f"memattn_fwd_{l}")
        y_main = y_main3.reshape(N, MIX_W)
        y_mem = y_mem3.reshape(N, MEM_W)
        y = _mm_sum([(y_main, W[("w_mix_out", l)], (0, 0)), (y_mem, W[("w_mix_out", l)], (MIX_W, 0))], n=D,
                    name=f"mm_mixout_{l}")
        x1, h2 = _rms_pair_fwd(y, vec(g_mix_post[l]), xs, [vec(g_ffn_pre[l])], name=f"rms_mixpost_ffnpre_{l}")
        gather_wait(pending_ffn, h2)
        if l + 1 < depth:
            pending_ffn = pending_next_ffn
        u3 = _mm(h2, W[("w_ffn_up", l)], tb=True, name=f"mm_up_{l}").reshape(Bl, T, 2 * F)
        act3 = _ffn_mid_fwd(u3, W["w_ffn_conv"][l], vec(b_ffn_conv[l]), name=f"ffn_mid_fwd_{l}")
        act = act3.reshape(N, F)
        f = _mm(act, W[("w_ffn_down", l)], name=f"mm_down_{l}")
        sv.update(h1=h1, memn=memn, mkv3=mkv3, proj3=proj3, q_off=q_off, y_main=y_main, y_mem=y_mem, y=y, x1=x1,
                  h2=h2, u3=u3, act=act, f=f)
        saved.append(sv)
        if l + 1 < depth:
            gains = [vec(g_mix_pre[l + 1])] + ([vec(g_kv)] if l + 1 == n_a else [])
            xs, h1, *rest = _rms_pair_fwd(f, vec(g_ffn_post[l]), x1, gains, name=f"rms_ffnpost_mixpre_{l}")
            if rest:
                kvn = rest[0]
            gather_wait(pending, xs)
        else:
            xs = _rms_fwd(f, vec(g_ffn_post[l]), F32, res=x1, name=f"rms_ffnpost_{l}")

    dxs, loss_vec = _loss_bwd(xs, loss_target.reshape(N, D))
    loss = lax.psum(jnp.sum(loss_vec), ("x", "y", "c"))

    G = {n: [None] * w_loc[n].shape[0] for n in REPL + SMALL_SHARDED if n != "g_kv"}
    GW = {}

    def dw(key, off, a, b_, nm):
        GW[key] = _mm(a, b_, ta=True, out_dtype=BF16, into=(GW.get(key), (1,) + W[key].shape, 0, off), name=nm)

    def grad_blocks(key):
        g = GW[key]
        return g.reshape(1, N_DEV, g.shape[1] // N_DEV, g.shape[2])

    reduces = []

    def reduce_start(keys, after, tag):
        srcs = [grad_blocks(kk) for kk in keys]
        lands = [lax.empty((len(FLIPS),) + s.shape[:1] + s.shape[2:], s.dtype) for s in srcs]
        started = _split_start(False, srcs, lands, after, name=f"rs_start_{tag}")
        reduces.append((keys, tag) + started)
        return started[-1]

    kv_parts = []
    df = None
    for l in reversed(range(depth)):
        sv = saved[l]
        proj3 = sv["proj3"]
        if df is None:
            df, dg = _rms_bwd(sv["f"], vec(g_ffn_post[l]), dxs, out_dtype=BF16, name=f"rmsb_ffnpost_{l}")
            G["g_ffn_post"][l] = dg[0]
        dact = _mm(df, W[("w_ffn_down", l)], tb=True, name=f"mmb_down_dx_{l}")
        dw(("w_ffn_down", l), (0, 0), sv["act"], df, f"mmb_down_dw_{l}")
        dug3, duv3, dwg, dwv, dbg, dbv = _ffn_mid_bwd(sv["u3"], dact.reshape(Bl, T, F),
                                                      W["w_ffn_conv"][l], vec(b_ffn_conv[l]), name=f"ffn_mid_bwd_{l}")
        G["w_ffn_conv"][l] = jnp.concatenate([dwg, dwv], axis=1)
        G["b_ffn_conv"][l] = jnp.concatenate([dbg, dbv], axis=1)[0]
        dug, duv = dug3.reshape(N, F), duv3.reshape(N, F)
        dw(("w_ffn_up", l), (0, 0), dug, sv["h2"], f"mmb_up_dw_g_{l}")
        dw(("w_ffn_up", l), (F, 0), duv, sv["h2"], f"mmb_up_dw_v_{l}")
        tok = reduce_start([("w_ffn_down", l), ("w_ffn_up", l)], dug, f"ffn_{l}")
        dh2 = _mm_sum([(dug, W[("w_ffn_up", l)], (0, 0)), (duv, W[("w_ffn_up", l)], (F, 0))], n=D, after=tok,
                      name=f"mmb_up_dx_{l}")
        dx1, dy, dg, dg2 = _rms_pair_bwd(sv["x1"], vec(g_ffn_pre[l]), dh2, dxs, sv["y"], vec(g_mix_post[l]),
                                         name=f"rmsb_ffnpre_mixpost_{l}")
        G["g_ffn_pre"][l] = dg[0]
        G["g_mix_post"][l] = dg2[0]
        dy_main = _mm(dy, W[("w_mix_out", l)], tb=True, n=MIX_W, k=D, name=f"mmb_mixout_dmain_{l}")
        dy_mem = _mm(dy, W[("w_mix_out", l)], tb=True, n=MEM_W, k=D, b_off=(MIX_W, 0),
                     name=f"mmb_mixout_dmem_{l}")
        dw(("w_mix_out", l), (0, 0), sv["y_main"], dy, f"mmb_mixout_dw_main_{l}")
        dw(("w_mix_out", l), (MIX_W, 0), sv["y_mem"], dy, f"mmb_mixout_dw_mem_{l}")
        dq_mem3, dmkv3 = _mem_attn_bwd(proj3, sv["q_off"], sv["mkv3"], dy_mem.reshape(Bl, T, MEM_W),
                                       name=f"memattn_bwd_{l}")
        dq_mem = dq_mem3.reshape(N, MEM_W)
        dmkv = dmkv3.reshape(Bl * Ml, 2 * MEM_W)
        dw(("w_mem_kv", l), (0, 0), sv["memn"], dmkv, f"mmb_memkv_dw_{l}")
        dmemn = _mm(dmkv, W[("w_mem_kv", l)], tb=True, name=f"mmb_memkv_dx_{l}")
        _, dg = _rms_bwd(mem2, vec(g_mem[l]), dmemn, name=f"rmsb_mem_{l}")
        G["g_mem"][l] = dg[0]
        dy_main3 = dy_main.reshape(Bl, T, MIX_W)
        if l < n_a:
            j = l
            dxc3, drp3, dip3, dugate3, dbr, dbi, dlam = _rglru_bwd(
                dy_main3, sv["xc3"], sv["gates3"], proj3, sv["hs3"], vec(b_rg_r[j]), vec(b_rg_i[j]),
                vec(W["lru_lambda"][j]), name=f"rglru_bwd_{l}")
            G["b_rg_r"][j] = dbr.reshape(nblk, bsz)
            G["b_rg_i"][j] = dbi.reshape(nblk, bsz)
            G["lru_lambda"][j] = dlam[0]
            drp, dip = drp3.reshape(N, MIX_W), dip3.reshape(N, MIX_W)
            xc2 = sv["xc3"].reshape(N, MIX_W)
            G["w_rg_r"][j] = _diag_blocks(_mm(xc2, drp, ta=True, name=f"mmb_gates_dw_r_{l}"), nblk, bsz)
            G["w_rg_i"][j] = _diag_blocks(_mm(xc2, dip, ta=True, name=f"mmb_gates_dw_i_{l}"), nblk, bsz)
            dxc = _mm_sum([(drp, wbd[j], (0, 0)), (dip, wbd[j], (0, MIX_W))], tb=True, n=MIX_W,
                          add=dxc3.reshape(N, MIX_W), name=f"mmb_gates_dx_{l}")
            dux3, dwc, dbc = _conv_bwd_call(dxc.reshape(Bl, T, MIX_W), proj3, MIX_W, MIX_W, W["w_conv_a"][j],
                                            name=f"conv_a_bwd_{l}")
            G["w_conv_a"][j] = dwc
            G["b_conv_a"][j] = dbc[0]
            pieces = [(dugate3.reshape(N, MIX_W), 0), (dux3.reshape(N, MIX_W), MIX_W), (dq_mem, 2 * MIX_W)]
            in_key = ("w_in_a", j)
        else:
            j = l - n_a
            dq3, dkc, dkp, dsk = _swa_bwd(proj3, kv3, sinks_b[j], dy_main3, name=f"swa_bwd_{l}")
            kv_parts.append((dkc, dkp))
            G["sinks_b"][j] = dsk[0, :SWA_HEADS]
            pieces = [(dq3.reshape(N, MIX_W), 0), (dq_mem, MIX_W)]
            in_key = ("w_in_b", j)
        in_t = in_key[0] in TRANSPOSED
        for pi, (piece, off) in enumerate(pieces):
            if in_t:
                dw(in_key, (off, 0), piece, sv["h1"], f"mmb_in_dw_{pi}_{l}")
            else:
                dw(in_key, (0, off), sv["h1"], piece, f"mmb_in_dw_{pi}_{l}")
        tok = reduce_start([("w_mix_out", l), ("w_mem_kv", l), in_key], dy, f"mix_{l}")
        dh1 = _mm_sum([(piece, W[in_key], (off, 0) if in_t else (0, off)) for piece, off in pieces], tb=not in_t, n=D,
                      after=tok, name=f"mmb_in_dx_{l}")
        if l > 0 and l != n_a:
            dxs, df, dg, dg2 = _rms_pair_bwd(sv["x0"], vec(g_mix_pre[l]), dh1, dx1, saved[l - 1]["f"],
                                             vec(g_ffn_post[l - 1]), name=f"rmsb_mixpre_ffnpost_{l}")
            G["g_ffn_post"][l - 1] = dg2[0]
        else:
            dxs, dg = _rms_bwd(sv["x0"], vec(g_mix_pre[l]), dh1, add=dx1, name=f"rmsb_mixpre_{l}")
            df = None
        G["g_mix_pre"][l] = dg[0]
        if l == n_a:
            dkv = _kv_grad_combine(kv_parts, name="kv_grad_combine").reshape(N, 2 * MEM_W)
            dw(("w_kv", None), (0, 0), kvn, dkv, "mmb_kv_dw")
            tok = reduce_start([("w_kv", None)], dkv, "kv")
            dkvn = _mm(dkv, W[("w_kv", None)], tb=True, after=tok, name="mmb_kv_dx")
            dxs, dg = _rms_bwd(x_kv, vec(g_kv), dkvn, add=dxs, name="rmsb_kv")
            G["g_kv"] = dg[0]
    grad_x = dxs.reshape(Bl, T, D)
    Gf = {n: (jnp.stack(g) if isinstance(g, list) else g) for n, g in G.items()}

    small4 = []
    for n in SMALL_SHARDED:
        t = _small_to_cols(Gf[n]).astype(BF16)
        small4.append(t.reshape(1, N_DEV, -1, t.shape[-1]))
    small_lands = [lax.empty((len(FLIPS),) + s.shape[:1] + s.shape[2:], s.dtype) for s in small4]
    small_started = _split_start(False, small4, small_lands, dxs, name="rs_start_small")
    r_blob = _pack([Gf[n] for n in REPL], REPL_ROWS, BF16)
    r_parts = _all_gather(r_blob, name="ag_repl_grads")
    parts = {}
    for keys, tag, ssem, rsem, srcs, lands, _ in reduces:
        for kk, s, g7 in zip(keys, srcs, _split_wait(False, ssem, rsem, srcs, lands, small_started[-1],
                                                     name=f"rs_wait_{tag}")):
            parts[kk] = (s, g7)

    res = [{} for _ in range(4)]
    for n, _ in SHARDED:
        if n in SMALL_SHARDED:
            continue
        idx = [None] if w_loc[n].ndim == 2 else list(range(w_loc[n].shape[0]))
        wmv = [as_rows(n, a[n]) for a in (w_loc, m_loc, v_loc)]
        shp3 = (len(idx),) + wmv[0].shape[-2:]
        outs = _adamw_layers([parts[(n, i)][0] for i in idx], [parts[(n, i)][1] for i in idx],
                             *[a.reshape(shp3) for a in wmv], name=f"adamw_{n}")
        for k in range(4):
            res[k][n] = as_rows(n, outs[k].reshape(wmv[0].shape))
    last = res[0]["w_kv"]
    small_got = _split_wait(False, *small_started[:4], last, name="rs_wait_small")
    for n, own, g7 in zip(SMALL_SHARDED, small4, small_got):
        shp3 = own.shape[:1] + own.shape[2:]
        outs = _adamw_layers([own], [g7], w_loc[n].reshape(shp3), m_loc[n].reshape(shp3), v_loc[n].reshape(shp3),
                             name=f"adamw_{n}")
        for k in range(4):
            res[k][n] = outs[k].reshape(w_loc[n].shape)
    outs_rp = _adamw_replicated(r_parts, _pack([w_loc[n] for n in REPL], REPL_ROWS, F32),
                                _pack([m_loc[n] for n in REPL], REPL_ROWS, F32),
                                _pack([v_loc[n] for n in REPL], REPL_ROWS, F32),
                                name="adamw_replicated")
    rp_shapes = [w_loc[n].shape for n in REPL]
    for k in range(4):
        res[k].update(zip(REPL, _unpack(outs_rp[k], rp_shapes)))
    out = [loss, grad_x]
    for k in range(4):
        out += [res[k][n] for n in WEIGHTS]
    return tuple(out)
```

```python
import math

import numpy as np
import jax
import jax.numpy as jnp
from jax import lax
from jax.experimental import pallas as pl
from jax.experimental.pallas import tpu as pltpu

F32 = jnp.float32
BF16 = jnp.bfloat16
S = jax.ShapeDtypeStruct
MESH = pl.DeviceIdType.MESH
ANY = pl.BlockSpec(memory_space=pl.ANY)

HEAD = 64
MEM_HEADS = 4
MEM_W = MEM_HEADS * HEAD
SWA_HEADS = 12
SWA_GROUP = 3
MIX_W = SWA_HEADS * HEAD
WIN = 128
LRU_C = 8.0
EPS = 1e-6
ADAM_LR, ADAM_B1, ADAM_B2, ADAM_EPS, ADAM_WD, ADAM_STEP = 0.001, 0.9, 0.999, 1e-08, 0.01, 10
GELU_C0 = math.sqrt(2.0 / math.pi)
GELU_C1 = 0.044715
N_DEV = 8
LANES = 128
CT = 128
VMEM_LIMIT = 48 * 1024 * 1024
MM_VMEM_BUDGET = 36 * 1024 * 1024
REPL_ROWS = 256

SHARDED = (("w_mem_kv", 1), ("w_mix_out", 1), ("w_ffn_up", 2), ("w_ffn_conv", 2), ("w_ffn_down", 1), ("w_in_a", 2),
           ("w_conv_a", 2), ("b_conv_a", 1), ("lru_lambda", 1), ("w_in_b", 1), ("w_kv", 0))
SMALL_SHARDED = ("w_ffn_conv", "w_conv_a", "b_conv_a", "lru_lambda")
TRANSPOSED = ("w_ffn_up", "w_in_a")
REPL = ("g_mix_pre", "g_mix_post", "g_ffn_pre", "g_ffn_post", "g_mem", "b_ffn_conv", "w_rg_r", "b_rg_r", "w_rg_i",
        "b_rg_i", "sinks_b", "g_kv")
WEIGHTS = ("g_mix_pre", "g_mix_post", "g_ffn_pre", "g_ffn_post", "g_mem", "w_mem_kv", "w_mix_out", "w_ffn_up",
           "w_ffn_conv", "b_ffn_conv", "w_ffn_down", "w_in_a", "w_conv_a", "b_conv_a", "w_rg_r", "b_rg_r", "w_rg_i",
           "b_rg_i", "lru_lambda", "w_in_b", "sinks_b", "g_kv", "w_kv")


def _alibi_slopes(n):
    def pow2(m):
        start = 2.0 ** (-8.0 / m)
        return [start ** (i + 1) for i in range(m)]
    c = 2 ** int(math.floor(math.log2(n)))
    s = pow2(c)
    if c != n:
        s = s + pow2(2 * c)[0::2][: n - c]
    return [float(v) for v in np.asarray(s, dtype=np.float32)]


SLOPES = _alibi_slopes(SWA_HEADS)


def _tile(n, cands):
    for c in cands:
        if n % c == 0:
            return c
    return n


def _cparams(*sem):
    return pltpu.CompilerParams(dimension_semantics=sem, vmem_limit_bytes=VMEM_LIMIT)


def _mm_tiles(M, N, K, a_bytes, b_bytes, o_bytes, add_bytes, offsets):
    m_off, n_offs, k_off = offsets
    tms = [c for c in (1024, 512, 256, 128) if M % c == 0 and m_off % c == 0] or [M]
    tns = [c for c in (1408, 1024, 896, 768, 512, 384, 256, 128)
           if N % c == 0 and all(o % c == 0 for o in n_offs)] or [N]
    tks = [c for c in (K, 2048, 1408, 1024, 512, 256, 128) if c <= K and K % c == 0 and k_off % c == 0]
    best = None
    for tk in tks:
        fits = []
        for tm in tms:
            for tn in tns:
                need = 2 * (tm * tk * a_bytes + tk * tn * b_bytes + tm * tn * (o_bytes + add_bytes))
                need += tm * tn * 4 * (2 if tk < K else 1)
                need += (tm * tk * 2 if a_bytes != 2 else 0) + (tk * tn * 2 if b_bytes != 2 else 0)
                if need <= MM_VMEM_BUDGET:
                    fits.append((tm * tn, min(tm, 512), tm, tn))
        if fits:
            _, _, tm, tn = max(fits)
            best = (tm, tn, tk)
            break
    assert best is not None, (M, N, K)
    return best


def _mm(a, b, *, ta=False, tb=False, n=None, k=None, b_off=(0, 0), out_dtype=F32, add=None, into=None, after=None,
        name="mm"):
    if ta:
        K, M = a.shape
    else:
        M, K = a.shape
    if tb:
        N = b.shape[-2] if n is None else n
    else:
        N = b.shape[-1] if n is None else n
    assert k is None or k == K
    ro, co = b_off
    n_off, k_off = (ro, co) if tb else (co, ro)
    oro, oco = (0, 0) if into is None else into[3]
    tm, tn, tk = _mm_tiles(M, N, K, a.dtype.itemsize, b.dtype.itemsize, jnp.dtype(out_dtype).itemsize,
                           0 if add is None else add.dtype.itemsize, (oro, (n_off, oco), k_off))
    nk = K // tk
    if tb:
        b_spec = pl.BlockSpec((tn, tk), lambda i, j, kk: (j + ro // tn, kk + co // tk))
        b_dims = (1,)
    else:
        b_spec = pl.BlockSpec((tk, tn), lambda i, j, kk: (kk + ro // tk, j + co // tn))
        b_dims = (0,)
    if ta:
        a_spec = pl.BlockSpec((tk, tm), lambda i, j, kk: (kk, i))
        a_dims = (0,)
    else:
        a_spec = pl.BlockSpec((tm, tk), lambda i, j, kk: (i, kk))
        a_dims = (1,)
    dims = ((a_dims, b_dims), ((), ()))
    add_spec = pl.BlockSpec((tm, tn), lambda i, j, kk: (i, j))
    has_add = add is not None
    if into is None:
        o_spec, o_shape, buf = add_spec, (M, N), None
    else:
        buf, o_shape, ol, _ = into
        assert not has_add
        o_spec = pl.BlockSpec((None, tm, tn), lambda i, j, kk: (ol, i + oro // tm, j + oco // tn))
    has_buf = buf is not None

    def body(*refs):
        refs = list(refs)
        acc_ref = refs.pop() if nk > 1 else None
        o_ref = refs.pop()
        a_ref, b_ref = refs[0], refs[1]
        add_ref = refs[2] if has_add else None
        part = lax.dot_general(a_ref[...].astype(BF16), b_ref[...].astype(BF16), dims, preferred_element_type=F32)

        def finish(r):
            if has_add:
                r = r + add_ref[...].astype(F32)
            o_ref[...] = r.astype(out_dtype)

        if nk == 1:
            finish(part)
        else:
            kk = pl.program_id(2)

            @pl.when(kk == 0)
            def _():
                acc_ref[...] = part

            @pl.when(kk > 0)
            def _():
                acc_ref[...] += part

            @pl.when(kk == nk - 1)
            def _():
                finish(acc_ref[...])

    in_specs = [a_spec, b_spec] + ([add_spec] if has_add else []) + ([ANY] if has_buf else [])
    args = (a, b) + ((add,) if has_add else ()) + ((buf,) if has_buf else ())
    if after is not None:
        in_specs, args = in_specs + [ANY], args + (after,)
    return pl.pallas_call(
        body, grid=(M // tm, N // tn, nk), in_specs=in_specs, out_specs=o_spec,
        out_shape=S(o_shape, out_dtype), scratch_shapes=[pltpu.VMEM((tm, tn), F32)] if nk > 1 else [],
        input_output_aliases={2: 0} if has_buf else {},
        compiler_params=_cparams("parallel", "parallel", "arbitrary"), name=name)(*args)


def _mm_sum(pieces, *, tb=False, n, out_dtype=F32, add=None, after=None, name="mm_sum"):
    M = pieces[0][0].shape[0]
    ks = [a.shape[1] for a, _, _ in pieces]
    a_bytes = max(a.dtype.itemsize for a, _, _ in pieces)
    b_bytes = max(b.dtype.itemsize for _, b, _ in pieces)
    n_offs = tuple(off[0] if tb else off[1] for _, _, off in pieces)
    for kp, (_, _, off) in zip(ks, pieces):
        assert (off[1] if tb else off[0]) % kp == 0
    tm, tn, tk = _mm_tiles(M, n, sum(ks), a_bytes, b_bytes, jnp.dtype(out_dtype).itemsize, 0, (0, n_offs, 0))
    assert tk == sum(ks)
    a_specs = [pl.BlockSpec((tm, kp), lambda i, j: (i, 0)) for kp in ks]
    if tb:
        b_specs = [pl.BlockSpec((tn, kp), lambda i, j, ro=off[0], co=off[1], kp=kp: (j + ro // tn, co // kp))
                   for kp, (_, _, off) in zip(ks, pieces)]
        dims = NT
    else:
        b_specs = [pl.BlockSpec((kp, tn), lambda i, j, ro=off[0], co=off[1], kp=kp: (ro // kp, j + co // tn))
                   for kp, (_, _, off) in zip(ks, pieces)]
        dims = (((1,), (0,)), ((), ()))
    npc = len(pieces)
    o_spec = pl.BlockSpec((tm, tn), lambda i, j: (i, j))

    def body(*refs):
        o_ref = refs[2 * npc + (add is not None) + (after is not None)]
        acc = refs[2 * npc][...].astype(F32) if add is not None else None
        for p in range(npc):
            part = lax.dot_general(refs[p][...].astype(BF16), refs[npc + p][...].astype(BF16), dims,
                                   preferred_element_type=F32)
            acc = part if acc is None else acc + part
        o_ref[...] = acc.astype(out_dtype)

    args = [a for a, _, _ in pieces] + [b for _, b, _ in pieces]
    in_specs = a_specs + b_specs
    if add is not None:
        in_specs, args = in_specs + [o_spec], args + [add]
    if after is not None:
        in_specs, args = in_specs + [ANY], args + [after]
    return pl.pallas_call(
        body, grid=(M // tm, n // tn), in_specs=in_specs, out_specs=o_spec,
        out_shape=S((M, n), out_dtype), compiler_params=_cparams("parallel", "parallel"), name=name)(*args)


def _rms_fwd(x, g, out_dtype, res=None, name="rms_fwd"):
    N, D = x.shape
    tm = _tile(N, (512, 256, 128))
    has_res = res is not None

    def body(*refs):
        if has_res:
            x_ref, g_ref, r_ref, o_ref = refs
        else:
            x_ref, g_ref, o_ref = refs
        xv = x_ref[...].astype(F32)
        y = xv * lax.rsqrt(jnp.mean(xv * xv, axis=-1, keepdims=True) + EPS) * g_ref[...]
        if has_res:
            y = y + r_ref[...]
        o_ref[...] = y.astype(out_dtype)

    row = pl.BlockSpec((tm, D), lambda i: (i, 0))
    vec = pl.BlockSpec((1, D), lambda i: (0, 0))
    return pl.pallas_call(
        body, grid=(N // tm,), in_specs=[row, vec] + ([row] if has_res else []), out_specs=row,
        out_shape=S((N, D), out_dtype), compiler_params=_cparams("parallel"), name=name)(
            *((x, g) + ((res,) if has_res else ())))


def _rms_bwd(x, g, dy, add=None, out_dtype=F32, name="rms_bwd"):
    N, D = x.shape
    tm = _tile(N, (512, 256, 128))
    has_add = add is not None

    def body(*refs):
        if has_add:
            x_ref, g_ref, dy_ref, add_ref, dx_ref, dg_ref = refs
        else:
            x_ref, g_ref, dy_ref, dx_ref, dg_ref = refs
        xv = x_ref[...].astype(F32)
        dyv = dy_ref[...].astype(F32)
        r = lax.rsqrt(jnp.mean(xv * xv, axis=-1, keepdims=True) + EPS)
        u = dyv * g_ref[...]
        dx = r * u - xv * (r * r * r * jnp.mean(u * xv, axis=-1, keepdims=True))
        if has_add:
            dx = dx + add_ref[...]
        dx_ref[...] = dx.astype(out_dtype)

        @pl.when(pl.program_id(0) == 0)
        def _():
            dg_ref[...] = jnp.zeros_like(dg_ref)

        dg_ref[...] += jnp.sum(dyv * xv * r, axis=0, keepdims=True)

    row = pl.BlockSpec((tm, D), lambda i: (i, 0))
    vec = pl.BlockSpec((1, D), lambda i: (0, 0))
    return pl.pallas_call(
        body, grid=(N // tm,), in_specs=[row, vec, row] + ([row] if has_add else []), out_specs=(row, vec),
        out_shape=(S((N, D), out_dtype), S((1, D), F32)), compiler_params=_cparams("arbitrary"), name=name)(
            *((x, g, dy) + ((add,) if has_add else ())))


def _rms_pair_fwd(y, g_post, res, gains, name):
    N, D = y.shape
    tm = _tile(N, (512, 256, 128))
    ng = len(gains)

    def body(*refs):
        y_ref, gp_ref, r_ref = refs[:3]
        g_refs = refs[3:3 + ng]
        x_ref = refs[3 + ng]
        h_refs = refs[4 + ng:]
        yv = y_ref[...]
        x = r_ref[...] + yv * lax.rsqrt(jnp.mean(yv * yv, axis=-1, keepdims=True) + EPS) * gp_ref[...]
        x_ref[...] = x
        xn = x * lax.rsqrt(jnp.mean(x * x, axis=-1, keepdims=True) + EPS)
        for g_ref, h_ref in zip(g_refs, h_refs):
            h_ref[...] = (xn * g_ref[...]).astype(BF16)

    row = pl.BlockSpec((tm, D), lambda i: (i, 0))
    vec = pl.BlockSpec((1, D), lambda i: (0, 0))
    return pl.pallas_call(
        body, grid=(N // tm,), in_specs=[row, vec, row] + [vec] * ng, out_specs=(row,) * (1 + ng),
        out_shape=(S((N, D), F32),) + (S((N, D), BF16),) * ng, compiler_params=_cparams("parallel"), name=name)(
            y, g_post, res, *gains)


def _rms_pair_bwd(xa, ga, dya, add, xb, gb, name):
    N, D = xa.shape
    tm = _tile(N, (512, 256, 128))

    def one(xv, g_ref, dyv):
        r = lax.rsqrt(jnp.mean(xv * xv, axis=-1, keepdims=True) + EPS)
        u = dyv * g_ref[...]
        dx = r * u - xv * (r * r * r * jnp.mean(u * xv, axis=-1, keepdims=True))
        return dx, jnp.sum(dyv * xv * r, axis=0, keepdims=True)

    def body(xa_ref, ga_ref, dya_ref, add_ref, xb_ref, gb_ref, da_ref, db_ref, dga_ref, dgb_ref):
        da, dga = one(xa_ref[...].astype(F32), ga_ref, dya_ref[...].astype(F32))
        da = da + add_ref[...]
        da_ref[...] = da
        db, dgb = one(xb_ref[...].astype(F32), gb_ref, da)
        db_ref[...] = db.astype(BF16)

        @pl.when(pl.program_id(0) == 0)
        def _():
            dga_ref[...] = jnp.zeros_like(dga_ref)
            dgb_ref[...] = jnp.zeros_like(dgb_ref)

        dga_ref[...] += dga
        dgb_ref[...] += dgb

    row = pl.BlockSpec((tm, D), lambda i: (i, 0))
    vec = pl.BlockSpec((1, D), lambda i: (0, 0))
    return pl.pallas_call(
        body, grid=(N // tm,), in_specs=[row, vec, row, row, row, vec], out_specs=(row, row, vec, vec),
        out_shape=(S((N, D), F32), S((N, D), BF16), S((1, D), F32), S((1, D), F32)),
        compiler_params=_cparams("arbitrary"), name=name)(xa, ga, dya, add, xb, gb)


def _shift_down(x, s, row):
    return jnp.where(row >= s, pltpu.roll(x, s, axis=0), 0.0)


def _shift_up(x, s, row):
    T = x.shape[0]
    return jnp.where(row < T - s, pltpu.roll(x, T - s, axis=0), 0.0)


SLAB = 16


def _conv_rows(x_ref, w_ref, b_ref, lo, hi):
    W = w_ref.shape[0]
    y = x_ref[lo:hi, :] * w_ref[W - 1:W, :] + b_ref[...]
    for s in range(1, W):
        y = y + x_ref[lo - s:hi - s, :] * w_ref[W - 1 - s:W - s, :]
    return y


def _taps(x_ref, W):
    T = x_ref.shape[0]
    head = x_ref[0:SLAB, :]
    row = lax.broadcasted_iota(jnp.int32, head.shape, 0)
    return [x_ref[...]] + [jnp.concatenate([_shift_down(head, s, row), x_ref[SLAB - s:T - s, :]], axis=0)
                           for s in range(1, W)]


def _conv_taps(xs, w_ref, b_ref):
    W = w_ref.shape[0]
    y = xs[0] * w_ref[W - 1:W, :] + b_ref[...]
    for s in range(1, W):
        y = y + xs[s] * w_ref[W - 1 - s:W - s, :]
    return y


def _conv_head(x_head, w_ref, b_ref):
    row = lax.broadcasted_iota(jnp.int32, x_head.shape, 0)
    return _conv_taps([x_head] + [_shift_down(x_head, s, row) for s in range(1, w_ref.shape[0])], w_ref, b_ref)


def _conv_bwd_taps(dy, xs, w_ref, row):
    W = w_ref.shape[0]
    dx = dy * w_ref[W - 1:W, :]
    dws = [None] * W
    dws[W - 1] = jnp.sum(dy * xs[0], axis=0, keepdims=True)
    for s in range(1, W):
        dx = dx + _shift_up(dy, s, row) * w_ref[W - 1 - s:W - s, :]
        dws[W - 1 - s] = jnp.sum(dy * xs[s], axis=0, keepdims=True)
    return dx, jnp.concatenate(dws, axis=0), jnp.sum(dy, axis=0, keepdims=True)


def _gelu(g):
    t = jnp.tanh(GELU_C0 * (g + GELU_C1 * g * g * g))
    return 0.5 * g * (1.0 + t), t


def _dgelu(g, t):
    return 0.5 * (1.0 + t) + 0.5 * g * (1.0 - t * t) * (GELU_C0 * (1.0 + 3.0 * GELU_C1 * g * g))


def _cspec(T, off=0, ct=CT):
    return pl.BlockSpec((1, T, ct), lambda j, b: (b, 0, j + off))


def _pspec(rows, off=0, ct=CT):
    return pl.BlockSpec((rows, ct), lambda j, b: (0, j + off))


def _conv_fwd_call(x3, x_off, C, w, b, name):
    Bl, T, _ = x3.shape
    W = w.shape[0]

    def body(x_ref, w_ref, b_ref, o_ref):
        o_ref[0, SLAB:T, :] = _conv_rows(x_ref.at[0], w_ref, b_ref, SLAB, T)
        o_ref[0, 0:SLAB, :] = _conv_head(x_ref[0, 0:SLAB, :], w_ref, b_ref)

    return pl.pallas_call(
        body, grid=(C // CT, Bl), in_specs=[_cspec(T, x_off // CT), _pspec(W), _pspec(1)], out_specs=_cspec(T),
        out_shape=S((Bl, T, C), F32), compiler_params=_cparams("parallel", "arbitrary"), name=name)(x3, w, b)


def _conv_bwd_call(dy3, x3, x_off, C, w, name):
    Bl, T, _ = x3.shape
    W = w.shape[0]

    def body(dy_ref, x_ref, w_ref, dx_ref, dw_ref, db_ref):
        row = lax.broadcasted_iota(jnp.int32, (T, CT), 0)
        dx, dw, db = _conv_bwd_taps(dy_ref[0], _taps(x_ref.at[0], W), w_ref, row)
        dx_ref[0] = dx.astype(BF16)

        @pl.when(pl.program_id(1) == 0)
        def _():
            dw_ref[...] = jnp.zeros_like(dw_ref)
            db_ref[...] = jnp.zeros_like(db_ref)

        dw_ref[...] += dw
        db_ref[...] += db

    return pl.pallas_call(
        body, grid=(C // CT, Bl), in_specs=[_cspec(T), _cspec(T, x_off // CT), _pspec(W)],
        out_specs=(_cspec(T), _pspec(W), _pspec(1)),
        out_shape=(S((Bl, T, C), BF16), S((W, C), F32), S((1, C), F32)),
        compiler_params=_cparams("parallel", "arbitrary"), name=name)(dy3, x3, w)


def _ffn_mid_fwd(u3, wc, bc, name):
    Bl, T, F2 = u3.shape
    F = F2 // 2
    nf = F // CT

    def body(ug_ref, uv_ref, wg_ref, wv_ref, bg_ref, bv_ref, o_ref):
        g = _conv_rows(ug_ref.at[0], wg_ref, bg_ref, SLAB, T)
        v = _conv_rows(uv_ref.at[0], wv_ref, bv_ref, SLAB, T)
        o_ref[0, SLAB:T, :] = (_gelu(g)[0] * v).astype(BF16)
        g = _conv_head(ug_ref[0, 0:SLAB, :], wg_ref, bg_ref)
        v = _conv_head(uv_ref[0, 0:SLAB, :], wv_ref, bv_ref)
        o_ref[0, 0:SLAB, :] = (_gelu(g)[0] * v).astype(BF16)

    return pl.pallas_call(
        body, grid=(nf, Bl),
        in_specs=[_cspec(T), _cspec(T, nf), _pspec(3), _pspec(3, nf), _pspec(1), _pspec(1, nf)], out_specs=_cspec(T),
        out_shape=S((Bl, T, F), BF16), compiler_params=_cparams("parallel", "arbitrary"), name=name)(
            u3, u3, wc, wc, bc, bc)


def _ffn_mid_bwd(u3, dact3, wc, bc, name):
    Bl, T, F2 = u3.shape
    F = F2 // 2
    nf = F // CT

    def body(ug_ref, uv_ref, da_ref, wg_ref, wv_ref, bg_ref, bv_ref, dug_ref, duv_ref, dwg_ref, dwv_ref, dbg_ref,
             dbv_ref):
        row = lax.broadcasted_iota(jnp.int32, (T, CT), 0)
        ugs = _taps(ug_ref.at[0], 3)
        uvs = _taps(uv_ref.at[0], 3)
        g = _conv_taps(ugs, wg_ref, bg_ref)
        v = _conv_taps(uvs, wv_ref, bv_ref)
        da = da_ref[0]
        gel, t = _gelu(g)
        dg = da * v * _dgelu(g, t)
        dv = da * gel
        dug, dwg, dbg = _conv_bwd_taps(dg, ugs, wg_ref, row)
        duv, dwv, dbv = _conv_bwd_taps(dv, uvs, wv_ref, row)
        dug_ref[0] = dug.astype(BF16)
        duv_ref[0] = duv.astype(BF16)

        @pl.when(pl.program_id(1) == 0)
        def _():
            dwg_ref[...] = jnp.zeros_like(dwg_ref)
            dwv_ref[...] = jnp.zeros_like(dwv_ref)
            dbg_ref[...] = jnp.zeros_like(dbg_ref)
            dbv_ref[...] = jnp.zeros_like(dbv_ref)

        dwg_ref[...] += dwg
        dwv_ref[...] += dwv
        dbg_ref[...] += dbg
        dbv_ref[...] += dbv

    return pl.pallas_call(
        body, grid=(nf, Bl),
        in_specs=[_cspec(T), _cspec(T, nf), _cspec(T), _pspec(3), _pspec(3, nf), _pspec(1), _pspec(1, nf)],
        out_specs=(_cspec(T), _cspec(T), _pspec(3), _pspec(3), _pspec(1), _pspec(1)),
        out_shape=(S((Bl, T, F), BF16), S((Bl, T, F), BF16), S((3, F), F32), S((3, F), F32), S((1, F), F32),
                   S((1, F), F32)),
        compiler_params=_cparams("parallel", "arbitrary"), name=name)(u3, u3, dact3, wc, wc, bc, bc)


def _lru_gates(xc, rp, ip, br_ref, bi_ref, lam_ref):
    r = jax.nn.sigmoid(rp + br_ref[...])
    i = jax.nn.sigmoid(ip + bi_ref[...])
    lam = lam_ref[...]
    sp = jnp.maximum(-lam, 0.0) + jnp.log1p(jnp.exp(-jnp.abs(lam)))
    log_a = (-LRU_C) * r * sp
    a = jnp.exp(log_a)
    z = 2.0 * log_a
    one_m_a2 = jnp.where(z > -0.05, -z * (1.0 + z * (0.5 + z * (1.0 / 6.0 + z * (1.0 / 24.0)))), 1.0 - a * a)
    mult = jnp.sqrt(one_m_a2)
    return r, i, sp, a, mult


SCAN_CHUNK = 64


def _scan_down(a, b):
    T = a.shape[0]
    ch = min(SCAN_CHUNK, T)
    row = lax.broadcasted_iota(jnp.int32, (ch, a.shape[1]), 0)
    outs, carry = [], None
    for c in range(T // ch):
        ac, bc = a[c * ch:(c + 1) * ch], b[c * ch:(c + 1) * ch]
        s = 1
        while s < ch:
            a_sh = jnp.where(row >= s, pltpu.roll(ac, s, axis=0), 1.0)
            bc = ac * _shift_down(bc, s, row) + bc
            ac = ac * a_sh
            s *= 2
        if carry is not None:
            bc = bc + ac * carry
        carry = bc[ch - 1:ch, :]
        outs.append(bc)
    return jnp.concatenate(outs, axis=0)


def _scan_up(an, g):
    T = an.shape[0]
    ch = min(SCAN_CHUNK, T)
    row = lax.broadcasted_iota(jnp.int32, (ch, an.shape[1]), 0)
    outs, carry = [], None
    for c in reversed(range(T // ch)):
        ac, gc = an[c * ch:(c + 1) * ch], g[c * ch:(c + 1) * ch]
        s = 1
        while s < ch:
            a_sh = jnp.where(row < ch - s, pltpu.roll(ac, ch - s, axis=0), 1.0)
            gc = ac * _shift_up(gc, s, row) + gc
            ac = ac * a_sh
            s *= 2
        if carry is not None:
            gc = gc + ac * carry
        carry = gc[0:1, :]
        outs.append(gc)
    return jnp.concatenate(outs[::-1], axis=0)


def _rglru_fwd(xc3, gates3, proj3, br, bi, lam, name):
    Bl, T, C = xc3.shape

    def body(xc_ref, rp_ref, ip_ref, ug_ref, br_ref, bi_ref, lam_ref, y_ref, h_ref):
        xc = xc_ref[0]
        r, i, sp, a, mult = _lru_gates(xc, rp_ref[0], ip_ref[0], br_ref, bi_ref, lam_ref)
        h = _scan_down(a, mult * (i * xc))
        h_ref[0] = h
        y_ref[0] = (h * _gelu(ug_ref[0])[0]).astype(BF16)

    return pl.pallas_call(
        body, grid=(C // CT, Bl),
        in_specs=[_cspec(T), _cspec(T), _cspec(T, C // CT), _cspec(T), _pspec(1), _pspec(1), _pspec(1)],
        out_specs=(_cspec(T), _cspec(T)), out_shape=(S((Bl, T, C), BF16), S((Bl, T, C), F32)),
        compiler_params=_cparams("parallel", "arbitrary"), name=name)(xc3, gates3, gates3, proj3, br, bi, lam)


def _rglru_bwd(dy3, xc3, gates3, proj3, h3, br, bi, lam, name):
    Bl, T, C = xc3.shape

    def body(dy_ref, xc_ref, rp_ref, ip_ref, ug_ref, h_ref, br_ref, bi_ref, lam_ref,
             dxc_ref, drp_ref, dip_ref, dug_ref, dbr_ref, dbi_ref, dlam_ref):
        row = lax.broadcasted_iota(jnp.int32, (T, CT), 0)
        xc = xc_ref[0]
        r, i, sp, a, mult = _lru_gates(xc, rp_ref[0], ip_ref[0], br_ref, bi_ref, lam_ref)
        h = h_ref[0]
        dy = dy_ref[0]
        ug = ug_ref[0]
        gel, t = _gelu(ug)
        dug_ref[0] = (dy * h * _dgelu(ug, t)).astype(BF16)
        gacc = _scan_up(_shift_up(a, 1, row), dy * gel)
        da = gacc * _shift_down(h, 1, row)
        ix = i * xc
        d_mult = gacc * ix
        d_i = gacc * mult * xc
        dxc_ref[0] = gacc * mult * i
        d_log_a = da * a - d_mult * (a * a) / mult
        d_r = d_log_a * ((-LRU_C) * sp)
        d_sp = jnp.sum(d_log_a * ((-LRU_C) * r), axis=0, keepdims=True)
        drp = d_r * r * (1.0 - r)
        dip = d_i * i * (1.0 - i)
        drp_ref[0] = drp.astype(BF16)
        dip_ref[0] = dip.astype(BF16)

        @pl.when(pl.program_id(1) == 0)
        def _():
            dbr_ref[...] = jnp.zeros_like(dbr_ref)
            dbi_ref[...] = jnp.zeros_like(dbi_ref)
            dlam_ref[...] = jnp.zeros_like(dlam_ref)

        dbr_ref[...] += jnp.sum(drp, axis=0, keepdims=True)
        dbi_ref[...] += jnp.sum(dip, axis=0, keepdims=True)
        dlam_ref[...] += d_sp * (-jax.nn.sigmoid(-lam_ref[...]))

    vec = S((1, C), F32)
    act = S((Bl, T, C), BF16)
    return pl.pallas_call(
        body, grid=(C // CT, Bl),
        in_specs=[_cspec(T), _cspec(T), _cspec(T), _cspec(T, C // CT), _cspec(T), _cspec(T)] + [_pspec(1)] * 3,
        out_specs=(_cspec(T), _cspec(T), _cspec(T), _cspec(T), _pspec(1), _pspec(1), _pspec(1)),
        out_shape=(S((Bl, T, C), F32), act, act, act, vec, vec, vec),
        compiler_params=_cparams("parallel", "arbitrary"), name=name)(dy3, xc3, gates3, gates3, proj3, h3, br, bi, lam)


NT = (((1,), (1,)), ((), ()))
TN = (((0,), (0,)), ((), ()))


def _hs(h):
    return slice(h * HEAD, (h + 1) * HEAD)


def _head_rows(x):
    head = lax.shift_right_logical(lax.broadcasted_iota(jnp.int32, x.shape, 1), HEAD.bit_length() - 1)
    return jnp.concatenate([jnp.where(head == h, x, jnp.zeros_like(x)) for h in range(MEM_HEADS)], axis=0)


def _head_sum(xbd):
    M = xbd.shape[0] // MEM_HEADS
    head = lax.shift_right_logical(lax.broadcasted_iota(jnp.int32, (M, xbd.shape[1]), 1), HEAD.bit_length() - 1)
    out = jnp.zeros((M, xbd.shape[1]), xbd.dtype)
    for h in range(MEM_HEADS):
        out = jnp.where(head == h, xbd[h * M:(h + 1) * M], out)
    return out


def _mem_probs(q, kbd):
    M = kbd.shape[0] // MEM_HEADS
    s = lax.dot_general(q, kbd, NT, preferred_element_type=F32) * (HEAD ** -0.5)
    ps = []
    for h in range(MEM_HEADS):
        sh = s[:, h * M:(h + 1) * M]
        e = jnp.exp(sh - jnp.max(sh, axis=-1, keepdims=True))
        ps.append(e / jnp.sum(e, axis=-1, keepdims=True))
    return ps


def _mem_attn_fwd(proj3, q_off, mkv3, name):
    Bl, T, _ = proj3.shape
    M = mkv3.shape[1]
    tq = _tile(T, (512, 256, 128))

    def body(q_ref, k_ref, v_ref, o_ref):
        q = q_ref[0].astype(BF16)
        kbd = _head_rows(k_ref[0].astype(BF16))
        vbd = _head_rows(v_ref[0].astype(BF16))
        p = jnp.concatenate(_mem_probs(q, kbd), axis=-1).astype(BF16)
        o_ref[0] = jnp.dot(p, vbd, preferred_element_type=F32).astype(BF16)

    return pl.pallas_call(
        body, grid=(Bl, T // tq),
        in_specs=[pl.BlockSpec((1, tq, MEM_W), lambda b, t: (b, t, q_off // MEM_W)),
                  pl.BlockSpec((1, M, MEM_W), lambda b, t: (b, 0, 0)),
                  pl.BlockSpec((1, M, MEM_W), lambda b, t: (b, 0, 1))],
        out_specs=pl.BlockSpec((1, tq, MEM_W), lambda b, t: (b, t, 0)),
        out_shape=S((Bl, T, MEM_W), BF16), compiler_params=_cparams("parallel", "parallel"), name=name)(
            proj3, mkv3, mkv3)


def _mem_attn_bwd(proj3, q_off, mkv3, do3, name):
    Bl, T, _ = proj3.shape
    M = mkv3.shape[1]
    tq = _tile(T, (512, 256, 128))
    scale = HEAD ** -0.5

    def body(q_ref, k_ref, v_ref, do_ref, dq_ref, dkv_ref):
        q = q_ref[0].astype(BF16)
        kbd = _head_rows(k_ref[0].astype(BF16))
        vbd = _head_rows(v_ref[0].astype(BF16))
        do = do_ref[0].astype(BF16)
        ps = _mem_probs(q, kbd)
        dvbd = lax.dot_general(jnp.concatenate(ps, axis=-1).astype(BF16), do, TN, preferred_element_type=F32)
        dp = lax.dot_general(do, vbd, NT, preferred_element_type=F32)
        dss = []
        for h in range(MEM_HEADS):
            dph = dp[:, h * M:(h + 1) * M]
            dss.append(ps[h] * (dph - jnp.sum(ps[h] * dph, axis=-1, keepdims=True)) * scale)
        ds = jnp.concatenate(dss, axis=-1).astype(BF16)
        dq_ref[0] = jnp.dot(ds, kbd, preferred_element_type=F32).astype(BF16)
        dkbd = lax.dot_general(ds, q, TN, preferred_element_type=F32)

        @pl.when(pl.program_id(1) == 0)
        def _():
            dkv_ref[...] = jnp.zeros_like(dkv_ref)

        dkv_ref[0] += jnp.concatenate([_head_sum(dkbd), _head_sum(dvbd)], axis=-1)

    return pl.pallas_call(
        body, grid=(Bl, T // tq),
        in_specs=[pl.BlockSpec((1, tq, MEM_W), lambda b, t: (b, t, q_off // MEM_W)),
                  pl.BlockSpec((1, M, MEM_W), lambda b, t: (b, 0, 0)),
                  pl.BlockSpec((1, M, MEM_W), lambda b, t: (b, 0, 1)),
                  pl.BlockSpec((1, tq, MEM_W), lambda b, t: (b, t, 0))],
        out_specs=(pl.BlockSpec((1, tq, MEM_W), lambda b, t: (b, t, 0)),
                   pl.BlockSpec((1, M, 2 * MEM_W), lambda b, t: (b, 0, 0))),
        out_shape=(S((Bl, T, MEM_W), BF16), S((Bl, M, 2 * MEM_W), F32)),
        compiler_params=_cparams("parallel", "arbitrary"), name=name)(proj3, mkv3, mkv3, do3)


GROUP_ROWS = SWA_GROUP * WIN


def _group_rows(x, kvh):
    return jnp.concatenate([x[:, _hs(SWA_GROUP * kvh + g)] for g in range(SWA_GROUP)], axis=0)


def _group_col(vals):
    grp = lax.shift_right_logical(lax.broadcasted_iota(jnp.int32, (GROUP_ROWS, 1), 0), WIN.bit_length() - 1)
    col = jnp.full((GROUP_ROWS, 1), vals[-1], F32)
    for g in range(SWA_GROUP - 2, -1, -1):
        col = jnp.where(grp == g, vals[g], col)
    return col


def _swa_probs(qh, kph, kch, sink, slope, has_prev):
    qi = jnp.bitwise_and(lax.broadcasted_iota(jnp.int32, (GROUP_ROWS, WIN), 0), WIN - 1)
    kj = lax.broadcasted_iota(jnp.int32, (GROUP_ROWS, WIN), 1)
    scale = HEAD ** -0.5
    sp = lax.dot_general(qh, kph, NT, preferred_element_type=F32) * scale
    sc = lax.dot_general(qh, kch, NT, preferred_element_type=F32) * scale
    dist_p = (qi + WIN - kj).astype(F32)
    dist_c = (qi - kj).astype(F32)
    neg = -jnp.inf
    sp = jnp.where(kj > qi + jnp.where(has_prev, 0, WIN), sp - slope * dist_p, neg)
    sc = jnp.where(kj <= qi, sc - slope * dist_c, neg)
    m = jnp.maximum(jnp.maximum(jnp.max(sp, axis=-1, keepdims=True), jnp.max(sc, axis=-1, keepdims=True)), sink)
    ep = jnp.exp(sp - m)
    ec = jnp.exp(sc - m)
    es = jnp.exp(sink - m)
    inv = 1.0 / (jnp.sum(ep, axis=-1, keepdims=True) + jnp.sum(ec, axis=-1, keepdims=True) + es)
    return ep * inv, ec * inv, es * inv


def _swa_specs(nb):
    prev = lambda n: jnp.maximum(n - 1, 0)
    q = pl.BlockSpec((1, WIN, MIX_W), lambda b, n: (b, n, 0))
    kp = pl.BlockSpec((1, WIN, MEM_W), lambda b, n: (b, prev(n), 0))
    kc = pl.BlockSpec((1, WIN, MEM_W), lambda b, n: (b, n, 0))
    vp = pl.BlockSpec((1, WIN, MEM_W), lambda b, n: (b, prev(n), 1))
    vc = pl.BlockSpec((1, WIN, MEM_W), lambda b, n: (b, n, 1))
    sm = pl.BlockSpec(memory_space=pltpu.SMEM)
    return q, kp, kc, vp, vc, sm


def _swa_fwd(proj3, kv3, sinks, name):
    Bl, T, _ = proj3.shape
    nb = T // WIN
    q_s, kp_s, kc_s, vp_s, vc_s, sm = _swa_specs(nb)

    def body(q_ref, kp_ref, kc_ref, vp_ref, vc_ref, sink_ref, o_ref):
        has_prev = pl.program_id(1) > 0
        q = q_ref[0].astype(BF16)
        kp, kc = kp_ref[0].astype(BF16), kc_ref[0].astype(BF16)
        vp, vc = vp_ref[0].astype(BF16), vc_ref[0].astype(BF16)
        outs = []
        for kvh in range(SWA_HEADS // SWA_GROUP):
            kvs = _hs(kvh)
            heads = range(SWA_GROUP * kvh, SWA_GROUP * (kvh + 1))
            pp, pc, _ = _swa_probs(_group_rows(q, kvh), kp[:, kvs], kc[:, kvs], _group_col([sink_ref[h] for h in heads]),
                                   _group_col([SLOPES[h] for h in heads]), has_prev)
            og = (jnp.dot(pp.astype(BF16), vp[:, kvs], preferred_element_type=F32)
                  + jnp.dot(pc.astype(BF16), vc[:, kvs], preferred_element_type=F32))
            outs += [og[g * WIN:(g + 1) * WIN] for g in range(SWA_GROUP)]
        o_ref[0] = jnp.concatenate(outs, axis=-1).astype(BF16)

    return pl.pallas_call(
        body, grid=(Bl, nb), in_specs=[q_s, kp_s, kc_s, vp_s, vc_s, sm], out_specs=q_s,
        out_shape=S((Bl, T, MIX_W), BF16), compiler_params=_cparams("parallel", "parallel"), name=name)(
            proj3, kv3, kv3, kv3, kv3, sinks)


def _swa_bwd(proj3, kv3, sinks, do3, name):
    Bl, T, _ = proj3.shape
    nb = T // WIN
    q_s, kp_s, kc_s, vp_s, vc_s, sm = _swa_specs(nb)
    kv_s = pl.BlockSpec((1, WIN, 2 * MEM_W), lambda b, n: (b, n, 0))
    sk_s = pl.BlockSpec((8, LANES), lambda b, n: (0, 0))
    scale = HEAD ** -0.5

    def body(q_ref, kp_ref, kc_ref, vp_ref, vc_ref, sink_ref, do_ref, dq_ref, dkc_ref, dkp_ref, dsk_ref):
        has_prev = pl.program_id(1) > 0
        q = q_ref[0].astype(BF16)
        kp, kc = kp_ref[0].astype(BF16), kc_ref[0].astype(BF16)
        vp, vc = vp_ref[0].astype(BF16), vc_ref[0].astype(BF16)
        do = do_ref[0].astype(BF16)
        lane = lax.broadcasted_iota(jnp.int32, (8, LANES), 1)
        srow = lax.broadcasted_iota(jnp.int32, (8, LANES), 0)
        dsk = jnp.zeros((8, LANES), F32)
        dqs = []
        dkc, dkp, dvc, dvp = [], [], [], []
        grp = lax.shift_right_logical(lax.broadcasted_iota(jnp.int32, (GROUP_ROWS, 1), 0), WIN.bit_length() - 1)
        for kvh in range(SWA_HEADS // SWA_GROUP):
            kvs = _hs(kvh)
            heads = range(SWA_GROUP * kvh, SWA_GROUP * (kvh + 1))
            qg, dog = _group_rows(q, kvh), _group_rows(do, kvh)
            pp, pc, ps = _swa_probs(qg, kp[:, kvs], kc[:, kvs], _group_col([sink_ref[h] for h in heads]),
                                    _group_col([SLOPES[h] for h in heads]), has_prev)
            dpp = lax.dot_general(dog, vp[:, kvs], NT, preferred_element_type=F32)
            dpc = lax.dot_general(dog, vc[:, kvs], NT, preferred_element_type=F32)
            delta = jnp.sum(pp * dpp, axis=-1, keepdims=True) + jnp.sum(pc * dpc, axis=-1, keepdims=True)
            dsp = (pp * (dpp - delta) * scale).astype(BF16)
            dsc = (pc * (dpc - delta) * scale).astype(BF16)
            dqg = (jnp.dot(dsp, kp[:, kvs], preferred_element_type=F32)
                   + jnp.dot(dsc, kc[:, kvs], preferred_element_type=F32))
            dqs += [dqg[g * WIN:(g + 1) * WIN] for g in range(SWA_GROUP)]
            dkc.append(lax.dot_general(dsc, qg, TN, preferred_element_type=F32))
            dkp.append(lax.dot_general(dsp, qg, TN, preferred_element_type=F32))
            dvc.append(lax.dot_general(pc.astype(BF16), dog, TN, preferred_element_type=F32))
            dvp.append(lax.dot_general(pp.astype(BF16), dog, TN, preferred_element_type=F32))
            dsink = ps * delta
            for g, h in enumerate(heads):
                dsk = dsk + jnp.where((lane == h) & (srow == 0), -jnp.sum(jnp.where(grp == g, dsink, 0.0)), 0.0)
        dq_ref[0] = jnp.concatenate(dqs, axis=-1).astype(BF16)
        dkc_ref[0] = jnp.concatenate(dkc + dvc, axis=-1)
        dkp_ref[0] = jnp.concatenate(dkp + dvp, axis=-1)

        @pl.when((pl.program_id(0) == 0) & (pl.program_id(1) == 0))
        def _():
            dsk_ref[...] = jnp.zeros_like(dsk_ref)

        dsk_ref[...] += dsk

    return pl.pallas_call(
        body, grid=(Bl, nb), in_specs=[q_s, kp_s, kc_s, vp_s, vc_s, sm, q_s], out_specs=(q_s, kv_s, kv_s, sk_s),
        out_shape=(S((Bl, T, MIX_W), BF16), S((Bl, T, 2 * MEM_W), F32), S((Bl, T, 2 * MEM_W), F32), S((8, LANES), F32)),
        compiler_params=_cparams("arbitrary", "arbitrary"), name=name)(proj3, kv3, kv3, kv3, kv3, sinks, do3)


def _kv_grad_combine(parts, name):
    Bl, T, W = parts[0][0].shape
    nb = T // WIN
    nl = len(parts)

    def body(*refs):
        o_ref = refs[-1]
        has_next = jnp.where(pl.program_id(1) == nb - 1, 0.0, 1.0)
        acc = None
        for l in range(nl):
            c = refs[2 * l][0] + has_next * refs[2 * l + 1][0]
            acc = c if acc is None else acc + c
        o_ref[0] = acc.astype(BF16)

    cur = pl.BlockSpec((1, WIN, W), lambda b, n: (b, n, 0))
    nxt = pl.BlockSpec((1, WIN, W), lambda b, n: (b, jnp.minimum(n + 1, nb - 1), 0))
    return pl.pallas_call(
        body, grid=(Bl, nb), in_specs=[cur, nxt] * nl, out_specs=cur, out_shape=S((Bl, T, W), BF16),
        compiler_params=_cparams("parallel", "parallel"), name=name)(*[a for pr in parts for a in pr])


def _loss_bwd(y, target, name="loss"):
    N, D = y.shape
    tm = _tile(N, (512, 256, 128))

    def body(y_ref, t_ref, dy_ref, l_ref):
        e = y_ref[...] - t_ref[...]
        dy_ref[...] = e * (1.0 / D)

        @pl.when(pl.program_id(0) == 0)
        def _():
            l_ref[...] = jnp.zeros_like(l_ref)

        l_ref[...] += jnp.sum(e * e, axis=0, keepdims=True) * (0.5 / D)

    row = pl.BlockSpec((tm, D), lambda i: (i, 0))
    vec = pl.BlockSpec((1, D), lambda i: (0, 0))
    return pl.pallas_call(
        body, grid=(N // tm,), in_specs=[row, row], out_specs=(row, vec), out_shape=(S((N, D), F32), S((1, D), F32)),
        compiler_params=_cparams("arbitrary"), name=name)(y, target)


def _all_gather(x, name):
    R, C = x.shape

    def body(x_ref, out_ref, send_sems, recv_sems, local_sem):
        mx, my, mc = lax.axis_index("x"), lax.axis_index("y"), lax.axis_index("c")
        me, sibling = (mx, my, mc), (mx, my, 1 - mc)
        chips = [(1 - mx, my), (mx, 1 - my), (1 - mx, 1 - my)]

        def rows(px, py, pc):
            return out_ref.at[4 * px + 2 * py + pc]

        def copy(kk, block, to, src=None):
            return pltpu.make_async_remote_copy(
                src_ref=rows(*block) if src is None else src, dst_ref=rows(*block), send_sem=send_sems.at[kk],
                recv_sem=recv_sems.at[kk], device_id=to, device_id_type=MESH)

        mine = pltpu.make_async_copy(x_ref, rows(*me), local_sem)
        mine.start()
        first = [copy(0, me, sibling, src=x_ref)]
        first += [copy(1 + j, me, (*chip, mc), src=x_ref) for j, chip in enumerate(chips)]
        for cp in first:
            cp.start()
        passed = [copy(4 + j, (*chip, mc), sibling) for j, chip in enumerate(chips)]
        for j, chip in enumerate(chips):
            copy(1 + j, (*chip, mc), me).wait_recv()
            passed[j].start()
        copy(0, sibling, me).wait_recv()
        for j, chip in enumerate(chips):
            copy(4 + j, (*chip, 1 - mc), me).wait_recv()
        for cp in first + passed:
            cp.wait_send()
        mine.wait()

    return pl.pallas_call(
        body, out_shape=S((N_DEV, R, C), x.dtype), in_specs=[ANY], out_specs=ANY,
        scratch_shapes=[pltpu.SemaphoreType.DMA((7,)), pltpu.SemaphoreType.DMA((7,)), pltpu.SemaphoreType.DMA(())],
        name=name)(x)


def _ag_weights(shards, row_sharded, name):
    n = len(shards)

    def full_shape(a, rows):
        if rows:
            return a.shape[:-2] + (N_DEV * a.shape[-2],) + a.shape[-1:]
        return (N_DEV,) + a.shape

    def body(*refs):
        x_refs, o_refs = refs[:n], refs[n:2 * n]
        send_sems, recv_sems, local_sems = refs[2 * n:]
        mx, my, mc = lax.axis_index("x"), lax.axis_index("y"), lax.axis_index("c")
        me, sibling = (mx, my, mc), (mx, my, 1 - mc)
        chips = [(1 - mx, my), (mx, 1 - my), (1 - mx, 1 - my)]

        def dst(t, px, py, pc):
            d = 4 * px + 2 * py + pc
            if not row_sharded[t]:
                return o_refs[t].at[d]
            r = shards[t].shape[-2]
            idx = (slice(None),) * (shards[t].ndim - 2) + (pl.ds(pl.multiple_of(d * r, 16), r), slice(None))
            return o_refs[t].at[idx]

        def copy(kk, t, block, to, src=None):
            return pltpu.make_async_remote_copy(
                src_ref=dst(t, *block) if src is None else src, dst_ref=dst(t, *block),
                send_sem=send_sems.at[kk * n + t], recv_sem=recv_sems.at[kk * n + t], device_id=to,
                device_id_type=MESH)

        mine = [pltpu.make_async_copy(x_refs[t], dst(t, *me), local_sems.at[t]) for t in range(n)]
        for cp in mine:
            cp.start()
        first = []
        for t in range(n):
            first.append(copy(0, t, me, sibling, src=x_refs[t]))
            first += [copy(1 + j, t, me, (*chip, mc), src=x_refs[t]) for j, chip in enumerate(chips)]
        for cp in first:
            cp.start()
        passed = []
        for j, chip in enumerate(chips):
            for t in range(n):
                copy(1 + j, t, (*chip, mc), me).wait_recv()
                cp = copy(4 + j, t, (*chip, mc), sibling)
                cp.start()
                passed.append(cp)
        for t in range(n):
            copy(0, t, sibling, me).wait_recv()
            for j, chip in enumerate(chips):
                copy(4 + j, t, (*chip, 1 - mc), me).wait_recv()
        for cp in first + passed:
            cp.wait_send()
        for cp in mine:
            cp.wait()

    return pl.pallas_call(
        body, out_shape=tuple(S(full_shape(a, r), a.dtype) for a, r in zip(shards, row_sharded)),
        in_specs=[ANY] * n, out_specs=tuple([ANY] * n),
        scratch_shapes=[pltpu.SemaphoreType.DMA((7 * n,)), pltpu.SemaphoreType.DMA((7 * n,)),
                        pltpu.SemaphoreType.DMA((n,))],
        name=name)(*shards)


FLIPS = [(fx, fy, fc) for fx in (0, 1) for fy in (0, 1) for fc in (0, 1)][1:]
HBM = pl.BlockSpec(memory_space=pltpu.HBM)
SEM = pl.BlockSpec(memory_space=pltpu.SEMAPHORE)
EFFECT = pltpu.SideEffectType.DATAFLOW_SIDE_EFFECTING


def _hbm(a):
    return pltpu.with_memory_space_constraint(a, pltpu.HBM)


def _flips(gather):
    return [(0, 0, 0)] + FLIPS if gather else FLIPS


def _split_copies(gather, s_refs, l_refs, send_sems, recv_sems):
    n = len(s_refs)
    mx, my, mc = lax.axis_index("x"), lax.axis_index("y"), lax.axis_index("c")
    me = 4 * mx + 2 * my + mc
    copies = []
    for k, (fx, fy, fc) in enumerate(_flips(gather)):
        px, py, pc = (1 - mx if fx else mx), (1 - my if fy else my), (1 - mc if fc else mc)
        for t in range(n):
            if gather:
                src = s_refs[t]
                r = src.shape[0]
                dst = l_refs[t].at[pl.ds(pl.multiple_of(me * r, 16), r), :]
            else:
                src = s_refs[t].at[:, 4 * px + 2 * py + pc]
                dst = l_refs[t].at[k]
            copies.append(pltpu.make_async_remote_copy(
                src_ref=src, dst_ref=dst, send_sem=send_sems.at[k * n + t], recv_sem=recv_sems.at[k * n + t],
                device_id=(px, py, pc), device_id_type=MESH))
    return copies


def _split_start(gather, srcs, lands, after, name):
    n = len(srcs)
    n_sem = len(_flips(gather)) * n

    def body(*refs):
        s_refs, l_refs = refs[:n], refs[n:2 * n]
        send_sems, recv_sems = refs[2 * n + 1], refs[2 * n + 2]
        token = refs[-1]
        for cp in _split_copies(gather, s_refs, l_refs, send_sems, recv_sems):
            cp.start()
        token[...] = jnp.zeros_like(token)

    outs = pl.pallas_call(
        body, name=name,
        out_shape=(pltpu.SemaphoreType.DMA((n_sem,)), pltpu.SemaphoreType.DMA((n_sem,)))
        + tuple(pltpu.HBM(a.shape, a.dtype) for a in lands) + (S((8, LANES), F32),),
        in_specs=[HBM] * (2 * n) + [ANY],
        out_specs=(SEM, SEM) + (HBM,) * n + (pl.BlockSpec(memory_space=pltpu.VMEM),),
        input_output_aliases={n + i: 2 + i for i in range(n)},
        compiler_params=pltpu.CompilerParams(has_side_effects=EFFECT),
    )(*[_hbm(a) for a in srcs], *[_hbm(a) for a in lands], after)
    return outs[0], outs[1], list(srcs), list(outs[2:2 + n]), outs[-1]


def _split_wait(gather, send_sems, recv_sems, srcs, lands, after, name):
    n = len(srcs)

    def body(*refs):
        s_refs, l_refs = refs[:n], refs[n:2 * n]
        ssem, rsem = refs[2 * n], refs[2 * n + 1]
        copies = _split_copies(gather, s_refs, l_refs, ssem, rsem)
        for cp in copies:
            cp.wait_send()
        for cp in copies:
            cp.wait_recv()

    outs = pl.pallas_call(
        body, name=name, out_shape=tuple(pltpu.HBM(a.shape, a.dtype) for a in lands),
        in_specs=[HBM] * (2 * n) + [SEM, SEM, ANY], out_specs=(HBM,) * n,
        input_output_aliases={n + i: i for i in range(n)},
        compiler_params=pltpu.CompilerParams(has_side_effects=EFFECT),
    )(*[_hbm(a) for a in srcs], *lands, send_sems, recv_sems, after)
    return list(outs)


def _adamw_math(w, g, m, v):
    m = ADAM_B1 * m + (1.0 - ADAM_B1) * g
    v = ADAM_B2 * v + (1.0 - ADAM_B2) * (g * g)
    m_hat = m / (1.0 - ADAM_B1 ** ADAM_STEP)
    v_hat = v / (1.0 - ADAM_B2 ** ADAM_STEP)
    delta = -ADAM_LR * (m_hat / (jnp.sqrt(v_hat) + ADAM_EPS) + ADAM_WD * w)
    return delta, m, v


def _adamw_layers(owns, gots, w, m, v, name):
    L, B, C = w.shape
    per_row = 2 * L * len(FLIPS) * C * owns[0].dtype.itemsize
    tb = max([t for t in range(16, B + 1, 16) if B % t == 0 and (t * per_row <= 24 * 1024 * 1024 or t == 16)] or [B])
    me = (4 * lax.axis_index("x") + 2 * lax.axis_index("y") + lax.axis_index("c")).astype(jnp.int32).reshape(1)

    def body(me_ref, *refs):
        own_refs, got_refs = refs[:L], refs[L:2 * L]
        w_ref, m_ref, v_ref = refs[2 * L:2 * L + 3]
        g_out, d_out, m_out, v_out = refs[2 * L + 3:]
        layer = pl.program_id(0)
        for kk in range(L):
            @pl.when(layer == kk)
            def _():
                g = own_refs[kk][0].astype(F32)
                for s in range(len(FLIPS)):
                    g = g + got_refs[kk][s].astype(F32)
                d, mn, vn = _adamw_math(w_ref[...], g, m_ref[...], v_ref[...])
                g_out[...] = g
                d_out[...] = d
                m_out[...] = mn
                v_out[...] = vn

    def row(kk, layer, i):
        return jnp.where(layer == kk, i, 0)

    blk = pl.BlockSpec((1, tb, C), lambda layer, i, me_ref: (layer, i, 0))
    own_specs = [pl.BlockSpec((1, 1, tb, C), lambda layer, i, me_ref, kk=kk: (0, me_ref[0], row(kk, layer, i), 0))
                 for kk in range(L)]
    got_specs = [pl.BlockSpec((len(FLIPS), 1, tb, C), lambda layer, i, me_ref, kk=kk: (0, 0, row(kk, layer, i), 0))
                 for kk in range(L)]
    return pl.pallas_call(
        body,
        grid_spec=pltpu.PrefetchScalarGridSpec(
            num_scalar_prefetch=1, grid=(L, B // tb), in_specs=own_specs + got_specs + [blk, blk, blk],
            out_specs=(blk, blk, blk, blk)),
        out_shape=(S((L, B, C), F32),) * 4, compiler_params=_cparams("arbitrary", "arbitrary"), name=name)(
            me, *owns, *gots, w, m, v)


def _adamw_replicated(parts, w, m, v, name):
    R, C = w.shape
    rb = _tile(R, (512, 256, 128, 64, 32, 16))

    def body(p_ref, w_ref, m_ref, v_ref, g_out, d_out, m_out, v_out):
        g = p_ref[0].astype(F32)
        for j in range(1, N_DEV):
            g = g + p_ref[j].astype(F32)
        d, mn, vn = _adamw_math(w_ref[...], g, m_ref[...], v_ref[...])
        g_out[...] = g
        d_out[...] = d
        m_out[...] = mn
        v_out[...] = vn

    blk = pl.BlockSpec((rb, C), lambda i: (i, 0))
    return pl.pallas_call(
        body, grid=(R // rb,), in_specs=[pl.BlockSpec((N_DEV, rb, C), lambda i: (0, i, 0)), blk, blk, blk],
        out_specs=(blk, blk, blk, blk), out_shape=(S((R, C), F32),) * 4, compiler_params=_cparams("parallel"),
        name=name)(parts, w, m, v)


def _pack(arrs, rows_mult, dtype):
    flat = jnp.concatenate([a.reshape(-1).astype(dtype) for a in arrs])
    n = flat.shape[0]
    per = rows_mult * LANES
    tot = -(-n // per) * per
    return jnp.pad(flat, (0, tot - n)).reshape(tot // LANES, LANES)


def _unpack(blob, shapes):
    flat = blob.reshape(-1)
    out, off = [], 0
    for shp in shapes:
        n = int(np.prod(shp))
        out.append(flat[off:off + n].reshape(shp))
        off += n
    return out


def _small_to_natural(g8):
    t = jnp.moveaxis(g8, 0, -2)
    return t.reshape(t.shape[:-2] + (N_DEV * t.shape[-1],))


def _small_to_cols(g):
    t = g.reshape(g.shape[:-1] + (N_DEV, g.shape[-1] // N_DEV))
    return jnp.moveaxis(t, -2, 0)


def _block_diag(w):
    nb, bs, _ = w.shape
    eye = jnp.eye(nb, dtype=w.dtype)
    return (eye[:, None, :, None] * w[:, :, None, :]).reshape(nb * bs, nb * bs)


def _diag_blocks(d, nb, bs):
    d4 = d.reshape(nb, bs, nb, bs)
    return jnp.stack([d4[i, :, i, :] for i in range(nb)])


def kernel(x, mem, g_mix_pre, g_mix_post, g_ffn_pre, g_ffn_post, g_mem, w_mem_kv, w_mix_out, w_ffn_up, w_ffn_conv, b_ffn_conv, w_ffn_down, w_in_a, w_conv_a, b_conv_a, w_rg_r, b_rg_r, w_rg_i, b_rg_i, lru_lambda, w_in_b, sinks_b, g_kv, w_kv, loss_target, m_g_mix_pre, m_g_mix_post, m_g_ffn_pre, m_g_ffn_post, m_g_mem, m_w_mem_kv, m_w_mix_out, m_w_ffn_up, m_w_ffn_conv, m_b_ffn_conv, m_w_ffn_down, m_w_in_a, m_w_conv_a, m_b_conv_a, m_w_rg_r, m_b_rg_r, m_w_rg_i, m_b_rg_i, m_lru_lambda, m_w_in_b, m_sinks_b, m_g_kv, m_w_kv, v_g_mix_pre, v_g_mix_post, v_g_ffn_pre, v_g_ffn_post, v_g_mem, v_w_mem_kv, v_w_mix_out, v_w_ffn_up, v_w_ffn_conv, v_b_ffn_conv, v_w_ffn_down, v_w_in_a, v_w_conv_a, v_b_conv_a, v_w_rg_r, v_b_rg_r, v_w_rg_i, v_b_rg_i, v_lru_lambda, v_w_in_b, v_sinks_b, v_g_kv, v_w_kv):
    w_loc = dict(g_mix_pre=g_mix_pre, g_mix_post=g_mix_post, g_ffn_pre=g_ffn_pre, g_ffn_post=g_ffn_post, g_mem=g_mem,
                 w_mem_kv=w_mem_kv, w_mix_out=w_mix_out, w_ffn_up=w_ffn_up, w_ffn_conv=w_ffn_conv,
                 b_ffn_conv=b_ffn_conv, w_ffn_down=w_ffn_down, w_in_a=w_in_a, w_conv_a=w_conv_a, b_conv_a=b_conv_a,
                 w_rg_r=w_rg_r, b_rg_r=b_rg_r, w_rg_i=w_rg_i, b_rg_i=b_rg_i, lru_lambda=lru_lambda, w_in_b=w_in_b,
                 sinks_b=sinks_b, g_kv=g_kv, w_kv=w_kv)
    m_loc = dict(g_mix_pre=m_g_mix_pre, g_mix_post=m_g_mix_post, g_ffn_pre=m_g_ffn_pre, g_ffn_post=m_g_ffn_post,
                 g_mem=m_g_mem, w_mem_kv=m_w_mem_kv, w_mix_out=m_w_mix_out, w_ffn_up=m_w_ffn_up,
                 w_ffn_conv=m_w_ffn_conv, b_ffn_conv=m_b_ffn_conv, w_ffn_down=m_w_ffn_down, w_in_a=m_w_in_a,
                 w_conv_a=m_w_conv_a, b_conv_a=m_b_conv_a, w_rg_r=m_w_rg_r, b_rg_r=m_b_rg_r, w_rg_i=m_w_rg_i,
                 b_rg_i=m_b_rg_i, lru_lambda=m_lru_lambda, w_in_b=m_w_in_b, sinks_b=m_sinks_b, g_kv=m_g_kv,
                 w_kv=m_w_kv)
    v_loc = dict(g_mix_pre=v_g_mix_pre, g_mix_post=v_g_mix_post, g_ffn_pre=v_g_ffn_pre, g_ffn_post=v_g_ffn_post,
                 g_mem=v_g_mem, w_mem_kv=v_w_mem_kv, w_mix_out=v_w_mix_out, w_ffn_up=v_w_ffn_up,
                 w_ffn_conv=v_w_ffn_conv, b_ffn_conv=v_b_ffn_conv, w_ffn_down=v_w_ffn_down, w_in_a=v_w_in_a,
                 w_conv_a=v_w_conv_a, b_conv_a=v_b_conv_a, w_rg_r=v_w_rg_r, b_rg_r=v_b_rg_r, w_rg_i=v_w_rg_i,
                 b_rg_i=v_b_rg_i, lru_lambda=v_lru_lambda, w_in_b=v_w_in_b, sinks_b=v_sinks_b, g_kv=v_g_kv,
                 w_kv=v_w_kv)

    Bl, T, D = x.shape
    Ml = mem.shape[1]
    N = Bl * T
    depth = g_mix_pre.shape[0]
    n_a = w_in_a.shape[0]
    F = w_ffn_down.shape[1] * N_DEV
    def as_rows(n, a):
        return jnp.swapaxes(a, -1, -2) if n in TRANSPOSED else a

    def mix_keys(l):
        keys = [("w_mem_kv", l), ("w_mix_out", l), ("w_in_a", l) if l < n_a else ("w_in_b", l - n_a)]
        return keys + ([("w_kv", None)] if l == n_a else [])

    def ffn_keys(l):
        return [("w_ffn_up", l), ("w_ffn_down", l)]

    def shard_of(key):
        n, i = key
        return as_rows(n, w_loc[n] if i is None else w_loc[n][i]).astype(BF16)

    W = {}
    keys0 = mix_keys(0)
    got0 = _ag_weights([shard_of(kk) for kk in keys0] + [w_loc[n] for n in SMALL_SHARDED],
                       [True] * len(keys0) + [False] * len(SMALL_SHARDED), name="ag_weights_0")
    W.update(zip(keys0, got0))
    for n, a in zip(SMALL_SHARDED, got0[len(keys0):]):
        W[n] = _small_to_natural(a)

    def gather_start(keys, after, tag):
        shards = [shard_of(kk) for kk in keys]
        lands = [lax.empty((N_DEV * s.shape[0],) + s.shape[1:], s.dtype) for s in shards]
        return (keys, tag) + _split_start(True, shards, lands, after, name=f"ag_start_{tag}")

    def gather_wait(pending, after):
        keys, tag, ssem, rsem, srcs, lands, _ = pending
        W.update(zip(keys, _split_wait(True, ssem, rsem, srcs, lands, after, name=f"ag_wait_{tag}")))

    pending_ffn = gather_start(ffn_keys(0), got0[0], "ffn_0")

    nblk, bsz = w_rg_r.shape[1], w_rg_r.shape[2]
    wbd = [jnp.concatenate([_block_diag(w_rg_r[j]), _block_diag(w_rg_i[j])], axis=1).astype(BF16) for j in range(n_a)]

    def vec(a):
        return a.reshape(1, -1)

    x2 = x.reshape(N, D)
    mem2 = mem.reshape(Bl * Ml, D)
    saved = []
    kvn = kv3 = x_kv = None
    xs = x2
    h1 = _rms_fwd(xs, vec(g_mix_pre[0]), BF16, name="rms_mixpre_0")
    for l in range(depth):
        sv = {"x0": xs}
        tok = None
        if l + 1 < depth:
            pending = gather_start(mix_keys(l + 1), pending_ffn[-1] if l == 0 else W[("w_mem_kv", l)], f"mix_{l + 1}")
            pending_next_ffn = gather_start(ffn_keys(l + 1), pending[-1], f"ffn_{l + 1}")
            tok = pending_next_ffn[-1]
        memn = _rms_fwd(mem2, vec(g_mem[l]), BF16, name=f"rms_mem_{l}")
        mkv3 = _mm(memn, W[("w_mem_kv", l)], after=tok, name=f"mm_memkv_{l}").reshape(Bl, Ml, 2 * MEM_W)
        if l < n_a:
            j = l
            proj = _mm(h1, W[("w_in_a", j)], tb=True, after=tok, name=f"mm_in_{l}")
            proj3 = proj.reshape(Bl, T, -1)
            xc3 = _conv_fwd_call(proj3, MIX_W, MIX_W, W["w_conv_a"][j], vec(W["b_conv_a"][j]), name=f"conv_a_{l}")
            gates3 = _mm(xc3.reshape(N, MIX_W), wbd[j], name=f"mm_gates_{l}").reshape(Bl, T, 2 * MIX_W)
            y_main3, hs3 = _rglru_fwd(xc3, gates3, proj3, vec(b_rg_r[j]), vec(b_rg_i[j]), vec(W["lru_lambda"][j]),
                                      name=f"rglru_fwd_{l}")
            q_off = 2 * MIX_W
            sv.update(xc3=xc3, gates3=gates3, hs3=hs3)
        else:
            j = l - n_a
            if l == n_a:
                x_kv = xs
                kv3 = _mm(kvn, W[("w_kv", None)], name="mm_kv").reshape(Bl, T, 2 * MEM_W)
            proj = _mm(h1, W[("w_in_b", j)], after=tok, name=f"mm_in_{l}")
            proj3 = proj.reshape(Bl, T, -1)
            y_main3 = _swa_fwd(proj3, kv3, sinks_b[j], name=f"swa_fwd_{l}")
            q_off = MIX_W
        y_mem3 = _mem_attn_fwd(proj3, q_off, mkv3, name=f"memattn_fwd_{l}")
        y_main = y_main3.reshape(N, MIX_W)
        y_mem = y_mem3.reshape(N, MEM_W)
        y = _mm_sum([(y_main, W[("w_mix_out", l)], (0, 0)), (y_mem, W[("w_mix_out", l)], (MIX_W, 0))], n=D,
                    name=f"mm_mixout_{l}")
        x1, h2 = _rms_pair_fwd(y, vec(g_mix_post[l]), xs, [vec(g_ffn_pre[l])], name=f"rms_mixpost_ffnpre_{l}")
        gather_wait(pending_ffn, h2)
        if l + 1 < depth:
            pending_ffn = pending_next_ffn
        u3 = _mm(h2, W[("w_ffn_up", l)], tb=True, name=f"mm_up_{l}").reshape(Bl, T, 2 * F)
        act3 = _ffn_mid_fwd(u3, W["w_ffn_conv"][l], vec(b_ffn_conv[l]), name=f"ffn_mid_fwd_{l}")
        act = act3.reshape(N, F)
        f = _mm(act, W[("w_ffn_down", l)], name=f"mm_down_{l}")
        sv.update(h1=h1, memn=memn, mkv3=mkv3, proj3=proj3, q_off=q_off, y_main=y_main, y_mem=y_mem, y=y, x1=x1,
                  h2=h2, u3=u3, act=act, f=f)
        saved.append(sv)
        if l + 1 < depth:
            gains = [vec(g_mix_pre[l + 1])] + ([vec(g_kv)] if l + 1 == n_a else [])
            xs, h1, *rest = _rms_pair_fwd(f, vec(g_ffn_post[l]), x1, gains, name=f"rms_ffnpost_mixpre_{l}")
            if rest:
                kvn = rest[0]
            gather_wait(pending, xs)
        else:
            xs = _rms_fwd(f, vec(g_ffn_post[l]), F32, res=x1, name=f"rms_ffnpost_{l}")

    dxs, loss_vec = _loss_bwd(xs, loss_target.reshape(N, D))
    loss = lax.psum(jnp.sum(loss_vec), ("x", "y", "c"))

    G = {n: [None] * w_loc[n].shape[0] for n in REPL + SMALL_SHARDED if n != "g_kv"}
    GW = {}

    def dw(key, off, a, b_, nm):
        GW[key] = _mm(a, b_, ta=True, out_dtype=BF16, into=(GW.get(key), (1,) + W[key].shape, 0, off), name=nm)

    def grad_blocks(key):
        g = GW[key]
        return g.reshape(1, N_DEV, g.shape[1] // N_DEV, g.shape[2])

    reduces = []

    def reduce_start(keys, after, tag):
        srcs = [grad_blocks(kk) for kk in keys]
        lands = [lax.empty((len(FLIPS),) + s.shape[:1] + s.shape[2:], s.dtype) for s in srcs]
        started = _split_start(False, srcs, lands, after, name=f"rs_start_{tag}")
        reduces.append((keys, tag) + started)
        return started[-1]

    kv_parts = []
    df = None
    for l in reversed(range(depth)):
        sv = saved[l]
        proj3 = sv["proj3"]
        if df is None:
            df, dg = _rms_bwd(sv["f"], vec(g_ffn_post[l]), dxs, out_dtype=BF16, name=f"rmsb_ffnpost_{l}")
            G["g_ffn_post"][l] = dg[0]
        dact = _mm(df, W[("w_ffn_down", l)], tb=True, name=f"mmb_down_dx_{l}")
        dw(("w_ffn_down", l), (0, 0), sv["act"], df, f"mmb_down_dw_{l}")
        dug3, duv3, dwg, dwv, dbg, dbv = _ffn_mid_bwd(sv["u3"], dact.reshape(Bl, T, F),
                                                      W["w_ffn_conv"][l], vec(b_ffn_conv[l]), name=f"ffn_mid_bwd_{l}")
        G["w_ffn_conv"][l] = jnp.concatenate([dwg, dwv], axis=1)
        G["b_ffn_conv"][l] = jnp.concatenate([dbg, dbv], axis=1)[0]
        dug, duv = dug3.reshape(N, F), duv3.reshape(N, F)
        dw(("w_ffn_up", l), (0, 0), dug, sv["h2"], f"mmb_up_dw_g_{l}")
        dw(("w_ffn_up", l), (F, 0), duv, sv["h2"], f"mmb_up_dw_v_{l}")
        tok = reduce_start([("w_ffn_down", l), ("w_ffn_up", l)], dug, f"ffn_{l}")
        dh2 = _mm_sum([(dug, W[("w_ffn_up", l)], (0, 0)), (duv, W[("w_ffn_up", l)], (F, 0))], n=D, after=tok,
                      name=f"mmb_up_dx_{l}")
        dx1, dy, dg, dg2 = _rms_pair_bwd(sv["x1"], vec(g_ffn_pre[l]), dh2, dxs, sv["y"], vec(g_mix_post[l]),
                                         name=f"rmsb_ffnpre_mixpost_{l}")
        G["g_ffn_pre"][l] = dg[0]
        G["g_mix_post"][l] = dg2[0]
        dy_main = _mm(dy, W[("w_mix_out", l)], tb=True, n=MIX_W, k=D, name=f"mmb_mixout_dmain_{l}")
        dy_mem = _mm(dy, W[("w_mix_out", l)], tb=True, n=MEM_W, k=D, b_off=(MIX_W, 0),
                     name=f"mmb_mixout_dmem_{l}")
        dw(("w_mix_out", l), (0, 0), sv["y_main"], dy, f"mmb_mixout_dw_main_{l}")
        dw(("w_mix_out", l), (MIX_W, 0), sv["y_mem"], dy, f"mmb_mixout_dw_mem_{l}")
        dq_mem3, dmkv3 = _mem_attn_bwd(proj3, sv["q_off"], sv["mkv3"], dy_mem.reshape(Bl, T, MEM_W),
                                       name=f"memattn_bwd_{l}")
        dq_mem = dq_mem3.reshape(N, MEM_W)
        dmkv = dmkv3.reshape(Bl * Ml, 2 * MEM_W)
        dw(("w_mem_kv", l), (0, 0), sv["memn"], dmkv, f"mmb_memkv_dw_{l}")
        dmemn = _mm(dmkv, W[("w_mem_kv", l)], tb=True, name=f"mmb_memkv_dx_{l}")
        _, dg = _rms_bwd(mem2, vec(g_mem[l]), dmemn, name=f"rmsb_mem_{l}")
        G["g_mem"][l] = dg[0]
        dy_main3 = dy_main.reshape(Bl, T, MIX_W)
        if l < n_a:
            j = l
            dxc3, drp3, dip3, dugate3, dbr, dbi, dlam = _rglru_bwd(
                dy_main3, sv["xc3"], sv["gates3"], proj3, sv["hs3"], vec(b_rg_r[j]), vec(b_rg_i[j]),
                vec(W["lru_lambda"][j]), name=f"rglru_bwd_{l}")
            G["b_rg_r"][j] = dbr.reshape(nblk, bsz)
            G["b_rg_i"][j] = dbi.reshape(nblk, bsz)
            G["lru_lambda"][j] = dlam[0]
            drp, dip = drp3.reshape(N, MIX_W), dip3.reshape(N, MIX_W)
            xc2 = sv["xc3"].reshape(N, MIX_W)
            G["w_rg_r"][j] = _diag_blocks(_mm(xc2, drp, ta=True, name=f"mmb_gates_dw_r_{l}"), nblk, bsz)
            G["w_rg_i"][j] = _diag_blocks(_mm(xc2, dip, ta=True, name=f"mmb_gates_dw_i_{l}"), nblk, bsz)
            dxc = _mm_sum([(drp, wbd[j], (0, 0)), (dip, wbd[j], (0, MIX_W))], tb=True, n=MIX_W,
                          add=dxc3.reshape(N, MIX_W), name=f"mmb_gates_dx_{l}")
            dux3, dwc, dbc = _conv_bwd_call(dxc.reshape(Bl, T, MIX_W), proj3, MIX_W, MIX_W, W["w_conv_a"][j],
                                            name=f"conv_a_bwd_{l}")
            G["w_conv_a"][j] = dwc
            G["b_conv_a"][j] = dbc[0]
            pieces = [(dugate3.reshape(N, MIX_W), 0), (dux3.reshape(N, MIX_W), MIX_W), (dq_mem, 2 * MIX_W)]
            in_key = ("w_in_a", j)
        else:
            j = l - n_a
            dq3, dkc, dkp, dsk = _swa_bwd(proj3, kv3, sinks_b[j], dy_main3, name=f"swa_bwd_{l}")
            kv_parts.append((dkc, dkp))
            G["sinks_b"][j] = dsk[0, :SWA_HEADS]
            pieces = [(dq3.reshape(N, MIX_W), 0), (dq_mem, MIX_W)]
            in_key = ("w_in_b", j)
        in_t = in_key[0] in TRANSPOSED
        for pi, (piece, off) in enumerate(pieces):
            if in_t:
                dw(in_key, (off, 0), piece, sv["h1"], f"mmb_in_dw_{pi}_{l}")
            else:
                dw(in_key, (0, off), sv["h1"], piece, f"mmb_in_dw_{pi}_{l}")
        tok = reduce_start([("w_mix_out", l), ("w_mem_kv", l), in_key], dy, f"mix_{l}")
        dh1 = _mm_sum([(piece, W[in_key], (off, 0) if in_t else (0, off)) for piece, off in pieces], tb=not in_t, n=D,
                      after=tok, name=f"mmb_in_dx_{l}")
        if l > 0 and l != n_a:
            dxs, df, dg, dg2 = _rms_pair_bwd(sv["x0"], vec(g_mix_pre[l]), dh1, dx1, saved[l - 1]["f"],
                                             vec(g_ffn_post[l - 1]), name=f"rmsb_mixpre_ffnpost_{l}")
            G["g_ffn_post"][l - 1] = dg2[0]
        else:
            dxs, dg = _rms_bwd(sv["x0"], vec(g_mix_pre[l]), dh1, add=dx1, name=f"rmsb_mixpre_{l}")
            df = None
        G["g_mix_pre"][l] = dg[0]
        if l == n_a:
            dkv = _kv_grad_combine(kv_parts, name="kv_grad_combine").reshape(N, 2 * MEM_W)
            dw(("w_kv", None), (0, 0), kvn, dkv, "mmb_kv_dw")
            tok = reduce_start([("w_kv", None)], dkv, "kv")
            dkvn = _mm(dkv, W[("w_kv", None)], tb=True, after=tok, name="mmb_kv_dx")
            dxs, dg = _rms_bwd(x_kv, vec(g_kv), dkvn, add=dxs, name="rmsb_kv")
            G["g_kv"] = dg[0]
    grad_x = dxs.reshape(Bl, T, D)
    Gf = {n: (jnp.stack(g) if isinstance(g, list) else g) for n, g in G.items()}

    small4 = []
    for n in SMALL_SHARDED:
        t = _small_to_cols(Gf[n]).astype(BF16)
        small4.append(t.reshape(1, N_DEV, -1, t.shape[-1]))
    small_lands = [lax.empty((len(FLIPS),) + s.shape[:1] + s.shape[2:], s.dtype) for s in small4]
    small_started = _split_start(False, small4, small_lands, dxs, name="rs_start_small")
    r_blob = _pack([Gf[n] for n in REPL], REPL_ROWS, BF16)
    r_parts = _all_gather(r_blob, name="ag_repl_grads")
    parts = {}
    for keys, tag, ssem, rsem, srcs, lands, _ in reduces:
        for kk, s, g7 in zip(keys, srcs, _split_wait(False, ssem, rsem, srcs, lands, small_started[-1],
                                                     name=f"rs_wait_{tag}")):
            parts[kk] = (s, g7)

    res = [{} for _ in range(4)]
    for n, _ in SHARDED:
        if n in SMALL_SHARDED:
            continue
        idx = [None] if w_loc[n].ndim == 2 else list(range(w_loc[n].shape[0]))
        wmv = [as_rows(n, a[n]) for a in (w_loc, m_loc, v_loc)]
        shp3 = (len(idx),) + wmv[0].shape[-2:]
        outs = _adamw_layers([parts[(n, i)][0] for i in idx], [parts[(n, i)][1] for i in idx],
                             *[a.reshape(shp3) for a in wmv], name=f"adamw_{n}")
        for k in range(4):
            res[k][n] = as_rows(n, outs[k].reshape(wmv[0].shape))
    last = res[0]["w_kv"]
    small_got = _split_wait(False, *small_started[:4], last, name="rs_wait_small")
    for n, own, g7 in zip(SMALL_SHARDED, small4, small_got):
        shp3 = own.shape[:1] + own.shape[2:]
        outs = _adamw_layers([own], [g7], w_loc[n].reshape(shp3), m_loc[n].reshape(shp3), v_loc[n].reshape(shp3),
                             name=f"adamw_{n}")
        for k in range(4):
            res[k][n] = outs[k].reshape(w_loc[n].shape)
    outs_rp = _adamw_replicated(r_parts, _pack([w_loc[n] for n in REPL], REPL_ROWS, F32),
                                _pack([m_loc[n] for n in REPL], REPL_ROWS, F32),
                                _pack([v_loc[n] for n in REPL], REPL_ROWS, F32),
                                name="adamw_replicated")
    rp_shapes = [w_loc[n].shape for n in REPL]
    for k in range(4):
        res[k].update(zip(REPL, _unpack(outs_rp[k], rp_shapes)))
    out = [loss, grad_x]
    for k in range(4):
        out += [res[k][n] for n in WEIGHTS]
    return tuple(out)
```

```python
import math

import numpy as np
import jax
import jax.numpy as jnp
from jax import lax
from jax.experimental import pallas as pl
from jax.experimental.pallas import tpu as pltpu

F32 = jnp.float32
BF16 = jnp.bfloat16
S = jax.ShapeDtypeStruct
MESH = pl.DeviceIdType.MESH
ANY = pl.BlockSpec(memory_space=pl.ANY)

HEAD = 64
MEM_HEADS = 4
MEM_W = MEM_HEADS * HEAD
SWA_HEADS = 12
SWA_GROUP = 3
MIX_W = SWA_HEADS * HEAD
WIN = 128
LRU_C = 8.0
EPS = 1e-6
ADAM_LR, ADAM_B1, ADAM_B2, ADAM_EPS, ADAM_WD, ADAM_STEP = 0.001, 0.9, 0.999, 1e-08, 0.01, 10
GELU_C0 = math.sqrt(2.0 / math.pi)
GELU_C1 = 0.044715
N_DEV = 8
LANES = 128
CT = 128
VMEM_LIMIT = 48 * 1024 * 1024
MM_VMEM_BUDGET = 36 * 1024 * 1024
REPL_ROWS = 256

SHARDED = (("w_mem_kv", 1), ("w_mix_out", 1), ("w_ffn_up", 2), ("w_ffn_conv", 2), ("w_ffn_down", 1), ("w_in_a", 2),
           ("w_conv_a", 2), ("b_conv_a", 1), ("lru_lambda", 1), ("w_in_b", 1), ("w_kv", 0))
SMALL_SHARDED = ("w_ffn_conv", "w_conv_a", "b_conv_a", "lru_lambda")
TRANSPOSED = ("w_ffn_up", "w_in_a")
REPL = ("g_mix_pre", "g_mix_post", "g_ffn_pre", "g_ffn_post", "g_mem", "b_ffn_conv", "w_rg_r", "b_rg_r", "w_rg_i",
        "b_rg_i", "sinks_b", "g_kv")
WEIGHTS = ("g_mix_pre", "g_mix_post", "g_ffn_pre", "g_ffn_post", "g_mem", "w_mem_kv", "w_mix_out", "w_ffn_up",
           "w_ffn_conv", "b_ffn_conv", "w_ffn_down", "w_in_a", "w_conv_a", "b_conv_a", "w_rg_r", "b_rg_r", "w_rg_i",
           "b_rg_i", "lru_lambda", "w_in_b", "sinks_b", "g_kv", "w_kv")


def _alibi_slopes(n):
    def pow2(m):
        start = 2.0 ** (-8.0 / m)
        return [start ** (i + 1) for i in range(m)]
    c = 2 ** int(math.floor(math.log2(n)))
    s = pow2(c)
    if c != n:
        s = s + pow2(2 * c)[0::2][: n - c]
    return [float(v) for v in np.asarray(s, dtype=np.float32)]


SLOPES = _alibi_slopes(SWA_HEADS)


def _tile(n, cands):
    for c in cands:
        if n % c == 0:
            return c
    return n


def _cparams(*sem):
    return pltpu.CompilerParams(dimension_semantics=sem, vmem_limit_bytes=VMEM_LIMIT)


def _mm_tiles(M, N, K, a_bytes, b_bytes, o_bytes, add_bytes, offsets):
    m_off, n_offs, k_off = offsets
    tms = [c for c in (1024, 512, 256, 128) if M % c == 0 and m_off % c == 0] or [M]
    tns = [c for c in (1408, 1024, 896, 768, 512, 384, 256, 128)
           if N % c == 0 and all(o % c == 0 for o in n_offs)] or [N]
    tks = [c for c in (K, 2048, 1408, 1024, 512, 256, 128) if c <= K and K % c == 0 and k_off % c == 0]
    best = None
    for tk in tks:
        fits = []
        for tm in tms:
            for tn in tns:
                need = 2 * (tm * tk * a_bytes + tk * tn * b_bytes + tm * tn * (o_bytes + add_bytes))
                need += tm * tn * 4 * (2 if tk < K else 1)
                need += (tm * tk * 2 if a_bytes != 2 else 0) + (tk * tn * 2 if b_bytes != 2 else 0)
                if need <= MM_VMEM_BUDGET:
                    fits.append((tm * tn, min(tm, 512), tm, tn))
        if fits:
            _, _, tm, tn = max(fits)
            best = (tm, tn, tk)
            break
    assert best is not None, (M, N, K)
    return best


def _mm(a, b, *, ta=False, tb=False, n=None, k=None, b_off=(0, 0), out_dtype=F32, add=None, into=None, after=None,
        name="mm"):
    if ta:
        K, M = a.shape
    else:
        M, K = a.shape
    if tb:
        N = b.shape[-2] if n is None else n
    else:
        N = b.shape[-1] if n is None else n
    assert k is None or k == K
    ro, co = b_off
    n_off, k_off = (ro, co) if tb else (co, ro)
    oro, oco = (0, 0) if into is None else into[3]
    tm, tn, tk = _mm_tiles(M, N, K, a.dtype.itemsize, b.dtype.itemsize, jnp.dtype(out_dtype).itemsize,
                           0 if add is None else add.dtype.itemsize, (oro, (n_off, oco), k_off))
    nk = K // tk
    if tb:
        b_spec = pl.BlockSpec((tn, tk), lambda i, j, kk: (j + ro // tn, kk + co // tk))
        b_dims = (1,)
    else:
        b_spec = pl.BlockSpec((tk, tn), lambda i, j, kk: (kk + ro // tk, j + co // tn))
        b_dims = (0,)
    if ta:
        a_spec = pl.BlockSpec((tk, tm), lambda i, j, kk: (kk, i))
        a_dims = (0,)
    else:
        a_spec = pl.BlockSpec((tm, tk), lambda i, j, kk: (i, kk))
        a_dims = (1,)
    dims = ((a_dims, b_dims), ((), ()))
    add_spec = pl.BlockSpec((tm, tn), lambda i, j, kk: (i, j))
    has_add = add is not None
    if into is None:
        o_spec, o_shape, buf = add_spec, (M, N), None
    else:
        buf, o_shape, ol, _ = into
        assert not has_add
        o_spec = pl.BlockSpec((None, tm, tn), lambda i, j, kk: (ol, i + oro // tm, j + oco // tn))
    has_buf = buf is not None

    def body(*refs):
        refs = list(refs)
        acc_ref = refs.pop() if nk > 1 else None
        o_ref = refs.pop()
        a_ref, b_ref = refs[0], refs[1]
        add_ref = refs[2] if has_add else None
        part = lax.dot_general(a_ref[...].astype(BF16), b_ref[...].astype(BF16), dims, preferred_element_type=F32)

        def finish(r):
            if has_add:
                r = r + add_ref[...].astype(F32)
            o_ref[...] = r.astype(out_dtype)

        if nk == 1:
            finish(part)
        else:
            kk = pl.program_id(2)

            @pl.when(kk == 0)
            def _():
                acc_ref[...] = part

            @pl.when(kk > 0)
            def _():
                acc_ref[...] += part

            @pl.when(kk == nk - 1)
            def _():
                finish(acc_ref[...])

    in_specs = [a_spec, b_spec] + ([add_spec] if has_add else []) + ([ANY] if has_buf else [])
    args = (a, b) + ((add,) if has_add else ()) + ((buf,) if has_buf else ())
    if after is not None:
        in_specs, args = in_specs + [ANY], args + (after,)
    return pl.pallas_call(
        body, grid=(M // tm, N // tn, nk), in_specs=in_specs, out_specs=o_spec,
        out_shape=S(o_shape, out_dtype), scratch_shapes=[pltpu.VMEM((tm, tn), F32)] if nk > 1 else [],
        input_output_aliases={2: 0} if has_buf else {},
        compiler_params=_cparams("parallel", "parallel", "arbitrary"), name=name)(*args)


def _mm_sum(pieces, *, tb=False, n, out_dtype=F32, add=None, after=None, name="mm_sum"):
    M = pieces[0][0].shape[0]
    ks = [a.shape[1] for a, _, _ in pieces]
    a_bytes = max(a.dtype.itemsize for a, _, _ in pieces)
    b_bytes = max(b.dtype.itemsize for _, b, _ in pieces)
    n_offs = tuple(off[0] if tb else off[1] for _, _, off in pieces)
    for kp, (_, _, off) in zip(ks, pieces):
        assert (off[1] if tb else off[0]) % kp == 0
    tm, tn, tk = _mm_tiles(M, n, sum(ks), a_bytes, b_bytes, jnp.dtype(out_dtype).itemsize, 0, (0, n_offs, 0))
    assert tk == sum(ks)
    a_specs = [pl.BlockSpec((tm, kp), lambda i, j: (i, 0)) for kp in ks]
    if tb:
        b_specs = [pl.BlockSpec((tn, kp), lambda i, j, ro=off[0], co=off[1], kp=kp: (j + ro // tn, co // kp))
                   for kp, (_, _, off) in zip(ks, pieces)]
        dims = NT
    else:
        b_specs = [pl.BlockSpec((kp, tn), lambda i, j, ro=off[0], co=off[1], kp=kp: (ro // kp, j + co // tn))
                   for kp, (_, _, off) in zip(ks, pieces)]
        dims = (((1,), (0,)), ((), ()))
    npc = len(pieces)
    o_spec = pl.BlockSpec((tm, tn), lambda i, j: (i, j))

    def body(*refs):
        o_ref = refs[2 * npc + (add is not None) + (after is not None)]
        acc = refs[2 * npc][...].astype(F32) if add is not None else None
        for p in range(npc):
            part = lax.dot_general(refs[p][...].astype(BF16), refs[npc + p][...].astype(BF16), dims,
                                   preferred_element_type=F32)
            acc = part if acc is None else acc + part
        o_ref[...] = acc.astype(out_dtype)

    args = [a for a, _, _ in pieces] + [b for _, b, _ in pieces]
    in_specs = a_specs + b_specs
    if add is not None:
        in_specs, args = in_specs + [o_spec], args + [add]
    if after is not None:
        in_specs, args = in_specs + [ANY], args + [after]
    return pl.pallas_call(
        body, grid=(M // tm, n // tn), in_specs=in_specs, out_specs=o_spec,
        out_shape=S((M, n), out_dtype), compiler_params=_cparams("parallel", "parallel"), name=name)(*args)


def _rms_fwd(x, g, out_dtype, res=None, name="rms_fwd"):
    N, D = x.shape
    tm = _tile(N, (512, 256, 128))
    has_res = res is not None

    def body(*refs):
        if has_res:
            x_ref, g_ref, r_ref, o_ref = refs
        else:
            x_ref, g_ref, o_ref = refs
        xv = x_ref[...].astype(F32)
        y = xv * lax.rsqrt(jnp.mean(xv * xv, axis=-1, keepdims=True) + EPS) * g_ref[...]
        if has_res:
            y = y + r_ref[...]
        o_ref[...] = y.astype(out_dtype)

    row = pl.BlockSpec((tm, D), lambda i: (i, 0))
    vec = pl.BlockSpec((1, D), lambda i: (0, 0))
    return pl.pallas_call(
        body, grid=(N // tm,), in_specs=[row, vec] + ([row] if has_res else []), out_specs=row,
        out_shape=S((N, D), out_dtype), compiler_params=_cparams("parallel"), name=name)(
            *((x, g) + ((res,) if has_res else ())))


def _rms_bwd(x, g, dy, add=None, out_dtype=F32, name="rms_bwd"):
    N, D = x.shape
    tm = _tile(N, (512, 256, 128))
    has_add = add is not None

    def body(*refs):
        if has_add:
            x_ref, g_ref, dy_ref, add_ref, dx_ref, dg_ref = refs
        else:
            x_ref, g_ref, dy_ref, dx_ref, dg_ref = refs
        xv = x_ref[...].astype(F32)
        dyv = dy_ref[...].astype(F32)
        r = lax.rsqrt(jnp.mean(xv * xv, axis=-1, keepdims=True) + EPS)
        u = dyv * g_ref[...]
        dx = r * u - xv * (r * r * r * jnp.mean(u * xv, axis=-1, keepdims=True))
        if has_add:
            dx = dx + add_ref[...]
        dx_ref[...] = dx.astype(out_dtype)

        @pl.when(pl.program_id(0) == 0)
        def _():
            dg_ref[...] = jnp.zeros_like(dg_ref)

        dg_ref[...] += jnp.sum(dyv * xv * r, axis=0, keepdims=True)

    row = pl.BlockSpec((tm, D), lambda i: (i, 0))
    vec = pl.BlockSpec((1, D), lambda i: (0, 0))
    return pl.pallas_call(
        body, grid=(N // tm,), in_specs=[row, vec, row] + ([row] if has_add else []), out_specs=(row, vec),
        out_shape=(S((N, D), out_dtype), S((1, D), F32)), compiler_params=_cparams("arbitrary"), name=name)(
            *((x, g, dy) + ((add,) if has_add else ())))


def _rms_pair_fwd(y, g_post, res, gains, name):
    N, D = y.shape
    tm = _tile(N, (512, 256, 128))
    ng = len(gains)

    def body(*refs):
        y_ref, gp_ref, r_ref = refs[:3]
        g_refs = refs[3:3 + ng]
        x_ref = refs[3 + ng]
        h_refs = refs[4 + ng:]
        yv = y_ref[...]
        x = r_ref[...] + yv * lax.rsqrt(jnp.mean(yv * yv, axis=-1, keepdims=True) + EPS) * gp_ref[...]
        x_ref[...] = x
        xn = x * lax.rsqrt(jnp.mean(x * x, axis=-1, keepdims=True) + EPS)
        for g_ref, h_ref in zip(g_refs, h_refs):
            h_ref[...] = (xn * g_ref[...]).astype(BF16)

    row = pl.BlockSpec((tm, D), lambda i: (i, 0))
    vec = pl.BlockSpec((1, D), lambda i: (0, 0))
    return pl.pallas_call(
        body, grid=(N // tm,), in_specs=[row, vec, row] + [vec] * ng, out_specs=(row,) * (1 + ng),
        out_shape=(S((N, D), F32),) + (S((N, D), BF16),) * ng, compiler_params=_cparams("parallel"), name=name)(
            y, g_post, res, *gains)


def _rms_pair_bwd(xa, ga, dya, add, xb, gb, name):
    N, D = xa.shape
    tm = _tile(N, (512, 256, 128))

    def one(xv, g_ref, dyv):
        r = lax.rsqrt(jnp.mean(xv * xv, axis=-1, keepdims=True) + EPS)
        u = dyv * g_ref[...]
        dx = r * u - xv * (r * r * r * jnp.mean(u * xv, axis=-1, keepdims=True))
        return dx, jnp.sum(dyv * xv * r, axis=0, keepdims=True)

    def body(xa_ref, ga_ref, dya_ref, add_ref, xb_ref, gb_ref, da_ref, db_ref, dga_ref, dgb_ref):
        da, dga = one(xa_ref[...].astype(F32), ga_ref, dya_ref[...].astype(F32))
        da = da + add_ref[...]
        da_ref[...] = da
        db, dgb = one(xb_ref[...].astype(F32), gb_ref, da)
        db_ref[...] = db.astype(BF16)

        @pl.when(pl.program_id(0) == 0)
        def _():
            dga_ref[...] = jnp.zeros_like(dga_ref)
            dgb_ref[...] = jnp.zeros_like(dgb_ref)

        dga_ref[...] += dga
        dgb_ref[...] += dgb

    row = pl.BlockSpec((tm, D), lambda i: (i, 0))
    vec = pl.BlockSpec((1, D), lambda i: (0, 0))
    return pl.pallas_call(
        body, grid=(N // tm,), in_specs=[row, vec, row, row, row, vec], out_specs=(row, row, vec, vec),
        out_shape=(S((N, D), F32), S((N, D), BF16), S((1, D), F32), S((1, D), F32)),
        compiler_params=_cparams("arbitrary"), name=name)(xa, ga, dya, add, xb, gb)


def _shift_down(x, s, row):
    return jnp.where(row >= s, pltpu.roll(x, s, axis=0), 0.0)


def _shift_up(x, s, row):
    T = x.shape[0]
    return jnp.where(row < T - s, pltpu.roll(x, T - s, axis=0), 0.0)


SLAB = 16


def _conv_rows(x_ref, w_ref, b_ref, lo, hi):
    W = w_ref.shape[0]
    y = x_ref[lo:hi, :] * w_ref[W - 1:W, :] + b_ref[...]
    for s in range(1, W):
        y = y + x_ref[lo - s:hi - s, :] * w_ref[W - 1 - s:W - s, :]
    return y


def _taps(x_ref, W):
    T = x_ref.shape[0]
    head = x_ref[0:SLAB, :]
    row = lax.broadcasted_iota(jnp.int32, head.shape, 0)
    return [x_ref[...]] + [jnp.concatenate([_shift_down(head, s, row), x_ref[SLAB - s:T - s, :]], axis=0)
                           for s in range(1, W)]


def _conv_taps(xs, w_ref, b_ref):
    W = w_ref.shape[0]
    y = xs[0] * w_ref[W - 1:W, :] + b_ref[...]
    for s in range(1, W):
        y = y + xs[s] * w_ref[W - 1 - s:W - s, :]
    return y


def _conv_head(x_head, w_ref, b_ref):
    row = lax.broadcasted_iota(jnp.int32, x_head.shape, 0)
    return _conv_taps([x_head] + [_shift_down(x_head, s, row) for s in range(1, w_ref.shape[0])], w_ref, b_ref)


def _conv_bwd_taps(dy, xs, w_ref, row):
    W = w_ref.shape[0]
    dx = dy * w_ref[W - 1:W, :]
    dws = [None] * W
    dws[W - 1] = jnp.sum(dy * xs[0], axis=0, keepdims=True)
    for s in range(1, W):
        dx = dx + _shift_up(dy, s, row) * w_ref[W - 1 - s:W - s, :]
        dws[W - 1 - s] = jnp.sum(dy * xs[s], axis=0, keepdims=True)
    return dx, jnp.concatenate(dws, axis=0), jnp.sum(dy, axis=0, keepdims=True)


def _gelu(g):
    t = jnp.tanh(GELU_C0 * (g + GELU_C1 * g * g * g))
    return 0.5 * g * (1.0 + t), t


def _dgelu(g, t):
    return 0.5 * (1.0 + t) + 0.5 * g * (1.0 - t * t) * (GELU_C0 * (1.0 + 3.0 * GELU_C1 * g * g))


def _cspec(T, off=0, ct=CT):
    return pl.BlockSpec((1, T, ct), lambda j, b: (b, 0, j + off))


def _pspec(rows, off=0, ct=CT):
    return pl.BlockSpec((rows, ct), lambda j, b: (0, j + off))


def _conv_fwd_call(x3, x_off, C, w, b, name):
    Bl, T, _ = x3.shape
    W = w.shape[0]

    def body(x_ref, w_ref, b_ref, o_ref):
        o_ref[0, SLAB:T, :] = _conv_rows(x_ref.at[0], w_ref, b_ref, SLAB, T)
        o_ref[0, 0:SLAB, :] = _conv_head(x_ref[0, 0:SLAB, :], w_ref, b_ref)

    return pl.pallas_call(
        body, grid=(C // CT, Bl), in_specs=[_cspec(T, x_off // CT), _pspec(W), _pspec(1)], out_specs=_cspec(T),
        out_shape=S((Bl, T, C), F32), compiler_params=_cparams("parallel", "arbitrary"), name=name)(x3, w, b)


def _conv_bwd_call(dy3, x3, x_off, C, w, name):
    Bl, T, _ = x3.shape
    W = w.shape[0]

    def body(dy_ref, x_ref, w_ref, dx_ref, dw_ref, db_ref):
        row = lax.broadcasted_iota(jnp.int32, (T, CT), 0)
        dx, dw, db = _conv_bwd_taps(dy_ref[0], _taps(x_ref.at[0], W), w_ref, row)
        dx_ref[0] = dx.astype(BF16)

        @pl.when(pl.program_id(1) == 0)
        def _():
            dw_ref[...] = jnp.zeros_like(dw_ref)
            db_ref[...] = jnp.zeros_like(db_ref)

        dw_ref[...] += dw
        db_ref[...] += db

    return pl.pallas_call(
        body, grid=(C // CT, Bl), in_specs=[_cspec(T), _cspec(T, x_off // CT), _pspec(W)],
        out_specs=(_cspec(T), _pspec(W), _pspec(1)),
        out_shape=(S((Bl, T, C), BF16), S((W, C), F32), S((1, C), F32)),
        compiler_params=_cparams("parallel", "arbitrary"), name=name)(dy3, x3, w)


def _ffn_mid_fwd(u3, wc, bc, name):
    Bl, T, F2 = u3.shape
    F = F2 // 2
    nf = F // CT

    def body(ug_ref, uv_ref, wg_ref, wv_ref, bg_ref, bv_ref, o_ref):
        g = _conv_rows(ug_ref.at[0], wg_ref, bg_ref, SLAB, T)
        v = _conv_rows(uv_ref.at[0], wv_ref, bv_ref, SLAB, T)
        o_ref[0, SLAB:T, :] = (_gelu(g)[0] * v).astype(BF16)
        g = _conv_head(ug_ref[0, 0:SLAB, :], wg_ref, bg_ref)
        v = _conv_head(uv_ref[0, 0:SLAB, :], wv_ref, bv_ref)
        o_ref[0, 0:SLAB, :] = (_gelu(g)[0] * v).astype(BF16)

    return pl.pallas_call(
        body, grid=(nf, Bl),
        in_specs=[_cspec(T), _cspec(T, nf), _pspec(3), _pspec(3, nf), _pspec(1), _pspec(1, nf)], out_specs=_cspec(T),
        out_shape=S((Bl, T, F), BF16), compiler_params=_cparams("parallel", "arbitrary"), name=name)(
            u3, u3, wc, wc, bc, bc)


def _ffn_mid_bwd(u3, dact3, wc, bc, name):
    Bl, T, F2 = u3.shape
    F = F2 // 2
    nf = F // CT

    def body(ug_ref, uv_ref, da_ref, wg_ref, wv_ref, bg_ref, bv_ref, dug_ref, duv_ref, dwg_ref, dwv_ref, dbg_ref,
             dbv_ref):
        row = lax.broadcasted_iota(jnp.int32, (T, CT), 0)
        ugs = _taps(ug_ref.at[0], 3)
        uvs = _taps(uv_ref.at[0], 3)
        g = _conv_taps(ugs, wg_ref, bg_ref)
        v = _conv_taps(uvs, wv_ref, bv_ref)
        da = da_ref[0]
        gel, t = _gelu(g)
        dg = da * v * _dgelu(g, t)
        dv = da * gel
        dug, dwg, dbg = _conv_bwd_taps(dg, ugs, wg_ref, row)
        duv, dwv, dbv = _conv_bwd_taps(dv, uvs, wv_ref, row)
        dug_ref[0] = dug.astype(BF16)
        duv_ref[0] = duv.astype(BF16)

        @pl.when(pl.program_id(1) == 0)
        def _():
            dwg_ref[...] = jnp.zeros_like(dwg_ref)
            dwv_ref[...] = jnp.zeros_like(dwv_ref)
            dbg_ref[...] = jnp.zeros_like(dbg_ref)
            dbv_ref[...] = jnp.zeros_like(dbv_ref)

        dwg_ref[...] += dwg
        dwv_ref[...] += dwv
        dbg_ref[...] += dbg
        dbv_ref[...] += dbv

    return pl.pallas_call(
        body, grid=(nf, Bl),
        in_specs=[_cspec(T), _cspec(T, nf), _cspec(T), _pspec(3), _pspec(3, nf), _pspec(1), _pspec(1, nf)],
        out_specs=(_cspec(T), _cspec(T), _pspec(3), _pspec(3), _pspec(1), _pspec(1)),
        out_shape=(S((Bl, T, F), BF16), S((Bl, T, F), BF16), S((3, F), F32), S((3, F), F32), S((1, F), F32),
                   S((1, F), F32)),
        compiler_params=_cparams("parallel", "arbitrary"), name=name)(u3, u3, dact3, wc, wc, bc, bc)


def _lru_gates(xc, rp, ip, br_ref, bi_ref, lam_ref):
    r = jax.nn.sigmoid(rp + br_ref[...])
    i = jax.nn.sigmoid(ip + bi_ref[...])
    lam = lam_ref[...]
    sp = jnp.maximum(-lam, 0.0) + jnp.log1p(jnp.exp(-jnp.abs(lam)))
    log_a = (-LRU_C) * r * sp
    a = jnp.exp(log_a)
    z = 2.0 * log_a
    one_m_a2 = jnp.where(z > -0.05, -z * (1.0 + z * (0.5 + z * (1.0 / 6.0 + z * (1.0 / 24.0)))), 1.0 - a * a)
    mult = jnp.sqrt(one_m_a2)
    return r, i, sp, a, mult


SCAN_CHUNK = 64


def _scan_down(a, b):
    T = a.shape[0]
    ch = min(SCAN_CHUNK, T)
    row = lax.broadcasted_iota(jnp.int32, (ch, a.shape[1]), 0)
    outs, carry = [], None
    for c in range(T // ch):
        ac, bc = a[c * ch:(c + 1) * ch], b[c * ch:(c + 1) * ch]
        s = 1
        while s < ch:
            a_sh = jnp.where(row >= s, pltpu.roll(ac, s, axis=0), 1.0)
            bc = ac * _shift_down(bc, s, row) + bc
            ac = ac * a_sh
            s *= 2
        if carry is not None:
            bc = bc + ac * carry
        carry = bc[ch - 1:ch, :]
        outs.append(bc)
    return jnp.concatenate(outs, axis=0)


def _scan_up(an, g):
    T = an.shape[0]
    ch = min(SCAN_CHUNK, T)
    row = lax.broadcasted_iota(jnp.int32, (ch, an.shape[1]), 0)
    outs, carry = [], None
    for c in reversed(range(T // ch)):
        ac, gc = an[c * ch:(c + 1) * ch], g[c * ch:(c + 1) * ch]
        s = 1
        while s < ch:
            a_sh = jnp.where(row < ch - s, pltpu.roll(ac, ch - s, axis=0), 1.0)
            gc = ac * _shift_up(gc, s, row) + gc
            ac = ac * a_sh
            s *= 2
        if carry is not None:
            gc = gc + ac * carry
        carry = gc[0:1, :]
        outs.append(gc)
    return jnp.concatenate(outs[::-1], axis=0)


def _rglru_fwd(xc3, gates3, proj3, br, bi, lam, name):
    Bl, T, C = xc3.shape

    def body(xc_ref, rp_ref, ip_ref, ug_ref, br_ref, bi_ref, lam_ref, y_ref, h_ref):
        xc = xc_ref[0]
        r, i, sp, a, mult = _lru_gates(xc, rp_ref[0], ip_ref[0], br_ref, bi_ref, lam_ref)
        h = _scan_down(a, mult * (i * xc))
        h_ref[0] = h
        y_ref[0] = (h * _gelu(ug_ref[0])[0]).astype(BF16)

    return pl.pallas_call(
        body, grid=(C // CT, Bl),
        in_specs=[_cspec(T), _cspec(T), _cspec(T, C // CT), _cspec(T), _pspec(1), _pspec(1), _pspec(1)],
        out_specs=(_cspec(T), _cspec(T)), out_shape=(S((Bl, T, C), BF16), S((Bl, T, C), F32)),
        compiler_params=_cparams("parallel", "arbitrary"), name=name)(xc3, gates3, gates3, proj3, br, bi, lam)


def _rglru_bwd(dy3, xc3, gates3, proj3, h3, br, bi, lam, name):
    Bl, T, C = xc3.shape

    def body(dy_ref, xc_ref, rp_ref, ip_ref, ug_ref, h_ref, br_ref, bi_ref, lam_ref,
             dxc_ref, drp_ref, dip_ref, dug_ref, dbr_ref, dbi_ref, dlam_ref):
        row = lax.broadcasted_iota(jnp.int32, (T, CT), 0)
        xc = xc_ref[0]
        r, i, sp, a, mult = _lru_gates(xc, rp_ref[0], ip_ref[0], br_ref, bi_ref, lam_ref)
        h = h_ref[0]
        dy = dy_ref[0]
        ug = ug_ref[0]
        gel, t = _gelu(ug)
        dug_ref[0] = (dy * h * _dgelu(ug, t)).astype(BF16)
        gacc = _scan_up(_shift_up(a, 1, row), dy * gel)
        da = gacc * _shift_down(h, 1, row)
        ix = i * xc
        d_mult = gacc * ix
        d_i = gacc * mult * xc
        dxc_ref[0] = gacc * mult * i
        d_log_a = da * a - d_mult * (a * a) / mult
        d_r = d_log_a * ((-LRU_C) * sp)
        d_sp = jnp.sum(d_log_a * ((-LRU_C) * r), axis=0, keepdims=True)
        drp = d_r * r * (1.0 - r)
        dip = d_i * i * (1.0 - i)
        drp_ref[0] = drp.astype(BF16)
        dip_ref[0] = dip.astype(BF16)

        @pl.when(pl.program_id(1) == 0)
        def _():
            dbr_ref[...] = jnp.zeros_like(dbr_ref)
            dbi_ref[...] = jnp.zeros_like(dbi_ref)
            dlam_ref[...] = jnp.zeros_like(dlam_ref)

        dbr_ref[...] += jnp.sum(drp, axis=0, keepdims=True)
        dbi_ref[...] += jnp.sum(dip, axis=0, keepdims=True)
        dlam_ref[...] += d_sp * (-jax.nn.sigmoid(-lam_ref[...]))

    vec = S((1, C), F32)
    act = S((Bl, T, C), BF16)
    return pl.pallas_call(
        body, grid=(C // CT, Bl),
        in_specs=[_cspec(T), _cspec(T), _cspec(T), _cspec(T, C // CT), _cspec(T), _cspec(T)] + [_pspec(1)] * 3,
        out_specs=(_cspec(T), _cspec(T), _cspec(T), _cspec(T), _pspec(1), _pspec(1), _pspec(1)),
        out_shape=(S((Bl, T, C), F32), act, act, act, vec, vec, vec),
        compiler_params=_cparams("parallel", "arbitrary"), name=name)(dy3, xc3, gates3, gates3, proj3, h3, br, bi, lam)


NT = (((1,), (1,)), ((), ()))
TN = (((0,), (0,)), ((), ()))


def _hs(h):
    return slice(h * HEAD, (h + 1) * HEAD)


def _head_rows(x):
    head = lax.shift_right_logical(lax.broadcasted_iota(jnp.int32, x.shape, 1), HEAD.bit_length() - 1)
    return jnp.concatenate([jnp.where(head == h, x, jnp.zeros_like(x)) for h in range(MEM_HEADS)], axis=0)


def _head_sum(xbd):
    M = xbd.shape[0] // MEM_HEADS
    head = lax.shift_right_logical(lax.broadcasted_iota(jnp.int32, (M, xbd.shape[1]), 1), HEAD.bit_length() - 1)
    out = jnp.zeros((M, xbd.shape[1]), xbd.dtype)
    for h in range(MEM_HEADS):
        out = jnp.where(head == h, xbd[h * M:(h + 1) * M], out)
    return out


def _mem_probs(q, kbd):
    M = kbd.shape[0] // MEM_HEADS
    s = lax.dot_general(q, kbd, NT, preferred_element_type=F32) * (HEAD ** -0.5)
    ps = []
    for h in range(MEM_HEADS):
        sh = s[:, h * M:(h + 1) * M]
        e = jnp.exp(sh - jnp.max(sh, axis=-1, keepdims=True))
        ps.append(e / jnp.sum(e, axis=-1, keepdims=True))
    return ps


def _mem_attn_fwd(proj3, q_off, mkv3, name):
    Bl, T, _ = proj3.shape
    M = mkv3.shape[1]
    tq = _tile(T, (512, 256, 128))

    def body(q_ref, k_ref, v_ref, o_ref):
        q = q_ref[0].astype(BF16)
        kbd = _head_rows(k_ref[0].astype(BF16))
        vbd = _head_rows(v_ref[0].astype(BF16))
        p = jnp.concatenate(_mem_probs(q, kbd), axis=-1).astype(BF16)
        o_ref[0] = jnp.dot(p, vbd, preferred_element_type=F32).astype(BF16)

    return pl.pallas_call(
        body, grid=(Bl, T // tq),
        in_specs=[pl.BlockSpec((1, tq, MEM_W), lambda b, t: (b, t, q_off // MEM_W)),
                  pl.BlockSpec((1, M, MEM_W), lambda b, t: (b, 0, 0)),
                  pl.BlockSpec((1, M, MEM_W), lambda b, t: (b, 0, 1))],
        out_specs=pl.BlockSpec((1, tq, MEM_W), lambda b, t: (b, t, 0)),
        out_shape=S((Bl, T, MEM_W), BF16), compiler_params=_cparams("parallel", "parallel"), name=name)(
            proj3, mkv3, mkv3)


def _mem_attn_bwd(proj3, q_off, mkv3, do3, name):
    Bl, T, _ = proj3.shape
    M = mkv3.shape[1]
    tq = _tile(T, (512, 256, 128))
    scale = HEAD ** -0.5

    def body(q_ref, k_ref, v_ref, do_ref, dq_ref, dkv_ref):
        q = q_ref[0].astype(BF16)
        kbd = _head_rows(k_ref[0].astype(BF16))
        vbd = _head_rows(v_ref[0].astype(BF16))
        do = do_ref[0].astype(BF16)
        ps = _mem_probs(q, kbd)
        dvbd = lax.dot_general(jnp.concatenate(ps, axis=-1).astype(BF16), do, TN, preferred_element_type=F32)
        dp = lax.dot_general(do, vbd, NT, preferred_element_type=F32)
        dss = []
        for h in range(MEM_HEADS):
            dph = dp[:, h * M:(h + 1) * M]
            dss.append(ps[h] * (dph - jnp.sum(ps[h] * dph, axis=-1, keepdims=True)) * scale)
        ds = jnp.concatenate(dss, axis=-1).astype(BF16)
        dq_ref[0] = jnp.dot(ds, kbd, preferred_element_type=F32).astype(BF16)
        dkbd = lax.dot_general(ds, q, TN, preferred_element_type=F32)

        @pl.when(pl.program_id(1) == 0)
        def _():
            dkv_ref[...] = jnp.zeros_like(dkv_ref)

        dkv_ref[0] += jnp.concatenate([_head_sum(dkbd), _head_sum(dvbd)], axis=-1)

    return pl.pallas_call(
        body, grid=(Bl, T // tq),
        in_specs=[pl.BlockSpec((1, tq, MEM_W), lambda b, t: (b, t, q_off // MEM_W)),
                  pl.BlockSpec((1, M, MEM_W), lambda b, t: (b, 0, 0)),
                  pl.BlockSpec((1, M, MEM_W), lambda b, t: (b, 0, 1)),
                  pl.BlockSpec((1, tq, MEM_W), lambda b, t: (b, t, 0))],
        out_specs=(pl.BlockSpec((1, tq, MEM_W), lambda b, t: (b, t, 0)),
                   pl.BlockSpec((1, M, 2 * MEM_W), lambda b, t: (b, 0, 0))),
        out_shape=(S((Bl, T, MEM_W), BF16), S((Bl, M, 2 * MEM_W), F32)),
        compiler_params=_cparams("parallel", "arbitrary"), name=name)(proj3, mkv3, mkv3, do3)


GROUP_ROWS = SWA_GROUP * WIN


def _group_rows(x, kvh):
    return jnp.concatenate([x[:, _hs(SWA_GROUP * kvh + g)] for g in range(SWA_GROUP)], axis=0)


def _group_col(vals):
    grp = lax.shift_right_logical(lax.broadcasted_iota(jnp.int32, (GROUP_ROWS, 1), 0), WIN.bit_length() - 1)
    col = jnp.full((GROUP_ROWS, 1), vals[-1], F32)
    for g in range(SWA_GROUP - 2, -1, -1):
        col = jnp.where(grp == g, vals[g], col)
    return col


def _swa_probs(qh, kph, kch, sink, slope, has_prev):
    qi = jnp.bitwise_and(lax.broadcasted_iota(jnp.int32, (GROUP_ROWS, WIN), 0), WIN - 1)
    kj = lax.broadcasted_iota(jnp.int32, (GROUP_ROWS, WIN), 1)
    scale = HEAD ** -0.5
    sp = lax.dot_general(qh, kph, NT, preferred_element_type=F32) * scale
    sc = lax.dot_general(qh, kch, NT, preferred_element_type=F32) * scale
    dist_p = (qi + WIN - kj).astype(F32)
    dist_c = (qi - kj).astype(F32)
    neg = -jnp.inf
    sp = jnp.where(kj > qi + jnp.where(has_prev, 0, WIN), sp - slope * dist_p, neg)
    sc = jnp.where(kj <= qi, sc - slope * dist_c, neg)
    m = jnp.maximum(jnp.maximum(jnp.max(sp, axis=-1, keepdims=True), jnp.max(sc, axis=-1, keepdims=True)), sink)
    ep = jnp.exp(sp - m)
    ec = jnp.exp(sc - m)
    es = jnp.exp(sink - m)
    inv = 1.0 / (jnp.sum(ep, axis=-1, keepdims=True) + jnp.sum(ec, axis=-1, keepdims=True) + es)
    return ep * inv, ec * inv, es * inv


def _swa_specs(nb):
    prev = lambda n: jnp.maximum(n - 1, 0)
    q = pl.BlockSpec((1, WIN, MIX_W), lambda b, n: (b, n, 0))
    kp = pl.BlockSpec((1, WIN, MEM_W), lambda b, n: (b, prev(n), 0))
    kc = pl.BlockSpec((1, WIN, MEM_W), lambda b, n: (b, n, 0))
    vp = pl.BlockSpec((1, WIN, MEM_W), lambda b, n: (b, prev(n), 1))
    vc = pl.BlockSpec((1, WIN, MEM_W), lambda b, n: (b, n, 1))
    sm = pl.BlockSpec(memory_space=pltpu.SMEM)
    return q, kp, kc, vp, vc, sm


def _swa_fwd(proj3, kv3, sinks, name):
    Bl, T, _ = proj3.shape
    nb = T // WIN
    q_s, kp_s, kc_s, vp_s, vc_s, sm = _swa_specs(nb)

    def body(q_ref, kp_ref, kc_ref, vp_ref, vc_ref, sink_ref, o_ref):
        has_prev = pl.program_id(1) > 0
        q = q_ref[0].astype(BF16)
        kp, kc = kp_ref[0].astype(BF16), kc_ref[0].astype(BF16)
        vp, vc = vp_ref[0].astype(BF16), vc_ref[0].astype(BF16)
        outs = []
        for kvh in range(SWA_HEADS // SWA_GROUP):
            kvs = _hs(kvh)
            heads = range(SWA_GROUP * kvh, SWA_GROUP * (kvh + 1))
            pp, pc, _ = _swa_probs(_group_rows(q, kvh), kp[:, kvs], kc[:, kvs], _group_col([sink_ref[h] for h in heads]),
                                   _group_col([SLOPES[h] for h in heads]), has_prev)
            og = (jnp.dot(pp.astype(BF16), vp[:, kvs], preferred_element_type=F32)
                  + jnp.dot(pc.astype(BF16), vc[:, kvs], preferred_element_type=F32))
            outs += [og[g * WIN:(g + 1) * WIN] for g in range(SWA_GROUP)]
        o_ref[0] = jnp.concatenate(outs, axis=-1).astype(BF16)

    return pl.pallas_call(
        body, grid=(Bl, nb), in_specs=[q_s, kp_s, kc_s, vp_s, vc_s, sm], out_specs=q_s,
        out_shape=S((Bl, T, MIX_W), BF16), compiler_params=_cparams("parallel", "parallel"), name=name)(
            proj3, kv3, kv3, kv3, kv3, sinks)


def _swa_bwd(proj3, kv3, sinks, do3, name):
    Bl, T, _ = proj3.shape
    nb = T // WIN
    q_s, kp_s, kc_s, vp_s, vc_s, sm = _swa_specs(nb)
    kv_s = pl.BlockSpec((1, WIN, 2 * MEM_W), lambda b, n: (b, n, 0))
    sk_s = pl.BlockSpec((8, LANES), lambda b, n: (0, 0))
    scale = HEAD ** -0.5

    def body(q_ref, kp_ref, kc_ref, vp_ref, vc_ref, sink_ref, do_ref, dq_ref, dkc_ref, dkp_ref, dsk_ref):
        has_prev = pl.program_id(1) > 0
        q = q_ref[0].astype(BF16)
        kp, kc = kp_ref[0].astype(BF16), kc_ref[0].astype(BF16)
        vp, vc = vp_ref[0].astype(BF16), vc_ref[0].astype(BF16)
        do = do_ref[0].astype(BF16)
        lane = lax.broadcasted_iota(jnp.int32, (8, LANES), 1)
        srow = lax.broadcasted_iota(jnp.int32, (8, LANES), 0)
        dsk = jnp.zeros((8, LANES), F32)
        dqs = []
        dkc, dkp, dvc, dvp = [], [], [], []
        grp = lax.shift_right_logical(lax.broadcasted_iota(jnp.int32, (GROUP_ROWS, 1), 0), WIN.bit_length() - 1)
        for kvh in range(SWA_HEADS // SWA_GROUP):
            kvs = _hs(kvh)
            heads = range(SWA_GROUP * kvh, SWA_GROUP * (kvh + 1))
            qg, dog = _group_rows(q, kvh), _group_rows(do, kvh)
            pp, pc, ps = _swa_probs(qg, kp[:, kvs], kc[:, kvs], _group_col([sink_ref[h] for h in heads]),
                                    _group_col([SLOPES[h] for h in heads]), has_prev)
            dpp = lax.dot_general(dog, vp[:, kvs], NT, preferred_element_type=F32)
            dpc = lax.dot_general(dog, vc[:, kvs], NT, preferred_element_type=F32)
            delta = jnp.sum(pp * dpp, axis=-1, keepdims=True) + jnp.sum(pc * dpc, axis=-1, keepdims=True)
            dsp = (pp * (dpp - delta) * scale).astype(BF16)
            dsc = (pc * (dpc - delta) * scale).astype(BF16)
            dqg = (jnp.dot(dsp, kp[:, kvs], preferred_element_type=F32)
                   + jnp.dot(dsc, kc[:, kvs], preferred_element_type=F32))
            dqs += [dqg[g * WIN:(g + 1) * WIN] for g in range(SWA_GROUP)]
            dkc.append(lax.dot_general(dsc, qg, TN, preferred_element_type=F32))
            dkp.append(lax.dot_general(dsp, qg, TN, preferred_element_type=F32))
            dvc.append(lax.dot_general(pc.astype(BF16), dog, TN, preferred_element_type=F32))
            dvp.append(lax.dot_general(pp.astype(BF16), dog, TN, preferred_element_type=F32))
            dsink = ps * delta
            for g, h in enumerate(heads):
                dsk = dsk + jnp.where((lane == h) & (srow == 0), -jnp.sum(jnp.where(grp == g, dsink, 0.0)), 0.0)
        dq_ref[0] = jnp.concatenate(dqs, axis=-1).astype(BF16)
        dkc_ref[0] = jnp.concatenate(dkc + dvc, axis=-1)
        dkp_ref[0] = jnp.concatenate(dkp + dvp, axis=-1)

        @pl.when((pl.program_id(0) == 0) & (pl.program_id(1) == 0))
        def _():
            dsk_ref[...] = jnp.zeros_like(dsk_ref)

        dsk_ref[...] += dsk

    return pl.pallas_call(
        body, grid=(Bl, nb), in_specs=[q_s, kp_s, kc_s, vp_s, vc_s, sm, q_s], out_specs=(q_s, kv_s, kv_s, sk_s),
        out_shape=(S((Bl, T, MIX_W), BF16), S((Bl, T, 2 * MEM_W), F32), S((Bl, T, 2 * MEM_W), F32), S((8, LANES), F32)),
        compiler_params=_cparams("arbitrary", "arbitrary"), name=name)(proj3, kv3, kv3, kv3, kv3, sinks, do3)


def _kv_grad_combine(parts, name):
    Bl, T, W = parts[0][0].shape
    nb = T // WIN
    nl = len(parts)

    def body(*refs):
        o_ref = refs[-1]
        has_next = jnp.where(pl.program_id(1) == nb - 1, 0.0, 1.0)
        acc = None
        for l in range(nl):
            c = refs[2 * l][0] + has_next * refs[2 * l + 1][0]
            acc = c if acc is None else acc + c
        o_ref[0] = acc.astype(BF16)

    cur = pl.BlockSpec((1, WIN, W), lambda b, n: (b, n, 0))
    nxt = pl.BlockSpec((1, WIN, W), lambda b, n: (b, jnp.minimum(n + 1, nb - 1), 0))
    return pl.pallas_call(
        body, grid=(Bl, nb), in_specs=[cur, nxt] * nl, out_specs=cur, out_shape=S((Bl, T, W), BF16),
        compiler_params=_cparams("parallel", "parallel"), name=name)(*[a for pr in parts for a in pr])


def _loss_bwd(y, target, name="loss"):
    N, D = y.shape
    tm = _tile(N, (512, 256, 128))

    def body(y_ref, t_ref, dy_ref, l_ref):
        e = y_ref[...] - t_ref[...]
        dy_ref[...] = e * (1.0 / D)

        @pl.when(pl.program_id(0) == 0)
        def _():
            l_ref[...] = jnp.zeros_like(l_ref)

        l_ref[...] += jnp.sum(e * e, axis=0, keepdims=True) * (0.5 / D)

    row = pl.BlockSpec((tm, D), lambda i: (i, 0))
    vec = pl.BlockSpec((1, D), lambda i: (0, 0))
    return pl.pallas_call(
        body, grid=(N // tm,), in_specs=[row, row], out_specs=(row, vec), out_shape=(S((N, D), F32), S((1, D), F32)),
        compiler_params=_cparams("arbitrary"), name=name)(y, target)


def _all_gather(x, name):
    R, C = x.shape

    def body(x_ref, out_ref, send_sems, recv_sems, local_sem):
        mx, my, mc = lax.axis_index("x"), lax.axis_index("y"), lax.axis_index("c")
        me, sibling = (mx, my, mc), (mx, my, 1 - mc)
        chips = [(1 - mx, my), (mx, 1 - my), (1 - mx, 1 - my)]

        def rows(px, py, pc):
            return out_ref.at[4 * px + 2 * py + pc]

        def copy(kk, block, to, src=None):
            return pltpu.make_async_remote_copy(
                src_ref=rows(*block) if src is None else src, dst_ref=rows(*block), send_sem=send_sems.at[kk],
                recv_sem=recv_sems.at[kk], device_id=to, device_id_type=MESH)

        mine = pltpu.make_async_copy(x_ref, rows(*me), local_sem)
        mine.start()
        first = [copy(0, me, sibling, src=x_ref)]
        first += [copy(1 + j, me, (*chip, mc), src=x_ref) for j, chip in enumerate(chips)]
        for cp in first:
            cp.start()
        passed = [copy(4 + j, (*chip, mc), sibling) for j, chip in enumerate(chips)]
        for j, chip in enumerate(chips):
            copy(1 + j, (*chip, mc), me).wait_recv()
            passed[j].start()
        copy(0, sibling, me).wait_recv()
        for j, chip in enumerate(chips):
            copy(4 + j, (*chip, 1 - mc), me).wait_recv()
        for cp in first + passed:
            cp.wait_send()
        mine.wait()

    return pl.pallas_call(
        body, out_shape=S((N_DEV, R, C), x.dtype), in_specs=[ANY], out_specs=ANY,
        scratch_shapes=[pltpu.SemaphoreType.DMA((7,)), pltpu.SemaphoreType.DMA((7,)), pltpu.SemaphoreType.DMA(())],
        name=name)(x)


def _ag_weights(shards, row_sharded, name):
    n = len(shards)

    def full_shape(a, rows):
        if rows:
            return a.shape[:-2] + (N_DEV * a.shape[-2],) + a.shape[-1:]
        return (N_DEV,) + a.shape

    def body(*refs):
        x_refs, o_refs = refs[:n], refs[n:2 * n]
        send_sems, recv_sems, local_sems = refs[2 * n:]
        mx, my, mc = lax.axis_index("x"), lax.axis_index("y"), lax.axis_index("c")
        me, sibling = (mx, my, mc), (mx, my, 1 - mc)
        chips = [(1 - mx, my), (mx, 1 - my), (1 - mx, 1 - my)]

        def dst(t, px, py, pc):
            d = 4 * px + 2 * py + pc
            if not row_sharded[t]:
                return o_refs[t].at[d]
            r = shards[t].shape[-2]
            idx = (slice(None),) * (shards[t].ndim - 2) + (pl.ds(pl.multiple_of(d * r, 16), r), slice(None))
            return o_refs[t].at[idx]

        def copy(kk, t, block, to, src=None):
            return pltpu.make_async_remote_copy(
                src_ref=dst(t, *block) if src is None else src, dst_ref=dst(t, *block),
                send_sem=send_sems.at[kk * n + t], recv_sem=recv_sems.at[kk * n + t], device_id=to,
                device_id_type=MESH)

        mine = [pltpu.make_async_copy(x_refs[t], dst(t, *me), local_sems.at[t]) for t in range(n)]
        for cp in mine:
            cp.start()
        first = []
        for t in range(n):
            first.append(copy(0, t, me, sibling, src=x_refs[t]))
            first += [copy(1 + j, t, me, (*chip, mc), src=x_refs[t]) for j, chip in enumerate(chips)]
        for cp in first:
            cp.start()
        passed = []
        for j, chip in enumerate(chips):
            for t in range(n):
                copy(1 + j, t, (*chip, mc), me).wait_recv()
                cp = copy(4 + j, t, (*chip, mc), sibling)
                cp.start()
                passed.append(cp)
        for t in range(n):
            copy(0, t, sibling, me).wait_recv()
            for j, chip in enumerate(chips):
                copy(4 + j, t, (*chip, 1 - mc), me).wait_recv()
        for cp in first + passed:
            cp.wait_send()
        for cp in mine:
            cp.wait()

    return pl.pallas_call(
        body, out_shape=tuple(S(full_shape(a, r), a.dtype) for a, r in zip(shards, row_sharded)),
        in_specs=[ANY] * n, out_specs=tuple([ANY] * n),
        scratch_shapes=[pltpu.SemaphoreType.DMA((7 * n,)), pltpu.SemaphoreType.DMA((7 * n,)),
                        pltpu.SemaphoreType.DMA((n,))],
        name=name)(*shards)


FLIPS = [(fx, fy, fc) for fx in (0, 1) for fy in (0, 1) for fc in (0, 1)][1:]
HBM = pl.BlockSpec(memory_space=pltpu.HBM)
SEM = pl.BlockSpec(memory_space=pltpu.SEMAPHORE)
EFFECT = pltpu.SideEffectType.DATAFLOW_SIDE_EFFECTING


def _hbm(a):
    return pltpu.with_memory_space_constraint(a, pltpu.HBM)


def _flips(gather):
    return [(0, 0, 0)] + FLIPS if gather else FLIPS


def _split_copies(gather, s_refs, l_refs, send_sems, recv_sems):
    n = len(s_refs)
    mx, my, mc = lax.axis_index("x"), lax.axis_index("y"), lax.axis_index("c")
    me = 4 * mx + 2 * my + mc
    copies = []
    for k, (fx, fy, fc) in enumerate(_flips(gather)):
        px, py, pc = (1 - mx if fx else mx), (1 - my if fy else my), (1 - mc if fc else mc)
        for t in range(n):
            if gather:
                src = s_refs[t]
                r = src.shape[0]
                dst = l_refs[t].at[pl.ds(pl.multiple_of(me * r, 16), r), :]
            else:
                src = s_refs[t].at[:, 4 * px + 2 * py + pc]
                dst = l_refs[t].at[k]
            copies.append(pltpu.make_async_remote_copy(
                src_ref=src, dst_ref=dst, send_sem=send_sems.at[k * n + t], recv_sem=recv_sems.at[k * n + t],
                device_id=(px, py, pc), device_id_type=MESH))
    return copies


def _split_start(gather, srcs, lands, after, name):
    n = len(srcs)
    n_sem = len(_flips(gather)) * n

    def body(*refs):
        s_refs, l_refs = refs[:n], refs[n:2 * n]
        send_sems, recv_sems = refs[2 * n + 1], refs[2 * n + 2]
        token = refs[-1]
        for cp in _split_copies(gather, s_refs, l_refs, send_sems, recv_sems):
            cp.start()
        token[...] = jnp.zeros_like(token)

    outs = pl.pallas_call(
        body, name=name,
        out_shape=(pltpu.SemaphoreType.DMA((n_sem,)), pltpu.SemaphoreType.DMA((n_sem,)))
        + tuple(pltpu.HBM(a.shape, a.dtype) for a in lands) + (S((8, LANES), F32),),
        in_specs=[HBM] * (2 * n) + [ANY],
        out_specs=(SEM, SEM) + (HBM,) * n + (pl.BlockSpec(memory_space=pltpu.VMEM),),
        input_output_aliases={n + i: 2 + i for i in range(n)},
        compiler_params=pltpu.CompilerParams(has_side_effects=EFFECT),
    )(*[_hbm(a) for a in srcs], *[_hbm(a) for a in lands], after)
    return outs[0], outs[1], list(srcs), list(outs[2:2 + n]), outs[-1]


def _split_wait(gather, send_sems, recv_sems, srcs, lands, after, name):
    n = len(srcs)

    def body(*refs):
        s_refs, l_refs = refs[:n], refs[n:2 * n]
        ssem, rsem = refs[2 * n], refs[2 * n + 1]
        copies = _split_copies(gather, s_refs, l_refs, ssem, rsem)
        for cp in copies:
            cp.wait_send()
        for cp in copies:
            cp.wait_recv()

    outs = pl.pallas_call(
        body, name=name, out_shape=tuple(pltpu.HBM(a.shape, a.dtype) for a in lands),
        in_specs=[HBM] * (2 * n) + [SEM, SEM, ANY], out_specs=(HBM,) * n,
        input_output_aliases={n + i: i for i in range(n)},
        compiler_params=pltpu.CompilerParams(has_side_effects=EFFECT),
    )(*[_hbm(a) for a in srcs], *lands, send_sems, recv_sems, after)
    return list(outs)


def _adamw_math(w, g, m, v):
    m = ADAM_B1 * m + (1.0 - ADAM_B1) * g
    v = ADAM_B2 * v + (1.0 - ADAM_B2) * (g * g)
    m_hat = m / (1.0 - ADAM_B1 ** ADAM_STEP)
    v_hat = v / (1.0 - ADAM_B2 ** ADAM_STEP)
    delta = -ADAM_LR * (m_hat / (jnp.sqrt(v_hat) + ADAM_EPS) + ADAM_WD * w)
    return delta, m, v


def _adamw_layers(owns, gots, w, m, v, name):
    L, B, C = w.shape
    per_row = 2 * L * len(FLIPS) * C * owns[0].dtype.itemsize
    tb = max([t for t in range(16, B + 1, 16) if B % t == 0 and (t * per_row <= 24 * 1024 * 1024 or t == 16)] or [B])
    me = (4 * lax.axis_index("x") + 2 * lax.axis_index("y") + lax.axis_index("c")).astype(jnp.int32).reshape(1)

    def body(me_ref, *refs):
        own_refs, got_refs = refs[:L], refs[L:2 * L]
        w_ref, m_ref, v_ref = refs[2 * L:2 * L + 3]
        g_out, d_out, m_out, v_out = refs[2 * L + 3:]
        layer = pl.program_id(0)
        for kk in range(L):
            @pl.when(layer == kk)
            def _():
                g = own_refs[kk][0].astype(F32)
                for s in range(len(FLIPS)):
                    g = g + got_refs[kk][s].astype(F32)
                d, mn, vn = _adamw_math(w_ref[...], g, m_ref[...], v_ref[...])
                g_out[...] = g
                d_out[...] = d
                m_out[...] = mn
                v_out[...] = vn

    def row(kk, layer, i):
        return jnp.where(layer == kk, i, 0)

    blk = pl.BlockSpec((1, tb, C), lambda layer, i, me_ref: (layer, i, 0))
    own_specs = [pl.BlockSpec((1, 1, tb, C), lambda layer, i, me_ref, kk=kk: (0, me_ref[0], row(kk, layer, i), 0))
                 for kk in range(L)]
    got_specs = [pl.BlockSpec((len(FLIPS), 1, tb, C), lambda layer, i, me_ref, kk=kk: (0, 0, row(kk, layer, i), 0))
                 for kk in range(L)]
    return pl.pallas_call(
        body,
        grid_spec=pltpu.PrefetchScalarGridSpec(
            num_scalar_prefetch=1, grid=(L, B // tb), in_specs=own_specs + got_specs + [blk, blk, blk],
            out_specs=(blk, blk, blk, blk)),
        out_shape=(S((L, B, C), F32),) * 4, compiler_params=_cparams("arbitrary", "arbitrary"), name=name)(
            me, *owns, *gots, w, m, v)


def _adamw_replicated(parts, w, m, v, name):
    R, C = w.shape
    rb = _tile(R, (512, 256, 128, 64, 32, 16))

    def body(p_ref, w_ref, m_ref, v_ref, g_out, d_out, m_out, v_out):
        g = p_ref[0].astype(F32)
        for j in range(1, N_DEV):
            g = g + p_ref[j].astype(F32)
        d, mn, vn = _adamw_math(w_ref[...], g, m_ref[...], v_ref[...])
        g_out[...] = g
        d_out[...] = d
        m_out[...] = mn
        v_out[...] = vn

    blk = pl.BlockSpec((rb, C), lambda i: (i, 0))
    return pl.pallas_call(
        body, grid=(R // rb,), in_specs=[pl.BlockSpec((N_DEV, rb, C), lambda i: (0, i, 0)), blk, blk, blk],
        out_specs=(blk, blk, blk, blk), out_shape=(S((R, C), F32),) * 4, compiler_params=_cparams("parallel"),
        name=name)(parts, w, m, v)


def _pack(arrs, rows_mult, dtype):
    flat = jnp.concatenate([a.reshape(-1).astype(dtype) for a in arrs])
    n = flat.shape[0]
    per = rows_mult * LANES
    tot = -(-n // per) * per
    return jnp.pad(flat, (0, tot - n)).reshape(tot // LANES, LANES)


def _unpack(blob, shapes):
    flat = blob.reshape(-1)
    out, off = [], 0
    for shp in shapes:
        n = int(np.prod(shp))
        out.append(flat[off:off + n].reshape(shp))
        off += n
    return out


def _small_to_natural(g8):
    t = jnp.moveaxis(g8, 0, -2)
    return t.reshape(t.shape[:-2] + (N_DEV * t.shape[-1],))


def _small_to_cols(g):
    t = g.reshape(g.shape[:-1] + (N_DEV, g.shape[-1] // N_DEV))
    return jnp.moveaxis(t, -2, 0)


def _block_diag(w):
    nb, bs, _ = w.shape
    eye = jnp.eye(nb, dtype=w.dtype)
    return (eye[:, None, :, None] * w[:, :, None, :]).reshape(nb * bs, nb * bs)


def _diag_blocks(d, nb, bs):
    d4 = d.reshape(nb, bs, nb, bs)
    return jnp.stack([d4[i, :, i, :] for i in range(nb)])


def kernel(x, mem, g_mix_pre, g_mix_post, g_ffn_pre, g_ffn_post, g_mem, w_mem_kv, w_mix_out, w_ffn_up, w_ffn_conv, b_ffn_conv, w_ffn_down, w_in_a, w_conv_a, b_conv_a, w_rg_r, b_rg_r, w_rg_i, b_rg_i, lru_lambda, w_in_b, sinks_b, g_kv, w_kv, loss_target, m_g_mix_pre, m_g_mix_post, m_g_ffn_pre, m_g_ffn_post, m_g_mem, m_w_mem_kv, m_w_mix_out, m_w_ffn_up, m_w_ffn_conv, m_b_ffn_conv, m_w_ffn_down, m_w_in_a, m_w_conv_a, m_b_conv_a, m_w_rg_r, m_b_rg_r, m_w_rg_i, m_b_rg_i, m_lru_lambda, m_w_in_b, m_sinks_b, m_g_kv, m_w_kv, v_g_mix_pre, v_g_mix_post, v_g_ffn_pre, v_g_ffn_post, v_g_mem, v_w_mem_kv, v_w_mix_out, v_w_ffn_up, v_w_ffn_conv, v_b_ffn_conv, v_w_ffn_down, v_w_in_a, v_w_conv_a, v_b_conv_a, v_w_rg_r, v_b_rg_r, v_w_rg_i, v_b_rg_i, v_lru_lambda, v_w_in_b, v_sinks_b, v_g_kv, v_w_kv):
    w_loc = dict(g_mix_pre=g_mix_pre, g_mix_post=g_mix_post, g_ffn_pre=g_ffn_pre, g_ffn_post=g_ffn_post, g_mem=g_mem,
                 w_mem_kv=w_mem_kv, w_mix_out=w_mix_out, w_ffn_up=w_ffn_up, w_ffn_conv=w_ffn_conv,
                 b_ffn_conv=b_ffn_conv, w_ffn_down=w_ffn_down, w_in_a=w_in_a, w_conv_a=w_conv_a, b_conv_a=b_conv_a,
                 w_rg_r=w_rg_r, b_rg_r=b_rg_r, w_rg_i=w_rg_i, b_rg_i=b_rg_i, lru_lambda=lru_lambda, w_in_b=w_in_b,
                 sinks_b=sinks_b, g_kv=g_kv, w_kv=w_kv)
    m_loc = dict(g_mix_pre=m_g_mix_pre, g_mix_post=m_g_mix_post, g_ffn_pre=m_g_ffn_pre, g_ffn_post=m_g_ffn_post,
                 g_mem=m_g_mem, w_mem_kv=m_w_mem_kv, w_mix_out=m_w_mix_out, w_ffn_up=m_w_ffn_up,
                 w_ffn_conv=m_w_ffn_conv, b_ffn_conv=m_b_ffn_conv, w_ffn_down=m_w_ffn_down, w_in_a=m_w_in_a,
                 w_conv_a=m_w_conv_a, b_conv_a=m_b_conv_a, w_rg_r=m_w_rg_r, b_rg_r=m_b_rg_r, w_rg_i=m_w_rg_i,
                 b_rg_i=m_b_rg_i, lru_lambda=m_lru_lambda, w_in_b=m_w_in_b, sinks_b=m_sinks_b, g_kv=m_g_kv,
                 w_kv=m_w_kv)
    v_loc = dict(g_mix_pre=v_g_mix_pre, g_mix_post=v_g_mix_post, g_ffn_pre=v_g_ffn_pre, g_ffn_post=v_g_ffn_post,
                 g_mem=v_g_mem, w_mem_kv=v_w_mem_kv, w_mix_out=v_w_mix_out, w_ffn_up=v_w_ffn_up,
                 w_ffn_conv=v_w_ffn_conv, b_ffn_conv=v_b_ffn_conv, w_ffn_down=v_w_ffn_down, w_in_a=v_w_in_a,
                 w_conv_a=v_w_conv_a, b_conv_a=v_b_conv_a, w_rg_r=v_w_rg_r, b_rg_r=v_b_rg_r, w_rg_i=v_w_rg_i,
                 b_rg_i=v_b_rg_i, lru_lambda=v_lru_lambda, w_in_b=v_w_in_b, sinks_b=v_sinks_b, g_kv=v_g_kv,
                 w_kv=v_w_kv)

    Bl, T, D = x.shape
    Ml = mem.shape[1]
    N = Bl * T
    depth = g_mix_pre.shape[0]
    n_a = w_in_a.shape[0]
    F = w_ffn_down.shape[1] * N_DEV
    def as_rows(n, a):
        return jnp.swapaxes(a, -1, -2) if n in TRANSPOSED else a

    def mix_keys(l):
        keys = [("w_mem_kv", l), ("w_mix_out", l), ("w_in_a", l) if l < n_a else ("w_in_b", l - n_a)]
        return keys + ([("w_kv", None)] if l == n_a else [])

    def ffn_keys(l):
        return [("w_ffn_up", l), ("w_ffn_down", l)]

    def shard_of(key):
        n, i = key
        return as_rows(n, w_loc[n] if i is None else w_loc[n][i]).astype(BF16)

    W = {}
    keys0, keys0_rest = mix_keys(0)[2:], mix_keys(0)[:2]
    got0 = _ag_weights([shard_of(kk) for kk in keys0] + [w_loc[n] for n in SMALL_SHARDED],
                       [True] * len(keys0) + [False] * len(SMALL_SHARDED), name="ag_weights_0")
    W.update(zip(keys0, got0))
    for n, a in zip(SMALL_SHARDED, got0[len(keys0):]):
        W[n] = _small_to_natural(a)

    def gather_start(keys, after, tag):
        shards = [shard_of(kk) for kk in keys]
        lands = [lax.empty((N_DEV * s.shape[0],) + s.shape[1:], s.dtype) for s in shards]
        return (keys, tag) + _split_start(True, shards, lands, after, name=f"ag_start_{tag}")

    def gather_wait(pending, after):
        keys, tag, ssem, rsem, srcs, lands, _ = pending
        W.update(zip(keys, _split_wait(True, ssem, rsem, srcs, lands, after, name=f"ag_wait_{tag}")))

    pending_rest = gather_start(keys0_rest, got0[0], "mix_0")
    pending_ffn = gather_start(ffn_keys(0), pending_rest[-1], "ffn_0")

    nblk, bsz = w_rg_r.shape[1], w_rg_r.shape[2]
    wbd = [jnp.concatenate([_block_diag(w_rg_r[j]), _block_diag(w_rg_i[j])], axis=1).astype(BF16) for j in range(n_a)]

    def vec(a):
        return a.reshape(1, -1)

    x2 = x.reshape(N, D)
    mem2 = mem.reshape(Bl * Ml, D)
    saved = []
    kvn = kv3 = x_kv = None
    xs = x2
    h1 = _rms_fwd(xs, vec(g_mix_pre[0]), BF16, name="rms_mixpre_0")
    for l in range(depth):
        sv = {"x0": xs}
        tok = None
        if l + 1 < depth:
            pending = gather_start(mix_keys(l + 1), pending_ffn[-1] if l == 0 else W[("w_mem_kv", l)], f"mix_{l + 1}")
            pending_next_ffn = gather_start(ffn_keys(l + 1), pending[-1], f"ffn_{l + 1}")
            tok = pending_next_ffn[-1]
        memn = _rms_fwd(mem2, vec(g_mem[l]), BF16, name=f"rms_mem_{l}")
        if l < n_a:
            j = l
            proj = _mm(h1, W[("w_in_a", j)], tb=True, after=tok, name=f"mm_in_{l}")
            proj3 = proj.reshape(Bl, T, -1)
            xc3 = _conv_fwd_call(proj3, MIX_W, MIX_W, W["w_conv_a"][j], vec(W["b_conv_a"][j]), name=f"conv_a_{l}")
            gates3 = _mm(xc3.reshape(N, MIX_W), wbd[j], name=f"mm_gates_{l}").reshape(Bl, T, 2 * MIX_W)
            y_main3, hs3 = _rglru_fwd(xc3, gates3, proj3, vec(b_rg_r[j]), vec(b_rg_i[j]), vec(W["lru_lambda"][j]),
                                      name=f"rglru_fwd_{l}")
            q_off = 2 * MIX_W
            sv.update(xc3=xc3, gates3=gates3, hs3=hs3)
        else:
            j = l - n_a
            if l == n_a:
                x_kv = xs
                kv3 = _mm(kvn, W[("w_kv", None)], name="mm_kv").reshape(Bl, T, 2 * MEM_W)
            proj = _mm(h1, W[("w_in_b", j)], after=tok, name=f"mm_in_{l}")
            proj3 = proj.reshape(Bl, T, -1)
            y_main3 = _swa_fwd(proj3, kv3, sinks_b[j], name=f"swa_fwd_{l}")
            q_off = MIX_W
        if l == 0:
            gather_wait(pending_rest, y_main3)
        mkv3 = _mm(memn, W[("w_mem_kv", l)], name=f"mm_memkv_{l}").reshape(Bl, Ml, 2 * MEM_W)
        y_mem3 = _mem_attn_fwd(proj3, q_off, mkv3, name=f"memattn_fwd_{l}")
        y_main = y_main3.reshape(N, MIX_W)
        y_mem = y_mem3.reshape(N, MEM_W)
        y = _mm_sum([(y_main, W[("w_mix_out", l)], (0, 0)), (y_mem, W[("w_mix_out", l)], (MIX_W, 0))], n=D,
                    name=f"mm_mixout_{l}")
        x1, h2 = _rms_pair_fwd(y, vec(g_mix_post[l]), xs, [vec(g_ffn_pre[l])], name=f"rms_mixpost_ffnpre_{l}")
        gather_wait(pending_ffn, h2)
        if l + 1 < depth:
            pending_ffn = pending_next_ffn
        u3 = _mm(h2, W[("w_ffn_up", l)], tb=True, name=f"mm_up_{l}").reshape(Bl, T, 2 * F)
        act3 = _ffn_mid_fwd(u3, W["w_ffn_conv"][l], vec(b_ffn_conv[l]), name=f"ffn_mid_fwd_{l}")
        act = act3.reshape(N, F)
        f = _mm(act, W[("w_ffn_down", l)], name=f"mm_down_{l}")
        sv.update(h1=h1, memn=memn, mkv3=mkv3, proj3=proj3, q_off=q_off, y_main=y_main, y_mem=y_mem, y=y, x1=x1,
                  h2=h2, u3=u3, act=act, f=f)
        saved.append(sv)
        if l + 1 < depth:
            gains = [vec(g_mix_pre[l + 1])] + ([vec(g_kv)] if l + 1 == n_a else [])
            xs, h1, *rest = _rms_pair_fwd(f, vec(g_ffn_post[l]), x1, gains, name=f"rms_ffnpost_mixpre_{l}")
            if rest:
                kvn = rest[0]
            gather_wait(pending, xs)
        else:
            xs = _rms_fwd(f, vec(g_ffn_post[l]), F32, res=x1, name=f"rms_ffnpost_{l}")

    dxs, loss_vec = _loss_bwd(xs, loss_target.reshape(N, D))
    loss = lax.psum(jnp.sum(loss_vec), ("x", "y", "c"))

    G = {n: [None] * w_loc[n].shape[0] for n in REPL + SMALL_SHARDED if n != "g_kv"}
    GW = {}

    def dw(key, off, a, b_, nm):
        GW[key] = _mm(a, b_, ta=True, out_dtype=BF16, into=(GW.get(key), (1,) + W[key].shape, 0, off), name=nm)

    def grad_blocks(key):
        g = GW[key]
        return g.reshape(1, N_DEV, g.shape[1] // N_DEV, g.shape[2])

    reduces = []

    def reduce_start(keys, after, tag):
        srcs = [grad_blocks(kk) for kk in keys]
        lands = [lax.empty((len(FLIPS),) + s.shape[:1] + s.shape[2:], s.dtype) for s in srcs]
        started = _split_start(False, srcs, lands, after, name=f"rs_start_{tag}")
        reduces.append((keys, tag) + started)
        return started[-1]

    kv_parts = []
    df = None
    for l in reversed(range(depth)):
        sv = saved[l]
        proj3 = sv["proj3"]
        if df is None:
            df, dg = _rms_bwd(sv["f"], vec(g_ffn_post[l]), dxs, out_dtype=BF16, name=f"rmsb_ffnpost_{l}")
            G["g_ffn_post"][l] = dg[0]
        dact = _mm(df, W[("w_ffn_down", l)], tb=True, name=f"mmb_down_dx_{l}")
        dw(("w_ffn_down", l), (0, 0), sv["act"], df, f"mmb_down_dw_{l}")
        dug3, duv3, dwg, dwv, dbg, dbv = _ffn_mid_bwd(sv["u3"], dact.reshape(Bl, T, F),
                                                      W["w_ffn_conv"][l], vec(b_ffn_conv[l]), name=f"ffn_mid_bwd_{l}")
        G["w_ffn_conv"][l] = jnp.concatenate([dwg, dwv], axis=1)
        G["b_ffn_conv"][l] = jnp.concatenate([dbg, dbv], axis=1)[0]
        dug, duv = dug3.reshape(N, F), duv3.reshape(N, F)
        dw(("w_ffn_up", l), (0, 0), dug, sv["h2"], f"mmb_up_dw_g_{l}")
        dw(("w_ffn_up", l), (F, 0), duv, sv["h2"], f"mmb_up_dw_v_{l}")
        tok = reduce_start([("w_ffn_down", l), ("w_ffn_up", l)], dug, f"ffn_{l}")
        dh2 = _mm_sum([(dug, W[("w_ffn_up", l)], (0, 0)), (duv, W[("w_ffn_up", l)], (F, 0))], n=D, after=tok,
                      name=f"mmb_up_dx_{l}")
        dx1, dy, dg, dg2 = _rms_pair_bwd(sv["x1"], vec(g_ffn_pre[l]), dh2, dxs, sv["y"], vec(g_mix_post[l]),
                                         name=f"rmsb_ffnpre_mixpost_{l}")
        G["g_ffn_pre"][l] = dg[0]
        G["g_mix_post"][l] = dg2[0]
        dy_main = _mm(dy, W[("w_mix_out", l)], tb=True, n=MIX_W, k=D, name=f"mmb_mixout_dmain_{l}")
        dy_mem = _mm(dy, W[("w_mix_out", l)], tb=True, n=MEM_W, k=D, b_off=(MIX_W, 0),
                     name=f"mmb_mixout_dmem_{l}")
        dw(("w_mix_out", l), (0, 0), sv["y_main"], dy, f"mmb_mixout_dw_main_{l}")
        dw(("w_mix_out", l), (MIX_W, 0), sv["y_mem"], dy, f"mmb_mixout_dw_mem_{l}")
        dq_mem3, dmkv3 = _mem_attn_bwd(proj3, sv["q_off"], sv["mkv3"], dy_mem.reshape(Bl, T, MEM_W),
                                       name=f"memattn_bwd_{l}")
        dq_mem = dq_mem3.reshape(N, MEM_W)
        dmkv = dmkv3.reshape(Bl * Ml, 2 * MEM_W)
        dw(("w_mem_kv", l), (0, 0), sv["memn"], dmkv, f"mmb_memkv_dw_{l}")
        dmemn = _mm(dmkv, W[("w_mem_kv", l)], tb=True, name=f"mmb_memkv_dx_{l}")
        _, dg = _rms_bwd(mem2, vec(g_mem[l]), dmemn, name=f"rmsb_mem_{l}")
        G["g_mem"][l] = dg[0]
        dy_main3 = dy_main.reshape(Bl, T, MIX_W)
        if l < n_a:
            j = l
            dxc3, drp3, dip3, dugate3, dbr, dbi, dlam = _rglru_bwd(
                dy_main3, sv["xc3"], sv["gates3"], proj3, sv["hs3"], vec(b_rg_r[j]), vec(b_rg_i[j]),
                vec(W["lru_lambda"][j]), name=f"rglru_bwd_{l}")
            G["b_rg_r"][j] = dbr.reshape(nblk, bsz)
            G["b_rg_i"][j] = dbi.reshape(nblk, bsz)
            G["lru_lambda"][j] = dlam[0]
            drp, dip = drp3.reshape(N, MIX_W), dip3.reshape(N, MIX_W)
            xc2 = sv["xc3"].reshape(N, MIX_W)
            G["w_rg_r"][j] = _diag_blocks(_mm(xc2, drp, ta=True, name=f"mmb_gates_dw_r_{l}"), nblk, bsz)
            G["w_rg_i"][j] = _diag_blocks(_mm(xc2, dip, ta=True, name=f"mmb_gates_dw_i_{l}"), nblk, bsz)
            dxc = _mm_sum([(drp, wbd[j], (0, 0)), (dip, wbd[j], (0, MIX_W))], tb=True, n=MIX_W,
                          add=dxc3.reshape(N, MIX_W), name=f"mmb_gates_dx_{l}")
            dux3, dwc, dbc = _conv_bwd_call(dxc.reshape(Bl, T, MIX_W), proj3, MIX_W, MIX_W, W["w_conv_a"][j],
                                            name=f"conv_a_bwd_{l}")
            G["w_conv_a"][j] = dwc
            G["b_conv_a"][j] = dbc[0]
            pieces = [(dugate3.reshape(N, MIX_W), 0), (dux3.reshape(N, MIX_W), MIX_W), (dq_mem, 2 * MIX_W)]
            in_key = ("w_in_a", j)
        else:
            j = l - n_a
            dq3, dkc, dkp, dsk = _swa_bwd(proj3, kv3, sinks_b[j], dy_main3, name=f"swa_bwd_{l}")
            kv_parts.append((dkc, dkp))
            G["sinks_b"][j] = dsk[0, :SWA_HEADS]
            pieces = [(dq3.reshape(N, MIX_W), 0), (dq_mem, MIX_W)]
            in_key = ("w_in_b", j)
        in_t = in_key[0] in TRANSPOSED
        for pi, (piece, off) in enumerate(pieces):
            if in_t:
                dw(in_key, (off, 0), piece, sv["h1"], f"mmb_in_dw_{pi}_{l}")
            else:
                dw(in_key, (0, off), sv["h1"], piece, f"mmb_in_dw_{pi}_{l}")
        tok = reduce_start([("w_mix_out", l), ("w_mem_kv", l), in_key], dy, f"mix_{l}")
        dh1 = _mm_sum([(piece, W[in_key], (off, 0) if in_t else (0, off)) for piece, off in pieces], tb=not in_t, n=D,
                      after=tok, name=f"mmb_in_dx_{l}")
        if l > 0 and l != n_a:
            dxs, df, dg, dg2 = _rms_pair_bwd(sv["x0"], vec(g_mix_pre[l]), dh1, dx1, saved[l - 1]["f"],
                                             vec(g_ffn_post[l - 1]), name=f"rmsb_mixpre_ffnpost_{l}")
            G["g_ffn_post"][l - 1] = dg2[0]
        else:
            dxs, dg = _rms_bwd(sv["x0"], vec(g_mix_pre[l]), dh1, add=dx1, name=f"rmsb_mixpre_{l}")
            df = None
        G["g_mix_pre"][l] = dg[0]
        if l == n_a:
            dkv = _kv_grad_combine(kv_parts, name="kv_grad_combine").reshape(N, 2 * MEM_W)
            dw(("w_kv", None), (0, 0), kvn, dkv, "mmb_kv_dw")
            tok = reduce_start([("w_kv", None)], dkv, "kv")
            dkvn = _mm(dkv, W[("w_kv", None)], tb=True, after=tok, name="mmb_kv_dx")
            dxs, dg = _rms_bwd(x_kv, vec(g_kv), dkvn, add=dxs, name="rmsb_kv")
            G["g_kv"] = dg[0]
    grad_x = dxs.reshape(Bl, T, D)
    Gf = {n: (jnp.stack(g) if isinstance(g, list) else g) for n, g in G.items()}

    small4 = []
    for n in SMALL_SHARDED:
        t = _small_to_cols(Gf[n]).astype(BF16)
        small4.append(t.reshape(1, N_DEV, -1, t.shape[-1]))
    small_lands = [lax.empty((len(FLIPS),) + s.shape[:1] + s.shape[2:], s.dtype) for s in small4]
    small_started = _split_start(False, small4, small_lands, dxs, name="rs_start_small")
    r_blob = _pack([Gf[n] for n in REPL], REPL_ROWS, BF16)
    r_parts = _all_gather(r_blob, name="ag_repl_grads")
    parts = {}
    for keys, tag, ssem, rsem, srcs, lands, _ in reduces:
        for kk, s, g7 in zip(keys, srcs, _split_wait(False, ssem, rsem, srcs, lands, small_started[-1],
                                                     name=f"rs_wait_{tag}")):
            parts[kk] = (s, g7)

    res = [{} for _ in range(4)]
    for n, _ in SHARDED:
        if n in SMALL_SHARDED:
            continue
        idx = [None] if w_loc[n].ndim == 2 else list(range(w_loc[n].shape[0]))
        wmv = [as_rows(n, a[n]) for a in (w_loc, m_loc, v_loc)]
        shp3 = (len(idx),) + wmv[0].shape[-2:]
        outs = _adamw_layers([parts[(n, i)][0] for i in idx], [parts[(n, i)][1] for i in idx],
                             *[a.reshape(shp3) for a in wmv], name=f"adamw_{n}")
        for k in range(4):
            res[k][n] = as_rows(n, outs[k].reshape(wmv[0].shape))
    last = res[0]["w_kv"]
    small_got = _split_wait(False, *small_started[:4], last, name="rs_wait_small")
    for n, own, g7 in zip(SMALL_SHARDED, small4, small_got):
        shp3 = own.shape[:1] + own.shape[2:]
        outs = _adamw_layers([own], [g7], w_loc[n].reshape(shp3), m_loc[n].reshape(shp3), v_loc[n].reshape(shp3),
                             name=f"adamw_{n}")
        for k in range(4):
            res[k][n] = outs[k].reshape(w_loc[n].shape)
    outs_rp = _adamw_replicated(r_parts, _pack([w_loc[n] for n in REPL], REPL_ROWS, F32),
                                _pack([m_loc[n] for n in REPL], REPL_ROWS, F32),
                                _pack([v_loc[n] for n in REPL], REPL_ROWS, F32),
                                name="adamw_replicated")
    rp_shapes = [w_loc[n].shape for n in REPL]
    for k in range(4):
        res[k].update(zip(REPL, _unpack(outs_rp[k], rp_shapes)))
    out = [loss, grad_x]
    for k in range(4):
        out += [res[k][n] for n in WEIGHTS]
    return tuple(out)
```

```python
import math

import numpy as np
import jax
import jax.numpy as jnp
from jax import lax
from jax.experimental import pallas as pl
from jax.experimental.pallas import tpu as pltpu

F32 = jnp.float32
BF16 = jnp.bfloat16
S = jax.ShapeDtypeStruct
MESH = pl.DeviceIdType.MESH
ANY = pl.BlockSpec(memory_space=pl.ANY)

HEAD = 64
MEM_HEADS = 4
MEM_W = MEM_HEADS * HEAD
SWA_HEADS = 12
SWA_GROUP = 3
MIX_W = SWA_HEADS * HEAD
WIN = 128
LRU_C = 8.0
EPS = 1e-6
ADAM_LR, ADAM_B1, ADAM_B2, ADAM_EPS, ADAM_WD, ADAM_STEP = 0.001, 0.9, 0.999, 1e-08, 0.01, 10
GELU_C0 = math.sqrt(2.0 / math.pi)
GELU_C1 = 0.044715
N_DEV = 8
LANES = 128
CT = 128
VMEM_LIMIT = 48 * 1024 * 1024
MM_VMEM_BUDGET = 36 * 1024 * 1024
REPL_ROWS = 256

SHARDED = (("w_mem_kv", 1), ("w_mix_out", 1), ("w_ffn_up", 2), ("w_ffn_conv", 2), ("w_ffn_down", 1), ("w_in_a", 2),
           ("w_conv_a", 2), ("b_conv_a", 1), ("lru_lambda", 1), ("w_in_b", 1), ("w_kv", 0))
SMALL_SHARDED = ("w_ffn_conv", "w_conv_a", "b_conv_a", "lru_lambda")
TRANSPOSED = ("w_ffn_up", "w_in_a")
REPL = ("g_mix_pre", "g_mix_post", "g_ffn_pre", "g_ffn_post", "g_mem", "b_ffn_conv", "w_rg_r", "b_rg_r", "w_rg_i",
        "b_rg_i", "sinks_b", "g_kv")
WEIGHTS = ("g_mix_pre", "g_mix_post", "g_ffn_pre", "g_ffn_post", "g_mem", "w_mem_kv", "w_mix_out", "w_ffn_up",
           "w_ffn_conv", "b_ffn_conv", "w_ffn_down", "w_in_a", "w_conv_a", "b_conv_a", "w_rg_r", "b_rg_r", "w_rg_i",
           "b_rg_i", "lru_lambda", "w_in_b", "sinks_b", "g_kv", "w_kv")


def _alibi_slopes(n):
    def pow2(m):
        start = 2.0 ** (-8.0 / m)
        return [start ** (i + 1) for i in range(m)]
    c = 2 ** int(math.floor(math.log2(n)))
    s = pow2(c)
    if c != n:
        s = s + pow2(2 * c)[0::2][: n - c]
    return [float(v) for v in np.asarray(s, dtype=np.float32)]


SLOPES = _alibi_slopes(SWA_HEADS)


def _tile(n, cands):
    for c in cands:
        if n % c == 0:
            return c
    return n


def _cparams(*sem):
    return pltpu.CompilerParams(dimension_semantics=sem, vmem_limit_bytes=VMEM_LIMIT)


def _mm_tiles(M, N, K, a_bytes, b_bytes, o_bytes, add_bytes, offsets):
    m_off, n_offs, k_off = offsets
    tms = [c for c in (1024, 512, 256, 128) if M % c == 0 and m_off % c == 0] or [M]
    tns = [c for c in (1408, 1024, 896, 768, 512, 384, 256, 128)
           if N % c == 0 and all(o % c == 0 for o in n_offs)] or [N]
    tks = [c for c in (K, 2048, 1408, 1024, 512, 256, 128) if c <= K and K % c == 0 and k_off % c == 0]
    best = None
    for tk in tks:
        fits = []
        for tm in tms:
            for tn in tns:
                need = 2 * (tm * tk * a_bytes + tk * tn * b_bytes + tm * tn * (o_bytes + add_bytes))
                need += tm * tn * 4 * (2 if tk < K else 1)
                need += (tm * tk * 2 if a_bytes != 2 else 0) + (tk * tn * 2 if b_bytes != 2 else 0)
                if need <= MM_VMEM_BUDGET:
                    fits.append((tm * tn, min(tm, 512), tm, tn))
        if fits:
            _, _, tm, tn = max(fits)
            best = (tm, tn, tk)
            break
    assert best is not None, (M, N, K)
    return best


def _mm(a, b, *, ta=False, tb=False, n=None, k=None, b_off=(0, 0), out_dtype=F32, add=None, into=None, after=None,
        name="mm"):
    if ta:
        K, M = a.shape
    else:
        M, K = a.shape
    if tb:
        N = b.shape[-2] if n is None else n
    else:
        N = b.shape[-1] if n is None else n
    assert k is None or k == K
    ro, co = b_off
    n_off, k_off = (ro, co) if tb else (co, ro)
    oro, oco = (0, 0) if into is None else into[3]
    tm, tn, tk = _mm_tiles(M, N, K, a.dtype.itemsize, b.dtype.itemsize, jnp.dtype(out_dtype).itemsize,
                           0 if add is None else add.dtype.itemsize, (oro, (n_off, oco), k_off))
    nk = K // tk
    if tb:
        b_spec = pl.BlockSpec((tn, tk), lambda i, j, kk: (j + ro // tn, kk + co // tk))
        b_dims = (1,)
    else:
        b_spec = pl.BlockSpec((tk, tn), lambda i, j, kk: (kk + ro // tk, j + co // tn))
        b_dims = (0,)
    if ta:
        a_spec = pl.BlockSpec((tk, tm), lambda i, j, kk: (kk, i))
        a_dims = (0,)
    else:
        a_spec = pl.BlockSpec((tm, tk), lambda i, j, kk: (i, kk))
        a_dims = (1,)
    dims = ((a_dims, b_dims), ((), ()))
    add_spec = pl.BlockSpec((tm, tn), lambda i, j, kk: (i, j))
    has_add = add is not None
    if into is None:
        o_spec, o_shape, buf = add_spec, (M, N), None
    else:
        buf, o_shape, ol, _ = into
        assert not has_add
        o_spec = pl.BlockSpec((None, tm, tn), lambda i, j, kk: (ol, i + oro // tm, j + oco // tn))
    has_buf = buf is not None

    def body(*refs):
        refs = list(refs)
        acc_ref = refs.pop() if nk > 1 else None
        o_ref = refs.pop()
        a_ref, b_ref = refs[0], refs[1]
        add_ref = refs[2] if has_add else None
        part = lax.dot_general(a_ref[...].astype(BF16), b_ref[...].astype(BF16), dims, preferred_element_type=F32)

        def finish(r):
            if has_add:
                r = r + add_ref[...].astype(F32)
            o_ref[...] = r.astype(out_dtype)

        if nk == 1:
            finish(part)
        else:
            kk = pl.program_id(2)

            @pl.when(kk == 0)
            def _():
                acc_ref[...] = part

            @pl.when(kk > 0)
            def _():
                acc_ref[...] += part

            @pl.when(kk == nk - 1)
            def _():
                finish(acc_ref[...])

    in_specs = [a_spec, b_spec] + ([add_spec] if has_add else []) + ([ANY] if has_buf else [])
    args = (a, b) + ((add,) if has_add else ()) + ((buf,) if has_buf else ())
    if after is not None:
        in_specs, args = in_specs + [ANY], args + (after,)
    return pl.pallas_call(
        body, grid=(M // tm, N // tn, nk), in_specs=in_specs, out_specs=o_spec,
        out_shape=S(o_shape, out_dtype), scratch_shapes=[pltpu.VMEM((tm, tn), F32)] if nk > 1 else [],
        input_output_aliases={2: 0} if has_buf else {},
        compiler_params=_cparams("parallel", "parallel", "arbitrary"), name=name)(*args)


def _mm_sum(pieces, *, tb=False, n, out_dtype=F32, add=None, after=None, name="mm_sum"):
    M = pieces[0][0].shape[0]
    ks = [a.shape[1] for a, _, _ in pieces]
    a_bytes = max(a.dtype.itemsize for a, _, _ in pieces)
    b_bytes = max(b.dtype.itemsize for _, b, _ in pieces)
    n_offs = tuple(off[0] if tb else off[1] for _, _, off in pieces)
    for kp, (_, _, off) in zip(ks, pieces):
        assert (off[1] if tb else off[0]) % kp == 0
    tm, tn, tk = _mm_tiles(M, n, sum(ks), a_bytes, b_bytes, jnp.dtype(out_dtype).itemsize, 0, (0, n_offs, 0))
    assert tk == sum(ks)
    a_specs = [pl.BlockSpec((tm, kp), lambda i, j: (i, 0)) for kp in ks]
    if tb:
        b_specs = [pl.BlockSpec((tn, kp), lambda i, j, ro=off[0], co=off[1], kp=kp: (j + ro // tn, co // kp))
                   for kp, (_, _, off) in zip(ks, pieces)]
        dims = NT
    else:
        b_specs = [pl.BlockSpec((kp, tn), lambda i, j, ro=off[0], co=off[1], kp=kp: (ro // kp, j + co // tn))
                   for kp, (_, _, off) in zip(ks, pieces)]
        dims = (((1,), (0,)), ((), ()))
    npc = len(pieces)
    o_spec = pl.BlockSpec((tm, tn), lambda i, j: (i, j))

    def body(*refs):
        o_ref = refs[2 * npc + (add is not None) + (after is not None)]
        acc = refs[2 * npc][...].astype(F32) if add is not None else None
        for p in range(npc):
            part = lax.dot_general(refs[p][...].astype(BF16), refs[npc + p][...].astype(BF16), dims,
                                   preferred_element_type=F32)
            acc = part if acc is None else acc + part
        o_ref[...] = acc.astype(out_dtype)

    args = [a for a, _, _ in pieces] + [b for _, b, _ in pieces]
    in_specs = a_specs + b_specs
    if add is not None:
        in_specs, args = in_specs + [o_spec], args + [add]
    if after is not None:
        in_specs, args = in_specs + [ANY], args + [after]
    return pl.pallas_call(
        body, grid=(M // tm, n // tn), in_specs=in_specs, out_specs=o_spec,
        out_shape=S((M, n), out_dtype), compiler_params=_cparams("parallel", "parallel"), name=name)(*args)


def _rms_fwd(x, g, out_dtype, res=None, name="rms_fwd"):
    N, D = x.shape
    tm = _tile(N, (512, 256, 128))
    has_res = res is not None

    def body(*refs):
        if has_res:
            x_ref, g_ref, r_ref, o_ref = refs
        else:
            x_ref, g_ref, o_ref = refs
        xv = x_ref[...].astype(F32)
        y = xv * lax.rsqrt(jnp.mean(xv * xv, axis=-1, keepdims=True) + EPS) * g_ref[...]
        if has_res:
            y = y + r_ref[...]
        o_ref[...] = y.astype(out_dtype)

    row = pl.BlockSpec((tm, D), lambda i: (i, 0))
    vec = pl.BlockSpec((1, D), lambda i: (0, 0))
    return pl.pallas_call(
        body, grid=(N // tm,), in_specs=[row, vec] + ([row] if has_res else []), out_specs=row,
        out_shape=S((N, D), out_dtype), compiler_params=_cparams("parallel"), name=name)(
            *((x, g) + ((res,) if has_res else ())))


def _rms_bwd(x, g, dy, add=None, out_dtype=F32, name="rms_bwd"):
    N, D = x.shape
    tm = _tile(N, (512, 256, 128))
    has_add = add is not None

    def body(*refs):
        if has_add:
            x_ref, g_ref, dy_ref, add_ref, dx_ref, dg_ref = refs
        else:
            x_ref, g_ref, dy_ref, dx_ref, dg_ref = refs
        xv = x_ref[...].astype(F32)
        dyv = dy_ref[...].astype(F32)
        r = lax.rsqrt(jnp.mean(xv * xv, axis=-1, keepdims=True) + EPS)
        u = dyv * g_ref[...]
        dx = r * u - xv * (r * r * r * jnp.mean(u * xv, axis=-1, keepdims=True))
        if has_add:
            dx = dx + add_ref[...]
        dx_ref[...] = dx.astype(out_dtype)

        @pl.when(pl.program_id(0) == 0)
        def _():
            dg_ref[...] = jnp.zeros_like(dg_ref)

        dg_ref[...] += jnp.sum(dyv * xv * r, axis=0, keepdims=True)

    row = pl.BlockSpec((tm, D), lambda i: (i, 0))
    vec = pl.BlockSpec((1, D), lambda i: (0, 0))
    return pl.pallas_call(
        body, grid=(N // tm,), in_specs=[row, vec, row] + ([row] if has_add else []), out_specs=(row, vec),
        out_shape=(S((N, D), out_dtype), S((1, D), F32)), compiler_params=_cparams("arbitrary"), name=name)(
            *((x, g, dy) + ((add,) if has_add else ())))


def _rms_pair_fwd(y, g_post, res, gains, name):
    N, D = y.shape
    tm = _tile(N, (512, 256, 128))
    ng = len(gains)

    def body(*refs):
        y_ref, gp_ref, r_ref = refs[:3]
        g_refs = refs[3:3 + ng]
        x_ref = refs[3 + ng]
        h_refs = refs[4 + ng:]
        yv = y_ref[...]
        x = r_ref[...] + yv * lax.rsqrt(jnp.mean(yv * yv, axis=-1, keepdims=True) + EPS) * gp_ref[...]
        x_ref[...] = x
        xn = x * lax.rsqrt(jnp.mean(x * x, axis=-1, keepdims=True) + EPS)
        for g_ref, h_ref in zip(g_refs, h_refs):
            h_ref[...] = (xn * g_ref[...]).astype(BF16)

    row = pl.BlockSpec((tm, D), lambda i: (i, 0))
    vec = pl.BlockSpec((1, D), lambda i: (0, 0))
    return pl.pallas_call(
        body, grid=(N // tm,), in_specs=[row, vec, row] + [vec] * ng, out_specs=(row,) * (1 + ng),
        out_shape=(S((N, D), F32),) + (S((N, D), BF16),) * ng, compiler_params=_cparams("parallel"), name=name)(
            y, g_post, res, *gains)


def _rms_pair_bwd(xa, ga, dya, add, xb, gb, name):
    N, D = xa.shape
    tm = _tile(N, (512, 256, 128))

    def one(xv, g_ref, dyv):
        r = lax.rsqrt(jnp.mean(xv * xv, axis=-1, keepdims=True) + EPS)
        u = dyv * g_ref[...]
        dx = r * u - xv * (r * r * r * jnp.mean(u * xv, axis=-1, keepdims=True))
        return dx, jnp.sum(dyv * xv * r, axis=0, keepdims=True)

    def body(xa_ref, ga_ref, dya_ref, add_ref, xb_ref, gb_ref, da_ref, db_ref, dga_ref, dgb_ref):
        da, dga = one(xa_ref[...].astype(F32), ga_ref, dya_ref[...].astype(F32))
        da = da + add_ref[...]
        da_ref[...] = da
        db, dgb = one(xb_ref[...].astype(F32), gb_ref, da)
        db_ref[...] = db.astype(BF16)

        @pl.when(pl.program_id(0) == 0)
        def _():
            dga_ref[...] = jnp.zeros_like(dga_ref)
            dgb_ref[...] = jnp.zeros_like(dgb_ref)

        dga_ref[...] += dga
        dgb_ref[...] += dgb

    row = pl.BlockSpec((tm, D), lambda i: (i, 0))
    vec = pl.BlockSpec((1, D), lambda i: (0, 0))
    return pl.pallas_call(
        body, grid=(N // tm,), in_specs=[row, vec, row, row, row, vec], out_specs=(row, row, vec, vec),
        out_shape=(S((N, D), F32), S((N, D), BF16), S((1, D), F32), S((1, D), F32)),
        compiler_params=_cparams("arbitrary"), name=name)(xa, ga, dya, add, xb, gb)


def _shift_down(x, s, row):
    return jnp.where(row >= s, pltpu.roll(x, s, axis=0), 0.0)


def _shift_up(x, s, row):
    T = x.shape[0]
    return jnp.where(row < T - s, pltpu.roll(x, T - s, axis=0), 0.0)


SLAB = 16


def _conv_rows(x_ref, w_ref, b_ref, lo, hi):
    W = w_ref.shape[0]
    y = x_ref[lo:hi, :] * w_ref[W - 1:W, :] + b_ref[...]
    for s in range(1, W):
        y = y + x_ref[lo - s:hi - s, :] * w_ref[W - 1 - s:W - s, :]
    return y


def _taps(x_ref, W):
    T = x_ref.shape[0]
    head = x_ref[0:SLAB, :]
    row = lax.broadcasted_iota(jnp.int32, head.shape, 0)
    return [x_ref[...]] + [jnp.concatenate([_shift_down(head, s, row), x_ref[SLAB - s:T - s, :]], axis=0)
                           for s in range(1, W)]


def _conv_taps(xs, w_ref, b_ref):
    W = w_ref.shape[0]
    y = xs[0] * w_ref[W - 1:W, :] + b_ref[...]
    for s in range(1, W):
        y = y + xs[s] * w_ref[W - 1 - s:W - s, :]
    return y


def _conv_head(x_head, w_ref, b_ref):
    row = lax.broadcasted_iota(jnp.int32, x_head.shape, 0)
    return _conv_taps([x_head] + [_shift_down(x_head, s, row) for s in range(1, w_ref.shape[0])], w_ref, b_ref)


def _conv_bwd_taps(dy, xs, w_ref, row):
    W = w_ref.shape[0]
    dx = dy * w_ref[W - 1:W, :]
    dws = [None] * W
    dws[W - 1] = jnp.sum(dy * xs[0], axis=0, keepdims=True)
    for s in range(1, W):
        dx = dx + _shift_up(dy, s, row) * w_ref[W - 1 - s:W - s, :]
        dws[W - 1 - s] = jnp.sum(dy * xs[s], axis=0, keepdims=True)
    return dx, jnp.concatenate(dws, axis=0), jnp.sum(dy, axis=0, keepdims=True)


def _gelu(g):
    t = jnp.tanh(GELU_C0 * (g + GELU_C1 * g * g * g))
    return 0.5 * g * (1.0 + t), t


def _dgelu(g, t):
    return 0.5 * (1.0 + t) + 0.5 * g * (1.0 - t * t) * (GELU_C0 * (1.0 + 3.0 * GELU_C1 * g * g))


def _cspec(T, off=0, ct=CT):
    return pl.BlockSpec((1, T, ct), lambda j, b: (b, 0, j + off))


def _pspec(rows, off=0, ct=CT):
    return pl.BlockSpec((rows, ct), lambda j, b: (0, j + off))


def _conv_fwd_call(x3, x_off, C, w, b, name):
    Bl, T, _ = x3.shape
    W = w.shape[0]

    def body(x_ref, w_ref, b_ref, o_ref):
        o_ref[0, SLAB:T, :] = _conv_rows(x_ref.at[0], w_ref, b_ref, SLAB, T)
        o_ref[0, 0:SLAB, :] = _conv_head(x_ref[0, 0:SLAB, :], w_ref, b_ref)

    return pl.pallas_call(
        body, grid=(C // CT, Bl), in_specs=[_cspec(T, x_off // CT), _pspec(W), _pspec(1)], out_specs=_cspec(T),
        out_shape=S((Bl, T, C), F32), compiler_params=_cparams("parallel", "arbitrary"), name=name)(x3, w, b)


def _conv_bwd_call(dy3, x3, x_off, C, w, name):
    Bl, T, _ = x3.shape
    W = w.shape[0]

    def body(dy_ref, x_ref, w_ref, dx_ref, dw_ref, db_ref):
        row = lax.broadcasted_iota(jnp.int32, (T, CT), 0)
        dx, dw, db = _conv_bwd_taps(dy_ref[0], _taps(x_ref.at[0], W), w_ref, row)
        dx_ref[0] = dx.astype(BF16)

        @pl.when(pl.program_id(1) == 0)
        def _():
            dw_ref[...] = jnp.zeros_like(dw_ref)
            db_ref[...] = jnp.zeros_like(db_ref)

        dw_ref[...] += dw
        db_ref[...] += db

    return pl.pallas_call(
        body, grid=(C // CT, Bl), in_specs=[_cspec(T), _cspec(T, x_off // CT), _pspec(W)],
        out_specs=(_cspec(T), _pspec(W), _pspec(1)),
        out_shape=(S((Bl, T, C), BF16), S((W, C), F32), S((1, C), F32)),
        compiler_params=_cparams("parallel", "arbitrary"), name=name)(dy3, x3, w)


def _ffn_mid_fwd(u3, wc, bc, name):
    Bl, T, F2 = u3.shape
    F = F2 // 2
    nf = F // CT

    def body(ug_ref, uv_ref, wg_ref, wv_ref, bg_ref, bv_ref, o_ref):
        g = _conv_rows(ug_ref.at[0], wg_ref, bg_ref, SLAB, T)
        v = _conv_rows(uv_ref.at[0], wv_ref, bv_ref, SLAB, T)
        o_ref[0, SLAB:T, :] = (_gelu(g)[0] * v).astype(BF16)
        g = _conv_head(ug_ref[0, 0:SLAB, :], wg_ref, bg_ref)
        v = _conv_head(uv_ref[0, 0:SLAB, :], wv_ref, bv_ref)
        o_ref[0, 0:SLAB, :] = (_gelu(g)[0] * v).astype(BF16)

    return pl.pallas_call(
        body, grid=(nf, Bl),
        in_specs=[_cspec(T), _cspec(T, nf), _pspec(3), _pspec(3, nf), _pspec(1), _pspec(1, nf)], out_specs=_cspec(T),
        out_shape=S((Bl, T, F), BF16), compiler_params=_cparams("parallel", "arbitrary"), name=name)(
            u3, u3, wc, wc, bc, bc)


def _ffn_mid_bwd(u3, dact3, wc, bc, name):
    Bl, T, F2 = u3.shape
    F = F2 // 2
    nf = F // CT

    def body(ug_ref, uv_ref, da_ref, wg_ref, wv_ref, bg_ref, bv_ref, dug_ref, duv_ref, dwg_ref, dwv_ref, dbg_ref,
             dbv_ref):
        row = lax.broadcasted_iota(jnp.int32, (T, CT), 0)
        ugs = _taps(ug_ref.at[0], 3)
        uvs = _taps(uv_ref.at[0], 3)
        g = _conv_taps(ugs, wg_ref, bg_ref)
        v = _conv_taps(uvs, wv_ref, bv_ref)
        da = da_ref[0]
        gel, t = _gelu(g)
        dg = da * v * _dgelu(g, t)
        dv = da * gel
        dug, dwg, dbg = _conv_bwd_taps(dg, ugs, wg_ref, row)
        duv, dwv, dbv = _conv_bwd_taps(dv, uvs, wv_ref, row)
        dug_ref[0] = dug.astype(BF16)
        duv_ref[0] = duv.astype(BF16)

        @pl.when(pl.program_id(1) == 0)
        def _():
            dwg_ref[...] = jnp.zeros_like(dwg_ref)
            dwv_ref[...] = jnp.zeros_like(dwv_ref)
            dbg_ref[...] = jnp.zeros_like(dbg_ref)
            dbv_ref[...] = jnp.zeros_like(dbv_ref)

        dwg_ref[...] += dwg
        dwv_ref[...] += dwv
        dbg_ref[...] += dbg
        dbv_ref[...] += dbv

    return pl.pallas_call(
        body, grid=(nf, Bl),
        in_specs=[_cspec(T), _cspec(T, nf), _cspec(T), _pspec(3), _pspec(3, nf), _pspec(1), _pspec(1, nf)],
        out_specs=(_cspec(T), _cspec(T), _pspec(3), _pspec(3), _pspec(1), _pspec(1)),
        out_shape=(S((Bl, T, F), BF16), S((Bl, T, F), BF16), S((3, F), F32), S((3, F), F32), S((1, F), F32),
                   S((1, F), F32)),
        compiler_params=_cparams("parallel", "arbitrary"), name=name)(u3, u3, dact3, wc, wc, bc, bc)


def _lru_gates(xc, rp, ip, br_ref, bi_ref, lam_ref):
    r = jax.nn.sigmoid(rp + br_ref[...])
    i = jax.nn.sigmoid(ip + bi_ref[...])
    lam = lam_ref[...]
    sp = jnp.maximum(-lam, 0.0) + jnp.log1p(jnp.exp(-jnp.abs(lam)))
    log_a = (-LRU_C) * r * sp
    a = jnp.exp(log_a)
    z = 2.0 * log_a
    one_m_a2 = jnp.where(z > -0.05, -z * (1.0 + z * (0.5 + z * (1.0 / 6.0 + z * (1.0 / 24.0)))), 1.0 - a * a)
    mult = jnp.sqrt(one_m_a2)
    return r, i, sp, a, mult


SCAN_CHUNK = 64


def _scan_down(a, b):
    T = a.shape[0]
    ch = min(SCAN_CHUNK, T)
    row = lax.broadcasted_iota(jnp.int32, (ch, a.shape[1]), 0)
    outs, carry = [], None
    for c in range(T // ch):
        ac, bc = a[c * ch:(c + 1) * ch], b[c * ch:(c + 1) * ch]
        s = 1
        while s < ch:
            a_sh = jnp.where(row >= s, pltpu.roll(ac, s, axis=0), 1.0)
            bc = ac * _shift_down(bc, s, row) + bc
            ac = ac * a_sh
            s *= 2
        if carry is not None:
            bc = bc + ac * carry
        carry = bc[ch - 1:ch, :]
        outs.append(bc)
    return jnp.concatenate(outs, axis=0)


def _scan_up(an, g):
    T = an.shape[0]
    ch = min(SCAN_CHUNK, T)
    row = lax.broadcasted_iota(jnp.int32, (ch, an.shape[1]), 0)
    outs, carry = [], None
    for c in reversed(range(T // ch)):
        ac, gc = an[c * ch:(c + 1) * ch], g[c * ch:(c + 1) * ch]
        s = 1
        while s < ch:
            a_sh = jnp.where(row < ch - s, pltpu.roll(ac, ch - s, axis=0), 1.0)
            gc = ac * _shift_up(gc, s, row) + gc
            ac = ac * a_sh
            s *= 2
        if carry is not None:
            gc = gc + ac * carry
        carry = gc[0:1, :]
        outs.append(gc)
    return jnp.concatenate(outs[::-1], axis=0)


def _rglru_fwd(xc3, gates3, proj3, br, bi, lam, name):
    Bl, T, C = xc3.shape

    def body(xc_ref, rp_ref, ip_ref, ug_ref, br_ref, bi_ref, lam_ref, y_ref, h_ref):
        xc = xc_ref[0]
        r, i, sp, a, mult = _lru_gates(xc, rp_ref[0], ip_ref[0], br_ref, bi_ref, lam_ref)
        h = _scan_down(a, mult * (i * xc))
        h_ref[0] = h
        y_ref[0] = (h * _gelu(ug_ref[0])[0]).astype(BF16)

    return pl.pallas_call(
        body, grid=(C // CT, Bl),
        in_specs=[_cspec(T), _cspec(T), _cspec(T, C // CT), _cspec(T), _pspec(1), _pspec(1), _pspec(1)],
        out_specs=(_cspec(T), _cspec(T)), out_shape=(S((Bl, T, C), BF16), S((Bl, T, C), F32)),
        compiler_params=_cparams("parallel", "arbitrary"), name=name)(xc3, gates3, gates3, proj3, br, bi, lam)


def _rglru_bwd(dy3, xc3, gates3, proj3, h3, br, bi, lam, name):
    Bl, T, C = xc3.shape

    def body(dy_ref, xc_ref, rp_ref, ip_ref, ug_ref, h_ref, br_ref, bi_ref, lam_ref,
             dxc_ref, drp_ref, dip_ref, dug_ref, dbr_ref, dbi_ref, dlam_ref):
        row = lax.broadcasted_iota(jnp.int32, (T, CT), 0)
        xc = xc_ref[0]
        r, i, sp, a, mult = _lru_gates(xc, rp_ref[0], ip_ref[0], br_ref, bi_ref, lam_ref)
        h = h_ref[0]
        dy = dy_ref[0]
        ug = ug_ref[0]
        gel, t = _gelu(ug)
        dug_ref[0] = (dy * h * _dgelu(ug, t)).astype(BF16)
        gacc = _scan_up(_shift_up(a, 1, row), dy * gel)
        da = gacc * _shift_down(h, 1, row)
        ix = i * xc
        d_mult = gacc * ix
        d_i = gacc * mult * xc
        dxc_ref[0] = gacc * mult * i
        d_log_a = da * a - d_mult * (a * a) / mult
        d_r = d_log_a * ((-LRU_C) * sp)
        d_sp = jnp.sum(d_log_a * ((-LRU_C) * r), axis=0, keepdims=True)
        drp = d_r * r * (1.0 - r)
        dip = d_i * i * (1.0 - i)
        drp_ref[0] = drp.astype(BF16)
        dip_ref[0] = dip.astype(BF16)

        @pl.when(pl.program_id(1) == 0)
        def _():
            dbr_ref[...] = jnp.zeros_like(dbr_ref)
            dbi_ref[...] = jnp.zeros_like(dbi_ref)
            dlam_ref[...] = jnp.zeros_like(dlam_ref)

        dbr_ref[...] += jnp.sum(drp, axis=0, keepdims=True)
        dbi_ref[...] += jnp.sum(dip, axis=0, keepdims=True)
        dlam_ref[...] += d_sp * (-jax.nn.sigmoid(-lam_ref[...]))

    vec = S((1, C), F32)
    act = S((Bl, T, C), BF16)
    return pl.pallas_call(
        body, grid=(C // CT, Bl),
        in_specs=[_cspec(T), _cspec(T), _cspec(T), _cspec(T, C // CT), _cspec(T), _cspec(T)] + [_pspec(1)] * 3,
        out_specs=(_cspec(T), _cspec(T), _cspec(T), _cspec(T), _pspec(1), _pspec(1), _pspec(1)),
        out_shape=(S((Bl, T, C), F32), act, act, act, vec, vec, vec),
        compiler_params=_cparams("parallel", "arbitrary"), name=name)(dy3, xc3, gates3, gates3, proj3, h3, br, bi, lam)


NT = (((1,), (1,)), ((), ()))
TN = (((0,), (0,)), ((), ()))


def _hs(h):
    return slice(h * HEAD, (h + 1) * HEAD)


def _head_rows(x):
    head = lax.shift_right_logical(lax.broadcasted_iota(jnp.int32, x.shape, 1), HEAD.bit_length() - 1)
    return jnp.concatenate([jnp.where(head == h, x, jnp.zeros_like(x)) for h in range(MEM_HEADS)], axis=0)


def _head_sum(xbd):
    M = xbd.shape[0] // MEM_HEADS
    head = lax.shift_right_logical(lax.broadcasted_iota(jnp.int32, (M, xbd.shape[1]), 1), HEAD.bit_length() - 1)
    out = jnp.zeros((M, xbd.shape[1]), xbd.dtype)
    for h in range(MEM_HEADS):
        out = jnp.where(head == h, xbd[h * M:(h + 1) * M], out)
    return out


def _mem_probs(q, kbd):
    M = kbd.shape[0] // MEM_HEADS
    s = lax.dot_general(q, kbd, NT, preferred_element_type=F32) * (HEAD ** -0.5)
    ps = []
    for h in range(MEM_HEADS):
        sh = s[:, h * M:(h + 1) * M]
        e = jnp.exp(sh - jnp.max(sh, axis=-1, keepdims=True))
        ps.append(e / jnp.sum(e, axis=-1, keepdims=True))
    return ps


def _mem_attn_fwd(proj3, q_off, mkv3, name):
    Bl, T, _ = proj3.shape
    M = mkv3.shape[1]
    tq = _tile(T, (512, 256, 128))

    def body(q_ref, k_ref, v_ref, o_ref):
        q = q_ref[0].astype(BF16)
        kbd = _head_rows(k_ref[0].astype(BF16))
        vbd = _head_rows(v_ref[0].astype(BF16))
        p = jnp.concatenate(_mem_probs(q, kbd), axis=-1).astype(BF16)
        o_ref[0] = jnp.dot(p, vbd, preferred_element_type=F32).astype(BF16)

    return pl.pallas_call(
        body, grid=(Bl, T // tq),
        in_specs=[pl.BlockSpec((1, tq, MEM_W), lambda b, t: (b, t, q_off // MEM_W)),
                  pl.BlockSpec((1, M, MEM_W), lambda b, t: (b, 0, 0)),
                  pl.BlockSpec((1, M, MEM_W), lambda b, t: (b, 0, 1))],
        out_specs=pl.BlockSpec((1, tq, MEM_W), lambda b, t: (b, t, 0)),
        out_shape=S((Bl, T, MEM_W), BF16), compiler_params=_cparams("parallel", "parallel"), name=name)(
            proj3, mkv3, mkv3)


def _mem_attn_bwd(proj3, q_off, mkv3, do3, name):
    Bl, T, _ = proj3.shape
    M = mkv3.shape[1]
    tq = _tile(T, (512, 256, 128))
    scale = HEAD ** -0.5

    def body(q_ref, k_ref, v_ref, do_ref, dq_ref, dkv_ref):
        q = q_ref[0].astype(BF16)
        kbd = _head_rows(k_ref[0].astype(BF16))
        vbd = _head_rows(v_ref[0].astype(BF16))
        do = do_ref[0].astype(BF16)
        ps = _mem_probs(q, kbd)
        dvbd = lax.dot_general(jnp.concatenate(ps, axis=-1).astype(BF16), do, TN, preferred_element_type=F32)
        dp = lax.dot_general(do, vbd, NT, preferred_element_type=F32)
        dss = []
        for h in range(MEM_HEADS):
            dph = dp[:, h * M:(h + 1) * M]
            dss.append(ps[h] * (dph - jnp.sum(ps[h] * dph, axis=-1, keepdims=True)) * scale)
        ds = jnp.concatenate(dss, axis=-1).astype(BF16)
        dq_ref[0] = jnp.dot(ds, kbd, preferred_element_type=F32).astype(BF16)
        dkbd = lax.dot_general(ds, q, TN, preferred_element_type=F32)

        @pl.when(pl.program_id(1) == 0)
        def _():
            dkv_ref[...] = jnp.zeros_like(dkv_ref)

        dkv_ref[0] += jnp.concatenate([_head_sum(dkbd), _head_sum(dvbd)], axis=-1)

    return pl.pallas_call(
        body, grid=(Bl, T // tq),
        in_specs=[pl.BlockSpec((1, tq, MEM_W), lambda b, t: (b, t, q_off // MEM_W)),
                  pl.BlockSpec((1, M, MEM_W), lambda b, t: (b, 0, 0)),
                  pl.BlockSpec((1, M, MEM_W), lambda b, t: (b, 0, 1)),
                  pl.BlockSpec((1, tq, MEM_W), lambda b, t: (b, t, 0))],
        out_specs=(pl.BlockSpec((1, tq, MEM_W), lambda b, t: (b, t, 0)),
                   pl.BlockSpec((1, M, 2 * MEM_W), lambda b, t: (b, 0, 0))),
        out_shape=(S((Bl, T, MEM_W), BF16), S((Bl, M, 2 * MEM_W), F32)),
        compiler_params=_cparams("parallel", "arbitrary"), name=name)(proj3, mkv3, mkv3, do3)


GROUP_ROWS = SWA_GROUP * WIN


def _group_rows(x, kvh):
    return jnp.concatenate([x[:, _hs(SWA_GROUP * kvh + g)] for g in range(SWA_GROUP)], axis=0)


def _group_col(vals):
    grp = lax.shift_right_logical(lax.broadcasted_iota(jnp.int32, (GROUP_ROWS, 1), 0), WIN.bit_length() - 1)
    col = jnp.full((GROUP_ROWS, 1), vals[-1], F32)
    for g in range(SWA_GROUP - 2, -1, -1):
        col = jnp.where(grp == g, vals[g], col)
    return col


def _swa_probs(qh, kph, kch, sink, slope, has_prev):
    qi = jnp.bitwise_and(lax.broadcasted_iota(jnp.int32, (GROUP_ROWS, WIN), 0), WIN - 1)
    kj = lax.broadcasted_iota(jnp.int32, (GROUP_ROWS, WIN), 1)
    scale = HEAD ** -0.5
    sp = lax.dot_general(qh, kph, NT, preferred_element_type=F32) * scale
    sc = lax.dot_general(qh, kch, NT, preferred_element_type=F32) * scale
    dist_p = (qi + WIN - kj).astype(F32)
    dist_c = (qi - kj).astype(F32)
    neg = -jnp.inf
    sp = jnp.where(kj > qi + jnp.where(has_prev, 0, WIN), sp - slope * dist_p, neg)
    sc = jnp.where(kj <= qi, sc - slope * dist_c, neg)
    m = jnp.maximum(jnp.maximum(jnp.max(sp, axis=-1, keepdims=True), jnp.max(sc, axis=-1, keepdims=True)), sink)
    ep = jnp.exp(sp - m)
    ec = jnp.exp(sc - m)
    es = jnp.exp(sink - m)
    inv = 1.0 / (jnp.sum(ep, axis=-1, keepdims=True) + jnp.sum(ec, axis=-1, keepdims=True) + es)
    return ep * inv, ec * inv, es * inv


def _swa_specs(nb):
    prev = lambda n: jnp.maximum(n - 1, 0)
    q = pl.BlockSpec((1, WIN, MIX_W), lambda b, n: (b, n, 0))
    kp = pl.BlockSpec((1, WIN, MEM_W), lambda b, n: (b, prev(n), 0))
    kc = pl.BlockSpec((1, WIN, MEM_W), lambda b, n: (b, n, 0))
    vp = pl.BlockSpec((1, WIN, MEM_W), lambda b, n: (b, prev(n), 1))
    vc = pl.BlockSpec((1, WIN, MEM_W), lambda b, n: (b, n, 1))
    sm = pl.BlockSpec(memory_space=pltpu.SMEM)
    return q, kp, kc, vp, vc, sm


def _swa_fwd(proj3, kv3, sinks, name):
    Bl, T, _ = proj3.shape
    nb = T // WIN
    q_s, kp_s, kc_s, vp_s, vc_s, sm = _swa_specs(nb)

    def body(q_ref, kp_ref, kc_ref, vp_ref, vc_ref, sink_ref, o_ref):
        has_prev = pl.program_id(1) > 0
        q = q_ref[0].astype(BF16)
        kp, kc = kp_ref[0].astype(BF16), kc_ref[0].astype(BF16)
        vp, vc = vp_ref[0].astype(BF16), vc_ref[0].astype(BF16)
        outs = []
        for kvh in range(SWA_HEADS // SWA_GROUP):
            kvs = _hs(kvh)
            heads = range(SWA_GROUP * kvh, SWA_GROUP * (kvh + 1))
            pp, pc, _ = _swa_probs(_group_rows(q, kvh), kp[:, kvs], kc[:, kvs], _group_col([sink_ref[h] for h in heads]),
                                   _group_col([SLOPES[h] for h in heads]), has_prev)
            og = (jnp.dot(pp.astype(BF16), vp[:, kvs], preferred_element_type=F32)
                  + jnp.dot(pc.astype(BF16), vc[:, kvs], preferred_element_type=F32))
            outs += [og[g * WIN:(g + 1) * WIN] for g in range(SWA_GROUP)]
        o_ref[0] = jnp.concatenate(outs, axis=-1).astype(BF16)

    return pl.pallas_call(
        body, grid=(Bl, nb), in_specs=[q_s, kp_s, kc_s, vp_s, vc_s, sm], out_specs=q_s,
        out_shape=S((Bl, T, MIX_W), BF16), compiler_params=_cparams("parallel", "parallel"), name=name)(
            proj3, kv3, kv3, kv3, kv3, sinks)


def _swa_bwd(proj3, kv3, sinks, do3, name):
    Bl, T, _ = proj3.shape
    nb = T // WIN
    q_s, kp_s, kc_s, vp_s, vc_s, sm = _swa_specs(nb)
    kv_s = pl.BlockSpec((1, WIN, 2 * MEM_W), lambda b, n: (b, n, 0))
    sk_s = pl.BlockSpec((8, LANES), lambda b, n: (0, 0))
    scale = HEAD ** -0.5

    def body(q_ref, kp_ref, kc_ref, vp_ref, vc_ref, sink_ref, do_ref, dq_ref, dkc_ref, dkp_ref, dsk_ref):
        has_prev = pl.program_id(1) > 0
        q = q_ref[0].astype(BF16)
        kp, kc = kp_ref[0].astype(BF16), kc_ref[0].astype(BF16)
        vp, vc = vp_ref[0].astype(BF16), vc_ref[0].astype(BF16)
        do = do_ref[0].astype(BF16)
        lane = lax.broadcasted_iota(jnp.int32, (8, LANES), 1)
        srow = lax.broadcasted_iota(jnp.int32, (8, LANES), 0)
        dsk = jnp.zeros((8, LANES), F32)
        dqs = []
        dkc, dkp, dvc, dvp = [], [], [], []
        grp = lax.shift_right_logical(lax.broadcasted_iota(jnp.int32, (GROUP_ROWS, 1), 0), WIN.bit_length() - 1)
        for kvh in range(SWA_HEADS // SWA_GROUP):
            kvs = _hs(kvh)
            heads = range(SWA_GROUP * kvh, SWA_GROUP * (kvh + 1))
            qg, dog = _group_rows(q, kvh), _group_rows(do, kvh)
            pp, pc, ps = _swa_probs(qg, kp[:, kvs], kc[:, kvs], _group_col([sink_ref[h] for h in heads]),
                                    _group_col([SLOPES[h] for h in heads]), has_prev)
            dpp = lax.dot_general(dog, vp[:, kvs], NT, preferred_element_type=F32)
            dpc = lax.dot_general(dog, vc[:, kvs], NT, preferred_element_type=F32)
            delta = jnp.sum(pp * dpp, axis=-1, keepdims=True) + jnp.sum(pc * dpc, axis=-1, keepdims=True)
            dsp = (pp * (dpp - delta) * scale).astype(BF16)
            dsc = (pc * (dpc - delta) * scale).astype(BF16)
            dqg = (jnp.dot(dsp, kp[:, kvs], preferred_element_type=F32)
                   + jnp.dot(dsc, kc[:, kvs], preferred_element_type=F32))
            dqs += [dqg[g * WIN:(g + 1) * WIN] for g in range(SWA_GROUP)]
            dkc.append(lax.dot_general(dsc, qg, TN, preferred_element_type=F32))
            dkp.append(lax.dot_general(dsp, qg, TN, preferred_element_type=F32))
            dvc.append(lax.dot_general(pc.astype(BF16), dog, TN, preferred_element_type=F32))
            dvp.append(lax.dot_general(pp.astype(BF16), dog, TN, preferred_element_type=F32))
            dsink = ps * delta
            for g, h in enumerate(heads):
                dsk = dsk + jnp.where((lane == h) & (srow == 0), -jnp.sum(jnp.where(grp == g, dsink, 0.0)), 0.0)
        dq_ref[0] = jnp.concatenate(dqs, axis=-1).astype(BF16)
        dkc_ref[0] = jnp.concatenate(dkc + dvc, axis=-1)
        dkp_ref[0] = jnp.concatenate(dkp + dvp, axis=-1)

        @pl.when((pl.program_id(0) == 0) & (pl.program_id(1) == 0))
        def _():
            dsk_ref[...] = jnp.zeros_like(dsk_ref)

        dsk_ref[...] += dsk

    return pl.pallas_call(
        body, grid=(Bl, nb), in_specs=[q_s, kp_s, kc_s, vp_s, vc_s, sm, q_s], out_specs=(q_s, kv_s, kv_s, sk_s),
        out_shape=(S((Bl, T, MIX_W), BF16), S((Bl, T, 2 * MEM_W), F32), S((Bl, T, 2 * MEM_W), F32), S((8, LANES), F32)),
        compiler_params=_cparams("arbitrary", "arbitrary"), name=name)(proj3, kv3, kv3, kv3, kv3, sinks, do3)


def _kv_grad_combine(parts, name):
    Bl, T, W = parts[0][0].shape
    nb = T // WIN
    nl = len(parts)

    def body(*refs):
        o_ref = refs[-1]
        has_next = jnp.where(pl.program_id(1) == nb - 1, 0.0, 1.0)
        acc = None
        for l in range(nl):
            c = refs[2 * l][0] + has_next * refs[2 * l + 1][0]
            acc = c if acc is None else acc + c
        o_ref[0] = acc.astype(BF16)

    cur = pl.BlockSpec((1, WIN, W), lambda b, n: (b, n, 0))
    nxt = pl.BlockSpec((1, WIN, W), lambda b, n: (b, jnp.minimum(n + 1, nb - 1), 0))
    return pl.pallas_call(
        body, grid=(Bl, nb), in_specs=[cur, nxt] * nl, out_specs=cur, out_shape=S((Bl, T, W), BF16),
        compiler_params=_cparams("parallel", "parallel"), name=name)(*[a for pr in parts for a in pr])


def _loss_bwd(y, target, name="loss"):
    N, D = y.shape
    tm = _tile(N, (512, 256, 128))

    def body(y_ref, t_ref, dy_ref, l_ref):
        e = y_ref[...] - t_ref[...]
        dy_ref[...] = e * (1.0 / D)

        @pl.when(pl.program_id(0) == 0)
        def _():
            l_ref[...] = jnp.zeros_like(l_ref)

        l_ref[...] += jnp.sum(e * e, axis=0, keepdims=True) * (0.5 / D)

    row = pl.BlockSpec((tm, D), lambda i: (i, 0))
    vec = pl.BlockSpec((1, D), lambda i: (0, 0))
    return pl.pallas_call(
        body, grid=(N // tm,), in_specs=[row, row], out_specs=(row, vec), out_shape=(S((N, D), F32), S((1, D), F32)),
        compiler_params=_cparams("arbitrary"), name=name)(y, target)


def _all_gather(x, name):
    R, C = x.shape

    def body(x_ref, out_ref, send_sems, recv_sems, local_sem):
        mx, my, mc = lax.axis_index("x"), lax.axis_index("y"), lax.axis_index("c")
        me, sibling = (mx, my, mc), (mx, my, 1 - mc)
        chips = [(1 - mx, my), (mx, 1 - my), (1 - mx, 1 - my)]

        def rows(px, py, pc):
            return out_ref.at[4 * px + 2 * py + pc]

        def copy(kk, block, to, src=None):
            return pltpu.make_async_remote_copy(
                src_ref=rows(*block) if src is None else src, dst_ref=rows(*block), send_sem=send_sems.at[kk],
                recv_sem=recv_sems.at[kk], device_id=to, device_id_type=MESH)

        mine = pltpu.make_async_copy(x_ref, rows(*me), local_sem)
        mine.start()
        first = [copy(0, me, sibling, src=x_ref)]
        first += [copy(1 + j, me, (*chip, mc), src=x_ref) for j, chip in enumerate(chips)]
        for cp in first:
            cp.start()
        passed = [copy(4 + j, (*chip, mc), sibling) for j, chip in enumerate(chips)]
        for j, chip in enumerate(chips):
            copy(1 + j, (*chip, mc), me).wait_recv()
            passed[j].start()
        copy(0, sibling, me).wait_recv()
        for j, chip in enumerate(chips):
            copy(4 + j, (*chip, 1 - mc), me).wait_recv()
        for cp in first + passed:
            cp.wait_send()
        mine.wait()

    return pl.pallas_call(
        body, out_shape=S((N_DEV, R, C), x.dtype), in_specs=[ANY], out_specs=ANY,
        scratch_shapes=[pltpu.SemaphoreType.DMA((7,)), pltpu.SemaphoreType.DMA((7,)), pltpu.SemaphoreType.DMA(())],
        name=name)(x)


def _ag_weights(shards, row_sharded, name):
    n = len(shards)

    def full_shape(a, rows):
        if rows:
            return a.shape[:-2] + (N_DEV * a.shape[-2],) + a.shape[-1:]
        return (N_DEV,) + a.shape

    def body(*refs):
        x_refs, o_refs = refs[:n], refs[n:2 * n]
        send_sems, recv_sems, local_sems = refs[2 * n:]
        mx, my, mc = lax.axis_index("x"), lax.axis_index("y"), lax.axis_index("c")
        me, sibling = (mx, my, mc), (mx, my, 1 - mc)
        chips = [(1 - mx, my), (mx, 1 - my), (1 - mx, 1 - my)]

        def dst(t, px, py, pc):
            d = 4 * px + 2 * py + pc
            if not row_sharded[t]:
                return o_refs[t].at[d]
            r = shards[t].shape[-2]
            idx = (slice(None),) * (shards[t].ndim - 2) + (pl.ds(pl.multiple_of(d * r, 16), r), slice(None))
            return o_refs[t].at[idx]

        def copy(kk, t, block, to, src=None):
            return pltpu.make_async_remote_copy(
                src_ref=dst(t, *block) if src is None else src, dst_ref=dst(t, *block),
                send_sem=send_sems.at[kk * n + t], recv_sem=recv_sems.at[kk * n + t], device_id=to,
                device_id_type=MESH)

        mine = [pltpu.make_async_copy(x_refs[t], dst(t, *me), local_sems.at[t]) for t in range(n)]
        for cp in mine:
            cp.start()
        first = []
        for t in range(n):
            first.append(copy(0, t, me, sibling, src=x_refs[t]))
            first += [copy(1 + j, t, me, (*chip, mc), src=x_refs[t]) for j, chip in enumerate(chips)]
        for cp in first:
            cp.start()
        passed = []
        for j, chip in enumerate(chips):
            for t in range(n):
                copy(1 + j, t, (*chip, mc), me).wait_recv()
                cp = copy(4 + j, t, (*chip, mc), sibling)
                cp.start()
                passed.append(cp)
        for t in range(n):
            copy(0, t, sibling, me).wait_recv()
            for j, chip in enumerate(chips):
                copy(4 + j, t, (*chip, 1 - mc), me).wait_recv()
        for cp in first + passed:
            cp.wait_send()
        for cp in mine:
            cp.wait()

    return pl.pallas_call(
        body, out_shape=tuple(S(full_shape(a, r), a.dtype) for a, r in zip(shards, row_sharded)),
        in_specs=[ANY] * n, out_specs=tuple([ANY] * n),
        scratch_shapes=[pltpu.SemaphoreType.DMA((7 * n,)), pltpu.SemaphoreType.DMA((7 * n,)),
                        pltpu.SemaphoreType.DMA((n,))],
        name=name)(*shards)


FLIPS = [(fx, fy, fc) for fx in (0, 1) for fy in (0, 1) for fc in (0, 1)][1:]
HBM = pl.BlockSpec(memory_space=pltpu.HBM)
SEM = pl.BlockSpec(memory_space=pltpu.SEMAPHORE)
EFFECT = pltpu.SideEffectType.DATAFLOW_SIDE_EFFECTING


def _hbm(a):
    return pltpu.with_memory_space_constraint(a, pltpu.HBM)


def _flips(gather):
    return [(0, 0, 0)] + FLIPS if gather else FLIPS


def _split_copies(gather, s_refs, l_refs, send_sems, recv_sems):
    n = len(s_refs)
    mx, my, mc = lax.axis_index("x"), lax.axis_index("y"), lax.axis_index("c")
    me = 4 * mx + 2 * my + mc
    copies = []
    for k, (fx, fy, fc) in enumerate(_flips(gather)):
        px, py, pc = (1 - mx if fx else mx), (1 - my if fy else my), (1 - mc if fc else mc)
        for t in range(n):
            if gather:
                src = s_refs[t]
                r = src.shape[0]
                dst = l_refs[t].at[pl.ds(pl.multiple_of(me * r, 16), r), :]
            else:
                src = s_refs[t].at[:, 4 * px + 2 * py + pc]
                dst = l_refs[t].at[k]
            copies.append(pltpu.make_async_remote_copy(
                src_ref=src, dst_ref=dst, send_sem=send_sems.at[k * n + t], recv_sem=recv_sems.at[k * n + t],
                device_id=(px, py, pc), device_id_type=MESH))
    return copies


def _split_start(gather, srcs, lands, after, name):
    n = len(srcs)
    n_sem = len(_flips(gather)) * n

    def body(*refs):
        s_refs, l_refs = refs[:n], refs[n:2 * n]
        send_sems, recv_sems = refs[2 * n + 1], refs[2 * n + 2]
        token = refs[-1]
        for cp in _split_copies(gather, s_refs, l_refs, send_sems, recv_sems):
            cp.start()
        token[...] = jnp.zeros_like(token)

    outs = pl.pallas_call(
        body, name=name,
        out_shape=(pltpu.SemaphoreType.DMA((n_sem,)), pltpu.SemaphoreType.DMA((n_sem,)))
        + tuple(pltpu.HBM(a.shape, a.dtype) for a in lands) + (S((8, LANES), F32),),
        in_specs=[HBM] * (2 * n) + [ANY],
        out_specs=(SEM, SEM) + (HBM,) * n + (pl.BlockSpec(memory_space=pltpu.VMEM),),
        input_output_aliases={n + i: 2 + i for i in range(n)},
        compiler_params=pltpu.CompilerParams(has_side_effects=EFFECT),
    )(*[_hbm(a) for a in srcs], *[_hbm(a) for a in lands], after)
    return outs[0], outs[1], list(srcs), list(outs[2:2 + n]), outs[-1]


def _split_wait(gather, send_sems, recv_sems, srcs, lands, after, name):
    n = len(srcs)

    def body(*refs):
        s_refs, l_refs = refs[:n], refs[n:2 * n]
        ssem, rsem = refs[2 * n], refs[2 * n + 1]
        copies = _split_copies(gather, s_refs, l_refs, ssem, rsem)
        for cp in copies:
            cp.wait_send()
        for cp in copies:
            cp.wait_recv()

    outs = pl.pallas_call(
        body, name=name, out_shape=tuple(pltpu.HBM(a.shape, a.dtype) for a in lands),
        in_specs=[HBM] * (2 * n) + [SEM, SEM, ANY], out_specs=(HBM,) * n,
        input_output_aliases={n + i: i for i in range(n)},
        compiler_params=pltpu.CompilerParams(has_side_effects=EFFECT),
    )(*[_hbm(a) for a in srcs], *lands, send_sems, recv_sems, after)
    return list(outs)


def _adamw_math(w, g, m, v):
    m = ADAM_B1 * m + (1.0 - ADAM_B1) * g
    v = ADAM_B2 * v + (1.0 - ADAM_B2) * (g * g)
    m_hat = m / (1.0 - ADAM_B1 ** ADAM_STEP)
    v_hat = v / (1.0 - ADAM_B2 ** ADAM_STEP)
    delta = -ADAM_LR * (m_hat / (jnp.sqrt(v_hat) + ADAM_EPS) + ADAM_WD * w)
    return delta, m, v


def _adamw_layers(owns, gots, w, m, v, name):
    L, B, C = w.shape
    per_row = 2 * L * len(FLIPS) * C * owns[0].dtype.itemsize
    tb = max([t for t in range(16, B + 1, 16) if B % t == 0 and (t * per_row <= 24 * 1024 * 1024 or t == 16)] or [B])
    me = (4 * lax.axis_index("x") + 2 * lax.axis_index("y") + lax.axis_index("c")).astype(jnp.int32).reshape(1)

    def body(me_ref, *refs):
        own_refs, got_refs = refs[:L], refs[L:2 * L]
        w_ref, m_ref, v_ref = refs[2 * L:2 * L + 3]
        g_out, d_out, m_out, v_out = refs[2 * L + 3:]
        layer = pl.program_id(0)
        for kk in range(L):
            @pl.when(layer == kk)
            def _():
                g = own_refs[kk][0].astype(F32)
                for s in range(len(FLIPS)):
                    g = g + got_refs[kk][s].astype(F32)
                d, mn, vn = _adamw_math(w_ref[...], g, m_ref[...], v_ref[...])
                g_out[...] = g
                d_out[...] = d
                m_out[...] = mn
                v_out[...] = vn

    def row(kk, layer, i):
        return jnp.where(layer == kk, i, 0)

    blk = pl.BlockSpec((1, tb, C), lambda layer, i, me_ref: (layer, i, 0))
    own_specs = [pl.BlockSpec((1, 1, tb, C), lambda layer, i, me_ref, kk=kk: (0, me_ref[0], row(kk, layer, i), 0))
                 for kk in range(L)]
    got_specs = [pl.BlockSpec((len(FLIPS), 1, tb, C), lambda layer, i, me_ref, kk=kk: (0, 0, row(kk, layer, i), 0))
                 for kk in range(L)]
    return pl.pallas_call(
        body,
        grid_spec=pltpu.PrefetchScalarGridSpec(
            num_scalar_prefetch=1, grid=(L, B // tb), in_specs=own_specs + got_specs + [blk, blk, blk],
            out_specs=(blk, blk, blk, blk)),
        out_shape=(S((L, B, C), F32),) * 4, compiler_params=_cparams("arbitrary", "arbitrary"), name=name)(
            me, *owns, *gots, w, m, v)


def _adamw_replicated(parts, w, m, v, name):
    R, C = w.shape
    rb = _tile(R, (512, 256, 128, 64, 32, 16))

    def body(p_ref, w_ref, m_ref, v_ref, g_out, d_out, m_out, v_out):
        g = p_ref[0].astype(F32)
        for j in range(1, N_DEV):
            g = g + p_ref[j].astype(F32)
        d, mn, vn = _adamw_math(w_ref[...], g, m_ref[...], v_ref[...])
        g_out[...] = g
        d_out[...] = d
        m_out[...] = mn
        v_out[...] = vn

    blk = pl.BlockSpec((rb, C), lambda i: (i, 0))
    return pl.pallas_call(
        body, grid=(R // rb,), in_specs=[pl.BlockSpec((N_DEV, rb, C), lambda i: (0, i, 0)), blk, blk, blk],
        out_specs=(blk, blk, blk, blk), out_shape=(S((R, C), F32),) * 4, compiler_params=_cparams("parallel"),
        name=name)(parts, w, m, v)


def _pack(arrs, rows_mult, dtype):
    flat = jnp.concatenate([a.reshape(-1).astype(dtype) for a in arrs])
    n = flat.shape[0]
    per = rows_mult * LANES
    tot = -(-n // per) * per
    return jnp.pad(flat, (0, tot - n)).reshape(tot // LANES, LANES)


def _unpack(blob, shapes):
    flat = blob.reshape(-1)
    out, off = [], 0
    for shp in shapes:
        n = int(np.prod(shp))
        out.append(flat[off:off + n].reshape(shp))
        off += n
    return out


def _small_to_natural(g8):
    t = jnp.moveaxis(g8, 0, -2)
    return t.reshape(t.shape[:-2] + (N_DEV * t.shape[-1],))


def _small_to_cols(g):
    t = g.reshape(g.shape[:-1] + (N_DEV, g.shape[-1] // N_DEV))
    return jnp.moveaxis(t, -2, 0)


def _block_diag(w):
    nb, bs, _ = w.shape
    eye = jnp.eye(nb, dtype=w.dtype)
    return (eye[:, None, :, None] * w[:, :, None, :]).reshape(nb * bs, nb * bs)


def _diag_blocks(d, nb, bs):
    d4 = d.reshape(nb, bs, nb, bs)
    return jnp.stack([d4[i, :, i, :] for i in range(nb)])


def kernel(x, mem, g_mix_pre, g_mix_post, g_ffn_pre, g_ffn_post, g_mem, w_mem_kv, w_mix_out, w_ffn_up, w_ffn_conv, b_ffn_conv, w_ffn_down, w_in_a, w_conv_a, b_conv_a, w_rg_r, b_rg_r, w_rg_i, b_rg_i, lru_lambda, w_in_b, sinks_b, g_kv, w_kv, loss_target, m_g_mix_pre, m_g_mix_post, m_g_ffn_pre, m_g_ffn_post, m_g_mem, m_w_mem_kv, m_w_mix_out, m_w_ffn_up, m_w_ffn_conv, m_b_ffn_conv, m_w_ffn_down, m_w_in_a, m_w_conv_a, m_b_conv_a, m_w_rg_r, m_b_rg_r, m_w_rg_i, m_b_rg_i, m_lru_lambda, m_w_in_b, m_sinks_b, m_g_kv, m_w_kv, v_g_mix_pre, v_g_mix_post, v_g_ffn_pre, v_g_ffn_post, v_g_mem, v_w_mem_kv, v_w_mix_out, v_w_ffn_up, v_w_ffn_conv, v_b_ffn_conv, v_w_ffn_down, v_w_in_a, v_w_conv_a, v_b_conv_a, v_w_rg_r, v_b_rg_r, v_w_rg_i, v_b_rg_i, v_lru_lambda, v_w_in_b, v_sinks_b, v_g_kv, v_w_kv):
    w_loc = dict(g_mix_pre=g_mix_pre, g_mix_post=g_mix_post, g_ffn_pre=g_ffn_pre, g_ffn_post=g_ffn_post, g_mem=g_mem,
                 w_mem_kv=w_mem_kv, w_mix_out=w_mix_out, w_ffn_up=w_ffn_up, w_ffn_conv=w_ffn_conv,
                 b_ffn_conv=b_ffn_conv, w_ffn_down=w_ffn_down, w_in_a=w_in_a, w_conv_a=w_conv_a, b_conv_a=b_conv_a,
                 w_rg_r=w_rg_r, b_rg_r=b_rg_r, w_rg_i=w_rg_i, b_rg_i=b_rg_i, lru_lambda=lru_lambda, w_in_b=w_in_b,
                 sinks_b=sinks_b, g_kv=g_kv, w_kv=w_kv)
    m_loc = dict(g_mix_pre=m_g_mix_pre, g_mix_post=m_g_mix_post, g_ffn_pre=m_g_ffn_pre, g_ffn_post=m_g_ffn_post,
                 g_mem=m_g_mem, w_mem_kv=m_w_mem_kv, w_mix_out=m_w_mix_out, w_ffn_up=m_w_ffn_up,
                 w_ffn_conv=m_w_ffn_conv, b_ffn_conv=m_b_ffn_conv, w_ffn_down=m_w_ffn_down, w_in_a=m_w_in_a,
                 w_conv_a=m_w_conv_a, b_conv_a=m_b_conv_a, w_rg_r=m_w_rg_r, b_rg_r=m_b_rg_r, w_rg_i=m_w_rg_i,
                 b_rg_i=m_b_rg_i, lru_lambda=m_lru_lambda, w_in_b=m_w_in_b, sinks_b=m_sinks_b, g_kv=m_g_kv,
                 w_kv=m_w_kv)
    v_loc = dict(g_mix_pre=v_g_mix_pre, g_mix_post=v_g_mix_post, g_ffn_pre=v_g_ffn_pre, g_ffn_post=v_g_ffn_post,
                 g_mem=v_g_mem, w_mem_kv=v_w_mem_kv, w_mix_out=v_w_mix_out, w_ffn_up=v_w_ffn_up,
                 w_ffn_conv=v_w_ffn_conv, b_ffn_conv=v_b_ffn_conv, w_ffn_down=v_w_ffn_down, w_in_a=v_w_in_a,
                 w_conv_a=v_w_conv_a, b_conv_a=v_b_conv_a, w_rg_r=v_w_rg_r, b_rg_r=v_b_rg_r, w_rg_i=v_w_rg_i,
                 b_rg_i=v_b_rg_i, lru_lambda=v_lru_lambda, w_in_b=v_w_in_b, sinks_b=v_sinks_b, g_kv=v_g_kv,
                 w_kv=v_w_kv)

    Bl, T, D = x.shape
    Ml = mem.shape[1]
    N = Bl * T
    depth = g_mix_pre.shape[0]
    n_a = w_in_a.shape[0]
    F = w_ffn_down.shape[1] * N_DEV
    def as_rows(n, a):
        return jnp.swapaxes(a, -1, -2) if n in TRANSPOSED else a

    def mix_keys(l):
        keys = [("w_mem_kv", l), ("w_mix_out", l), ("w_in_a", l) if l < n_a else ("w_in_b", l - n_a)]
        return keys + ([("w_kv", None)] if l == n_a else [])

    def ffn_keys(l):
        return [("w_ffn_up", l), ("w_ffn_down", l)]

    def shard_of(key):
        n, i = key
        return as_rows(n, w_loc[n] if i is None else w_loc[n][i]).astype(BF16)

    W = {}
    keys0, keys0_rest = mix_keys(0)[2:], mix_keys(0)[:2]
    got0 = _ag_weights([shard_of(kk) for kk in keys0] + [w_loc[n] for n in SMALL_SHARDED],
                       [True] * len(keys0) + [False] * len(SMALL_SHARDED), name="ag_weights_0")
    W.update(zip(keys0, got0))
    for n, a in zip(SMALL_SHARDED, got0[len(keys0):]):
        W[n] = _small_to_natural(a)

    def gather_start(keys, after, tag):
        shards = [shard_of(kk) for kk in keys]
        lands = [lax.empty((N_DEV * s.shape[0],) + s.shape[1:], s.dtype) for s in shards]
        return (keys, tag) + _split_start(True, shards, lands, after, name=f"ag_start_{tag}")

    def gather_wait(pending, after):
        keys, tag, ssem, rsem, srcs, lands, _ = pending
        W.update(zip(keys, _split_wait(True, ssem, rsem, srcs, lands, after, name=f"ag_wait_{tag}")))

    pending_rest = gather_start(keys0_rest, got0[0], "mix_0")
    pending_ffn = gather_start(ffn_keys(0)[:1], pending_rest[-1], "ffn_up_0")
    pending_down0 = gather_start(ffn_keys(0)[1:], pending_ffn[-1], "ffn_down_0")

    nblk, bsz = w_rg_r.shape[1], w_rg_r.shape[2]
    wbd = [jnp.concatenate([_block_diag(w_rg_r[j]), _block_diag(w_rg_i[j])], axis=1).astype(BF16) for j in range(n_a)]

    def vec(a):
        return a.reshape(1, -1)

    x2 = x.reshape(N, D)
    mem2 = mem.reshape(Bl * Ml, D)
    saved = []
    kvn = kv3 = x_kv = None
    xs = x2
    h1 = _rms_fwd(xs, vec(g_mix_pre[0]), BF16, name="rms_mixpre_0")
    for l in range(depth):
        sv = {"x0": xs}
        tok = None
        if l + 1 < depth:
            pending = gather_start(mix_keys(l + 1), pending_down0[-1] if l == 0 else W[("w_mem_kv", l)],
                                   f"mix_{l + 1}")
            pending_next_ffn = gather_start(ffn_keys(l + 1), pending[-1], f"ffn_{l + 1}")
            tok = pending_next_ffn[-1]
        memn = _rms_fwd(mem2, vec(g_mem[l]), BF16, name=f"rms_mem_{l}")
        if l < n_a:
            j = l
            proj = _mm(h1, W[("w_in_a", j)], tb=True, after=tok, name=f"mm_in_{l}")
            proj3 = proj.reshape(Bl, T, -1)
            xc3 = _conv_fwd_call(proj3, MIX_W, MIX_W, W["w_conv_a"][j], vec(W["b_conv_a"][j]), name=f"conv_a_{l}")
            gates3 = _mm(xc3.reshape(N, MIX_W), wbd[j], name=f"mm_gates_{l}").reshape(Bl, T, 2 * MIX_W)
            y_main3, hs3 = _rglru_fwd(xc3, gates3, proj3, vec(b_rg_r[j]), vec(b_rg_i[j]), vec(W["lru_lambda"][j]),
                                      name=f"rglru_fwd_{l}")
            q_off = 2 * MIX_W
            sv.update(xc3=xc3, gates3=gates3, hs3=hs3)
        else:
            j = l - n_a
            if l == n_a:
                x_kv = xs
                kv3 = _mm(kvn, W[("w_kv", None)], name="mm_kv").reshape(Bl, T, 2 * MEM_W)
            proj = _mm(h1, W[("w_in_b", j)], after=tok, name=f"mm_in_{l}")
            proj3 = proj.reshape(Bl, T, -1)
            y_main3 = _swa_fwd(proj3, kv3, sinks_b[j], name=f"swa_fwd_{l}")
            q_off = MIX_W
        if l == 0:
            gather_wait(pending_rest, y_main3)
        mkv3 = _mm(memn, W[("w_mem_kv", l)], name=f"mm_memkv_{l}").reshape(Bl, Ml, 2 * MEM_W)
        y_mem3 = _mem_attn_fwd(proj3, q_off, mkv3, name=f"memattn_fwd_{l}")
        y_main = y_main3.reshape(N, MIX_W)
        y_mem = y_mem3.reshape(N, MEM_W)
        y = _mm_sum([(y_main, W[("w_mix_out", l)], (0, 0)), (y_mem, W[("w_mix_out", l)], (MIX_W, 0))], n=D,
                    name=f"mm_mixout_{l}")
        x1, h2 = _rms_pair_fwd(y, vec(g_mix_post[l]), xs, [vec(g_ffn_pre[l])], name=f"rms_mixpost_ffnpre_{l}")
        gather_wait(pending_ffn, h2)
        if l + 1 < depth:
            pending_ffn = pending_next_ffn
        u3 = _mm(h2, W[("w_ffn_up", l)], tb=True, name=f"mm_up_{l}").reshape(Bl, T, 2 * F)
        act3 = _ffn_mid_fwd(u3, W["w_ffn_conv"][l], vec(b_ffn_conv[l]), name=f"ffn_mid_fwd_{l}")
        act = act3.reshape(N, F)
        if l == 0:
            gather_wait(pending_down0, act)
        f = _mm(act, W[("w_ffn_down", l)], name=f"mm_down_{l}")
        sv.update(h1=h1, memn=memn, mkv3=mkv3, proj3=proj3, q_off=q_off, y_main=y_main, y_mem=y_mem, y=y, x1=x1,
                  h2=h2, u3=u3, act=act, f=f)
        saved.append(sv)
        if l + 1 < depth:
            gains = [vec(g_mix_pre[l + 1])] + ([vec(g_kv)] if l + 1 == n_a else [])
            xs, h1, *rest = _rms_pair_fwd(f, vec(g_ffn_post[l]), x1, gains, name=f"rms_ffnpost_mixpre_{l}")
            if rest:
                kvn = rest[0]
            gather_wait(pending, xs)
        else:
            xs = _rms_fwd(f, vec(g_ffn_post[l]), F32, res=x1, name=f"rms_ffnpost_{l}")

    dxs, loss_vec = _loss_bwd(xs, loss_target.reshape(N, D))
    loss = lax.psum(jnp.sum(loss_vec), ("x", "y", "c"))

    G = {n: [None] * w_loc[n].shape[0] for n in REPL + SMALL_SHARDED if n != "g_kv"}
    GW = {}

    def dw(key, off, a, b_, nm):
        GW[key] = _mm(a, b_, ta=True, out_dtype=BF16, into=(GW.get(key), (1,) + W[key].shape, 0, off), name=nm)

    def grad_blocks(key):
        g = GW[key]
        return g.reshape(1, N_DEV, g.shape[1] // N_DEV, g.shape[2])

    reduces = []

    def reduce_start(keys, after, tag):
        srcs = [grad_blocks(kk) for kk in keys]
        lands = [lax.empty((len(FLIPS),) + s.shape[:1] + s.shape[2:], s.dtype) for s in srcs]
        started = _split_start(False, srcs, lands, after, name=f"rs_start_{tag}")
        reduces.append((keys, tag) + started)
        return started[-1]

    kv_parts = []
    df = None
    for l in reversed(range(depth)):
        sv = saved[l]
        proj3 = sv["proj3"]
        if df is None:
            df, dg = _rms_bwd(sv["f"], vec(g_ffn_post[l]), dxs, out_dtype=BF16, name=f"rmsb_ffnpost_{l}")
            G["g_ffn_post"][l] = dg[0]
        dact = _mm(df, W[("w_ffn_down", l)], tb=True, name=f"mmb_down_dx_{l}")
        dw(("w_ffn_down", l), (0, 0), sv["act"], df, f"mmb_down_dw_{l}")
        dug3, duv3, dwg, dwv, dbg, dbv = _ffn_mid_bwd(sv["u3"], dact.reshape(Bl, T, F),
                                                      W["w_ffn_conv"][l], vec(b_ffn_conv[l]), name=f"ffn_mid_bwd_{l}")
        G["w_ffn_conv"][l] = jnp.concatenate([dwg, dwv], axis=1)
        G["b_ffn_conv"][l] = jnp.concatenate([dbg, dbv], axis=1)[0]
        dug, duv = dug3.reshape(N, F), duv3.reshape(N, F)
        dw(("w_ffn_up", l), (0, 0), dug, sv["h2"], f"mmb_up_dw_g_{l}")
        dw(("w_ffn_up", l), (F, 0), duv, sv["h2"], f"mmb_up_dw_v_{l}")
        tok = reduce_start([("w_ffn_down", l), ("w_ffn_up", l)], dug, f"ffn_{l}")
        dh2 = _mm_sum([(dug, W[("w_ffn_up", l)], (0, 0)), (duv, W[("w_ffn_up", l)], (F, 0))], n=D, after=tok,
                      name=f"mmb_up_dx_{l}")
        dx1, dy, dg, dg2 = _rms_pair_bwd(sv["x1"], vec(g_ffn_pre[l]), dh2, dxs, sv["y"], vec(g_mix_post[l]),
                                         name=f"rmsb_ffnpre_mixpost_{l}")
        G["g_ffn_pre"][l] = dg[0]
        G["g_mix_post"][l] = dg2[0]
        dy_main = _mm(dy, W[("w_mix_out", l)], tb=True, n=MIX_W, k=D, name=f"mmb_mixout_dmain_{l}")
        dy_mem = _mm(dy, W[("w_mix_out", l)], tb=True, n=MEM_W, k=D, b_off=(MIX_W, 0),
                     name=f"mmb_mixout_dmem_{l}")
        dw(("w_mix_out", l), (0, 0), sv["y_main"], dy, f"mmb_mixout_dw_main_{l}")
        dw(("w_mix_out", l), (MIX_W, 0), sv["y_mem"], dy, f"mmb_mixout_dw_mem_{l}")
        dq_mem3, dmkv3 = _mem_attn_bwd(proj3, sv["q_off"], sv["mkv3"], dy_mem.reshape(Bl, T, MEM_W),
                                       name=f"memattn_bwd_{l}")
        dq_mem = dq_mem3.reshape(N, MEM_W)
        dmkv = dmkv3.reshape(Bl * Ml, 2 * MEM_W)
        dw(("w_mem_kv", l), (0, 0), sv["memn"], dmkv, f"mmb_memkv_dw_{l}")
        dmemn = _mm(dmkv, W[("w_mem_kv", l)], tb=True, name=f"mmb_memkv_dx_{l}")
        _, dg = _rms_bwd(mem2, vec(g_mem[l]), dmemn, name=f"rmsb_mem_{l}")
        G["g_mem"][l] = dg[0]
        dy_main3 = dy_main.reshape(Bl, T, MIX_W)
        if l < n_a:
            j = l
            dxc3, drp3, dip3, dugate3, dbr, dbi, dlam = _rglru_bwd(
                dy_main3, sv["xc3"], sv["gates3"], proj3, sv["hs3"], vec(b_rg_r[j]), vec(b_rg_i[j]),
                vec(W["lru_lambda"][j]), name=f"rglru_bwd_{l}")
            G["b_rg_r"][j] = dbr.reshape(nblk, bsz)
            G["b_rg_i"][j] = dbi.reshape(nblk, bsz)
            G["lru_lambda"][j] = dlam[0]
            drp, dip = drp3.reshape(N, MIX_W), dip3.reshape(N, MIX_W)
            xc2 = sv["xc3"].reshape(N, MIX_W)
            G["w_rg_r"][j] = _diag_blocks(_mm(xc2, drp, ta=True, name=f"mmb_gates_dw_r_{l}"), nblk, bsz)
            G["w_rg_i"][j] = _diag_blocks(_mm(xc2, dip, ta=True, name=f"mmb_gates_dw_i_{l}"), nblk, bsz)
            dxc = _mm_sum([(drp, wbd[j], (0, 0)), (dip, wbd[j], (0, MIX_W))], tb=True, n=MIX_W,
                          add=dxc3.reshape(N, MIX_W), name=f"mmb_gates_dx_{l}")
            dux3, dwc, dbc = _conv_bwd_call(dxc.reshape(Bl, T, MIX_W), proj3, MIX_W, MIX_W, W["w_conv_a"][j],
                                            name=f"conv_a_bwd_{l}")
            G["w_conv_a"][j] = dwc
            G["b_conv_a"][j] = dbc[0]
            pieces = [(dugate3.reshape(N, MIX_W), 0), (dux3.reshape(N, MIX_W), MIX_W), (dq_mem, 2 * MIX_W)]
            in_key = ("w_in_a", j)
        else:
            j = l - n_a
            dq3, dkc, dkp, dsk = _swa_bwd(proj3, kv3, sinks_b[j], dy_main3, name=f"swa_bwd_{l}")
            kv_parts.append((dkc, dkp))
            G["sinks_b"][j] = dsk[0, :SWA_HEADS]
            pieces = [(dq3.reshape(N, MIX_W), 0), (dq_mem, MIX_W)]
            in_key = ("w_in_b", j)
        in_t = in_key[0] in TRANSPOSED
        for pi, (piece, off) in enumerate(pieces):
            if in_t:
                dw(in_key, (off, 0), piece, sv["h1"], f"mmb_in_dw_{pi}_{l}")
            else:
                dw(in_key, (0, off), sv["h1"], piece, f"mmb_in_dw_{pi}_{l}")
        tok = reduce_start([("w_mix_out", l), ("w_mem_kv", l), in_key], dy, f"mix_{l}")
        dh1 = _mm_sum([(piece, W[in_key], (off, 0) if in_t else (0, off)) for piece, off in pieces], tb=not in_t, n=D,
                      after=tok, name=f"mmb_in_dx_{l}")
        if l > 0 and l != n_a:
            dxs, df, dg, dg2 = _rms_pair_bwd(sv["x0"], vec(g_mix_pre[l]), dh1, dx1, saved[l - 1]["f"],
                                             vec(g_ffn_post[l - 1]), name=f"rmsb_mixpre_ffnpost_{l}")
            G["g_ffn_post"][l - 1] = dg2[0]
        else:
            dxs, dg = _rms_bwd(sv["x0"], vec(g_mix_pre[l]), dh1, add=dx1, name=f"rmsb_mixpre_{l}")
            df = None
        G["g_mix_pre"][l] = dg[0]
        if l == n_a:
            dkv = _kv_grad_combine(kv_parts, name="kv_grad_combine").reshape(N, 2 * MEM_W)
            dw(("w_kv", None), (0, 0), kvn, dkv, "mmb_kv_dw")
            tok = reduce_start([("w_kv", None)], dkv, "kv")
            dkvn = _mm(dkv, W[("w_kv", None)], tb=True, after=tok, name="mmb_kv_dx")
            dxs, dg = _rms_bwd(x_kv, vec(g_kv), dkvn, add=dxs, name="rmsb_kv")
            G["g_kv"] = dg[0]
    grad_x = dxs.reshape(Bl, T, D)
    Gf = {n: (jnp.stack(g) if isinstance(g, list) else g) for n, g in G.items()}

    small4 = []
    for n in SMALL_SHARDED:
        t = _small_to_cols(Gf[n]).astype(BF16)
        small4.append(t.reshape(1, N_DEV, -1, t.shape[-1]))
    small_lands = [lax.empty((len(FLIPS),) + s.shape[:1] + s.shape[2:], s.dtype) for s in small4]
    small_started = _split_start(False, small4, small_lands, dxs, name="rs_start_small")
    r_blob = _pack([Gf[n] for n in REPL], REPL_ROWS, BF16)
    r_parts = _all_gather(r_blob, name="ag_repl_grads")
    parts = {}
    for keys, tag, ssem, rsem, srcs, lands, _ in reduces:
        for kk, s, g7 in zip(keys, srcs, _split_wait(False, ssem, rsem, srcs, lands, small_started[-1],
                                                     name=f"rs_wait_{tag}")):
            parts[kk] = (s, g7)

    res = [{} for _ in range(4)]
    for n, _ in SHARDED:
        if n in SMALL_SHARDED:
            continue
        idx = [None] if w_loc[n].ndim == 2 else list(range(w_loc[n].shape[0]))
        wmv = [as_rows(n, a[n]) for a in (w_loc, m_loc, v_loc)]
        shp3 = (len(idx),) + wmv[0].shape[-2:]
        outs = _adamw_layers([parts[(n, i)][0] for i in idx], [parts[(n, i)][1] for i in idx],
                             *[a.reshape(shp3) for a in wmv], name=f"adamw_{n}")
        for k in range(4):
            res[k][n] = as_rows(n, outs[k].reshape(wmv[0].shape))
    last = res[0]["w_kv"]
    small_got = _split_wait(False, *small_started[:4], last, name="rs_wait_small")
    for n, own, g7 in zip(SMALL_SHARDED, small4, small_got):
        shp3 = own.shape[:1] + own.shape[2:]
        outs = _adamw_layers([own], [g7], w_loc[n].reshape(shp3), m_loc[n].reshape(shp3), v_loc[n].reshape(shp3),
                             name=f"adamw_{n}")
        for k in range(4):
            res[k][n] = outs[k].reshape(w_loc[n].shape)
    outs_rp = _adamw_replicated(r_parts, _pack([w_loc[n] for n in REPL], REPL_ROWS, F32),
                                _pack([m_loc[n] for n in REPL], REPL_ROWS, F32),
                                _pack([v_loc[n] for n in REPL], REPL_ROWS, F32),
                                name="adamw_replicated")
    rp_shapes = [w_loc[n].shape for n in REPL]
    for k in range(4):
        res[k].update(zip(REPL, _unpack(outs_rp[k], rp_shapes)))
    out = [loss, grad_x]
    for k in range(4):
        out += [res[k][n] for n in WEIGHTS]
    return tuple(out)
```

```python
import math

import numpy as np
import jax
import jax.numpy as jnp
from jax import lax
from jax.experimental import pallas as pl
from jax.experimental.pallas import tpu as pltpu

F32 = jnp.float32
BF16 = jnp.bfloat16
S = jax.ShapeDtypeStruct
MESH = pl.DeviceIdType.MESH
ANY = pl.BlockSpec(memory_space=pl.ANY)

HEAD = 64
MEM_HEADS = 4
MEM_W = MEM_HEADS * HEAD
SWA_HEADS = 12
SWA_GROUP = 3
MIX_W = SWA_HEADS * HEAD
WIN = 128
LRU_C = 8.0
EPS = 1e-6
ADAM_LR, ADAM_B1, ADAM_B2, ADAM_EPS, ADAM_WD, ADAM_STEP = 0.001, 0.9, 0.999, 1e-08, 0.01, 10
GELU_C0 = math.sqrt(2.0 / math.pi)
GELU_C1 = 0.044715
N_DEV = 8
LANES = 128
CT = 128
VMEM_LIMIT = 48 * 1024 * 1024
MM_VMEM_BUDGET = 36 * 1024 * 1024
REPL_ROWS = 256

SHARDED = (("w_mem_kv", 1), ("w_mix_out", 1), ("w_ffn_up", 2), ("w_ffn_conv", 2), ("w_ffn_down", 1), ("w_in_a", 2),
           ("w_conv_a", 2), ("b_conv_a", 1), ("lru_lambda", 1), ("w_in_b", 1), ("w_kv", 0))
SMALL_SHARDED = ("w_ffn_conv", "w_conv_a", "b_conv_a", "lru_lambda")
TRANSPOSED = ("w_ffn_up", "w_in_a")
REPL = ("g_mix_pre", "g_mix_post", "g_ffn_pre", "g_ffn_post", "g_mem", "b_ffn_conv", "w_rg_r", "b_rg_r", "w_rg_i",
        "b_rg_i", "sinks_b", "g_kv")
WEIGHTS = ("g_mix_pre", "g_mix_post", "g_ffn_pre", "g_ffn_post", "g_mem", "w_mem_kv", "w_mix_out", "w_ffn_up",
           "w_ffn_conv", "b_ffn_conv", "w_ffn_down", "w_in_a", "w_conv_a", "b_conv_a", "w_rg_r", "b_rg_r", "w_rg_i",
           "b_rg_i", "lru_lambda", "w_in_b", "sinks_b", "g_kv", "w_kv")


def _alibi_slopes(n):
    def pow2(m):
        start = 2.0 ** (-8.0 / m)
        return [start ** (i + 1) for i in range(m)]
    c = 2 ** int(math.floor(math.log2(n)))
    s = pow2(c)
    if c != n:
        s = s + pow2(2 * c)[0::2][: n - c]
    return [float(v) for v in np.asarray(s, dtype=np.float32)]


SLOPES = _alibi_slopes(SWA_HEADS)


def _tile(n, cands):
    for c in cands:
        if n % c == 0:
            return c
    return n


def _cparams(*sem):
    return pltpu.CompilerParams(dimension_semantics=sem, vmem_limit_bytes=VMEM_LIMIT)


def _mm_tiles(M, N, K, a_bytes, b_bytes, o_bytes, add_bytes, offsets):
    m_off, n_offs, k_off = offsets
    tms = [c for c in (1024, 512, 256, 128) if M % c == 0 and m_off % c == 0] or [M]
    tns = [c for c in (1408, 1024, 896, 768, 512, 384, 256, 128)
           if N % c == 0 and all(o % c == 0 for o in n_offs)] or [N]
    tks = [c for c in (K, 2048, 1408, 1024, 512, 256, 128) if c <= K and K % c == 0 and k_off % c == 0]
    best = None
    for tk in tks:
        fits = []
        for tm in tms:
            for tn in tns:
                need = 2 * (tm * tk * a_bytes + tk * tn * b_bytes + tm * tn * (o_bytes + add_bytes))
                need += tm * tn * 4 * (2 if tk < K else 1)
                need += (tm * tk * 2 if a_bytes != 2 else 0) + (tk * tn * 2 if b_bytes != 2 else 0)
                if need <= MM_VMEM_BUDGET:
                    fits.append((tm * tn, min(tm, 512), tm, tn))
        if fits:
            _, _, tm, tn = max(fits)
            best = (tm, tn, tk)
            break
    assert best is not None, (M, N, K)
    return best


def _mm(a, b, *, ta=False, tb=False, n=None, k=None, b_off=(0, 0), out_dtype=F32, add=None, into=None, after=None,
        name="mm"):
    if ta:
        K, M = a.shape
    else:
        M, K = a.shape
    if tb:
        N = b.shape[-2] if n is None else n
    else:
        N = b.shape[-1] if n is None else n
    assert k is None or k == K
    ro, co = b_off
    n_off, k_off = (ro, co) if tb else (co, ro)
    oro, oco = (0, 0) if into is None else into[3]
    tm, tn, tk = _mm_tiles(M, N, K, a.dtype.itemsize, b.dtype.itemsize, jnp.dtype(out_dtype).itemsize,
                           0 if add is None else add.dtype.itemsize, (oro, (n_off, oco), k_off))
    nk = K // tk
    if tb:
        b_spec = pl.BlockSpec((tn, tk), lambda i, j, kk: (j + ro // tn, kk + co // tk))
        b_dims = (1,)
    else:
        b_spec = pl.BlockSpec((tk, tn), lambda i, j, kk: (kk + ro // tk, j + co // tn))
        b_dims = (0,)
    if ta:
        a_spec = pl.BlockSpec((tk, tm), lambda i, j, kk: (kk, i))
        a_dims = (0,)
    else:
        a_spec = pl.BlockSpec((tm, tk), lambda i, j, kk: (i, kk))
        a_dims = (1,)
    dims = ((a_dims, b_dims), ((), ()))
    add_spec = pl.BlockSpec((tm, tn), lambda i, j, kk: (i, j))
    has_add = add is not None
    if into is None:
        o_spec, o_shape, buf = add_spec, (M, N), None
    else:
        buf, o_shape, ol, _ = into
        assert not has_add
        o_spec = pl.BlockSpec((None, tm, tn), lambda i, j, kk: (ol, i + oro // tm, j + oco // tn))
    has_buf = buf is not None

    def body(*refs):
        refs = list(refs)
        acc_ref = refs.pop() if nk > 1 else None
        o_ref = refs.pop()
        a_ref, b_ref = refs[0], refs[1]
        add_ref = refs[2] if has_add else None
        part = lax.dot_general(a_ref[...].astype(BF16), b_ref[...].astype(BF16), dims, preferred_element_type=F32)

        def finish(r):
            if has_add:
                r = r + add_ref[...].astype(F32)
            o_ref[...] = r.astype(out_dtype)

        if nk == 1:
            finish(part)
        else:
            kk = pl.program_id(2)

            @pl.when(kk == 0)
            def _():
                acc_ref[...] = part

            @pl.when(kk > 0)
            def _():
                acc_ref[...] += part

            @pl.when(kk == nk - 1)
            def _():
                finish(acc_ref[...])

    in_specs = [a_spec, b_spec] + ([add_spec] if has_add else []) + ([ANY] if has_buf else [])
    args = (a, b) + ((add,) if has_add else ()) + ((buf,) if has_buf else ())
    if after is not None:
        in_specs, args = in_specs + [ANY], args + (after,)
    return pl.pallas_call(
        body, grid=(M // tm, N // tn, nk), in_specs=in_specs, out_specs=o_spec,
        out_shape=S(o_shape, out_dtype), scratch_shapes=[pltpu.VMEM((tm, tn), F32)] if nk > 1 else [],
        input_output_aliases={2: 0} if has_buf else {},
        compiler_params=_cparams("parallel", "parallel", "arbitrary"), name=name)(*args)


def _mm_sum(pieces, *, tb=False, n, out_dtype=F32, add=None, after=None, name="mm_sum"):
    M = pieces[0][0].shape[0]
    ks = [a.shape[1] for a, _, _ in pieces]
    a_bytes = max(a.dtype.itemsize for a, _, _ in pieces)
    b_bytes = max(b.dtype.itemsize for _, b, _ in pieces)
    n_offs = tuple(off[0] if tb else off[1] for _, _, off in pieces)
    for kp, (_, _, off) in zip(ks, pieces):
        assert (off[1] if tb else off[0]) % kp == 0
    tm, tn, tk = _mm_tiles(M, n, sum(ks), a_bytes, b_bytes, jnp.dtype(out_dtype).itemsize, 0, (0, n_offs, 0))
    assert tk == sum(ks)
    a_specs = [pl.BlockSpec((tm, kp), lambda i, j: (i, 0)) for kp in ks]
    if tb:
        b_specs = [pl.BlockSpec((tn, kp), lambda i, j, ro=off[0], co=off[1], kp=kp: (j + ro // tn, co // kp))
                   for kp, (_, _, off) in zip(ks, pieces)]
        dims = NT
    else:
        b_specs = [pl.BlockSpec((kp, tn), lambda i, j, ro=off[0], co=off[1], kp=kp: (ro // kp, j + co // tn))
                   for kp, (_, _, off) in zip(ks, pieces)]
        dims = (((1,), (0,)), ((), ()))
    npc = len(pieces)
    o_spec = pl.BlockSpec((tm, tn), lambda i, j: (i, j))

    def body(*refs):
        o_ref = refs[2 * npc + (add is not None) + (after is not None)]
        acc = refs[2 * npc][...].astype(F32) if add is not None else None
        for p in range(npc):
            part = lax.dot_general(refs[p][...].astype(BF16), refs[npc + p][...].astype(BF16), dims,
                                   preferred_element_type=F32)
            acc = part if acc is None else acc + part
        o_ref[...] = acc.astype(out_dtype)

    args = [a for a, _, _ in pieces] + [b for _, b, _ in pieces]
    in_specs = a_specs + b_specs
    if add is not None:
        in_specs, args = in_specs + [o_spec], args + [add]
    if after is not None:
        in_specs, args = in_specs + [ANY], args + [after]
    return pl.pallas_call(
        body, grid=(M // tm, n // tn), in_specs=in_specs, out_specs=o_spec,
        out_shape=S((M, n), out_dtype), compiler_params=_cparams("parallel", "parallel"), name=name)(*args)


def _rms_fwd(x, g, out_dtype, res=None, name="rms_fwd"):
    N, D = x.shape
    tm = _tile(N, (512, 256, 128))
    has_res = res is not None

    def body(*refs):
        if has_res:
            x_ref, g_ref, r_ref, o_ref = refs
        else:
            x_ref, g_ref, o_ref = refs
        xv = x_ref[...].astype(F32)
        y = xv * lax.rsqrt(jnp.mean(xv * xv, axis=-1, keepdims=True) + EPS) * g_ref[...]
        if has_res:
            y = y + r_ref[...]
        o_ref[...] = y.astype(out_dtype)

    row = pl.BlockSpec((tm, D), lambda i: (i, 0))
    vec = pl.BlockSpec((1, D), lambda i: (0, 0))
    return pl.pallas_call(
        body, grid=(N // tm,), in_specs=[row, vec] + ([row] if has_res else []), out_specs=row,
        out_shape=S((N, D), out_dtype), compiler_params=_cparams("parallel"), name=name)(
            *((x, g) + ((res,) if has_res else ())))


def _rms_bwd(x, g, dy, add=None, out_dtype=F32, name="rms_bwd"):
    N, D = x.shape
    tm = _tile(N, (512, 256, 128))
    has_add = add is not None

    def body(*refs):
        if has_add:
            x_ref, g_ref, dy_ref, add_ref, dx_ref, dg_ref = refs
        else:
            x_ref, g_ref, dy_ref, dx_ref, dg_ref = refs
        xv = x_ref[...].astype(F32)
        dyv = dy_ref[...].astype(F32)
        r = lax.rsqrt(jnp.mean(xv * xv, axis=-1, keepdims=True) + EPS)
        u = dyv * g_ref[...]
        dx = r * u - xv * (r * r * r * jnp.mean(u * xv, axis=-1, keepdims=True))
        if has_add:
            dx = dx + add_ref[...]
        dx_ref[...] = dx.astype(out_dtype)

        @pl.when(pl.program_id(0) == 0)
        def _():
            dg_ref[...] = jnp.zeros_like(dg_ref)

        dg_ref[...] += jnp.sum(dyv * xv * r, axis=0, keepdims=True)

    row = pl.BlockSpec((tm, D), lambda i: (i, 0))
    vec = pl.BlockSpec((1, D), lambda i: (0, 0))
    return pl.pallas_call(
        body, grid=(N // tm,), in_specs=[row, vec, row] + ([row] if has_add else []), out_specs=(row, vec),
        out_shape=(S((N, D), out_dtype), S((1, D), F32)), compiler_params=_cparams("arbitrary"), name=name)(
            *((x, g, dy) + ((add,) if has_add else ())))


def _rms_pair_fwd(y, g_post, res, gains, name):
    N, D = y.shape
    tm = _tile(N, (512, 256, 128))
    ng = len(gains)

    def body(*refs):
        y_ref, gp_ref, r_ref = refs[:3]
        g_refs = refs[3:3 + ng]
        x_ref = refs[3 + ng]
        h_refs = refs[4 + ng:]
        yv = y_ref[...]
        x = r_ref[...] + yv * lax.rsqrt(jnp.mean(yv * yv, axis=-1, keepdims=True) + EPS) * gp_ref[...]
        x_ref[...] = x
        xn = x * lax.rsqrt(jnp.mean(x * x, axis=-1, keepdims=True) + EPS)
        for g_ref, h_ref in zip(g_refs, h_refs):
            h_ref[...] = (xn * g_ref[...]).astype(BF16)

    row = pl.BlockSpec((tm, D), lambda i: (i, 0))
    vec = pl.BlockSpec((1, D), lambda i: (0, 0))
    return pl.pallas_call(
        body, grid=(N // tm,), in_specs=[row, vec, row] + [vec] * ng, out_specs=(row,) * (1 + ng),
        out_shape=(S((N, D), F32),) + (S((N, D), BF16),) * ng, compiler_params=_cparams("parallel"), name=name)(
            y, g_post, res, *gains)


def _rms_pair_bwd(xa, ga, dya, add, xb, gb, name):
    N, D = xa.shape
    tm = _tile(N, (512, 256, 128))

    def one(xv, g_ref, dyv):
        r = lax.rsqrt(jnp.mean(xv * xv, axis=-1, keepdims=True) + EPS)
        u = dyv * g_ref[...]
        dx = r * u - xv * (r * r * r * jnp.mean(u * xv, axis=-1, keepdims=True))
        return dx, jnp.sum(dyv * xv * r, axis=0, keepdims=True)

    def body(xa_ref, ga_ref, dya_ref, add_ref, xb_ref, gb_ref, da_ref, db_ref, dga_ref, dgb_ref):
        da, dga = one(xa_ref[...].astype(F32), ga_ref, dya_ref[...].astype(F32))
        da = da + add_ref[...]
        da_ref[...] = da
        db, dgb = one(xb_ref[...].astype(F32), gb_ref, da)
        db_ref[...] = db.astype(BF16)

        @pl.when(pl.program_id(0) == 0)
        def _():
            dga_ref[...] = jnp.zeros_like(dga_ref)
            dgb_ref[...] = jnp.zeros_like(dgb_ref)

        dga_ref[...] += dga
        dgb_ref[...] += dgb

    row = pl.BlockSpec((tm, D), lambda i: (i, 0))
    vec = pl.BlockSpec((1, D), lambda i: (0, 0))
    return pl.pallas_call(
        body, grid=(N // tm,), in_specs=[row, vec, row, row, row, vec], out_specs=(row, row, vec, vec),
        out_shape=(S((N, D), F32), S((N, D), BF16), S((1, D), F32), S((1, D), F32)),
        compiler_params=_cparams("arbitrary"), name=name)(xa, ga, dya, add, xb, gb)


def _shift_down(x, s, row):
    return jnp.where(row >= s, pltpu.roll(x, s, axis=0), 0.0)


def _shift_up(x, s, row):
    T = x.shape[0]
    return jnp.where(row < T - s, pltpu.roll(x, T - s, axis=0), 0.0)


SLAB = 16


def _conv_rows(x_ref, w_ref, b_ref, lo, hi):
    W = w_ref.shape[0]
    y = x_ref[lo:hi, :] * w_ref[W - 1:W, :] + b_ref[...]
    for s in range(1, W):
        y = y + x_ref[lo - s:hi - s, :] * w_ref[W - 1 - s:W - s, :]
    return y


def _taps(x_ref, W):
    T = x_ref.shape[0]
    head = x_ref[0:SLAB, :]
    row = lax.broadcasted_iota(jnp.int32, head.shape, 0)
    return [x_ref[...]] + [jnp.concatenate([_shift_down(head, s, row), x_ref[SLAB - s:T - s, :]], axis=0)
                           for s in range(1, W)]


def _conv_taps(xs, w_ref, b_ref):
    W = w_ref.shape[0]
    y = xs[0] * w_ref[W - 1:W, :] + b_ref[...]
    for s in range(1, W):
        y = y + xs[s] * w_ref[W - 1 - s:W - s, :]
    return y


def _conv_head(x_head, w_ref, b_ref):
    row = lax.broadcasted_iota(jnp.int32, x_head.shape, 0)
    return _conv_taps([x_head] + [_shift_down(x_head, s, row) for s in range(1, w_ref.shape[0])], w_ref, b_ref)


def _conv_bwd_taps(dy, xs, w_ref, row):
    W = w_ref.shape[0]
    dx = dy * w_ref[W - 1:W, :]
    dws = [None] * W
    dws[W - 1] = jnp.sum(dy * xs[0], axis=0, keepdims=True)
    for s in range(1, W):
        dx = dx + _shift_up(dy, s, row) * w_ref[W - 1 - s:W - s, :]
        dws[W - 1 - s] = jnp.sum(dy * xs[s], axis=0, keepdims=True)
    return dx, jnp.concatenate(dws, axis=0), jnp.sum(dy, axis=0, keepdims=True)


def _gelu(g):
    t = jnp.tanh(GELU_C0 * (g + GELU_C1 * g * g * g))
    return 0.5 * g * (1.0 + t), t


def _dgelu(g, t):
    return 0.5 * (1.0 + t) + 0.5 * g * (1.0 - t * t) * (GELU_C0 * (1.0 + 3.0 * GELU_C1 * g * g))


def _cspec(T, off=0, ct=CT):
    return pl.BlockSpec((1, T, ct), lambda j, b: (b, 0, j + off))


def _pspec(rows, off=0, ct=CT):
    return pl.BlockSpec((rows, ct), lambda j, b: (0, j + off))


def _conv_fwd_call(x3, x_off, C, w, b, name):
    Bl, T, _ = x3.shape
    W = w.shape[0]

    def body(x_ref, w_ref, b_ref, o_ref):
        o_ref[0, SLAB:T, :] = _conv_rows(x_ref.at[0], w_ref, b_ref, SLAB, T)
        o_ref[0, 0:SLAB, :] = _conv_head(x_ref[0, 0:SLAB, :], w_ref, b_ref)

    return pl.pallas_call(
        body, grid=(C // CT, Bl), in_specs=[_cspec(T, x_off // CT), _pspec(W), _pspec(1)], out_specs=_cspec(T),
        out_shape=S((Bl, T, C), F32), compiler_params=_cparams("parallel", "arbitrary"), name=name)(x3, w, b)


def _conv_bwd_call(dy3, x3, x_off, C, w, name):
    Bl, T, _ = x3.shape
    W = w.shape[0]

    def body(dy_ref, x_ref, w_ref, dx_ref, dw_ref, db_ref):
        row = lax.broadcasted_iota(jnp.int32, (T, CT), 0)
        dx, dw, db = _conv_bwd_taps(dy_ref[0], _taps(x_ref.at[0], W), w_ref, row)
        dx_ref[0] = dx.astype(BF16)

        @pl.when(pl.program_id(1) == 0)
        def _():
            dw_ref[...] = jnp.zeros_like(dw_ref)
            db_ref[...] = jnp.zeros_like(db_ref)

        dw_ref[...] += dw
        db_ref[...] += db

    return pl.pallas_call(
        body, grid=(C // CT, Bl), in_specs=[_cspec(T), _cspec(T, x_off // CT), _pspec(W)],
        out_specs=(_cspec(T), _pspec(W), _pspec(1)),
        out_shape=(S((Bl, T, C), BF16), S((W, C), F32), S((1, C), F32)),
        compiler_params=_cparams("parallel", "arbitrary"), name=name)(dy3, x3, w)


def _ffn_mid_fwd(u3, wc, bc, name):
    Bl, T, F2 = u3.shape
    F = F2 // 2
    nf = F // CT

    def body(ug_ref, uv_ref, wg_ref, wv_ref, bg_ref, bv_ref, o_ref):
        g = _conv_rows(ug_ref.at[0], wg_ref, bg_ref, SLAB, T)
        v = _conv_rows(uv_ref.at[0], wv_ref, bv_ref, SLAB, T)
        o_ref[0, SLAB:T, :] = (_gelu(g)[0] * v).astype(BF16)
        g = _conv_head(ug_ref[0, 0:SLAB, :], wg_ref, bg_ref)
        v = _conv_head(uv_ref[0, 0:SLAB, :], wv_ref, bv_ref)
        o_ref[0, 0:SLAB, :] = (_gelu(g)[0] * v).astype(BF16)

    return pl.pallas_call(
        body, grid=(nf, Bl),
        in_specs=[_cspec(T), _cspec(T, nf), _pspec(3), _pspec(3, nf), _pspec(1), _pspec(1, nf)], out_specs=_cspec(T),
        out_shape=S((Bl, T, F), BF16), compiler_params=_cparams("parallel", "arbitrary"), name=name)(
            u3, u3, wc, wc, bc, bc)


def _ffn_mid_bwd(u3, dact3, wc, bc, name):
    Bl, T, F2 = u3.shape
    F = F2 // 2
    nf = F // CT

    def body(ug_ref, uv_ref, da_ref, wg_ref, wv_ref, bg_ref, bv_ref, dug_ref, duv_ref, dwg_ref, dwv_ref, dbg_ref,
             dbv_ref):
        row = lax.broadcasted_iota(jnp.int32, (T, CT), 0)
        ugs = _taps(ug_ref.at[0], 3)
        uvs = _taps(uv_ref.at[0], 3)
        g = _conv_taps(ugs, wg_ref, bg_ref)
        v = _conv_taps(uvs, wv_ref, bv_ref)
        da = da_ref[0]
        gel, t = _gelu(g)
        dg = da * v * _dgelu(g, t)
        dv = da * gel
        dug, dwg, dbg = _conv_bwd_taps(dg, ugs, wg_ref, row)
        duv, dwv, dbv = _conv_bwd_taps(dv, uvs, wv_ref, row)
        dug_ref[0] = dug.astype(BF16)
        duv_ref[0] = duv.astype(BF16)

        @pl.when(pl.program_id(1) == 0)
        def _():
            dwg_ref[...] = jnp.zeros_like(dwg_ref)
            dwv_ref[...] = jnp.zeros_like(dwv_ref)
            dbg_ref[...] = jnp.zeros_like(dbg_ref)
            dbv_ref[...] = jnp.zeros_like(dbv_ref)

        dwg_ref[...] += dwg
        dwv_ref[...] += dwv
        dbg_ref[...] += dbg
        dbv_ref[...] += dbv

    return pl.pallas_call(
        body, grid=(nf, Bl),
        in_specs=[_cspec(T), _cspec(T, nf), _cspec(T), _pspec(3), _pspec(3, nf), _pspec(1), _pspec(1, nf)],
        out_specs=(_cspec(T), _cspec(T), _pspec(3), _pspec(3), _pspec(1), _pspec(1)),
        out_shape=(S((Bl, T, F), BF16), S((Bl, T, F), BF16), S((3, F), F32), S((3, F), F32), S((1, F), F32),
                   S((1, F), F32)),
        compiler_params=_cparams("parallel", "arbitrary"), name=name)(u3, u3, dact3, wc, wc, bc, bc)


def _lru_gates(xc, rp, ip, br_ref, bi_ref, lam_ref):
    r = jax.nn.sigmoid(rp + br_ref[...])
    i = jax.nn.sigmoid(ip + bi_ref[...])
    lam = lam_ref[...]
    sp = jnp.maximum(-lam, 0.0) + jnp.log1p(jnp.exp(-jnp.abs(lam)))
    log_a = (-LRU_C) * r * sp
    a = jnp.exp(log_a)
    z = 2.0 * log_a
    one_m_a2 = jnp.where(z > -0.05, -z * (1.0 + z * (0.5 + z * (1.0 / 6.0 + z * (1.0 / 24.0)))), 1.0 - a * a)
    mult = jnp.sqrt(one_m_a2)
    return r, i, sp, a, mult


SCAN_CHUNK = 64


def _scan_down(a, b):
    T = a.shape[0]
    ch = min(SCAN_CHUNK, T)
    row = lax.broadcasted_iota(jnp.int32, (ch, a.shape[1]), 0)
    outs, carry = [], None
    for c in range(T // ch):
        ac, bc = a[c * ch:(c + 1) * ch], b[c * ch:(c + 1) * ch]
        s = 1
        while s < ch:
            a_sh = jnp.where(row >= s, pltpu.roll(ac, s, axis=0), 1.0)
            bc = ac * _shift_down(bc, s, row) + bc
            ac = ac * a_sh
            s *= 2
        if carry is not None:
            bc = bc + ac * carry
        carry = bc[ch - 1:ch, :]
        outs.append(bc)
    return jnp.concatenate(outs, axis=0)


def _scan_up(an, g):
    T = an.shape[0]
    ch = min(SCAN_CHUNK, T)
    row = lax.broadcasted_iota(jnp.int32, (ch, an.shape[1]), 0)
    outs, carry = [], None
    for c in reversed(range(T // ch)):
        ac, gc = an[c * ch:(c + 1) * ch], g[c * ch:(c + 1) * ch]
        s = 1
        while s < ch:
            a_sh = jnp.where(row < ch - s, pltpu.roll(ac, ch - s, axis=0), 1.0)
            gc = ac * _shift_up(gc, s, row) + gc
            ac = ac * a_sh
            s *= 2
        if carry is not None:
            gc = gc + ac * carry
        carry = gc[0:1, :]
        outs.append(gc)
    return jnp.concatenate(outs[::-1], axis=0)


def _rglru_fwd(xc3, gates3, proj3, br, bi, lam, name):
    Bl, T, C = xc3.shape

    def body(xc_ref, rp_ref, ip_ref, ug_ref, br_ref, bi_ref, lam_ref, y_ref, h_ref):
        xc = xc_ref[0]
        r, i, sp, a, mult = _lru_gates(xc, rp_ref[0], ip_ref[0], br_ref, bi_ref, lam_ref)
        h = _scan_down(a, mult * (i * xc))
        h_ref[0] = h
        y_ref[0] = (h * _gelu(ug_ref[0])[0]).astype(BF16)

    return pl.pallas_call(
        body, grid=(C // CT, Bl),
        in_specs=[_cspec(T), _cspec(T), _cspec(T, C // CT), _cspec(T), _pspec(1), _pspec(1), _pspec(1)],
        out_specs=(_cspec(T), _cspec(T)), out_shape=(S((Bl, T, C), BF16), S((Bl, T, C), F32)),
        compiler_params=_cparams("parallel", "arbitrary"), name=name)(xc3, gates3, gates3, proj3, br, bi, lam)


def _rglru_bwd(dy3, xc3, gates3, proj3, h3, br, bi, lam, name):
    Bl, T, C = xc3.shape

    def body(dy_ref, xc_ref, rp_ref, ip_ref, ug_ref, h_ref, br_ref, bi_ref, lam_ref,
             dxc_ref, drp_ref, dip_ref, dug_ref, dbr_ref, dbi_ref, dlam_ref):
        row = lax.broadcasted_iota(jnp.int32, (T, CT), 0)
        xc = xc_ref[0]
        r, i, sp, a, mult = _lru_gates(xc, rp_ref[0], ip_ref[0], br_ref, bi_ref, lam_ref)
        h = h_ref[0]
        dy = dy_ref[0]
        ug = ug_ref[0]
        gel, t = _gelu(ug)
        dug_ref[0] = (dy * h * _dgelu(ug, t)).astype(BF16)
        gacc = _scan_up(_shift_up(a, 1, row), dy * gel)
        da = gacc * _shift_down(h, 1, row)
        ix = i * xc
        d_mult = gacc * ix
        d_i = gacc * mult * xc
        dxc_ref[0] = gacc * mult * i
        d_log_a = da * a - d_mult * (a * a) / mult
        d_r = d_log_a * ((-LRU_C) * sp)
        d_sp = jnp.sum(d_log_a * ((-LRU_C) * r), axis=0, keepdims=True)
        drp = d_r * r * (1.0 - r)
        dip = d_i * i * (1.0 - i)
        drp_ref[0] = drp.astype(BF16)
        dip_ref[0] = dip.astype(BF16)

        @pl.when(pl.program_id(1) == 0)
        def _():
            dbr_ref[...] = jnp.zeros_like(dbr_ref)
            dbi_ref[...] = jnp.zeros_like(dbi_ref)
            dlam_ref[...] = jnp.zeros_like(dlam_ref)

        dbr_ref[...] += jnp.sum(drp, axis=0, keepdims=True)
        dbi_ref[...] += jnp.sum(dip, axis=0, keepdims=True)
        dlam_ref[...] += d_sp * (-jax.nn.sigmoid(-lam_ref[...]))

    vec = S((1, C), F32)
    act = S((Bl, T, C), BF16)
    return pl.pallas_call(
        body, grid=(C // CT, Bl),
        in_specs=[_cspec(T), _cspec(T), _cspec(T), _cspec(T, C // CT), _cspec(T), _cspec(T)] + [_pspec(1)] * 3,
        out_specs=(_cspec(T), _cspec(T), _cspec(T), _cspec(T), _pspec(1), _pspec(1), _pspec(1)),
        out_shape=(S((Bl, T, C), F32), act, act, act, vec, vec, vec),
        compiler_params=_cparams("parallel", "arbitrary"), name=name)(dy3, xc3, gates3, gates3, proj3, h3, br, bi, lam)


NT = (((1,), (1,)), ((), ()))
TN = (((0,), (0,)), ((), ()))


def _hs(h):
    return slice(h * HEAD, (h + 1) * HEAD)


def _head_rows(x):
    head = lax.shift_right_logical(lax.broadcasted_iota(jnp.int32, x.shape, 1), HEAD.bit_length() - 1)
    return jnp.concatenate([jnp.where(head == h, x, jnp.zeros_like(x)) for h in range(MEM_HEADS)], axis=0)


def _head_sum(xbd):
    M = xbd.shape[0] // MEM_HEADS
    head = lax.shift_right_logical(lax.broadcasted_iota(jnp.int32, (M, xbd.shape[1]), 1), HEAD.bit_length() - 1)
    out = jnp.zeros((M, xbd.shape[1]), xbd.dtype)
    for h in range(MEM_HEADS):
        out = jnp.where(head == h, xbd[h * M:(h + 1) * M], out)
    return out


def _mem_probs(q, kbd):
    M = kbd.shape[0] // MEM_HEADS
    s = lax.dot_general(q, kbd, NT, preferred_element_type=F32) * (HEAD ** -0.5)
    ps = []
    for h in range(MEM_HEADS):
        sh = s[:, h * M:(h + 1) * M]
        e = jnp.exp(sh - jnp.max(sh, axis=-1, keepdims=True))
        ps.append(e / jnp.sum(e, axis=-1, keepdims=True))
    return ps


def _mem_attn_fwd(proj3, q_off, mkv3, name):
    Bl, T, _ = proj3.shape
    M = mkv3.shape[1]
    tq = _tile(T, (512, 256, 128))

    def body(q_ref, k_ref, v_ref, o_ref):
        q = q_ref[0].astype(BF16)
        kbd = _head_rows(k_ref[0].astype(BF16))
        vbd = _head_rows(v_ref[0].astype(BF16))
        p = jnp.concatenate(_mem_probs(q, kbd), axis=-1).astype(BF16)
        o_ref[0] = jnp.dot(p, vbd, preferred_element_type=F32).astype(BF16)

    return pl.pallas_call(
        body, grid=(Bl, T // tq),
        in_specs=[pl.BlockSpec((1, tq, MEM_W), lambda b, t: (b, t, q_off // MEM_W)),
                  pl.BlockSpec((1, M, MEM_W), lambda b, t: (b, 0, 0)),
                  pl.BlockSpec((1, M, MEM_W), lambda b, t: (b, 0, 1))],
        out_specs=pl.BlockSpec((1, tq, MEM_W), lambda b, t: (b, t, 0)),
        out_shape=S((Bl, T, MEM_W), BF16), compiler_params=_cparams("parallel", "parallel"), name=name)(
            proj3, mkv3, mkv3)


def _mem_attn_bwd(proj3, q_off, mkv3, do3, name):
    Bl, T, _ = proj3.shape
    M = mkv3.shape[1]
    tq = _tile(T, (512, 256, 128))
    scale = HEAD ** -0.5

    def body(q_ref, k_ref, v_ref, do_ref, dq_ref, dkv_ref):
        q = q_ref[0].astype(BF16)
        kbd = _head_rows(k_ref[0].astype(BF16))
        vbd = _head_rows(v_ref[0].astype(BF16))
        do = do_ref[0].astype(BF16)
        ps = _mem_probs(q, kbd)
        dvbd = lax.dot_general(jnp.concatenate(ps, axis=-1).astype(BF16), do, TN, preferred_element_type=F32)
        dp = lax.dot_general(do, vbd, NT, preferred_element_type=F32)
        dss = []
        for h in range(MEM_HEADS):
            dph = dp[:, h * M:(h + 1) * M]
            dss.append(ps[h] * (dph - jnp.sum(ps[h] * dph, axis=-1, keepdims=True)) * scale)
        ds = jnp.concatenate(dss, axis=-1).astype(BF16)
        dq_ref[0] = jnp.dot(ds, kbd, preferred_element_type=F32).astype(BF16)
        dkbd = lax.dot_general(ds, q, TN, preferred_element_type=F32)

        @pl.when(pl.program_id(1) == 0)
        def _():
            dkv_ref[...] = jnp.zeros_like(dkv_ref)

        dkv_ref[0] += jnp.concatenate([_head_sum(dkbd), _head_sum(dvbd)], axis=-1)

    return pl.pallas_call(
        body, grid=(Bl, T // tq),
        in_specs=[pl.BlockSpec((1, tq, MEM_W), lambda b, t: (b, t, q_off // MEM_W)),
                  pl.BlockSpec((1, M, MEM_W), lambda b, t: (b, 0, 0)),
                  pl.BlockSpec((1, M, MEM_W), lambda b, t: (b, 0, 1)),
                  pl.BlockSpec((1, tq, MEM_W), lambda b, t: (b, t, 0))],
        out_specs=(pl.BlockSpec((1, tq, MEM_W), lambda b, t: (b, t, 0)),
                   pl.BlockSpec((1, M, 2 * MEM_W), lambda b, t: (b, 0, 0))),
        out_shape=(S((Bl, T, MEM_W), BF16), S((Bl, M, 2 * MEM_W), F32)),
        compiler_params=_cparams("parallel", "arbitrary"), name=name)(proj3, mkv3, mkv3, do3)


GROUP_ROWS = SWA_GROUP * WIN


def _group_rows(x, kvh):
    return jnp.concatenate([x[:, _hs(SWA_GROUP * kvh + g)] for g in range(SWA_GROUP)], axis=0)


def _group_col(vals):
    grp = lax.shift_right_logical(lax.broadcasted_iota(jnp.int32, (GROUP_ROWS, 1), 0), WIN.bit_length() - 1)
    col = jnp.full((GROUP_ROWS, 1), vals[-1], F32)
    for g in range(SWA_GROUP - 2, -1, -1):
        col = jnp.where(grp == g, vals[g], col)
    return col


def _swa_probs(qh, kph, kch, sink, slope, has_prev):
    qi = jnp.bitwise_and(lax.broadcasted_iota(jnp.int32, (GROUP_ROWS, WIN), 0), WIN - 1)
    kj = lax.broadcasted_iota(jnp.int32, (GROUP_ROWS, WIN), 1)
    scale = HEAD ** -0.5
    sp = lax.dot_general(qh, kph, NT, preferred_element_type=F32) * scale
    sc = lax.dot_general(qh, kch, NT, preferred_element_type=F32) * scale
    dist_p = (qi + WIN - kj).astype(F32)
    dist_c = (qi - kj).astype(F32)
    neg = -jnp.inf
    sp = jnp.where(kj > qi + jnp.where(has_prev, 0, WIN), sp - slope * dist_p, neg)
    sc = jnp.where(kj <= qi, sc - slope * dist_c, neg)
    m = jnp.maximum(jnp.maximum(jnp.max(sp, axis=-1, keepdims=True), jnp.max(sc, axis=-1, keepdims=True)), sink)
    ep = jnp.exp(sp - m)
    ec = jnp.exp(sc - m)
    es = jnp.exp(sink - m)
    inv = 1.0 / (jnp.sum(ep, axis=-1, keepdims=True) + jnp.sum(ec, axis=-1, keepdims=True) + es)
    return ep * inv, ec * inv, es * inv


def _swa_specs(nb):
    prev = lambda n: jnp.maximum(n - 1, 0)
    q = pl.BlockSpec((1, WIN, MIX_W), lambda b, n: (b, n, 0))
    kp = pl.BlockSpec((1, WIN, MEM_W), lambda b, n: (b, prev(n), 0))
    kc = pl.BlockSpec((1, WIN, MEM_W), lambda b, n: (b, n, 0))
    vp = pl.BlockSpec((1, WIN, MEM_W), lambda b, n: (b, prev(n), 1))
    vc = pl.BlockSpec((1, WIN, MEM_W), lambda b, n: (b, n, 1))
    sm = pl.BlockSpec(memory_space=pltpu.SMEM)
    return q, kp, kc, vp, vc, sm


def _swa_fwd(proj3, kv3, sinks, name):
    Bl, T, _ = proj3.shape
    nb = T // WIN
    q_s, kp_s, kc_s, vp_s, vc_s, sm = _swa_specs(nb)

    def body(q_ref, kp_ref, kc_ref, vp_ref, vc_ref, sink_ref, o_ref):
        has_prev = pl.program_id(1) > 0
        q = q_ref[0].astype(BF16)
        kp, kc = kp_ref[0].astype(BF16), kc_ref[0].astype(BF16)
        vp, vc = vp_ref[0].astype(BF16), vc_ref[0].astype(BF16)
        outs = []
        for kvh in range(SWA_HEADS // SWA_GROUP):
            kvs = _hs(kvh)
            heads = range(SWA_GROUP * kvh, SWA_GROUP * (kvh + 1))
            pp, pc, _ = _swa_probs(_group_rows(q, kvh), kp[:, kvs], kc[:, kvs], _group_col([sink_ref[h] for h in heads]),
                                   _group_col([SLOPES[h] for h in heads]), has_prev)
            og = (jnp.dot(pp.astype(BF16), vp[:, kvs], preferred_element_type=F32)
                  + jnp.dot(pc.astype(BF16), vc[:, kvs], preferred_element_type=F32))
            outs += [og[g * WIN:(g + 1) * WIN] for g in range(SWA_GROUP)]
        o_ref[0] = jnp.concatenate(outs, axis=-1).astype(BF16)

    return pl.pallas_call(
        body, grid=(Bl, nb), in_specs=[q_s, kp_s, kc_s, vp_s, vc_s, sm], out_specs=q_s,
        out_shape=S((Bl, T, MIX_W), BF16), compiler_params=_cparams("parallel", "parallel"), name=name)(
            proj3, kv3, kv3, kv3, kv3, sinks)


def _swa_bwd(proj3, kv3, sinks, do3, name):
    Bl, T, _ = proj3.shape
    nb = T // WIN
    q_s, kp_s, kc_s, vp_s, vc_s, sm = _swa_specs(nb)
    kv_s = pl.BlockSpec((1, WIN, 2 * MEM_W), lambda b, n: (b, n, 0))
    sk_s = pl.BlockSpec((8, LANES), lambda b, n: (0, 0))
    scale = HEAD ** -0.5

    def body(q_ref, kp_ref, kc_ref, vp_ref, vc_ref, sink_ref, do_ref, dq_ref, dkc_ref, dkp_ref, dsk_ref):
        has_prev = pl.program_id(1) > 0
        q = q_ref[0].astype(BF16)
        kp, kc = kp_ref[0].astype(BF16), kc_ref[0].astype(BF16)
        vp, vc = vp_ref[0].astype(BF16), vc_ref[0].astype(BF16)
        do = do_ref[0].astype(BF16)
        lane = lax.broadcasted_iota(jnp.int32, (8, LANES), 1)
        srow = lax.broadcasted_iota(jnp.int32, (8, LANES), 0)
        dsk = jnp.zeros((8, LANES), F32)
        dqs = []
        dkc, dkp, dvc, dvp = [], [], [], []
        grp = lax.shift_right_logical(lax.broadcasted_iota(jnp.int32, (GROUP_ROWS, 1), 0), WIN.bit_length() - 1)
        for kvh in range(SWA_HEADS // SWA_GROUP):
            kvs = _hs(kvh)
            heads = range(SWA_GROUP * kvh, SWA_GROUP * (kvh + 1))
            qg, dog = _group_rows(q, kvh), _group_rows(do, kvh)
            pp, pc, ps = _swa_probs(qg, kp[:, kvs], kc[:, kvs], _group_col([sink_ref[h] for h in heads]),
                                    _group_col([SLOPES[h] for h in heads]), has_prev)
            dpp = lax.dot_general(dog, vp[:, kvs], NT, preferred_element_type=F32)
            dpc = lax.dot_general(dog, vc[:, kvs], NT, preferred_element_type=F32)
            delta = jnp.sum(pp * dpp, axis=-1, keepdims=True) + jnp.sum(pc * dpc, axis=-1, keepdims=True)
            dsp = (pp * (dpp - delta) * scale).astype(BF16)
            dsc = (pc * (dpc - delta) * scale).astype(BF16)
            dqg = (jnp.dot(dsp, kp[:, kvs], preferred_element_type=F32)
                   + jnp.dot(dsc, kc[:, kvs], preferred_element_type=F32))
            dqs += [dqg[g * WIN:(g + 1) * WIN] for g in range(SWA_GROUP)]
            dkc.append(lax.dot_general(dsc, qg, TN, preferred_element_type=F32))
            dkp.append(lax.dot_general(dsp, qg, TN, preferred_element_type=F32))
            dvc.append(lax.dot_general(pc.astype(BF16), dog, TN, preferred_element_type=F32))
            dvp.append(lax.dot_general(pp.astype(BF16), dog, TN, preferred_element_type=F32))
            dsink = ps * delta
            for g, h in enumerate(heads):
                dsk = dsk + jnp.where((lane == h) & (srow == 0), -jnp.sum(jnp.where(grp == g, dsink, 0.0)), 0.0)
        dq_ref[0] = jnp.concatenate(dqs, axis=-1).astype(BF16)
        dkc_ref[0] = jnp.concatenate(dkc + dvc, axis=-1)
        dkp_ref[0] = jnp.concatenate(dkp + dvp, axis=-1)

        @pl.when((pl.program_id(0) == 0) & (pl.program_id(1) == 0))
        def _():
            dsk_ref[...] = jnp.zeros_like(dsk_ref)

        dsk_ref[...] += dsk

    return pl.pallas_call(
        body, grid=(Bl, nb), in_specs=[q_s, kp_s, kc_s, vp_s, vc_s, sm, q_s], out_specs=(q_s, kv_s, kv_s, sk_s),
        out_shape=(S((Bl, T, MIX_W), BF16), S((Bl, T, 2 * MEM_W), F32), S((Bl, T, 2 * MEM_W), F32), S((8, LANES), F32)),
        compiler_params=_cparams("arbitrary", "arbitrary"), name=name)(proj3, kv3, kv3, kv3, kv3, sinks, do3)


def _kv_grad_combine(parts, name):
    Bl, T, W = parts[0][0].shape
    nb = T // WIN
    nl = len(parts)

    def body(*refs):
        o_ref = refs[-1]
        has_next = jnp.where(pl.program_id(1) == nb - 1, 0.0, 1.0)
        acc = None
        for l in range(nl):
            c = refs[2 * l][0] + has_next * refs[2 * l + 1][0]
            acc = c if acc is None else acc + c
        o_ref[0] = acc.astype(BF16)

    cur = pl.BlockSpec((1, WIN, W), lambda b, n: (b, n, 0))
    nxt = pl.BlockSpec((1, WIN, W), lambda b, n: (b, jnp.minimum(n + 1, nb - 1), 0))
    return pl.pallas_call(
        body, grid=(Bl, nb), in_specs=[cur, nxt] * nl, out_specs=cur, out_shape=S((Bl, T, W), BF16),
        compiler_params=_cparams("parallel", "parallel"), name=name)(*[a for pr in parts for a in pr])


def _head_turnaround(f, g_post, res, target, name="loss"):
    N, D = f.shape
    tm = _tile(N, (512, 256, 128))

    def body(f_ref, g_ref, r_ref, t_ref, dx_ref, df_ref, l_ref, dg_ref):
        fv = f_ref[...]
        r = lax.rsqrt(jnp.mean(fv * fv, axis=-1, keepdims=True) + EPS)
        e = r_ref[...] + fv * r * g_ref[...] - t_ref[...]
        dx = e * (1.0 / D)
        dx_ref[...] = dx
        u = dx * g_ref[...]
        df_ref[...] = (r * u - fv * (r * r * r * jnp.mean(u * fv, axis=-1, keepdims=True))).astype(BF16)

        @pl.when(pl.program_id(0) == 0)
        def _():
            l_ref[...] = jnp.zeros_like(l_ref)
            dg_ref[...] = jnp.zeros_like(dg_ref)

        l_ref[...] += jnp.sum(e * e, axis=0, keepdims=True) * (0.5 / D)
        dg_ref[...] += jnp.sum(dx * fv * r, axis=0, keepdims=True)

    row = pl.BlockSpec((tm, D), lambda i: (i, 0))
    vec = pl.BlockSpec((1, D), lambda i: (0, 0))
    return pl.pallas_call(
        body, grid=(N // tm,), in_specs=[row, vec, row, row], out_specs=(row, row, vec, vec),
        out_shape=(S((N, D), F32), S((N, D), BF16), S((1, D), F32), S((1, D), F32)),
        compiler_params=_cparams("arbitrary"), name=name)(f, g_post, res, target)


def _all_gather(x, name):
    R, C = x.shape

    def body(x_ref, out_ref, send_sems, recv_sems, local_sem):
        mx, my, mc = lax.axis_index("x"), lax.axis_index("y"), lax.axis_index("c")
        me, sibling = (mx, my, mc), (mx, my, 1 - mc)
        chips = [(1 - mx, my), (mx, 1 - my), (1 - mx, 1 - my)]

        def rows(px, py, pc):
            return out_ref.at[4 * px + 2 * py + pc]

        def copy(kk, block, to, src=None):
            return pltpu.make_async_remote_copy(
                src_ref=rows(*block) if src is None else src, dst_ref=rows(*block), send_sem=send_sems.at[kk],
                recv_sem=recv_sems.at[kk], device_id=to, device_id_type=MESH)

        mine = pltpu.make_async_copy(x_ref, rows(*me), local_sem)
        mine.start()
        first = [copy(0, me, sibling, src=x_ref)]
        first += [copy(1 + j, me, (*chip, mc), src=x_ref) for j, chip in enumerate(chips)]
        for cp in first:
            cp.start()
        passed = [copy(4 + j, (*chip, mc), sibling) for j, chip in enumerate(chips)]
        for j, chip in enumerate(chips):
            copy(1 + j, (*chip, mc), me).wait_recv()
            passed[j].start()
        copy(0, sibling, me).wait_recv()
        for j, chip in enumerate(chips):
            copy(4 + j, (*chip, 1 - mc), me).wait_recv()
        for cp in first + passed:
            cp.wait_send()
        mine.wait()

    return pl.pallas_call(
        body, out_shape=S((N_DEV, R, C), x.dtype), in_specs=[ANY], out_specs=ANY,
        scratch_shapes=[pltpu.SemaphoreType.DMA((7,)), pltpu.SemaphoreType.DMA((7,)), pltpu.SemaphoreType.DMA(())],
        name=name)(x)


def _ag_weights(shards, row_sharded, name):
    n = len(shards)

    def full_shape(a, rows):
        if rows:
            return a.shape[:-2] + (N_DEV * a.shape[-2],) + a.shape[-1:]
        return (N_DEV,) + a.shape

    def body(*refs):
        x_refs, o_refs = refs[:n], refs[n:2 * n]
        send_sems, recv_sems, local_sems = refs[2 * n:]
        mx, my, mc = lax.axis_index("x"), lax.axis_index("y"), lax.axis_index("c")
        me, sibling = (mx, my, mc), (mx, my, 1 - mc)
        chips = [(1 - mx, my), (mx, 1 - my), (1 - mx, 1 - my)]

        def dst(t, px, py, pc):
            d = 4 * px + 2 * py + pc
            if not row_sharded[t]:
                return o_refs[t].at[d]
            r = shards[t].shape[-2]
            idx = (slice(None),) * (shards[t].ndim - 2) + (pl.ds(pl.multiple_of(d * r, 16), r), slice(None))
            return o_refs[t].at[idx]

        def copy(kk, t, block, to, src=None):
            return pltpu.make_async_remote_copy(
                src_ref=dst(t, *block) if src is None else src, dst_ref=dst(t, *block),
                send_sem=send_sems.at[kk * n + t], recv_sem=recv_sems.at[kk * n + t], device_id=to,
                device_id_type=MESH)

        mine = [pltpu.make_async_copy(x_refs[t], dst(t, *me), local_sems.at[t]) for t in range(n)]
        for cp in mine:
            cp.start()
        first = []
        for t in range(n):
            first.append(copy(0, t, me, sibling, src=x_refs[t]))
            first += [copy(1 + j, t, me, (*chip, mc), src=x_refs[t]) for j, chip in enumerate(chips)]
        for cp in first:
            cp.start()
        passed = []
        for j, chip in enumerate(chips):
            for t in range(n):
                copy(1 + j, t, (*chip, mc), me).wait_recv()
                cp = copy(4 + j, t, (*chip, mc), sibling)
                cp.start()
                passed.append(cp)
        for t in range(n):
            copy(0, t, sibling, me).wait_recv()
            for j, chip in enumerate(chips):
                copy(4 + j, t, (*chip, 1 - mc), me).wait_recv()
        for cp in first + passed:
            cp.wait_send()
        for cp in mine:
            cp.wait()

    return pl.pallas_call(
        body, out_shape=tuple(S(full_shape(a, r), a.dtype) for a, r in zip(shards, row_sharded)),
        in_specs=[ANY] * n, out_specs=tuple([ANY] * n),
        scratch_shapes=[pltpu.SemaphoreType.DMA((7 * n,)), pltpu.SemaphoreType.DMA((7 * n,)),
                        pltpu.SemaphoreType.DMA((n,))],
        name=name)(*shards)


FLIPS = [(fx, fy, fc) for fx in (0, 1) for fy in (0, 1) for fc in (0, 1)][1:]
HBM = pl.BlockSpec(memory_space=pltpu.HBM)
SEM = pl.BlockSpec(memory_space=pltpu.SEMAPHORE)
EFFECT = pltpu.SideEffectType.DATAFLOW_SIDE_EFFECTING


def _hbm(a):
    return pltpu.with_memory_space_constraint(a, pltpu.HBM)


def _flips(gather):
    return [(0, 0, 0)] + FLIPS if gather else FLIPS


def _split_copies(gather, s_refs, l_refs, send_sems, recv_sems):
    n = len(s_refs)
    mx, my, mc = lax.axis_index("x"), lax.axis_index("y"), lax.axis_index("c")
    me = 4 * mx + 2 * my + mc
    copies = []
    for k, (fx, fy, fc) in enumerate(_flips(gather)):
        px, py, pc = (1 - mx if fx else mx), (1 - my if fy else my), (1 - mc if fc else mc)
        for t in range(n):
            if gather:
                src = s_refs[t]
                r = src.shape[0]
                dst = l_refs[t].at[pl.ds(pl.multiple_of(me * r, 16), r), :]
            else:
                src = s_refs[t].at[:, 4 * px + 2 * py + pc]
                dst = l_refs[t].at[k]
            copies.append(pltpu.make_async_remote_copy(
                src_ref=src, dst_ref=dst, send_sem=send_sems.at[k * n + t], recv_sem=recv_sems.at[k * n + t],
                device_id=(px, py, pc), device_id_type=MESH))
    return copies


def _split_start(gather, srcs, lands, after, name):
    n = len(srcs)
    n_sem = len(_flips(gather)) * n

    def body(*refs):
        s_refs, l_refs = refs[:n], refs[n:2 * n]
        send_sems, recv_sems = refs[2 * n + 1], refs[2 * n + 2]
        token = refs[-1]
        for cp in _split_copies(gather, s_refs, l_refs, send_sems, recv_sems):
            cp.start()
        token[...] = jnp.zeros_like(token)

    outs = pl.pallas_call(
        body, name=name,
        out_shape=(pltpu.SemaphoreType.DMA((n_sem,)), pltpu.SemaphoreType.DMA((n_sem,)))
        + tuple(pltpu.HBM(a.shape, a.dtype) for a in lands) + (S((8, LANES), F32),),
        in_specs=[HBM] * (2 * n) + [ANY],
        out_specs=(SEM, SEM) + (HBM,) * n + (pl.BlockSpec(memory_space=pltpu.VMEM),),
        input_output_aliases={n + i: 2 + i for i in range(n)},
        compiler_params=pltpu.CompilerParams(has_side_effects=EFFECT),
    )(*[_hbm(a) for a in srcs], *[_hbm(a) for a in lands], after)
    return outs[0], outs[1], list(srcs), list(outs[2:2 + n]), outs[-1]


def _split_wait(gather, send_sems, recv_sems, srcs, lands, after, name):
    n = len(srcs)

    def body(*refs):
        s_refs, l_refs = refs[:n], refs[n:2 * n]
        ssem, rsem = refs[2 * n], refs[2 * n + 1]
        copies = _split_copies(gather, s_refs, l_refs, ssem, rsem)
        for cp in copies:
            cp.wait_send()
        for cp in copies:
            cp.wait_recv()

    outs = pl.pallas_call(
        body, name=name, out_shape=tuple(pltpu.HBM(a.shape, a.dtype) for a in lands),
        in_specs=[HBM] * (2 * n) + [SEM, SEM, ANY], out_specs=(HBM,) * n,
        input_output_aliases={n + i: i for i in range(n)},
        compiler_params=pltpu.CompilerParams(has_side_effects=EFFECT),
    )(*[_hbm(a) for a in srcs], *lands, send_sems, recv_sems, after)
    return list(outs)


def _adamw_math(w, g, m, v):
    m = ADAM_B1 * m + (1.0 - ADAM_B1) * g
    v = ADAM_B2 * v + (1.0 - ADAM_B2) * (g * g)
    m_hat = m / (1.0 - ADAM_B1 ** ADAM_STEP)
    v_hat = v / (1.0 - ADAM_B2 ** ADAM_STEP)
    delta = -ADAM_LR * (m_hat / (jnp.sqrt(v_hat) + ADAM_EPS) + ADAM_WD * w)
    return delta, m, v


def _adamw_layers(owns, gots, w, m, v, name):
    L, B, C = w.shape
    per_row = 2 * L * len(FLIPS) * C * owns[0].dtype.itemsize
    tb = max([t for t in range(16, B + 1, 16) if B % t == 0 and (t * per_row <= 24 * 1024 * 1024 or t == 16)] or [B])
    me = (4 * lax.axis_index("x") + 2 * lax.axis_index("y") + lax.axis_index("c")).astype(jnp.int32).reshape(1)

    def body(me_ref, *refs):
        own_refs, got_refs = refs[:L], refs[L:2 * L]
        w_ref, m_ref, v_ref = refs[2 * L:2 * L + 3]
        g_out, d_out, m_out, v_out = refs[2 * L + 3:]
        layer = pl.program_id(0)
        for kk in range(L):
            @pl.when(layer == kk)
            def _():
                g = own_refs[kk][0].astype(F32)
                for s in range(len(FLIPS)):
                    g = g + got_refs[kk][s].astype(F32)
                d, mn, vn = _adamw_math(w_ref[...], g, m_ref[...], v_ref[...])
                g_out[...] = g
                d_out[...] = d
                m_out[...] = mn
                v_out[...] = vn

    def row(kk, layer, i):
        return jnp.where(layer == kk, i, 0)

    blk = pl.BlockSpec((1, tb, C), lambda layer, i, me_ref: (layer, i, 0))
    own_specs = [pl.BlockSpec((1, 1, tb, C), lambda layer, i, me_ref, kk=kk: (0, me_ref[0], row(kk, layer, i), 0))
                 for kk in range(L)]
    got_specs = [pl.BlockSpec((len(FLIPS), 1, tb, C), lambda layer, i, me_ref, kk=kk: (0, 0, row(kk, layer, i), 0))
                 for kk in range(L)]
    return pl.pallas_call(
        body,
        grid_spec=pltpu.PrefetchScalarGridSpec(
            num_scalar_prefetch=1, grid=(L, B // tb), in_specs=own_specs + got_specs + [blk, blk, blk],
            out_specs=(blk, blk, blk, blk)),
        out_shape=(S((L, B, C), F32),) * 4, compiler_params=_cparams("arbitrary", "arbitrary"), name=name)(
            me, *owns, *gots, w, m, v)


def _adamw_replicated(parts, w, m, v, name):
    R, C = w.shape
    rb = _tile(R, (512, 256, 128, 64, 32, 16))

    def body(p_ref, w_ref, m_ref, v_ref, g_out, d_out, m_out, v_out):
        g = p_ref[0].astype(F32)
        for j in range(1, N_DEV):
            g = g + p_ref[j].astype(F32)
        d, mn, vn = _adamw_math(w_ref[...], g, m_ref[...], v_ref[...])
        g_out[...] = g
        d_out[...] = d
        m_out[...] = mn
        v_out[...] = vn

    blk = pl.BlockSpec((rb, C), lambda i: (i, 0))
    return pl.pallas_call(
        body, grid=(R // rb,), in_specs=[pl.BlockSpec((N_DEV, rb, C), lambda i: (0, i, 0)), blk, blk, blk],
        out_specs=(blk, blk, blk, blk), out_shape=(S((R, C), F32),) * 4, compiler_params=_cparams("parallel"),
        name=name)(parts, w, m, v)


def _pack(arrs, rows_mult, dtype):
    flat = jnp.concatenate([a.reshape(-1).astype(dtype) for a in arrs])
    n = flat.shape[0]
    per = rows_mult * LANES
    tot = -(-n // per) * per
    return jnp.pad(flat, (0, tot - n)).reshape(tot // LANES, LANES)


def _unpack(blob, shapes):
    flat = blob.reshape(-1)
    out, off = [], 0
    for shp in shapes:
        n = int(np.prod(shp))
        out.append(flat[off:off + n].reshape(shp))
        off += n
    return out


def _small_to_natural(g8):
    t = jnp.moveaxis(g8, 0, -2)
    return t.reshape(t.shape[:-2] + (N_DEV * t.shape[-1],))


def _small_to_cols(g):
    t = g.reshape(g.shape[:-1] + (N_DEV, g.shape[-1] // N_DEV))
    return jnp.moveaxis(t, -2, 0)


def _block_diag(w):
    nb, bs, _ = w.shape
    eye = jnp.eye(nb, dtype=w.dtype)
    return (eye[:, None, :, None] * w[:, :, None, :]).reshape(nb * bs, nb * bs)


def _diag_blocks(d, nb, bs):
    d4 = d.reshape(nb, bs, nb, bs)
    return jnp.stack([d4[i, :, i, :] for i in range(nb)])


def kernel(x, mem, g_mix_pre, g_mix_post, g_ffn_pre, g_ffn_post, g_mem, w_mem_kv, w_mix_out, w_ffn_up, w_ffn_conv, b_ffn_conv, w_ffn_down, w_in_a, w_conv_a, b_conv_a, w_rg_r, b_rg_r, w_rg_i, b_rg_i, lru_lambda, w_in_b, sinks_b, g_kv, w_kv, loss_target, m_g_mix_pre, m_g_mix_post, m_g_ffn_pre, m_g_ffn_post, m_g_mem, m_w_mem_kv, m_w_mix_out, m_w_ffn_up, m_w_ffn_conv, m_b_ffn_conv, m_w_ffn_down, m_w_in_a, m_w_conv_a, m_b_conv_a, m_w_rg_r, m_b_rg_r, m_w_rg_i, m_b_rg_i, m_lru_lambda, m_w_in_b, m_sinks_b, m_g_kv, m_w_kv, v_g_mix_pre, v_g_mix_post, v_g_ffn_pre, v_g_ffn_post, v_g_mem, v_w_mem_kv, v_w_mix_out, v_w_ffn_up, v_w_ffn_conv, v_b_ffn_conv, v_w_ffn_down, v_w_in_a, v_w_conv_a, v_b_conv_a, v_w_rg_r, v_b_rg_r, v_w_rg_i, v_b_rg_i, v_lru_lambda, v_w_in_b, v_sinks_b, v_g_kv, v_w_kv):
    w_loc = dict(g_mix_pre=g_mix_pre, g_mix_post=g_mix_post, g_ffn_pre=g_ffn_pre, g_ffn_post=g_ffn_post, g_mem=g_mem,
                 w_mem_kv=w_mem_kv, w_mix_out=w_mix_out, w_ffn_up=w_ffn_up, w_ffn_conv=w_ffn_conv,
                 b_ffn_conv=b_ffn_conv, w_ffn_down=w_ffn_down, w_in_a=w_in_a, w_conv_a=w_conv_a, b_conv_a=b_conv_a,
                 w_rg_r=w_rg_r, b_rg_r=b_rg_r, w_rg_i=w_rg_i, b_rg_i=b_rg_i, lru_lambda=lru_lambda, w_in_b=w_in_b,
                 sinks_b=sinks_b, g_kv=g_kv, w_kv=w_kv)
    m_loc = dict(g_mix_pre=m_g_mix_pre, g_mix_post=m_g_mix_post, g_ffn_pre=m_g_ffn_pre, g_ffn_post=m_g_ffn_post,
                 g_mem=m_g_mem, w_mem_kv=m_w_mem_kv, w_mix_out=m_w_mix_out, w_ffn_up=m_w_ffn_up,
                 w_ffn_conv=m_w_ffn_conv, b_ffn_conv=m_b_ffn_conv, w_ffn_down=m_w_ffn_down, w_in_a=m_w_in_a,
                 w_conv_a=m_w_conv_a, b_conv_a=m_b_conv_a, w_rg_r=m_w_rg_r, b_rg_r=m_b_rg_r, w_rg_i=m_w_rg_i,
                 b_rg_i=m_b_rg_i, lru_lambda=m_lru_lambda, w_in_b=m_w_in_b, sinks_b=m_sinks_b, g_kv=m_g_kv,
                 w_kv=m_w_kv)
    v_loc = dict(g_mix_pre=v_g_mix_pre, g_mix_post=v_g_mix_post, g_ffn_pre=v_g_ffn_pre, g_ffn_post=v_g_ffn_post,
                 g_mem=v_g_mem, w_mem_kv=v_w_mem_kv, w_mix_out=v_w_mix_out, w_ffn_up=v_w_ffn_up,
                 w_ffn_conv=v_w_ffn_conv, b_ffn_conv=v_b_ffn_conv, w_ffn_down=v_w_ffn_down, w_in_a=v_w_in_a,
                 w_conv_a=v_w_conv_a, b_conv_a=v_b_conv_a, w_rg_r=v_w_rg_r, b_rg_r=v_b_rg_r, w_rg_i=v_w_rg_i,
                 b_rg_i=v_b_rg_i, lru_lambda=v_lru_lambda, w_in_b=v_w_in_b, sinks_b=v_sinks_b, g_kv=v_g_kv,
                 w_kv=v_w_kv)

    Bl, T, D = x.shape
    Ml = mem.shape[1]
    N = Bl * T
    depth = g_mix_pre.shape[0]
    n_a = w_in_a.shape[0]
    F = w_ffn_down.shape[1] * N_DEV
    def as_rows(n, a):
        return jnp.swapaxes(a, -1, -2) if n in TRANSPOSED else a

    def mix_keys(l):
        keys = [("w_mem_kv", l), ("w_mix_out", l), ("w_in_a", l) if l < n_a else ("w_in_b", l - n_a)]
        return keys + ([("w_kv", None)] if l == n_a else [])

    def ffn_keys(l):
        return [("w_ffn_up", l), ("w_ffn_down", l)]

    def shard_of(key):
        n, i = key
        return as_rows(n, w_loc[n] if i is None else w_loc[n][i]).astype(BF16)

    W = {}
    keys0, keys0_rest = mix_keys(0)[2:], mix_keys(0)[:2]
    got0 = _ag_weights([shard_of(kk) for kk in keys0] + [w_loc[n] for n in SMALL_SHARDED],
                       [True] * len(keys0) + [False] * len(SMALL_SHARDED), name="ag_weights_0")
    W.update(zip(keys0, got0))
    for n, a in zip(SMALL_SHARDED, got0[len(keys0):]):
        W[n] = _small_to_natural(a)

    def gather_start(keys, after, tag):
        shards = [shard_of(kk) for kk in keys]
        lands = [lax.empty((N_DEV * s.shape[0],) + s.shape[1:], s.dtype) for s in shards]
        return (keys, tag) + _split_start(True, shards, lands, after, name=f"ag_start_{tag}")

    def gather_wait(pending, after):
        keys, tag, ssem, rsem, srcs, lands, _ = pending
        W.update(zip(keys, _split_wait(True, ssem, rsem, srcs, lands, after, name=f"ag_wait_{tag}")))

    pending_rest = gather_start(keys0_rest, got0[0], "mix_0")
    pending_ffn = gather_start(ffn_keys(0)[:1], pending_rest[-1], "ffn_up_0")
    pending_down0 = gather_start(ffn_keys(0)[1:], pending_ffn[-1], "ffn_down_0")

    nblk, bsz = w_rg_r.shape[1], w_rg_r.shape[2]
    wbd = [jnp.concatenate([_block_diag(w_rg_r[j]), _block_diag(w_rg_i[j])], axis=1).astype(BF16) for j in range(n_a)]

    def vec(a):
        return a.reshape(1, -1)

    x2 = x.reshape(N, D)
    mem2 = mem.reshape(Bl * Ml, D)
    saved = []
    kvn = kv3 = x_kv = None
    xs = x2
    h1 = _rms_fwd(xs, vec(g_mix_pre[0]), BF16, name="rms_mixpre_0")
    for l in range(depth):
        sv = {"x0": xs}
        tok = None
        if l + 1 < depth:
            pending = gather_start(mix_keys(l + 1), pending_down0[-1] if l == 0 else W[("w_mem_kv", l)],
                                   f"mix_{l + 1}")
            pending_next_ffn = gather_start(ffn_keys(l + 1), pending[-1], f"ffn_{l + 1}")
            tok = pending_next_ffn[-1]
        memn = _rms_fwd(mem2, vec(g_mem[l]), BF16, name=f"rms_mem_{l}")
        if l < n_a:
            j = l
            proj = _mm(h1, W[("w_in_a", j)], tb=True, after=tok, name=f"mm_in_{l}")
            proj3 = proj.reshape(Bl, T, -1)
            xc3 = _conv_fwd_call(proj3, MIX_W, MIX_W, W["w_conv_a"][j], vec(W["b_conv_a"][j]), name=f"conv_a_{l}")
            gates3 = _mm(xc3.reshape(N, MIX_W), wbd[j], name=f"mm_gates_{l}").reshape(Bl, T, 2 * MIX_W)
            y_main3, hs3 = _rglru_fwd(xc3, gates3, proj3, vec(b_rg_r[j]), vec(b_rg_i[j]), vec(W["lru_lambda"][j]),
                                      name=f"rglru_fwd_{l}")
            q_off = 2 * MIX_W
            sv.update(xc3=xc3, gates3=gates3, hs3=hs3)
        else:
            j = l - n_a
            if l == n_a:
                x_kv = xs
                kv3 = _mm(kvn, W[("w_kv", None)], name="mm_kv").reshape(Bl, T, 2 * MEM_W)
            proj = _mm(h1, W[("w_in_b", j)], after=tok, name=f"mm_in_{l}")
            proj3 = proj.reshape(Bl, T, -1)
            y_main3 = _swa_fwd(proj3, kv3, sinks_b[j], name=f"swa_fwd_{l}")
            q_off = MIX_W
        if l == 0:
            gather_wait(pending_rest, y_main3)
        mkv3 = _mm(memn, W[("w_mem_kv", l)], name=f"mm_memkv_{l}").reshape(Bl, Ml, 2 * MEM_W)
        y_mem3 = _mem_attn_fwd(proj3, q_off, mkv3, name=f"memattn_fwd_{l}")
        y_main = y_main3.reshape(N, MIX_W)
        y_mem = y_mem3.reshape(N, MEM_W)
        y = _mm_sum([(y_main, W[("w_mix_out", l)], (0, 0)), (y_mem, W[("w_mix_out", l)], (MIX_W, 0))], n=D,
                    name=f"mm_mixout_{l}")
        x1, h2 = _rms_pair_fwd(y, vec(g_mix_post[l]), xs, [vec(g_ffn_pre[l])], name=f"rms_mixpost_ffnpre_{l}")
        gather_wait(pending_ffn, h2)
        if l + 1 < depth:
            pending_ffn = pending_next_ffn
        u3 = _mm(h2, W[("w_ffn_up", l)], tb=True, name=f"mm_up_{l}").reshape(Bl, T, 2 * F)
        act3 = _ffn_mid_fwd(u3, W["w_ffn_conv"][l], vec(b_ffn_conv[l]), name=f"ffn_mid_fwd_{l}")
        act = act3.reshape(N, F)
        if l == 0:
            gather_wait(pending_down0, act)
        f = _mm(act, W[("w_ffn_down", l)], name=f"mm_down_{l}")
        sv.update(h1=h1, memn=memn, mkv3=mkv3, proj3=proj3, q_off=q_off, y_main=y_main, y_mem=y_mem, y=y, x1=x1,
                  h2=h2, u3=u3, act=act, f=f)
        saved.append(sv)
        if l + 1 < depth:
            gains = [vec(g_mix_pre[l + 1])] + ([vec(g_kv)] if l + 1 == n_a else [])
            xs, h1, *rest = _rms_pair_fwd(f, vec(g_ffn_post[l]), x1, gains, name=f"rms_ffnpost_mixpre_{l}")
            if rest:
                kvn = rest[0]
            gather_wait(pending, xs)

    last = depth - 1
    dxs, df_last, loss_vec, dg_last = _head_turnaround(saved[last]["f"], vec(g_ffn_post[last]), saved[last]["x1"],
                                                       loss_target.reshape(N, D))
    loss = lax.psum(jnp.sum(loss_vec), ("x", "y", "c"))

    G = {n: [None] * w_loc[n].shape[0] for n in REPL + SMALL_SHARDED if n != "g_kv"}
    GW = {}

    def dw(key, off, a, b_, nm):
        GW[key] = _mm(a, b_, ta=True, out_dtype=BF16, into=(GW.get(key), (1,) + W[key].shape, 0, off), name=nm)

    def grad_blocks(key):
        g = GW[key]
        return g.reshape(1, N_DEV, g.shape[1] // N_DEV, g.shape[2])

    reduces = []

    def reduce_start(keys, after, tag):
        srcs = [grad_blocks(kk) for kk in keys]
        lands = [lax.empty((len(FLIPS),) + s.shape[:1] + s.shape[2:], s.dtype) for s in srcs]
        started = _split_start(False, srcs, lands, after, name=f"rs_start_{tag}")
        reduces.append((keys, tag) + started)
        return started[-1]

    kv_parts = []
    G["g_ffn_post"][last] = dg_last[0]
    df = df_last
    for l in reversed(range(depth)):
        sv = saved[l]
        proj3 = sv["proj3"]
        if df is None:
            df, dg = _rms_bwd(sv["f"], vec(g_ffn_post[l]), dxs, out_dtype=BF16, name=f"rmsb_ffnpost_{l}")
            G["g_ffn_post"][l] = dg[0]
        dact = _mm(df, W[("w_ffn_down", l)], tb=True, name=f"mmb_down_dx_{l}")
        dw(("w_ffn_down", l), (0, 0), sv["act"], df, f"mmb_down_dw_{l}")
        dug3, duv3, dwg, dwv, dbg, dbv = _ffn_mid_bwd(sv["u3"], dact.reshape(Bl, T, F),
                                                      W["w_ffn_conv"][l], vec(b_ffn_conv[l]), name=f"ffn_mid_bwd_{l}")
        G["w_ffn_conv"][l] = jnp.concatenate([dwg, dwv], axis=1)
        G["b_ffn_conv"][l] = jnp.concatenate([dbg, dbv], axis=1)[0]
        dug, duv = dug3.reshape(N, F), duv3.reshape(N, F)
        dw(("w_ffn_up", l), (0, 0), dug, sv["h2"], f"mmb_up_dw_g_{l}")
        dw(("w_ffn_up", l), (F, 0), duv, sv["h2"], f"mmb_up_dw_v_{l}")
        tok = reduce_start([("w_ffn_down", l), ("w_ffn_up", l)], dug, f"ffn_{l}")
        dh2 = _mm_sum([(dug, W[("w_ffn_up", l)], (0, 0)), (duv, W[("w_ffn_up", l)], (F, 0))], n=D, after=tok,
                      name=f"mmb_up_dx_{l}")
        dx1, dy, dg, dg2 = _rms_pair_bwd(sv["x1"], vec(g_ffn_pre[l]), dh2, dxs, sv["y"], vec(g_mix_post[l]),
                                         name=f"rmsb_ffnpre_mixpost_{l}")
        G["g_ffn_pre"][l] = dg[0]
        G["g_mix_post"][l] = dg2[0]
        dy_main = _mm(dy, W[("w_mix_out", l)], tb=True, n=MIX_W, k=D, name=f"mmb_mixout_dmain_{l}")
        dy_mem = _mm(dy, W[("w_mix_out", l)], tb=True, n=MEM_W, k=D, b_off=(MIX_W, 0),
                     name=f"mmb_mixout_dmem_{l}")
        dw(("w_mix_out", l), (0, 0), sv["y_main"], dy, f"mmb_mixout_dw_main_{l}")
        dw(("w_mix_out", l), (MIX_W, 0), sv["y_mem"], dy, f"mmb_mixout_dw_mem_{l}")
        dq_mem3, dmkv3 = _mem_attn_bwd(proj3, sv["q_off"], sv["mkv3"], dy_mem.reshape(Bl, T, MEM_W),
                                       name=f"memattn_bwd_{l}")
        dq_mem = dq_mem3.reshape(N, MEM_W)
        dmkv = dmkv3.reshape(Bl * Ml, 2 * MEM_W)
        dw(("w_mem_kv", l), (0, 0), sv["memn"], dmkv, f"mmb_memkv_dw_{l}")
        dmemn = _mm(dmkv, W[("w_mem_kv", l)], tb=True, name=f"mmb_memkv_dx_{l}")
        _, dg = _rms_bwd(mem2, vec(g_mem[l]), dmemn, name=f"rmsb_mem_{l}")
        G["g_mem"][l] = dg[0]
        dy_main3 = dy_main.reshape(Bl, T, MIX_W)
        if l < n_a:
            j = l
            dxc3, drp3, dip3, dugate3, dbr, dbi, dlam = _rglru_bwd(
                dy_main3, sv["xc3"], sv["gates3"], proj3, sv["hs3"], vec(b_rg_r[j]), vec(b_rg_i[j]),
                vec(W["lru_lambda"][j]), name=f"rglru_bwd_{l}")
            G["b_rg_r"][j] = dbr.reshape(nblk, bsz)
            G["b_rg_i"][j] = dbi.reshape(nblk, bsz)
            G["lru_lambda"][j] = dlam[0]
            drp, dip = drp3.reshape(N, MIX_W), dip3.reshape(N, MIX_W)
            xc2 = sv["xc3"].reshape(N, MIX_W)
            G["w_rg_r"][j] = _diag_blocks(_mm(xc2, drp, ta=True, name=f"mmb_gates_dw_r_{l}"), nblk, bsz)
            G["w_rg_i"][j] = _diag_blocks(_mm(xc2, dip, ta=True, name=f"mmb_gates_dw_i_{l}"), nblk, bsz)
            dxc = _mm_sum([(drp, wbd[j], (0, 0)), (dip, wbd[j], (0, MIX_W))], tb=True, n=MIX_W,
                          add=dxc3.reshape(N, MIX_W), name=f"mmb_gates_dx_{l}")
            dux3, dwc, dbc = _conv_bwd_call(dxc.reshape(Bl, T, MIX_W), proj3, MIX_W, MIX_W, W["w_conv_a"][j],
                                            name=f"conv_a_bwd_{l}")
            G["w_conv_a"][j] = dwc
            G["b_conv_a"][j] = dbc[0]
            pieces = [(dugate3.reshape(N, MIX_W), 0), (dux3.reshape(N, MIX_W), MIX_W), (dq_mem, 2 * MIX_W)]
            in_key = ("w_in_a", j)
        else:
            j = l - n_a
            dq3, dkc, dkp, dsk = _swa_bwd(proj3, kv3, sinks_b[j], dy_main3, name=f"swa_bwd_{l}")
            kv_parts.append((dkc, dkp))
            G["sinks_b"][j] = dsk[0, :SWA_HEADS]
            pieces = [(dq3.reshape(N, MIX_W), 0), (dq_mem, MIX_W)]
            in_key = ("w_in_b", j)
        in_t = in_key[0] in TRANSPOSED
        for pi, (piece, off) in enumerate(pieces):
            if in_t:
                dw(in_key, (off, 0), piece, sv["h1"], f"mmb_in_dw_{pi}_{l}")
            else:
                dw(in_key, (0, off), sv["h1"], piece, f"mmb_in_dw_{pi}_{l}")
        tok = reduce_start([("w_mix_out", l), ("w_mem_kv", l), in_key], dy, f"mix_{l}")
        dh1 = _mm_sum([(piece, W[in_key], (off, 0) if in_t else (0, off)) for piece, off in pieces], tb=not in_t, n=D,
                      after=tok, name=f"mmb_in_dx_{l}")
        if l > 0 and l != n_a:
            dxs, df, dg, dg2 = _rms_pair_bwd(sv["x0"], vec(g_mix_pre[l]), dh1, dx1, saved[l - 1]["f"],
                                             vec(g_ffn_post[l - 1]), name=f"rmsb_mixpre_ffnpost_{l}")
            G["g_ffn_post"][l - 1] = dg2[0]
        else:
            dxs, dg = _rms_bwd(sv["x0"], vec(g_mix_pre[l]), dh1, add=dx1, name=f"rmsb_mixpre_{l}")
            df = None
        G["g_mix_pre"][l] = dg[0]
        if l == n_a:
            dkv = _kv_grad_combine(kv_parts, name="kv_grad_combine").reshape(N, 2 * MEM_W)
            dw(("w_kv", None), (0, 0), kvn, dkv, "mmb_kv_dw")
            tok = reduce_start([("w_kv", None)], dkv, "kv")
            dkvn = _mm(dkv, W[("w_kv", None)], tb=True, after=tok, name="mmb_kv_dx")
            dxs, dg = _rms_bwd(x_kv, vec(g_kv), dkvn, add=dxs, name="rmsb_kv")
            G["g_kv"] = dg[0]
    grad_x = dxs.reshape(Bl, T, D)
    Gf = {n: (jnp.stack(g) if isinstance(g, list) else g) for n, g in G.items()}

    small4 = []
    for n in SMALL_SHARDED:
        t = _small_to_cols(Gf[n]).astype(BF16)
        small4.append(t.reshape(1, N_DEV, -1, t.shape[-1]))
    small_lands = [lax.empty((len(FLIPS),) + s.shape[:1] + s.shape[2:], s.dtype) for s in small4]
    small_started = _split_start(False, small4, small_lands, dxs, name="rs_start_small")
    r_blob = _pack([Gf[n] for n in REPL], REPL_ROWS, BF16)
    r_parts = _all_gather(r_blob, name="ag_repl_grads")
    parts = {}
    for keys, tag, ssem, rsem, srcs, lands, _ in reduces:
        for kk, s, g7 in zip(keys, srcs, _split_wait(False, ssem, rsem, srcs, lands, small_started[-1],
                                                     name=f"rs_wait_{tag}")):
            parts[kk] = (s, g7)

    res = [{} for _ in range(4)]
    for n, _ in SHARDED:
        if n in SMALL_SHARDED:
            continue
        idx = [None] if w_loc[n].ndim == 2 else list(range(w_loc[n].shape[0]))
        wmv = [as_rows(n, a[n]) for a in (w_loc, m_loc, v_loc)]
        shp3 = (len(idx),) + wmv[0].shape[-2:]
        outs = _adamw_layers([parts[(n, i)][0] for i in idx], [parts[(n, i)][1] for i in idx],
                             *[a.reshape(shp3) for a in wmv], name=f"adamw_{n}")
        for k in range(4):
            res[k][n] = as_rows(n, outs[k].reshape(wmv[0].shape))
    last = res[0]["w_kv"]
    small_got = _split_wait(False, *small_started[:4], last, name="rs_wait_small")
    for n, own, g7 in zip(SMALL_SHARDED, small4, small_got):
        shp3 = own.shape[:1] + own.shape[2:]
        outs = _adamw_layers([own], [g7], w_loc[n].reshape(shp3), m_loc[n].reshape(shp3), v_loc[n].reshape(shp3),
                             name=f"adamw_{n}")
        for k in range(4):
            res[k][n] = outs[k].reshape(w_loc[n].shape)
    outs_rp = _adamw_replicated(r_parts, _pack([w_loc[n] for n in REPL], REPL_ROWS, F32),
                                _pack([m_loc[n] for n in REPL], REPL_ROWS, F32),
                                _pack([v_loc[n] for n in REPL], REPL_ROWS, F32),
                                name="adamw_replicated")
    rp_shapes = [w_loc[n].shape for n in REPL]
    for k in range(4):
        res[k].update(zip(REPL, _unpack(outs_rp[k], rp_shapes)))
    out = [loss, grad_x]
    for k in range(4):
        out += [res[k][n] for n in WEIGHTS]
    return tuple(out)
```
